```python
import math
import jax, jax.numpy as jnp
from jax import lax
import numpy as np


D_MODEL = 1024
BATCH = 8
SEQ = 8192
DEPTH = 2

CHUNK = 64
CONV_WIDTH = 4
RMS_EPS = 1e-6
GDN_HEADS = D_MODEL // 256
GDN_DK = 128
GDN_DV = 128
GDN_QK = GDN_HEADS * GDN_DK
GDN_V = GDN_HEADS * GDN_DV
SSD_HEADDIM = 64
SSD_HEADS = D_MODEL // 128
SSD_INNER = SSD_HEADS * SSD_HEADDIM
SSD_GROUPS = 2
SSD_STATE = 128
SSD_BC = SSD_GROUPS * SSD_STATE
LRU_WIDTH = D_MODEL // 2
LRU_BLOCKS = 8
LRU_BLOCK = LRU_WIDTH // LRU_BLOCKS
LRU_C = 8.0
N_BRANCH = 3
BRANCH_WIDTH = 512
D_FF = 4 * D_MODEL
N_MOD = 6
IN_SPLITS = (GDN_QK, GDN_QK, GDN_V, GDN_V, GDN_HEADS, GDN_HEADS,
             SSD_INNER, SSD_INNER, SSD_BC, SSD_BC, SSD_HEADS,
             LRU_WIDTH, LRU_WIDTH,
             N_BRANCH * D_MODEL)
D_IN = sum(IN_SPLITS)

kernel_name = 'hybrid_gdn_ssd_rglru_adaln_block'


def rmsnorm(x, w):
    xf = x.astype(jnp.float32)
    y = xf * lax.rsqrt(jnp.mean(xf * xf, axis=-1, keepdims=True) + RMS_EPS)
    return (y * w.astype(jnp.float32)).astype(x.dtype)


def l2norm(x):
    return x * lax.rsqrt(jnp.sum(x * x, axis=-1, keepdims=True) + RMS_EPS)


def split_cols(t, sizes):
    idx = np.cumsum(sizes)[:-1].tolist()
    return jnp.split(t, idx, axis=-1)


def causal_conv(x, w):
    width = w.shape[0]
    seq = x.shape[1]
    xp = jnp.pad(x, ((0, 0), (width - 1, 0), (0, 0)))
    return sum(xp[:, k:k + seq] * w[k] for k in range(width))


def gated_deltanet(q, k, v, z, b_raw, a_raw, a_log, dt_bias, norm_w):
    f32 = jnp.float32
    bsz, seq, _ = q.shape
    nc = seq // CHUNK

    def heads(t, d):
        return t.astype(f32).reshape(bsz, nc, CHUNK, GDN_HEADS, d).transpose(0, 1, 3, 2, 4)

    def per_head(t):
        return t.astype(f32).reshape(bsz, nc, CHUNK, GDN_HEADS).transpose(0, 1, 3, 2)

    q = l2norm(heads(q, GDN_DK)) * (GDN_DK ** -0.5)
    k = l2norm(heads(k, GDN_DK))
    v = heads(v, GDN_DV)
    beta = jax.nn.sigmoid(per_head(b_raw))
    g = -jnp.exp(a_log.astype(f32))[:, None] * jax.nn.softplus(per_head(a_raw) + dt_bias.astype(f32)[:, None])
    gcum = jnp.cumsum(g, axis=-1)
    causal = jnp.tril(jnp.ones((CHUNK, CHUNK), bool))
    strict = jnp.tril(jnp.ones((CHUNK, CHUNK), bool), -1)
    decay = jnp.exp(jnp.where(causal, gcum[..., :, None] - gcum[..., None, :], -jnp.inf))
    kk = jnp.einsum('bnhcd,bnhed->bnhce', k, k)
    m = jnp.where(strict, beta[..., :, None] * kk * decay, 0.0)
    eye = jnp.eye(CHUNK, dtype=f32)
    rhs = jnp.concatenate([beta[..., None] * v, (beta * jnp.exp(gcum))[..., None] * k], axis=-1)
    sol = lax.linalg.triangular_solve(eye + m, rhs, left_side=True, lower=True, unit_diagonal=True)
    u, w = sol[..., :GDN_DV], sol[..., GDN_DV:]
    qk = jnp.einsum('bnhcd,bnhed->bnhce', q, k) * decay
    q_dec = q * jnp.exp(gcum)[..., None]
    k_dec = k * jnp.exp(gcum[..., -1:] - gcum)[..., None]
    g_tot = jnp.exp(gcum[..., -1])

    def step(state, inp):
        u_c, w_c, qk_c, qd_c, kd_c, gt_c = inp
        v_new = u_c - jnp.einsum('bhck,bhkv->bhcv', w_c, state)
        o_c = jnp.einsum('bhck,bhkv->bhcv', qd_c, state) + jnp.einsum('bhce,bhev->bhcv', qk_c, v_new)
        state = state * gt_c[..., None, None] + jnp.einsum('bhck,bhcv->bhkv', kd_c, v_new)
        return state, o_c

    xs = tuple(jnp.moveaxis(t, 1, 0) for t in (u, w, qk, q_dec, k_dec, g_tot))
    s0 = jnp.zeros((bsz, GDN_HEADS, GDN_DK, GDN_DV), f32)
    _, o = lax.scan(step, s0, xs)
    o = o.transpose(1, 0, 3, 2, 4).reshape(bsz, seq, GDN_HEADS, GDN_DV)
    zh = z.astype(f32).reshape(bsz, seq, GDN_HEADS, GDN_DV)
    o = rmsnorm(o, norm_w) * jax.nn.silu(zh)
    return o.reshape(bsz, seq, GDN_V)


def ssd_scan(xs, bm, cm, dt_raw, a_log, dt_bias, d_skip):
    f32 = jnp.float32
    bsz, seq, _ = xs.shape
    nc = seq // CHUNK
    hpg = SSD_HEADS // SSD_GROUPS
    x = xs.astype(f32).reshape(bsz, nc, CHUNK, SSD_GROUPS, hpg, SSD_HEADDIM)
    bm = bm.astype(f32).reshape(bsz, nc, CHUNK, SSD_GROUPS, SSD_STATE)
    cm = cm.astype(f32).reshape(bsz, nc, CHUNK, SSD_GROUPS, SSD_STATE)
    dt = jax.nn.softplus(dt_raw.astype(f32) + dt_bias.astype(f32)).reshape(bsz, nc, CHUNK, SSD_GROUPS, hpg)
    a = -jnp.exp(a_log.astype(f32)).reshape(SSD_GROUPS, hpg)
    acum = jnp.cumsum(dt * a, axis=2)
    xdt = x * dt[..., None]
    causal = jnp.tril(jnp.ones((CHUNK, CHUNK), bool))
    seg = acum[:, :, :, None] - acum[:, :, None, :]
    lmat = jnp.exp(jnp.where(causal[:, :, None, None], seg, -jnp.inf))
    cb = jnp.einsum('bncgs,bnegs->bnceg', cm, bm)
    y_diag = jnp.einsum('bnceg,bncegh,bneghp->bncghp', cb, lmat, xdt)
    decay_end = jnp.exp(acum[:, :, -1:] - acum)
    chunk_states = jnp.einsum('bncgs,bncgh,bncghp->bnghps', bm, decay_end, xdt)
    chunk_decay = jnp.exp(acum[:, :, -1])

    def step(state, inp):
        st, dc = inp
        return state * dc[..., None, None] + st, state

    s0 = jnp.zeros((bsz, SSD_GROUPS, hpg, SSD_HEADDIM, SSD_STATE), f32)
    _, prev = lax.scan(step, s0, (jnp.moveaxis(chunk_states, 1, 0), jnp.moveaxis(chunk_decay, 1, 0)))
    prev = jnp.moveaxis(prev, 0, 1)
    y_off = jnp.einsum('bncgs,bnghps,bncgh->bncghp', cm, prev, jnp.exp(acum))
    y = y_diag + y_off + d_skip.astype(f32).reshape(SSD_GROUPS, hpg)[:, :, None] * x
    return y.reshape(bsz, seq, SSD_INNER)


def rg_lru(x, w_a, b_a, w_x, b_x, lam):
    f32 = jnp.float32
    bsz, seq, _ = x.shape
    xf = x.astype(f32)
    xb = xf.reshape(bsz, seq, LRU_BLOCKS, LRU_BLOCK)
    r = jax.nn.sigmoid(jnp.einsum('bsnd,nde->bsne', xb, w_a.astype(f32)).reshape(bsz, seq, LRU_WIDTH) + b_a.astype(f32))
    i = jax.nn.sigmoid(jnp.einsum('bsnd,nde->bsne', xb, w_x.astype(f32)).reshape(bsz, seq, LRU_WIDTH) + b_x.astype(f32))
    log_a = -LRU_C * r * jax.nn.softplus(-lam.astype(f32))
    a = jnp.exp(log_a)
    u = jnp.sqrt(-jnp.expm1(2.0 * log_a)) * (i * xf)

    def combine(left, right):
        a_l, u_l = left
        a_r, u_r = right
        return a_l * a_r, a_r * u_l + u_r

    _, hs = lax.associative_scan(combine, (a, u), axis=1)
    return hs


def hybrid_mixer(h, w_in, gdn_conv_w, gdn_a_log, gdn_dt_bias, gdn_norm,
                 ssd_conv_w, ssd_conv_b, ssd_a_log, ssd_dt_bias, ssd_d, ssd_norm,
                 lru_conv_w, lru_conv_b, lru_w_a, lru_b_a, lru_w_x, lru_b_x, lru_lambda,
                 w_branch, w_out):
    bsz, seq, _ = h.shape
    f32 = jnp.float32
    proj = h @ w_in
    (q, k, v, gdn_z, gdn_b, gdn_a, ssd_x, ssd_z, ssd_bm, ssd_cm, ssd_dt,
     lru_x, lru_gate, gate_logits) = split_cols(proj, IN_SPLITS)
    qkv = jax.nn.silu(causal_conv(jnp.concatenate([q, k, v], axis=-1), gdn_conv_w))
    q, k, v = split_cols(qkv, (GDN_QK, GDN_QK, GDN_V))
    y_a = gated_deltanet(q, k, v, gdn_z, gdn_b, gdn_a, gdn_a_log, gdn_dt_bias, gdn_norm)
    xbc = jax.nn.silu(causal_conv(jnp.concatenate([ssd_x, ssd_bm, ssd_cm], axis=-1), ssd_conv_w) + ssd_conv_b)
    sx, sb, sc = split_cols(xbc, (SSD_INNER, SSD_BC, SSD_BC))
    y = ssd_scan(sx, sb, sc, ssd_dt, ssd_a_log, ssd_dt_bias, ssd_d)
    gz = (y * jax.nn.silu(ssd_z.astype(f32))).reshape(bsz, seq, SSD_GROUPS, SSD_INNER // SSD_GROUPS)
    y_b = rmsnorm(gz, ssd_norm.reshape(SSD_GROUPS, SSD_INNER // SSD_GROUPS)).reshape(bsz, seq, SSD_INNER)
    xc = causal_conv(lru_x, lru_conv_w) + lru_conv_b
    y_c = rg_lru(xc, lru_w_a, lru_b_a, lru_w_x, lru_b_x, lru_lambda) * jax.nn.gelu(lru_gate.astype(f32))
    gates = jax.nn.sigmoid(gate_logits.reshape(bsz, seq, N_BRANCH, D_MODEL))
    merged = sum(gates[:, :, r] * (y_r.astype(h.dtype) @ w_branch[r]) for r, y_r in enumerate((y_a, y_b, y_c)))
    return merged @ w_out


def _fwd_setup_inputs(seed: int = 0) -> dict:
    key = jax.random.key(seed)
    ks = iter(jax.random.split(key, 40))
    L = DEPTH

    def nrm(shape, scale):
        return jax.random.normal(next(ks), shape, jnp.float32) * scale

    def gain(shape):
        return 1.0 + nrm(shape, 0.1)

    def dt_bias_init(n):
        dt = jnp.exp(jax.random.uniform(next(ks), (L, n), jnp.float32, math.log(1e-3), math.log(1e-1)))
        return dt + jnp.log(-jnp.expm1(-dt))

    def a_log_init(n):
        return jnp.log(jax.random.uniform(next(ks), (L, n), jnp.float32, 1.0, 16.0))

    x = nrm((BATCH, SEQ, D_MODEL), 1.0)
    c = nrm((BATCH, D_MODEL), 1.0)
    ada_w = nrm((L, D_MODEL, N_MOD * D_MODEL), 0.3 * D_MODEL ** -0.5)
    ada_b = nrm((L, N_MOD * D_MODEL), 0.02)
    norm_mix = gain((L, D_MODEL))
    w_in = nrm((L, D_MODEL, D_IN), D_MODEL ** -0.5)
    gdn_conv_w = nrm((L, CONV_WIDTH, 2 * GDN_QK + GDN_V), CONV_WIDTH ** -0.5)
    gdn_a_log = a_log_init(GDN_HEADS)
    gdn_dt_bias = dt_bias_init(GDN_HEADS)
    gdn_norm = gain((L, GDN_DV))
    ssd_conv_w = nrm((L, CONV_WIDTH, SSD_INNER + 2 * SSD_BC), CONV_WIDTH ** -0.5)
    ssd_conv_b = nrm((L, SSD_INNER + 2 * SSD_BC), 0.02)
    ssd_a_log = a_log_init(SSD_HEADS)
    ssd_dt_bias = dt_bias_init(SSD_HEADS)
    ssd_d = gain((L, SSD_HEADS))
    ssd_norm = gain((L, SSD_INNER))
    lru_conv_w = nrm((L, CONV_WIDTH, LRU_WIDTH), CONV_WIDTH ** -0.5)
    lru_conv_b = nrm((L, LRU_WIDTH), 0.02)
    lru_w_a = nrm((L, LRU_BLOCKS, LRU_BLOCK, LRU_BLOCK), LRU_BLOCK ** -0.5)
    lru_b_a = nrm((L, LRU_WIDTH), 0.02)
    lru_w_x = nrm((L, LRU_BLOCKS, LRU_BLOCK, LRU_BLOCK), LRU_BLOCK ** -0.5)
    lru_b_x = nrm((L, LRU_WIDTH), 0.02)
    a_pow = jax.random.uniform(next(ks), (L, LRU_WIDTH), jnp.float32, 0.9, 0.999)
    s = a_pow ** (1.0 / LRU_C)
    lru_lambda = jnp.log(s) - jnp.log1p(-s)
    w_branch = nrm((L, N_BRANCH, BRANCH_WIDTH, D_MODEL), BRANCH_WIDTH ** -0.5)
    w_out = nrm((L, D_MODEL, D_MODEL), D_MODEL ** -0.5)
    norm_mlp = gain((L, D_MODEL))
    w_up = nrm((L, D_MODEL, D_FF), D_MODEL ** -0.5)
    w_down = nrm((L, D_FF, D_MODEL), D_FF ** -0.5)
    final_norm = gain((D_MODEL,))
    return {'x': x, 'c': c, 'ada_w': ada_w, 'ada_b': ada_b, 'norm_mix': norm_mix, 'w_in': w_in,
            'gdn_conv_w': gdn_conv_w, 'gdn_a_log': gdn_a_log, 'gdn_dt_bias': gdn_dt_bias, 'gdn_norm': gdn_norm,
            'ssd_conv_w': ssd_conv_w, 'ssd_conv_b': ssd_conv_b, 'ssd_a_log': ssd_a_log, 'ssd_dt_bias': ssd_dt_bias,
            'ssd_d': ssd_d, 'ssd_norm': ssd_norm,
            'lru_conv_w': lru_conv_w, 'lru_conv_b': lru_conv_b, 'lru_w_a': lru_w_a, 'lru_b_a': lru_b_a,
            'lru_w_x': lru_w_x, 'lru_b_x': lru_b_x, 'lru_lambda': lru_lambda,
            'w_branch': w_branch, 'w_out': w_out, 'norm_mlp': norm_mlp, 'w_up': w_up, 'w_down': w_down,
            'final_norm': final_norm}


def _fwd_reference(x, c, ada_w, ada_b, norm_mix, w_in, gdn_conv_w, gdn_a_log, gdn_dt_bias, gdn_norm,
              ssd_conv_w, ssd_conv_b, ssd_a_log, ssd_dt_bias, ssd_d, ssd_norm,
              lru_conv_w, lru_conv_b, lru_w_a, lru_b_a, lru_w_x, lru_b_x, lru_lambda,
              w_branch, w_out, norm_mlp, w_up, w_down, final_norm):
    for l in range(DEPTH):
        mod = jax.nn.silu(c) @ ada_w[l] + ada_b[l]
        sh1, sc1, gt1, sh2, sc2, gt2 = jnp.split(mod[:, None, :], N_MOD, axis=-1)
        h = rmsnorm(x, norm_mix[l]) * (1 + sc1) + sh1
        mix = hybrid_mixer(h, w_in[l], gdn_conv_w[l], gdn_a_log[l], gdn_dt_bias[l], gdn_norm[l],
                           ssd_conv_w[l], ssd_conv_b[l], ssd_a_log[l], ssd_dt_bias[l], ssd_d[l], ssd_norm[l],
                           lru_conv_w[l], lru_conv_b[l], lru_w_a[l], lru_b_a[l], lru_w_x[l], lru_b_x[l],
                           lru_lambda[l], w_branch[l], w_out[l])
        x = x + gt1 * mix
        h = rmsnorm(x, norm_mlp[l]) * (1 + sc2) + sh2
        x = x + gt2 * (jnp.square(jax.nn.relu(h @ w_up[l])) @ w_down[l])
    return rmsnorm(x, final_norm)


import jax as _jax
import jax.numpy as _jnp

TWIN_FORMAT = 'train_step'
FWD_PARAMS = ['x', 'c', 'ada_w', 'ada_b', 'norm_mix', 'w_in', 'gdn_conv_w', 'gdn_a_log', 'gdn_dt_bias', 'gdn_norm', 'ssd_conv_w', 'ssd_conv_b', 'ssd_a_log', 'ssd_dt_bias', 'ssd_d', 'ssd_norm', 'lru_conv_w', 'lru_conv_b', 'lru_w_a', 'lru_b_a', 'lru_w_x', 'lru_b_x', 'lru_lambda', 'w_branch', 'w_out', 'norm_mlp', 'w_up', 'w_down', 'final_norm']
TWIN_WEIGHTS = ['ada_w', 'ada_b', 'norm_mix', 'w_in', 'gdn_conv_w', 'gdn_a_log', 'gdn_dt_bias', 'gdn_norm', 'ssd_conv_w', 'ssd_conv_b', 'ssd_a_log', 'ssd_dt_bias', 'ssd_d', 'ssd_norm', 'lru_conv_w', 'lru_conv_b', 'lru_w_a', 'lru_b_a', 'lru_w_x', 'lru_b_x', 'lru_lambda', 'w_branch', 'w_out', 'norm_mlp', 'w_up', 'w_down', 'final_norm']
TWIN_DIFF_INPUT = 'x'
TWIN_INPUTS = ['x', 'c', 'ada_w', 'ada_b', 'norm_mix', 'w_in', 'gdn_conv_w', 'gdn_a_log', 'gdn_dt_bias', 'gdn_norm', 'ssd_conv_w', 'ssd_conv_b', 'ssd_a_log', 'ssd_dt_bias', 'ssd_d', 'ssd_norm', 'lru_conv_w', 'lru_conv_b', 'lru_w_a', 'lru_b_a', 'lru_w_x', 'lru_b_x', 'lru_lambda', 'w_branch', 'w_out', 'norm_mlp', 'w_up', 'w_down', 'final_norm', 'loss_target', 'm_ada_w', 'm_ada_b', 'm_norm_mix', 'm_w_in', 'm_gdn_conv_w', 'm_gdn_a_log', 'm_gdn_dt_bias', 'm_gdn_norm', 'm_ssd_conv_w', 'm_ssd_conv_b', 'm_ssd_a_log', 'm_ssd_dt_bias', 'm_ssd_d', 'm_ssd_norm', 'm_lru_conv_w', 'm_lru_conv_b', 'm_lru_w_a', 'm_lru_b_a', 'm_lru_w_x', 'm_lru_b_x', 'm_lru_lambda', 'm_w_branch', 'm_w_out', 'm_norm_mlp', 'm_w_up', 'm_w_down', 'm_final_norm', 'v_ada_w', 'v_ada_b', 'v_norm_mix', 'v_w_in', 'v_gdn_conv_w', 'v_gdn_a_log', 'v_gdn_dt_bias', 'v_gdn_norm', 'v_ssd_conv_w', 'v_ssd_conv_b', 'v_ssd_a_log', 'v_ssd_dt_bias', 'v_ssd_d', 'v_ssd_norm', 'v_lru_conv_w', 'v_lru_conv_b', 'v_lru_w_a', 'v_lru_b_a', 'v_lru_w_x', 'v_lru_b_x', 'v_lru_lambda', 'v_w_branch', 'v_w_out', 'v_norm_mlp', 'v_w_up', 'v_w_down', 'v_final_norm']
TWIN_OUTPUTS = ['loss', 'grad_x', 'grad_ada_w', 'grad_ada_b', 'grad_norm_mix', 'grad_w_in', 'grad_gdn_conv_w', 'grad_gdn_a_log', 'grad_gdn_dt_bias', 'grad_gdn_norm', 'grad_ssd_conv_w', 'grad_ssd_conv_b', 'grad_ssd_a_log', 'grad_ssd_dt_bias', 'grad_ssd_d', 'grad_ssd_norm', 'grad_lru_conv_w', 'grad_lru_conv_b', 'grad_lru_w_a', 'grad_lru_b_a', 'grad_lru_w_x', 'grad_lru_b_x', 'grad_lru_lambda', 'grad_w_branch', 'grad_w_out', 'grad_norm_mlp', 'grad_w_up', 'grad_w_down', 'grad_final_norm', 'delta_ada_w', 'delta_ada_b', 'delta_norm_mix', 'delta_w_in', 'delta_gdn_conv_w', 'delta_gdn_a_log', 'delta_gdn_dt_bias', 'delta_gdn_norm', 'delta_ssd_conv_w', 'delta_ssd_conv_b', 'delta_ssd_a_log', 'delta_ssd_dt_bias', 'delta_ssd_d', 'delta_ssd_norm', 'delta_lru_conv_w', 'delta_lru_conv_b', 'delta_lru_w_a', 'delta_lru_b_a', 'delta_lru_w_x', 'delta_lru_b_x', 'delta_lru_lambda', 'delta_w_branch', 'delta_w_out', 'delta_norm_mlp', 'delta_w_up', 'delta_w_down', 'delta_final_norm', 'new_m_ada_w', 'new_m_ada_b', 'new_m_norm_mix', 'new_m_w_in', 'new_m_gdn_conv_w', 'new_m_gdn_a_log', 'new_m_gdn_dt_bias', 'new_m_gdn_norm', 'new_m_ssd_conv_w', 'new_m_ssd_conv_b', 'new_m_ssd_a_log', 'new_m_ssd_dt_bias', 'new_m_ssd_d', 'new_m_ssd_norm', 'new_m_lru_conv_w', 'new_m_lru_conv_b', 'new_m_lru_w_a', 'new_m_lru_b_a', 'new_m_lru_w_x', 'new_m_lru_b_x', 'new_m_lru_lambda', 'new_m_w_branch', 'new_m_w_out', 'new_m_norm_mlp', 'new_m_w_up', 'new_m_w_down', 'new_m_final_norm', 'new_v_ada_w', 'new_v_ada_b', 'new_v_norm_mix', 'new_v_w_in', 'new_v_gdn_conv_w', 'new_v_gdn_a_log', 'new_v_gdn_dt_bias', 'new_v_gdn_norm', 'new_v_ssd_conv_w', 'new_v_ssd_conv_b', 'new_v_ssd_a_log', 'new_v_ssd_dt_bias', 'new_v_ssd_d', 'new_v_ssd_norm', 'new_v_lru_conv_w', 'new_v_lru_conv_b', 'new_v_lru_w_a', 'new_v_lru_b_a', 'new_v_lru_w_x', 'new_v_lru_b_x', 'new_v_lru_lambda', 'new_v_w_branch', 'new_v_w_out', 'new_v_norm_mlp', 'new_v_w_up', 'new_v_w_down', 'new_v_final_norm']
TWIN_LEAF_KINDS = {'loss': 'loss', 'grad_x': 'grad_x', 'grad_ada_w': 'grad_w', 'grad_ada_b': 'grad_w', 'grad_norm_mix': 'grad_w', 'grad_w_in': 'grad_w', 'grad_gdn_conv_w': 'grad_w', 'grad_gdn_a_log': 'grad_w', 'grad_gdn_dt_bias': 'grad_w', 'grad_gdn_norm': 'grad_w', 'grad_ssd_conv_w': 'grad_w', 'grad_ssd_conv_b': 'grad_w', 'grad_ssd_a_log': 'grad_w', 'grad_ssd_dt_bias': 'grad_w', 'grad_ssd_d': 'grad_w', 'grad_ssd_norm': 'grad_w', 'grad_lru_conv_w': 'grad_w', 'grad_lru_conv_b': 'grad_w', 'grad_lru_w_a': 'grad_w', 'grad_lru_b_a': 'grad_w', 'grad_lru_w_x': 'grad_w', 'grad_lru_b_x': 'grad_w', 'grad_lru_lambda': 'grad_w', 'grad_w_branch': 'grad_w', 'grad_w_out': 'grad_w', 'grad_norm_mlp': 'grad_w', 'grad_w_up': 'grad_w', 'grad_w_down': 'grad_w', 'grad_final_norm': 'grad_w', 'delta_ada_w': 'delta_w', 'delta_ada_b': 'delta_w', 'delta_norm_mix': 'delta_w', 'delta_w_in': 'delta_w', 'delta_gdn_conv_w': 'delta_w', 'delta_gdn_a_log': 'delta_w', 'delta_gdn_dt_bias': 'delta_w', 'delta_gdn_norm': 'delta_w', 'delta_ssd_conv_w': 'delta_w', 'delta_ssd_conv_b': 'delta_w', 'delta_ssd_a_log': 'delta_w', 'delta_ssd_dt_bias': 'delta_w', 'delta_ssd_d': 'delta_w', 'delta_ssd_norm': 'delta_w', 'delta_lru_conv_w': 'delta_w', 'delta_lru_conv_b': 'delta_w', 'delta_lru_w_a': 'delta_w', 'delta_lru_b_a': 'delta_w', 'delta_lru_w_x': 'delta_w', 'delta_lru_b_x': 'delta_w', 'delta_lru_lambda': 'delta_w', 'delta_w_branch': 'delta_w', 'delta_w_out': 'delta_w', 'delta_norm_mlp': 'delta_w', 'delta_w_up': 'delta_w', 'delta_w_down': 'delta_w', 'delta_final_norm': 'delta_w', 'new_m_ada_w': 'new_m', 'new_m_ada_b': 'new_m', 'new_m_norm_mix': 'new_m', 'new_m_w_in': 'new_m', 'new_m_gdn_conv_w': 'new_m', 'new_m_gdn_a_log': 'new_m', 'new_m_gdn_dt_bias': 'new_m', 'new_m_gdn_norm': 'new_m', 'new_m_ssd_conv_w': 'new_m', 'new_m_ssd_conv_b': 'new_m', 'new_m_ssd_a_log': 'new_m', 'new_m_ssd_dt_bias': 'new_m', 'new_m_ssd_d': 'new_m', 'new_m_ssd_norm': 'new_m', 'new_m_lru_conv_w': 'new_m', 'new_m_lru_conv_b': 'new_m', 'new_m_lru_w_a': 'new_m', 'new_m_lru_b_a': 'new_m', 'new_m_lru_w_x': 'new_m', 'new_m_lru_b_x': 'new_m', 'new_m_lru_lambda': 'new_m', 'new_m_w_branch': 'new_m', 'new_m_w_out': 'new_m', 'new_m_norm_mlp': 'new_m', 'new_m_w_up': 'new_m', 'new_m_w_down': 'new_m', 'new_m_final_norm': 'new_m', 'new_v_ada_w': 'new_v', 'new_v_ada_b': 'new_v', 'new_v_norm_mix': 'new_v', 'new_v_w_in': 'new_v', 'new_v_gdn_conv_w': 'new_v', 'new_v_gdn_a_log': 'new_v', 'new_v_gdn_dt_bias': 'new_v', 'new_v_gdn_norm': 'new_v', 'new_v_ssd_conv_w': 'new_v', 'new_v_ssd_conv_b': 'new_v', 'new_v_ssd_a_log': 'new_v', 'new_v_ssd_dt_bias': 'new_v', 'new_v_ssd_d': 'new_v', 'new_v_ssd_norm': 'new_v', 'new_v_lru_conv_w': 'new_v', 'new_v_lru_conv_b': 'new_v', 'new_v_lru_w_a': 'new_v', 'new_v_lru_b_a': 'new_v', 'new_v_lru_w_x': 'new_v', 'new_v_lru_b_x': 'new_v', 'new_v_lru_lambda': 'new_v', 'new_v_w_branch': 'new_v', 'new_v_w_out': 'new_v', 'new_v_norm_mlp': 'new_v', 'new_v_w_up': 'new_v', 'new_v_w_down': 'new_v', 'new_v_final_norm': 'new_v'}


def _forward(args):
    return _fwd_reference(*[args[k] for k in FWD_PARAMS])


def _output_shape():
    def fwd():
        inp = _fwd_setup_inputs(0)
        return _fwd_reference(*[inp[k] for k in FWD_PARAMS])
    out = _jax.eval_shape(fwd)
    return out.shape, out.dtype

N_MICROBATCH = 1
ADAM_LR = 0.001
ADAM_B1 = 0.9
ADAM_B2 = 0.999
ADAM_EPS = 1e-08
ADAM_WD = 0.01
ADAM_STEP = 10
PER_EXAMPLE_BATCH_AXIS = {'x': 0, 'c': 0, 'loss_target': 0}
SHARED_INPUTS = []
_WEIGHT_DTYPES = {'ada_w': _jnp.float32, 'ada_b': _jnp.float32, 'norm_mix': _jnp.float32, 'w_in': _jnp.float32, 'gdn_conv_w': _jnp.float32, 'gdn_a_log': _jnp.float32, 'gdn_dt_bias': _jnp.float32, 'gdn_norm': _jnp.float32, 'ssd_conv_w': _jnp.float32, 'ssd_conv_b': _jnp.float32, 'ssd_a_log': _jnp.float32, 'ssd_dt_bias': _jnp.float32, 'ssd_d': _jnp.float32, 'ssd_norm': _jnp.float32, 'lru_conv_w': _jnp.float32, 'lru_conv_b': _jnp.float32, 'lru_w_a': _jnp.float32, 'lru_b_a': _jnp.float32, 'lru_w_x': _jnp.float32, 'lru_b_x': _jnp.float32, 'lru_lambda': _jnp.float32, 'w_branch': _jnp.float32, 'w_out': _jnp.float32, 'norm_mlp': _jnp.float32, 'w_up': _jnp.float32, 'w_down': _jnp.float32, 'final_norm': _jnp.float32}
MOMENT_SCALE = {'ada_w': 2.784065e-01, 'ada_b': 6.306589e-01, 'norm_mix': 5.598164e-02, 'w_in': 2.408983e-02, 'gdn_conv_w': 1.762695e-02, 'gdn_a_log': 1.129777e-01, 'gdn_dt_bias': 1.096645e-01, 'gdn_norm': 4.734450e-02, 'ssd_conv_w': 2.891000e-02, 'ssd_conv_b': 3.679379e-02, 'ssd_a_log': 1.655210e-01, 'ssd_dt_bias': 9.197045e-02, 'ssd_d': 4.941864e-01, 'ssd_norm': 3.586286e-02, 'lru_conv_w': 5.201906e-02, 'lru_conv_b': 2.494796e-01, 'lru_w_a': 8.475074e-03, 'lru_b_a': 1.036329e-02, 'lru_w_x': 1.616546e-02, 'lru_b_x': 2.058786e-02, 'lru_lambda': 2.407168e-02, 'w_branch': 2.589903e-02, 'w_out': 4.507436e-02, 'norm_mlp': 6.789089e-02, 'w_up': 3.569861e-02, 'w_down': 1.012445e-01, 'final_norm': 6.422312e+01}


def _to_microbatches(a, axis):
    t = _jnp.moveaxis(a, axis, 0)
    t = t.reshape((N_MICROBATCH, t.shape[0] // N_MICROBATCH) + t.shape[1:])
    return _jnp.moveaxis(t, 1, axis + 1)


def setup_inputs(seed: int = 0) -> dict:
    inp = _fwd_setup_inputs(seed)
    key = _jax.random.fold_in(_jax.random.key(seed), 7919)
    shape, _ = _output_shape()
    out = dict(inp)
    out["loss_target"] = _jax.random.normal(_jax.random.fold_in(key, 0), shape, _jnp.float32)
    for i, name in enumerate(TWIN_WEIGHTS):
        w = inp[name].astype(_jnp.float32)
        if MOMENT_SCALE is None:
            s = _jnp.sqrt(_jnp.mean(_jnp.square(w)) + 1e-30)
        else:
            s = MOMENT_SCALE[name]
        km, kv = _jax.random.split(_jax.random.fold_in(key, i + 1))
        out[name] = w
        out["m_" + name] = s * _jax.random.normal(km, w.shape, _jnp.float32)
        out["v_" + name] = (s * s) * _jax.random.uniform(kv, w.shape, _jnp.float32, 0.5, 1.5)
    if N_MICROBATCH > 1:
        for name, axis in PER_EXAMPLE_BATCH_AXIS.items():
            out[name] = _to_microbatches(out[name], axis)
    return {'x': out['x'], 'c': out['c'], 'ada_w': out['ada_w'], 'ada_b': out['ada_b'], 'norm_mix': out['norm_mix'], 'w_in': out['w_in'], 'gdn_conv_w': out['gdn_conv_w'], 'gdn_a_log': out['gdn_a_log'], 'gdn_dt_bias': out['gdn_dt_bias'], 'gdn_norm': out['gdn_norm'], 'ssd_conv_w': out['ssd_conv_w'], 'ssd_conv_b': out['ssd_conv_b'], 'ssd_a_log': out['ssd_a_log'], 'ssd_dt_bias': out['ssd_dt_bias'], 'ssd_d': out['ssd_d'], 'ssd_norm': out['ssd_norm'], 'lru_conv_w': out['lru_conv_w'], 'lru_conv_b': out['lru_conv_b'], 'lru_w_a': out['lru_w_a'], 'lru_b_a': out['lru_b_a'], 'lru_w_x': out['lru_w_x'], 'lru_b_x': out['lru_b_x'], 'lru_lambda': out['lru_lambda'], 'w_branch': out['w_branch'], 'w_out': out['w_out'], 'norm_mlp': out['norm_mlp'], 'w_up': out['w_up'], 'w_down': out['w_down'], 'final_norm': out['final_norm'], 'loss_target': out['loss_target'], 'm_ada_w': out['m_ada_w'], 'm_ada_b': out['m_ada_b'], 'm_norm_mix': out['m_norm_mix'], 'm_w_in': out['m_w_in'], 'm_gdn_conv_w': out['m_gdn_conv_w'], 'm_gdn_a_log': out['m_gdn_a_log'], 'm_gdn_dt_bias': out['m_gdn_dt_bias'], 'm_gdn_norm': out['m_gdn_norm'], 'm_ssd_conv_w': out['m_ssd_conv_w'], 'm_ssd_conv_b': out['m_ssd_conv_b'], 'm_ssd_a_log': out['m_ssd_a_log'], 'm_ssd_dt_bias': out['m_ssd_dt_bias'], 'm_ssd_d': out['m_ssd_d'], 'm_ssd_norm': out['m_ssd_norm'], 'm_lru_conv_w': out['m_lru_conv_w'], 'm_lru_conv_b': out['m_lru_conv_b'], 'm_lru_w_a': out['m_lru_w_a'], 'm_lru_b_a': out['m_lru_b_a'], 'm_lru_w_x': out['m_lru_w_x'], 'm_lru_b_x': out['m_lru_b_x'], 'm_lru_lambda': out['m_lru_lambda'], 'm_w_branch': out['m_w_branch'], 'm_w_out': out['m_w_out'], 'm_norm_mlp': out['m_norm_mlp'], 'm_w_up': out['m_w_up'], 'm_w_down': out['m_w_down'], 'm_final_norm': out['m_final_norm'], 'v_ada_w': out['v_ada_w'], 'v_ada_b': out['v_ada_b'], 'v_norm_mix': out['v_norm_mix'], 'v_w_in': out['v_w_in'], 'v_gdn_conv_w': out['v_gdn_conv_w'], 'v_gdn_a_log': out['v_gdn_a_log'], 'v_gdn_dt_bias': out['v_gdn_dt_bias'], 'v_gdn_norm': out['v_gdn_norm'], 'v_ssd_conv_w': out['v_ssd_conv_w'], 'v_ssd_conv_b': out['v_ssd_conv_b'], 'v_ssd_a_log': out['v_ssd_a_log'], 'v_ssd_dt_bias': out['v_ssd_dt_bias'], 'v_ssd_d': out['v_ssd_d'], 'v_ssd_norm': out['v_ssd_norm'], 'v_lru_conv_w': out['v_lru_conv_w'], 'v_lru_conv_b': out['v_lru_conv_b'], 'v_lru_w_a': out['v_lru_w_a'], 'v_lru_b_a': out['v_lru_b_a'], 'v_lru_w_x': out['v_lru_w_x'], 'v_lru_b_x': out['v_lru_b_x'], 'v_lru_lambda': out['v_lru_lambda'], 'v_w_branch': out['v_w_branch'], 'v_w_out': out['v_w_out'], 'v_norm_mlp': out['v_norm_mlp'], 'v_w_up': out['v_w_up'], 'v_w_down': out['v_w_down'], 'v_final_norm': out['v_final_norm']}


def _loss(weights, diff, rest, loss_target):
    with _jax.named_scope("forward"):
        args = {**rest, TWIN_DIFF_INPUT: diff, **{k: w.astype(_WEIGHT_DTYPES[k]) for k, w in weights.items()}}
        y = _forward(args)
    with _jax.named_scope("loss_head"):
        err = _jnp.square(y.astype(_jnp.float32) - loss_target)
        return 0.5 * _jnp.sum(_jnp.mean(err, axis=-1)) if err.ndim else 0.5 * err


def _adamw(w, g, m, v):
    m = ADAM_B1 * m + (1.0 - ADAM_B1) * g
    v = ADAM_B2 * v + (1.0 - ADAM_B2) * _jnp.square(g)
    m_hat = m / (1.0 - ADAM_B1 ** ADAM_STEP)
    v_hat = v / (1.0 - ADAM_B2 ** ADAM_STEP)
    delta = -ADAM_LR * (m_hat / (_jnp.sqrt(v_hat) + ADAM_EPS) + ADAM_WD * w)
    return delta, m, v


def reference(x, c, ada_w, ada_b, norm_mix, w_in, gdn_conv_w, gdn_a_log, gdn_dt_bias, gdn_norm, ssd_conv_w, ssd_conv_b, ssd_a_log, ssd_dt_bias, ssd_d, ssd_norm, lru_conv_w, lru_conv_b, lru_w_a, lru_b_a, lru_w_x, lru_b_x, lru_lambda, w_branch, w_out, norm_mlp, w_up, w_down, final_norm, loss_target, m_ada_w, m_ada_b, m_norm_mix, m_w_in, m_gdn_conv_w, m_gdn_a_log, m_gdn_dt_bias, m_gdn_norm, m_ssd_conv_w, m_ssd_conv_b, m_ssd_a_log, m_ssd_dt_bias, m_ssd_d, m_ssd_norm, m_lru_conv_w, m_lru_conv_b, m_lru_w_a, m_lru_b_a, m_lru_w_x, m_lru_b_x, m_lru_lambda, m_w_branch, m_w_out, m_norm_mlp, m_w_up, m_w_down, m_final_norm, v_ada_w, v_ada_b, v_norm_mix, v_w_in, v_gdn_conv_w, v_gdn_a_log, v_gdn_dt_bias, v_gdn_norm, v_ssd_conv_w, v_ssd_conv_b, v_ssd_a_log, v_ssd_dt_bias, v_ssd_d, v_ssd_norm, v_lru_conv_w, v_lru_conv_b, v_lru_w_a, v_lru_b_a, v_lru_w_x, v_lru_b_x, v_lru_lambda, v_w_branch, v_w_out, v_norm_mlp, v_w_up, v_w_down, v_final_norm):
    given = dict(x=x, c=c, ada_w=ada_w, ada_b=ada_b, norm_mix=norm_mix, w_in=w_in, gdn_conv_w=gdn_conv_w, gdn_a_log=gdn_a_log, gdn_dt_bias=gdn_dt_bias, gdn_norm=gdn_norm, ssd_conv_w=ssd_conv_w, ssd_conv_b=ssd_conv_b, ssd_a_log=ssd_a_log, ssd_dt_bias=ssd_dt_bias, ssd_d=ssd_d, ssd_norm=ssd_norm, lru_conv_w=lru_conv_w, lru_conv_b=lru_conv_b, lru_w_a=lru_w_a, lru_b_a=lru_b_a, lru_w_x=lru_w_x, lru_b_x=lru_b_x, lru_lambda=lru_lambda, w_branch=w_branch, w_out=w_out, norm_mlp=norm_mlp, w_up=w_up, w_down=w_down, final_norm=final_norm, loss_target=loss_target, m_ada_w=m_ada_w, m_ada_b=m_ada_b, m_norm_mix=m_norm_mix, m_w_in=m_w_in, m_gdn_conv_w=m_gdn_conv_w, m_gdn_a_log=m_gdn_a_log, m_gdn_dt_bias=m_gdn_dt_bias, m_gdn_norm=m_gdn_norm, m_ssd_conv_w=m_ssd_conv_w, m_ssd_conv_b=m_ssd_conv_b, m_ssd_a_log=m_ssd_a_log, m_ssd_dt_bias=m_ssd_dt_bias, m_ssd_d=m_ssd_d, m_ssd_norm=m_ssd_norm, m_lru_conv_w=m_lru_conv_w, m_lru_conv_b=m_lru_conv_b, m_lru_w_a=m_lru_w_a, m_lru_b_a=m_lru_b_a, m_lru_w_x=m_lru_w_x, m_lru_b_x=m_lru_b_x, m_lru_lambda=m_lru_lambda, m_w_branch=m_w_branch, m_w_out=m_w_out, m_norm_mlp=m_norm_mlp, m_w_up=m_w_up, m_w_down=m_w_down, m_final_norm=m_final_norm, v_ada_w=v_ada_w, v_ada_b=v_ada_b, v_norm_mix=v_norm_mix, v_w_in=v_w_in, v_gdn_conv_w=v_gdn_conv_w, v_gdn_a_log=v_gdn_a_log, v_gdn_dt_bias=v_gdn_dt_bias, v_gdn_norm=v_gdn_norm, v_ssd_conv_w=v_ssd_conv_w, v_ssd_conv_b=v_ssd_conv_b, v_ssd_a_log=v_ssd_a_log, v_ssd_dt_bias=v_ssd_dt_bias, v_ssd_d=v_ssd_d, v_ssd_norm=v_ssd_norm, v_lru_conv_w=v_lru_conv_w, v_lru_conv_b=v_lru_conv_b, v_lru_w_a=v_lru_w_a, v_lru_b_a=v_lru_b_a, v_lru_w_x=v_lru_w_x, v_lru_b_x=v_lru_b_x, v_lru_lambda=v_lru_lambda, v_w_branch=v_w_branch, v_w_out=v_w_out, v_norm_mlp=v_norm_mlp, v_w_up=v_w_up, v_w_down=v_w_down, v_final_norm=v_final_norm)
    weights = {n: given[n] for n in TWIN_WEIGHTS}
    shared = {n: given[n] for n in SHARED_INPUTS}
    per_example = {n: given[n] for n in ['x', 'c']}
    grad_fn = _jax.value_and_grad(_loss, argnums=(0, 1))

    def one_microbatch(ex, loss_target):
        ex = dict(ex)
        diff = ex.pop(TWIN_DIFF_INPUT)
        return grad_fn(weights, diff, {**shared, **ex}, loss_target)

    if N_MICROBATCH == 1:
        loss, (grad_w, grad_x) = one_microbatch(per_example, given["loss_target"])
    else:
        def body(carry, xs):
            loss_sum, grad_sum = carry
            l_k, (gw_k, gx_k) = one_microbatch(xs[0], xs[1])
            with _jax.named_scope("update"):
                return (loss_sum + l_k, _jax.tree.map(_jnp.add, grad_sum, gw_k)), gx_k

        init = (_jnp.zeros((), _jnp.float32), _jax.tree.map(_jnp.zeros_like, weights))
        (loss, grad_w), grad_x = _jax.lax.scan(body, init, (per_example, given["loss_target"]))
    with _jax.named_scope("update"):
        delta_w, new_m, new_v = {}, {}, {}
        for n in TWIN_WEIGHTS:
            delta_w[n], new_m[n], new_v[n] = _adamw(weights[n], grad_w[n], given["m_" + n], given["v_" + n])
    return (loss, grad_x, *[grad_w[n] for n in TWIN_WEIGHTS], *[delta_w[n] for n in TWIN_WEIGHTS],
            *[new_m[n] for n in TWIN_WEIGHTS], *[new_v[n] for n in TWIN_WEIGHTS])
```

```python
import functools
import math

import jax
import jax.numpy as jnp
from jax import lax
from jax.experimental import pallas as pl
from jax.experimental.pallas import tpu as pltpu

F32 = jnp.float32
BF16 = jnp.bfloat16

D_MODEL = 1024
DEPTH = 2
RMS_EPS = 1e-6
CHUNK = 128
GDN_HEADS = 4
SSD_HEADS = 8
LRU_C = 8.0
D_FF = 4096
N_MOD = 6
W_GDN = 2176
W_SSD = 1664
W_LRU = 512
W_GATE = 3072
ADAM_LR = 0.001
ADAM_B1 = 0.9
ADAM_B2 = 0.999
ADAM_EPS = 1e-08
ADAM_WD = 0.01
ADAM_STEP = 10
VMEM_LIMIT = 56 * 1024 * 1024
MESH = pl.DeviceIdType.MESH


def _dot(a, b, ta, tb):
    dn = (((0 if ta else 1,), (1 if tb else 0,)), ((), ()))
    return lax.dot_general(a.astype(BF16), b.astype(BF16), dn, preferred_element_type=F32)


@functools.partial(jax.custom_vjp, nondiff_argnums=(2, 3))
def mm(a, b, ta, tb):
    return _dot(a, b, ta, tb)


def _mm_fwd(a, b, ta, tb):
    return _dot(a, b, ta, tb), (a, b)


def _mm_bwd(ta, tb, res, g):
    a, b = res
    if not ta and not tb:
        return mm(g, b, False, True), mm(a, g, True, False)
    if not ta and tb:
        return mm(g, b, False, False), mm(g, a, True, False)
    assert ta and not tb
    return mm(b, g, False, True), mm(a, g, False, False)


mm.defvjp(_mm_fwd, _mm_bwd)


def _tri_apply(x, upper):
    t = x.shape[0]
    r = lax.broadcasted_iota(jnp.int32, (t, t), 0)
    c = lax.broadcasted_iota(jnp.int32, (t, t), 1)
    tri = jnp.where((r <= c) if upper else (r >= c), 1.0, 0.0).astype(BF16)
    x1 = x.astype(BF16)
    r1 = x - x1.astype(F32)
    x2 = r1.astype(BF16)
    x3 = (r1 - x2.astype(F32)).astype(BF16)
    d = lambda p: jnp.dot(tri, p, preferred_element_type=F32)
    return (d(x1) + d(x2)) + d(x3)


@jax.custom_vjp
def cumsum_rows(x):
    return _tri_apply(x, False)


cumsum_rows.defvjp(lambda x: (_tri_apply(x, False), None), lambda _, g: (_tri_apply(g, True),))


def _dot_split(a, b):
    a1, b1 = a.astype(BF16), b.astype(BF16)
    a2, b2 = (a - a1.astype(F32)).astype(BF16), (b - b1.astype(F32)).astype(BF16)
    d = lambda p, q: jnp.dot(p, q, preferred_element_type=F32)
    return d(a1, b1) + (d(a1, b2) + d(a2, b1))


def _neumann(m):
    t = m.shape[0]
    x = -m
    q = _dot(m, m, False, False)
    n = 2
    while True:
        x = x + q + _dot(x, q, False, False)
        n *= 2
        if n >= t:
            break
        q = _dot(q, q, False, False)
    r = -(x + m + _dot_split(m, x))
    return x + r + _dot(x, r, False, False)


@jax.custom_vjp
def tri_solve(m, rhs):
    return rhs + _dot(_neumann(m), rhs, False, False)


def _tri_solve_fwd(m, rhs):
    x = _neumann(m)
    sol = rhs + _dot(x, rhs, False, False)
    return sol, (x, sol)


def _tri_solve_bwd(res, g):
    x, sol = res
    d_rhs = g + _dot(x, g, True, False)
    return -_dot(d_rhs, sol, False, True), d_rhs


tri_solve.defvjp(_tri_solve_fwd, _tri_solve_bwd)


@functools.partial(jax.custom_vjp, nondiff_argnums=(1,))
def split_cols(x, sizes):
    out, o = [], 0
    for s in sizes:
        out.append(x[:, o:o + s])
        o += s
    return tuple(out)


split_cols.defvjp(lambda x, sizes: (split_cols(x, sizes), None),
                  lambda sizes, _, g: (jnp.concatenate(list(g), axis=1),))


@functools.partial(jax.custom_vjp, nondiff_argnums=(1,))
def _last_rows(x, t):
    return x[t - 8:, :]


_last_rows.defvjp(lambda x, t: (_last_rows(x, t), None),
                  lambda t, _, g: (jnp.concatenate([jnp.zeros((t - 8, g.shape[1]), g.dtype), g], axis=0),))


def last8(x):
    return _last_rows(x, x.shape[0])


def _shifted(xp, d, t):
    return (pltpu.roll(xp, d, 0) if d else xp)[8:8 + t, :]


@jax.custom_vjp
def conv4(x, tail, w):
    t = x.shape[0]
    xp = jnp.concatenate([tail, x], axis=0)
    return sum(_shifted(xp, 3 - k, t) * w[k:k + 1, :] for k in range(4))


def _conv4_fwd(x, tail, w):
    return conv4(x, tail, w), (x, tail, w)


def _conv4_bwd(res, g):
    x, tail, w = res
    t = x.shape[0]
    xp = jnp.concatenate([tail, x], axis=0)
    zero8 = jnp.zeros((8, g.shape[1]), g.dtype)
    d_xp = jnp.zeros_like(xp)
    d_w = []
    for k in range(4):
        gk = jnp.concatenate([zero8, g * w[k:k + 1, :]], axis=0)
        d_xp = d_xp + (pltpu.roll(gk, t + 8 - (3 - k), 0) if k < 3 else gk)
        d_w.append(jnp.sum(g * _shifted(xp, 3 - k, t), axis=0, keepdims=True))
    return d_xp[8:, :], d_xp[:8, :], jnp.concatenate(d_w, axis=0)


conv4.defvjp(_conv4_fwd, _conv4_bwd)


def _sigmoid(x):
    return 0.5 * (jnp.tanh(0.5 * x) + 1.0)


def _silu(x):
    return x * _sigmoid(x)


def _softplus(x):
    ax = jnp.where(x > 0, x, -x)
    return jnp.where(x > 0, x, 0.0) + jnp.log(1.0 + jnp.exp(-ax))


def _gelu(x):
    return 0.5 * x * (1.0 + jnp.tanh(math.sqrt(2.0 / math.pi) * (x + 0.044715 * (x * x * x))))


def _expm1(x):
    series = x * (1.0 + x * (0.5 + x * (1.0 / 6.0 + x * (1.0 / 24.0))))
    return jnp.where(jnp.abs(x) < 0.03, series, jnp.exp(x) - 1.0)


def _rms(x, w):
    return x * lax.rsqrt(jnp.mean(x * x, axis=-1, keepdims=True) + RMS_EPS) * w


def _lane_pick(x, j):
    lane = lax.broadcasted_iota(jnp.int32, (1, x.shape[1]), 1)
    return jnp.sum(jnp.where(lane == j, x, 0.0), axis=1, keepdims=True)


def _row_pick(x, j):
    row = lax.broadcasted_iota(jnp.int32, (x.shape[0], 1), 0)
    return jnp.sum(jnp.where(row == j, x, 0.0), axis=0, keepdims=True)


def gdn_fn(carry, seq, params):
    *states, tail = carry
    (tile,) = seq
    conv_w, alog_row, dtb_row, norm_w = params
    t = tile.shape[0]
    qkv_raw, z, sm = split_cols(tile, (1536, 512, 128))
    qkv = _silu(conv4(qkv_raw, tail, conv_w))
    parts = split_cols(qkv, (128,) * 12)
    zs = split_cols(z, (128,) * 4)
    lane = lax.broadcasted_iota(jnp.int32, (1, 128), 1)
    beta_all = _sigmoid(sm)
    g_all = jnp.where((lane >= 4) & (lane < 8), -jnp.exp(alog_row) * _softplus(sm + dtb_row), 0.0)
    gc_all = cumsum_rows(g_all)
    gr_all = gc_all.T
    gl_all = _row_pick(gc_all, t - 1)
    r = lax.broadcasted_iota(jnp.int32, (t, t), 0)
    c = lax.broadcasted_iota(jnp.int32, (t, t), 1)
    outs, new_states = [], []
    for h in range(GDN_HEADS):
        q, k, v = parts[h], parts[4 + h], parts[8 + h]
        qn = q * lax.rsqrt(jnp.sum(q * q, axis=-1, keepdims=True) + RMS_EPS) * (128.0 ** -0.5)
        kn = k * lax.rsqrt(jnp.sum(k * k, axis=-1, keepdims=True) + RMS_EPS)
        beta = _lane_pick(beta_all, h)
        gc = _lane_pick(gc_all, 4 + h)
        gr = _row_pick(gr_all, 4 + h)
        gl = _lane_pick(gl_all, 4 + h)
        decay = jnp.exp(jnp.where(r >= c, gc - gr, -1e30))
        m = jnp.where(r > c, beta * mm(kn, kn, False, True) * decay, 0.0)
        eg = jnp.exp(gc)
        sol = tri_solve(m, jnp.concatenate([beta * v, (beta * eg) * kn], axis=1))
        u, w = split_cols(sol, (128, 128))
        qk = mm(qn, kn, False, True) * decay
        v_new = u - mm(w, states[h], False, False)
        o = mm(qn * eg, states[h], False, False) + mm(qk, v_new, False, False)
        new_states.append(states[h] * jnp.exp(gl) + mm(kn * jnp.exp(gl - gc), v_new, True, False))
        outs.append(_rms(o, norm_w) * _silu(zs[h]))
    return (*new_states, last8(qkv_raw)), (jnp.concatenate(outs, axis=1),)


def ssd_fn(carry, seq, params):
    *states, tail = carry
    (tile,) = seq
    conv_w, conv_b, alog_row, dtb_row, d_row, norm_w = params
    t = tile.shape[0]
    xbc_raw, z, sm = split_cols(tile, (1024, 512, 128))
    xbc = _silu(conv4(xbc_raw, tail, conv_w) + conv_b)
    x0, x1, x2, x3, b0, b1, c0, c1 = split_cols(xbc, (128,) * 8)
    xs, bs, cs = (x0, x1, x2, x3), (b0, b1), (c0, c1)
    ds = split_cols(d_row, (128,) * 4)
    lane = lax.broadcasted_iota(jnp.int32, (1, 128), 1)
    sub = lax.broadcasted_iota(jnp.int32, (128, 1), 0)
    low = lane < 64
    dt_all = jnp.where(lane < SSD_HEADS, _softplus(sm + dtb_row), 0.0)
    ac_all = cumsum_rows(dt_all * (-jnp.exp(alog_row)))
    ar_all = ac_all.T
    al_all = _row_pick(ac_all, t - 1)
    r = lax.broadcasted_iota(jnp.int32, (t, t), 0)
    c = lax.broadcasted_iota(jnp.int32, (t, t), 1)
    ys, new_states = [], []
    for p in range(4):
        g = p // 2
        cb = mm(cs[g], bs[g], False, True)
        col = [_lane_pick(ac_all, 2 * p + j) for j in range(2)]
        row = [_row_pick(ar_all, 2 * p + j) for j in range(2)]
        last = [_lane_pick(al_all, 2 * p + j) for j in range(2)]
        dt = [_lane_pick(dt_all, 2 * p + j) for j in range(2)]
        xdt = xs[p] * jnp.where(low, dt[0], dt[1])
        y = ds[p] * xs[p]
        for j in range(2):
            lm = jnp.exp(jnp.where(r >= c, col[j] - row[j], -1e30))
            y = y + mm(cb * lm, jnp.where(low if j == 0 else ~low, xdt, 0.0), False, False)
        y = y + mm(cs[g], states[p], False, True) * jnp.where(low, jnp.exp(col[0]), jnp.exp(col[1]))
        decay_end = jnp.where(low, jnp.exp(last[0] - col[0]), jnp.exp(last[1] - col[1]))
        st = mm(xdt * decay_end, bs[g], True, False)
        new_states.append(states[p] * jnp.where(sub < 64, jnp.exp(last[0]), jnp.exp(last[1])) + st)
        ys.append(y)
    gz = jnp.concatenate(ys, axis=1) * _silu(z)
    g0, g1 = split_cols(gz, (256, 256))
    n0, n1 = split_cols(norm_w, (256, 256))
    out = jnp.concatenate([_rms(g0, n0), _rms(g1, n1)], axis=1)
    return (*new_states, last8(xbc_raw)), (out,)


def lru_in_fn(carry, seq, params):
    (tail,) = carry
    (x,) = seq
    conv_w, conv_b, w_a, b_a, w_x, b_x, lam = params
    xc = conv4(x, tail, conv_w) + conv_b
    r = _sigmoid(mm(xc, w_a, False, False) + b_a)
    i = _sigmoid(mm(xc, w_x, False, False) + b_x)
    log_a = -LRU_C * r * _softplus(-lam)
    u = jnp.sqrt(-_expm1(2.0 * log_a)) * (i * xc)
    return (last8(x),), (jnp.exp(log_a), u)


def lru_out_fn(carry, seq, params):
    hs, gate = seq
    return (), (hs * _gelu(gate),)


def merge_fn(carry, seq, params):
    ya, yb, yc, gl = seq
    g = split_cols(_sigmoid(gl), (D_MODEL,) * 3)
    merged = sum(g[r] * mm(y, params[r], False, False) for r, y in enumerate((ya, yb, yc)))
    return (), (merged,)


def _adaln(x, w, sc, sh):
    return _rms(x, w) * (1.0 + sc) + sh


def norm1_fn(carry, seq, params):
    (x,) = seq
    return (), (_adaln(x, *params), x)


def resid_norm_fn(carry, seq, params):
    x, mix = seq
    gt, w, sc, sh = params
    x1 = x + gt * mix
    return (), (_adaln(x1, w, sc, sh), x1)


def act_fn(carry, seq, params):
    (up,) = seq
    r = jnp.where(up > 0, up, 0.0)
    return (), (r * r,)


def resid_fn(carry, seq, params):
    x, dn = seq
    (gt,) = params
    return (), (x + gt * dn,)


def silu_fn(carry, seq, params):
    return (), (_silu(seq[0]),)


def _full_spec(a):
    nd = a.ndim
    return pl.BlockSpec(a.shape, lambda i: (0,) * nd)


def _cparams(*sem):
    return pltpu.CompilerParams(dimension_semantics=sem, vmem_limit_bytes=VMEM_LIMIT)


def scan_fwd(fn, name, tile, seqs, params, carry_shapes, outs, save_carry=False):
    rows = seqs[0].shape[0]
    tile = min(tile, rows)
    n = rows // tile
    ns, npar, nc, no = len(seqs), len(params), len(carry_shapes), len(outs)

    def body(*refs):
        seq_refs, refs = refs[:ns], refs[ns:]
        par_refs, refs = refs[:npar], refs[npar:]
        out_refs, refs = refs[:no], refs[no:]
        save_refs, refs = (refs[:nc], refs[nc:]) if save_carry else ((), refs)
        carry_refs = refs

        @pl.when(pl.program_id(0) == 0)
        def _():
            for cr in carry_refs:
                cr[...] = jnp.zeros_like(cr)

        carry = tuple(cr[...] for cr in carry_refs)
        for sr, cv in zip(save_refs, carry):
            sr[0] = cv
        new_carry, res = fn(carry, tuple(r[...].astype(F32) for r in seq_refs),
                            tuple(r[...].astype(F32) for r in par_refs))
        for r, v in zip(out_refs, res):
            r[...] = v.astype(r.dtype)
        for cr, v in zip(carry_refs, new_carry):
            cr[...] = v

    out_shape = [jax.ShapeDtypeStruct((rows, w), dt) for w, dt in outs]
    out_specs = [pl.BlockSpec((tile, w), lambda i: (i, 0)) for w, _ in outs]
    if save_carry:
        out_shape += [jax.ShapeDtypeStruct((n, *s), F32) for s in carry_shapes]
        out_specs += [pl.BlockSpec((1, *s), lambda i: (i, 0, 0)) for s in carry_shapes]
    res = pl.pallas_call(
        body, name=name, grid=(n,),
        in_specs=[pl.BlockSpec((tile, s.shape[1]), lambda i: (i, 0)) for s in seqs] + [_full_spec(p) for p in params],
        out_specs=out_specs, out_shape=out_shape,
        scratch_shapes=[pltpu.VMEM(s, F32) for s in carry_shapes],
        compiler_params=_cparams("arbitrary"),
    )(*seqs, *params)
    return res[:no], res[no:]


def scan_bwd(fn, name, tile, seqs, params, saved, douts, n_dseq, n_dpar):
    rows = seqs[0].shape[0]
    tile = min(tile, rows)
    n = rows // tile
    ns, npar, nc, no = len(seqs), len(params), len(saved), len(douts)

    def body(*refs):
        seq_refs, refs = refs[:ns], refs[ns:]
        par_refs, refs = refs[:npar], refs[npar:]
        save_refs, refs = refs[:nc], refs[nc:]
        dout_refs, refs = refs[:no], refs[no:]
        dseq_refs, refs = refs[:n_dseq], refs[n_dseq:]
        dpar_refs, refs = refs[:n_dpar], refs[n_dpar:]
        dcarry_refs = refs

        @pl.when(pl.program_id(0) == 0)
        def _():
            for r in (*dpar_refs, *dcarry_refs):
                r[...] = jnp.zeros_like(r)

        carry = tuple(r[0] for r in save_refs)
        seq = tuple(r[...].astype(F32) for r in seq_refs)
        par = tuple(r[...].astype(F32) for r in par_refs)

        def f(carry, dseq, dpar):
            return fn(carry, (*dseq, *seq[n_dseq:]), (*dpar, *par[n_dpar:]))

        _, vjp = jax.vjp(f, carry, seq[:n_dseq], par[:n_dpar])
        d_carry, d_seq, d_par = vjp((tuple(r[...] for r in dcarry_refs),
                                     tuple(r[...].astype(F32) for r in dout_refs)))
        for r, v in zip(dseq_refs, d_seq):
            r[...] = v
        for r, v in zip(dpar_refs, d_par):
            r[...] += v
        for r, v in zip(dcarry_refs, d_carry):
            r[...] = v

    rev = lambda i: (n - 1 - i, 0)
    res = pl.pallas_call(
        body, name=name, grid=(n,),
        in_specs=([pl.BlockSpec((tile, s.shape[1]), rev) for s in seqs] + [_full_spec(p) for p in params]
                  + [pl.BlockSpec((1, *s.shape[1:]), lambda i: (n - 1 - i, 0, 0)) for s in saved]
                  + [pl.BlockSpec((tile, d.shape[1]), rev) for d in douts]),
        out_specs=([pl.BlockSpec((tile, s.shape[1]), rev) for s in seqs[:n_dseq]]
                   + [_full_spec(p) for p in params[:n_dpar]]),
        out_shape=([jax.ShapeDtypeStruct((rows, s.shape[1]), F32) for s in seqs[:n_dseq]]
                   + [jax.ShapeDtypeStruct(p.shape, F32) for p in params[:n_dpar]]),
        scratch_shapes=[pltpu.VMEM(s.shape[1:], F32) for s in saved],
        compiler_params=_cparams("arbitrary"),
    )(*seqs, *params, *saved, *douts)
    return res[:n_dseq], res[n_dseq:]


def _tile_of(dim, pref):
    if dim <= pref:
        return dim
    best = max((t for t in range(128, pref + 1, 128) if dim % t == 0), default=None)
    if best is None or (best < 512 and dim <= 2304):
        return dim
    return best


def _row_tile(rows, pref):
    if rows <= pref:
        return rows
    return max(t for t in range(8, pref + 1, 8) if rows % t == 0)


def matmul(a, b, name, ta=False, tb=False, out_dtype=F32, add=None, bias=None, tm=512, tn=1024, tk=1024):
    m, k = (a.shape[1], a.shape[0]) if ta else a.shape
    n = b.shape[0] if tb else b.shape[1]
    assert k == (b.shape[1] if tb else b.shape[0])
    tm, tn, tk = _tile_of(m, tm), _tile_of(n, tn), _tile_of(k, tk)
    nk = k // tk
    dn = (((0 if ta else 1,), (1 if tb else 0,)), ((), ()))
    has_add, has_bias = add is not None, bias is not None

    def body(*refs):
        a_ref, b_ref, refs = refs[0], refs[1], refs[2:]
        add_ref, refs = (refs[0], refs[1:]) if has_add else (None, refs)
        bias_ref, refs = (refs[0], refs[1:]) if has_bias else (None, refs)
        o_ref, acc_ref = refs
        kk = pl.program_id(2)

        @pl.when(kk == 0)
        def _():
            acc_ref[...] = jnp.zeros_like(acc_ref)

        acc_ref[...] += lax.dot_general(a_ref[...].astype(BF16), b_ref[...].astype(BF16), dn,
                                        preferred_element_type=F32)

        @pl.when(kk == nk - 1)
        def _():
            r = acc_ref[...]
            if has_add:
                r = r + add_ref[...]
            if has_bias:
                r = r + bias_ref[...]
            o_ref[...] = r.astype(o_ref.dtype)

    a_spec = pl.BlockSpec((tk, tm), lambda i, j, l: (l, i)) if ta else pl.BlockSpec((tm, tk), lambda i, j, l: (i, l))
    b_spec = pl.BlockSpec((tn, tk), lambda i, j, l: (j, l)) if tb else pl.BlockSpec((tk, tn), lambda i, j, l: (l, j))
    o_spec = pl.BlockSpec((tm, tn), lambda i, j, l: (i, j))
    in_specs, args = [a_spec, b_spec], [a, b]
    if has_add:
        in_specs.append(o_spec)
        args.append(add)
    if has_bias:
        in_specs.append(pl.BlockSpec((1, tn), lambda i, j, l: (0, j)))
        args.append(bias)
    return pl.pallas_call(
        body, name=name, grid=(m // tm, n // tn, nk), in_specs=in_specs, out_specs=o_spec,
        out_shape=jax.ShapeDtypeStruct((m, n), out_dtype),
        scratch_shapes=[pltpu.VMEM((tm, tn), F32)],
        compiler_params=_cparams("parallel", "parallel", "arbitrary"),
    )(*args)


LIN_TILE = 512


def linscan_fwd(a, u, name):
    rows, w = a.shape
    tile = min(LIN_TILE, rows)

    def body(a_ref, u_ref, h_ref, hc):
        @pl.when(pl.program_id(0) == 0)
        def _():
            hc[...] = jnp.zeros_like(hc)

        def step(t, h):
            h = a_ref[pl.ds(t, 1), :] * h + u_ref[pl.ds(t, 1), :]
            h_ref[pl.ds(t, 1), :] = h
            return h

        hc[...] = lax.fori_loop(0, tile, step, hc[...], unroll=8)

    spec = pl.BlockSpec((tile, w), lambda i: (i, 0))
    return pl.pallas_call(
        body, name=name, grid=(rows // tile,), in_specs=[spec, spec], out_specs=spec,
        out_shape=jax.ShapeDtypeStruct((rows, w), F32), scratch_shapes=[pltpu.VMEM((1, w), F32)],
        compiler_params=_cparams("arbitrary"),
    )(a, u)


def linscan_bwd(a, hs, dh, name):
    rows, w = a.shape
    tile = min(LIN_TILE, rows)
    n = rows // tile
    per = tile // 8

    def body(a_ref, h_ref, hprev_ref, dh_ref, da_ref, du_ref, cc):
        i = pl.program_id(0)

        @pl.when(i == 0)
        def _():
            cc[...] = jnp.zeros_like(cc)

        def step(s, c):
            t = tile - 1 - s
            g = dh_ref[pl.ds(t, 1), :] + c
            du_ref[pl.ds(t, 1), :] = g
            da_ref[pl.ds(t, 1), :] = g * h_ref[pl.ds(t - 1, 1), :]
            return a_ref[pl.ds(t, 1), :] * g

        c = lax.fori_loop(0, tile - 1, step, cc[...], unroll=8)
        g = dh_ref[0:1, :] + c
        du_ref[0:1, :] = g
        da_ref[0:1, :] = g * jnp.where(i == n - 1, 0.0, hprev_ref[7:8, :])
        cc[...] = a_ref[0:1, :] * g

    rev = pl.BlockSpec((tile, w), lambda i: (n - 1 - i, 0))
    prev = pl.BlockSpec((8, w), lambda i: (jnp.maximum((n - 1 - i) * per - 1, 0), 0))
    return pl.pallas_call(
        body, name=name, grid=(n,), in_specs=[rev, rev, prev, rev], out_specs=[rev, rev],
        out_shape=[jax.ShapeDtypeStruct((rows, w), F32)] * 2, scratch_shapes=[pltpu.VMEM((1, w), F32)],
        compiler_params=_cparams("arbitrary"),
    )(a, hs, hs, dh)


def loss_head(x, target, w, name):
    rows, d = x.shape
    tile = min(512, rows)

    def body(x_ref, t_ref, w_ref, loss_ref, dx_ref, dw_ref):
        @pl.when(pl.program_id(0) == 0)
        def _():
            loss_ref[...] = jnp.zeros_like(loss_ref)
            dw_ref[...] = jnp.zeros_like(dw_ref)

        tv = t_ref[...]

        def f(xv, wv):
            e = _rms(xv, wv) - tv
            return 0.5 * jnp.sum(jnp.mean(e * e, axis=-1, keepdims=True), axis=0, keepdims=True)

        val, vjp = jax.vjp(f, x_ref[...], w_ref[...])
        dxv, dwv = vjp(jnp.ones((1, 1), F32))
        loss_ref[...] += jnp.broadcast_to(val, loss_ref.shape)
        dx_ref[...] = dxv
        dw_ref[...] += dwv

    spec = pl.BlockSpec((tile, d), lambda i: (i, 0))
    return pl.pallas_call(
        body, name=name, grid=(rows // tile,), in_specs=[spec, spec, _full_spec(w)],
        out_specs=[pl.BlockSpec((8, 128), lambda i: (0, 0)), spec, _full_spec(w)],
        out_shape=[jax.ShapeDtypeStruct((8, 128), F32), jax.ShapeDtypeStruct((rows, d), F32),
                   jax.ShapeDtypeStruct(w.shape, F32)],
        compiler_params=_cparams("arbitrary"),
    )(x, target, w)


def adamw(g, w, m, v, name):
    rows, cols = g.shape
    tile = _row_tile(rows, 256)

    def body(g_ref, w_ref, m_ref, v_ref, d_ref, nm_ref, nv_ref):
        gv = g_ref[...]
        nm = ADAM_B1 * m_ref[...] + (1.0 - ADAM_B1) * gv
        nv = ADAM_B2 * v_ref[...] + (1.0 - ADAM_B2) * (gv * gv)
        m_hat = nm / (1.0 - ADAM_B1 ** ADAM_STEP)
        v_hat = nv / (1.0 - ADAM_B2 ** ADAM_STEP)
        d_ref[...] = -ADAM_LR * (m_hat / (jnp.sqrt(v_hat) + ADAM_EPS) + ADAM_WD * w_ref[...])
        nm_ref[...] = nm
        nv_ref[...] = nv

    spec = pl.BlockSpec((tile, cols), lambda i: (i, 0))
    return pl.pallas_call(
        body, name=name, grid=(rows // tile,), in_specs=[spec] * 4, out_specs=[spec] * 3,
        out_shape=[jax.ShapeDtypeStruct((rows, cols), F32)] * 3,
        compiler_params=_cparams("parallel"),
    )(g, w, m, v)


def add4(parts, name):
    _, rows, cols = parts.shape
    tile = _row_tile(rows, 512)

    def body(p_ref, o_ref):
        o_ref[...] = ((p_ref[0] + p_ref[1]) + p_ref[2]) + p_ref[3]

    return pl.pallas_call(
        body, name=name, grid=(rows // tile,),
        in_specs=[pl.BlockSpec((4, tile, cols), lambda i: (0, i, 0))],
        out_specs=pl.BlockSpec((tile, cols), lambda i: (i, 0)),
        out_shape=jax.ShapeDtypeStruct((rows, cols), F32),
        compiler_params=_cparams("parallel"),
    )(parts)


def add2(p, q, name):
    rows, cols = p.shape
    tile = _row_tile(rows, 512)

    def body(p_ref, q_ref, o_ref):
        o_ref[...] = p_ref[...] + q_ref[...]

    spec = pl.BlockSpec((tile, cols), lambda i: (i, 0))
    return pl.pallas_call(
        body, name=name, grid=(rows // tile,), in_specs=[spec, spec], out_specs=spec,
        out_shape=jax.ShapeDtypeStruct((rows, cols), F32), compiler_params=_cparams("parallel"),
    )(p, q)


def add8(parts, name):
    _, rows, cols = parts.shape

    def body(p_ref, o_ref):
        acc = p_ref[0]
        for k in range(1, 8):
            acc = acc + p_ref[k]
        o_ref[...] = acc

    return pl.pallas_call(
        body, name=name, in_specs=[pl.BlockSpec(memory_space=pltpu.VMEM)],
        out_specs=pl.BlockSpec(memory_space=pltpu.VMEM),
        out_shape=jax.ShapeDtypeStruct((rows, cols), F32),
    )(parts)


def _place():
    return lax.axis_index("x"), lax.axis_index("y"), lax.axis_index("c")


def _other_chips(x, y):
    return [(1 - x, y), (x, 1 - y), (1 - x, 1 - y)]


_ANY = pl.BlockSpec(memory_space=pl.ANY)


def gather4(shard, name):
    rows, cols = shard.shape

    def body(s_ref, out_ref, send_sems, recv_sems, local_sem):
        x, y, c = _place()
        me = 2 * x + y
        mine = pltpu.make_async_copy(s_ref, out_ref.at[me], local_sem)
        mine.start()

        def copy(j, chip, block):
            return pltpu.make_async_remote_copy(
                src_ref=s_ref, dst_ref=out_ref.at[block], send_sem=send_sems.at[j], recv_sem=recv_sems.at[j],
                device_id=(chip[0], chip[1], c), device_id_type=MESH)

        chips = _other_chips(x, y)
        sends = [copy(j, chip, me) for j, chip in enumerate(chips)]
        for cp in sends:
            cp.start()
        for j, chip in enumerate(chips):
            copy(j, chip, 2 * chip[0] + chip[1]).wait_recv()
        for cp in sends:
            cp.wait_send()
        mine.wait()

    return pl.pallas_call(
        body, name=name, in_specs=[_ANY], out_specs=_ANY,
        out_shape=jax.ShapeDtypeStruct((4, rows, cols), shard.dtype),
        scratch_shapes=[pltpu.SemaphoreType.DMA((3,)), pltpu.SemaphoreType.DMA((3,)), pltpu.SemaphoreType.DMA],
        compiler_params=pltpu.CompilerParams(has_side_effects=True),
    )(shard)


def exchange4(parts, name):
    _, rows, cols = parts.shape

    def body(p_ref, out_ref, send_sems, recv_sems, local_sem):
        x, y, c = _place()
        me = 2 * x + y
        mine = pltpu.make_async_copy(p_ref.at[me], out_ref.at[me], local_sem)
        mine.start()

        def copy(j, chip, src_block, dst_block):
            return pltpu.make_async_remote_copy(
                src_ref=p_ref.at[src_block], dst_ref=out_ref.at[dst_block], send_sem=send_sems.at[j],
                recv_sem=recv_sems.at[j], device_id=(chip[0], chip[1], c), device_id_type=MESH)

        chips = _other_chips(x, y)
        sends = [copy(j, chip, 2 * chip[0] + chip[1], me) for j, chip in enumerate(chips)]
        for cp in sends:
            cp.start()
        for j, chip in enumerate(chips):
            copy(j, chip, me, 2 * chip[0] + chip[1]).wait_recv()
        for cp in sends:
            cp.wait_send()
        mine.wait()

    return pl.pallas_call(
        body, name=name, in_specs=[_ANY], out_specs=_ANY,
        out_shape=jax.ShapeDtypeStruct(parts.shape, parts.dtype),
        scratch_shapes=[pltpu.SemaphoreType.DMA((3,)), pltpu.SemaphoreType.DMA((3,)), pltpu.SemaphoreType.DMA],
        compiler_params=pltpu.CompilerParams(has_side_effects=True),
    )(parts)


def swap_sibling(p, name):
    def body(p_ref, out_ref, send_sem, recv_sem):
        x, y, c = _place()
        cp = pltpu.make_async_remote_copy(src_ref=p_ref, dst_ref=out_ref, send_sem=send_sem, recv_sem=recv_sem,
                                          device_id=(x, y, 1 - c), device_id_type=MESH)
        cp.start()
        cp.wait()

    return pl.pallas_call(
        body, name=name, in_specs=[_ANY], out_specs=_ANY,
        out_shape=jax.ShapeDtypeStruct(p.shape, p.dtype),
        scratch_shapes=[pltpu.SemaphoreType.DMA, pltpu.SemaphoreType.DMA],
        compiler_params=pltpu.CompilerParams(has_side_effects=True),
    )(p)


def allgather8(block, name):
    m_per, n = block.shape

    def body(x_ref, out_ref, send_sems, recv_sems, local_sem):
        x, y, c = _place()
        me, sibling = (x, y, c), (x, y, 1 - c)
        chips = _other_chips(x, y)

        def rows(px, py, pc):
            return out_ref.at[pl.ds((4 * px + 2 * py + pc) * m_per, m_per), :]

        def copy(k, blk, to, src=None):
            return pltpu.make_async_remote_copy(
                src_ref=rows(*blk) if src is None else src, dst_ref=rows(*blk), send_sem=send_sems.at[k],
                recv_sem=recv_sems.at[k], device_id=to, device_id_type=MESH)

        mine = pltpu.make_async_copy(x_ref, rows(*me), local_sem)
        mine.start()
        first = [copy(0, me, sibling, src=x_ref)]
        first += [copy(1 + j, me, (*chip, c), src=x_ref) for j, chip in enumerate(chips)]
        for cp in first:
            cp.start()
        passed = [copy(4 + j, (*chip, c), sibling) for j, chip in enumerate(chips)]
        for j, chip in enumerate(chips):
            copy(1 + j, (*chip, c), me).wait_recv()
            passed[j].start()
        copy(0, sibling, me).wait_recv()
        for j, chip in enumerate(chips):
            copy(4 + j, (*chip, 1 - c), me).wait_recv()
        for cp in first + passed:
            cp.wait_send()
        mine.wait()

    return pl.pallas_call(
        body, name=name, in_specs=[pl.BlockSpec(memory_space=pltpu.VMEM)],
        out_specs=pl.BlockSpec(memory_space=pltpu.VMEM),
        out_shape=jax.ShapeDtypeStruct((8 * m_per, n), block.dtype),
        scratch_shapes=[pltpu.SemaphoreType.DMA((7,)), pltpu.SemaphoreType.DMA((7,)), pltpu.SemaphoreType.DMA],
    )(block)


BIG = (("ada_w", 2), ("w_in", 2), ("w_branch", 3), ("w_out", 1), ("w_up", 2), ("w_down", 1))
CONV = ("gdn_conv_w", "ssd_conv_w", "lru_conv_w")
SMALL = ("ada_b", "norm_mix", "gdn_a_log", "gdn_dt_bias", "gdn_norm", "ssd_conv_b", "ssd_a_log", "ssd_dt_bias",
         "ssd_d", "ssd_norm", "lru_conv_b", "lru_w_a", "lru_b_a", "lru_w_x", "lru_b_x", "lru_lambda", "norm_mlp",
         "final_norm")
WEIGHTS = ("ada_w", "ada_b", "norm_mix", "w_in", "gdn_conv_w", "gdn_a_log", "gdn_dt_bias", "gdn_norm", "ssd_conv_w",
           "ssd_conv_b", "ssd_a_log", "ssd_dt_bias", "ssd_d", "ssd_norm", "lru_conv_w", "lru_conv_b", "lru_w_a",
           "lru_b_a", "lru_w_x", "lru_b_x", "lru_lambda", "w_branch", "w_out", "norm_mlp", "w_up", "w_down",
           "final_norm")
PACK_COLS = 1024


def _pack(arrays, dtype):
    flat = jnp.concatenate([a.reshape(-1).astype(dtype) for a in arrays])
    pad = -flat.shape[0] % (8 * PACK_COLS)
    if pad:
        flat = jnp.concatenate([flat, jnp.zeros((pad,), dtype)])
    return flat.reshape(-1, PACK_COLS)


def _unpack(pack, shapes):
    flat = pack.reshape(-1)
    out, o = [], 0
    for s in shapes:
        n = math.prod(s)
        out.append(flat[o:o + n].reshape(s))
        o += n
    return out


def _split_w_in(w):
    pad = jnp.zeros((w.shape[0], 120), w.dtype)
    gdn = jnp.concatenate([w[:, 0:2056], pad], axis=1)
    ssd = jnp.concatenate([w[:, 2056:2568], w[:, 3080:3592], w[:, 2568:3080], w[:, 3592:3600], pad], axis=1)
    return gdn, ssd, w[:, 3600:4112], w[:, 4112:4624], w[:, 4624:7696]


def _join_w_in(gdn, ssd, lx, lg, gate):
    return jnp.concatenate([gdn[:, 0:2056], ssd[:, 0:512], ssd[:, 1024:1536], ssd[:, 512:1024], ssd[:, 1536:1544],
                            lx, lg, gate], axis=1)


def _lanes(v, at, width=128):
    return jnp.zeros((1, width), F32).at[0, at:at + v.shape[0]].set(v)


def _block_diag(w):
    return (jnp.eye(8, dtype=w.dtype)[:, None, :, None] * w[:, :, None, :]).reshape(512, 512)


def _diag_blocks(w):
    return jnp.stack([w[n * 64:(n + 1) * 64, n * 64:(n + 1) * 64] for n in range(8)])


TOK_TILE = 512
WIDE_TILE = 256


def _layer_params(p, l):
    row = lambda v: v.reshape(1, -1)
    gdn = (p["gdn_conv_w"][l], _lanes(p["gdn_a_log"][l], 4), _lanes(p["gdn_dt_bias"][l], 4), row(p["gdn_norm"][l]))
    ssd = (p["ssd_conv_w"][l], row(p["ssd_conv_b"][l]), _lanes(p["ssd_a_log"][l], 0), _lanes(p["ssd_dt_bias"][l], 0),
           row(jnp.repeat(p["ssd_d"][l], 64)), row(p["ssd_norm"][l]))
    lru = (p["lru_conv_w"][l], row(p["lru_conv_b"][l]), _block_diag(p["lru_w_a"][l]), row(p["lru_b_a"][l]),
           _block_diag(p["lru_w_x"][l]), row(p["lru_b_x"][l]), row(p["lru_lambda"][l]))
    return dict(gdn=gdn, ssd=ssd, lru=lru, w_in=_split_w_in(p["w_in"][l]),
                wb=tuple(p["w_branch"][l, r] for r in range(3)), w_out=p["w_out"][l], w_up=p["w_up"][l],
                w_down=p["w_down"][l], ada_w=p["ada_w"][l], ada_b=row(p["ada_b"][l]),
                norm_mix=row(p["norm_mix"][l]), norm_mlp=row(p["norm_mlp"][l]))


def _layer_fwd(x, silu_c, lp, l):
    nm = lambda s: f"l{l}_{s}"
    mod = matmul(silu_c, lp["ada_w"], nm("mod"), bias=lp["ada_b"])
    sh1, sc1, gt1, sh2, sc2, gt2 = (mod[0:1, k * D_MODEL:(k + 1) * D_MODEL] for k in range(N_MOD))
    (h,), _ = scan_fwd(norm1_fn, nm("norm1"), TOK_TILE, [x], [lp["norm_mix"], sc1, sh1], [], [(D_MODEL, BF16)])
    w_gdn, w_ssd, w_lx, w_lg, w_gate = lp["w_in"]
    p_gdn = matmul(h, w_gdn, nm("in_gdn"))
    p_ssd = matmul(h, w_ssd, nm("in_ssd"))
    p_lx = matmul(h, w_lx, nm("in_lx"))
    p_lg = matmul(h, w_lg, nm("in_lg"))
    p_gate = matmul(h, w_gate, nm("in_gate"))
    (ya,), sv_gdn = scan_fwd(gdn_fn, nm("gdn"), CHUNK, [p_gdn], lp["gdn"], [(128, 128)] * 4 + [(8, 1536)],
                             [(512, F32)], save_carry=True)
    (yb,), sv_ssd = scan_fwd(ssd_fn, nm("ssd"), CHUNK, [p_ssd], lp["ssd"], [(128, 128)] * 4 + [(8, 1024)],
                             [(512, F32)], save_carry=True)
    (a, u), sv_lru = scan_fwd(lru_in_fn, nm("lru_in"), TOK_TILE, [p_lx], lp["lru"], [(8, 512)],
                              [(512, F32), (512, F32)], save_carry=True)
    hs = linscan_fwd(a, u, nm("lru_scan"))
    (yc,), _ = scan_fwd(lru_out_fn, nm("lru_out"), TOK_TILE, [hs, p_lg], [], [], [(512, F32)])
    (merged,), _ = scan_fwd(merge_fn, nm("merge"), WIDE_TILE, [ya, yb, yc, p_gate], lp["wb"], [], [(D_MODEL, BF16)])
    mix = matmul(merged, lp["w_out"], nm("out"))
    (h2, x1), _ = scan_fwd(resid_norm_fn, nm("norm2"), TOK_TILE, [x, mix], [gt1, lp["norm_mlp"], sc2, sh2], [],
                           [(D_MODEL, BF16), (D_MODEL, F32)])
    up = matmul(h2, lp["w_up"], nm("up"))
    (act,), _ = scan_fwd(act_fn, nm("act"), WIDE_TILE, [up], [], [], [(D_FF, BF16)])
    dn = matmul(act, lp["w_down"], nm("down"))
    (x2,), _ = scan_fwd(resid_fn, nm("resid"), TOK_TILE, [x1, dn], [gt2], [], [(D_MODEL, F32)])
    saved = dict(x=x, h=h, p_gdn=p_gdn, p_ssd=p_ssd, p_lx=p_lx, p_lg=p_lg, p_gate=p_gate, sv_gdn=sv_gdn,
                 sv_ssd=sv_ssd, sv_lru=sv_lru, a=a, hs=hs, ya=ya, yb=yb, yc=yc, merged=merged, mix=mix, x1=x1,
                 h2=h2, up=up, act=act, dn=dn, mod=(sh1, sc1, gt1, sh2, sc2, gt2))
    return x2, saved


def _layer_bwd(d_x2, silu_c, lp, sv, l):
    nm = lambda s: f"l{l}_b_{s}"
    sh1, sc1, gt1, sh2, sc2, gt2 = sv["mod"]
    (d_x1, d_dn), (d_gt2,) = scan_bwd(resid_fn, nm("resid"), TOK_TILE, [sv["x1"], sv["dn"]], [gt2], [], [d_x2], 2, 1)
    d_act = matmul(d_dn, lp["w_down"], nm("down_x"), tb=True)
    g_w_down = matmul(sv["act"], d_dn, nm("down_w"), ta=True)
    (d_up,), _ = scan_bwd(act_fn, nm("act"), WIDE_TILE, [sv["up"]], [], [], [d_act], 1, 0)
    d_h2 = matmul(d_up, lp["w_up"], nm("up_x"), tb=True)
    g_w_up = matmul(sv["h2"], d_up, nm("up_w"), ta=True)
    (d_x, d_mix), (d_gt1, g_norm_mlp, d_sc2, d_sh2) = scan_bwd(
        resid_norm_fn, nm("norm2"), TOK_TILE, [sv["x"], sv["mix"]], [gt1, lp["norm_mlp"], sc2, sh2], [],
        [d_h2, d_x1], 2, 4)
    d_merged = matmul(d_mix, lp["w_out"], nm("out_x"), tb=True)
    g_w_out = matmul(sv["merged"], d_mix, nm("out_w"), ta=True)
    (d_ya, d_yb, d_yc, d_pgate), g_wb = scan_bwd(
        merge_fn, nm("merge"), WIDE_TILE, [sv["ya"], sv["yb"], sv["yc"], sv["p_gate"]], lp["wb"], [], [d_merged], 4, 3)
    (d_hs, d_plg), _ = scan_bwd(lru_out_fn, nm("lru_out"), TOK_TILE, [sv["hs"], sv["p_lg"]], [], [], [d_yc], 2, 0)
    d_a, d_u = linscan_bwd(sv["a"], sv["hs"], d_hs, nm("lru_scan"))
    (d_plx,), g_lru = scan_bwd(lru_in_fn, nm("lru_in"), TOK_TILE, [sv["p_lx"]], lp["lru"], sv["sv_lru"],
                               [d_a, d_u], 1, 7)
    (d_pssd,), g_ssd = scan_bwd(ssd_fn, nm("ssd"), CHUNK, [sv["p_ssd"]], lp["ssd"], sv["sv_ssd"], [d_yb], 1, 6)
    (d_pgdn,), g_gdn = scan_bwd(gdn_fn, nm("gdn"), CHUNK, [sv["p_gdn"]], lp["gdn"], sv["sv_gdn"], [d_ya], 1, 4)
    d_h = None
    g_w_in = []
    for tag, dp, w in zip(("gdn", "ssd", "lx", "lg", "gate"), (d_pgdn, d_pssd, d_plx, d_plg, d_pgate), lp["w_in"]):
        d_h = matmul(dp, w, nm("in_x_" + tag), tb=True, add=d_h)
        g_w_in.append(matmul(sv["h"], dp, nm("in_w_" + tag), ta=True))
    (d_x0,), (g_norm_mix, d_sc1, d_sh1) = scan_bwd(norm1_fn, nm("norm1"), TOK_TILE, [sv["x"]],
                                                   [lp["norm_mix"], sc1, sh1], [], [d_h, d_x], 1, 3)
    d_mod = jnp.concatenate([d_sh1, d_sc1, d_gt1, d_sh2, d_sc2, d_gt2], axis=1)
    d_mod8 = jnp.concatenate([d_mod, jnp.zeros((7, d_mod.shape[1]), F32)], axis=0)
    g_ada_w = matmul(silu_c, d_mod8, nm("mod_w"), ta=True)
    flat = lambda v: v.reshape(-1)
    grads = dict(
        ada_w=g_ada_w, ada_b=flat(d_mod), norm_mix=flat(g_norm_mix), w_in=_join_w_in(*g_w_in),
        gdn_conv_w=g_gdn[0], gdn_a_log=g_gdn[1][0, 4:8], gdn_dt_bias=g_gdn[2][0, 4:8], gdn_norm=flat(g_gdn[3]),
        ssd_conv_w=g_ssd[0], ssd_conv_b=flat(g_ssd[1]), ssd_a_log=g_ssd[2][0, 0:8], ssd_dt_bias=g_ssd[3][0, 0:8],
        ssd_d=g_ssd[4].reshape(8, 64).sum(axis=1), ssd_norm=flat(g_ssd[5]),
        lru_conv_w=g_lru[0], lru_conv_b=flat(g_lru[1]), lru_w_a=_diag_blocks(g_lru[2]), lru_b_a=flat(g_lru[3]),
        lru_w_x=_diag_blocks(g_lru[4]), lru_b_x=flat(g_lru[5]), lru_lambda=flat(g_lru[6]),
        w_branch=jnp.stack(g_wb), w_out=g_w_out, norm_mlp=flat(g_norm_mlp), w_up=g_w_up, w_down=g_w_down)
    return d_x0, grads


def local_step(x, c, target, p):
    c8 = jnp.concatenate([c, jnp.zeros((7, c.shape[1]), F32)], axis=0)
    (silu_c,), _ = scan_fwd(silu_fn, "silu_c", 8, [c8], [], [], [(D_MODEL, F32)])
    lps = [_layer_params(p, l) for l in range(DEPTH)]
    saved = []
    for l in range(DEPTH):
        x, sv = _layer_fwd(x, silu_c, lps[l], l)
        saved.append(sv)
    loss, d_x, g_final = loss_head(x, target, p["final_norm"].reshape(1, -1), "loss_head")
    layer_grads = [None] * DEPTH
    for l in reversed(range(DEPTH)):
        d_x, layer_grads[l] = _layer_bwd(d_x, silu_c, lps[l], saved[l], l)
    grads = {k: jnp.stack([layer_grads[l][k] for l in range(DEPTH)]) for k in layer_grads[0]}
    grads["final_norm"] = g_final.reshape(-1)
    return loss, d_x, grads


def _shard(a, axis, k):
    n = a.shape[axis] // 4
    return lax.slice_in_dim(a, k * n, (k + 1) * n, axis=axis)


def _adam_nd(g, w, m, v, name):
    two = lambda a: a.reshape(-1, a.shape[-1])
    return tuple(r.reshape(w.shape) for r in adamw(two(g), two(w), two(m), two(v), name))


def _step(w, m, v, x, c, target):
    chip = 2 * lax.axis_index("x") + lax.axis_index("y")
    big_shapes = [w[n].shape for n, _ in BIG]
    conv_shapes = [w[n].shape for n in CONV]
    small_shapes = [w[n].shape for n in SMALL]

    big_all = gather4(_pack([w[n] for n, _ in BIG], BF16), "gather_big")
    conv_all = allgather8(_pack([w[n] for n in CONV], F32), "gather_conv").reshape(8, -1, PACK_COLS)
    big_parts = [_unpack(big_all[k], big_shapes) for k in range(4)]
    conv_parts = [_unpack(conv_all[2 * k], conv_shapes) for k in range(4)]
    p = {n: w[n] for n in SMALL}
    for i, (n, axis) in enumerate(BIG):
        p[n] = jnp.concatenate([big_parts[k][i] for k in range(4)], axis=axis)
    for i, n in enumerate(CONV):
        p[n] = jnp.concatenate([conv_parts[k][i] for k in range(4)], axis=2)

    loss_blk, grad_x, g = local_step(x[0], c, target[0], p)

    parts = jnp.stack([_pack([_shard(g[n], axis, k) for n, axis in BIG], F32) for k in range(4)])
    core_sum = add4(exchange4(parts, "reduce_big"), "reduce_big_sum")
    big_g = _unpack(add2(core_sum, swap_sibling(core_sum, "reduce_big_swap"), "reduce_big_total"), big_shapes)

    small_pack = _pack([loss_blk[0, 0:1]] + [g[n] for n in SMALL] + [g[n] for n in CONV], F32)
    small_all = allgather8(small_pack, "gather_small").reshape(8, -1, PACK_COLS)
    total = _unpack(add8(small_all, "reduce_small"), [(1,)] + small_shapes + [g[n].shape for n in CONV])
    loss = total[0][0]
    small_g = dict(zip(SMALL, total[1:1 + len(SMALL)]))
    conv_g = {n: lax.dynamic_slice_in_dim(t, chip * w[n].shape[2], w[n].shape[2], axis=2)
              for n, t in zip(CONV, total[1 + len(SMALL):])}

    grad, delta, new_m, new_v = {}, {}, {}, {}
    for (n, _), gn in zip(BIG, big_g):
        grad[n] = gn
        delta[n], new_m[n], new_v[n] = _adam_nd(gn, w[n], m[n], v[n], "adam_" + n)
    for names, gs, shapes, tag in ((SMALL, small_g, small_shapes, "small"), (CONV, conv_g, conv_shapes, "conv")):
        pk = lambda d: _pack([d[n] for n in names], F32)
        res = adamw(pk(gs), pk(w), pk(m), pk(v), "adam_" + tag)
        for out, r in zip((delta, new_m, new_v), res):
            out.update(zip(names, _unpack(r, shapes)))
        grad.update({n: gs[n] for n in names})
    outs = [loss, grad_x[None]]
    for d in (grad, delta, new_m, new_v):
        outs += [d[n] for n in WEIGHTS]
    return tuple(outs)


def kernel(x, c, ada_w, ada_b, norm_mix, w_in, gdn_conv_w, gdn_a_log, gdn_dt_bias, gdn_norm, ssd_conv_w, ssd_conv_b, ssd_a_log, ssd_dt_bias, ssd_d, ssd_norm, lru_conv_w, lru_conv_b, lru_w_a, lru_b_a, lru_w_x, lru_b_x, lru_lambda, w_branch, w_out, norm_mlp, w_up, w_down, final_norm, loss_target, m_ada_w, m_ada_b, m_norm_mix, m_w_in, m_gdn_conv_w, m_gdn_a_log, m_gdn_dt_bias, m_gdn_norm, m_ssd_conv_w, m_ssd_conv_b, m_ssd_a_log, m_ssd_dt_bias, m_ssd_d, m_ssd_norm, m_lru_conv_w, m_lru_conv_b, m_lru_w_a, m_lru_b_a, m_lru_w_x, m_lru_b_x, m_lru_lambda, m_w_branch, m_w_out, m_norm_mlp, m_w_up, m_w_down, m_final_norm, v_ada_w, v_ada_b, v_norm_mix, v_w_in, v_gdn_conv_w, v_gdn_a_log, v_gdn_dt_bias, v_gdn_norm, v_ssd_conv_w, v_ssd_conv_b, v_ssd_a_log, v_ssd_dt_bias, v_ssd_d, v_ssd_norm, v_lru_conv_w, v_lru_conv_b, v_lru_w_a, v_lru_b_a, v_lru_w_x, v_lru_b_x, v_lru_lambda, v_w_branch, v_w_out, v_norm_mlp, v_w_up, v_w_down, v_final_norm):
    given = dict(locals())
    w = {n: given[n] for n in WEIGHTS}
    m = {n: given["m_" + n] for n in WEIGHTS}
    v = {n: given["v_" + n] for n in WEIGHTS}
    return _step(w, m, v, x, c, loss_target)
```

```python
import functools
import math

import jax
import jax.numpy as jnp
from jax import lax
from jax.experimental import pallas as pl
from jax.experimental.pallas import tpu as pltpu

F32 = jnp.float32
BF16 = jnp.bfloat16

D_MODEL = 1024
DEPTH = 2
RMS_EPS = 1e-6
CHUNK = 128
GDN_HEADS = 4
SSD_HEADS = 8
LRU_C = 8.0
D_FF = 4096
N_MOD = 6
W_GDN = 2176
W_SSD = 1664
W_LRU = 512
W_GATE = 3072
ADAM_LR = 0.001
ADAM_B1 = 0.9
ADAM_B2 = 0.999
ADAM_EPS = 1e-08
ADAM_WD = 0.01
ADAM_STEP = 10
VMEM_LIMIT = 56 * 1024 * 1024
MESH = pl.DeviceIdType.MESH


def _dot(a, b, ta, tb):
    dn = (((0 if ta else 1,), (1 if tb else 0,)), ((), ()))
    return lax.dot_general(a.astype(BF16), b.astype(BF16), dn, preferred_element_type=F32)


@functools.partial(jax.custom_vjp, nondiff_argnums=(2, 3))
def mm(a, b, ta, tb):
    return _dot(a, b, ta, tb)


def _mm_fwd(a, b, ta, tb):
    return _dot(a, b, ta, tb), (a, b)


def _mm_bwd(ta, tb, res, g):
    a, b = res
    if not ta and not tb:
        return mm(g, b, False, True), mm(a, g, True, False)
    if not ta and tb:
        return mm(g, b, False, False), mm(g, a, True, False)
    assert ta and not tb
    return mm(b, g, False, True), mm(a, g, False, False)


mm.defvjp(_mm_fwd, _mm_bwd)


def _tri_apply(x, upper):
    t = x.shape[0]
    r = lax.broadcasted_iota(jnp.int32, (t, t), 0)
    c = lax.broadcasted_iota(jnp.int32, (t, t), 1)
    tri = jnp.where((r <= c) if upper else (r >= c), 1.0, 0.0).astype(BF16)
    x1 = x.astype(BF16)
    r1 = x - x1.astype(F32)
    x2 = r1.astype(BF16)
    x3 = (r1 - x2.astype(F32)).astype(BF16)
    d = lambda p: jnp.dot(tri, p, preferred_element_type=F32)
    return (d(x1) + d(x2)) + d(x3)


@jax.custom_vjp
def cumsum_rows(x):
    return _tri_apply(x, False)


cumsum_rows.defvjp(lambda x: (_tri_apply(x, False), None), lambda _, g: (_tri_apply(g, True),))


def _dot_split(a, b):
    a1, b1 = a.astype(BF16), b.astype(BF16)
    a2, b2 = (a - a1.astype(F32)).astype(BF16), (b - b1.astype(F32)).astype(BF16)
    d = lambda p, q: jnp.dot(p, q, preferred_element_type=F32)
    return d(a1, b1) + (d(a1, b2) + d(a2, b1))


def _neumann(m):
    t = m.shape[0]
    x = -m
    q = _dot(m, m, False, False)
    n = 2
    while True:
        x = x + q + _dot(x, q, False, False)
        n *= 2
        if n >= t:
            break
        q = _dot(q, q, False, False)
    r = -(x + m + _dot_split(m, x))
    return x + r + _dot(x, r, False, False)


@jax.custom_vjp
def tri_solve(m, rhs):
    return rhs + _dot(_neumann(m), rhs, False, False)


def _tri_solve_fwd(m, rhs):
    x = _neumann(m)
    sol = rhs + _dot(x, rhs, False, False)
    return sol, (x, sol)


def _tri_solve_bwd(res, g):
    x, sol = res
    d_rhs = g + _dot(x, g, True, False)
    return -_dot(d_rhs, sol, False, True), d_rhs


tri_solve.defvjp(_tri_solve_fwd, _tri_solve_bwd)


@functools.partial(jax.custom_vjp, nondiff_argnums=(1,))
def split_cols(x, sizes):
    out, o = [], 0
    for s in sizes:
        out.append(x[:, o:o + s])
        o += s
    return tuple(out)


split_cols.defvjp(lambda x, sizes: (split_cols(x, sizes), None),
                  lambda sizes, _, g: (jnp.concatenate(list(g), axis=1),))


@functools.partial(jax.custom_vjp, nondiff_argnums=(1,))
def _last_rows(x, t):
    return x[t - 8:, :]


_last_rows.defvjp(lambda x, t: (_last_rows(x, t), None),
                  lambda t, _, g: (jnp.concatenate([jnp.zeros((t - 8, g.shape[1]), g.dtype), g], axis=0),))


def last8(x):
    return _last_rows(x, x.shape[0])


def _shifted(xp, d, t):
    return (pltpu.roll(xp, d, 0) if d else xp)[8:8 + t, :]


@jax.custom_vjp
def conv4(x, tail, w):
    t = x.shape[0]
    xp = jnp.concatenate([tail, x], axis=0)
    return sum(_shifted(xp, 3 - k, t) * w[k:k + 1, :] for k in range(4))


def _conv4_fwd(x, tail, w):
    return conv4(x, tail, w), (x, tail, w)


def _conv4_bwd(res, g):
    x, tail, w = res
    t = x.shape[0]
    xp = jnp.concatenate([tail, x], axis=0)
    zero8 = jnp.zeros((8, g.shape[1]), g.dtype)
    d_xp = jnp.zeros_like(xp)
    d_w = []
    for k in range(4):
        gk = jnp.concatenate([zero8, g * w[k:k + 1, :]], axis=0)
        d_xp = d_xp + (pltpu.roll(gk, t + 8 - (3 - k), 0) if k < 3 else gk)
        d_w.append(jnp.sum(g * _shifted(xp, 3 - k, t), axis=0, keepdims=True))
    return d_xp[8:, :], d_xp[:8, :], jnp.concatenate(d_w, axis=0)


conv4.defvjp(_conv4_fwd, _conv4_bwd)


def _sigmoid(x):
    return 0.5 * (jnp.tanh(0.5 * x) + 1.0)


def _silu(x):
    return x * _sigmoid(x)


def _softplus(x):
    ax = jnp.where(x > 0, x, -x)
    return jnp.where(x > 0, x, 0.0) + jnp.log(1.0 + jnp.exp(-ax))


def _gelu(x):
    return 0.5 * x * (1.0 + jnp.tanh(math.sqrt(2.0 / math.pi) * (x + 0.044715 * (x * x * x))))


def _expm1(x):
    series = x * (1.0 + x * (0.5 + x * (1.0 / 6.0 + x * (1.0 / 24.0))))
    return jnp.where(jnp.abs(x) < 0.03, series, jnp.exp(x) - 1.0)


def _rms(x, w):
    return x * lax.rsqrt(jnp.mean(x * x, axis=-1, keepdims=True) + RMS_EPS) * w


def _lane_pick(x, j):
    lane = lax.broadcasted_iota(jnp.int32, (1, x.shape[1]), 1)
    return jnp.sum(jnp.where(lane == j, x, 0.0), axis=1, keepdims=True)


def _row_pick(x, j):
    row = lax.broadcasted_iota(jnp.int32, (x.shape[0], 1), 0)
    return jnp.sum(jnp.where(row == j, x, 0.0), axis=0, keepdims=True)


def gdn_fn(carry, seq, params):
    *states, tail = carry
    (tile,) = seq
    conv_w, alog_row, dtb_row, norm_w = params
    t = tile.shape[0]
    qkv_raw, z, sm = split_cols(tile, (1536, 512, 128))
    qkv = _silu(conv4(qkv_raw, tail, conv_w))
    parts = split_cols(qkv, (128,) * 12)
    zs = split_cols(z, (128,) * 4)
    lane = lax.broadcasted_iota(jnp.int32, (1, 128), 1)
    beta_all = _sigmoid(sm)
    g_all = jnp.where((lane >= 4) & (lane < 8), -jnp.exp(alog_row) * _softplus(sm + dtb_row), 0.0)
    gc_all = cumsum_rows(g_all)
    gr_all = gc_all.T
    gl_all = _row_pick(gc_all, t - 1)
    r = lax.broadcasted_iota(jnp.int32, (t, t), 0)
    c = lax.broadcasted_iota(jnp.int32, (t, t), 1)
    outs, new_states = [], []
    for h in range(GDN_HEADS):
        q, k, v = parts[h], parts[4 + h], parts[8 + h]
        qn = q * lax.rsqrt(jnp.sum(q * q, axis=-1, keepdims=True) + RMS_EPS) * (128.0 ** -0.5)
        kn = k * lax.rsqrt(jnp.sum(k * k, axis=-1, keepdims=True) + RMS_EPS)
        beta = _lane_pick(beta_all, h)
        gc = _lane_pick(gc_all, 4 + h)
        gr = _row_pick(gr_all, 4 + h)
        gl = _lane_pick(gl_all, 4 + h)
        decay = jnp.exp(jnp.where(r >= c, gc - gr, -1e30))
        m = jnp.where(r > c, beta * mm(kn, kn, False, True) * decay, 0.0)
        eg = jnp.exp(gc)
        sol = tri_solve(m, jnp.concatenate([beta * v, (beta * eg) * kn], axis=1))
        u, w = split_cols(sol, (128, 128))
        qk = mm(qn, kn, False, True) * decay
        v_new = u - mm(w, states[h], False, False)
        o = mm(qn * eg, states[h], False, False) + mm(qk, v_new, False, False)
        new_states.append(states[h] * jnp.exp(gl) + mm(kn * jnp.exp(gl - gc), v_new, True, False))
        outs.append(_rms(o, norm_w) * _silu(zs[h]))
    return (*new_states, last8(qkv_raw)), (jnp.concatenate(outs, axis=1),)


def ssd_fn(carry, seq, params):
    *states, tail = carry
    (tile,) = seq
    conv_w, conv_b, alog_row, dtb_row, d_row, norm_w = params
    t = tile.shape[0]
    xbc_raw, z, sm = split_cols(tile, (1024, 512, 128))
    xbc = _silu(conv4(xbc_raw, tail, conv_w) + conv_b)
    x0, x1, x2, x3, b0, b1, c0, c1 = split_cols(xbc, (128,) * 8)
    xs, bs, cs = (x0, x1, x2, x3), (b0, b1), (c0, c1)
    ds = split_cols(d_row, (128,) * 4)
    lane = lax.broadcasted_iota(jnp.int32, (1, 128), 1)
    sub = lax.broadcasted_iota(jnp.int32, (128, 1), 0)
    low = lane < 64
    dt_all = jnp.where(lane < SSD_HEADS, _softplus(sm + dtb_row), 0.0)
    ac_all = cumsum_rows(dt_all * (-jnp.exp(alog_row)))
    ar_all = ac_all.T
    al_all = _row_pick(ac_all, t - 1)
    r = lax.broadcasted_iota(jnp.int32, (t, t), 0)
    c = lax.broadcasted_iota(jnp.int32, (t, t), 1)
    ys, new_states = [], []
    for p in range(4):
        g = p // 2
        cb = mm(cs[g], bs[g], False, True)
        col = [_lane_pick(ac_all, 2 * p + j) for j in range(2)]
        row = [_row_pick(ar_all, 2 * p + j) for j in range(2)]
        last = [_lane_pick(al_all, 2 * p + j) for j in range(2)]
        dt = [_lane_pick(dt_all, 2 * p + j) for j in range(2)]
        xdt = xs[p] * jnp.where(low, dt[0], dt[1])
        y = ds[p] * xs[p]
        for j in range(2):
            lm = jnp.exp(jnp.where(r >= c, col[j] - row[j], -1e30))
            y = y + mm(cb * lm, jnp.where(low if j == 0 else ~low, xdt, 0.0), False, False)
        y = y + mm(cs[g], states[p], False, True) * jnp.where(low, jnp.exp(col[0]), jnp.exp(col[1]))
        decay_end = jnp.where(low, jnp.exp(last[0] - col[0]), jnp.exp(last[1] - col[1]))
        st = mm(xdt * decay_end, bs[g], True, False)
        new_states.append(states[p] * jnp.where(sub < 64, jnp.exp(last[0]), jnp.exp(last[1])) + st)
        ys.append(y)
    gz = jnp.concatenate(ys, axis=1) * _silu(z)
    g0, g1 = split_cols(gz, (256, 256))
    n0, n1 = split_cols(norm_w, (256, 256))
    out = jnp.concatenate([_rms(g0, n0), _rms(g1, n1)], axis=1)
    return (*new_states, last8(xbc_raw)), (out,)


def lru_in_fn(carry, seq, params):
    (tail,) = carry
    (x,) = seq
    conv_w, conv_b, w_a, b_a, w_x, b_x, lam = params
    xc = conv4(x, tail, conv_w) + conv_b
    r = _sigmoid(mm(xc, w_a, False, False) + b_a)
    i = _sigmoid(mm(xc, w_x, False, False) + b_x)
    log_a = -LRU_C * r * _softplus(-lam)
    u = jnp.sqrt(-_expm1(2.0 * log_a)) * (i * xc)
    return (last8(x),), (jnp.exp(log_a), u)


def lru_out_fn(carry, seq, params):
    hs, gate = seq
    return (), (hs * _gelu(gate),)


def merge_fn(carry, seq, params):
    ya, yb, yc, gl = seq
    g = split_cols(_sigmoid(gl), (D_MODEL,) * 3)
    merged = sum(g[r] * mm(y, params[r], False, False) for r, y in enumerate((ya, yb, yc)))
    return (), (merged,)


def _adaln(x, w, sc, sh):
    return _rms(x, w) * (1.0 + sc) + sh


def norm1_fn(carry, seq, params):
    (x,) = seq
    return (), (_adaln(x, *params), x)


def resid_norm_fn(carry, seq, params):
    x, mix = seq
    gt, w, sc, sh = params
    x1 = x + gt * mix
    return (), (_adaln(x1, w, sc, sh), x1)


def act_fn(carry, seq, params):
    (up,) = seq
    r = jnp.where(up > 0, up, 0.0)
    return (), (r * r,)


def resid_fn(carry, seq, params):
    x, dn = seq
    (gt,) = params
    return (), (x + gt * dn,)


def silu_fn(carry, seq, params):
    return (), (_silu(seq[0]),)


def _full_spec(a):
    nd = a.ndim
    return pl.BlockSpec(a.shape, lambda i: (0,) * nd)


def _cparams(*sem):
    return pltpu.CompilerParams(dimension_semantics=sem, vmem_limit_bytes=VMEM_LIMIT)


def scan_fwd(fn, name, tile, seqs, params, carry_shapes, outs, save_carry=False):
    rows = seqs[0].shape[0]
    tile = min(tile, rows)
    n = rows // tile
    ns, npar, nc, no = len(seqs), len(params), len(carry_shapes), len(outs)

    def body(*refs):
        seq_refs, refs = refs[:ns], refs[ns:]
        par_refs, refs = refs[:npar], refs[npar:]
        out_refs, refs = refs[:no], refs[no:]
        save_refs, refs = (refs[:nc], refs[nc:]) if save_carry else ((), refs)
        carry_refs = refs

        @pl.when(pl.program_id(0) == 0)
        def _():
            for cr in carry_refs:
                cr[...] = jnp.zeros_like(cr)

        carry = tuple(cr[...] for cr in carry_refs)
        for sr, cv in zip(save_refs, carry):
            sr[0] = cv
        new_carry, res = fn(carry, tuple(r[...].astype(F32) for r in seq_refs),
                            tuple(r[...].astype(F32) for r in par_refs))
        for r, v in zip(out_refs, res):
            r[...] = v.astype(r.dtype)
        for cr, v in zip(carry_refs, new_carry):
            cr[...] = v

    out_shape = [jax.ShapeDtypeStruct((rows, w), dt) for w, dt in outs]
    out_specs = [pl.BlockSpec((tile, w), lambda i: (i, 0)) for w, _ in outs]
    if save_carry:
        out_shape += [jax.ShapeDtypeStruct((n, *s), F32) for s in carry_shapes]
        out_specs += [pl.BlockSpec((1, *s), lambda i: (i, 0, 0)) for s in carry_shapes]
    res = pl.pallas_call(
        body, name=name, grid=(n,),
        in_specs=[pl.BlockSpec((tile, s.shape[1]), lambda i: (i, 0)) for s in seqs] + [_full_spec(p) for p in params],
        out_specs=out_specs, out_shape=out_shape,
        scratch_shapes=[pltpu.VMEM(s, F32) for s in carry_shapes],
        compiler_params=_cparams("arbitrary"),
    )(*seqs, *params)
    return res[:no], res[no:]


def scan_bwd(fn, name, tile, seqs, params, saved, douts, n_dseq, n_dpar):
    rows = seqs[0].shape[0]
    tile = min(tile, rows)
    n = rows // tile
    ns, npar, nc, no = len(seqs), len(params), len(saved), len(douts)

    def body(*refs):
        seq_refs, refs = refs[:ns], refs[ns:]
        par_refs, refs = refs[:npar], refs[npar:]
        save_refs, refs = refs[:nc], refs[nc:]
        dout_refs, refs = refs[:no], refs[no:]
        dseq_refs, refs = refs[:n_dseq], refs[n_dseq:]
        dpar_refs, refs = refs[:n_dpar], refs[n_dpar:]
        dcarry_refs = refs

        @pl.when(pl.program_id(0) == 0)
        def _():
            for r in (*dpar_refs, *dcarry_refs):
                r[...] = jnp.zeros_like(r)

        carry = tuple(r[0] for r in save_refs)
        seq = tuple(r[...].astype(F32) for r in seq_refs)
        par = tuple(r[...].astype(F32) for r in par_refs)

        def f(carry, dseq, dpar):
            return fn(carry, (*dseq, *seq[n_dseq:]), (*dpar, *par[n_dpar:]))

        _, vjp = jax.vjp(f, carry, seq[:n_dseq], par[:n_dpar])
        d_carry, d_seq, d_par = vjp((tuple(r[...] for r in dcarry_refs),
                                     tuple(r[...].astype(F32) for r in dout_refs)))
        for r, v in zip(dseq_refs, d_seq):
            r[...] = v
        for r, v in zip(dpar_refs, d_par):
            r[...] += v
        for r, v in zip(dcarry_refs, d_carry):
            r[...] = v

    rev = lambda i: (n - 1 - i, 0)
    res = pl.pallas_call(
        body, name=name, grid=(n,),
        in_specs=([pl.BlockSpec((tile, s.shape[1]), rev) for s in seqs] + [_full_spec(p) for p in params]
                  + [pl.BlockSpec((1, *s.shape[1:]), lambda i: (n - 1 - i, 0, 0)) for s in saved]
                  + [pl.BlockSpec((tile, d.shape[1]), rev) for d in douts]),
        out_specs=([pl.BlockSpec((tile, s.shape[1]), rev) for s in seqs[:n_dseq]]
                   + [_full_spec(p) for p in params[:n_dpar]]),
        out_shape=([jax.ShapeDtypeStruct((rows, s.shape[1]), F32) for s in seqs[:n_dseq]]
                   + [jax.ShapeDtypeStruct(p.shape, F32) for p in params[:n_dpar]]),
        scratch_shapes=[pltpu.VMEM(s.shape[1:], F32) for s in saved],
        compiler_params=_cparams("arbitrary"),
    )(*seqs, *params, *saved, *douts)
    return res[:n_dseq], res[n_dseq:]


def _tile_of(dim, pref):
    if dim <= pref:
        return dim
    best = max((t for t in range(128, pref + 1, 128) if dim % t == 0), default=None)
    if best is None or (best < 512 and dim <= 2304):
        return dim
    return best


def _row_tile(rows, pref):
    if rows <= pref:
        return rows
    return max(t for t in range(8, pref + 1, 8) if rows % t == 0)


def matmul(a, b, name, ta=False, tb=False, out_dtype=F32, add=None, bias=None, tm=512, tn=1024, tk=1024):
    m, k = (a.shape[1], a.shape[0]) if ta else a.shape
    n = b.shape[0] if tb else b.shape[1]
    assert k == (b.shape[1] if tb else b.shape[0])
    tm, tn, tk = _tile_of(m, tm), _tile_of(n, tn), _tile_of(k, tk)
    nk = k // tk
    dn = (((0 if ta else 1,), (1 if tb else 0,)), ((), ()))
    has_add, has_bias = add is not None, bias is not None

    def body(*refs):
        a_ref, b_ref, refs = refs[0], refs[1], refs[2:]
        add_ref, refs = (refs[0], refs[1:]) if has_add else (None, refs)
        bias_ref, refs = (refs[0], refs[1:]) if has_bias else (None, refs)
        o_ref, acc_ref = refs
        kk = pl.program_id(2)

        @pl.when(kk == 0)
        def _():
            acc_ref[...] = jnp.zeros_like(acc_ref)

        acc_ref[...] += lax.dot_general(a_ref[...].astype(BF16), b_ref[...].astype(BF16), dn,
                                        preferred_element_type=F32)

        @pl.when(kk == nk - 1)
        def _():
            r = acc_ref[...]
            if has_add:
                r = r + add_ref[...]
            if has_bias:
                r = r + bias_ref[...]
            o_ref[...] = r.astype(o_ref.dtype)

    a_spec = pl.BlockSpec((tk, tm), lambda i, j, l: (l, i)) if ta else pl.BlockSpec((tm, tk), lambda i, j, l: (i, l))
    b_spec = pl.BlockSpec((tn, tk), lambda i, j, l: (j, l)) if tb else pl.BlockSpec((tk, tn), lambda i, j, l: (l, j))
    o_spec = pl.BlockSpec((tm, tn), lambda i, j, l: (i, j))
    in_specs, args = [a_spec, b_spec], [a, b]
    if has_add:
        in_specs.append(o_spec)
        args.append(add)
    if has_bias:
        in_specs.append(pl.BlockSpec((1, tn), lambda i, j, l: (0, j)))
        args.append(bias)
    return pl.pallas_call(
        body, name=name, grid=(m // tm, n // tn, nk), in_specs=in_specs, out_specs=o_spec,
        out_shape=jax.ShapeDtypeStruct((m, n), out_dtype),
        scratch_shapes=[pltpu.VMEM((tm, tn), F32)],
        compiler_params=_cparams("parallel", "parallel", "arbitrary"),
    )(*args)


LIN_TILE = 512


def linscan_fwd(a, u, name):
    rows, w = a.shape
    tile = min(LIN_TILE, rows)

    def body(a_ref, u_ref, h_ref, hc):
        @pl.when(pl.program_id(0) == 0)
        def _():
            hc[...] = jnp.zeros_like(hc)

        def step(t, h):
            h = a_ref[pl.ds(t, 1), :] * h + u_ref[pl.ds(t, 1), :]
            h_ref[pl.ds(t, 1), :] = h
            return h

        hc[...] = lax.fori_loop(0, tile, step, hc[...], unroll=8)

    spec = pl.BlockSpec((tile, w), lambda i: (i, 0))
    return pl.pallas_call(
        body, name=name, grid=(rows // tile,), in_specs=[spec, spec], out_specs=spec,
        out_shape=jax.ShapeDtypeStruct((rows, w), F32), scratch_shapes=[pltpu.VMEM((1, w), F32)],
        compiler_params=_cparams("arbitrary"),
    )(a, u)


def linscan_bwd(a, hs, dh, name):
    rows, w = a.shape
    tile = min(LIN_TILE, rows)
    n = rows // tile
    per = tile // 8

    def body(a_ref, h_ref, hprev_ref, dh_ref, da_ref, du_ref, cc):
        i = pl.program_id(0)

        @pl.when(i == 0)
        def _():
            cc[...] = jnp.zeros_like(cc)

        def step(s, c):
            t = tile - 1 - s
            g = dh_ref[pl.ds(t, 1), :] + c
            du_ref[pl.ds(t, 1), :] = g
            da_ref[pl.ds(t, 1), :] = g * h_ref[pl.ds(t - 1, 1), :]
            return a_ref[pl.ds(t, 1), :] * g

        c = lax.fori_loop(0, tile - 1, step, cc[...], unroll=8)
        g = dh_ref[0:1, :] + c
        du_ref[0:1, :] = g
        da_ref[0:1, :] = g * jnp.where(i == n - 1, 0.0, hprev_ref[7:8, :])
        cc[...] = a_ref[0:1, :] * g

    rev = pl.BlockSpec((tile, w), lambda i: (n - 1 - i, 0))
    prev = pl.BlockSpec((8, w), lambda i: (jnp.maximum((n - 1 - i) * per - 1, 0), 0))
    return pl.pallas_call(
        body, name=name, grid=(n,), in_specs=[rev, rev, prev, rev], out_specs=[rev, rev],
        out_shape=[jax.ShapeDtypeStruct((rows, w), F32)] * 2, scratch_shapes=[pltpu.VMEM((1, w), F32)],
        compiler_params=_cparams("arbitrary"),
    )(a, hs, hs, dh)


def loss_head(x, target, w, name):
    rows, d = x.shape
    tile = min(512, rows)

    def body(x_ref, t_ref, w_ref, loss_ref, dx_ref, dw_ref):
        @pl.when(pl.program_id(0) == 0)
        def _():
            loss_ref[...] = jnp.zeros_like(loss_ref)
            dw_ref[...] = jnp.zeros_like(dw_ref)

        tv = t_ref[...]

        def f(xv, wv):
            e = _rms(xv, wv) - tv
            return 0.5 * jnp.sum(jnp.mean(e * e, axis=-1, keepdims=True), axis=0, keepdims=True)

        val, vjp = jax.vjp(f, x_ref[...], w_ref[...])
        dxv, dwv = vjp(jnp.ones((1, 1), F32))
        loss_ref[...] += jnp.broadcast_to(val, loss_ref.shape)
        dx_ref[...] = dxv
        dw_ref[...] += dwv

    spec = pl.BlockSpec((tile, d), lambda i: (i, 0))
    return pl.pallas_call(
        body, name=name, grid=(rows // tile,), in_specs=[spec, spec, _full_spec(w)],
        out_specs=[pl.BlockSpec((8, 128), lambda i: (0, 0)), spec, _full_spec(w)],
        out_shape=[jax.ShapeDtypeStruct((8, 128), F32), jax.ShapeDtypeStruct((rows, d), F32),
                   jax.ShapeDtypeStruct(w.shape, F32)],
        compiler_params=_cparams("arbitrary"),
    )(x, target, w)


def adamw(g, w, m, v, name):
    rows, cols = g.shape
    tile = _row_tile(rows, 256)

    def body(g_ref, w_ref, m_ref, v_ref, d_ref, nm_ref, nv_ref):
        gv = g_ref[...]
        nm = ADAM_B1 * m_ref[...] + (1.0 - ADAM_B1) * gv
        nv = ADAM_B2 * v_ref[...] + (1.0 - ADAM_B2) * (gv * gv)
        m_hat = nm / (1.0 - ADAM_B1 ** ADAM_STEP)
        v_hat = nv / (1.0 - ADAM_B2 ** ADAM_STEP)
        d_ref[...] = -ADAM_LR * (m_hat / (jnp.sqrt(v_hat) + ADAM_EPS) + ADAM_WD * w_ref[...])
        nm_ref[...] = nm
        nv_ref[...] = nv

    spec = pl.BlockSpec((tile, cols), lambda i: (i, 0))
    return pl.pallas_call(
        body, name=name, grid=(rows // tile,), in_specs=[spec] * 4, out_specs=[spec] * 3,
        out_shape=[jax.ShapeDtypeStruct((rows, cols), F32)] * 3,
        compiler_params=_cparams("parallel"),
    )(g, w, m, v)


def add_cast(p, q, name):
    shape = p.shape
    p, q = p.reshape(-1, shape[-1]), q.reshape(-1, shape[-1])
    rows, cols = p.shape
    tile = _row_tile(rows, 256)

    def body(p_ref, q_ref, o_ref, ob_ref):
        s = p_ref[...] + q_ref[...]
        o_ref[...] = s
        ob_ref[...] = s.astype(BF16)

    spec = pl.BlockSpec((tile, cols), lambda i: (i, 0))
    s, sb = pl.pallas_call(
        body, name=name, grid=(rows // tile,), in_specs=[spec, spec], out_specs=[spec, spec],
        out_shape=[jax.ShapeDtypeStruct((rows, cols), F32), jax.ShapeDtypeStruct((rows, cols), BF16)],
        compiler_params=_cparams("parallel"),
    )(p, q)
    return s.reshape(shape), sb.reshape(shape)


def sum4(own, recv, by_cols, chip, name):
    _, r, c = recv.shape
    tile = _row_tile(r, 256)
    nt = r // tile
    own_map = (lambda i, k: (i, k[0])) if by_cols else (lambda i, k: (k[0] * nt + i, 0))

    def body(k_ref, own_ref, recv_ref, o_ref):
        o_ref[...] = ((own_ref[...] + recv_ref[0].astype(F32)) + recv_ref[1].astype(F32)) + recv_ref[2].astype(F32)

    return pl.pallas_call(
        body, name=name,
        grid_spec=pltpu.PrefetchScalarGridSpec(
            num_scalar_prefetch=1, grid=(nt,),
            in_specs=[pl.BlockSpec((tile, c), own_map), pl.BlockSpec((3, tile, c), lambda i, k: (0, i, 0))],
            out_specs=pl.BlockSpec((tile, c), lambda i, k: (i, 0))),
        out_shape=jax.ShapeDtypeStruct((r, c), F32),
        compiler_params=_cparams("arbitrary"),
    )(jnp.reshape(chip, (1,)).astype(jnp.int32), own, recv)


def add8(parts, name):
    _, rows, cols = parts.shape

    def body(p_ref, o_ref):
        acc = p_ref[0]
        for k in range(1, 8):
            acc = acc + p_ref[k]
        o_ref[...] = acc

    return pl.pallas_call(
        body, name=name, in_specs=[pl.BlockSpec(memory_space=pltpu.VMEM)],
        out_specs=pl.BlockSpec(memory_space=pltpu.VMEM),
        out_shape=jax.ShapeDtypeStruct((rows, cols), F32),
    )(parts)


def _place():
    return lax.axis_index("x"), lax.axis_index("y"), lax.axis_index("c")


def _other_chips(x, y):
    return [(1 - x, y), (x, 1 - y), (1 - x, 1 - y)]


_ANY = pl.BlockSpec(memory_space=pl.ANY)


BIG_LAYOUT = (("ada_w", "col", (1024, 6144)), ("w_in", "chip", (4, 1024, 1924)), ("w_branch", "col", (3, 512, 1024)),
              ("w_out", "row", (1024, 1024)), ("w_up", "col", (1024, 4096)), ("w_down", "row", (4096, 1024)))
N_BIG = len(BIG_LAYOUT)


def _local_shape(kind, full):
    if kind == "col":
        return (*full[:-1], full[-1] // 4)
    if kind == "row":
        return (full[0] // 4, *full[1:])
    return full[1:]


def _window(ref, kind, k, local):
    if kind == "chip":
        return ref.at[k]
    if kind == "row":
        return ref.at[pl.ds(pl.multiple_of(k * local[0], 8), local[0])]
    idx = (slice(None),) * (len(local) - 1) + (pl.ds(pl.multiple_of(k * local[-1], 128), local[-1]),)
    return ref.at[idx]


def _dma_call(body, name, n_in, out_shape, sems, aliases=None):
    return pl.pallas_call(
        body, name=name, in_specs=[_ANY] * n_in, out_specs=[_ANY] * len(out_shape), out_shape=out_shape,
        scratch_shapes=[pltpu.SemaphoreType.DMA((n,)) for n in sems],
        input_output_aliases=aliases or {},
        compiler_params=pltpu.CompilerParams(has_side_effects=True))


def _remote(src, dst, send_sem, recv_sem, to):
    return pltpu.make_async_remote_copy(src_ref=src, dst_ref=dst, send_sem=send_sem, recv_sem=recv_sem,
                                        device_id=to, device_id_type=MESH)


def gather_big(shards, name):
    locals_ = [_local_shape(kind, full) for _, kind, full in BIG_LAYOUT]

    def body(*refs):
        sh, refs = refs[:N_BIG], refs[N_BIG:]
        full, refs = (refs[:N_BIG], refs[N_BIG:2 * N_BIG]), refs[2 * N_BIG:]
        send_sems, recv_sems, local_sems, pass_send, pass_recv = refs
        x, y, c = _place()
        me = 2 * x + y
        chips = _other_chips(x, y)
        for cc in (0, 1):
            @pl.when(c == cc)
            def _():
                win = lambda n, k: _window(full[cc][n], BIG_LAYOUT[n][1], k, locals_[n])
                mine, sends = [], []
                for n in range(N_BIG):
                    mine.append(pltpu.make_async_copy(sh[n].at[cc], win(n, me), local_sems.at[n]))
                    mine[n].start()
                    for j, chip in enumerate(chips):
                        sends.append(_remote(sh[n].at[cc], win(n, me), send_sems.at[3 * n + j],
                                             recv_sems.at[3 * n + j], (chip[0], chip[1], c)))
                        sends[-1].start()
                for n in range(N_BIG):
                    for j, chip in enumerate(chips):
                        _remote(sh[n].at[cc], win(n, 2 * chip[0] + chip[1]), send_sems.at[3 * n + j],
                                recv_sems.at[3 * n + j], (chip[0], chip[1], c)).wait_recv()
                    mine[n].wait()
                    sends.append(_remote(full[cc][n], full[cc][n], pass_send.at[n], pass_recv.at[n], (x, y, 1 - c)))
                    sends[-1].start()
                for n in range(N_BIG):
                    _remote(full[1 - cc][n], full[1 - cc][n], pass_send.at[n], pass_recv.at[n],
                            (x, y, 1 - c)).wait_recv()
                for cp in sends:
                    cp.wait_send()

    out_shape = [jax.ShapeDtypeStruct(full, BF16) for _, _, full in BIG_LAYOUT] * 2
    res = _dma_call(body, name, N_BIG, out_shape, (3 * N_BIG, 3 * N_BIG, N_BIG, N_BIG, N_BIG))(*shards)
    return res[:N_BIG], res[N_BIG:]


def reduce_d2d(g0, g1, name):
    def body(*refs):
        g, refs = (refs[:N_BIG], refs[N_BIG:2 * N_BIG]), refs[2 * N_BIG:]
        own, sib, refs = refs[:N_BIG], refs[N_BIG:2 * N_BIG], refs[2 * N_BIG:]
        send_sems, recv_sems, local_sems = refs
        x, y, c = _place()
        for cc in (0, 1):
            @pl.when(c == cc)
            def _():
                started = []
                for n in range(N_BIG):
                    cp = _remote(g[1 - cc][n], sib[n], send_sems.at[n], recv_sems.at[n], (x, y, 1 - c))
                    cp.start()
                    started.append(cp)
                    lc = pltpu.make_async_copy(g[cc][n], own[n], local_sems.at[n])
                    lc.start()
                    started.append(lc)
                for n in range(N_BIG):
                    _remote(g[1 - cc][n], sib[n], send_sems.at[n], recv_sems.at[n], (x, y, 1 - c)).wait_recv()
                for i, cp in enumerate(started):
                    if i % 2:
                        cp.wait()
                    else:
                        cp.wait_send()

    out_shape = [jax.ShapeDtypeStruct(a.shape, a.dtype) for a in g0] * 2
    res = _dma_call(body, name, 2 * N_BIG, out_shape, (N_BIG, N_BIG, N_BIG))(*g0, *g1)
    return res[:N_BIG], res[N_BIG:]


def reduce_ici(sums, name):
    locals_ = [_local_shape(kind, full) for _, kind, full in BIG_LAYOUT]

    def body(*refs):
        src, recv, (send_sems, recv_sems) = refs[:N_BIG], refs[N_BIG:2 * N_BIG], refs[2 * N_BIG:]
        x, y, c = _place()
        chips = _other_chips(x, y)
        copies = []
        for n in range(N_BIG):
            for j, chip in enumerate(chips):
                cp = _remote(_window(src[n], BIG_LAYOUT[n][1], 2 * chip[0] + chip[1], locals_[n]), recv[n].at[j],
                             send_sems.at[3 * n + j], recv_sems.at[3 * n + j], (chip[0], chip[1], c))
                cp.start()
                copies.append(cp)
        for cp in copies:
            cp.wait_recv()
        for cp in copies:
            cp.wait_send()

    out_shape = [jax.ShapeDtypeStruct((3, *ls), a.dtype) for ls, a in zip(locals_, sums)]
    return _dma_call(body, name, N_BIG, out_shape, (3 * N_BIG, 3 * N_BIG))(*sums)


def share_d2d(finals, name):
    def body(*refs):
        f, out, (send_sems, recv_sems, local_sems) = refs[:N_BIG], refs[N_BIG:2 * N_BIG], refs[2 * N_BIG:]
        x, y, c = _place()
        started = []
        for n in range(N_BIG):
            cp = _remote(f[n], out[n].at[c], send_sems.at[n], recv_sems.at[n], (x, y, 1 - c))
            cp.start()
            lc = pltpu.make_async_copy(f[n], out[n].at[c], local_sems.at[n])
            lc.start()
            started += [cp, lc]
        for n in range(N_BIG):
            _remote(f[n], out[n].at[1 - c], send_sems.at[n], recv_sems.at[n], (x, y, 1 - c)).wait_recv()
        for i, cp in enumerate(started):
            if i % 2:
                cp.wait()
            else:
                cp.wait_send()

    out_shape = [jax.ShapeDtypeStruct((2, *a.shape), a.dtype) for a in finals]
    return _dma_call(body, name, N_BIG, out_shape, (N_BIG, N_BIG, N_BIG))(*finals)


def allgather8(block, name):
    m_per, n = block.shape

    def body(x_ref, out_ref, send_sems, recv_sems, local_sem):
        x, y, c = _place()
        me, sibling = (x, y, c), (x, y, 1 - c)
        chips = _other_chips(x, y)

        def rows(px, py, pc):
            return out_ref.at[pl.ds((4 * px + 2 * py + pc) * m_per, m_per), :]

        def copy(k, blk, to, src=None):
            return pltpu.make_async_remote_copy(
                src_ref=rows(*blk) if src is None else src, dst_ref=rows(*blk), send_sem=send_sems.at[k],
                recv_sem=recv_sems.at[k], device_id=to, device_id_type=MESH)

        mine = pltpu.make_async_copy(x_ref, rows(*me), local_sem)
        mine.start()
        first = [copy(0, me, sibling, src=x_ref)]
        first += [copy(1 + j, me, (*chip, c), src=x_ref) for j, chip in enumerate(chips)]
        for cp in first:
            cp.start()
        passed = [copy(4 + j, (*chip, c), sibling) for j, chip in enumerate(chips)]
        for j, chip in enumerate(chips):
            copy(1 + j, (*chip, c), me).wait_recv()
            passed[j].start()
        copy(0, sibling, me).wait_recv()
        for j, chip in enumerate(chips):
            copy(4 + j, (*chip, 1 - c), me).wait_recv()
        for cp in first + passed:
            cp.wait_send()
        mine.wait()

    return pl.pallas_call(
        body, name=name, in_specs=[pl.BlockSpec(memory_space=pltpu.VMEM)],
        out_specs=pl.BlockSpec(memory_space=pltpu.VMEM),
        out_shape=jax.ShapeDtypeStruct((8 * m_per, n), block.dtype),
        scratch_shapes=[pltpu.SemaphoreType.DMA((7,)), pltpu.SemaphoreType.DMA((7,)), pltpu.SemaphoreType.DMA],
    )(block)


CONV = ("gdn_conv_w", "ssd_conv_w", "lru_conv_w")
SMALL = ("ada_b", "norm_mix", "gdn_a_log", "gdn_dt_bias", "gdn_norm", "ssd_conv_b", "ssd_a_log", "ssd_dt_bias",
         "ssd_d", "ssd_norm", "lru_conv_b", "lru_w_a", "lru_b_a", "lru_w_x", "lru_b_x", "lru_lambda", "norm_mlp",
         "final_norm")
WEIGHTS = ("ada_w", "ada_b", "norm_mix", "w_in", "gdn_conv_w", "gdn_a_log", "gdn_dt_bias", "gdn_norm", "ssd_conv_w",
           "ssd_conv_b", "ssd_a_log", "ssd_dt_bias", "ssd_d", "ssd_norm", "lru_conv_w", "lru_conv_b", "lru_w_a",
           "lru_b_a", "lru_w_x", "lru_b_x", "lru_lambda", "w_branch", "w_out", "norm_mlp", "w_up", "w_down",
           "final_norm")
PACK_COLS = 1024


def _pack(arrays, dtype):
    flat = jnp.concatenate([a.reshape(-1).astype(dtype) for a in arrays])
    pad = -flat.shape[0] % (8 * PACK_COLS)
    if pad:
        flat = jnp.concatenate([flat, jnp.zeros((pad,), dtype)])
    return flat.reshape(-1, PACK_COLS)


def _unpack(pack, shapes):
    flat = pack.reshape(-1)
    out, o = [], 0
    for s in shapes:
        n = math.prod(s)
        out.append(flat[o:o + n].reshape(s))
        o += n
    return out


def _split_w_in(w4):
    w = jnp.concatenate([w4[k] for k in range(4)], axis=1)
    pad = jnp.zeros((w.shape[0], 120), w.dtype)
    gdn = jnp.concatenate([w[:, 0:2056], pad], axis=1)
    ssd = jnp.concatenate([w[:, 2056:2568], w[:, 3080:3592], w[:, 2568:3080], w[:, 3592:3600], pad], axis=1)
    return gdn, ssd, w[:, 3600:4112], w[:, 4112:4624], w[:, 4624:7696]


def _join_w_in(gdn, ssd, lx, lg, gate):
    w = jnp.concatenate([gdn[:, 0:2056], ssd[:, 0:512], ssd[:, 1024:1536], ssd[:, 512:1024], ssd[:, 1536:1544],
                         lx, lg, gate], axis=1)
    return jnp.stack([w[:, k * 1924:(k + 1) * 1924] for k in range(4)])


def _lanes(v, at, width=128):
    return jnp.zeros((1, width), F32).at[0, at:at + v.shape[0]].set(v)


def _block_diag(w):
    return (jnp.eye(8, dtype=w.dtype)[:, None, :, None] * w[:, :, None, :]).reshape(512, 512)


def _diag_blocks(w):
    return jnp.stack([w[n * 64:(n + 1) * 64, n * 64:(n + 1) * 64] for n in range(8)])


TOK_TILE = 512
WIDE_TILE = 256


def _layer_params(p, big, l):
    row = lambda v: v.reshape(1, -1)
    b = dict(zip((n for n, _, _ in BIG_LAYOUT), big))
    gdn = (p["gdn_conv_w"][l], _lanes(p["gdn_a_log"][l], 4), _lanes(p["gdn_dt_bias"][l], 4), row(p["gdn_norm"][l]))
    ssd = (p["ssd_conv_w"][l], row(p["ssd_conv_b"][l]), _lanes(p["ssd_a_log"][l], 0), _lanes(p["ssd_dt_bias"][l], 0),
           row(jnp.repeat(p["ssd_d"][l], 64)), row(p["ssd_norm"][l]))
    lru = (p["lru_conv_w"][l], row(p["lru_conv_b"][l]), _block_diag(p["lru_w_a"][l]), row(p["lru_b_a"][l]),
           _block_diag(p["lru_w_x"][l]), row(p["lru_b_x"][l]), row(p["lru_lambda"][l]))
    return dict(gdn=gdn, ssd=ssd, lru=lru, w_in=_split_w_in(b["w_in"]),
                wb=tuple(b["w_branch"][r] for r in range(3)), w_out=b["w_out"], w_up=b["w_up"],
                w_down=b["w_down"], ada_w=b["ada_w"], ada_b=row(p["ada_b"][l]),
                norm_mix=row(p["norm_mix"][l]), norm_mlp=row(p["norm_mlp"][l]))


def _layer_fwd(x, silu_c, lp, l):
    nm = lambda s: f"l{l}_{s}"
    mod = matmul(silu_c, lp["ada_w"], nm("mod"), bias=lp["ada_b"])
    sh1, sc1, gt1, sh2, sc2, gt2 = (mod[0:1, k * D_MODEL:(k + 1) * D_MODEL] for k in range(N_MOD))
    (h,), _ = scan_fwd(norm1_fn, nm("norm1"), TOK_TILE, [x], [lp["norm_mix"], sc1, sh1], [], [(D_MODEL, BF16)])
    w_gdn, w_ssd, w_lx, w_lg, w_gate = lp["w_in"]
    p_gdn = matmul(h, w_gdn, nm("in_gdn"))
    p_ssd = matmul(h, w_ssd, nm("in_ssd"))
    p_lx = matmul(h, w_lx, nm("in_lx"))
    p_lg = matmul(h, w_lg, nm("in_lg"))
    p_gate = matmul(h, w_gate, nm("in_gate"))
    (ya,), sv_gdn = scan_fwd(gdn_fn, nm("gdn"), CHUNK, [p_gdn], lp["gdn"], [(128, 128)] * 4 + [(8, 1536)],
                             [(512, F32)], save_carry=True)
    (yb,), sv_ssd = scan_fwd(ssd_fn, nm("ssd"), CHUNK, [p_ssd], lp["ssd"], [(128, 128)] * 4 + [(8, 1024)],
                             [(512, F32)], save_carry=True)
    (a, u), sv_lru = scan_fwd(lru_in_fn, nm("lru_in"), TOK_TILE, [p_lx], lp["lru"], [(8, 512)],
                              [(512, F32), (512, F32)], save_carry=True)
    hs = linscan_fwd(a, u, nm("lru_scan"))
    (yc,), _ = scan_fwd(lru_out_fn, nm("lru_out"), TOK_TILE, [hs, p_lg], [], [], [(512, F32)])
    (merged,), _ = scan_fwd(merge_fn, nm("merge"), WIDE_TILE, [ya, yb, yc, p_gate], lp["wb"], [], [(D_MODEL, BF16)])
    mix = matmul(merged, lp["w_out"], nm("out"))
    (h2, x1), _ = scan_fwd(resid_norm_fn, nm("norm2"), TOK_TILE, [x, mix], [gt1, lp["norm_mlp"], sc2, sh2], [],
                           [(D_MODEL, BF16), (D_MODEL, F32)])
    up = matmul(h2, lp["w_up"], nm("up"))
    (act,), _ = scan_fwd(act_fn, nm("act"), WIDE_TILE, [up], [], [], [(D_FF, BF16)])
    dn = matmul(act, lp["w_down"], nm("down"))
    (x2,), _ = scan_fwd(resid_fn, nm("resid"), TOK_TILE, [x1, dn], [gt2], [], [(D_MODEL, F32)])
    saved = dict(x=x, h=h, p_gdn=p_gdn, p_ssd=p_ssd, p_lx=p_lx, p_lg=p_lg, p_gate=p_gate, sv_gdn=sv_gdn,
                 sv_ssd=sv_ssd, sv_lru=sv_lru, a=a, hs=hs, ya=ya, yb=yb, yc=yc, merged=merged, mix=mix, x1=x1,
                 h2=h2, up=up, act=act, dn=dn, mod=(sh1, sc1, gt1, sh2, sc2, gt2))
    return x2, saved


def _layer_bwd(d_x2, silu_c, lp, sv, l):
    nm = lambda s: f"l{l}_b_{s}"
    sh1, sc1, gt1, sh2, sc2, gt2 = sv["mod"]
    (d_x1, d_dn), (d_gt2,) = scan_bwd(resid_fn, nm("resid"), TOK_TILE, [sv["x1"], sv["dn"]], [gt2], [], [d_x2], 2, 1)
    d_act = matmul(d_dn, lp["w_down"], nm("down_x"), tb=True)
    g_w_down = matmul(sv["act"], d_dn, nm("down_w"), ta=True)
    (d_up,), _ = scan_bwd(act_fn, nm("act"), WIDE_TILE, [sv["up"]], [], [], [d_act], 1, 0)
    d_h2 = matmul(d_up, lp["w_up"], nm("up_x"), tb=True)
    g_w_up = matmul(sv["h2"], d_up, nm("up_w"), ta=True)
    (d_x, d_mix), (d_gt1, g_norm_mlp, d_sc2, d_sh2) = scan_bwd(
        resid_norm_fn, nm("norm2"), TOK_TILE, [sv["x"], sv["mix"]], [gt1, lp["norm_mlp"], sc2, sh2], [],
        [d_h2, d_x1], 2, 4)
    d_merged = matmul(d_mix, lp["w_out"], nm("out_x"), tb=True)
    g_w_out = matmul(sv["merged"], d_mix, nm("out_w"), ta=True)
    (d_ya, d_yb, d_yc, d_pgate), g_wb = scan_bwd(
        merge_fn, nm("merge"), WIDE_TILE, [sv["ya"], sv["yb"], sv["yc"], sv["p_gate"]], lp["wb"], [], [d_merged], 4, 3)
    (d_hs, d_plg), _ = scan_bwd(lru_out_fn, nm("lru_out"), TOK_TILE, [sv["hs"], sv["p_lg"]], [], [], [d_yc], 2, 0)
    d_a, d_u = linscan_bwd(sv["a"], sv["hs"], d_hs, nm("lru_scan"))
    (d_plx,), g_lru = scan_bwd(lru_in_fn, nm("lru_in"), TOK_TILE, [sv["p_lx"]], lp["lru"], sv["sv_lru"],
                               [d_a, d_u], 1, 7)
    (d_pssd,), g_ssd = scan_bwd(ssd_fn, nm("ssd"), CHUNK, [sv["p_ssd"]], lp["ssd"], sv["sv_ssd"], [d_yb], 1, 6)
    (d_pgdn,), g_gdn = scan_bwd(gdn_fn, nm("gdn"), CHUNK, [sv["p_gdn"]], lp["gdn"], sv["sv_gdn"], [d_ya], 1, 4)
    d_h = None
    g_w_in = []
    for tag, dp, w in zip(("gdn", "ssd", "lx", "lg", "gate"), (d_pgdn, d_pssd, d_plx, d_plg, d_pgate), lp["w_in"]):
        d_h = matmul(dp, w, nm("in_x_" + tag), tb=True, add=d_h)
        g_w_in.append(matmul(sv["h"], dp, nm("in_w_" + tag), ta=True))
    (d_x0,), (g_norm_mix, d_sc1, d_sh1) = scan_bwd(norm1_fn, nm("norm1"), TOK_TILE, [sv["x"]],
                                                   [lp["norm_mix"], sc1, sh1], [], [d_h, d_x], 1, 3)
    d_mod = jnp.concatenate([d_sh1, d_sc1, d_gt1, d_sh2, d_sc2, d_gt2], axis=1)
    d_mod8 = jnp.concatenate([d_mod, jnp.zeros((7, d_mod.shape[1]), F32)], axis=0)
    g_ada_w = matmul(silu_c, d_mod8, nm("mod_w"), ta=True)
    flat = lambda v: v.reshape(-1)
    grads = dict(
        ada_b=flat(d_mod), norm_mix=flat(g_norm_mix),
        gdn_conv_w=g_gdn[0], gdn_a_log=g_gdn[1][0, 4:8], gdn_dt_bias=g_gdn[2][0, 4:8], gdn_norm=flat(g_gdn[3]),
        ssd_conv_w=g_ssd[0], ssd_conv_b=flat(g_ssd[1]), ssd_a_log=g_ssd[2][0, 0:8], ssd_dt_bias=g_ssd[3][0, 0:8],
        ssd_d=g_ssd[4].reshape(8, 64).sum(axis=1), ssd_norm=flat(g_ssd[5]),
        lru_conv_w=g_lru[0], lru_conv_b=flat(g_lru[1]), lru_w_a=_diag_blocks(g_lru[2]), lru_b_a=flat(g_lru[3]),
        lru_w_x=_diag_blocks(g_lru[4]), lru_b_x=flat(g_lru[5]), lru_lambda=flat(g_lru[6]),
        norm_mlp=flat(g_norm_mlp))
    big = [g_ada_w, _join_w_in(*g_w_in), jnp.stack(g_wb), g_w_out, g_w_up, g_w_down]
    return d_x0, grads, big


def local_step(x, c, target, p, big):
    c8 = jnp.concatenate([c, jnp.zeros((7, c.shape[1]), F32)], axis=0)
    (silu_c,), _ = scan_fwd(silu_fn, "silu_c", 8, [c8], [], [], [(D_MODEL, F32)])
    lps = [_layer_params(p, big[l], l) for l in range(DEPTH)]
    saved = []
    for l in range(DEPTH):
        x, sv = _layer_fwd(x, silu_c, lps[l], l)
        saved.append(sv)
    loss, d_x, g_final = loss_head(x, target, p["final_norm"].reshape(1, -1), "loss_head")
    layer_grads, big_grads = [None] * DEPTH, [None] * DEPTH
    for l in reversed(range(DEPTH)):
        d_x, layer_grads[l], big_grads[l] = _layer_bwd(d_x, silu_c, lps[l], saved[l], l)
    grads = {k: jnp.stack([layer_grads[l][k] for l in range(DEPTH)]) for k in layer_grads[0]}
    grads["final_norm"] = g_final.reshape(-1)
    return loss, d_x, grads, big_grads


def _adam_nd(g, w, m, v, name):
    two = lambda a: a.reshape(-1, a.shape[-1])
    return tuple(r.reshape(w.shape) for r in adamw(two(g), two(w), two(m), two(v), name))


def _reduce_big(g0, g1, chip):
    own, sib = reduce_d2d(g0, g1, "reduce_pool")
    pooled = [add_cast(o, s, "reduce_pool_" + n) for (n, _, _), o, s in zip(BIG_LAYOUT, own, sib)]
    recv = reduce_ici([pb for _, pb in pooled], "reduce_ici")
    finals = []
    for (n, kind, full), (pf, _), r in zip(BIG_LAYOUT, pooled, recv):
        local = r.shape[1:]
        by_cols = kind == "col"
        own2 = pf.reshape(-1, full[-1])
        r3 = r.reshape(3, -1, local[-1])
        finals.append(sum4(own2, r3, by_cols, chip, "reduce_sum_" + n).reshape(local))
    return share_d2d(finals, "reduce_share")


def _step(w, m, v, x, c, target):
    chip = 2 * lax.axis_index("x") + lax.axis_index("y")
    conv_shapes = [w[n].shape for n in CONV]
    small_shapes = [w[n].shape for n in SMALL]

    big = gather_big([w[n].astype(BF16) for n, _, _ in BIG_LAYOUT], "gather_big")
    conv_all = allgather8(_pack([w[n] for n in CONV], F32), "gather_conv").reshape(8, -1, PACK_COLS)
    conv_parts = [_unpack(conv_all[2 * k], conv_shapes) for k in range(4)]
    p = {n: w[n] for n in SMALL}
    for i, n in enumerate(CONV):
        p[n] = jnp.concatenate([conv_parts[k][i] for k in range(4)], axis=2)

    loss_blk, grad_x, g, big_g = local_step(x[0], c, target[0], p, big)
    big_g = _reduce_big(big_g[0], big_g[1], chip)

    small_pack = _pack([loss_blk[0, 0:1]] + [g[n] for n in SMALL] + [g[n] for n in CONV], F32)
    small_all = allgather8(small_pack, "gather_small").reshape(8, -1, PACK_COLS)
    total = _unpack(add8(small_all, "reduce_small"), [(1,)] + small_shapes + [g[n].shape for n in CONV])
    loss = total[0][0]
    small_g = dict(zip(SMALL, total[1:1 + len(SMALL)]))
    conv_g = {n: lax.dynamic_slice_in_dim(t, chip * w[n].shape[2], w[n].shape[2], axis=2)
              for n, t in zip(CONV, total[1 + len(SMALL):])}

    grad, delta, new_m, new_v = {}, {}, {}, {}
    for (n, _, _), gn in zip(BIG_LAYOUT, big_g):
        grad[n] = gn
        delta[n], new_m[n], new_v[n] = _adam_nd(gn, w[n], m[n], v[n], "adam_" + n)
    for names, gs, shapes, tag in ((SMALL, small_g, small_shapes, "small"), (CONV, conv_g, conv_shapes, "conv")):
        pk = lambda d: _pack([d[n] for n in names], F32)
        res = adamw(pk(gs), pk(w), pk(m), pk(v), "adam_" + tag)
        for out, r in zip((delta, new_m, new_v), res):
            out.update(zip(names, _unpack(r, shapes)))
        grad.update({n: gs[n] for n in names})
    outs = [loss, grad_x[None]]
    for d in (grad, delta, new_m, new_v):
        outs += [d[n] for n in WEIGHTS]
    return tuple(outs)


def kernel(x, c, ada_w, ada_b, norm_mix, w_in, gdn_conv_w, gdn_a_log, gdn_dt_bias, gdn_norm, ssd_conv_w, ssd_conv_b, ssd_a_log, ssd_dt_bias, ssd_d, ssd_norm, lru_conv_w, lru_conv_b, lru_w_a, lru_b_a, lru_w_x, lru_b_x, lru_lambda, w_branch, w_out, norm_mlp, w_up, w_down, final_norm, loss_target, m_ada_w, m_ada_b, m_norm_mix, m_w_in, m_gdn_conv_w, m_gdn_a_log, m_gdn_dt_bias, m_gdn_norm, m_ssd_conv_w, m_ssd_conv_b, m_ssd_a_log, m_ssd_dt_bias, m_ssd_d, m_ssd_norm, m_lru_conv_w, m_lru_conv_b, m_lru_w_a, m_lru_b_a, m_lru_w_x, m_lru_b_x, m_lru_lambda, m_w_branch, m_w_out, m_norm_mlp, m_w_up, m_w_down, m_final_norm, v_ada_w, v_ada_b, v_norm_mix, v_w_in, v_gdn_conv_w, v_gdn_a_log, v_gdn_dt_bias, v_gdn_norm, v_ssd_conv_w, v_ssd_conv_b, v_ssd_a_log, v_ssd_dt_bias, v_ssd_d, v_ssd_norm, v_lru_conv_w, v_lru_conv_b, v_lru_w_a, v_lru_b_a, v_lru_w_x, v_lru_b_x, v_lru_lambda, v_w_branch, v_w_out, v_norm_mlp, v_w_up, v_w_down, v_final_norm):
    given = dict(locals())
    w = {n: given[n] for n in WEIGHTS}
    m = {n: given["m_" + n] for n in WEIGHTS}
    v = {n: given["v_" + n] for n in WEIGHTS}
    return _step(w, m, v, x, c, loss_target)
```

```python
import functools
import math

import jax
import jax.numpy as jnp
from jax import lax
from jax.experimental import pallas as pl
from jax.experimental.pallas import tpu as pltpu

F32 = jnp.float32
BF16 = jnp.bfloat16

D_MODEL = 1024
DEPTH = 2
RMS_EPS = 1e-6
CHUNK = 128
GDN_HEADS = 4
SSD_HEADS = 8
LRU_C = 8.0
D_FF = 4096
N_MOD = 6
W_GDN = 2176
W_SSD = 1664
W_LRU = 512
W_GATE = 3072
ADAM_LR = 0.001
ADAM_B1 = 0.9
ADAM_B2 = 0.999
ADAM_EPS = 1e-08
ADAM_WD = 0.01
ADAM_STEP = 10
VMEM_LIMIT = 56 * 1024 * 1024
MESH = pl.DeviceIdType.MESH


def _dot(a, b, ta, tb):
    dn = (((0 if ta else 1,), (1 if tb else 0,)), ((), ()))
    return lax.dot_general(a.astype(BF16), b.astype(BF16), dn, preferred_element_type=F32)


@functools.partial(jax.custom_vjp, nondiff_argnums=(2, 3))
def mm(a, b, ta, tb):
    return _dot(a, b, ta, tb)


def _mm_fwd(a, b, ta, tb):
    return _dot(a, b, ta, tb), (a, b)


def _mm_bwd(ta, tb, res, g):
    a, b = res
    if not ta and not tb:
        return mm(g, b, False, True), mm(a, g, True, False)
    if not ta and tb:
        return mm(g, b, False, False), mm(g, a, True, False)
    assert ta and not tb
    return mm(b, g, False, True), mm(a, g, False, False)


mm.defvjp(_mm_fwd, _mm_bwd)


def _tri_apply(x, upper):
    t = x.shape[0]
    r = lax.broadcasted_iota(jnp.int32, (t, t), 0)
    c = lax.broadcasted_iota(jnp.int32, (t, t), 1)
    tri = jnp.where((r <= c) if upper else (r >= c), 1.0, 0.0).astype(BF16)
    x1 = x.astype(BF16)
    r1 = x - x1.astype(F32)
    x2 = r1.astype(BF16)
    x3 = (r1 - x2.astype(F32)).astype(BF16)
    d = lambda p: jnp.dot(tri, p, preferred_element_type=F32)
    return (d(x1) + d(x2)) + d(x3)


@jax.custom_vjp
def cumsum_rows(x):
    return _tri_apply(x, False)


cumsum_rows.defvjp(lambda x: (_tri_apply(x, False), None), lambda _, g: (_tri_apply(g, True),))


def _dot_split(a, b):
    a1, b1 = a.astype(BF16), b.astype(BF16)
    a2, b2 = (a - a1.astype(F32)).astype(BF16), (b - b1.astype(F32)).astype(BF16)
    d = lambda p, q: jnp.dot(p, q, preferred_element_type=F32)
    return d(a1, b1) + (d(a1, b2) + d(a2, b1))


def _neumann(ms):
    t = ms[0].shape[0]
    xs = [-m for m in ms]
    qs = [_dot(m, m, False, False) for m in ms]
    n = 2
    while True:
        xs = [x + q + _dot(x, q, False, False) for x, q in zip(xs, qs)]
        n *= 2
        if n >= t:
            break
        qs = [_dot(q, q, False, False) for q in qs]
    rs = [-(x + m + _dot_split(m, x)) for x, m in zip(xs, ms)]
    return [x + r + _dot(x, r, False, False) for x, r in zip(xs, rs)]


@jax.custom_vjp
def tri_solve(ms, rhss):
    return tuple(rhs + _dot(x, rhs, False, False) for x, rhs in zip(_neumann(ms), rhss))


def _tri_solve_fwd(ms, rhss):
    xs = _neumann(ms)
    sols = tuple(rhs + _dot(x, rhs, False, False) for x, rhs in zip(xs, rhss))
    return sols, (tuple(xs), sols)


def _tri_solve_bwd(res, gs):
    xs, sols = res
    d_rhss = tuple(g + _dot(x, g, True, False) for x, g in zip(xs, gs))
    return tuple(-_dot(d, sol, False, True) for d, sol in zip(d_rhss, sols)), d_rhss


tri_solve.defvjp(_tri_solve_fwd, _tri_solve_bwd)


@functools.partial(jax.custom_vjp, nondiff_argnums=(1,))
def split_cols(x, sizes):
    out, o = [], 0
    for s in sizes:
        out.append(x[:, o:o + s])
        o += s
    return tuple(out)


split_cols.defvjp(lambda x, sizes: (split_cols(x, sizes), None),
                  lambda sizes, _, g: (jnp.concatenate(list(g), axis=1),))


@functools.partial(jax.custom_vjp, nondiff_argnums=(1,))
def _last_rows(x, t):
    return x[t - 8:, :]


_last_rows.defvjp(lambda x, t: (_last_rows(x, t), None),
                  lambda t, _, g: (jnp.concatenate([jnp.zeros((t - 8, g.shape[1]), g.dtype), g], axis=0),))


def last8(x):
    return _last_rows(x, x.shape[0])


def _shifted(xp, d, t):
    return (pltpu.roll(xp, d, 0) if d else xp)[8:8 + t, :]


@jax.custom_vjp
def conv4(x, tail, w):
    t = x.shape[0]
    xp = jnp.concatenate([tail, x], axis=0)
    return sum(_shifted(xp, 3 - k, t) * w[k:k + 1, :] for k in range(4))


def _conv4_fwd(x, tail, w):
    return conv4(x, tail, w), (x, tail, w)


def _conv4_bwd(res, g):
    x, tail, w = res
    t = x.shape[0]
    xp = jnp.concatenate([tail, x], axis=0)
    zero8 = jnp.zeros((8, g.shape[1]), g.dtype)
    d_xp = jnp.zeros_like(xp)
    d_w = []
    for k in range(4):
        gk = jnp.concatenate([zero8, g * w[k:k + 1, :]], axis=0)
        d_xp = d_xp + (pltpu.roll(gk, t + 8 - (3 - k), 0) if k < 3 else gk)
        d_w.append(jnp.sum(g * _shifted(xp, 3 - k, t), axis=0, keepdims=True))
    return d_xp[8:, :], d_xp[:8, :], jnp.concatenate(d_w, axis=0)


conv4.defvjp(_conv4_fwd, _conv4_bwd)


def _sigmoid(x):
    return 0.5 * (jnp.tanh(0.5 * x) + 1.0)


def _silu(x):
    return x * _sigmoid(x)


def _softplus(x):
    ax = jnp.where(x > 0, x, -x)
    return jnp.where(x > 0, x, 0.0) + jnp.log(1.0 + jnp.exp(-ax))


def _gelu(x):
    return 0.5 * x * (1.0 + jnp.tanh(math.sqrt(2.0 / math.pi) * (x + 0.044715 * (x * x * x))))


def _expm1(x):
    series = x * (1.0 + x * (0.5 + x * (1.0 / 6.0 + x * (1.0 / 24.0))))
    return jnp.where(jnp.abs(x) < 0.03, series, jnp.exp(x) - 1.0)


def _rms(x, w):
    return x * lax.rsqrt(jnp.mean(x * x, axis=-1, keepdims=True) + RMS_EPS) * w


def _lane_pick(x, j):
    lane = lax.broadcasted_iota(jnp.int32, (1, x.shape[1]), 1)
    return jnp.sum(jnp.where(lane == j, x, 0.0), axis=1, keepdims=True)


def _row_pick(x, j):
    row = lax.broadcasted_iota(jnp.int32, (x.shape[0], 1), 0)
    return jnp.sum(jnp.where(row == j, x, 0.0), axis=0, keepdims=True)


def gdn_fn(carry, seq, params):
    *states, tail = carry
    (tile,) = seq
    conv_w, alog_row, dtb_row, norm_w = params
    t = tile.shape[0]
    qkv_raw, z, sm = split_cols(tile, (1536, 512, 128))
    qkv = _silu(conv4(qkv_raw, tail, conv_w))
    parts = split_cols(qkv, (128,) * 12)
    zs = split_cols(z, (128,) * 4)
    lane = lax.broadcasted_iota(jnp.int32, (1, 128), 1)
    beta_all = _sigmoid(sm)
    g_all = jnp.where((lane >= 4) & (lane < 8), -jnp.exp(alog_row) * _softplus(sm + dtb_row), 0.0)
    gc_all = cumsum_rows(g_all)
    gr_all = gc_all.T
    gl_all = _row_pick(gc_all, t - 1)
    r = lax.broadcasted_iota(jnp.int32, (t, t), 0)
    c = lax.broadcasted_iota(jnp.int32, (t, t), 1)
    heads = range(GDN_HEADS)
    l2 = lambda a: a * lax.rsqrt(jnp.sum(a * a, axis=-1, keepdims=True) + RMS_EPS)
    qn = [l2(parts[h]) * (128.0 ** -0.5) for h in heads]
    kn = [l2(parts[4 + h]) for h in heads]
    beta = [_lane_pick(beta_all, h) for h in heads]
    gc = [_lane_pick(gc_all, 4 + h) for h in heads]
    gl = [_lane_pick(gl_all, 4 + h) for h in heads]
    decay = [jnp.exp(jnp.where(r >= c, gc[h] - _row_pick(gr_all, 4 + h), -1e30)) for h in heads]
    kk = [mm(kn[h], kn[h], False, True) for h in heads]
    qk = [mm(qn[h], kn[h], False, True) for h in heads]
    m = tuple(jnp.where(r > c, beta[h] * kk[h] * decay[h], 0.0) for h in heads)
    eg = [jnp.exp(gc[h]) for h in heads]
    rhs = tuple(jnp.concatenate([beta[h] * parts[8 + h], (beta[h] * eg[h]) * kn[h]], axis=1) for h in heads)
    uw = [split_cols(s, (128, 128)) for s in tri_solve(m, rhs)]
    ws = [mm(uw[h][1], states[h], False, False) for h in heads]
    qs = [mm(qn[h] * eg[h], states[h], False, False) for h in heads]
    v_new = [uw[h][0] - ws[h] for h in heads]
    o = [qs[h] + mm(qk[h] * decay[h], v_new[h], False, False) for h in heads]
    kv = [mm(kn[h] * jnp.exp(gl[h] - gc[h]), v_new[h], True, False) for h in heads]
    new_states = [states[h] * jnp.exp(gl[h]) + kv[h] for h in heads]
    outs = [_rms(o[h], norm_w) * _silu(zs[h]) for h in heads]
    return (*new_states, last8(qkv_raw)), (jnp.concatenate(outs, axis=1),)


def ssd_fn(carry, seq, params):
    *states, tail = carry
    (tile,) = seq
    conv_w, conv_b, alog_row, dtb_row, d_row, norm_w = params
    t = tile.shape[0]
    xbc_raw, z, sm = split_cols(tile, (1024, 512, 128))
    xbc = _silu(conv4(xbc_raw, tail, conv_w) + conv_b)
    x0, x1, x2, x3, b0, b1, c0, c1 = split_cols(xbc, (128,) * 8)
    xs, bs, cs = (x0, x1, x2, x3), (b0, b1), (c0, c1)
    ds = split_cols(d_row, (128,) * 4)
    lane = lax.broadcasted_iota(jnp.int32, (1, 128), 1)
    sub = lax.broadcasted_iota(jnp.int32, (128, 1), 0)
    low = lane < 64
    dt_all = jnp.where(lane < SSD_HEADS, _softplus(sm + dtb_row), 0.0)
    ac_all = cumsum_rows(dt_all * (-jnp.exp(alog_row)))
    ar_all = ac_all.T
    al_all = _row_pick(ac_all, t - 1)
    r = lax.broadcasted_iota(jnp.int32, (t, t), 0)
    c = lax.broadcasted_iota(jnp.int32, (t, t), 1)
    ys, new_states = [], []
    for p in range(4):
        g = p // 2
        cb = mm(cs[g], bs[g], False, True)
        col = [_lane_pick(ac_all, 2 * p + j) for j in range(2)]
        row = [_row_pick(ar_all, 2 * p + j) for j in range(2)]
        last = [_lane_pick(al_all, 2 * p + j) for j in range(2)]
        dt = [_lane_pick(dt_all, 2 * p + j) for j in range(2)]
        xdt = xs[p] * jnp.where(low, dt[0], dt[1])
        y = ds[p] * xs[p]
        for j in range(2):
            lm = jnp.exp(jnp.where(r >= c, col[j] - row[j], -1e30))
            y = y + mm(cb * lm, jnp.where(low if j == 0 else ~low, xdt, 0.0), False, False)
        y = y + mm(cs[g], states[p], False, True) * jnp.where(low, jnp.exp(col[0]), jnp.exp(col[1]))
        decay_end = jnp.where(low, jnp.exp(last[0] - col[0]), jnp.exp(last[1] - col[1]))
        st = mm(xdt * decay_end, bs[g], True, False)
        new_states.append(states[p] * jnp.where(sub < 64, jnp.exp(last[0]), jnp.exp(last[1])) + st)
        ys.append(y)
    gz = jnp.concatenate(ys, axis=1) * _silu(z)
    g0, g1 = split_cols(gz, (256, 256))
    n0, n1 = split_cols(norm_w, (256, 256))
    out = jnp.concatenate([_rms(g0, n0), _rms(g1, n1)], axis=1)
    return (*new_states, last8(xbc_raw)), (out,)


def lru_in_fn(carry, seq, params):
    (tail,) = carry
    (x,) = seq
    conv_w, conv_b, w_a, b_a, w_x, b_x, lam = params
    xc = conv4(x, tail, conv_w) + conv_b
    r = _sigmoid(mm(xc, w_a, False, False) + b_a)
    i = _sigmoid(mm(xc, w_x, False, False) + b_x)
    log_a = -LRU_C * r * _softplus(-lam)
    u = jnp.sqrt(-_expm1(2.0 * log_a)) * (i * xc)
    return (last8(x),), (jnp.exp(log_a), u)


def lru_out_fn(carry, seq, params):
    hs, gate = seq
    return (), (hs * _gelu(gate),)


def merge_fn(carry, seq, params):
    ya, yb, yc, gl = seq
    g = split_cols(_sigmoid(gl), (D_MODEL,) * 3)
    merged = sum(g[r] * mm(y, params[r], False, False) for r, y in enumerate((ya, yb, yc)))
    return (), (merged,)


def _adaln(x, w, sc, sh):
    return _rms(x, w) * (1.0 + sc) + sh


def norm1_fn(carry, seq, params):
    (x,) = seq
    return (), (_adaln(x, *params), x)


def resid_norm_fn(carry, seq, params):
    x, mix = seq
    gt, w, sc, sh = params
    x1 = x + gt * mix
    return (), (_adaln(x1, w, sc, sh), x1)


def act_fn(carry, seq, params):
    (up,) = seq
    r = jnp.where(up > 0, up, 0.0)
    return (), (r * r,)


def resid_fn(carry, seq, params):
    x, dn = seq
    (gt,) = params
    return (), (x + gt * dn,)


def silu_fn(carry, seq, params):
    return (), (_silu(seq[0]),)


def _full_spec(a):
    nd = a.ndim
    return pl.BlockSpec(a.shape, lambda i: (0,) * nd)


def _cparams(*sem):
    return pltpu.CompilerParams(dimension_semantics=sem, vmem_limit_bytes=VMEM_LIMIT)


def scan_fwd(fn, name, tile, seqs, params, carry_shapes, outs, save_carry=False):
    rows = seqs[0].shape[0]
    tile = min(tile, rows)
    n = rows // tile
    ns, npar, nc, no = len(seqs), len(params), len(carry_shapes), len(outs)

    def body(*refs):
        seq_refs, refs = refs[:ns], refs[ns:]
        par_refs, refs = refs[:npar], refs[npar:]
        out_refs, refs = refs[:no], refs[no:]
        save_refs, refs = (refs[:nc], refs[nc:]) if save_carry else ((), refs)
        carry_refs = refs

        @pl.when(pl.program_id(0) == 0)
        def _():
            for cr in carry_refs:
                cr[...] = jnp.zeros_like(cr)

        carry = tuple(cr[...] for cr in carry_refs)
        for sr, cv in zip(save_refs, carry):
            sr[0] = cv
        new_carry, res = fn(carry, tuple(r[...].astype(F32) for r in seq_refs),
                            tuple(r[...].astype(F32) for r in par_refs))
        for r, v in zip(out_refs, res):
            r[...] = v.astype(r.dtype)
        for cr, v in zip(carry_refs, new_carry):
            cr[...] = v

    out_shape = [jax.ShapeDtypeStruct((rows, w), dt) for w, dt in outs]
    out_specs = [pl.BlockSpec((tile, w), lambda i: (i, 0)) for w, _ in outs]
    if save_carry:
        out_shape += [jax.ShapeDtypeStruct((n, *s), F32) for s in carry_shapes]
        out_specs += [pl.BlockSpec((1, *s), lambda i: (i, 0, 0)) for s in carry_shapes]
    res = pl.pallas_call(
        body, name=name, grid=(n,),
        in_specs=[pl.BlockSpec((tile, s.shape[1]), lambda i: (i, 0)) for s in seqs] + [_full_spec(p) for p in params],
        out_specs=out_specs, out_shape=out_shape,
        scratch_shapes=[pltpu.VMEM(s, F32) for s in carry_shapes],
        compiler_params=_cparams("arbitrary"),
    )(*seqs, *params)
    return res[:no], res[no:]


def scan_bwd(fn, name, tile, seqs, params, saved, douts, n_dseq, n_dpar):
    rows = seqs[0].shape[0]
    tile = min(tile, rows)
    n = rows // tile
    ns, npar, nc, no = len(seqs), len(params), len(saved), len(douts)

    def body(*refs):
        seq_refs, refs = refs[:ns], refs[ns:]
        par_refs, refs = refs[:npar], refs[npar:]
        save_refs, refs = refs[:nc], refs[nc:]
        dout_refs, refs = refs[:no], refs[no:]
        dseq_refs, refs = refs[:n_dseq], refs[n_dseq:]
        dpar_refs, refs = refs[:n_dpar], refs[n_dpar:]
        dcarry_refs = refs

        @pl.when(pl.program_id(0) == 0)
        def _():
            for r in (*dpar_refs, *dcarry_refs):
                r[...] = jnp.zeros_like(r)

        carry = tuple(r[0] for r in save_refs)
        seq = tuple(r[...].astype(F32) for r in seq_refs)
        par = tuple(r[...].astype(F32) for r in par_refs)

        def f(carry, dseq, dpar):
            return fn(carry, (*dseq, *seq[n_dseq:]), (*dpar, *par[n_dpar:]))

        _, vjp = jax.vjp(f, carry, seq[:n_dseq], par[:n_dpar])
        d_carry, d_seq, d_par = vjp((tuple(r[...] for r in dcarry_refs),
                                     tuple(r[...].astype(F32) for r in dout_refs)))
        for r, v in zip(dseq_refs, d_seq):
            r[...] = v
        for r, v in zip(dpar_refs, d_par):
            r[...] += v
        for r, v in zip(dcarry_refs, d_carry):
            r[...] = v

    rev = lambda i: (n - 1 - i, 0)
    res = pl.pallas_call(
        body, name=name, grid=(n,),
        in_specs=([pl.BlockSpec((tile, s.shape[1]), rev) for s in seqs] + [_full_spec(p) for p in params]
                  + [pl.BlockSpec((1, *s.shape[1:]), lambda i: (n - 1 - i, 0, 0)) for s in saved]
                  + [pl.BlockSpec((tile, d.shape[1]), rev) for d in douts]),
        out_specs=([pl.BlockSpec((tile, s.shape[1]), rev) for s in seqs[:n_dseq]]
                   + [_full_spec(p) for p in params[:n_dpar]]),
        out_shape=([jax.ShapeDtypeStruct((rows, s.shape[1]), F32) for s in seqs[:n_dseq]]
                   + [jax.ShapeDtypeStruct(p.shape, F32) for p in params[:n_dpar]]),
        scratch_shapes=[pltpu.VMEM(s.shape[1:], F32) for s in saved],
        compiler_params=_cparams("arbitrary"),
    )(*seqs, *params, *saved, *douts)
    return res[:n_dseq], res[n_dseq:]


def _tile_of(dim, pref):
    if dim <= pref:
        return dim
    best = max((t for t in range(128, pref + 1, 128) if dim % t == 0), default=None)
    if best is None or (best < 512 and dim <= 2304):
        return dim
    return best


def _row_tile(rows, pref):
    if rows <= pref:
        return rows
    return max(t for t in range(8, pref + 1, 8) if rows % t == 0)


def matmul(a, b, name, ta=False, tb=False, out_dtype=F32, add=None, bias=None, tm=512, tn=1024, tk=1024):
    m, k = (a.shape[1], a.shape[0]) if ta else a.shape
    n = b.shape[0] if tb else b.shape[1]
    assert k == (b.shape[1] if tb else b.shape[0])
    tm, tn, tk = _tile_of(m, tm), _tile_of(n, tn), _tile_of(k, tk)
    nk = k // tk
    dn = (((0 if ta else 1,), (1 if tb else 0,)), ((), ()))
    has_add, has_bias = add is not None, bias is not None

    def body(*refs):
        a_ref, b_ref, refs = refs[0], refs[1], refs[2:]
        add_ref, refs = (refs[0], refs[1:]) if has_add else (None, refs)
        bias_ref, refs = (refs[0], refs[1:]) if has_bias else (None, refs)
        o_ref, acc_ref = refs
        kk = pl.program_id(2)

        @pl.when(kk == 0)
        def _():
            acc_ref[...] = jnp.zeros_like(acc_ref)

        acc_ref[...] += lax.dot_general(a_ref[...].astype(BF16), b_ref[...].astype(BF16), dn,
                                        preferred_element_type=F32)

        @pl.when(kk == nk - 1)
        def _():
            r = acc_ref[...]
            if has_add:
                r = r + add_ref[...]
            if has_bias:
                r = r + bias_ref[...]
            o_ref[...] = r.astype(o_ref.dtype)

    a_spec = pl.BlockSpec((tk, tm), lambda i, j, l: (l, i)) if ta else pl.BlockSpec((tm, tk), lambda i, j, l: (i, l))
    b_spec = pl.BlockSpec((tn, tk), lambda i, j, l: (j, l)) if tb else pl.BlockSpec((tk, tn), lambda i, j, l: (l, j))
    o_spec = pl.BlockSpec((tm, tn), lambda i, j, l: (i, j))
    in_specs, args = [a_spec, b_spec], [a, b]
    if has_add:
        in_specs.append(o_spec)
        args.append(add)
    if has_bias:
        in_specs.append(pl.BlockSpec((1, tn), lambda i, j, l: (0, j)))
        args.append(bias)
    return pl.pallas_call(
        body, name=name, grid=(m // tm, n // tn, nk), in_specs=in_specs, out_specs=o_spec,
        out_shape=jax.ShapeDtypeStruct((m, n), out_dtype),
        scratch_shapes=[pltpu.VMEM((tm, tn), F32)],
        compiler_params=_cparams("parallel", "parallel", "arbitrary"),
    )(*args)


LIN_TILE = 512


def linscan_fwd(a, u, name):
    rows, w = a.shape
    tile = min(LIN_TILE, rows)

    def body(a_ref, u_ref, h_ref, hc):
        @pl.when(pl.program_id(0) == 0)
        def _():
            hc[...] = jnp.zeros_like(hc)

        def step(t, h):
            h = a_ref[pl.ds(t, 1), :] * h + u_ref[pl.ds(t, 1), :]
            h_ref[pl.ds(t, 1), :] = h
            return h

        hc[...] = lax.fori_loop(0, tile, step, hc[...], unroll=8)

    spec = pl.BlockSpec((tile, w), lambda i: (i, 0))
    return pl.pallas_call(
        body, name=name, grid=(rows // tile,), in_specs=[spec, spec], out_specs=spec,
        out_shape=jax.ShapeDtypeStruct((rows, w), F32), scratch_shapes=[pltpu.VMEM((1, w), F32)],
        compiler_params=_cparams("arbitrary"),
    )(a, u)


def linscan_bwd(a, hs, dh, name):
    rows, w = a.shape
    tile = min(LIN_TILE, rows)
    n = rows // tile
    per = tile // 8

    def body(a_ref, h_ref, hprev_ref, dh_ref, da_ref, du_ref, cc):
        i = pl.program_id(0)

        @pl.when(i == 0)
        def _():
            cc[...] = jnp.zeros_like(cc)

        def step(s, c):
            t = tile - 1 - s
            g = dh_ref[pl.ds(t, 1), :] + c
            du_ref[pl.ds(t, 1), :] = g
            da_ref[pl.ds(t, 1), :] = g * h_ref[pl.ds(t - 1, 1), :]
            return a_ref[pl.ds(t, 1), :] * g

        c = lax.fori_loop(0, tile - 1, step, cc[...], unroll=8)
        g = dh_ref[0:1, :] + c
        du_ref[0:1, :] = g
        da_ref[0:1, :] = g * jnp.where(i == n - 1, 0.0, hprev_ref[7:8, :])
        cc[...] = a_ref[0:1, :] * g

    rev = pl.BlockSpec((tile, w), lambda i: (n - 1 - i, 0))
    prev = pl.BlockSpec((8, w), lambda i: (jnp.maximum((n - 1 - i) * per - 1, 0), 0))
    return pl.pallas_call(
        body, name=name, grid=(n,), in_specs=[rev, rev, prev, rev], out_specs=[rev, rev],
        out_shape=[jax.ShapeDtypeStruct((rows, w), F32)] * 2, scratch_shapes=[pltpu.VMEM((1, w), F32)],
        compiler_params=_cparams("arbitrary"),
    )(a, hs, hs, dh)


def loss_head(x, target, w, name):
    rows, d = x.shape
    tile = min(512, rows)

    def body(x_ref, t_ref, w_ref, loss_ref, dx_ref, dw_ref):
        @pl.when(pl.program_id(0) == 0)
        def _():
            loss_ref[...] = jnp.zeros_like(loss_ref)
            dw_ref[...] = jnp.zeros_like(dw_ref)

        tv = t_ref[...]

        def f(xv, wv):
            e = _rms(xv, wv) - tv
            return 0.5 * jnp.sum(jnp.mean(e * e, axis=-1, keepdims=True), axis=0, keepdims=True)

        val, vjp = jax.vjp(f, x_ref[...], w_ref[...])
        dxv, dwv = vjp(jnp.ones((1, 1), F32))
        loss_ref[...] += jnp.broadcast_to(val, loss_ref.shape)
        dx_ref[...] = dxv
        dw_ref[...] += dwv

    spec = pl.BlockSpec((tile, d), lambda i: (i, 0))
    return pl.pallas_call(
        body, name=name, grid=(rows // tile,), in_specs=[spec, spec, _full_spec(w)],
        out_specs=[pl.BlockSpec((8, 128), lambda i: (0, 0)), spec, _full_spec(w)],
        out_shape=[jax.ShapeDtypeStruct((8, 128), F32), jax.ShapeDtypeStruct((rows, d), F32),
                   jax.ShapeDtypeStruct(w.shape, F32)],
        compiler_params=_cparams("arbitrary"),
    )(x, target, w)


def adamw(g, w, m, v, name):
    rows, cols = g.shape
    tile = _row_tile(rows, 256)

    def body(g_ref, w_ref, m_ref, v_ref, d_ref, nm_ref, nv_ref):
        gv = g_ref[...]
        nm = ADAM_B1 * m_ref[...] + (1.0 - ADAM_B1) * gv
        nv = ADAM_B2 * v_ref[...] + (1.0 - ADAM_B2) * (gv * gv)
        m_hat = nm / (1.0 - ADAM_B1 ** ADAM_STEP)
        v_hat = nv / (1.0 - ADAM_B2 ** ADAM_STEP)
        d_ref[...] = -ADAM_LR * (m_hat / (jnp.sqrt(v_hat) + ADAM_EPS) + ADAM_WD * w_ref[...])
        nm_ref[...] = nm
        nv_ref[...] = nv

    spec = pl.BlockSpec((tile, cols), lambda i: (i, 0))
    return pl.pallas_call(
        body, name=name, grid=(rows // tile,), in_specs=[spec] * 4, out_specs=[spec] * 3,
        out_shape=[jax.ShapeDtypeStruct((rows, cols), F32)] * 3,
        compiler_params=_cparams("parallel"),
    )(g, w, m, v)


def add_cast(g0, g1, sib, place, name):
    shape = sib.shape
    g0, g1, sib = (a.reshape(-1, shape[-1]) for a in (g0, g1, sib))
    rows, cols = sib.shape
    tile = _row_tile(rows, max(8, min(256, (512 * 1024) // cols)))

    def body(k_ref, g0_ref, g1_ref, s_ref, o_ref, ob_ref):
        s = jnp.where(k_ref[1] == 0, g0_ref[...], g1_ref[...]) + s_ref[...]
        o_ref[...] = s
        ob_ref[...] = s.astype(BF16)

    spec = pl.BlockSpec((tile, cols), lambda i, k: (i, 0))
    s, sb = pl.pallas_call(
        body, name=name,
        grid_spec=pltpu.PrefetchScalarGridSpec(
            num_scalar_prefetch=1, grid=(rows // tile,),
            in_specs=[pl.BlockSpec((tile, cols), lambda i, k: (i * (1 - k[1]), 0)),
                      pl.BlockSpec((tile, cols), lambda i, k: (i * k[1], 0)), spec],
            out_specs=[spec, spec]),
        out_shape=[jax.ShapeDtypeStruct((rows, cols), F32), jax.ShapeDtypeStruct((rows, cols), BF16)],
        compiler_params=_cparams("arbitrary"),
    )(place, g0, g1, sib)
    return s.reshape(shape), sb.reshape(shape)


def sum4(own, recv, by_cols, place, name):
    _, r, c = recv.shape
    tile = _row_tile(r, 256)
    nt = r // tile
    own_map = (lambda i, k: (i, k[0])) if by_cols else (lambda i, k: (k[0] * nt + i, 0))

    def body(k_ref, own_ref, recv_ref, o_ref):
        o_ref[...] = ((own_ref[...] + recv_ref[0].astype(F32)) + recv_ref[1].astype(F32)) + recv_ref[2].astype(F32)

    return pl.pallas_call(
        body, name=name,
        grid_spec=pltpu.PrefetchScalarGridSpec(
            num_scalar_prefetch=1, grid=(nt,),
            in_specs=[pl.BlockSpec((tile, c), own_map), pl.BlockSpec((3, tile, c), lambda i, k: (0, i, 0))],
            out_specs=pl.BlockSpec((None, tile, c), lambda i, k: (k[1], i, 0))),
        out_shape=jax.ShapeDtypeStruct((2, r, c), F32),
        compiler_params=_cparams("arbitrary"),
    )(place, own, recv)


def add8(parts, name):
    _, rows, cols = parts.shape

    def body(p_ref, o_ref):
        acc = p_ref[0]
        for k in range(1, 8):
            acc = acc + p_ref[k]
        o_ref[...] = acc

    return pl.pallas_call(
        body, name=name, in_specs=[pl.BlockSpec(memory_space=pltpu.VMEM)],
        out_specs=pl.BlockSpec(memory_space=pltpu.VMEM),
        out_shape=jax.ShapeDtypeStruct((rows, cols), F32),
    )(parts)


def _place():
    return lax.axis_index("x"), lax.axis_index("y"), lax.axis_index("c")


def _other_chips(x, y):
    return [(1 - x, y), (x, 1 - y), (1 - x, 1 - y)]


_ANY = pl.BlockSpec(memory_space=pl.ANY)


BIG_LAYOUT = (("ada_w", "col", (1024, 6144)), ("w_in", "chip", (4, 1024, 1924)), ("w_branch", "col", (3, 512, 1024)),
              ("w_out", "row", (1024, 1024)), ("w_up", "col", (1024, 4096)), ("w_down", "row", (4096, 1024)))
N_BIG = len(BIG_LAYOUT)


def _local_shape(kind, full):
    if kind == "col":
        return (*full[:-1], full[-1] // 4)
    if kind == "row":
        return (full[0] // 4, *full[1:])
    return full[1:]


def _window(ref, kind, k, local):
    if kind == "chip":
        return ref.at[k]
    if kind == "row":
        return ref.at[pl.ds(pl.multiple_of(k * local[0], 8), local[0])]
    idx = (slice(None),) * (len(local) - 1) + (pl.ds(pl.multiple_of(k * local[-1], 128), local[-1]),)
    return ref.at[idx]


def _dma_call(body, name, n_in, out_shape, sems, aliases=None):
    return pl.pallas_call(
        body, name=name, in_specs=[_ANY] * n_in, out_specs=[_ANY] * len(out_shape), out_shape=out_shape,
        scratch_shapes=[pltpu.SemaphoreType.DMA((n,)) for n in sems],
        input_output_aliases=aliases or {},
        compiler_params=pltpu.CompilerParams(has_side_effects=True))


def _remote(src, dst, send_sem, recv_sem, to):
    return pltpu.make_async_remote_copy(src_ref=src, dst_ref=dst, send_sem=send_sem, recv_sem=recv_sem,
                                        device_id=to, device_id_type=MESH)


def gather_big(shards, name):
    locals_ = [_local_shape(kind, full) for _, kind, full in BIG_LAYOUT]

    def body(*refs):
        sh, refs = refs[:N_BIG], refs[N_BIG:]
        full, refs = (refs[:N_BIG], refs[N_BIG:2 * N_BIG]), refs[2 * N_BIG:]
        send_sems, recv_sems, local_sems, pass_send, pass_recv = refs
        x, y, c = _place()
        me = 2 * x + y
        chips = _other_chips(x, y)
        for cc in (0, 1):
            @pl.when(c == cc)
            def _():
                win = lambda n, k: _window(full[cc][n], BIG_LAYOUT[n][1], k, locals_[n])
                mine, sends = [], []
                for n in range(N_BIG):
                    mine.append(pltpu.make_async_copy(sh[n].at[cc], win(n, me), local_sems.at[n]))
                    mine[n].start()
                    for j, chip in enumerate(chips):
                        sends.append(_remote(sh[n].at[cc], win(n, me), send_sems.at[3 * n + j],
                                             recv_sems.at[3 * n + j], (chip[0], chip[1], c)))
                        sends[-1].start()
                for n in range(N_BIG):
                    for j, chip in enumerate(chips):
                        _remote(sh[n].at[cc], win(n, 2 * chip[0] + chip[1]), send_sems.at[3 * n + j],
                                recv_sems.at[3 * n + j], (chip[0], chip[1], c)).wait_recv()
                    mine[n].wait()
                    sends.append(_remote(full[cc][n], full[cc][n], pass_send.at[n], pass_recv.at[n], (x, y, 1 - c)))
                    sends[-1].start()
                for n in range(N_BIG):
                    _remote(full[1 - cc][n], full[1 - cc][n], pass_send.at[n], pass_recv.at[n],
                            (x, y, 1 - c)).wait_recv()
                for cp in sends:
                    cp.wait_send()

    out_shape = [jax.ShapeDtypeStruct(full, BF16) for _, _, full in BIG_LAYOUT] * 2
    res = _dma_call(body, name, N_BIG, out_shape, (3 * N_BIG, 3 * N_BIG, N_BIG, N_BIG, N_BIG))(*shards)
    return res[:N_BIG], res[N_BIG:]


def reduce_d2d(g0, g1, name):
    def body(*refs):
        g, refs = (refs[:N_BIG], refs[N_BIG:2 * N_BIG]), refs[2 * N_BIG:]
        sib, (send_sems, recv_sems) = refs[:N_BIG], refs[N_BIG:]
        x, y, c = _place()
        for cc in (0, 1):
            @pl.when(c == cc)
            def _():
                sends = [_remote(g[1 - cc][n], sib[n], send_sems.at[n], recv_sems.at[n], (x, y, 1 - c))
                         for n in range(N_BIG)]
                for cp in sends:
                    cp.start()
                for cp in sends:
                    cp.wait_recv()
                for cp in sends:
                    cp.wait_send()

    out_shape = [jax.ShapeDtypeStruct(a.shape, a.dtype) for a in g0]
    return _dma_call(body, name, 2 * N_BIG, out_shape, (N_BIG, N_BIG))(*g0, *g1)


def reduce_ici(sums, name):
    locals_ = [_local_shape(kind, full) for _, kind, full in BIG_LAYOUT]

    def body(*refs):
        src, recv, (send_sems, recv_sems) = refs[:N_BIG], refs[N_BIG:2 * N_BIG], refs[2 * N_BIG:]
        x, y, c = _place()
        chips = _other_chips(x, y)
        copies = []
        for n in range(N_BIG):
            for j, chip in enumerate(chips):
                cp = _remote(_window(src[n], BIG_LAYOUT[n][1], 2 * chip[0] + chip[1], locals_[n]), recv[n].at[j],
                             send_sems.at[3 * n + j], recv_sems.at[3 * n + j], (chip[0], chip[1], c))
                cp.start()
                copies.append(cp)
        for cp in copies:
            cp.wait_recv()
        for cp in copies:
            cp.wait_send()

    out_shape = [jax.ShapeDtypeStruct((3, *ls), a.dtype) for ls, a in zip(locals_, sums)]
    return _dma_call(body, name, N_BIG, out_shape, (3 * N_BIG, 3 * N_BIG))(*sums)


def share_d2d(finals, name):
    def body(*refs):
        out, (send_sems, recv_sems) = refs[N_BIG:2 * N_BIG], refs[2 * N_BIG:]
        x, y, c = _place()
        sends = [_remote(out[n].at[c], out[n].at[c], send_sems.at[n], recv_sems.at[n], (x, y, 1 - c))
                 for n in range(N_BIG)]
        for cp in sends:
            cp.start()
        for n in range(N_BIG):
            _remote(out[n].at[c], out[n].at[1 - c], send_sems.at[n], recv_sems.at[n], (x, y, 1 - c)).wait_recv()
        for cp in sends:
            cp.wait_send()

    out_shape = [jax.ShapeDtypeStruct(a.shape, a.dtype) for a in finals]
    return _dma_call(body, name, N_BIG, out_shape, (N_BIG, N_BIG), aliases={n: n for n in range(N_BIG)})(*finals)


def allgather8(block, name):
    m_per, n = block.shape

    def body(x_ref, out_ref, send_sems, recv_sems, local_sem):
        x, y, c = _place()
        me, sibling = (x, y, c), (x, y, 1 - c)
        chips = _other_chips(x, y)

        def rows(px, py, pc):
            return out_ref.at[pl.ds((4 * px + 2 * py + pc) * m_per, m_per), :]

        def copy(k, blk, to, src=None):
            return pltpu.make_async_remote_copy(
                src_ref=rows(*blk) if src is None else src, dst_ref=rows(*blk), send_sem=send_sems.at[k],
                recv_sem=recv_sems.at[k], device_id=to, device_id_type=MESH)

        mine = pltpu.make_async_copy(x_ref, rows(*me), local_sem)
        mine.start()
        first = [copy(0, me, sibling, src=x_ref)]
        first += [copy(1 + j, me, (*chip, c), src=x_ref) for j, chip in enumerate(chips)]
        for cp in first:
            cp.start()
        passed = [copy(4 + j, (*chip, c), sibling) for j, chip in enumerate(chips)]
        for j, chip in enumerate(chips):
            copy(1 + j, (*chip, c), me).wait_recv()
            passed[j].start()
        copy(0, sibling, me).wait_recv()
        for j, chip in enumerate(chips):
            copy(4 + j, (*chip, 1 - c), me).wait_recv()
        for cp in first + passed:
            cp.wait_send()
        mine.wait()

    return pl.pallas_call(
        body, name=name, in_specs=[pl.BlockSpec(memory_space=pltpu.VMEM)],
        out_specs=pl.BlockSpec(memory_space=pltpu.VMEM),
        out_shape=jax.ShapeDtypeStruct((8 * m_per, n), block.dtype),
        scratch_shapes=[pltpu.SemaphoreType.DMA((7,)), pltpu.SemaphoreType.DMA((7,)), pltpu.SemaphoreType.DMA],
    )(block)


CONV = ("gdn_conv_w", "ssd_conv_w", "lru_conv_w")
SMALL = ("ada_b", "norm_mix", "gdn_a_log", "gdn_dt_bias", "gdn_norm", "ssd_conv_b", "ssd_a_log", "ssd_dt_bias",
         "ssd_d", "ssd_norm", "lru_conv_b", "lru_w_a", "lru_b_a", "lru_w_x", "lru_b_x", "lru_lambda", "norm_mlp",
         "final_norm")
WEIGHTS = ("ada_w", "ada_b", "norm_mix", "w_in", "gdn_conv_w", "gdn_a_log", "gdn_dt_bias", "gdn_norm", "ssd_conv_w",
           "ssd_conv_b", "ssd_a_log", "ssd_dt_bias", "ssd_d", "ssd_norm", "lru_conv_w", "lru_conv_b", "lru_w_a",
           "lru_b_a", "lru_w_x", "lru_b_x", "lru_lambda", "w_branch", "w_out", "norm_mlp", "w_up", "w_down",
           "final_norm")
PACK_COLS = 1024


def _pack(arrays, dtype):
    flat = jnp.concatenate([a.reshape(-1).astype(dtype) for a in arrays])
    pad = -flat.shape[0] % (8 * PACK_COLS)
    if pad:
        flat = jnp.concatenate([flat, jnp.zeros((pad,), dtype)])
    return flat.reshape(-1, PACK_COLS)


def _unpack(pack, shapes):
    flat = pack.reshape(-1)
    out, o = [], 0
    for s in shapes:
        n = math.prod(s)
        out.append(flat[o:o + n].reshape(s))
        o += n
    return out


def _split_w_in(w4):
    w = jnp.concatenate([w4[k] for k in range(4)], axis=1)
    pad = jnp.zeros((w.shape[0], 120), w.dtype)
    gdn = jnp.concatenate([w[:, 0:2056], pad], axis=1)
    ssd = jnp.concatenate([w[:, 2056:2568], w[:, 3080:3592], w[:, 2568:3080], w[:, 3592:3600], pad], axis=1)
    return gdn, ssd, w[:, 3600:4112], w[:, 4112:4624], w[:, 4624:7696]


def _join_w_in(gdn, ssd, lx, lg, gate):
    w = jnp.concatenate([gdn[:, 0:2056], ssd[:, 0:512], ssd[:, 1024:1536], ssd[:, 512:1024], ssd[:, 1536:1544],
                         lx, lg, gate], axis=1)
    return jnp.stack([w[:, k * 1924:(k + 1) * 1924] for k in range(4)])


def _lanes(v, at, width=128):
    return jnp.zeros((1, width), F32).at[0, at:at + v.shape[0]].set(v)


def _block_diag(w):
    return (jnp.eye(8, dtype=w.dtype)[:, None, :, None] * w[:, :, None, :]).reshape(512, 512)


def _diag_blocks(w):
    return jnp.stack([w[n * 64:(n + 1) * 64, n * 64:(n + 1) * 64] for n in range(8)])


TOK_TILE = 512
WIDE_TILE = 256


def _layer_params(p, big, l):
    row = lambda v: v.reshape(1, -1)
    b = dict(zip((n for n, _, _ in BIG_LAYOUT), big))
    gdn = (p["gdn_conv_w"][l], _lanes(p["gdn_a_log"][l], 4), _lanes(p["gdn_dt_bias"][l], 4), row(p["gdn_norm"][l]))
    ssd = (p["ssd_conv_w"][l], row(p["ssd_conv_b"][l]), _lanes(p["ssd_a_log"][l], 0), _lanes(p["ssd_dt_bias"][l], 0),
           row(jnp.repeat(p["ssd_d"][l], 64)), row(p["ssd_norm"][l]))
    lru = (p["lru_conv_w"][l], row(p["lru_conv_b"][l]), _block_diag(p["lru_w_a"][l]), row(p["lru_b_a"][l]),
           _block_diag(p["lru_w_x"][l]), row(p["lru_b_x"][l]), row(p["lru_lambda"][l]))
    return dict(gdn=gdn, ssd=ssd, lru=lru, w_in=_split_w_in(b["w_in"]),
                wb=tuple(b["w_branch"][r] for r in range(3)), w_out=b["w_out"], w_up=b["w_up"],
                w_down=b["w_down"], ada_w=b["ada_w"], ada_b=row(p["ada_b"][l]),
                norm_mix=row(p["norm_mix"][l]), norm_mlp=row(p["norm_mlp"][l]))


def _layer_fwd(x, silu_c, lp, l):
    nm = lambda s: f"l{l}_{s}"
    mod = matmul(silu_c, lp["ada_w"], nm("mod"), bias=lp["ada_b"])
    sh1, sc1, gt1, sh2, sc2, gt2 = (mod[0:1, k * D_MODEL:(k + 1) * D_MODEL] for k in range(N_MOD))
    (h,), _ = scan_fwd(norm1_fn, nm("norm1"), TOK_TILE, [x], [lp["norm_mix"], sc1, sh1], [], [(D_MODEL, BF16)])
    w_gdn, w_ssd, w_lx, w_lg, w_gate = lp["w_in"]
    p_gdn = matmul(h, w_gdn, nm("in_gdn"))
    p_ssd = matmul(h, w_ssd, nm("in_ssd"))
    p_lx = matmul(h, w_lx, nm("in_lx"))
    p_lg = matmul(h, w_lg, nm("in_lg"))
    p_gate = matmul(h, w_gate, nm("in_gate"))
    (ya,), sv_gdn = scan_fwd(gdn_fn, nm("gdn"), CHUNK, [p_gdn], lp["gdn"], [(128, 128)] * 4 + [(8, 1536)],
                             [(512, F32)], save_carry=True)
    (yb,), sv_ssd = scan_fwd(ssd_fn, nm("ssd"), CHUNK, [p_ssd], lp["ssd"], [(128, 128)] * 4 + [(8, 1024)],
                             [(512, F32)], save_carry=True)
    (a, u), sv_lru = scan_fwd(lru_in_fn, nm("lru_in"), TOK_TILE, [p_lx], lp["lru"], [(8, 512)],
                              [(512, F32), (512, F32)], save_carry=True)
    hs = linscan_fwd(a, u, nm("lru_scan"))
    (yc,), _ = scan_fwd(lru_out_fn, nm("lru_out"), TOK_TILE, [hs, p_lg], [], [], [(512, F32)])
    (merged,), _ = scan_fwd(merge_fn, nm("merge"), WIDE_TILE, [ya, yb, yc, p_gate], lp["wb"], [], [(D_MODEL, BF16)])
    mix = matmul(merged, lp["w_out"], nm("out"))
    (h2, x1), _ = scan_fwd(resid_norm_fn, nm("norm2"), TOK_TILE, [x, mix], [gt1, lp["norm_mlp"], sc2, sh2], [],
                           [(D_MODEL, BF16), (D_MODEL, F32)])
    up = matmul(h2, lp["w_up"], nm("up"))
    (act,), _ = scan_fwd(act_fn, nm("act"), WIDE_TILE, [up], [], [], [(D_FF, BF16)])
    dn = matmul(act, lp["w_down"], nm("down"))
    (x2,), _ = scan_fwd(resid_fn, nm("resid"), TOK_TILE, [x1, dn], [gt2], [], [(D_MODEL, F32)])
    saved = dict(x=x, h=h, p_gdn=p_gdn, p_ssd=p_ssd, p_lx=p_lx, p_lg=p_lg, p_gate=p_gate, sv_gdn=sv_gdn,
                 sv_ssd=sv_ssd, sv_lru=sv_lru, a=a, hs=hs, ya=ya, yb=yb, yc=yc, merged=merged, mix=mix, x1=x1,
                 h2=h2, up=up, act=act, dn=dn, mod=(sh1, sc1, gt1, sh2, sc2, gt2))
    return x2, saved


def _layer_bwd(d_x2, silu_c, lp, sv, l):
    nm = lambda s: f"l{l}_b_{s}"
    sh1, sc1, gt1, sh2, sc2, gt2 = sv["mod"]
    (d_x1, d_dn), (d_gt2,) = scan_bwd(resid_fn, nm("resid"), TOK_TILE, [sv["x1"], sv["dn"]], [gt2], [], [d_x2], 2, 1)
    d_act = matmul(d_dn, lp["w_down"], nm("down_x"), tb=True)
    g_w_down = matmul(sv["act"], d_dn, nm("down_w"), ta=True)
    (d_up,), _ = scan_bwd(act_fn, nm("act"), WIDE_TILE, [sv["up"]], [], [], [d_act], 1, 0)
    d_h2 = matmul(d_up, lp["w_up"], nm("up_x"), tb=True)
    g_w_up = matmul(sv["h2"], d_up, nm("up_w"), ta=True)
    (d_x, d_mix), (d_gt1, g_norm_mlp, d_sc2, d_sh2) = scan_bwd(
        resid_norm_fn, nm("norm2"), TOK_TILE, [sv["x"], sv["mix"]], [gt1, lp["norm_mlp"], sc2, sh2], [],
        [d_h2, d_x1], 2, 4)
    d_merged = matmul(d_mix, lp["w_out"], nm("out_x"), tb=True)
    g_w_out = matmul(sv["merged"], d_mix, nm("out_w"), ta=True)
    (d_ya, d_yb, d_yc, d_pgate), g_wb = scan_bwd(
        merge_fn, nm("merge"), WIDE_TILE, [sv["ya"], sv["yb"], sv["yc"], sv["p_gate"]], lp["wb"], [], [d_merged], 4, 3)
    (d_hs, d_plg), _ = scan_bwd(lru_out_fn, nm("lru_out"), TOK_TILE, [sv["hs"], sv["p_lg"]], [], [], [d_yc], 2, 0)
    d_a, d_u = linscan_bwd(sv["a"], sv["hs"], d_hs, nm("lru_scan"))
    (d_plx,), g_lru = scan_bwd(lru_in_fn, nm("lru_in"), TOK_TILE, [sv["p_lx"]], lp["lru"], sv["sv_lru"],
                               [d_a, d_u], 1, 7)
    (d_pssd,), g_ssd = scan_bwd(ssd_fn, nm("ssd"), CHUNK, [sv["p_ssd"]], lp["ssd"], sv["sv_ssd"], [d_yb], 1, 6)
    (d_pgdn,), g_gdn = scan_bwd(gdn_fn, nm("gdn"), CHUNK, [sv["p_gdn"]], lp["gdn"], sv["sv_gdn"], [d_ya], 1, 4)
    d_h = None
    g_w_in = []
    for tag, dp, w in zip(("gdn", "ssd", "lx", "lg", "gate"), (d_pgdn, d_pssd, d_plx, d_plg, d_pgate), lp["w_in"]):
        d_h = matmul(dp, w, nm("in_x_" + tag), tb=True, add=d_h)
        g_w_in.append(matmul(sv["h"], dp, nm("in_w_" + tag), ta=True))
    (d_x0,), (g_norm_mix, d_sc1, d_sh1) = scan_bwd(norm1_fn, nm("norm1"), TOK_TILE, [sv["x"]],
                                                   [lp["norm_mix"], sc1, sh1], [], [d_h, d_x], 1, 3)
    d_mod = jnp.concatenate([d_sh1, d_sc1, d_gt1, d_sh2, d_sc2, d_gt2], axis=1)
    d_mod8 = jnp.concatenate([d_mod, jnp.zeros((7, d_mod.shape[1]), F32)], axis=0)
    g_ada_w = matmul(silu_c, d_mod8, nm("mod_w"), ta=True)
    flat = lambda v: v.reshape(-1)
    grads = dict(
        ada_b=flat(d_mod), norm_mix=flat(g_norm_mix),
        gdn_conv_w=g_gdn[0], gdn_a_log=g_gdn[1][0, 4:8], gdn_dt_bias=g_gdn[2][0, 4:8], gdn_norm=flat(g_gdn[3]),
        ssd_conv_w=g_ssd[0], ssd_conv_b=flat(g_ssd[1]), ssd_a_log=g_ssd[2][0, 0:8], ssd_dt_bias=g_ssd[3][0, 0:8],
        ssd_d=g_ssd[4].reshape(8, 64).sum(axis=1), ssd_norm=flat(g_ssd[5]),
        lru_conv_w=g_lru[0], lru_conv_b=flat(g_lru[1]), lru_w_a=_diag_blocks(g_lru[2]), lru_b_a=flat(g_lru[3]),
        lru_w_x=_diag_blocks(g_lru[4]), lru_b_x=flat(g_lru[5]), lru_lambda=flat(g_lru[6]),
        norm_mlp=flat(g_norm_mlp))
    big = [g_ada_w, _join_w_in(*g_w_in), jnp.stack(g_wb), g_w_out, g_w_up, g_w_down]
    return d_x0, grads, big


def local_step(x, c, target, p, big):
    c8 = jnp.concatenate([c, jnp.zeros((7, c.shape[1]), F32)], axis=0)
    (silu_c,), _ = scan_fwd(silu_fn, "silu_c", 8, [c8], [], [], [(D_MODEL, F32)])
    lps = [_layer_params(p, big[l], l) for l in range(DEPTH)]
    saved = []
    for l in range(DEPTH):
        x, sv = _layer_fwd(x, silu_c, lps[l], l)
        saved.append(sv)
    loss, d_x, g_final = loss_head(x, target, p["final_norm"].reshape(1, -1), "loss_head")
    layer_grads, big_grads = [None] * DEPTH, [None] * DEPTH
    for l in reversed(range(DEPTH)):
        d_x, layer_grads[l], big_grads[l] = _layer_bwd(d_x, silu_c, lps[l], saved[l], l)
    grads = {k: jnp.stack([layer_grads[l][k] for l in range(DEPTH)]) for k in layer_grads[0]}
    grads["final_norm"] = g_final.reshape(-1)
    return loss, d_x, grads, big_grads


def _adam_nd(g, w, m, v, name):
    two = lambda a: a.reshape(-1, a.shape[-1])
    return tuple(r.reshape(w.shape) for r in adamw(two(g), two(w), two(m), two(v), name))


def _reduce_big(g0, g1, place):
    sib = reduce_d2d(g0, g1, "reduce_pool")
    pooled = [add_cast(a, b, s, place, "reduce_pool_" + n) for (n, _, _), a, b, s in zip(BIG_LAYOUT, g0, g1, sib)]
    recv = reduce_ici([pb for _, pb in pooled], "reduce_ici")
    finals = []
    for (n, kind, full), (pf, _), r in zip(BIG_LAYOUT, pooled, recv):
        local = r.shape[1:]
        own2 = pf.reshape(-1, full[-1])
        r3 = r.reshape(3, -1, local[-1])
        finals.append(sum4(own2, r3, kind == "col", place, "reduce_sum_" + n).reshape(2, *local))
    return share_d2d(finals, "reduce_share")


def _step(w, m, v, x, c, target):
    chip = 2 * lax.axis_index("x") + lax.axis_index("y")
    place = jnp.stack([chip, lax.axis_index("c")]).astype(jnp.int32)
    conv_shapes = [w[n].shape for n in CONV]
    small_shapes = [w[n].shape for n in SMALL]

    big = gather_big([w[n].astype(BF16) for n, _, _ in BIG_LAYOUT], "gather_big")
    conv_all = allgather8(_pack([w[n] for n in CONV], F32), "gather_conv").reshape(8, -1, PACK_COLS)
    conv_parts = [_unpack(conv_all[2 * k], conv_shapes) for k in range(4)]
    p = {n: w[n] for n in SMALL}
    for i, n in enumerate(CONV):
        p[n] = jnp.concatenate([conv_parts[k][i] for k in range(4)], axis=2)

    loss_blk, grad_x, g, big_g = local_step(x[0], c, target[0], p, big)
    big_g = _reduce_big(big_g[0], big_g[1], place)

    small_pack = _pack([loss_blk[0, 0:1]] + [g[n] for n in SMALL] + [g[n] for n in CONV], F32)
    small_all = allgather8(small_pack, "gather_small").reshape(8, -1, PACK_COLS)
    total = _unpack(add8(small_all, "reduce_small"), [(1,)] + small_shapes + [g[n].shape for n in CONV])
    loss = total[0][0]
    small_g = dict(zip(SMALL, total[1:1 + len(SMALL)]))
    conv_g = {n: lax.dynamic_slice_in_dim(t, chip * w[n].shape[2], w[n].shape[2], axis=2)
              for n, t in zip(CONV, total[1 + len(SMALL):])}

    grad, delta, new_m, new_v = {}, {}, {}, {}
    for (n, _, _), gn in zip(BIG_LAYOUT, big_g):
        grad[n] = gn
        delta[n], new_m[n], new_v[n] = _adam_nd(gn, w[n], m[n], v[n], "adam_" + n)
    for names, gs, shapes, tag in ((SMALL, small_g, small_shapes, "small"), (CONV, conv_g, conv_shapes, "conv")):
        pk = lambda d: _pack([d[n] for n in names], F32)
        res = adamw(pk(gs), pk(w), pk(m), pk(v), "adam_" + tag)
        for out, r in zip((delta, new_m, new_v), res):
            out.update(zip(names, _unpack(r, shapes)))
        grad.update({n: gs[n] for n in names})
    outs = [loss, grad_x[None]]
    for d in (grad, delta, new_m, new_v):
        outs += [d[n] for n in WEIGHTS]
    return tuple(outs)


def kernel(x, c, ada_w, ada_b, norm_mix, w_in, gdn_conv_w, gdn_a_log, gdn_dt_bias, gdn_norm, ssd_conv_w, ssd_conv_b, ssd_a_log, ssd_dt_bias, ssd_d, ssd_norm, lru_conv_w, lru_conv_b, lru_w_a, lru_b_a, lru_w_x, lru_b_x, lru_lambda, w_branch, w_out, norm_mlp, w_up, w_down, final_norm, loss_target, m_ada_w, m_ada_b, m_norm_mix, m_w_in, m_gdn_conv_w, m_gdn_a_log, m_gdn_dt_bias, m_gdn_norm, m_ssd_conv_w, m_ssd_conv_b, m_ssd_a_log, m_ssd_dt_bias, m_ssd_d, m_ssd_norm, m_lru_conv_w, m_lru_conv_b, m_lru_w_a, m_lru_b_a, m_lru_w_x, m_lru_b_x, m_lru_lambda, m_w_branch, m_w_out, m_norm_mlp, m_w_up, m_w_down, m_final_norm, v_ada_w, v_ada_b, v_norm_mix, v_w_in, v_gdn_conv_w, v_gdn_a_log, v_gdn_dt_bias, v_gdn_norm, v_ssd_conv_w, v_ssd_conv_b, v_ssd_a_log, v_ssd_dt_bias, v_ssd_d, v_ssd_norm, v_lru_conv_w, v_lru_conv_b, v_lru_w_a, v_lru_b_a, v_lru_w_x, v_lru_b_x, v_lru_lambda, v_w_branch, v_w_out, v_norm_mlp, v_w_up, v_w_down, v_final_norm):
    given = dict(locals())
    w = {n: given[n] for n in WEIGHTS}
    m = {n: given["m_" + n] for n in WEIGHTS}
    v = {n: given["v_" + n] for n in WEIGHTS}
    return _step(w, m, v, x, c, loss_target)
```

```python
import functools
import math

import jax
import jax.numpy as jnp
from jax import lax
from jax.experimental import pallas as pl
from jax.experimental.pallas import tpu as pltpu

F32 = jnp.float32
BF16 = jnp.bfloat16

D_MODEL = 1024
DEPTH = 2
RMS_EPS = 1e-6
CHUNK = 128
GDN_HEADS = 4
SSD_HEADS = 8
LRU_C = 8.0
D_FF = 4096
N_MOD = 6
W_GDN = 2176
W_SSD = 1664
W_LRU = 512
W_GATE = 3072
ADAM_LR = 0.001
ADAM_B1 = 0.9
ADAM_B2 = 0.999
ADAM_EPS = 1e-08
ADAM_WD = 0.01
ADAM_STEP = 10
VMEM_LIMIT = 56 * 1024 * 1024
MESH = pl.DeviceIdType.MESH


def _dot(a, b, ta, tb):
    dn = (((0 if ta else 1,), (1 if tb else 0,)), ((), ()))
    return lax.dot_general(a.astype(BF16), b.astype(BF16), dn, preferred_element_type=F32)


@functools.partial(jax.custom_vjp, nondiff_argnums=(2, 3))
def mm(a, b, ta, tb):
    return _dot(a, b, ta, tb)


def _mm_fwd(a, b, ta, tb):
    return _dot(a, b, ta, tb), (a, b)


def _mm_bwd(ta, tb, res, g):
    a, b = res
    if not ta and not tb:
        return mm(g, b, False, True), mm(a, g, True, False)
    if not ta and tb:
        return mm(g, b, False, False), mm(g, a, True, False)
    assert ta and not tb
    return mm(b, g, False, True), mm(a, g, False, False)


mm.defvjp(_mm_fwd, _mm_bwd)


def _tri_apply(x, upper):
    t = x.shape[0]
    r = lax.broadcasted_iota(jnp.int32, (t, t), 0)
    c = lax.broadcasted_iota(jnp.int32, (t, t), 1)
    tri = jnp.where((r <= c) if upper else (r >= c), 1.0, 0.0).astype(BF16)
    x1 = x.astype(BF16)
    r1 = x - x1.astype(F32)
    x2 = r1.astype(BF16)
    x3 = (r1 - x2.astype(F32)).astype(BF16)
    d = lambda p: jnp.dot(tri, p, preferred_element_type=F32)
    return (d(x1) + d(x2)) + d(x3)


@jax.custom_vjp
def cumsum_rows(x):
    return _tri_apply(x, False)


cumsum_rows.defvjp(lambda x: (_tri_apply(x, False), None), lambda _, g: (_tri_apply(g, True),))


def _dot_split(a, b):
    a1, b1 = a.astype(BF16), b.astype(BF16)
    a2, b2 = (a - a1.astype(F32)).astype(BF16), (b - b1.astype(F32)).astype(BF16)
    d = lambda p, q: jnp.dot(p, q, preferred_element_type=F32)
    return d(a1, b1) + (d(a1, b2) + d(a2, b1))


def _neumann(ms):
    t = ms[0].shape[0]
    xs = [-m for m in ms]
    qs = [_dot(m, m, False, False) for m in ms]
    n = 2
    while True:
        xs = [x + q + _dot(x, q, False, False) for x, q in zip(xs, qs)]
        n *= 2
        if n >= t:
            break
        qs = [_dot(q, q, False, False) for q in qs]
    rs = [-(x + m + _dot_split(m, x)) for x, m in zip(xs, ms)]
    return [x + r + _dot(x, r, False, False) for x, r in zip(xs, rs)]


@jax.custom_vjp
def tri_solve(ms, rhss):
    return tuple(rhs + _dot(x, rhs, False, False) for x, rhs in zip(_neumann(ms), rhss))


def _tri_solve_fwd(ms, rhss):
    xs = _neumann(ms)
    sols = tuple(rhs + _dot(x, rhs, False, False) for x, rhs in zip(xs, rhss))
    return sols, (tuple(xs), sols)


def _tri_solve_bwd(res, gs):
    xs, sols = res
    d_rhss = tuple(g + _dot(x, g, True, False) for x, g in zip(xs, gs))
    return tuple(-_dot(d, sol, False, True) for d, sol in zip(d_rhss, sols)), d_rhss


tri_solve.defvjp(_tri_solve_fwd, _tri_solve_bwd)


@functools.partial(jax.custom_vjp, nondiff_argnums=(1,))
def split_cols(x, sizes):
    out, o = [], 0
    for s in sizes:
        out.append(x[:, o:o + s])
        o += s
    return tuple(out)


split_cols.defvjp(lambda x, sizes: (split_cols(x, sizes), None),
                  lambda sizes, _, g: (jnp.concatenate(list(g), axis=1),))


@functools.partial(jax.custom_vjp, nondiff_argnums=(1,))
def _last_rows(x, t):
    return x[t - 8:, :]


_last_rows.defvjp(lambda x, t: (_last_rows(x, t), None),
                  lambda t, _, g: (jnp.concatenate([jnp.zeros((t - 8, g.shape[1]), g.dtype), g], axis=0),))


def last8(x):
    return _last_rows(x, x.shape[0])


def _shifted(xp, d, t):
    return (pltpu.roll(xp, d, 0) if d else xp)[8:8 + t, :]


@jax.custom_vjp
def conv4(x, tail, w):
    t = x.shape[0]
    xp = jnp.concatenate([tail, x], axis=0)
    return sum(_shifted(xp, 3 - k, t) * w[k:k + 1, :] for k in range(4))


def _conv4_fwd(x, tail, w):
    return conv4(x, tail, w), (x, tail, w)


def _conv4_bwd(res, g):
    x, tail, w = res
    t = x.shape[0]
    xp = jnp.concatenate([tail, x], axis=0)
    zero8 = jnp.zeros((8, g.shape[1]), g.dtype)
    d_xp = jnp.zeros_like(xp)
    d_w = []
    for k in range(4):
        gk = jnp.concatenate([zero8, g * w[k:k + 1, :]], axis=0)
        d_xp = d_xp + (pltpu.roll(gk, t + 8 - (3 - k), 0) if k < 3 else gk)
        d_w.append(jnp.sum(g * _shifted(xp, 3 - k, t), axis=0, keepdims=True))
    return d_xp[8:, :], d_xp[:8, :], jnp.concatenate(d_w, axis=0)


conv4.defvjp(_conv4_fwd, _conv4_bwd)


def _sigmoid(x):
    return 0.5 * (jnp.tanh(0.5 * x) + 1.0)


def _silu(x):
    return x * _sigmoid(x)


def _softplus(x):
    ax = jnp.where(x > 0, x, -x)
    return jnp.where(x > 0, x, 0.0) + jnp.log(1.0 + jnp.exp(-ax))


def _gelu(x):
    return 0.5 * x * (1.0 + jnp.tanh(math.sqrt(2.0 / math.pi) * (x + 0.044715 * (x * x * x))))


def _expm1(x):
    series = x * (1.0 + x * (0.5 + x * (1.0 / 6.0 + x * (1.0 / 24.0))))
    return jnp.where(jnp.abs(x) < 0.03, series, jnp.exp(x) - 1.0)


def _rms(x, w):
    return x * lax.rsqrt(jnp.mean(x * x, axis=-1, keepdims=True) + RMS_EPS) * w


def _lane_pick(x, j):
    lane = lax.broadcasted_iota(jnp.int32, (1, x.shape[1]), 1)
    return jnp.sum(jnp.where(lane == j, x, 0.0), axis=1, keepdims=True)


def _row_pick(x, j):
    row = lax.broadcasted_iota(jnp.int32, (x.shape[0], 1), 0)
    return jnp.sum(jnp.where(row == j, x, 0.0), axis=0, keepdims=True)


def gdn_fn(carry, seq, params):
    *states, tail = carry
    (tile,) = seq
    conv_w, alog_row, dtb_row, norm_w = params
    t = tile.shape[0]
    qkv_raw, z, sm = split_cols(tile, (1536, 512, 128))
    qkv = _silu(conv4(qkv_raw, tail, conv_w))
    parts = split_cols(qkv, (128,) * 12)
    zs = split_cols(z, (128,) * 4)
    lane = lax.broadcasted_iota(jnp.int32, (1, 128), 1)
    beta_all = _sigmoid(sm)
    g_all = jnp.where((lane >= 4) & (lane < 8), -jnp.exp(alog_row) * _softplus(sm + dtb_row), 0.0)
    gc_all = cumsum_rows(g_all)
    gr_all = gc_all.T
    gl_all = _row_pick(gc_all, t - 1)
    r = lax.broadcasted_iota(jnp.int32, (t, t), 0)
    c = lax.broadcasted_iota(jnp.int32, (t, t), 1)
    heads = range(GDN_HEADS)
    l2 = lambda a: a * lax.rsqrt(jnp.sum(a * a, axis=-1, keepdims=True) + RMS_EPS)
    qn = [l2(parts[h]) * (128.0 ** -0.5) for h in heads]
    kn = [l2(parts[4 + h]) for h in heads]
    beta = [_lane_pick(beta_all, h) for h in heads]
    gc = [_lane_pick(gc_all, 4 + h) for h in heads]
    gl = [_lane_pick(gl_all, 4 + h) for h in heads]
    decay = [jnp.exp(jnp.where(r >= c, gc[h] - _row_pick(gr_all, 4 + h), -1e30)) for h in heads]
    kk = [mm(kn[h], kn[h], False, True) for h in heads]
    qk = [mm(qn[h], kn[h], False, True) for h in heads]
    m = tuple(jnp.where(r > c, beta[h] * kk[h] * decay[h], 0.0) for h in heads)
    eg = [jnp.exp(gc[h]) for h in heads]
    rhs = tuple(jnp.concatenate([beta[h] * parts[8 + h], (beta[h] * eg[h]) * kn[h]], axis=1) for h in heads)
    uw = [split_cols(s, (128, 128)) for s in tri_solve(m, rhs)]
    ws = [mm(uw[h][1], states[h], False, False) for h in heads]
    qs = [mm(qn[h] * eg[h], states[h], False, False) for h in heads]
    v_new = [uw[h][0] - ws[h] for h in heads]
    o = [qs[h] + mm(qk[h] * decay[h], v_new[h], False, False) for h in heads]
    kv = [mm(kn[h] * jnp.exp(gl[h] - gc[h]), v_new[h], True, False) for h in heads]
    new_states = [states[h] * jnp.exp(gl[h]) + kv[h] for h in heads]
    outs = [_rms(o[h], norm_w) * _silu(zs[h]) for h in heads]
    return (*new_states, last8(qkv_raw)), (jnp.concatenate(outs, axis=1),)


def ssd_fn(carry, seq, params):
    *states, tail = carry
    (tile,) = seq
    conv_w, conv_b, alog_row, dtb_row, d_row, norm_w = params
    t = tile.shape[0]
    xbc_raw, z, sm = split_cols(tile, (1024, 512, 128))
    xbc = _silu(conv4(xbc_raw, tail, conv_w) + conv_b)
    x0, x1, x2, x3, b0, b1, c0, c1 = split_cols(xbc, (128,) * 8)
    xs, bs, cs = (x0, x1, x2, x3), (b0, b1), (c0, c1)
    ds = split_cols(d_row, (128,) * 4)
    lane = lax.broadcasted_iota(jnp.int32, (1, 128), 1)
    sub = lax.broadcasted_iota(jnp.int32, (128, 1), 0)
    low = lane < 64
    dt_all = jnp.where(lane < SSD_HEADS, _softplus(sm + dtb_row), 0.0)
    ac_all = cumsum_rows(dt_all * (-jnp.exp(alog_row)))
    ar_all = ac_all.T
    al_all = _row_pick(ac_all, t - 1)
    r = lax.broadcasted_iota(jnp.int32, (t, t), 0)
    c = lax.broadcasted_iota(jnp.int32, (t, t), 1)
    pairs, heads = range(4), range(SSD_HEADS)
    col = [_lane_pick(ac_all, h) for h in heads]
    last = [_lane_pick(al_all, h) for h in heads]
    dt = [_lane_pick(dt_all, h) for h in heads]
    lm = [jnp.exp(jnp.where(r >= c, col[h] - _row_pick(ar_all, h), -1e30)) for h in heads]
    cb = [mm(cs[g], bs[g], False, True) for g in range(2)]
    both = lambda a, b: jnp.where(low, a, b)
    xdt = [xs[p] * both(dt[2 * p], dt[2 * p + 1]) for p in pairs]
    y_off = [mm(cs[p // 2], states[p], False, True) for p in pairs]
    y_lo = [mm(cb[p // 2] * lm[2 * p], jnp.where(low, xdt[p], 0.0), False, False) for p in pairs]
    y_hi = [mm(cb[p // 2] * lm[2 * p + 1], jnp.where(low, 0.0, xdt[p]), False, False) for p in pairs]
    st = [mm(xdt[p] * both(jnp.exp(last[2 * p] - col[2 * p]), jnp.exp(last[2 * p + 1] - col[2 * p + 1])),
             bs[p // 2], True, False) for p in pairs]
    ys = [ds[p] * xs[p] + y_lo[p] + y_hi[p] + y_off[p] * both(jnp.exp(col[2 * p]), jnp.exp(col[2 * p + 1]))
          for p in pairs]
    new_states = [states[p] * jnp.where(sub < 64, jnp.exp(last[2 * p]), jnp.exp(last[2 * p + 1])) + st[p]
                  for p in pairs]
    gz = jnp.concatenate(ys, axis=1) * _silu(z)
    g0, g1 = split_cols(gz, (256, 256))
    n0, n1 = split_cols(norm_w, (256, 256))
    out = jnp.concatenate([_rms(g0, n0), _rms(g1, n1)], axis=1)
    return (*new_states, last8(xbc_raw)), (out,)


def lru_in_fn(carry, seq, params):
    (tail,) = carry
    (x,) = seq
    conv_w, conv_b, w_a, b_a, w_x, b_x, lam = params
    xc = conv4(x, tail, conv_w) + conv_b
    r = _sigmoid(mm(xc, w_a, False, False) + b_a)
    i = _sigmoid(mm(xc, w_x, False, False) + b_x)
    log_a = -LRU_C * r * _softplus(-lam)
    u = jnp.sqrt(-_expm1(2.0 * log_a)) * (i * xc)
    return (last8(x),), (jnp.exp(log_a), u)


def lru_out_fn(carry, seq, params):
    hs, gate = seq
    return (), (hs * _gelu(gate),)


def merge_fn(carry, seq, params):
    ya, yb, yc, gl = seq
    g = split_cols(_sigmoid(gl), (D_MODEL,) * 3)
    merged = sum(g[r] * mm(y, params[r], False, False) for r, y in enumerate((ya, yb, yc)))
    return (), (merged,)


def _adaln(x, w, sc, sh):
    return _rms(x, w) * (1.0 + sc) + sh


def norm1_fn(carry, seq, params):
    (x,) = seq
    return (), (_adaln(x, *params), x)


def resid_norm_fn(carry, seq, params):
    x, mix = seq
    gt, w, sc, sh = params
    x1 = x + gt * mix
    return (), (_adaln(x1, w, sc, sh), x1)


def act_fn(carry, seq, params):
    (up,) = seq
    r = jnp.where(up > 0, up, 0.0)
    return (), (r * r,)


def resid_fn(carry, seq, params):
    x, dn = seq
    (gt,) = params
    return (), (x + gt * dn,)


def silu_fn(carry, seq, params):
    return (), (_silu(seq[0]),)


def _full_spec(a):
    nd = a.ndim
    return pl.BlockSpec(a.shape, lambda i: (0,) * nd)


def _cparams(*sem):
    return pltpu.CompilerParams(dimension_semantics=sem, vmem_limit_bytes=VMEM_LIMIT)


def scan_fwd(fn, name, tile, seqs, params, carry_shapes, outs, save_carry=False):
    rows = seqs[0].shape[0]
    tile = min(tile, rows)
    n = rows // tile
    ns, npar, nc, no = len(seqs), len(params), len(carry_shapes), len(outs)

    def body(*refs):
        seq_refs, refs = refs[:ns], refs[ns:]
        par_refs, refs = refs[:npar], refs[npar:]
        out_refs, refs = refs[:no], refs[no:]
        save_refs, refs = (refs[:nc], refs[nc:]) if save_carry else ((), refs)
        carry_refs = refs

        @pl.when(pl.program_id(0) == 0)
        def _():
            for cr in carry_refs:
                cr[...] = jnp.zeros_like(cr)

        carry = tuple(cr[...] for cr in carry_refs)
        for sr, cv in zip(save_refs, carry):
            sr[0] = cv
        new_carry, res = fn(carry, tuple(r[...].astype(F32) for r in seq_refs),
                            tuple(r[...].astype(F32) for r in par_refs))
        for r, v in zip(out_refs, res):
            r[...] = v.astype(r.dtype)
        for cr, v in zip(carry_refs, new_carry):
            cr[...] = v

    out_shape = [jax.ShapeDtypeStruct((rows, w), dt) for w, dt in outs]
    out_specs = [pl.BlockSpec((tile, w), lambda i: (i, 0)) for w, _ in outs]
    if save_carry:
        out_shape += [jax.ShapeDtypeStruct((n, *s), F32) for s in carry_shapes]
        out_specs += [pl.BlockSpec((1, *s), lambda i: (i, 0, 0)) for s in carry_shapes]
    res = pl.pallas_call(
        body, name=name, grid=(n,),
        in_specs=[pl.BlockSpec((tile, s.shape[1]), lambda i: (i, 0)) for s in seqs] + [_full_spec(p) for p in params],
        out_specs=out_specs, out_shape=out_shape,
        scratch_shapes=[pltpu.VMEM(s, F32) for s in carry_shapes],
        compiler_params=_cparams("arbitrary"),
    )(*seqs, *params)
    return res[:no], res[no:]


def scan_bwd(fn, name, tile, seqs, params, saved, douts, n_dseq, n_dpar, dseq_dtypes=None):
    dseq_dtypes = dseq_dtypes or [F32] * n_dseq
    rows = seqs[0].shape[0]
    tile = min(tile, rows)
    n = rows // tile
    ns, npar, nc, no = len(seqs), len(params), len(saved), len(douts)

    def body(*refs):
        seq_refs, refs = refs[:ns], refs[ns:]
        par_refs, refs = refs[:npar], refs[npar:]
        save_refs, refs = refs[:nc], refs[nc:]
        dout_refs, refs = refs[:no], refs[no:]
        dseq_refs, refs = refs[:n_dseq], refs[n_dseq:]
        dpar_refs, refs = refs[:n_dpar], refs[n_dpar:]
        dcarry_refs = refs

        @pl.when(pl.program_id(0) == 0)
        def _():
            for r in (*dpar_refs, *dcarry_refs):
                r[...] = jnp.zeros_like(r)

        carry = tuple(r[0] for r in save_refs)
        seq = tuple(r[...].astype(F32) for r in seq_refs)
        par = tuple(r[...].astype(F32) for r in par_refs)

        def f(carry, dseq, dpar):
            return fn(carry, (*dseq, *seq[n_dseq:]), (*dpar, *par[n_dpar:]))

        _, vjp = jax.vjp(f, carry, seq[:n_dseq], par[:n_dpar])
        d_carry, d_seq, d_par = vjp((tuple(r[...] for r in dcarry_refs),
                                     tuple(r[...].astype(F32) for r in dout_refs)))
        for r, v in zip(dseq_refs, d_seq):
            r[...] = v.astype(r.dtype)
        for r, v in zip(dpar_refs, d_par):
            r[...] += v
        for r, v in zip(dcarry_refs, d_carry):
            r[...] = v

    rev = lambda i: (n - 1 - i, 0)
    res = pl.pallas_call(
        body, name=name, grid=(n,),
        in_specs=([pl.BlockSpec((tile, s.shape[1]), rev) for s in seqs] + [_full_spec(p) for p in params]
                  + [pl.BlockSpec((1, *s.shape[1:]), lambda i: (n - 1 - i, 0, 0)) for s in saved]
                  + [pl.BlockSpec((tile, d.shape[1]), rev) for d in douts]),
        out_specs=([pl.BlockSpec((tile, s.shape[1]), rev) for s in seqs[:n_dseq]]
                   + [_full_spec(p) for p in params[:n_dpar]]),
        out_shape=([jax.ShapeDtypeStruct((rows, s.shape[1]), dt) for s, dt in zip(seqs[:n_dseq], dseq_dtypes)]
                   + [jax.ShapeDtypeStruct(p.shape, F32) for p in params[:n_dpar]]),
        scratch_shapes=[pltpu.VMEM(s.shape[1:], F32) for s in saved],
        compiler_params=_cparams("arbitrary"),
    )(*seqs, *params, *saved, *douts)
    return res[:n_dseq], res[n_dseq:]


def _tile_of(dim, pref):
    if dim <= pref:
        return dim
    best = max((t for t in range(128, pref + 1, 128) if dim % t == 0), default=None)
    if best is None or (best < 512 and dim <= 2304):
        return dim
    return best


def _row_tile(rows, pref):
    if rows <= pref:
        return rows
    return max(t for t in range(8, pref + 1, 8) if rows % t == 0)


def matmul(a, b, name, ta=False, tb=False, out_dtype=F32, add=None, bias=None, tm=1024, tn=1024, tk=1024):
    m, k = (a.shape[1], a.shape[0]) if ta else a.shape
    n = b.shape[0] if tb else b.shape[1]
    assert k == (b.shape[1] if tb else b.shape[0])
    tm, tn, tk = _tile_of(m, tm), _tile_of(n, tn), _tile_of(k, tk)
    nm, nn, nk = m // tm, n // tn, k // tk
    assert nk == 1 or out_dtype == F32
    dn = (((0 if ta else 1,), (1 if tb else 0,)), ((), ()))
    has_add, has_bias = add is not None, bias is not None
    n_inner = a.size * a.dtype.itemsize * (nn - 1) >= b.size * b.dtype.itemsize * (nm - 1)
    ij = (lambda g0, g1: (g0, g1)) if n_inner else (lambda g0, g1: (g1, g0))

    def body(*refs):
        a_ref, b_ref, refs = refs[0], refs[1], refs[2:]
        add_ref, refs = (refs[0], refs[1:]) if has_add else (None, refs)
        bias_ref, refs = (refs[0], refs[1:]) if has_bias else (None, refs)
        (o_ref,) = refs
        r = lax.dot_general(a_ref[...].astype(BF16), b_ref[...].astype(BF16), dn, preferred_element_type=F32)

        def first():
            v = r
            if has_add:
                v = v + add_ref[...]
            if has_bias:
                v = v + bias_ref[...]
            o_ref[...] = v.astype(o_ref.dtype)

        if nk == 1:
            first()
        else:
            pl.when(pl.program_id(2) == 0)(first)

            @pl.when(pl.program_id(2) > 0)
            def _():
                o_ref[...] += r

    def spec(shape, fn):
        return pl.BlockSpec(shape, lambda g0, g1, l: fn(*ij(g0, g1), l))

    a_spec = spec((tk, tm), lambda i, j, l: (l, i)) if ta else spec((tm, tk), lambda i, j, l: (i, l))
    b_spec = spec((tn, tk), lambda i, j, l: (j, l)) if tb else spec((tk, tn), lambda i, j, l: (l, j))
    o_spec = spec((tm, tn), lambda i, j, l: (i, j))
    in_specs, args = [a_spec, b_spec], [a, b]
    if has_add:
        in_specs.append(o_spec)
        args.append(add)
    if has_bias:
        in_specs.append(spec((1, tn), lambda i, j, l: (0, j)))
        args.append(bias)
    return pl.pallas_call(
        body, name=name, grid=(nm, nn, nk) if n_inner else (nn, nm, nk), in_specs=in_specs, out_specs=o_spec,
        out_shape=jax.ShapeDtypeStruct((m, n), out_dtype),
        compiler_params=_cparams("parallel", "parallel", "arbitrary"),
    )(*args)


LIN_TILE = 512


def linscan_fwd(a, u, name):
    rows, w = a.shape
    tile = min(LIN_TILE, rows)

    def body(a_ref, u_ref, h_ref, hc):
        @pl.when(pl.program_id(0) == 0)
        def _():
            hc[...] = jnp.zeros_like(hc)

        def step(t, h):
            h = a_ref[pl.ds(t, 1), :] * h + u_ref[pl.ds(t, 1), :]
            h_ref[pl.ds(t, 1), :] = h
            return h

        hc[...] = lax.fori_loop(0, tile, step, hc[...], unroll=8)

    spec = pl.BlockSpec((tile, w), lambda i: (i, 0))
    return pl.pallas_call(
        body, name=name, grid=(rows // tile,), in_specs=[spec, spec], out_specs=spec,
        out_shape=jax.ShapeDtypeStruct((rows, w), F32), scratch_shapes=[pltpu.VMEM((1, w), F32)],
        compiler_params=_cparams("arbitrary"),
    )(a, u)


def linscan_bwd(a, hs, dh, name):
    rows, w = a.shape
    tile = min(LIN_TILE, rows)
    n = rows // tile
    per = tile // 8

    def body(a_ref, h_ref, hprev_ref, dh_ref, da_ref, du_ref, cc):
        i = pl.program_id(0)

        @pl.when(i == 0)
        def _():
            cc[...] = jnp.zeros_like(cc)

        def step(s, c):
            t = tile - 1 - s
            g = dh_ref[pl.ds(t, 1), :] + c
            du_ref[pl.ds(t, 1), :] = g
            da_ref[pl.ds(t, 1), :] = g * h_ref[pl.ds(t - 1, 1), :]
            return a_ref[pl.ds(t, 1), :] * g

        c = lax.fori_loop(0, tile - 1, step, cc[...], unroll=8)
        g = dh_ref[0:1, :] + c
        du_ref[0:1, :] = g
        da_ref[0:1, :] = g * jnp.where(i == n - 1, 0.0, hprev_ref[7:8, :])
        cc[...] = a_ref[0:1, :] * g

    rev = pl.BlockSpec((tile, w), lambda i: (n - 1 - i, 0))
    prev = pl.BlockSpec((8, w), lambda i: (jnp.maximum((n - 1 - i) * per - 1, 0), 0))
    return pl.pallas_call(
        body, name=name, grid=(n,), in_specs=[rev, rev, prev, rev], out_specs=[rev, rev],
        out_shape=[jax.ShapeDtypeStruct((rows, w), F32)] * 2, scratch_shapes=[pltpu.VMEM((1, w), F32)],
        compiler_params=_cparams("arbitrary"),
    )(a, hs, hs, dh)


def loss_head(x, target, w, name):
    rows, d = x.shape
    tile = min(512, rows)

    def body(x_ref, t_ref, w_ref, loss_ref, dx_ref, dw_ref):
        @pl.when(pl.program_id(0) == 0)
        def _():
            loss_ref[...] = jnp.zeros_like(loss_ref)
            dw_ref[...] = jnp.zeros_like(dw_ref)

        tv = t_ref[...]

        def f(xv, wv):
            e = _rms(xv, wv) - tv
            return 0.5 * jnp.sum(jnp.mean(e * e, axis=-1, keepdims=True), axis=0, keepdims=True)

        val, vjp = jax.vjp(f, x_ref[...], w_ref[...])
        dxv, dwv = vjp(jnp.ones((1, 1), F32))
        loss_ref[...] += jnp.broadcast_to(val, loss_ref.shape)
        dx_ref[...] = dxv
        dw_ref[...] += dwv

    spec = pl.BlockSpec((tile, d), lambda i: (i, 0))
    return pl.pallas_call(
        body, name=name, grid=(rows // tile,), in_specs=[spec, spec, _full_spec(w)],
        out_specs=[pl.BlockSpec((8, 128), lambda i: (0, 0)), spec, _full_spec(w)],
        out_shape=[jax.ShapeDtypeStruct((8, 128), F32), jax.ShapeDtypeStruct((rows, d), F32),
                   jax.ShapeDtypeStruct(w.shape, F32)],
        compiler_params=_cparams("arbitrary"),
    )(x, target, w)


def adamw(g, w, m, v, name):
    rows, cols = g.shape
    tile = _row_tile(rows, 256)

    def body(g_ref, w_ref, m_ref, v_ref, d_ref, nm_ref, nv_ref):
        gv = g_ref[...]
        nm = ADAM_B1 * m_ref[...] + (1.0 - ADAM_B1) * gv
        nv = ADAM_B2 * v_ref[...] + (1.0 - ADAM_B2) * (gv * gv)
        m_hat = nm / (1.0 - ADAM_B1 ** ADAM_STEP)
        v_hat = nv / (1.0 - ADAM_B2 ** ADAM_STEP)
        d_ref[...] = -ADAM_LR * (m_hat / (jnp.sqrt(v_hat) + ADAM_EPS) + ADAM_WD * w_ref[...])
        nm_ref[...] = nm
        nv_ref[...] = nv

    spec = pl.BlockSpec((tile, cols), lambda i: (i, 0))
    return pl.pallas_call(
        body, name=name, grid=(rows // tile,), in_specs=[spec] * 4, out_specs=[spec] * 3,
        out_shape=[jax.ShapeDtypeStruct((rows, cols), F32)] * 3,
        compiler_params=_cparams("parallel"),
    )(g, w, m, v)


def add_cast(g0, g1, sib, place, name):
    shape = sib.shape
    g0, g1, sib = (a.reshape(-1, shape[-1]) for a in (g0, g1, sib))
    rows, cols = sib.shape
    tile = _row_tile(rows, max(8, min(256, (512 * 1024) // cols)))

    def body(k_ref, g0_ref, g1_ref, s_ref, o_ref, ob_ref):
        s = jnp.where(k_ref[1] == 0, g0_ref[...], g1_ref[...]) + s_ref[...]
        o_ref[...] = s
        ob_ref[...] = s.astype(BF16)

    spec = pl.BlockSpec((tile, cols), lambda i, k: (i, 0))
    s, sb = pl.pallas_call(
        body, name=name,
        grid_spec=pltpu.PrefetchScalarGridSpec(
            num_scalar_prefetch=1, grid=(rows // tile,),
            in_specs=[pl.BlockSpec((tile, cols), lambda i, k: (i * (1 - k[1]), 0)),
                      pl.BlockSpec((tile, cols), lambda i, k: (i * k[1], 0)), spec],
            out_specs=[spec, spec]),
        out_shape=[jax.ShapeDtypeStruct((rows, cols), F32), jax.ShapeDtypeStruct((rows, cols), BF16)],
        compiler_params=_cparams("arbitrary"),
    )(place, g0, g1, sib)
    return s.reshape(shape), sb.reshape(shape)


def sum4(own, recv, by_cols, place, name):
    _, r, c = recv.shape
    tile = _row_tile(r, 256)
    nt = r // tile
    own_map = (lambda i, k: (i, k[0])) if by_cols else (lambda i, k: (k[0] * nt + i, 0))

    def body(k_ref, own_ref, recv_ref, o_ref):
        o_ref[...] = ((own_ref[...] + recv_ref[0].astype(F32)) + recv_ref[1].astype(F32)) + recv_ref[2].astype(F32)

    return pl.pallas_call(
        body, name=name,
        grid_spec=pltpu.PrefetchScalarGridSpec(
            num_scalar_prefetch=1, grid=(nt,),
            in_specs=[pl.BlockSpec((tile, c), own_map), pl.BlockSpec((3, tile, c), lambda i, k: (0, i, 0))],
            out_specs=pl.BlockSpec((None, tile, c), lambda i, k: (k[1], i, 0))),
        out_shape=jax.ShapeDtypeStruct((2, r, c), F32),
        compiler_params=_cparams("arbitrary"),
    )(place, own, recv)


def add8(parts, name):
    _, rows, cols = parts.shape

    def body(p_ref, o_ref):
        acc = p_ref[0]
        for k in range(1, 8):
            acc = acc + p_ref[k]
        o_ref[...] = acc

    return pl.pallas_call(
        body, name=name, in_specs=[pl.BlockSpec(memory_space=pltpu.VMEM)],
        out_specs=pl.BlockSpec(memory_space=pltpu.VMEM),
        out_shape=jax.ShapeDtypeStruct((rows, cols), F32),
    )(parts)


def _place():
    return lax.axis_index("x"), lax.axis_index("y"), lax.axis_index("c")


def _other_chips(x, y):
    return [(1 - x, y), (x, 1 - y), (1 - x, 1 - y)]


_ANY = pl.BlockSpec(memory_space=pl.ANY)


BIG_LAYOUT = (("ada_w", "col", (1024, 6144)), ("w_in", "chip", (4, 1024, 1924)), ("w_branch", "col", (3, 512, 1024)),
              ("w_out", "row", (1024, 1024)), ("w_up", "col", (1024, 4096)), ("w_down", "row", (4096, 1024)))
N_BIG = len(BIG_LAYOUT)


def _local_shape(kind, full):
    if kind == "col":
        return (*full[:-1], full[-1] // 4)
    if kind == "row":
        return (full[0] // 4, *full[1:])
    return full[1:]


def _window(ref, kind, k, local):
    if kind == "chip":
        return ref.at[k]
    if kind == "row":
        return ref.at[pl.ds(pl.multiple_of(k * local[0], 8), local[0])]
    idx = (slice(None),) * (len(local) - 1) + (pl.ds(pl.multiple_of(k * local[-1], 128), local[-1]),)
    return ref.at[idx]


def _dma_call(body, name, n_in, out_shape, sems, aliases=None):
    return pl.pallas_call(
        body, name=name, in_specs=[_ANY] * n_in, out_specs=[_ANY] * len(out_shape), out_shape=out_shape,
        scratch_shapes=[pltpu.SemaphoreType.DMA((n,)) for n in sems],
        input_output_aliases=aliases or {},
        compiler_params=pltpu.CompilerParams(has_side_effects=True))


def _remote(src, dst, send_sem, recv_sem, to):
    return pltpu.make_async_remote_copy(src_ref=src, dst_ref=dst, send_sem=send_sem, recv_sem=recv_sem,
                                        device_id=to, device_id_type=MESH)


def gather_big(shards, name):
    locals_ = [_local_shape(kind, full) for _, kind, full in BIG_LAYOUT]

    def body(*refs):
        sh, refs = refs[:N_BIG], refs[N_BIG:]
        full, refs = (refs[:N_BIG], refs[N_BIG:2 * N_BIG]), refs[2 * N_BIG:]
        send_sems, recv_sems, local_sems, pass_send, pass_recv = refs
        x, y, c = _place()
        me = 2 * x + y
        chips = _other_chips(x, y)
        for cc in (0, 1):
            @pl.when(c == cc)
            def _():
                win = lambda n, k: _window(full[cc][n], BIG_LAYOUT[n][1], k, locals_[n])
                mine, sends = [], []
                for n in range(N_BIG):
                    mine.append(pltpu.make_async_copy(sh[n].at[cc], win(n, me), local_sems.at[n]))
                    mine[n].start()
                    for j, chip in enumerate(chips):
                        sends.append(_remote(sh[n].at[cc], win(n, me), send_sems.at[3 * n + j],
                                             recv_sems.at[3 * n + j], (chip[0], chip[1], c)))
                        sends[-1].start()
                for n in range(N_BIG):
                    for j, chip in enumerate(chips):
                        _remote(sh[n].at[cc], win(n, 2 * chip[0] + chip[1]), send_sems.at[3 * n + j],
                                recv_sems.at[3 * n + j], (chip[0], chip[1], c)).wait_recv()
                    mine[n].wait()
                    sends.append(_remote(full[cc][n], full[cc][n], pass_send.at[n], pass_recv.at[n], (x, y, 1 - c)))
                    sends[-1].start()
                for n in range(N_BIG):
                    _remote(full[1 - cc][n], full[1 - cc][n], pass_send.at[n], pass_recv.at[n],
                            (x, y, 1 - c)).wait_recv()
                for cp in sends:
                    cp.wait_send()

    out_shape = [jax.ShapeDtypeStruct(full, BF16) for _, _, full in BIG_LAYOUT] * 2
    res = _dma_call(body, name, N_BIG, out_shape, (3 * N_BIG, 3 * N_BIG, N_BIG, N_BIG, N_BIG))(*shards)
    return res[:N_BIG], res[N_BIG:]


def reduce_d2d(g0, g1, name):
    def body(*refs):
        g, refs = (refs[:N_BIG], refs[N_BIG:2 * N_BIG]), refs[2 * N_BIG:]
        sib, (send_sems, recv_sems) = refs[:N_BIG], refs[N_BIG:]
        x, y, c = _place()
        for cc in (0, 1):
            @pl.when(c == cc)
            def _():
                sends = [_remote(g[1 - cc][n], sib[n], send_sems.at[n], recv_sems.at[n], (x, y, 1 - c))
                         for n in range(N_BIG)]
                for cp in sends:
                    cp.start()
                for cp in sends:
                    cp.wait_recv()
                for cp in sends:
                    cp.wait_send()

    out_shape = [jax.ShapeDtypeStruct(a.shape, a.dtype) for a in g0]
    return _dma_call(body, name, 2 * N_BIG, out_shape, (N_BIG, N_BIG))(*g0, *g1)


def reduce_ici(sums, name):
    locals_ = [_local_shape(kind, full) for _, kind, full in BIG_LAYOUT]

    def body(*refs):
        src, recv, (send_sems, recv_sems) = refs[:N_BIG], refs[N_BIG:2 * N_BIG], refs[2 * N_BIG:]
        x, y, c = _place()
        chips = _other_chips(x, y)
        copies = []
        for n in range(N_BIG):
            for j, chip in enumerate(chips):
                cp = _remote(_window(src[n], BIG_LAYOUT[n][1], 2 * chip[0] + chip[1], locals_[n]), recv[n].at[j],
                             send_sems.at[3 * n + j], recv_sems.at[3 * n + j], (chip[0], chip[1], c))
                cp.start()
                copies.append(cp)
        for cp in copies:
            cp.wait_recv()
        for cp in copies:
            cp.wait_send()

    out_shape = [jax.ShapeDtypeStruct((3, *ls), a.dtype) for ls, a in zip(locals_, sums)]
    return _dma_call(body, name, N_BIG, out_shape, (3 * N_BIG, 3 * N_BIG))(*sums)


def share_d2d(finals, name):
    def body(*refs):
        out, (send_sems, recv_sems) = refs[N_BIG:2 * N_BIG], refs[2 * N_BIG:]
        x, y, c = _place()
        sends = [_remote(out[n].at[c], out[n].at[c], send_sems.at[n], recv_sems.at[n], (x, y, 1 - c))
                 for n in range(N_BIG)]
        for cp in sends:
            cp.start()
        for n in range(N_BIG):
            _remote(out[n].at[c], out[n].at[1 - c], send_sems.at[n], recv_sems.at[n], (x, y, 1 - c)).wait_recv()
        for cp in sends:
            cp.wait_send()

    out_shape = [jax.ShapeDtypeStruct(a.shape, a.dtype) for a in finals]
    return _dma_call(body, name, N_BIG, out_shape, (N_BIG, N_BIG), aliases={n: n for n in range(N_BIG)})(*finals)


def allgather8(block, name):
    m_per, n = block.shape

    def body(x_ref, out_ref, send_sems, recv_sems, local_sem):
        x, y, c = _place()
        me, sibling = (x, y, c), (x, y, 1 - c)
        chips = _other_chips(x, y)

        def rows(px, py, pc):
            return out_ref.at[pl.ds((4 * px + 2 * py + pc) * m_per, m_per), :]

        def copy(k, blk, to, src=None):
            return pltpu.make_async_remote_copy(
                src_ref=rows(*blk) if src is None else src, dst_ref=rows(*blk), send_sem=send_sems.at[k],
                recv_sem=recv_sems.at[k], device_id=to, device_id_type=MESH)

        mine = pltpu.make_async_copy(x_ref, rows(*me), local_sem)
        mine.start()
        first = [copy(0, me, sibling, src=x_ref)]
        first += [copy(1 + j, me, (*chip, c), src=x_ref) for j, chip in enumerate(chips)]
        for cp in first:
            cp.start()
        passed = [copy(4 + j, (*chip, c), sibling) for j, chip in enumerate(chips)]
        for j, chip in enumerate(chips):
            copy(1 + j, (*chip, c), me).wait_recv()
            passed[j].start()
        copy(0, sibling, me).wait_recv()
        for j, chip in enumerate(chips):
            copy(4 + j, (*chip, 1 - c), me).wait_recv()
        for cp in first + passed:
            cp.wait_send()
        mine.wait()

    return pl.pallas_call(
        body, name=name, in_specs=[pl.BlockSpec(memory_space=pltpu.VMEM)],
        out_specs=pl.BlockSpec(memory_space=pltpu.VMEM),
        out_shape=jax.ShapeDtypeStruct((8 * m_per, n), block.dtype),
        scratch_shapes=[pltpu.SemaphoreType.DMA((7,)), pltpu.SemaphoreType.DMA((7,)), pltpu.SemaphoreType.DMA],
    )(block)


CONV = ("gdn_conv_w", "ssd_conv_w", "lru_conv_w")
SMALL = ("ada_b", "norm_mix", "gdn_a_log", "gdn_dt_bias", "gdn_norm", "ssd_conv_b", "ssd_a_log", "ssd_dt_bias",
         "ssd_d", "ssd_norm", "lru_conv_b", "lru_w_a", "lru_b_a", "lru_w_x", "lru_b_x", "lru_lambda", "norm_mlp",
         "final_norm")
WEIGHTS = ("ada_w", "ada_b", "norm_mix", "w_in", "gdn_conv_w", "gdn_a_log", "gdn_dt_bias", "gdn_norm", "ssd_conv_w",
           "ssd_conv_b", "ssd_a_log", "ssd_dt_bias", "ssd_d", "ssd_norm", "lru_conv_w", "lru_conv_b", "lru_w_a",
           "lru_b_a", "lru_w_x", "lru_b_x", "lru_lambda", "w_branch", "w_out", "norm_mlp", "w_up", "w_down",
           "final_norm")
PACK_COLS = 1024


def _pack(arrays, dtype):
    flat = jnp.concatenate([a.reshape(-1).astype(dtype) for a in arrays])
    pad = -flat.shape[0] % (8 * PACK_COLS)
    if pad:
        flat = jnp.concatenate([flat, jnp.zeros((pad,), dtype)])
    return flat.reshape(-1, PACK_COLS)


def _unpack(pack, shapes):
    flat = pack.reshape(-1)
    out, o = [], 0
    for s in shapes:
        n = math.prod(s)
        out.append(flat[o:o + n].reshape(s))
        o += n
    return out


def _split_w_in(w4):
    w = jnp.concatenate([w4[k] for k in range(4)], axis=1)
    pad = jnp.zeros((w.shape[0], 120), w.dtype)
    gdn = jnp.concatenate([w[:, 0:2056], pad], axis=1)
    ssd = jnp.concatenate([w[:, 2056:2568], w[:, 3080:3592], w[:, 2568:3080], w[:, 3592:3600], pad], axis=1)
    return gdn, ssd, w[:, 3600:4112], w[:, 4112:4624], w[:, 4624:7696]


def _join_w_in(gdn, ssd, lx, lg, gate):
    w = jnp.concatenate([gdn[:, 0:2056], ssd[:, 0:512], ssd[:, 1024:1536], ssd[:, 512:1024], ssd[:, 1536:1544],
                         lx, lg, gate], axis=1)
    return jnp.stack([w[:, k * 1924:(k + 1) * 1924] for k in range(4)])


def _lanes(v, at, width=128):
    return jnp.zeros((1, width), F32).at[0, at:at + v.shape[0]].set(v)


def _block_diag(w):
    return (jnp.eye(8, dtype=w.dtype)[:, None, :, None] * w[:, :, None, :]).reshape(512, 512)


def _diag_blocks(w):
    return jnp.stack([w[n * 64:(n + 1) * 64, n * 64:(n + 1) * 64] for n in range(8)])


TOK_TILE = 512
WIDE_TILE = 256


def _layer_params(p, big, l):
    row = lambda v: v.reshape(1, -1)
    b = dict(zip((n for n, _, _ in BIG_LAYOUT), big))
    gdn = (p["gdn_conv_w"][l], _lanes(p["gdn_a_log"][l], 4), _lanes(p["gdn_dt_bias"][l], 4), row(p["gdn_norm"][l]))
    ssd = (p["ssd_conv_w"][l], row(p["ssd_conv_b"][l]), _lanes(p["ssd_a_log"][l], 0), _lanes(p["ssd_dt_bias"][l], 0),
           row(jnp.repeat(p["ssd_d"][l], 64)), row(p["ssd_norm"][l]))
    lru = (p["lru_conv_w"][l], row(p["lru_conv_b"][l]), _block_diag(p["lru_w_a"][l]), row(p["lru_b_a"][l]),
           _block_diag(p["lru_w_x"][l]), row(p["lru_b_x"][l]), row(p["lru_lambda"][l]))
    return dict(gdn=gdn, ssd=ssd, lru=lru, w_in=_split_w_in(b["w_in"]),
                wb=tuple(b["w_branch"][r] for r in range(3)), w_out=b["w_out"], w_up=b["w_up"],
                w_down=b["w_down"], ada_w=b["ada_w"], ada_b=row(p["ada_b"][l]),
                norm_mix=row(p["norm_mix"][l]), norm_mlp=row(p["norm_mlp"][l]))


def _layer_fwd(x, silu_c, lp, l):
    nm = lambda s: f"l{l}_{s}"
    mod = matmul(silu_c, lp["ada_w"], nm("mod"), bias=lp["ada_b"])
    sh1, sc1, gt1, sh2, sc2, gt2 = (mod[0:1, k * D_MODEL:(k + 1) * D_MODEL] for k in range(N_MOD))
    (h,), _ = scan_fwd(norm1_fn, nm("norm1"), TOK_TILE, [x], [lp["norm_mix"], sc1, sh1], [], [(D_MODEL, BF16)])
    w_gdn, w_ssd, w_lx, w_lg, w_gate = lp["w_in"]
    p_gdn = matmul(h, w_gdn, nm("in_gdn"))
    p_ssd = matmul(h, w_ssd, nm("in_ssd"))
    p_lx = matmul(h, w_lx, nm("in_lx"))
    p_lg = matmul(h, w_lg, nm("in_lg"))
    p_gate = matmul(h, w_gate, nm("in_gate"))
    (ya,), sv_gdn = scan_fwd(gdn_fn, nm("gdn"), CHUNK, [p_gdn], lp["gdn"], [(128, 128)] * 4 + [(8, 1536)],
                             [(512, F32)], save_carry=True)
    (yb,), sv_ssd = scan_fwd(ssd_fn, nm("ssd"), CHUNK, [p_ssd], lp["ssd"], [(128, 128)] * 4 + [(8, 1024)],
                             [(512, F32)], save_carry=True)
    (a, u), sv_lru = scan_fwd(lru_in_fn, nm("lru_in"), TOK_TILE, [p_lx], lp["lru"], [(8, 512)],
                              [(512, F32), (512, F32)], save_carry=True)
    hs = linscan_fwd(a, u, nm("lru_scan"))
    (yc,), _ = scan_fwd(lru_out_fn, nm("lru_out"), TOK_TILE, [hs, p_lg], [], [], [(512, F32)])
    (merged,), _ = scan_fwd(merge_fn, nm("merge"), WIDE_TILE, [ya, yb, yc, p_gate], lp["wb"], [], [(D_MODEL, BF16)])
    mix = matmul(merged, lp["w_out"], nm("out"))
    (h2, x1), _ = scan_fwd(resid_norm_fn, nm("norm2"), TOK_TILE, [x, mix], [gt1, lp["norm_mlp"], sc2, sh2], [],
                           [(D_MODEL, BF16), (D_MODEL, F32)])
    up = matmul(h2, lp["w_up"], nm("up"))
    (act,), _ = scan_fwd(act_fn, nm("act"), WIDE_TILE, [up], [], [], [(D_FF, BF16)])
    dn = matmul(act, lp["w_down"], nm("down"))
    (x2,), _ = scan_fwd(resid_fn, nm("resid"), TOK_TILE, [x1, dn], [gt2], [], [(D_MODEL, F32)])
    saved = dict(x=x, h=h, p_gdn=p_gdn, p_ssd=p_ssd, p_lx=p_lx, p_lg=p_lg, p_gate=p_gate, sv_gdn=sv_gdn,
                 sv_ssd=sv_ssd, sv_lru=sv_lru, a=a, hs=hs, ya=ya, yb=yb, yc=yc, merged=merged, mix=mix, x1=x1,
                 h2=h2, up=up, act=act, dn=dn, mod=(sh1, sc1, gt1, sh2, sc2, gt2))
    return x2, saved


def _layer_bwd(d_x2, silu_c, lp, sv, l):
    nm = lambda s: f"l{l}_b_{s}"
    sh1, sc1, gt1, sh2, sc2, gt2 = sv["mod"]
    (d_x1, d_dn), (d_gt2,) = scan_bwd(resid_fn, nm("resid"), TOK_TILE, [sv["x1"], sv["dn"]], [gt2], [], [d_x2], 2, 1,
                                      [F32, BF16])
    d_act = matmul(d_dn, lp["w_down"], nm("down_x"), tb=True)
    g_w_down = matmul(sv["act"], d_dn, nm("down_w"), ta=True)
    (d_up,), _ = scan_bwd(act_fn, nm("act"), WIDE_TILE, [sv["up"]], [], [], [d_act], 1, 0, [BF16])
    d_h2 = matmul(d_up, lp["w_up"], nm("up_x"), tb=True)
    g_w_up = matmul(sv["h2"], d_up, nm("up_w"), ta=True)
    (d_x, d_mix), (d_gt1, g_norm_mlp, d_sc2, d_sh2) = scan_bwd(
        resid_norm_fn, nm("norm2"), TOK_TILE, [sv["x"], sv["mix"]], [gt1, lp["norm_mlp"], sc2, sh2], [],
        [d_h2, d_x1], 2, 4, [F32, BF16])
    d_merged = matmul(d_mix, lp["w_out"], nm("out_x"), tb=True)
    g_w_out = matmul(sv["merged"], d_mix, nm("out_w"), ta=True)
    (d_ya, d_yb, d_yc, d_pgate), g_wb = scan_bwd(
        merge_fn, nm("merge"), WIDE_TILE, [sv["ya"], sv["yb"], sv["yc"], sv["p_gate"]], lp["wb"], [], [d_merged], 4, 3,
        [F32, F32, F32, BF16])
    (d_hs, d_plg), _ = scan_bwd(lru_out_fn, nm("lru_out"), TOK_TILE, [sv["hs"], sv["p_lg"]], [], [], [d_yc], 2, 0,
                                [F32, BF16])
    d_a, d_u = linscan_bwd(sv["a"], sv["hs"], d_hs, nm("lru_scan"))
    (d_plx,), g_lru = scan_bwd(lru_in_fn, nm("lru_in"), TOK_TILE, [sv["p_lx"]], lp["lru"], sv["sv_lru"],
                               [d_a, d_u], 1, 7, [BF16])
    (d_pssd,), g_ssd = scan_bwd(ssd_fn, nm("ssd"), CHUNK, [sv["p_ssd"]], lp["ssd"], sv["sv_ssd"], [d_yb], 1, 6,
                                [BF16])
    (d_pgdn,), g_gdn = scan_bwd(gdn_fn, nm("gdn"), CHUNK, [sv["p_gdn"]], lp["gdn"], sv["sv_gdn"], [d_ya], 1, 4,
                                [BF16])
    d_h = None
    g_w_in = []
    for tag, dp, w in zip(("gdn", "ssd", "lx", "lg", "gate"), (d_pgdn, d_pssd, d_plx, d_plg, d_pgate), lp["w_in"]):
        d_h = matmul(dp, w, nm("in_x_" + tag), tb=True, add=d_h)
        g_w_in.append(matmul(sv["h"], dp, nm("in_w_" + tag), ta=True))
    (d_x0,), (g_norm_mix, d_sc1, d_sh1) = scan_bwd(norm1_fn, nm("norm1"), TOK_TILE, [sv["x"]],
                                                   [lp["norm_mix"], sc1, sh1], [], [d_h, d_x], 1, 3)
    d_mod = jnp.concatenate([d_sh1, d_sc1, d_gt1, d_sh2, d_sc2, d_gt2], axis=1)
    d_mod8 = jnp.concatenate([d_mod, jnp.zeros((7, d_mod.shape[1]), F32)], axis=0)
    g_ada_w = matmul(silu_c, d_mod8, nm("mod_w"), ta=True)
    flat = lambda v: v.reshape(-1)
    grads = dict(
        ada_b=flat(d_mod), norm_mix=flat(g_norm_mix),
        gdn_conv_w=g_gdn[0], gdn_a_log=g_gdn[1][0, 4:8], gdn_dt_bias=g_gdn[2][0, 4:8], gdn_norm=flat(g_gdn[3]),
        ssd_conv_w=g_ssd[0], ssd_conv_b=flat(g_ssd[1]), ssd_a_log=g_ssd[2][0, 0:8], ssd_dt_bias=g_ssd[3][0, 0:8],
        ssd_d=g_ssd[4].reshape(8, 64).sum(axis=1), ssd_norm=flat(g_ssd[5]),
        lru_conv_w=g_lru[0], lru_conv_b=flat(g_lru[1]), lru_w_a=_diag_blocks(g_lru[2]), lru_b_a=flat(g_lru[3]),
        lru_w_x=_diag_blocks(g_lru[4]), lru_b_x=flat(g_lru[5]), lru_lambda=flat(g_lru[6]),
        norm_mlp=flat(g_norm_mlp))
    big = [g_ada_w, _join_w_in(*g_w_in), jnp.stack(g_wb), g_w_out, g_w_up, g_w_down]
    return d_x0, grads, big


def local_step(x, c, target, p, big):
    c8 = jnp.concatenate([c, jnp.zeros((7, c.shape[1]), F32)], axis=0)
    (silu_c,), _ = scan_fwd(silu_fn, "silu_c", 8, [c8], [], [], [(D_MODEL, F32)])
    lps = [_layer_params(p, big[l], l) for l in range(DEPTH)]
    saved = []
    for l in range(DEPTH):
        x, sv = _layer_fwd(x, silu_c, lps[l], l)
        saved.append(sv)
    loss, d_x, g_final = loss_head(x, target, p["final_norm"].reshape(1, -1), "loss_head")
    layer_grads, big_grads = [None] * DEPTH, [None] * DEPTH
    for l in reversed(range(DEPTH)):
        d_x, layer_grads[l], big_grads[l] = _layer_bwd(d_x, silu_c, lps[l], saved[l], l)
    grads = {k: jnp.stack([layer_grads[l][k] for l in range(DEPTH)]) for k in layer_grads[0]}
    grads["final_norm"] = g_final.reshape(-1)
    return loss, d_x, grads, big_grads


def _adam_nd(g, w, m, v, name):
    two = lambda a: a.reshape(-1, a.shape[-1])
    return tuple(r.reshape(w.shape) for r in adamw(two(g), two(w), two(m), two(v), name))


def _reduce_big(g0, g1, place):
    sib = reduce_d2d(g0, g1, "reduce_pool")
    pooled = [add_cast(a, b, s, place, "reduce_pool_" + n) for (n, _, _), a, b, s in zip(BIG_LAYOUT, g0, g1, sib)]
    recv = reduce_ici([pb for _, pb in pooled], "reduce_ici")
    finals = []
    for (n, kind, full), (pf, _), r in zip(BIG_LAYOUT, pooled, recv):
        local = r.shape[1:]
        own2 = pf.reshape(-1, full[-1])
        r3 = r.reshape(3, -1, local[-1])
        finals.append(sum4(own2, r3, kind == "col", place, "reduce_sum_" + n).reshape(2, *local))
    return share_d2d(finals, "reduce_share")


def _step(w, m, v, x, c, target):
    chip = 2 * lax.axis_index("x") + lax.axis_index("y")
    place = jnp.stack([chip, lax.axis_index("c")]).astype(jnp.int32)
    conv_shapes = [w[n].shape for n in CONV]
    small_shapes = [w[n].shape for n in SMALL]

    big = gather_big([w[n].astype(BF16) for n, _, _ in BIG_LAYOUT], "gather_big")
    conv_all = allgather8(_pack([w[n] for n in CONV], F32), "gather_conv").reshape(8, -1, PACK_COLS)
    conv_parts = [_unpack(conv_all[2 * k], conv_shapes) for k in range(4)]
    p = {n: w[n] for n in SMALL}
    for i, n in enumerate(CONV):
        p[n] = jnp.concatenate([conv_parts[k][i] for k in range(4)], axis=2)

    loss_blk, grad_x, g, big_g = local_step(x[0], c, target[0], p, big)
    big_g = _reduce_big(big_g[0], big_g[1], place)

    small_pack = _pack([loss_blk[0, 0:1]] + [g[n] for n in SMALL] + [g[n] for n in CONV], F32)
    small_all = allgather8(small_pack, "gather_small").reshape(8, -1, PACK_COLS)
    total = _unpack(add8(small_all, "reduce_small"), [(1,)] + small_shapes + [g[n].shape for n in CONV])
    loss = total[0][0]
    small_g = dict(zip(SMALL, total[1:1 + len(SMALL)]))
    conv_g = {n: lax.dynamic_slice_in_dim(t, chip * w[n].shape[2], w[n].shape[2], axis=2)
              for n, t in zip(CONV, total[1 + len(SMALL):])}

    grad, delta, new_m, new_v = {}, {}, {}, {}
    for (n, _, _), gn in zip(BIG_LAYOUT, big_g):
        grad[n] = gn
        delta[n], new_m[n], new_v[n] = _adam_nd(gn, w[n], m[n], v[n], "adam_" + n)
    for names, gs, shapes, tag in ((SMALL, small_g, small_shapes, "small"), (CONV, conv_g, conv_shapes, "conv")):
        pk = lambda d: _pack([d[n] for n in names], F32)
        res = adamw(pk(gs), pk(w), pk(m), pk(v), "adam_" + tag)
        for out, r in zip((delta, new_m, new_v), res):
            out.update(zip(names, _unpack(r, shapes)))
        grad.update({n: gs[n] for n in names})
    outs = [loss, grad_x[None]]
    for d in (grad, delta, new_m, new_v):
        outs += [d[n] for n in WEIGHTS]
    return tuple(outs)


def kernel(x, c, ada_w, ada_b, norm_mix, w_in, gdn_conv_w, gdn_a_log, gdn_dt_bias, gdn_norm, ssd_conv_w, ssd_conv_b, ssd_a_log, ssd_dt_bias, ssd_d, ssd_norm, lru_conv_w, lru_conv_b, lru_w_a, lru_b_a, lru_w_x, lru_b_x, lru_lambda, w_branch, w_out, norm_mlp, w_up, w_down, final_norm, loss_target, m_ada_w, m_ada_b, m_norm_mix, m_w_in, m_gdn_conv_w, m_gdn_a_log, m_gdn_dt_bias, m_gdn_norm, m_ssd_conv_w, m_ssd_conv_b, m_ssd_a_log, m_ssd_dt_bias, m_ssd_d, m_ssd_norm, m_lru_conv_w, m_lru_conv_b, m_lru_w_a, m_lru_b_a, m_lru_w_x, m_lru_b_x, m_lru_lambda, m_w_branch, m_w_out, m_norm_mlp, m_w_up, m_w_down, m_final_norm, v_ada_w, v_ada_b, v_norm_mix, v_w_in, v_gdn_conv_w, v_gdn_a_log, v_gdn_dt_bias, v_gdn_norm, v_ssd_conv_w, v_ssd_conv_b, v_ssd_a_log, v_ssd_dt_bias, v_ssd_d, v_ssd_norm, v_lru_conv_w, v_lru_conv_b, v_lru_w_a, v_lru_b_a, v_lru_w_x, v_lru_b_x, v_lru_lambda, v_w_branch, v_w_out, v_norm_mlp, v_w_up, v_w_down, v_final_norm):
    given = dict(locals())
    w = {n: given[n] for n in WEIGHTS}
    m = {n: given["m_" + n] for n in WEIGHTS}
    v = {n: given["v_" + n] for n in WEIGHTS}
    return _step(w, m, v, x, c, loss_target)
```

```python
import functools
import math

import jax
import jax.numpy as jnp
from jax import lax
from jax.experimental import pallas as pl
from jax.experimental.pallas import tpu as pltpu

F32 = jnp.float32
BF16 = jnp.bfloat16

D_MODEL = 1024
DEPTH = 2
RMS_EPS = 1e-6
CHUNK = 128
GDN_HEADS = 4
SSD_HEADS = 8
LRU_C = 8.0
D_FF = 4096
N_MOD = 6
W_GDN = 2176
W_SSD = 1664
W_LRU = 512
W_GATE = 3072
ADAM_LR = 0.001
ADAM_B1 = 0.9
ADAM_B2 = 0.999
ADAM_EPS = 1e-08
ADAM_WD = 0.01
ADAM_STEP = 10
VMEM_LIMIT = 56 * 1024 * 1024
MESH = pl.DeviceIdType.MESH


def _dot(a, b, ta, tb):
    dn = (((0 if ta else 1,), (1 if tb else 0,)), ((), ()))
    return lax.dot_general(a.astype(BF16), b.astype(BF16), dn, preferred_element_type=F32)


@functools.partial(jax.custom_vjp, nondiff_argnums=(2, 3))
def mm(a, b, ta, tb):
    return _dot(a, b, ta, tb)


def _mm_fwd(a, b, ta, tb):
    return _dot(a, b, ta, tb), (a, b)


def _mm_bwd(ta, tb, res, g):
    a, b = res
    if not ta and not tb:
        return mm(g, b, False, True), mm(a, g, True, False)
    if not ta and tb:
        return mm(g, b, False, False), mm(g, a, True, False)
    assert ta and not tb
    return mm(b, g, False, True), mm(a, g, False, False)


mm.defvjp(_mm_fwd, _mm_bwd)


def _tri_apply(x, upper):
    t = x.shape[0]
    r = lax.broadcasted_iota(jnp.int32, (t, t), 0)
    c = lax.broadcasted_iota(jnp.int32, (t, t), 1)
    tri = jnp.where((r <= c) if upper else (r >= c), 1.0, 0.0).astype(BF16)
    x1 = x.astype(BF16)
    r1 = x - x1.astype(F32)
    x2 = r1.astype(BF16)
    x3 = (r1 - x2.astype(F32)).astype(BF16)
    d = lambda p: jnp.dot(tri, p, preferred_element_type=F32)
    return (d(x1) + d(x2)) + d(x3)


@jax.custom_vjp
def cumsum_rows(x):
    return _tri_apply(x, False)


cumsum_rows.defvjp(lambda x: (_tri_apply(x, False), None), lambda _, g: (_tri_apply(g, True),))


def _dot_split(a, b):
    a1, b1 = a.astype(BF16), b.astype(BF16)
    a2, b2 = (a - a1.astype(F32)).astype(BF16), (b - b1.astype(F32)).astype(BF16)
    d = lambda p, q: jnp.dot(p, q, preferred_element_type=F32)
    return d(a1, b1) + (d(a1, b2) + d(a2, b1))


def _neumann(ms):
    t = ms[0].shape[0]
    xs = [-m for m in ms]
    qs = [_dot(m, m, False, False) for m in ms]
    n = 2
    while True:
        xs = [x + q + _dot(x, q, False, False) for x, q in zip(xs, qs)]
        n *= 2
        if n >= t:
            break
        qs = [_dot(q, q, False, False) for q in qs]
    rs = [-(x + m + _dot_split(m, x)) for x, m in zip(xs, ms)]
    return [x + r + _dot(x, r, False, False) for x, r in zip(xs, rs)]


@jax.custom_vjp
def tri_solve(ms, rhss):
    return tuple(rhs + _dot(x, rhs, False, False) for x, rhs in zip(_neumann(ms), rhss))


def _tri_solve_fwd(ms, rhss):
    xs = _neumann(ms)
    sols = tuple(rhs + _dot(x, rhs, False, False) for x, rhs in zip(xs, rhss))
    return sols, (tuple(xs), sols)


def _tri_solve_bwd(res, gs):
    xs, sols = res
    d_rhss = tuple(g + _dot(x, g, True, False) for x, g in zip(xs, gs))
    return tuple(-_dot(d, sol, False, True) for d, sol in zip(d_rhss, sols)), d_rhss


tri_solve.defvjp(_tri_solve_fwd, _tri_solve_bwd)


@functools.partial(jax.custom_vjp, nondiff_argnums=(1,))
def split_cols(x, sizes):
    out, o = [], 0
    for s in sizes:
        out.append(x[:, o:o + s])
        o += s
    return tuple(out)


split_cols.defvjp(lambda x, sizes: (split_cols(x, sizes), None),
                  lambda sizes, _, g: (jnp.concatenate(list(g), axis=1),))


@functools.partial(jax.custom_vjp, nondiff_argnums=(1,))
def _last_rows(x, t):
    return x[t - 8:, :]


_last_rows.defvjp(lambda x, t: (_last_rows(x, t), None),
                  lambda t, _, g: (jnp.concatenate([jnp.zeros((t - 8, g.shape[1]), g.dtype), g], axis=0),))


def last8(x):
    return _last_rows(x, x.shape[0])


def _shifted(xp, d, t):
    return (pltpu.roll(xp, d, 0) if d else xp)[8:8 + t, :]


@jax.custom_vjp
def conv4(x, tail, w):
    t = x.shape[0]
    xp = jnp.concatenate([tail, x], axis=0)
    return sum(_shifted(xp, 3 - k, t) * w[k:k + 1, :] for k in range(4))


def _conv4_fwd(x, tail, w):
    return conv4(x, tail, w), (x, tail, w)


def _conv4_bwd(res, g):
    x, tail, w = res
    t = x.shape[0]
    xp = jnp.concatenate([tail, x], axis=0)
    zero8 = jnp.zeros((8, g.shape[1]), g.dtype)
    d_xp = jnp.zeros_like(xp)
    d_w = []
    for k in range(4):
        gk = jnp.concatenate([zero8, g * w[k:k + 1, :]], axis=0)
        d_xp = d_xp + (pltpu.roll(gk, t + 8 - (3 - k), 0) if k < 3 else gk)
        d_w.append(jnp.sum(g * _shifted(xp, 3 - k, t), axis=0, keepdims=True))
    return d_xp[8:, :], d_xp[:8, :], jnp.concatenate(d_w, axis=0)


conv4.defvjp(_conv4_fwd, _conv4_bwd)


def _sigmoid(x):
    return 0.5 * (jnp.tanh(0.5 * x) + 1.0)


def _silu(x):
    return x * _sigmoid(x)


def _softplus(x):
    ax = jnp.where(x > 0, x, -x)
    return jnp.where(x > 0, x, 0.0) + jnp.log(1.0 + jnp.exp(-ax))


def _gelu(x):
    return 0.5 * x * (1.0 + jnp.tanh(math.sqrt(2.0 / math.pi) * (x + 0.044715 * (x * x * x))))


def _expm1(x):
    series = x * (1.0 + x * (0.5 + x * (1.0 / 6.0 + x * (1.0 / 24.0))))
    return jnp.where(jnp.abs(x) < 0.03, series, jnp.exp(x) - 1.0)


def _rms(x, w):
    return x * lax.rsqrt(jnp.mean(x * x, axis=-1, keepdims=True) + RMS_EPS) * w


def _lane_pick(x, j):
    lane = lax.broadcasted_iota(jnp.int32, (1, x.shape[1]), 1)
    return jnp.sum(jnp.where(lane == j, x, 0.0), axis=1, keepdims=True)


def _row_pick(x, j):
    row = lax.broadcasted_iota(jnp.int32, (x.shape[0], 1), 0)
    return jnp.sum(jnp.where(row == j, x, 0.0), axis=0, keepdims=True)


def gdn_fn(carry, seq, params):
    *states, tail = carry
    (tile,) = seq
    conv_w, alog_row, dtb_row, norm_w = params
    t = tile.shape[0]
    qkv_raw, z, sm = split_cols(tile, (1536, 512, 128))
    qkv = _silu(conv4(qkv_raw, tail, conv_w))
    parts = split_cols(qkv, (128,) * 12)
    zs = split_cols(z, (128,) * 4)
    lane = lax.broadcasted_iota(jnp.int32, (1, 128), 1)
    beta_all = _sigmoid(sm)
    g_all = jnp.where((lane >= 4) & (lane < 8), -jnp.exp(alog_row) * _softplus(sm + dtb_row), 0.0)
    gc_all = cumsum_rows(g_all)
    gr_all = gc_all.T
    gl_all = _row_pick(gc_all, t - 1)
    r = lax.broadcasted_iota(jnp.int32, (t, t), 0)
    c = lax.broadcasted_iota(jnp.int32, (t, t), 1)
    heads = range(GDN_HEADS)
    l2 = lambda a: a * lax.rsqrt(jnp.sum(a * a, axis=-1, keepdims=True) + RMS_EPS)
    qn = [l2(parts[h]) * (128.0 ** -0.5) for h in heads]
    kn = [l2(parts[4 + h]) for h in heads]
    beta = [_lane_pick(beta_all, h) for h in heads]
    gc = [_lane_pick(gc_all, 4 + h) for h in heads]
    gl = [_lane_pick(gl_all, 4 + h) for h in heads]
    decay = [jnp.exp(jnp.where(r >= c, gc[h] - _row_pick(gr_all, 4 + h), -1e30)) for h in heads]
    kk = [mm(kn[h], kn[h], False, True) for h in heads]
    qk = [mm(qn[h], kn[h], False, True) for h in heads]
    m = tuple(jnp.where(r > c, beta[h] * kk[h] * decay[h], 0.0) for h in heads)
    eg = [jnp.exp(gc[h]) for h in heads]
    rhs = tuple(jnp.concatenate([beta[h] * parts[8 + h], (beta[h] * eg[h]) * kn[h]], axis=1) for h in heads)
    uw = [split_cols(s, (128, 128)) for s in tri_solve(m, rhs)]
    ws = [mm(uw[h][1], states[h], False, False) for h in heads]
    qs = [mm(qn[h] * eg[h], states[h], False, False) for h in heads]
    v_new = [uw[h][0] - ws[h] for h in heads]
    o = [qs[h] + mm(qk[h] * decay[h], v_new[h], False, False) for h in heads]
    kv = [mm(kn[h] * jnp.exp(gl[h] - gc[h]), v_new[h], True, False) for h in heads]
    new_states = [states[h] * jnp.exp(gl[h]) + kv[h] for h in heads]
    outs = [_rms(o[h], norm_w) * _silu(zs[h]) for h in heads]
    return (*new_states, last8(qkv_raw)), (jnp.concatenate(outs, axis=1),)


def ssd_fn(carry, seq, params):
    *states, tail = carry
    (tile,) = seq
    conv_w, conv_b, alog_row, dtb_row, d_row, norm_w = params
    t = tile.shape[0]
    xbc_raw, z, sm = split_cols(tile, (1024, 512, 128))
    xbc = _silu(conv4(xbc_raw, tail, conv_w) + conv_b)
    x0, x1, x2, x3, b0, b1, c0, c1 = split_cols(xbc, (128,) * 8)
    xs, bs, cs = (x0, x1, x2, x3), (b0, b1), (c0, c1)
    ds = split_cols(d_row, (128,) * 4)
    lane = lax.broadcasted_iota(jnp.int32, (1, 128), 1)
    sub = lax.broadcasted_iota(jnp.int32, (128, 1), 0)
    low = lane < 64
    dt_all = jnp.where(lane < SSD_HEADS, _softplus(sm + dtb_row), 0.0)
    ac_all = cumsum_rows(dt_all * (-jnp.exp(alog_row)))
    ar_all = ac_all.T
    al_all = _row_pick(ac_all, t - 1)
    r = lax.broadcasted_iota(jnp.int32, (t, t), 0)
    c = lax.broadcasted_iota(jnp.int32, (t, t), 1)
    pairs, heads = range(4), range(SSD_HEADS)
    col = [_lane_pick(ac_all, h) for h in heads]
    last = [_lane_pick(al_all, h) for h in heads]
    dt = [_lane_pick(dt_all, h) for h in heads]
    lm = [jnp.exp(jnp.where(r >= c, col[h] - _row_pick(ar_all, h), -1e30)) for h in heads]
    cb = [mm(cs[g], bs[g], False, True) for g in range(2)]
    both = lambda a, b: jnp.where(low, a, b)
    xdt = [xs[p] * both(dt[2 * p], dt[2 * p + 1]) for p in pairs]
    y_off = [mm(cs[p // 2], states[p], False, True) for p in pairs]
    y_lo = [mm(cb[p // 2] * lm[2 * p], jnp.where(low, xdt[p], 0.0), False, False) for p in pairs]
    y_hi = [mm(cb[p // 2] * lm[2 * p + 1], jnp.where(low, 0.0, xdt[p]), False, False) for p in pairs]
    st = [mm(xdt[p] * both(jnp.exp(last[2 * p] - col[2 * p]), jnp.exp(last[2 * p + 1] - col[2 * p + 1])),
             bs[p // 2], True, False) for p in pairs]
    ys = [ds[p] * xs[p] + y_lo[p] + y_hi[p] + y_off[p] * both(jnp.exp(col[2 * p]), jnp.exp(col[2 * p + 1]))
          for p in pairs]
    new_states = [states[p] * jnp.where(sub < 64, jnp.exp(last[2 * p]), jnp.exp(last[2 * p + 1])) + st[p]
                  for p in pairs]
    gz = jnp.concatenate(ys, axis=1) * _silu(z)
    g0, g1 = split_cols(gz, (256, 256))
    n0, n1 = split_cols(norm_w, (256, 256))
    out = jnp.concatenate([_rms(g0, n0), _rms(g1, n1)], axis=1)
    return (*new_states, last8(xbc_raw)), (out,)


def lru_in_fn(carry, seq, params):
    (tail,) = carry
    (x,) = seq
    conv_w, conv_b, w_a, b_a, w_x, b_x, lam = params
    xc = conv4(x, tail, conv_w) + conv_b
    r = _sigmoid(mm(xc, w_a, False, False) + b_a)
    i = _sigmoid(mm(xc, w_x, False, False) + b_x)
    log_a = -LRU_C * r * _softplus(-lam)
    u = jnp.sqrt(-_expm1(2.0 * log_a)) * (i * xc)
    return (last8(x),), (jnp.exp(log_a), u)


def lru_out_fn(carry, seq, params):
    hs, gate = seq
    return (), (hs * _gelu(gate),)


def merge_fn(carry, seq, params):
    ya, yb, yc, gl = seq
    g = split_cols(_sigmoid(gl), (D_MODEL,) * 3)
    merged = sum(g[r] * mm(y, params[r], False, False) for r, y in enumerate((ya, yb, yc)))
    return (), (merged,)


def _adaln(x, w, sc, sh):
    return _rms(x, w) * (1.0 + sc) + sh


def norm1_fn(carry, seq, params):
    (x,) = seq
    return (), (_adaln(x, *params), x)


def resid_norm_fn(carry, seq, params):
    x, mix = seq
    gt, w, sc, sh = params
    x1 = x + gt * mix
    return (), (_adaln(x1, w, sc, sh), x1)


def resid_fn(carry, seq, params):
    x, dn = seq
    (gt,) = params
    return (), (x + gt * dn,)


def silu_fn(carry, seq, params):
    return (), (_silu(seq[0]),)


def _full_spec(a):
    nd = a.ndim
    return pl.BlockSpec(a.shape, lambda i: (0,) * nd)


def _cparams(*sem):
    return pltpu.CompilerParams(dimension_semantics=sem, vmem_limit_bytes=VMEM_LIMIT)


def scan_fwd(fn, name, tile, seqs, params, carry_shapes, outs, save_carry=False):
    rows = seqs[0].shape[0]
    tile = min(tile, rows)
    n = rows // tile
    ns, npar, nc, no = len(seqs), len(params), len(carry_shapes), len(outs)

    def body(*refs):
        seq_refs, refs = refs[:ns], refs[ns:]
        par_refs, refs = refs[:npar], refs[npar:]
        out_refs, refs = refs[:no], refs[no:]
        save_refs, refs = (refs[:nc], refs[nc:]) if save_carry else ((), refs)
        carry_refs = refs

        @pl.when(pl.program_id(0) == 0)
        def _():
            for cr in carry_refs:
                cr[...] = jnp.zeros_like(cr)

        carry = tuple(cr[...] for cr in carry_refs)
        for sr, cv in zip(save_refs, carry):
            sr[0] = cv
        new_carry, res = fn(carry, tuple(r[...].astype(F32) for r in seq_refs),
                            tuple(r[...].astype(F32) for r in par_refs))
        for r, v in zip(out_refs, res):
            r[...] = v.astype(r.dtype)
        for cr, v in zip(carry_refs, new_carry):
            cr[...] = v

    out_shape = [jax.ShapeDtypeStruct((rows, w), dt) for w, dt in outs]
    out_specs = [pl.BlockSpec((tile, w), lambda i: (i, 0)) for w, _ in outs]
    if save_carry:
        out_shape += [jax.ShapeDtypeStruct((n, *s), F32) for s in carry_shapes]
        out_specs += [pl.BlockSpec((1, *s), lambda i: (i, 0, 0)) for s in carry_shapes]
    res = pl.pallas_call(
        body, name=name, grid=(n,),
        in_specs=[pl.BlockSpec((tile, s.shape[1]), lambda i: (i, 0)) for s in seqs] + [_full_spec(p) for p in params],
        out_specs=out_specs, out_shape=out_shape,
        scratch_shapes=[pltpu.VMEM(s, F32) for s in carry_shapes],
        compiler_params=_cparams("arbitrary"),
    )(*seqs, *params)
    return res[:no], res[no:]


def scan_bwd(fn, name, tile, seqs, params, saved, douts, n_dseq, n_dpar, dseq_dtypes=None):
    dseq_dtypes = dseq_dtypes or [F32] * n_dseq
    rows = seqs[0].shape[0]
    tile = min(tile, rows)
    n = rows // tile
    ns, npar, nc, no = len(seqs), len(params), len(saved), len(douts)

    def body(*refs):
        seq_refs, refs = refs[:ns], refs[ns:]
        par_refs, refs = refs[:npar], refs[npar:]
        save_refs, refs = refs[:nc], refs[nc:]
        dout_refs, refs = refs[:no], refs[no:]
        dseq_refs, refs = refs[:n_dseq], refs[n_dseq:]
        dpar_refs, refs = refs[:n_dpar], refs[n_dpar:]
        dcarry_refs = refs

        @pl.when(pl.program_id(0) == 0)
        def _():
            for r in (*dpar_refs, *dcarry_refs):
                r[...] = jnp.zeros_like(r)

        carry = tuple(r[0] for r in save_refs)
        seq = tuple(r[...].astype(F32) for r in seq_refs)
        par = tuple(r[...].astype(F32) for r in par_refs)

        def f(carry, dseq, dpar):
            return fn(carry, (*dseq, *seq[n_dseq:]), (*dpar, *par[n_dpar:]))

        _, vjp = jax.vjp(f, carry, seq[:n_dseq], par[:n_dpar])
        d_carry, d_seq, d_par = vjp((tuple(r[...] for r in dcarry_refs),
                                     tuple(r[...].astype(F32) for r in dout_refs)))
        for r, v in zip(dseq_refs, d_seq):
            r[...] = v.astype(r.dtype)
        for r, v in zip(dpar_refs, d_par):
            r[...] += v
        for r, v in zip(dcarry_refs, d_carry):
            r[...] = v

    rev = lambda i: (n - 1 - i, 0)
    res = pl.pallas_call(
        body, name=name, grid=(n,),
        in_specs=([pl.BlockSpec((tile, s.shape[1]), rev) for s in seqs] + [_full_spec(p) for p in params]
                  + [pl.BlockSpec((1, *s.shape[1:]), lambda i: (n - 1 - i, 0, 0)) for s in saved]
                  + [pl.BlockSpec((tile, d.shape[1]), rev) for d in douts]),
        out_specs=([pl.BlockSpec((tile, s.shape[1]), rev) for s in seqs[:n_dseq]]
                   + [_full_spec(p) for p in params[:n_dpar]]),
        out_shape=([jax.ShapeDtypeStruct((rows, s.shape[1]), dt) for s, dt in zip(seqs[:n_dseq], dseq_dtypes)]
                   + [jax.ShapeDtypeStruct(p.shape, F32) for p in params[:n_dpar]]),
        scratch_shapes=[pltpu.VMEM(s.shape[1:], F32) for s in saved],
        compiler_params=_cparams("arbitrary"),
    )(*seqs, *params, *saved, *douts)
    return res[:n_dseq], res[n_dseq:]


def _tile_of(dim, pref):
    if dim <= pref:
        return dim
    best = max((t for t in range(128, pref + 1, 128) if dim % t == 0), default=None)
    if best is None or (best < 512 and dim <= 2304):
        return dim
    return best


def _row_tile(rows, pref):
    if rows <= pref:
        return rows
    return max(t for t in range(8, pref + 1, 8) if rows % t == 0)


def matmul(a, b, name, ta=False, tb=False, out_dtype=F32, add=None, bias=None, relu2=False, relu2_of=None,
           tm=1024, tn=1024, tk=1024):
    m, k = (a.shape[1], a.shape[0]) if ta else a.shape
    n = b.shape[0] if tb else b.shape[1]
    assert k == (b.shape[1] if tb else b.shape[0])
    tm, tn, tk = _tile_of(m, tm), _tile_of(n, tn), _tile_of(k, tk)
    nm, nn, nk = m // tm, n // tn, k // tk
    assert nk == 1 or (out_dtype == F32 and not relu2 and relu2_of is None)
    dn = (((0 if ta else 1,), (1 if tb else 0,)), ((), ()))
    has_add, has_bias, has_u = add is not None, bias is not None, relu2_of is not None
    n_inner = a.size * a.dtype.itemsize * (nn - 1) >= b.size * b.dtype.itemsize * (nm - 1)
    ij = (lambda g0, g1: (g0, g1)) if n_inner else (lambda g0, g1: (g1, g0))

    def body(*refs):
        a_ref, b_ref, refs = refs[0], refs[1], refs[2:]
        add_ref, refs = (refs[0], refs[1:]) if has_add else (None, refs)
        bias_ref, refs = (refs[0], refs[1:]) if has_bias else (None, refs)
        u_ref, refs = (refs[0], refs[1:]) if has_u else (None, refs)
        o_ref = refs[0]
        r = lax.dot_general(a_ref[...].astype(BF16), b_ref[...].astype(BF16), dn, preferred_element_type=F32)

        def first():
            v = r
            if has_add:
                v = v + add_ref[...]
            if has_bias:
                v = v + bias_ref[...]
            if has_u:
                v = v * (2.0 * jnp.maximum(u_ref[...], 0.0))
            o_ref[...] = v.astype(o_ref.dtype)
            if relu2:
                p = jnp.maximum(v, 0.0)
                refs[1][...] = (p * p).astype(BF16)

        if nk == 1:
            first()
        else:
            pl.when(pl.program_id(2) == 0)(first)

            @pl.when(pl.program_id(2) > 0)
            def _():
                o_ref[...] += r

    def spec(shape, fn):
        return pl.BlockSpec(shape, lambda g0, g1, l: fn(*ij(g0, g1), l))

    a_spec = spec((tk, tm), lambda i, j, l: (l, i)) if ta else spec((tm, tk), lambda i, j, l: (i, l))
    b_spec = spec((tn, tk), lambda i, j, l: (j, l)) if tb else spec((tk, tn), lambda i, j, l: (l, j))
    o_spec = spec((tm, tn), lambda i, j, l: (i, j))
    in_specs, args = [a_spec, b_spec], [a, b]
    if has_add:
        in_specs.append(o_spec)
        args.append(add)
    if has_bias:
        in_specs.append(spec((1, tn), lambda i, j, l: (0, j)))
        args.append(bias)
    if has_u:
        in_specs.append(o_spec)
        args.append(relu2_of)
    out_shape = [jax.ShapeDtypeStruct((m, n), out_dtype)] + ([jax.ShapeDtypeStruct((m, n), BF16)] if relu2 else [])
    res = pl.pallas_call(
        body, name=name, grid=(nm, nn, nk) if n_inner else (nn, nm, nk), in_specs=in_specs,
        out_specs=[o_spec] * len(out_shape), out_shape=out_shape,
        compiler_params=_cparams("parallel", "parallel", "arbitrary"),
    )(*args)
    return res if relu2 else res[0]


LIN_TILE = 512


def linscan_fwd(a, u, name):
    rows, w = a.shape
    tile = min(LIN_TILE, rows)

    def body(a_ref, u_ref, h_ref, hc):
        @pl.when(pl.program_id(0) == 0)
        def _():
            hc[...] = jnp.zeros_like(hc)

        def step(t, h):
            h = a_ref[pl.ds(t, 1), :] * h + u_ref[pl.ds(t, 1), :]
            h_ref[pl.ds(t, 1), :] = h
            return h

        hc[...] = lax.fori_loop(0, tile, step, hc[...], unroll=8)

    spec = pl.BlockSpec((tile, w), lambda i: (i, 0))
    return pl.pallas_call(
        body, name=name, grid=(rows // tile,), in_specs=[spec, spec], out_specs=spec,
        out_shape=jax.ShapeDtypeStruct((rows, w), F32), scratch_shapes=[pltpu.VMEM((1, w), F32)],
        compiler_params=_cparams("arbitrary"),
    )(a, u)


def linscan_bwd(a, hs, dh, name):
    rows, w = a.shape
    tile = min(LIN_TILE, rows)
    n = rows // tile
    per = tile // 8

    def body(a_ref, h_ref, hprev_ref, dh_ref, da_ref, du_ref, cc):
        i = pl.program_id(0)

        @pl.when(i == 0)
        def _():
            cc[...] = jnp.zeros_like(cc)

        def step(s, c):
            t = tile - 1 - s
            g = dh_ref[pl.ds(t, 1), :] + c
            du_ref[pl.ds(t, 1), :] = g
            da_ref[pl.ds(t, 1), :] = g * h_ref[pl.ds(t - 1, 1), :]
            return a_ref[pl.ds(t, 1), :] * g

        c = lax.fori_loop(0, tile - 1, step, cc[...], unroll=8)
        g = dh_ref[0:1, :] + c
        du_ref[0:1, :] = g
        da_ref[0:1, :] = g * jnp.where(i == n - 1, 0.0, hprev_ref[7:8, :])
        cc[...] = a_ref[0:1, :] * g

    rev = pl.BlockSpec((tile, w), lambda i: (n - 1 - i, 0))
    prev = pl.BlockSpec((8, w), lambda i: (jnp.maximum((n - 1 - i) * per - 1, 0), 0))
    return pl.pallas_call(
        body, name=name, grid=(n,), in_specs=[rev, rev, prev, rev], out_specs=[rev, rev],
        out_shape=[jax.ShapeDtypeStruct((rows, w), F32)] * 2, scratch_shapes=[pltpu.VMEM((1, w), F32)],
        compiler_params=_cparams("arbitrary"),
    )(a, hs, hs, dh)


def loss_head(x, target, w, name):
    rows, d = x.shape
    tile = min(512, rows)

    def body(x_ref, t_ref, w_ref, loss_ref, dx_ref, dw_ref):
        @pl.when(pl.program_id(0) == 0)
        def _():
            loss_ref[...] = jnp.zeros_like(loss_ref)
            dw_ref[...] = jnp.zeros_like(dw_ref)

        tv = t_ref[...]

        def f(xv, wv):
            e = _rms(xv, wv) - tv
            return 0.5 * jnp.sum(jnp.mean(e * e, axis=-1, keepdims=True), axis=0, keepdims=True)

        val, vjp = jax.vjp(f, x_ref[...], w_ref[...])
        dxv, dwv = vjp(jnp.ones((1, 1), F32))
        loss_ref[...] += jnp.broadcast_to(val, loss_ref.shape)
        dx_ref[...] = dxv
        dw_ref[...] += dwv

    spec = pl.BlockSpec((tile, d), lambda i: (i, 0))
    return pl.pallas_call(
        body, name=name, grid=(rows // tile,), in_specs=[spec, spec, _full_spec(w)],
        out_specs=[pl.BlockSpec((8, 128), lambda i: (0, 0)), spec, _full_spec(w)],
        out_shape=[jax.ShapeDtypeStruct((8, 128), F32), jax.ShapeDtypeStruct((rows, d), F32),
                   jax.ShapeDtypeStruct(w.shape, F32)],
        compiler_params=_cparams("arbitrary"),
    )(x, target, w)


def adamw(g, w, m, v, name, copy_g=False):
    rows, cols = g.shape
    tile = _row_tile(rows, 256)
    n_out = 4 if copy_g else 3

    def body(g_ref, w_ref, m_ref, v_ref, d_ref, nm_ref, nv_ref, *g_out):
        gv = g_ref[...]
        if copy_g:
            g_out[0][...] = gv
        nm = ADAM_B1 * m_ref[...] + (1.0 - ADAM_B1) * gv
        nv = ADAM_B2 * v_ref[...] + (1.0 - ADAM_B2) * (gv * gv)
        m_hat = nm / (1.0 - ADAM_B1 ** ADAM_STEP)
        v_hat = nv / (1.0 - ADAM_B2 ** ADAM_STEP)
        d_ref[...] = -ADAM_LR * (m_hat / (jnp.sqrt(v_hat) + ADAM_EPS) + ADAM_WD * w_ref[...])
        nm_ref[...] = nm
        nv_ref[...] = nv

    spec = pl.BlockSpec((tile, cols), lambda i: (i, 0))
    return pl.pallas_call(
        body, name=name, grid=(rows // tile,), in_specs=[spec] * 4, out_specs=[spec] * n_out,
        out_shape=[jax.ShapeDtypeStruct((rows, cols), F32)] * n_out,
        compiler_params=_cparams("parallel"),
    )(g, w, m, v)


def add_cast(g0, g1, sib, place, name):
    shape = sib.shape
    g0, g1, sib = (a.reshape(-1, shape[-1]) for a in (g0, g1, sib))
    rows, cols = sib.shape
    tile = _row_tile(rows, max(8, min(256, (512 * 1024) // cols)))

    def body(k_ref, g0_ref, g1_ref, s_ref, o_ref, ob_ref):
        s = jnp.where(k_ref[1] == 0, g0_ref[...], g1_ref[...]) + s_ref[...]
        o_ref[...] = s
        ob_ref[...] = s.astype(BF16)

    spec = pl.BlockSpec((tile, cols), lambda i, k: (i, 0))
    s, sb = pl.pallas_call(
        body, name=name,
        grid_spec=pltpu.PrefetchScalarGridSpec(
            num_scalar_prefetch=1, grid=(rows // tile,),
            in_specs=[pl.BlockSpec((tile, cols), lambda i, k: (i * (1 - k[1]), 0)),
                      pl.BlockSpec((tile, cols), lambda i, k: (i * k[1], 0)), spec],
            out_specs=[spec, spec]),
        out_shape=[jax.ShapeDtypeStruct((rows, cols), F32), jax.ShapeDtypeStruct((rows, cols), BF16)],
        compiler_params=_cparams("arbitrary"),
    )(place, g0, g1, sib)
    return s.reshape(shape), sb.reshape(shape)


def sum4(own, recv, by_cols, place, name):
    _, r, c = recv.shape
    tile = _row_tile(r, 256)
    nt = r // tile
    own_map = (lambda i, k: (i, k[0])) if by_cols else (lambda i, k: (k[0] * nt + i, 0))

    def body(k_ref, own_ref, recv_ref, o_ref):
        o_ref[...] = ((own_ref[...] + recv_ref[0].astype(F32)) + recv_ref[1].astype(F32)) + recv_ref[2].astype(F32)

    return pl.pallas_call(
        body, name=name,
        grid_spec=pltpu.PrefetchScalarGridSpec(
            num_scalar_prefetch=1, grid=(nt,),
            in_specs=[pl.BlockSpec((tile, c), own_map), pl.BlockSpec((3, tile, c), lambda i, k: (0, i, 0))],
            out_specs=pl.BlockSpec((None, tile, c), lambda i, k: (k[1], i, 0))),
        out_shape=jax.ShapeDtypeStruct((2, r, c), F32),
        compiler_params=_cparams("arbitrary"),
    )(place, own, recv)


def add8(parts, name):
    _, rows, cols = parts.shape

    def body(p_ref, o_ref):
        acc = p_ref[0]
        for k in range(1, 8):
            acc = acc + p_ref[k]
        o_ref[...] = acc

    return pl.pallas_call(
        body, name=name, in_specs=[pl.BlockSpec(memory_space=pltpu.VMEM)],
        out_specs=pl.BlockSpec(memory_space=pltpu.VMEM),
        out_shape=jax.ShapeDtypeStruct((rows, cols), F32),
    )(parts)


def _place():
    return lax.axis_index("x"), lax.axis_index("y"), lax.axis_index("c")


def _other_chips(x, y):
    return [(1 - x, y), (x, 1 - y), (1 - x, 1 - y)]


_ANY = pl.BlockSpec(memory_space=pl.ANY)


BIG_LAYOUT = (("ada_w", "col", (1024, 6144)), ("w_in", "chip", (4, 1024, 1924)), ("w_branch", "col", (3, 512, 1024)),
              ("w_out", "row", (1024, 1024)), ("w_up", "col", (1024, 4096)), ("w_down", "row", (4096, 1024)))
N_BIG = len(BIG_LAYOUT)
REDUCE_LAYOUT = BIG_LAYOUT[1:]


def _local_shape(kind, full):
    if kind == "col":
        return (*full[:-1], full[-1] // 4)
    if kind == "row":
        return (full[0] // 4, *full[1:])
    return full[1:]


def _window(ref, kind, k, local):
    if kind == "chip":
        return ref.at[k]
    if kind == "row":
        return ref.at[pl.ds(pl.multiple_of(k * local[0], 8), local[0])]
    idx = (slice(None),) * (len(local) - 1) + (pl.ds(pl.multiple_of(k * local[-1], 128), local[-1]),)
    return ref.at[idx]


def _dma_call(body, name, n_in, out_shape, sems, aliases=None):
    return pl.pallas_call(
        body, name=name, in_specs=[_ANY] * n_in, out_specs=[_ANY] * len(out_shape), out_shape=out_shape,
        scratch_shapes=[pltpu.SemaphoreType.DMA((n,)) for n in sems],
        input_output_aliases=aliases or {},
        compiler_params=pltpu.CompilerParams(has_side_effects=True))


def _remote(src, dst, send_sem, recv_sem, to):
    return pltpu.make_async_remote_copy(src_ref=src, dst_ref=dst, send_sem=send_sem, recv_sem=recv_sem,
                                        device_id=to, device_id_type=MESH)


def gather_big(shards, name):
    locals_ = [_local_shape(kind, full) for _, kind, full in BIG_LAYOUT]

    def body(*refs):
        sh, refs = refs[:N_BIG], refs[N_BIG:]
        full, refs = (refs[:N_BIG], refs[N_BIG:2 * N_BIG]), refs[2 * N_BIG:]
        send_sems, recv_sems, local_sems, pass_send, pass_recv = refs
        x, y, c = _place()
        me = 2 * x + y
        chips = _other_chips(x, y)
        for cc in (0, 1):
            @pl.when(c == cc)
            def _():
                win = lambda n, k: _window(full[cc][n], BIG_LAYOUT[n][1], k, locals_[n])
                mine, sends = [], []
                for n in range(N_BIG):
                    mine.append(pltpu.make_async_copy(sh[n].at[cc], win(n, me), local_sems.at[n]))
                    mine[n].start()
                    for j, chip in enumerate(chips):
                        sends.append(_remote(sh[n].at[cc], win(n, me), send_sems.at[3 * n + j],
                                             recv_sems.at[3 * n + j], (chip[0], chip[1], c)))
                        sends[-1].start()
                for n in range(N_BIG):
                    for j, chip in enumerate(chips):
                        _remote(sh[n].at[cc], win(n, 2 * chip[0] + chip[1]), send_sems.at[3 * n + j],
                                recv_sems.at[3 * n + j], (chip[0], chip[1], c)).wait_recv()
                    mine[n].wait()
                    sends.append(_remote(full[cc][n], full[cc][n], pass_send.at[n], pass_recv.at[n], (x, y, 1 - c)))
                    sends[-1].start()
                for n in range(N_BIG):
                    _remote(full[1 - cc][n], full[1 - cc][n], pass_send.at[n], pass_recv.at[n],
                            (x, y, 1 - c)).wait_recv()
                for cp in sends:
                    cp.wait_send()

    out_shape = [jax.ShapeDtypeStruct(full, BF16) for _, _, full in BIG_LAYOUT] * 2
    res = _dma_call(body, name, N_BIG, out_shape, (3 * N_BIG, 3 * N_BIG, N_BIG, N_BIG, N_BIG))(*shards)
    return res[:N_BIG], res[N_BIG:]


def reduce_d2d(g0, g1, name):
    nb = len(g0)

    def body(*refs):
        g, refs = (refs[:nb], refs[nb:2 * nb]), refs[2 * nb:]
        sib, (send_sems, recv_sems) = refs[:nb], refs[nb:]
        x, y, c = _place()
        for cc in (0, 1):
            @pl.when(c == cc)
            def _():
                sends = [_remote(g[1 - cc][n], sib[n], send_sems.at[n], recv_sems.at[n], (x, y, 1 - c))
                         for n in range(nb)]
                for cp in sends:
                    cp.start()
                for cp in sends:
                    cp.wait_recv()
                for cp in sends:
                    cp.wait_send()

    out_shape = [jax.ShapeDtypeStruct(a.shape, a.dtype) for a in g0]
    return _dma_call(body, name, 2 * nb, out_shape, (nb, nb))(*g0, *g1)


def reduce_ici(sums, layout, name):
    nb = len(sums)
    locals_ = [_local_shape(kind, full) for _, kind, full in layout]

    def body(*refs):
        src, recv, (send_sems, recv_sems) = refs[:nb], refs[nb:2 * nb], refs[2 * nb:]
        x, y, c = _place()
        chips = _other_chips(x, y)
        copies = []
        for n in range(nb):
            for j, chip in enumerate(chips):
                cp = _remote(_window(src[n], layout[n][1], 2 * chip[0] + chip[1], locals_[n]), recv[n].at[j],
                             send_sems.at[3 * n + j], recv_sems.at[3 * n + j], (chip[0], chip[1], c))
                cp.start()
                copies.append(cp)
        for cp in copies:
            cp.wait_recv()
        for cp in copies:
            cp.wait_send()

    out_shape = [jax.ShapeDtypeStruct((3, *ls), a.dtype) for ls, a in zip(locals_, sums)]
    return _dma_call(body, name, nb, out_shape, (3 * nb, 3 * nb))(*sums)


def share_d2d(finals, name):
    nb = len(finals)

    def body(*refs):
        out, (send_sems, recv_sems) = refs[nb:2 * nb], refs[2 * nb:]
        x, y, c = _place()
        sends = [_remote(out[n].at[c], out[n].at[c], send_sems.at[n], recv_sems.at[n], (x, y, 1 - c))
                 for n in range(nb)]
        for cp in sends:
            cp.start()
        for n in range(nb):
            _remote(out[n].at[c], out[n].at[1 - c], send_sems.at[n], recv_sems.at[n], (x, y, 1 - c)).wait_recv()
        for cp in sends:
            cp.wait_send()

    out_shape = [jax.ShapeDtypeStruct(a.shape, a.dtype) for a in finals]
    return _dma_call(body, name, nb, out_shape, (nb, nb), aliases={n: n for n in range(nb)})(*finals)


def allgather8(block, name):
    m_per, n = block.shape

    def body(x_ref, out_ref, send_sems, recv_sems, local_sem):
        x, y, c = _place()
        me, sibling = (x, y, c), (x, y, 1 - c)
        chips = _other_chips(x, y)

        def rows(px, py, pc):
            return out_ref.at[pl.ds((4 * px + 2 * py + pc) * m_per, m_per), :]

        def copy(k, blk, to, src=None):
            return pltpu.make_async_remote_copy(
                src_ref=rows(*blk) if src is None else src, dst_ref=rows(*blk), send_sem=send_sems.at[k],
                recv_sem=recv_sems.at[k], device_id=to, device_id_type=MESH)

        mine = pltpu.make_async_copy(x_ref, rows(*me), local_sem)
        mine.start()
        first = [copy(0, me, sibling, src=x_ref)]
        first += [copy(1 + j, me, (*chip, c), src=x_ref) for j, chip in enumerate(chips)]
        for cp in first:
            cp.start()
        passed = [copy(4 + j, (*chip, c), sibling) for j, chip in enumerate(chips)]
        for j, chip in enumerate(chips):
            copy(1 + j, (*chip, c), me).wait_recv()
            passed[j].start()
        copy(0, sibling, me).wait_recv()
        for j, chip in enumerate(chips):
            copy(4 + j, (*chip, 1 - c), me).wait_recv()
        for cp in first + passed:
            cp.wait_send()
        mine.wait()

    return pl.pallas_call(
        body, name=name, in_specs=[pl.BlockSpec(memory_space=pltpu.VMEM)],
        out_specs=pl.BlockSpec(memory_space=pltpu.VMEM),
        out_shape=jax.ShapeDtypeStruct((8 * m_per, n), block.dtype),
        scratch_shapes=[pltpu.SemaphoreType.DMA((7,)), pltpu.SemaphoreType.DMA((7,)), pltpu.SemaphoreType.DMA],
    )(block)


CONV = ("gdn_conv_w", "ssd_conv_w", "lru_conv_w")
SMALL = ("ada_b", "norm_mix", "gdn_a_log", "gdn_dt_bias", "gdn_norm", "ssd_conv_b", "ssd_a_log", "ssd_dt_bias",
         "ssd_d", "ssd_norm", "lru_conv_b", "lru_w_a", "lru_b_a", "lru_w_x", "lru_b_x", "lru_lambda", "norm_mlp",
         "final_norm")
WEIGHTS = ("ada_w", "ada_b", "norm_mix", "w_in", "gdn_conv_w", "gdn_a_log", "gdn_dt_bias", "gdn_norm", "ssd_conv_w",
           "ssd_conv_b", "ssd_a_log", "ssd_dt_bias", "ssd_d", "ssd_norm", "lru_conv_w", "lru_conv_b", "lru_w_a",
           "lru_b_a", "lru_w_x", "lru_b_x", "lru_lambda", "w_branch", "w_out", "norm_mlp", "w_up", "w_down",
           "final_norm")
PACK_COLS = 1024


def _pack(arrays, dtype):
    flat = jnp.concatenate([a.reshape(-1).astype(dtype) for a in arrays])
    pad = -flat.shape[0] % (8 * PACK_COLS)
    if pad:
        flat = jnp.concatenate([flat, jnp.zeros((pad,), dtype)])
    return flat.reshape(-1, PACK_COLS)


def _unpack(pack, shapes):
    flat = pack.reshape(-1)
    out, o = [], 0
    for s in shapes:
        n = math.prod(s)
        out.append(flat[o:o + n].reshape(s))
        o += n
    return out


def _split_w_in(w4):
    w = jnp.concatenate([w4[k] for k in range(4)], axis=1)
    pad = jnp.zeros((w.shape[0], 120), w.dtype)
    gdn = jnp.concatenate([w[:, 0:2056], pad], axis=1)
    ssd = jnp.concatenate([w[:, 2056:2568], w[:, 3080:3592], w[:, 2568:3080], w[:, 3592:3600], pad], axis=1)
    return gdn, ssd, w[:, 3600:4112], w[:, 4112:4624], w[:, 4624:7696]


def _join_w_in(gdn, ssd, lx, lg, gate):
    w = jnp.concatenate([gdn[:, 0:2056], ssd[:, 0:512], ssd[:, 1024:1536], ssd[:, 512:1024], ssd[:, 1536:1544],
                         lx, lg, gate], axis=1)
    return jnp.stack([w[:, k * 1924:(k + 1) * 1924] for k in range(4)])


def _lanes(v, at, width=128):
    return jnp.zeros((1, width), F32).at[0, at:at + v.shape[0]].set(v)


def _block_diag(w):
    return (jnp.eye(8, dtype=w.dtype)[:, None, :, None] * w[:, :, None, :]).reshape(512, 512)


def _diag_blocks(w):
    return jnp.stack([w[n * 64:(n + 1) * 64, n * 64:(n + 1) * 64] for n in range(8)])


TOK_TILE = 512
WIDE_TILE = 256


def _layer_params(p, big, l):
    row = lambda v: v.reshape(1, -1)
    b = dict(zip((n for n, _, _ in BIG_LAYOUT), big))
    gdn = (p["gdn_conv_w"][l], _lanes(p["gdn_a_log"][l], 4), _lanes(p["gdn_dt_bias"][l], 4), row(p["gdn_norm"][l]))
    ssd = (p["ssd_conv_w"][l], row(p["ssd_conv_b"][l]), _lanes(p["ssd_a_log"][l], 0), _lanes(p["ssd_dt_bias"][l], 0),
           row(jnp.repeat(p["ssd_d"][l], 64)), row(p["ssd_norm"][l]))
    lru = (p["lru_conv_w"][l], row(p["lru_conv_b"][l]), _block_diag(p["lru_w_a"][l]), row(p["lru_b_a"][l]),
           _block_diag(p["lru_w_x"][l]), row(p["lru_b_x"][l]), row(p["lru_lambda"][l]))
    return dict(gdn=gdn, ssd=ssd, lru=lru, w_in=_split_w_in(b["w_in"]),
                wb=tuple(b["w_branch"][r] for r in range(3)), w_out=b["w_out"], w_up=b["w_up"],
                w_down=b["w_down"], ada_w=b["ada_w"], ada_b=row(p["ada_b"][l]),
                norm_mix=row(p["norm_mix"][l]), norm_mlp=row(p["norm_mlp"][l]))


def _layer_fwd(x, silu_c, lp, l):
    nm = lambda s: f"l{l}_{s}"
    mod = matmul(silu_c, lp["ada_w"], nm("mod"), bias=lp["ada_b"])
    sh1, sc1, gt1, sh2, sc2, gt2 = (mod[0:1, k * D_MODEL:(k + 1) * D_MODEL] for k in range(N_MOD))
    (h,), _ = scan_fwd(norm1_fn, nm("norm1"), TOK_TILE, [x], [lp["norm_mix"], sc1, sh1], [], [(D_MODEL, BF16)])
    w_gdn, w_ssd, w_lx, w_lg, w_gate = lp["w_in"]
    p_gdn = matmul(h, w_gdn, nm("in_gdn"))
    p_ssd = matmul(h, w_ssd, nm("in_ssd"))
    p_lx = matmul(h, w_lx, nm("in_lx"))
    p_lg = matmul(h, w_lg, nm("in_lg"))
    p_gate = matmul(h, w_gate, nm("in_gate"))
    (ya,), sv_gdn = scan_fwd(gdn_fn, nm("gdn"), CHUNK, [p_gdn], lp["gdn"], [(128, 128)] * 4 + [(8, 1536)],
                             [(512, F32)], save_carry=True)
    (yb,), sv_ssd = scan_fwd(ssd_fn, nm("ssd"), CHUNK, [p_ssd], lp["ssd"], [(128, 128)] * 4 + [(8, 1024)],
                             [(512, F32)], save_carry=True)
    (a, u), sv_lru = scan_fwd(lru_in_fn, nm("lru_in"), TOK_TILE, [p_lx], lp["lru"], [(8, 512)],
                              [(512, F32), (512, F32)], save_carry=True)
    hs = linscan_fwd(a, u, nm("lru_scan"))
    (yc,), _ = scan_fwd(lru_out_fn, nm("lru_out"), TOK_TILE, [hs, p_lg], [], [], [(512, F32)])
    (merged,), _ = scan_fwd(merge_fn, nm("merge"), WIDE_TILE, [ya, yb, yc, p_gate], lp["wb"], [], [(D_MODEL, BF16)])
    mix = matmul(merged, lp["w_out"], nm("out"))
    (h2, x1), _ = scan_fwd(resid_norm_fn, nm("norm2"), TOK_TILE, [x, mix], [gt1, lp["norm_mlp"], sc2, sh2], [],
                           [(D_MODEL, BF16), (D_MODEL, F32)])
    up, act = matmul(h2, lp["w_up"], nm("up"), relu2=True)
    dn = matmul(act, lp["w_down"], nm("down"))
    (x2,), _ = scan_fwd(resid_fn, nm("resid"), TOK_TILE, [x1, dn], [gt2], [], [(D_MODEL, F32)])
    saved = dict(x=x, h=h, p_gdn=p_gdn, p_ssd=p_ssd, p_lx=p_lx, p_lg=p_lg, p_gate=p_gate, sv_gdn=sv_gdn,
                 sv_ssd=sv_ssd, sv_lru=sv_lru, a=a, hs=hs, ya=ya, yb=yb, yc=yc, merged=merged, mix=mix, x1=x1,
                 h2=h2, up=up, act=act, dn=dn, mod=(sh1, sc1, gt1, sh2, sc2, gt2))
    return x2, saved


def _layer_bwd(d_x2, silu_c, lp, sv, l):
    nm = lambda s: f"l{l}_b_{s}"
    sh1, sc1, gt1, sh2, sc2, gt2 = sv["mod"]
    (d_x1, d_dn), (d_gt2,) = scan_bwd(resid_fn, nm("resid"), TOK_TILE, [sv["x1"], sv["dn"]], [gt2], [], [d_x2], 2, 1,
                                      [F32, BF16])
    d_up = matmul(d_dn, lp["w_down"], nm("down_x"), tb=True, relu2_of=sv["up"], out_dtype=BF16)
    g_w_down = matmul(sv["act"], d_dn, nm("down_w"), ta=True)
    d_h2 = matmul(d_up, lp["w_up"], nm("up_x"), tb=True)
    g_w_up = matmul(sv["h2"], d_up, nm("up_w"), ta=True)
    (d_x, d_mix), (d_gt1, g_norm_mlp, d_sc2, d_sh2) = scan_bwd(
        resid_norm_fn, nm("norm2"), TOK_TILE, [sv["x"], sv["mix"]], [gt1, lp["norm_mlp"], sc2, sh2], [],
        [d_h2, d_x1], 2, 4, [F32, BF16])
    d_merged = matmul(d_mix, lp["w_out"], nm("out_x"), tb=True)
    g_w_out = matmul(sv["merged"], d_mix, nm("out_w"), ta=True)
    (d_ya, d_yb, d_yc, d_pgate), g_wb = scan_bwd(
        merge_fn, nm("merge"), WIDE_TILE, [sv["ya"], sv["yb"], sv["yc"], sv["p_gate"]], lp["wb"], [], [d_merged], 4, 3,
        [F32, F32, F32, BF16])
    (d_hs, d_plg), _ = scan_bwd(lru_out_fn, nm("lru_out"), TOK_TILE, [sv["hs"], sv["p_lg"]], [], [], [d_yc], 2, 0,
                                [F32, BF16])
    d_a, d_u = linscan_bwd(sv["a"], sv["hs"], d_hs, nm("lru_scan"))
    (d_plx,), g_lru = scan_bwd(lru_in_fn, nm("lru_in"), TOK_TILE, [sv["p_lx"]], lp["lru"], sv["sv_lru"],
                               [d_a, d_u], 1, 7, [BF16])
    (d_pssd,), g_ssd = scan_bwd(ssd_fn, nm("ssd"), CHUNK, [sv["p_ssd"]], lp["ssd"], sv["sv_ssd"], [d_yb], 1, 6,
                                [BF16])
    (d_pgdn,), g_gdn = scan_bwd(gdn_fn, nm("gdn"), CHUNK, [sv["p_gdn"]], lp["gdn"], sv["sv_gdn"], [d_ya], 1, 4,
                                [BF16])
    d_h = None
    g_w_in = []
    for tag, dp, w in zip(("gdn", "ssd", "lx", "lg", "gate"), (d_pgdn, d_pssd, d_plx, d_plg, d_pgate), lp["w_in"]):
        d_h = matmul(dp, w, nm("in_x_" + tag), tb=True, add=d_h)
        g_w_in.append(matmul(sv["h"], dp, nm("in_w_" + tag), ta=True))
    (d_x0,), (g_norm_mix, d_sc1, d_sh1) = scan_bwd(norm1_fn, nm("norm1"), TOK_TILE, [sv["x"]],
                                                   [lp["norm_mix"], sc1, sh1], [], [d_h, d_x], 1, 3)
    d_mod = jnp.concatenate([d_sh1, d_sc1, d_gt1, d_sh2, d_sc2, d_gt2], axis=1)
    flat = lambda v: v.reshape(-1)
    grads = dict(
        ada_b=flat(d_mod), norm_mix=flat(g_norm_mix),
        gdn_conv_w=g_gdn[0], gdn_a_log=g_gdn[1][0, 4:8], gdn_dt_bias=g_gdn[2][0, 4:8], gdn_norm=flat(g_gdn[3]),
        ssd_conv_w=g_ssd[0], ssd_conv_b=flat(g_ssd[1]), ssd_a_log=g_ssd[2][0, 0:8], ssd_dt_bias=g_ssd[3][0, 0:8],
        ssd_d=g_ssd[4].reshape(8, 64).sum(axis=1), ssd_norm=flat(g_ssd[5]),
        lru_conv_w=g_lru[0], lru_conv_b=flat(g_lru[1]), lru_w_a=_diag_blocks(g_lru[2]), lru_b_a=flat(g_lru[3]),
        lru_w_x=_diag_blocks(g_lru[4]), lru_b_x=flat(g_lru[5]), lru_lambda=flat(g_lru[6]),
        norm_mlp=flat(g_norm_mlp))
    big = [_join_w_in(*g_w_in), jnp.stack(g_wb), g_w_out, g_w_up, g_w_down]
    return d_x0, grads, big


def local_step(x, c, target, p, big):
    c8 = jnp.concatenate([c, jnp.zeros((7, c.shape[1]), F32)], axis=0)
    (silu_c,), _ = scan_fwd(silu_fn, "silu_c", 8, [c8], [], [], [(D_MODEL, F32)])
    lps = [_layer_params(p, big[l], l) for l in range(DEPTH)]
    saved = []
    for l in range(DEPTH):
        x, sv = _layer_fwd(x, silu_c, lps[l], l)
        saved.append(sv)
    loss, d_x, g_final = loss_head(x, target, p["final_norm"].reshape(1, -1), "loss_head")
    layer_grads, big_grads = [None] * DEPTH, [None] * DEPTH
    for l in reversed(range(DEPTH)):
        d_x, layer_grads[l], big_grads[l] = _layer_bwd(d_x, silu_c, lps[l], saved[l], l)
    grads = {k: jnp.stack([layer_grads[l][k] for l in range(DEPTH)]) for k in layer_grads[0]}
    grads["final_norm"] = g_final.reshape(-1)
    return loss, d_x, grads, big_grads, silu_c[0]


def _adam_nd(g, w, m, v, name):
    two = lambda a: a.reshape(-1, a.shape[-1])
    return tuple(r.reshape(w.shape) for r in adamw(two(g), two(w), two(m), two(v), name, copy_g=True))


def _reduce_big(g0, g1, place):
    sib = reduce_d2d(g0, g1, "reduce_pool")
    pooled = [add_cast(a, b, s, place, "reduce_pool_" + n) for (n, _, _), a, b, s in zip(REDUCE_LAYOUT, g0, g1, sib)]
    recv = reduce_ici([pb for _, pb in pooled], REDUCE_LAYOUT, "reduce_ici")
    finals = []
    for (n, kind, full), (pf, _), r in zip(REDUCE_LAYOUT, pooled, recv):
        local = r.shape[1:]
        own2 = pf.reshape(-1, full[-1])
        r3 = r.reshape(3, -1, local[-1])
        finals.append(sum4(own2, r3, kind == "col", place, "reduce_sum_" + n).reshape(2, *local))
    return share_d2d(finals, "reduce_share")


def _step(w, m, v, x, c, target):
    chip = 2 * lax.axis_index("x") + lax.axis_index("y")
    place = jnp.stack([chip, lax.axis_index("c")]).astype(jnp.int32)
    conv_shapes = [w[n].shape for n in CONV]
    small_shapes = [w[n].shape for n in SMALL]

    big = gather_big([w[n].astype(BF16) for n, _, _ in BIG_LAYOUT], "gather_big")
    conv_all = allgather8(_pack([w[n] for n in CONV], F32), "gather_conv").reshape(8, -1, PACK_COLS)
    conv_parts = [_unpack(conv_all[2 * k], conv_shapes) for k in range(4)]
    p = {n: w[n] for n in SMALL}
    for i, n in enumerate(CONV):
        p[n] = jnp.concatenate([conv_parts[k][i] for k in range(4)], axis=2)

    loss_blk, grad_x, g, big_g, silu_c = local_step(x[0], c, target[0], p, big)
    big_g = dict(zip((n for n, _, _ in REDUCE_LAYOUT), _reduce_big(big_g[0], big_g[1], place)))

    assert SMALL[0] == "ada_b"
    small_pack = _pack([g["ada_b"], silu_c, loss_blk[0, 0:1]] + [g[n] for n in SMALL[1:]] + [g[n] for n in CONV], F32)
    small_all = allgather8(small_pack, "gather_small").reshape(8, -1, PACK_COLS)
    total = _unpack(add8(small_all, "reduce_small"),
                    [small_shapes[0], (D_MODEL,), (1,)] + small_shapes[1:] + [g[n].shape for n in CONV])
    loss = total[2][0]
    small_g = dict(zip(SMALL, [total[0]] + total[3:2 + len(SMALL)]))
    conv_g = {n: lax.dynamic_slice_in_dim(t, chip * w[n].shape[2], w[n].shape[2], axis=2)
              for n, t in zip(CONV, total[2 + len(SMALL):])}

    cols = w["ada_w"].shape[2]
    silu_all = small_all[:, 2 * N_MOD, :]
    big_g["ada_w"] = jnp.stack([
        matmul(silu_all, lax.dynamic_slice_in_dim(small_all[:, N_MOD * l:N_MOD * (l + 1), :].reshape(8, -1),
                                                  chip * cols, cols, axis=1), f"ada_w_grad{l}", ta=True)
        for l in range(DEPTH)])

    grad, delta, new_m, new_v = {}, {}, {}, {}
    for n, _, _ in BIG_LAYOUT:
        delta[n], new_m[n], new_v[n], grad[n] = _adam_nd(big_g[n], w[n], m[n], v[n], "adam_" + n)
    for names, gs, shapes, tag in ((SMALL, small_g, small_shapes, "small"), (CONV, conv_g, conv_shapes, "conv")):
        pk = lambda d: _pack([d[n] for n in names], F32)
        res = adamw(pk(gs), pk(w), pk(m), pk(v), "adam_" + tag)
        for out, r in zip((delta, new_m, new_v), res):
            out.update(zip(names, _unpack(r, shapes)))
        grad.update({n: gs[n] for n in names})
    outs = [loss, grad_x[None]]
    for d in (grad, delta, new_m, new_v):
        outs += [d[n] for n in WEIGHTS]
    return tuple(outs)


def kernel(x, c, ada_w, ada_b, norm_mix, w_in, gdn_conv_w, gdn_a_log, gdn_dt_bias, gdn_norm, ssd_conv_w, ssd_conv_b, ssd_a_log, ssd_dt_bias, ssd_d, ssd_norm, lru_conv_w, lru_conv_b, lru_w_a, lru_b_a, lru_w_x, lru_b_x, lru_lambda, w_branch, w_out, norm_mlp, w_up, w_down, final_norm, loss_target, m_ada_w, m_ada_b, m_norm_mix, m_w_in, m_gdn_conv_w, m_gdn_a_log, m_gdn_dt_bias, m_gdn_norm, m_ssd_conv_w, m_ssd_conv_b, m_ssd_a_log, m_ssd_dt_bias, m_ssd_d, m_ssd_norm, m_lru_conv_w, m_lru_conv_b, m_lru_w_a, m_lru_b_a, m_lru_w_x, m_lru_b_x, m_lru_lambda, m_w_branch, m_w_out, m_norm_mlp, m_w_up, m_w_down, m_final_norm, v_ada_w, v_ada_b, v_norm_mix, v_w_in, v_gdn_conv_w, v_gdn_a_log, v_gdn_dt_bias, v_gdn_norm, v_ssd_conv_w, v_ssd_conv_b, v_ssd_a_log, v_ssd_dt_bias, v_ssd_d, v_ssd_norm, v_lru_conv_w, v_lru_conv_b, v_lru_w_a, v_lru_b_a, v_lru_w_x, v_lru_b_x, v_lru_lambda, v_w_branch, v_w_out, v_norm_mlp, v_w_up, v_w_down, v_final_norm):
    given = dict(locals())
    w = {n: given[n] for n in WEIGHTS}
    m = {n: given["m_" + n] for n in WEIGHTS}
    v = {n: given["v_" + n] for n in WEIGHTS}
    return _step(w, m, v, x, c, loss_target)
```

```python
import functools
import math

import jax
import jax.numpy as jnp
from jax import lax
from jax.experimental import pallas as pl
from jax.experimental.pallas import tpu as pltpu

F32 = jnp.float32
BF16 = jnp.bfloat16

D_MODEL = 1024
DEPTH = 2
RMS_EPS = 1e-6
CHUNK = 128
GDN_HEADS = 4
SSD_HEADS = 8
LRU_C = 8.0
D_FF = 4096
N_MOD = 6
W_GDN = 2176
W_SSD = 1664
W_LRU = 512
W_GATE = 3072
ADAM_LR = 0.001
ADAM_B1 = 0.9
ADAM_B2 = 0.999
ADAM_EPS = 1e-08
ADAM_WD = 0.01
ADAM_STEP = 10
VMEM_LIMIT = 56 * 1024 * 1024
MESH = pl.DeviceIdType.MESH


def _dot(a, b, ta, tb):
    dn = (((0 if ta else 1,), (1 if tb else 0,)), ((), ()))
    return lax.dot_general(a.astype(BF16), b.astype(BF16), dn, preferred_element_type=F32)


@functools.partial(jax.custom_vjp, nondiff_argnums=(2, 3))
def mm(a, b, ta, tb):
    return _dot(a, b, ta, tb)


def _mm_fwd(a, b, ta, tb):
    return _dot(a, b, ta, tb), (a, b)


def _mm_bwd(ta, tb, res, g):
    a, b = res
    if not ta and not tb:
        return mm(g, b, False, True), mm(a, g, True, False)
    if not ta and tb:
        return mm(g, b, False, False), mm(g, a, True, False)
    assert ta and not tb
    return mm(b, g, False, True), mm(a, g, False, False)


mm.defvjp(_mm_fwd, _mm_bwd)


def _tri_apply(x, upper):
    t = x.shape[0]
    r = lax.broadcasted_iota(jnp.int32, (t, t), 0)
    c = lax.broadcasted_iota(jnp.int32, (t, t), 1)
    tri = jnp.where((r <= c) if upper else (r >= c), 1.0, 0.0).astype(BF16)
    x1 = x.astype(BF16)
    r1 = x - x1.astype(F32)
    x2 = r1.astype(BF16)
    x3 = (r1 - x2.astype(F32)).astype(BF16)
    d = lambda p: jnp.dot(tri, p, preferred_element_type=F32)
    return (d(x1) + d(x2)) + d(x3)


@jax.custom_vjp
def cumsum_rows(x):
    return _tri_apply(x, False)


cumsum_rows.defvjp(lambda x: (_tri_apply(x, False), None), lambda _, g: (_tri_apply(g, True),))


def _dot_split(a, b):
    a1, b1 = a.astype(BF16), b.astype(BF16)
    a2, b2 = (a - a1.astype(F32)).astype(BF16), (b - b1.astype(F32)).astype(BF16)
    d = lambda p, q: jnp.dot(p, q, preferred_element_type=F32)
    return d(a1, b1) + (d(a1, b2) + d(a2, b1))


def _neumann(ms):
    t = ms[0].shape[0]
    xs = [-m for m in ms]
    qs = [_dot(m, m, False, False) for m in ms]
    n = 2
    while True:
        xs = [x + q + _dot(x, q, False, False) for x, q in zip(xs, qs)]
        n *= 2
        if n >= t:
            break
        qs = [_dot(q, q, False, False) for q in qs]
    rs = [-(x + m + _dot_split(m, x)) for x, m in zip(xs, ms)]
    return [x + r + _dot(x, r, False, False) for x, r in zip(xs, rs)]


@jax.custom_vjp
def tri_solve(ms, rhss):
    return tuple(rhs + _dot(x, rhs, False, False) for x, rhs in zip(_neumann(ms), rhss))


def _tri_solve_fwd(ms, rhss):
    xs = _neumann(ms)
    sols = tuple(rhs + _dot(x, rhs, False, False) for x, rhs in zip(xs, rhss))
    return sols, (tuple(xs), sols)


def _tri_solve_bwd(res, gs):
    xs, sols = res
    d_rhss = tuple(g + _dot(x, g, True, False) for x, g in zip(xs, gs))
    return tuple(-_dot(d, sol, False, True) for d, sol in zip(d_rhss, sols)), d_rhss


tri_solve.defvjp(_tri_solve_fwd, _tri_solve_bwd)


@functools.partial(jax.custom_vjp, nondiff_argnums=(1,))
def split_cols(x, sizes):
    out, o = [], 0
    for s in sizes:
        out.append(x[:, o:o + s])
        o += s
    return tuple(out)


split_cols.defvjp(lambda x, sizes: (split_cols(x, sizes), None),
                  lambda sizes, _, g: (jnp.concatenate(list(g), axis=1),))


@functools.partial(jax.custom_vjp, nondiff_argnums=(1,))
def _last_rows(x, t):
    return x[t - 8:, :]


_last_rows.defvjp(lambda x, t: (_last_rows(x, t), None),
                  lambda t, _, g: (jnp.concatenate([jnp.zeros((t - 8, g.shape[1]), g.dtype), g], axis=0),))


def last8(x):
    return _last_rows(x, x.shape[0])


def _shifted(xp, d, t):
    return (pltpu.roll(xp, d, 0) if d else xp)[8:8 + t, :]


@jax.custom_vjp
def conv4(x, tail, w):
    t = x.shape[0]
    xp = jnp.concatenate([tail, x], axis=0)
    return sum(_shifted(xp, 3 - k, t) * w[k:k + 1, :] for k in range(4))


def _conv4_fwd(x, tail, w):
    return conv4(x, tail, w), (x, tail, w)


def _conv4_bwd(res, g):
    x, tail, w = res
    t = x.shape[0]
    xp = jnp.concatenate([tail, x], axis=0)
    zero8 = jnp.zeros((8, g.shape[1]), g.dtype)
    d_xp = jnp.zeros_like(xp)
    d_w = []
    for k in range(4):
        gk = jnp.concatenate([zero8, g * w[k:k + 1, :]], axis=0)
        d_xp = d_xp + (pltpu.roll(gk, t + 8 - (3 - k), 0) if k < 3 else gk)
        d_w.append(jnp.sum(g * _shifted(xp, 3 - k, t), axis=0, keepdims=True))
    return d_xp[8:, :], d_xp[:8, :], jnp.concatenate(d_w, axis=0)


conv4.defvjp(_conv4_fwd, _conv4_bwd)


def _sigmoid(x):
    return 0.5 * (jnp.tanh(0.5 * x) + 1.0)


def _silu(x):
    return x * _sigmoid(x)


def _softplus(x):
    ax = jnp.where(x > 0, x, -x)
    return jnp.where(x > 0, x, 0.0) + jnp.log(1.0 + jnp.exp(-ax))


def _gelu(x):
    return 0.5 * x * (1.0 + jnp.tanh(math.sqrt(2.0 / math.pi) * (x + 0.044715 * (x * x * x))))


def _expm1(x):
    series = x * (1.0 + x * (0.5 + x * (1.0 / 6.0 + x * (1.0 / 24.0))))
    return jnp.where(jnp.abs(x) < 0.03, series, jnp.exp(x) - 1.0)


def _rms(x, w):
    return x * lax.rsqrt(jnp.mean(x * x, axis=-1, keepdims=True) + RMS_EPS) * w


def _lane_pick(x, j):
    lane = lax.broadcasted_iota(jnp.int32, (1, x.shape[1]), 1)
    return jnp.sum(jnp.where(lane == j, x, 0.0), axis=1, keepdims=True)


def _row_pick(x, j):
    row = lax.broadcasted_iota(jnp.int32, (x.shape[0], 1), 0)
    return jnp.sum(jnp.where(row == j, x, 0.0), axis=0, keepdims=True)


def gdn_fn(carry, seq, params):
    *states, tail = carry
    (tile,) = seq
    conv_w, alog_row, dtb_row, norm_w = params
    t = tile.shape[0]
    qkv_raw, z, sm = split_cols(tile, (1536, 512, 128))
    qkv = _silu(conv4(qkv_raw, tail, conv_w))
    parts = split_cols(qkv, (128,) * 12)
    zs = split_cols(z, (128,) * 4)
    lane = lax.broadcasted_iota(jnp.int32, (1, 128), 1)
    beta_all = _sigmoid(sm)
    g_all = jnp.where((lane >= 4) & (lane < 8), -jnp.exp(alog_row) * _softplus(sm + dtb_row), 0.0)
    gc_all = cumsum_rows(g_all)
    gr_all = gc_all.T
    gl_all = _row_pick(gc_all, t - 1)
    r = lax.broadcasted_iota(jnp.int32, (t, t), 0)
    c = lax.broadcasted_iota(jnp.int32, (t, t), 1)
    heads = range(GDN_HEADS)
    l2 = lambda a: a * lax.rsqrt(jnp.sum(a * a, axis=-1, keepdims=True) + RMS_EPS)
    qn = [l2(parts[h]) * (128.0 ** -0.5) for h in heads]
    kn = [l2(parts[4 + h]) for h in heads]
    beta = [_lane_pick(beta_all, h) for h in heads]
    gc = [_lane_pick(gc_all, 4 + h) for h in heads]
    gl = [_lane_pick(gl_all, 4 + h) for h in heads]
    decay = [jnp.exp(jnp.where(r >= c, gc[h] - _row_pick(gr_all, 4 + h), -1e30)) for h in heads]
    kk = [mm(kn[h], kn[h], False, True) for h in heads]
    qk = [mm(qn[h], kn[h], False, True) for h in heads]
    m = tuple(jnp.where(r > c, beta[h] * kk[h] * decay[h], 0.0) for h in heads)
    eg = [jnp.exp(gc[h]) for h in heads]
    rhs = tuple(jnp.concatenate([beta[h] * parts[8 + h], (beta[h] * eg[h]) * kn[h]], axis=1) for h in heads)
    uw = [split_cols(s, (128, 128)) for s in tri_solve(m, rhs)]
    ws = [mm(uw[h][1], states[h], False, False) for h in heads]
    qs = [mm(qn[h] * eg[h], states[h], False, False) for h in heads]
    v_new = [uw[h][0] - ws[h] for h in heads]
    o = [qs[h] + mm(qk[h] * decay[h], v_new[h], False, False) for h in heads]
    kv = [mm(kn[h] * jnp.exp(gl[h] - gc[h]), v_new[h], True, False) for h in heads]
    new_states = [states[h] * jnp.exp(gl[h]) + kv[h] for h in heads]
    outs = [_rms(o[h], norm_w) * _silu(zs[h]) for h in heads]
    return (*new_states, last8(qkv_raw)), (jnp.concatenate(outs, axis=1),)


def ssd_fn(carry, seq, params):
    *states, tail = carry
    (tile,) = seq
    conv_w, conv_b, alog_row, dtb_row, d_row, norm_w = params
    t = tile.shape[0]
    xbc_raw, z, sm = split_cols(tile, (1024, 512, 128))
    xbc = _silu(conv4(xbc_raw, tail, conv_w) + conv_b)
    x0, x1, x2, x3, b0, b1, c0, c1 = split_cols(xbc, (128,) * 8)
    xs, bs, cs = (x0, x1, x2, x3), (b0, b1), (c0, c1)
    ds = split_cols(d_row, (128,) * 4)
    lane = lax.broadcasted_iota(jnp.int32, (1, 128), 1)
    sub = lax.broadcasted_iota(jnp.int32, (128, 1), 0)
    low = lane < 64
    dt_all = jnp.where(lane < SSD_HEADS, _softplus(sm + dtb_row), 0.0)
    ac_all = cumsum_rows(dt_all * (-jnp.exp(alog_row)))
    ar_all = ac_all.T
    al_all = _row_pick(ac_all, t - 1)
    r = lax.broadcasted_iota(jnp.int32, (t, t), 0)
    c = lax.broadcasted_iota(jnp.int32, (t, t), 1)
    pairs, heads = range(4), range(SSD_HEADS)
    col = [_lane_pick(ac_all, h) for h in heads]
    last = [_lane_pick(al_all, h) for h in heads]
    dt = [_lane_pick(dt_all, h) for h in heads]
    lm = [jnp.exp(jnp.where(r >= c, col[h] - _row_pick(ar_all, h), -1e30)) for h in heads]
    cb = [mm(cs[g], bs[g], False, True) for g in range(2)]
    both = lambda a, b: jnp.where(low, a, b)
    xdt = [xs[p] * both(dt[2 * p], dt[2 * p + 1]) for p in pairs]
    y_off = [mm(cs[p // 2], states[p], False, True) for p in pairs]
    y_lo = [mm(cb[p // 2] * lm[2 * p], jnp.where(low, xdt[p], 0.0), False, False) for p in pairs]
    y_hi = [mm(cb[p // 2] * lm[2 * p + 1], jnp.where(low, 0.0, xdt[p]), False, False) for p in pairs]
    st = [mm(xdt[p] * both(jnp.exp(last[2 * p] - col[2 * p]), jnp.exp(last[2 * p + 1] - col[2 * p + 1])),
             bs[p // 2], True, False) for p in pairs]
    ys = [ds[p] * xs[p] + y_lo[p] + y_hi[p] + y_off[p] * both(jnp.exp(col[2 * p]), jnp.exp(col[2 * p + 1]))
          for p in pairs]
    new_states = [states[p] * jnp.where(sub < 64, jnp.exp(last[2 * p]), jnp.exp(last[2 * p + 1])) + st[p]
                  for p in pairs]
    gz = jnp.concatenate(ys, axis=1) * _silu(z)
    g0, g1 = split_cols(gz, (256, 256))
    n0, n1 = split_cols(norm_w, (256, 256))
    out = jnp.concatenate([_rms(g0, n0), _rms(g1, n1)], axis=1)
    return (*new_states, last8(xbc_raw)), (out,)


def lru_in_fn(carry, seq, params):
    (tail,) = carry
    (x,) = seq
    conv_w, conv_b, w_a, b_a, w_x, b_x, lam = params
    xc = conv4(x, tail, conv_w) + conv_b
    r = _sigmoid(mm(xc, w_a, False, False) + b_a)
    i = _sigmoid(mm(xc, w_x, False, False) + b_x)
    log_a = -LRU_C * r * _softplus(-lam)
    u = jnp.sqrt(-_expm1(2.0 * log_a)) * (i * xc)
    return (last8(x),), (jnp.exp(log_a), u)


def lru_out_fn(carry, seq, params):
    hs, gate = seq
    return (), (hs * _gelu(gate),)


def merge_fn(carry, seq, params):
    ya, yb, yc, gl = seq
    g = split_cols(_sigmoid(gl), (D_MODEL,) * 3)
    merged = sum(g[r] * mm(y, params[r], False, False) for r, y in enumerate((ya, yb, yc)))
    return (), (merged,)


def _adaln(x, w, sc, sh):
    return _rms(x, w) * (1.0 + sc) + sh


def norm1_fn(carry, seq, params):
    (x,) = seq
    return (), (_adaln(x, *params), x)


def resid_norm_fn(carry, seq, params):
    x, mix = seq
    gt, w, sc, sh = params
    x1 = x + gt * mix
    return (), (_adaln(x1, w, sc, sh), x1)


def resid_fn(carry, seq, params):
    x, dn = seq
    (gt,) = params
    return (), (x + gt * dn,)


def silu_fn(carry, seq, params):
    return (), (_silu(seq[0]),)


def _full_spec(a):
    nd = a.ndim
    return pl.BlockSpec(a.shape, lambda i: (0,) * nd)


def _cparams(*sem):
    return pltpu.CompilerParams(dimension_semantics=sem, vmem_limit_bytes=VMEM_LIMIT)


def scan_fwd(fn, name, tile, seqs, params, carry_shapes, outs, save_carry=False):
    rows = seqs[0].shape[0]
    tile = min(tile, rows)
    n = rows // tile
    ns, npar, nc, no = len(seqs), len(params), len(carry_shapes), len(outs)

    def body(*refs):
        seq_refs, refs = refs[:ns], refs[ns:]
        par_refs, refs = refs[:npar], refs[npar:]
        out_refs, refs = refs[:no], refs[no:]
        save_refs, refs = (refs[:nc], refs[nc:]) if save_carry else ((), refs)
        carry_refs = refs

        @pl.when(pl.program_id(0) == 0)
        def _():
            for cr in carry_refs:
                cr[...] = jnp.zeros_like(cr)

        carry = tuple(cr[...] for cr in carry_refs)
        for sr, cv in zip(save_refs, carry):
            sr[0] = cv
        new_carry, res = fn(carry, tuple(r[...].astype(F32) for r in seq_refs),
                            tuple(r[...].astype(F32) for r in par_refs))
        for r, v in zip(out_refs, res):
            r[...] = v.astype(r.dtype)
        for cr, v in zip(carry_refs, new_carry):
            cr[...] = v

    out_shape = [jax.ShapeDtypeStruct((rows, w), dt) for w, dt in outs]
    out_specs = [pl.BlockSpec((tile, w), lambda i: (i, 0)) for w, _ in outs]
    if save_carry:
        out_shape += [jax.ShapeDtypeStruct((n, *s), F32) for s in carry_shapes]
        out_specs += [pl.BlockSpec((1, *s), lambda i: (i, 0, 0)) for s in carry_shapes]
    res = pl.pallas_call(
        body, name=name, grid=(n,),
        in_specs=[pl.BlockSpec((tile, s.shape[1]), lambda i: (i, 0)) for s in seqs] + [_full_spec(p) for p in params],
        out_specs=out_specs, out_shape=out_shape,
        scratch_shapes=[pltpu.VMEM(s, F32) for s in carry_shapes],
        compiler_params=_cparams("arbitrary"),
    )(*seqs, *params)
    return res[:no], res[no:]


def scan_bwd(fn, name, tile, seqs, params, saved, douts, n_dseq, n_dpar, dseq_dtypes=None):
    dseq_dtypes = dseq_dtypes or [F32] * n_dseq
    rows = seqs[0].shape[0]
    tile = min(tile, rows)
    n = rows // tile
    ns, npar, nc, no = len(seqs), len(params), len(saved), len(douts)

    def body(*refs):
        seq_refs, refs = refs[:ns], refs[ns:]
        par_refs, refs = refs[:npar], refs[npar:]
        save_refs, refs = refs[:nc], refs[nc:]
        dout_refs, refs = refs[:no], refs[no:]
        dseq_refs, refs = refs[:n_dseq], refs[n_dseq:]
        dpar_refs, refs = refs[:n_dpar], refs[n_dpar:]
        dcarry_refs = refs

        @pl.when(pl.program_id(0) == 0)
        def _():
            for r in (*dpar_refs, *dcarry_refs):
                r[...] = jnp.zeros_like(r)

        carry = tuple(r[0] for r in save_refs)
        seq = tuple(r[...].astype(F32) for r in seq_refs)
        par = tuple(r[...].astype(F32) for r in par_refs)

        def f(carry, dseq, dpar):
            return fn(carry, (*dseq, *seq[n_dseq:]), (*dpar, *par[n_dpar:]))

        _, vjp = jax.vjp(f, carry, seq[:n_dseq], par[:n_dpar])
        d_carry, d_seq, d_par = vjp((tuple(r[...] for r in dcarry_refs),
                                     tuple(r[...].astype(F32) for r in dout_refs)))
        for r, v in zip(dseq_refs, d_seq):
            r[...] = v.astype(r.dtype)
        for r, v in zip(dpar_refs, d_par):
            r[...] += v
        for r, v in zip(dcarry_refs, d_carry):
            r[...] = v

    rev = lambda i: (n - 1 - i, 0)
    res = pl.pallas_call(
        body, name=name, grid=(n,),
        in_specs=([pl.BlockSpec((tile, s.shape[1]), rev) for s in seqs] + [_full_spec(p) for p in params]
                  + [pl.BlockSpec((1, *s.shape[1:]), lambda i: (n - 1 - i, 0, 0)) for s in saved]
                  + [pl.BlockSpec((tile, d.shape[1]), rev) for d in douts]),
        out_specs=([pl.BlockSpec((tile, s.shape[1]), rev) for s in seqs[:n_dseq]]
                   + [_full_spec(p) for p in params[:n_dpar]]),
        out_shape=([jax.ShapeDtypeStruct((rows, s.shape[1]), dt) for s, dt in zip(seqs[:n_dseq], dseq_dtypes)]
                   + [jax.ShapeDtypeStruct(p.shape, F32) for p in params[:n_dpar]]),
        scratch_shapes=[pltpu.VMEM(s.shape[1:], F32) for s in saved],
        compiler_params=_cparams("arbitrary"),
    )(*seqs, *params, *saved, *douts)
    return res[:n_dseq], res[n_dseq:]


def _tile_of(dim, pref):
    if dim <= pref:
        return dim
    best = max((t for t in range(128, pref + 1, 128) if dim % t == 0), default=None)
    if best is None or (best < 512 and dim <= 2304):
        return dim
    return best


def _row_tile(rows, pref):
    if rows <= pref:
        return rows
    return max(t for t in range(8, pref + 1, 8) if rows % t == 0)


def matmul(a, b, name, ta=False, tb=False, out_dtype=F32, add=None, bias=None, relu2=False, relu2_of=None,
           tm=1024, tn=1024, tk=1024):
    m, k = (a.shape[1], a.shape[0]) if ta else a.shape
    n = b.shape[0] if tb else b.shape[1]
    assert k == (b.shape[1] if tb else b.shape[0])
    tm, tn, tk = _tile_of(m, tm), _tile_of(n, tn), _tile_of(k, tk)
    nm, nn, nk = m // tm, n // tn, k // tk
    assert nk == 1 or (out_dtype == F32 and not relu2 and relu2_of is None)
    dn = (((0 if ta else 1,), (1 if tb else 0,)), ((), ()))
    has_add, has_bias, has_u = add is not None, bias is not None, relu2_of is not None
    n_inner = a.size * a.dtype.itemsize * (nn - 1) >= b.size * b.dtype.itemsize * (nm - 1)
    ij = (lambda g0, g1: (g0, g1)) if n_inner else (lambda g0, g1: (g1, g0))

    def body(*refs):
        a_ref, b_ref, refs = refs[0], refs[1], refs[2:]
        add_ref, refs = (refs[0], refs[1:]) if has_add else (None, refs)
        bias_ref, refs = (refs[0], refs[1:]) if has_bias else (None, refs)
        u_ref, refs = (refs[0], refs[1:]) if has_u else (None, refs)
        o_ref = refs[0]
        r = lax.dot_general(a_ref[...].astype(BF16), b_ref[...].astype(BF16), dn, preferred_element_type=F32)

        def first():
            v = r
            if has_add:
                v = v + add_ref[...]
            if has_bias:
                v = v + bias_ref[...]
            if has_u:
                v = v * (2.0 * jnp.maximum(u_ref[...], 0.0))
            o_ref[...] = v.astype(o_ref.dtype)
            if relu2:
                p = jnp.maximum(v, 0.0)
                refs[1][...] = (p * p).astype(BF16)

        if nk == 1:
            first()
        else:
            pl.when(pl.program_id(2) == 0)(first)

            @pl.when(pl.program_id(2) > 0)
            def _():
                o_ref[...] += r

    def spec(shape, fn):
        return pl.BlockSpec(shape, lambda g0, g1, l: fn(*ij(g0, g1), l))

    a_spec = spec((tk, tm), lambda i, j, l: (l, i)) if ta else spec((tm, tk), lambda i, j, l: (i, l))
    b_spec = spec((tn, tk), lambda i, j, l: (j, l)) if tb else spec((tk, tn), lambda i, j, l: (l, j))
    o_spec = spec((tm, tn), lambda i, j, l: (i, j))
    in_specs, args = [a_spec, b_spec], [a, b]
    if has_add:
        in_specs.append(o_spec)
        args.append(add)
    if has_bias:
        in_specs.append(spec((1, tn), lambda i, j, l: (0, j)))
        args.append(bias)
    if has_u:
        in_specs.append(o_spec)
        args.append(relu2_of)
    out_shape = [jax.ShapeDtypeStruct((m, n), out_dtype)] + ([jax.ShapeDtypeStruct((m, n), BF16)] if relu2 else [])
    res = pl.pallas_call(
        body, name=name, grid=(nm, nn, nk) if n_inner else (nn, nm, nk), in_specs=in_specs,
        out_specs=[o_spec] * len(out_shape), out_shape=out_shape,
        compiler_params=_cparams("parallel", "parallel", "arbitrary"),
    )(*args)
    return res if relu2 else res[0]


LIN_TILE = 512


def linscan_fwd(a, u, name):
    rows, w = a.shape
    tile = min(LIN_TILE, rows)

    def body(a_ref, u_ref, h_ref, hc):
        @pl.when(pl.program_id(0) == 0)
        def _():
            hc[...] = jnp.zeros_like(hc)

        row = lax.broadcasted_iota(jnp.int32, (8, 1), 0)

        def group(k, h_in):
            rows8 = pl.ds(pl.multiple_of(k * 8, 8), 8)
            pa, pu = a_ref[rows8, :], u_ref[rows8, :]
            for d in (1, 2, 4):
                pu = pu + pa * jnp.where(row >= d, pltpu.roll(pu, d, 0), 0.0)
                pa = pa * jnp.where(row >= d, pltpu.roll(pa, d, 0), 1.0)
            h_ref[rows8, :] = pa * h_in + pu
            return h_ref[pl.ds(k * 8 + 7, 1), :]

        hc[...] = lax.fori_loop(0, tile // 8, group, hc[...], unroll=4)

    spec = pl.BlockSpec((tile, w), lambda i: (i, 0))
    return pl.pallas_call(
        body, name=name, grid=(rows // tile,), in_specs=[spec, spec], out_specs=spec,
        out_shape=jax.ShapeDtypeStruct((rows, w), F32), scratch_shapes=[pltpu.VMEM((1, w), F32)],
        compiler_params=_cparams("arbitrary"),
    )(a, u)


def linscan_bwd(a, hs, dh, name):
    rows, w = a.shape
    tile = min(LIN_TILE, rows)
    n = rows // tile
    per = tile // 8

    def body(a_ref, h_ref, hprev_ref, dh_ref, da_ref, du_ref, cc):
        i = pl.program_id(0)

        @pl.when(i == 0)
        def _():
            cc[...] = jnp.zeros_like(cc)

        row = lax.broadcasted_iota(jnp.int32, (8, 1), 0)
        h_before = jnp.where(i == n - 1, 0.0, hprev_ref[7:8, :])

        def group(s, c_in):
            k = per - 1 - s
            rows8 = pl.ds(pl.multiple_of(k * 8, 8), 8)
            av, hv = a_ref[rows8, :], h_ref[rows8, :]
            pb = jnp.where(row < 7, pltpu.roll(av, 7, 0), 1.0)
            pg = dh_ref[rows8, :]
            for d in (1, 2, 4):
                pg = pg + pb * jnp.where(row < 8 - d, pltpu.roll(pg, 8 - d, 0), 0.0)
                pb = pb * jnp.where(row < 8 - d, pltpu.roll(pb, 8 - d, 0), 1.0)
            g = pg + pb * c_in
            du_ref[rows8, :] = g
            h_prev = jnp.where(k == 0, h_before, h_ref[pl.ds(jnp.maximum(k * 8 - 1, 0), 1), :])
            da_ref[rows8, :] = g * jnp.where(row >= 1, pltpu.roll(hv, 1, 0), h_prev)
            return a_ref[pl.ds(k * 8, 1), :] * du_ref[pl.ds(k * 8, 1), :]

        cc[...] = lax.fori_loop(0, per, group, cc[...], unroll=4)

    rev = pl.BlockSpec((tile, w), lambda i: (n - 1 - i, 0))
    prev = pl.BlockSpec((8, w), lambda i: (jnp.maximum((n - 1 - i) * per - 1, 0), 0))
    return pl.pallas_call(
        body, name=name, grid=(n,), in_specs=[rev, rev, prev, rev], out_specs=[rev, rev],
        out_shape=[jax.ShapeDtypeStruct((rows, w), F32)] * 2, scratch_shapes=[pltpu.VMEM((1, w), F32)],
        compiler_params=_cparams("arbitrary"),
    )(a, hs, hs, dh)


def loss_head(x, target, w, name):
    rows, d = x.shape
    tile = min(512, rows)

    def body(x_ref, t_ref, w_ref, loss_ref, dx_ref, dw_ref):
        @pl.when(pl.program_id(0) == 0)
        def _():
            loss_ref[...] = jnp.zeros_like(loss_ref)
            dw_ref[...] = jnp.zeros_like(dw_ref)

        tv = t_ref[...]

        def f(xv, wv):
            e = _rms(xv, wv) - tv
            return 0.5 * jnp.sum(jnp.mean(e * e, axis=-1, keepdims=True), axis=0, keepdims=True)

        val, vjp = jax.vjp(f, x_ref[...], w_ref[...])
        dxv, dwv = vjp(jnp.ones((1, 1), F32))
        loss_ref[...] += jnp.broadcast_to(val, loss_ref.shape)
        dx_ref[...] = dxv
        dw_ref[...] += dwv

    spec = pl.BlockSpec((tile, d), lambda i: (i, 0))
    return pl.pallas_call(
        body, name=name, grid=(rows // tile,), in_specs=[spec, spec, _full_spec(w)],
        out_specs=[pl.BlockSpec((8, 128), lambda i: (0, 0)), spec, _full_spec(w)],
        out_shape=[jax.ShapeDtypeStruct((8, 128), F32), jax.ShapeDtypeStruct((rows, d), F32),
                   jax.ShapeDtypeStruct(w.shape, F32)],
        compiler_params=_cparams("arbitrary"),
    )(x, target, w)


def adamw(g, w, m, v, name, copy_g=False):
    layers, rows, cols = g.shape
    tile = _row_tile(rows, 256)
    n_out = 4 if copy_g else 3

    def body(g_ref, w_ref, m_ref, v_ref, d_ref, nm_ref, nv_ref, *g_out):
        gv = g_ref[...]
        if copy_g:
            g_out[0][...] = gv
        nm = ADAM_B1 * m_ref[...] + (1.0 - ADAM_B1) * gv
        nv = ADAM_B2 * v_ref[...] + (1.0 - ADAM_B2) * (gv * gv)
        m_hat = nm / (1.0 - ADAM_B1 ** ADAM_STEP)
        v_hat = nv / (1.0 - ADAM_B2 ** ADAM_STEP)
        d_ref[...] = -ADAM_LR * (m_hat / (jnp.sqrt(v_hat) + ADAM_EPS) + ADAM_WD * w_ref[...])
        nm_ref[...] = nm
        nv_ref[...] = nv

    spec = pl.BlockSpec((None, tile, cols), lambda l, i: (l, i, 0))
    return pl.pallas_call(
        body, name=name, grid=(layers, rows // tile), in_specs=[spec] * 4, out_specs=[spec] * n_out,
        out_shape=[jax.ShapeDtypeStruct((layers, rows, cols), F32)] * n_out,
        compiler_params=_cparams("parallel", "parallel"),
    )(g, w, m, v)


def add_cast(g0, g1, sib, place, name):
    shape = sib.shape
    g0, g1, sib = (a.reshape(-1, shape[-1]) for a in (g0, g1, sib))
    rows, cols = sib.shape
    tile = _row_tile(rows, max(8, min(256, (512 * 1024) // cols)))

    def body(k_ref, g0_ref, g1_ref, s_ref, o_ref, ob_ref):
        s = jnp.where(k_ref[1] == 0, g0_ref[...], g1_ref[...]) + s_ref[...]
        o_ref[...] = s
        ob_ref[...] = s.astype(BF16)

    spec = pl.BlockSpec((tile, cols), lambda i, k: (i, 0))
    s, sb = pl.pallas_call(
        body, name=name,
        grid_spec=pltpu.PrefetchScalarGridSpec(
            num_scalar_prefetch=1, grid=(rows // tile,),
            in_specs=[pl.BlockSpec((tile, cols), lambda i, k: (i * (1 - k[1]), 0)),
                      pl.BlockSpec((tile, cols), lambda i, k: (i * k[1], 0)), spec],
            out_specs=[spec, spec]),
        out_shape=[jax.ShapeDtypeStruct((rows, cols), F32), jax.ShapeDtypeStruct((rows, cols), BF16)],
        compiler_params=_cparams("arbitrary"),
    )(place, g0, g1, sib)
    return s.reshape(shape), sb.reshape(shape)


def sum4(own, recv, by_cols, place, name):
    _, r, c = recv.shape
    tile = _row_tile(r, 256)
    nt = r // tile
    own_map = (lambda i, k: (i, k[0])) if by_cols else (lambda i, k: (k[0] * nt + i, 0))

    def body(k_ref, own_ref, recv_ref, o_ref):
        o_ref[...] = ((own_ref[...] + recv_ref[0].astype(F32)) + recv_ref[1].astype(F32)) + recv_ref[2].astype(F32)

    return pl.pallas_call(
        body, name=name,
        grid_spec=pltpu.PrefetchScalarGridSpec(
            num_scalar_prefetch=1, grid=(nt,),
            in_specs=[pl.BlockSpec((tile, c), own_map), pl.BlockSpec((3, tile, c), lambda i, k: (0, i, 0))],
            out_specs=pl.BlockSpec((None, tile, c), lambda i, k: (k[1], i, 0))),
        out_shape=jax.ShapeDtypeStruct((2, r, c), F32),
        compiler_params=_cparams("arbitrary"),
    )(place, own, recv)


def add8(parts, name):
    _, rows, cols = parts.shape

    def body(p_ref, o_ref):
        acc = p_ref[0]
        for k in range(1, 8):
            acc = acc + p_ref[k]
        o_ref[...] = acc

    return pl.pallas_call(
        body, name=name, in_specs=[pl.BlockSpec(memory_space=pltpu.VMEM)],
        out_specs=pl.BlockSpec(memory_space=pltpu.VMEM),
        out_shape=jax.ShapeDtypeStruct((rows, cols), F32),
    )(parts)


def _place():
    return lax.axis_index("x"), lax.axis_index("y"), lax.axis_index("c")


def _other_chips(x, y):
    return [(1 - x, y), (x, 1 - y), (1 - x, 1 - y)]


_ANY = pl.BlockSpec(memory_space=pl.ANY)


BIG_LAYOUT = (("ada_w", "col", (1024, 6144)), ("w_in", "chip", (4, 1024, 1924)), ("w_branch", "col", (3, 512, 1024)),
              ("w_out", "row", (1024, 1024)), ("w_up", "col", (1024, 4096)), ("w_down", "row", (4096, 1024)))
N_BIG = len(BIG_LAYOUT)
REDUCE_LAYOUT = BIG_LAYOUT[1:]


def _local_shape(kind, full):
    if kind == "col":
        return (*full[:-1], full[-1] // 4)
    if kind == "row":
        return (full[0] // 4, *full[1:])
    return full[1:]


def _window(ref, kind, k, local):
    if kind == "chip":
        return ref.at[k]
    if kind == "row":
        return ref.at[pl.ds(pl.multiple_of(k * local[0], 8), local[0])]
    idx = (slice(None),) * (len(local) - 1) + (pl.ds(pl.multiple_of(k * local[-1], 128), local[-1]),)
    return ref.at[idx]


def _dma_call(body, name, n_in, out_shape, sems, aliases=None):
    return pl.pallas_call(
        body, name=name, in_specs=[_ANY] * n_in, out_specs=[_ANY] * len(out_shape), out_shape=out_shape,
        scratch_shapes=[pltpu.SemaphoreType.DMA((n,)) for n in sems],
        input_output_aliases=aliases or {},
        compiler_params=pltpu.CompilerParams(has_side_effects=True))


def _remote(src, dst, send_sem, recv_sem, to):
    return pltpu.make_async_remote_copy(src_ref=src, dst_ref=dst, send_sem=send_sem, recv_sem=recv_sem,
                                        device_id=to, device_id_type=MESH)


def gather_big(shards, name):
    locals_ = [_local_shape(kind, full) for _, kind, full in BIG_LAYOUT]

    def body(*refs):
        sh, refs = refs[:N_BIG], refs[N_BIG:]
        full, refs = (refs[:N_BIG], refs[N_BIG:2 * N_BIG]), refs[2 * N_BIG:]
        send_sems, recv_sems, local_sems, pass_send, pass_recv = refs
        x, y, c = _place()
        me = 2 * x + y
        chips = _other_chips(x, y)
        for cc in (0, 1):
            @pl.when(c == cc)
            def _():
                win = lambda n, k: _window(full[cc][n], BIG_LAYOUT[n][1], k, locals_[n])
                mine, sends = [], []
                for n in range(N_BIG):
                    mine.append(pltpu.make_async_copy(sh[n].at[cc], win(n, me), local_sems.at[n]))
                    mine[n].start()
                    for j, chip in enumerate(chips):
                        sends.append(_remote(sh[n].at[cc], win(n, me), send_sems.at[3 * n + j],
                                             recv_sems.at[3 * n + j], (chip[0], chip[1], c)))
                        sends[-1].start()
                for n in range(N_BIG):
                    for j, chip in enumerate(chips):
                        _remote(sh[n].at[cc], win(n, 2 * chip[0] + chip[1]), send_sems.at[3 * n + j],
                                recv_sems.at[3 * n + j], (chip[0], chip[1], c)).wait_recv()
                    mine[n].wait()
                    sends.append(_remote(full[cc][n], full[cc][n], pass_send.at[n], pass_recv.at[n], (x, y, 1 - c)))
                    sends[-1].start()
                for n in range(N_BIG):
                    _remote(full[1 - cc][n], full[1 - cc][n], pass_send.at[n], pass_recv.at[n],
                            (x, y, 1 - c)).wait_recv()
                for cp in sends:
                    cp.wait_send()

    out_shape = [jax.ShapeDtypeStruct(full, BF16) for _, _, full in BIG_LAYOUT] * 2
    res = _dma_call(body, name, N_BIG, out_shape, (3 * N_BIG, 3 * N_BIG, N_BIG, N_BIG, N_BIG))(*shards)
    return res[:N_BIG], res[N_BIG:]


def reduce_d2d(g0, g1, name):
    nb = len(g0)

    def body(*refs):
        g, refs = (refs[:nb], refs[nb:2 * nb]), refs[2 * nb:]
        sib, (send_sems, recv_sems) = refs[:nb], refs[nb:]
        x, y, c = _place()
        for cc in (0, 1):
            @pl.when(c == cc)
            def _():
                sends = [_remote(g[1 - cc][n], sib[n], send_sems.at[n], recv_sems.at[n], (x, y, 1 - c))
                         for n in range(nb)]
                for cp in sends:
                    cp.start()
                for cp in sends:
                    cp.wait_recv()
                for cp in sends:
                    cp.wait_send()

    out_shape = [jax.ShapeDtypeStruct(a.shape, a.dtype) for a in g0]
    return _dma_call(body, name, 2 * nb, out_shape, (nb, nb))(*g0, *g1)


def reduce_ici(sums, layout, name):
    nb = len(sums)
    locals_ = [_local_shape(kind, full) for _, kind, full in layout]

    def body(*refs):
        src, recv, (send_sems, recv_sems) = refs[:nb], refs[nb:2 * nb], refs[2 * nb:]
        x, y, c = _place()
        chips = _other_chips(x, y)
        copies = []
        for n in range(nb):
            for j, chip in enumerate(chips):
                cp = _remote(_window(src[n], layout[n][1], 2 * chip[0] + chip[1], locals_[n]), recv[n].at[j],
                             send_sems.at[3 * n + j], recv_sems.at[3 * n + j], (chip[0], chip[1], c))
                cp.start()
                copies.append(cp)
        for cp in copies:
            cp.wait_recv()
        for cp in copies:
            cp.wait_send()

    out_shape = [jax.ShapeDtypeStruct((3, *ls), a.dtype) for ls, a in zip(locals_, sums)]
    return _dma_call(body, name, nb, out_shape, (3 * nb, 3 * nb))(*sums)


def share_d2d(finals, name):
    nb = len(finals)

    def body(*refs):
        out, (send_sems, recv_sems) = refs[nb:2 * nb], refs[2 * nb:]
        x, y, c = _place()
        sends = [_remote(out[n].at[c], out[n].at[c], send_sems.at[n], recv_sems.at[n], (x, y, 1 - c))
                 for n in range(nb)]
        for cp in sends:
            cp.start()
        for n in range(nb):
            _remote(out[n].at[c], out[n].at[1 - c], send_sems.at[n], recv_sems.at[n], (x, y, 1 - c)).wait_recv()
        for cp in sends:
            cp.wait_send()

    out_shape = [jax.ShapeDtypeStruct(a.shape, a.dtype) for a in finals]
    return _dma_call(body, name, nb, out_shape, (nb, nb), aliases={n: n for n in range(nb)})(*finals)


def allgather8(block, name):
    m_per, n = block.shape

    def body(x_ref, out_ref, send_sems, recv_sems, local_sem):
        x, y, c = _place()
        me, sibling = (x, y, c), (x, y, 1 - c)
        chips = _other_chips(x, y)

        def rows(px, py, pc):
            return out_ref.at[pl.ds((4 * px + 2 * py + pc) * m_per, m_per), :]

        def copy(k, blk, to, src=None):
            return pltpu.make_async_remote_copy(
                src_ref=rows(*blk) if src is None else src, dst_ref=rows(*blk), send_sem=send_sems.at[k],
                recv_sem=recv_sems.at[k], device_id=to, device_id_type=MESH)

        mine = pltpu.make_async_copy(x_ref, rows(*me), local_sem)
        mine.start()
        first = [copy(0, me, sibling, src=x_ref)]
        first += [copy(1 + j, me, (*chip, c), src=x_ref) for j, chip in enumerate(chips)]
        for cp in first:
            cp.start()
        passed = [copy(4 + j, (*chip, c), sibling) for j, chip in enumerate(chips)]
        for j, chip in enumerate(chips):
            copy(1 + j, (*chip, c), me).wait_recv()
            passed[j].start()
        copy(0, sibling, me).wait_recv()
        for j, chip in enumerate(chips):
            copy(4 + j, (*chip, 1 - c), me).wait_recv()
        for cp in first + passed:
            cp.wait_send()
        mine.wait()

    return pl.pallas_call(
        body, name=name, in_specs=[pl.BlockSpec(memory_space=pltpu.VMEM)],
        out_specs=pl.BlockSpec(memory_space=pltpu.VMEM),
        out_shape=jax.ShapeDtypeStruct((8 * m_per, n), block.dtype),
        scratch_shapes=[pltpu.SemaphoreType.DMA((7,)), pltpu.SemaphoreType.DMA((7,)), pltpu.SemaphoreType.DMA],
    )(block)


CONV = ("gdn_conv_w", "ssd_conv_w", "lru_conv_w")
SMALL = ("ada_b", "norm_mix", "gdn_a_log", "gdn_dt_bias", "gdn_norm", "ssd_conv_b", "ssd_a_log", "ssd_dt_bias",
         "ssd_d", "ssd_norm", "lru_conv_b", "lru_w_a", "lru_b_a", "lru_w_x", "lru_b_x", "lru_lambda", "norm_mlp",
         "final_norm")
WEIGHTS = ("ada_w", "ada_b", "norm_mix", "w_in", "gdn_conv_w", "gdn_a_log", "gdn_dt_bias", "gdn_norm", "ssd_conv_w",
           "ssd_conv_b", "ssd_a_log", "ssd_dt_bias", "ssd_d", "ssd_norm", "lru_conv_w", "lru_conv_b", "lru_w_a",
           "lru_b_a", "lru_w_x", "lru_b_x", "lru_lambda", "w_branch", "w_out", "norm_mlp", "w_up", "w_down",
           "final_norm")
PACK_COLS = 1024


def _pack_rows(shape):
    return -(-math.prod(shape) // PACK_COLS)


def _pack(arrays, dtype):
    parts = []
    for a in arrays:
        flat = a.reshape(-1).astype(dtype)
        pad = _pack_rows(a.shape) * PACK_COLS - flat.shape[0]
        parts.append((jnp.concatenate([flat, jnp.zeros((pad,), dtype)]) if pad else flat).reshape(-1, PACK_COLS))
    rows = sum(p.shape[0] for p in parts)
    if rows % 8:
        parts.append(jnp.zeros((8 - rows % 8, PACK_COLS), dtype))
    return jnp.concatenate(parts, axis=0)


def _unpack(pack, shapes):
    out, o = [], 0
    for s in shapes:
        r = _pack_rows(s)
        out.append(pack[o:o + r].reshape(-1)[:math.prod(s)].reshape(s))
        o += r
    return out


def _split_w_in(w4):
    w = jnp.concatenate([w4[k] for k in range(4)], axis=1)
    pad = jnp.zeros((w.shape[0], 120), w.dtype)
    gdn = jnp.concatenate([w[:, 0:2056], pad], axis=1)
    ssd = jnp.concatenate([w[:, 2056:2568], w[:, 3080:3592], w[:, 2568:3080], w[:, 3592:3600], pad], axis=1)
    return gdn, ssd, w[:, 3600:4112], w[:, 4112:4624], w[:, 4624:7696]


def _join_w_in(gdn, ssd, lx, lg, gate):
    w = jnp.concatenate([gdn[:, 0:2056], ssd[:, 0:512], ssd[:, 1024:1536], ssd[:, 512:1024], ssd[:, 1536:1544],
                         lx, lg, gate], axis=1)
    return jnp.stack([w[:, k * 1924:(k + 1) * 1924] for k in range(4)])


def _lanes(v, at, width=128):
    return jnp.zeros((1, width), F32).at[0, at:at + v.shape[0]].set(v)


def _block_diag(w):
    return (jnp.eye(8, dtype=w.dtype)[:, None, :, None] * w[:, :, None, :]).reshape(512, 512)


def _diag_blocks(w):
    return jnp.stack([w[n * 64:(n + 1) * 64, n * 64:(n + 1) * 64] for n in range(8)])


TOK_TILE = 512
WIDE_TILE = 256


def _layer_params(p, big, l):
    row = lambda v: v.reshape(1, -1)
    b = dict(zip((n for n, _, _ in BIG_LAYOUT), big))
    gdn = (p["gdn_conv_w"][l], _lanes(p["gdn_a_log"][l], 4), _lanes(p["gdn_dt_bias"][l], 4), row(p["gdn_norm"][l]))
    ssd = (p["ssd_conv_w"][l], row(p["ssd_conv_b"][l]), _lanes(p["ssd_a_log"][l], 0), _lanes(p["ssd_dt_bias"][l], 0),
           row(jnp.repeat(p["ssd_d"][l], 64)), row(p["ssd_norm"][l]))
    lru = (p["lru_conv_w"][l], row(p["lru_conv_b"][l]), _block_diag(p["lru_w_a"][l]), row(p["lru_b_a"][l]),
           _block_diag(p["lru_w_x"][l]), row(p["lru_b_x"][l]), row(p["lru_lambda"][l]))
    return dict(gdn=gdn, ssd=ssd, lru=lru, w_in=_split_w_in(b["w_in"]),
                wb=tuple(b["w_branch"][r] for r in range(3)), w_out=b["w_out"], w_up=b["w_up"],
                w_down=b["w_down"], ada_w=b["ada_w"], ada_b=row(p["ada_b"][l]),
                norm_mix=row(p["norm_mix"][l]), norm_mlp=row(p["norm_mlp"][l]))


def _layer_fwd(x, silu_c, lp, l):
    nm = lambda s: f"l{l}_{s}"
    mod = matmul(silu_c, lp["ada_w"], nm("mod"), bias=lp["ada_b"])
    sh1, sc1, gt1, sh2, sc2, gt2 = (mod[0:1, k * D_MODEL:(k + 1) * D_MODEL] for k in range(N_MOD))
    (h,), _ = scan_fwd(norm1_fn, nm("norm1"), TOK_TILE, [x], [lp["norm_mix"], sc1, sh1], [], [(D_MODEL, BF16)])
    w_gdn, w_ssd, w_lx, w_lg, w_gate = lp["w_in"]
    p_gdn = matmul(h, w_gdn, nm("in_gdn"))
    p_ssd = matmul(h, w_ssd, nm("in_ssd"))
    p_lx = matmul(h, w_lx, nm("in_lx"))
    p_lg = matmul(h, w_lg, nm("in_lg"))
    p_gate = matmul(h, w_gate, nm("in_gate"))
    (ya,), sv_gdn = scan_fwd(gdn_fn, nm("gdn"), CHUNK, [p_gdn], lp["gdn"], [(128, 128)] * 4 + [(8, 1536)],
                             [(512, F32)], save_carry=True)
    (yb,), sv_ssd = scan_fwd(ssd_fn, nm("ssd"), CHUNK, [p_ssd], lp["ssd"], [(128, 128)] * 4 + [(8, 1024)],
                             [(512, F32)], save_carry=True)
    (a, u), sv_lru = scan_fwd(lru_in_fn, nm("lru_in"), TOK_TILE, [p_lx], lp["lru"], [(8, 512)],
                              [(512, F32), (512, F32)], save_carry=True)
    hs = linscan_fwd(a, u, nm("lru_scan"))
    (yc,), _ = scan_fwd(lru_out_fn, nm("lru_out"), TOK_TILE, [hs, p_lg], [], [], [(512, F32)])
    (merged,), _ = scan_fwd(merge_fn, nm("merge"), WIDE_TILE, [ya, yb, yc, p_gate], lp["wb"], [], [(D_MODEL, BF16)])
    mix = matmul(merged, lp["w_out"], nm("out"))
    (h2, x1), _ = scan_fwd(resid_norm_fn, nm("norm2"), TOK_TILE, [x, mix], [gt1, lp["norm_mlp"], sc2, sh2], [],
                           [(D_MODEL, BF16), (D_MODEL, F32)])
    up, act = matmul(h2, lp["w_up"], nm("up"), relu2=True)
    dn = matmul(act, lp["w_down"], nm("down"))
    (x2,), _ = scan_fwd(resid_fn, nm("resid"), TOK_TILE, [x1, dn], [gt2], [], [(D_MODEL, F32)])
    saved = dict(x=x, h=h, p_gdn=p_gdn, p_ssd=p_ssd, p_lx=p_lx, p_lg=p_lg, p_gate=p_gate, sv_gdn=sv_gdn,
                 sv_ssd=sv_ssd, sv_lru=sv_lru, a=a, hs=hs, ya=ya, yb=yb, yc=yc, merged=merged, mix=mix, x1=x1,
                 h2=h2, up=up, act=act, dn=dn, mod=(sh1, sc1, gt1, sh2, sc2, gt2))
    return x2, saved


def _layer_bwd(d_x2, silu_c, lp, sv, l):
    nm = lambda s: f"l{l}_b_{s}"
    sh1, sc1, gt1, sh2, sc2, gt2 = sv["mod"]
    (d_x1, d_dn), (d_gt2,) = scan_bwd(resid_fn, nm("resid"), TOK_TILE, [sv["x1"], sv["dn"]], [gt2], [], [d_x2], 2, 1,
                                      [F32, BF16])
    d_up = matmul(d_dn, lp["w_down"], nm("down_x"), tb=True, relu2_of=sv["up"], out_dtype=BF16)
    g_w_down = matmul(sv["act"], d_dn, nm("down_w"), ta=True)
    d_h2 = matmul(d_up, lp["w_up"], nm("up_x"), tb=True)
    g_w_up = matmul(sv["h2"], d_up, nm("up_w"), ta=True)
    (d_x, d_mix), (d_gt1, g_norm_mlp, d_sc2, d_sh2) = scan_bwd(
        resid_norm_fn, nm("norm2"), TOK_TILE, [sv["x"], sv["mix"]], [gt1, lp["norm_mlp"], sc2, sh2], [],
        [d_h2, d_x1], 2, 4, [F32, BF16])
    d_merged = matmul(d_mix, lp["w_out"], nm("out_x"), tb=True)
    g_w_out = matmul(sv["merged"], d_mix, nm("out_w"), ta=True)
    (d_ya, d_yb, d_yc, d_pgate), g_wb = scan_bwd(
        merge_fn, nm("merge"), WIDE_TILE, [sv["ya"], sv["yb"], sv["yc"], sv["p_gate"]], lp["wb"], [], [d_merged], 4, 3,
        [F32, F32, F32, BF16])
    (d_hs, d_plg), _ = scan_bwd(lru_out_fn, nm("lru_out"), TOK_TILE, [sv["hs"], sv["p_lg"]], [], [], [d_yc], 2, 0,
                                [F32, BF16])
    d_a, d_u = linscan_bwd(sv["a"], sv["hs"], d_hs, nm("lru_scan"))
    (d_plx,), g_lru = scan_bwd(lru_in_fn, nm("lru_in"), TOK_TILE, [sv["p_lx"]], lp["lru"], sv["sv_lru"],
                               [d_a, d_u], 1, 7, [BF16])
    (d_pssd,), g_ssd = scan_bwd(ssd_fn, nm("ssd"), CHUNK, [sv["p_ssd"]], lp["ssd"], sv["sv_ssd"], [d_yb], 1, 6,
                                [BF16])
    (d_pgdn,), g_gdn = scan_bwd(gdn_fn, nm("gdn"), CHUNK, [sv["p_gdn"]], lp["gdn"], sv["sv_gdn"], [d_ya], 1, 4,
                                [BF16])
    d_h = None
    g_w_in = []
    for tag, dp, w in zip(("gdn", "ssd", "lx", "lg", "gate"), (d_pgdn, d_pssd, d_plx, d_plg, d_pgate), lp["w_in"]):
        d_h = matmul(dp, w, nm("in_x_" + tag), tb=True, add=d_h)
        g_w_in.append(matmul(sv["h"], dp, nm("in_w_" + tag), ta=True))
    (d_x0,), (g_norm_mix, d_sc1, d_sh1) = scan_bwd(norm1_fn, nm("norm1"), TOK_TILE, [sv["x"]],
                                                   [lp["norm_mix"], sc1, sh1], [], [d_h, d_x], 1, 3)
    d_mod = jnp.concatenate([d_sh1, d_sc1, d_gt1, d_sh2, d_sc2, d_gt2], axis=1)
    flat = lambda v: v.reshape(-1)
    grads = dict(
        ada_b=flat(d_mod), norm_mix=flat(g_norm_mix),
        gdn_conv_w=g_gdn[0], gdn_a_log=g_gdn[1][0, 4:8], gdn_dt_bias=g_gdn[2][0, 4:8], gdn_norm=flat(g_gdn[3]),
        ssd_conv_w=g_ssd[0], ssd_conv_b=flat(g_ssd[1]), ssd_a_log=g_ssd[2][0, 0:8], ssd_dt_bias=g_ssd[3][0, 0:8],
        ssd_d=g_ssd[4].reshape(8, 64).sum(axis=1), ssd_norm=flat(g_ssd[5]),
        lru_conv_w=g_lru[0], lru_conv_b=flat(g_lru[1]), lru_w_a=_diag_blocks(g_lru[2]), lru_b_a=flat(g_lru[3]),
        lru_w_x=_diag_blocks(g_lru[4]), lru_b_x=flat(g_lru[5]), lru_lambda=flat(g_lru[6]),
        norm_mlp=flat(g_norm_mlp))
    big = [_join_w_in(*g_w_in), jnp.stack(g_wb), g_w_out, g_w_up, g_w_down]
    return d_x0, grads, big


def local_step(x, c, target, p, big):
    c8 = jnp.concatenate([c, jnp.zeros((7, c.shape[1]), F32)], axis=0)
    (silu_c,), _ = scan_fwd(silu_fn, "silu_c", 8, [c8], [], [], [(D_MODEL, F32)])
    lps = [_layer_params(p, big[l], l) for l in range(DEPTH)]
    saved = []
    for l in range(DEPTH):
        x, sv = _layer_fwd(x, silu_c, lps[l], l)
        saved.append(sv)
    loss, d_x, g_final = loss_head(x, target, p["final_norm"].reshape(1, -1), "loss_head")
    layer_grads, big_grads = [None] * DEPTH, [None] * DEPTH
    for l in reversed(range(DEPTH)):
        d_x, layer_grads[l], big_grads[l] = _layer_bwd(d_x, silu_c, lps[l], saved[l], l)
    grads = {k: jnp.stack([layer_grads[l][k] for l in range(DEPTH)]) for k in layer_grads[0]}
    grads["final_norm"] = g_final.reshape(-1)
    return loss, d_x, grads, big_grads, silu_c[0]


def _adam_nd(g, w, m, v, name):
    three = lambda a: a.reshape(-1, *a.shape[-2:])
    return tuple(r.reshape(w.shape) for r in adamw(three(g), three(w), three(m), three(v), name, copy_g=True))


def _reduce_big(g0, g1, place):
    sib = reduce_d2d(g0, g1, "reduce_pool")
    pooled = [add_cast(a, b, s, place, "reduce_pool_" + n) for (n, _, _), a, b, s in zip(REDUCE_LAYOUT, g0, g1, sib)]
    recv = reduce_ici([pb for _, pb in pooled], REDUCE_LAYOUT, "reduce_ici")
    finals = []
    for (n, kind, full), (pf, _), r in zip(REDUCE_LAYOUT, pooled, recv):
        local = r.shape[1:]
        own2 = pf.reshape(-1, full[-1])
        r3 = r.reshape(3, -1, local[-1])
        finals.append(sum4(own2, r3, kind == "col", place, "reduce_sum_" + n).reshape(2, *local))
    return share_d2d(finals, "reduce_share")


def _step(w, m, v, x, c, target):
    chip = 2 * lax.axis_index("x") + lax.axis_index("y")
    place = jnp.stack([chip, lax.axis_index("c")]).astype(jnp.int32)
    conv_shapes = [w[n].shape for n in CONV]
    small_shapes = [w[n].shape for n in SMALL]

    big = gather_big([w[n].astype(BF16) for n, _, _ in BIG_LAYOUT], "gather_big")
    conv_all = allgather8(_pack([w[n] for n in CONV], F32), "gather_conv").reshape(8, -1, PACK_COLS)
    conv_parts = [_unpack(conv_all[2 * k], conv_shapes) for k in range(4)]
    p = {n: w[n] for n in SMALL}
    for i, n in enumerate(CONV):
        p[n] = jnp.concatenate([conv_parts[k][i] for k in range(4)], axis=2)

    loss_blk, grad_x, g, big_g, silu_c = local_step(x[0], c, target[0], p, big)
    big_g = dict(zip((n for n, _, _ in REDUCE_LAYOUT), _reduce_big(big_g[0], big_g[1], place)))

    assert SMALL[0] == "ada_b"
    small_pack = _pack([g["ada_b"], silu_c, loss_blk[0, 0:1]] + [g[n] for n in SMALL[1:]] + [g[n] for n in CONV], F32)
    small_all = allgather8(small_pack, "gather_small").reshape(8, -1, PACK_COLS)
    total = _unpack(add8(small_all, "reduce_small"),
                    [small_shapes[0], (D_MODEL,), (1,)] + small_shapes[1:] + [g[n].shape for n in CONV])
    loss = total[2][0]
    small_g = dict(zip(SMALL, [total[0]] + total[3:2 + len(SMALL)]))
    conv_g = {n: lax.dynamic_slice_in_dim(t, chip * w[n].shape[2], w[n].shape[2], axis=2)
              for n, t in zip(CONV, total[2 + len(SMALL):])}

    cols = w["ada_w"].shape[2]
    silu_all = small_all[:, 2 * N_MOD, :]
    big_g["ada_w"] = jnp.stack([
        matmul(silu_all, lax.dynamic_slice_in_dim(small_all[:, N_MOD * l:N_MOD * (l + 1), :].reshape(8, -1),
                                                  chip * cols, cols, axis=1), f"ada_w_grad{l}", ta=True)
        for l in range(DEPTH)])

    grad, delta, new_m, new_v = {}, {}, {}, {}
    for n, _, _ in BIG_LAYOUT:
        delta[n], new_m[n], new_v[n], grad[n] = _adam_nd(big_g[n], w[n], m[n], v[n], "adam_" + n)
    for names, gs, shapes, tag in ((SMALL, small_g, small_shapes, "small"), (CONV, conv_g, conv_shapes, "conv")):
        pk = lambda d: _pack([d[n] for n in names], F32)[None]
        res = adamw(pk(gs), pk(w), pk(m), pk(v), "adam_" + tag)
        for out, r in zip((delta, new_m, new_v), res):
            out.update(zip(names, _unpack(r[0], shapes)))
        grad.update({n: gs[n] for n in names})
    outs = [loss, grad_x[None]]
    for d in (grad, delta, new_m, new_v):
        outs += [d[n] for n in WEIGHTS]
    return tuple(outs)


def kernel(x, c, ada_w, ada_b, norm_mix, w_in, gdn_conv_w, gdn_a_log, gdn_dt_bias, gdn_norm, ssd_conv_w, ssd_conv_b, ssd_a_log, ssd_dt_bias, ssd_d, ssd_norm, lru_conv_w, lru_conv_b, lru_w_a, lru_b_a, lru_w_x, lru_b_x, lru_lambda, w_branch, w_out, norm_mlp, w_up, w_down, final_norm, loss_target, m_ada_w, m_ada_b, m_norm_mix, m_w_in, m_gdn_conv_w, m_gdn_a_log, m_gdn_dt_bias, m_gdn_norm, m_ssd_conv_w, m_ssd_conv_b, m_ssd_a_log, m_ssd_dt_bias, m_ssd_d, m_ssd_norm, m_lru_conv_w, m_lru_conv_b, m_lru_w_a, m_lru_b_a, m_lru_w_x, m_lru_b_x, m_lru_lambda, m_w_branch, m_w_out, m_norm_mlp, m_w_up, m_w_down, m_final_norm, v_ada_w, v_ada_b, v_norm_mix, v_w_in, v_gdn_conv_w, v_gdn_a_log, v_gdn_dt_bias, v_gdn_norm, v_ssd_conv_w, v_ssd_conv_b, v_ssd_a_log, v_ssd_dt_bias, v_ssd_d, v_ssd_norm, v_lru_conv_w, v_lru_conv_b, v_lru_w_a, v_lru_b_a, v_lru_w_x, v_lru_b_x, v_lru_lambda, v_w_branch, v_w_out, v_norm_mlp, v_w_up, v_w_down, v_final_norm):
    given = dict(locals())
    w = {n: given[n] for n in WEIGHTS}
    m = {n: given["m_" + n] for n in WEIGHTS}
    v = {n: given["v_" + n] for n in WEIGHTS}
    return _step(w, m, v, x, c, loss_target)
```

```python
import functools
import math

import jax
import jax.numpy as jnp
from jax import lax
from jax.experimental import pallas as pl
from jax.experimental.pallas import tpu as pltpu

F32 = jnp.float32
BF16 = jnp.bfloat16

D_MODEL = 1024
DEPTH = 2
RMS_EPS = 1e-6
CHUNK = 128
GDN_HEADS = 4
SSD_HEADS = 8
LRU_C = 8.0
D_FF = 4096
N_MOD = 6
W_GDN = 2176
W_SSD = 1664
W_LRU = 512
W_GATE = 3072
ADAM_LR = 0.001
ADAM_B1 = 0.9
ADAM_B2 = 0.999
ADAM_EPS = 1e-08
ADAM_WD = 0.01
ADAM_STEP = 10
VMEM_LIMIT = 56 * 1024 * 1024
MESH = pl.DeviceIdType.MESH


def _dot(a, b, ta, tb):
    dn = (((0 if ta else 1,), (1 if tb else 0,)), ((), ()))
    return lax.dot_general(a.astype(BF16), b.astype(BF16), dn, preferred_element_type=F32)


@functools.partial(jax.custom_vjp, nondiff_argnums=(2, 3))
def mm(a, b, ta, tb):
    return _dot(a, b, ta, tb)


def _mm_fwd(a, b, ta, tb):
    return _dot(a, b, ta, tb), (a, b)


def _mm_bwd(ta, tb, res, g):
    a, b = res
    if not ta and not tb:
        return mm(g, b, False, True), mm(a, g, True, False)
    if not ta and tb:
        return mm(g, b, False, False), mm(g, a, True, False)
    assert ta and not tb
    return mm(b, g, False, True), mm(a, g, False, False)


mm.defvjp(_mm_fwd, _mm_bwd)


def _tri_apply(x, upper):
    t = x.shape[0]
    r = lax.broadcasted_iota(jnp.int32, (t, t), 0)
    c = lax.broadcasted_iota(jnp.int32, (t, t), 1)
    tri = jnp.where((r <= c) if upper else (r >= c), 1.0, 0.0).astype(BF16)
    x1 = x.astype(BF16)
    r1 = x - x1.astype(F32)
    x2 = r1.astype(BF16)
    x3 = (r1 - x2.astype(F32)).astype(BF16)
    d = lambda p: jnp.dot(tri, p, preferred_element_type=F32)
    return (d(x1) + d(x2)) + d(x3)


@jax.custom_vjp
def cumsum_rows(x):
    return _tri_apply(x, False)


cumsum_rows.defvjp(lambda x: (_tri_apply(x, False), None), lambda _, g: (_tri_apply(g, True),))


def _dot_split(a, b):
    a1, b1 = a.astype(BF16), b.astype(BF16)
    a2, b2 = (a - a1.astype(F32)).astype(BF16), (b - b1.astype(F32)).astype(BF16)
    d = lambda p, q: jnp.dot(p, q, preferred_element_type=F32)
    return d(a1, b1) + (d(a1, b2) + d(a2, b1))


def _neumann(ms):
    t = ms[0].shape[0]
    xs = [-m for m in ms]
    qs = [_dot(m, m, False, False) for m in ms]
    n = 2
    while True:
        xs = [x + q + _dot(x, q, False, False) for x, q in zip(xs, qs)]
        n *= 2
        if n >= t:
            break
        qs = [_dot(q, q, False, False) for q in qs]
    rs = [-(x + m + _dot_split(m, x)) for x, m in zip(xs, ms)]
    return [x + r + _dot(x, r, False, False) for x, r in zip(xs, rs)]


@jax.custom_vjp
def tri_solve(ms, rhss):
    return tuple(rhs + _dot(x, rhs, False, False) for x, rhs in zip(_neumann(ms), rhss))


def _tri_solve_fwd(ms, rhss):
    xs = _neumann(ms)
    sols = tuple(rhs + _dot(x, rhs, False, False) for x, rhs in zip(xs, rhss))
    return sols, (tuple(xs), sols)


def _tri_solve_bwd(res, gs):
    xs, sols = res
    d_rhss = tuple(g + _dot(x, g, True, False) for x, g in zip(xs, gs))
    return tuple(-_dot(d, sol, False, True) for d, sol in zip(d_rhss, sols)), d_rhss


tri_solve.defvjp(_tri_solve_fwd, _tri_solve_bwd)


@functools.partial(jax.custom_vjp, nondiff_argnums=(1,))
def split_cols(x, sizes):
    out, o = [], 0
    for s in sizes:
        out.append(x[:, o:o + s])
        o += s
    return tuple(out)


split_cols.defvjp(lambda x, sizes: (split_cols(x, sizes), None),
                  lambda sizes, _, g: (jnp.concatenate(list(g), axis=1),))


@functools.partial(jax.custom_vjp, nondiff_argnums=(1,))
def _last_rows(x, t):
    return x[t - 8:, :]


_last_rows.defvjp(lambda x, t: (_last_rows(x, t), None),
                  lambda t, _, g: (jnp.concatenate([jnp.zeros((t - 8, g.shape[1]), g.dtype), g], axis=0),))


def last8(x):
    return _last_rows(x, x.shape[0])


def _shifted(xp, d, t):
    return (pltpu.roll(xp, d, 0) if d else xp)[8:8 + t, :]


@jax.custom_vjp
def conv4(x, tail, w):
    t = x.shape[0]
    xp = jnp.concatenate([tail, x], axis=0)
    return sum(_shifted(xp, 3 - k, t) * w[k:k + 1, :] for k in range(4))


def _conv4_fwd(x, tail, w):
    return conv4(x, tail, w), (x, tail, w)


def _conv4_bwd(res, g):
    x, tail, w = res
    t = x.shape[0]
    xp = jnp.concatenate([tail, x], axis=0)
    zero8 = jnp.zeros((8, g.shape[1]), g.dtype)
    d_xp = jnp.zeros_like(xp)
    d_w = []
    for k in range(4):
        gk = jnp.concatenate([zero8, g * w[k:k + 1, :]], axis=0)
        d_xp = d_xp + (pltpu.roll(gk, t + 8 - (3 - k), 0) if k < 3 else gk)
        d_w.append(jnp.sum(g * _shifted(xp, 3 - k, t), axis=0, keepdims=True))
    return d_xp[8:, :], d_xp[:8, :], jnp.concatenate(d_w, axis=0)


conv4.defvjp(_conv4_fwd, _conv4_bwd)


def _sigmoid(x):
    return 0.5 * (jnp.tanh(0.5 * x) + 1.0)


def _silu(x):
    return x * _sigmoid(x)


def _softplus(x):
    ax = jnp.where(x > 0, x, -x)
    return jnp.where(x > 0, x, 0.0) + jnp.log(1.0 + jnp.exp(-ax))


def _gelu(x):
    return 0.5 * x * (1.0 + jnp.tanh(math.sqrt(2.0 / math.pi) * (x + 0.044715 * (x * x * x))))


def _expm1(x):
    series = x * (1.0 + x * (0.5 + x * (1.0 / 6.0 + x * (1.0 / 24.0))))
    return jnp.where(jnp.abs(x) < 0.03, series, jnp.exp(x) - 1.0)


def _rms(x, w):
    return x * lax.rsqrt(jnp.mean(x * x, axis=-1, keepdims=True) + RMS_EPS) * w


def _lane_pick(x, j):
    lane = lax.broadcasted_iota(jnp.int32, (1, x.shape[1]), 1)
    return jnp.sum(jnp.where(lane == j, x, 0.0), axis=1, keepdims=True)


def _row_pick(x, j):
    row = lax.broadcasted_iota(jnp.int32, (x.shape[0], 1), 0)
    return jnp.sum(jnp.where(row == j, x, 0.0), axis=0, keepdims=True)


def gdn_fn(carry, seq, params):
    *states, tail = carry
    (tile,) = seq
    conv_w, alog_row, dtb_row, norm_w = params
    t = tile.shape[0]
    qkv_raw, z, sm = split_cols(tile, (1536, 512, 128))
    qkv = _silu(conv4(qkv_raw, tail, conv_w))
    parts = split_cols(qkv, (128,) * 12)
    zs = split_cols(z, (128,) * 4)
    lane = lax.broadcasted_iota(jnp.int32, (1, 128), 1)
    beta_all = _sigmoid(sm)
    g_all = jnp.where((lane >= 4) & (lane < 8), -jnp.exp(alog_row) * _softplus(sm + dtb_row), 0.0)
    gc_all = cumsum_rows(g_all)
    gr_all = gc_all.T
    gl_all = _row_pick(gc_all, t - 1)
    r = lax.broadcasted_iota(jnp.int32, (t, t), 0)
    c = lax.broadcasted_iota(jnp.int32, (t, t), 1)
    heads = range(GDN_HEADS)
    l2 = lambda a: a * lax.rsqrt(jnp.sum(a * a, axis=-1, keepdims=True) + RMS_EPS)
    qn = [l2(parts[h]) * (128.0 ** -0.5) for h in heads]
    kn = [l2(parts[4 + h]) for h in heads]
    beta = [_lane_pick(beta_all, h) for h in heads]
    gc = [_lane_pick(gc_all, 4 + h) for h in heads]
    gl = [_lane_pick(gl_all, 4 + h) for h in heads]
    decay = [jnp.exp(jnp.where(r >= c, gc[h] - _row_pick(gr_all, 4 + h), -1e30)) for h in heads]
    kk = [mm(kn[h], kn[h], False, True) for h in heads]
    qk = [mm(qn[h], kn[h], False, True) for h in heads]
    m = tuple(jnp.where(r > c, beta[h] * kk[h] * decay[h], 0.0) for h in heads)
    eg = [jnp.exp(gc[h]) for h in heads]
    rhs = tuple(jnp.concatenate([beta[h] * parts[8 + h], (beta[h] * eg[h]) * kn[h]], axis=1) for h in heads)
    uw = [split_cols(s, (128, 128)) for s in tri_solve(m, rhs)]
    ws = [mm(uw[h][1], states[h], False, False) for h in heads]
    qs = [mm(qn[h] * eg[h], states[h], False, False) for h in heads]
    v_new = [uw[h][0] - ws[h] for h in heads]
    o = [qs[h] + mm(qk[h] * decay[h], v_new[h], False, False) for h in heads]
    kv = [mm(kn[h] * jnp.exp(gl[h] - gc[h]), v_new[h], True, False) for h in heads]
    new_states = [states[h] * jnp.exp(gl[h]) + kv[h] for h in heads]
    outs = [_rms(o[h], norm_w) * _silu(zs[h]) for h in heads]
    return (*new_states, last8(qkv_raw)), (jnp.concatenate(outs, axis=1),)


def ssd_fn(carry, seq, params):
    *states, tail = carry
    (tile,) = seq
    conv_w, conv_b, alog_row, dtb_row, d_row, norm_w = params
    t = tile.shape[0]
    xbc_raw, z, sm = split_cols(tile, (1024, 512, 128))
    xbc = _silu(conv4(xbc_raw, tail, conv_w) + conv_b)
    x0, x1, x2, x3, b0, b1, c0, c1 = split_cols(xbc, (128,) * 8)
    xs, bs, cs = (x0, x1, x2, x3), (b0, b1), (c0, c1)
    ds = split_cols(d_row, (128,) * 4)
    lane = lax.broadcasted_iota(jnp.int32, (1, 128), 1)
    sub = lax.broadcasted_iota(jnp.int32, (128, 1), 0)
    low = lane < 64
    dt_all = jnp.where(lane < SSD_HEADS, _softplus(sm + dtb_row), 0.0)
    ac_all = cumsum_rows(dt_all * (-jnp.exp(alog_row)))
    ar_all = ac_all.T
    al_all = _row_pick(ac_all, t - 1)
    r = lax.broadcasted_iota(jnp.int32, (t, t), 0)
    c = lax.broadcasted_iota(jnp.int32, (t, t), 1)
    pairs, heads = range(4), range(SSD_HEADS)
    col = [_lane_pick(ac_all, h) for h in heads]
    last = [_lane_pick(al_all, h) for h in heads]
    dt = [_lane_pick(dt_all, h) for h in heads]
    lm = [jnp.exp(jnp.where(r >= c, col[h] - _row_pick(ar_all, h), -1e30)) for h in heads]
    cb = [mm(cs[g], bs[g], False, True) for g in range(2)]
    both = lambda a, b: jnp.where(low, a, b)
    xdt = [xs[p] * both(dt[2 * p], dt[2 * p + 1]) for p in pairs]
    y_off = [mm(cs[p // 2], states[p], False, True) for p in pairs]
    y_lo = [mm(cb[p // 2] * lm[2 * p], jnp.where(low, xdt[p], 0.0), False, False) for p in pairs]
    y_hi = [mm(cb[p // 2] * lm[2 * p + 1], jnp.where(low, 0.0, xdt[p]), False, False) for p in pairs]
    st = [mm(xdt[p] * both(jnp.exp(last[2 * p] - col[2 * p]), jnp.exp(last[2 * p + 1] - col[2 * p + 1])),
             bs[p // 2], True, False) for p in pairs]
    ys = [ds[p] * xs[p] + y_lo[p] + y_hi[p] + y_off[p] * both(jnp.exp(col[2 * p]), jnp.exp(col[2 * p + 1]))
          for p in pairs]
    new_states = [states[p] * jnp.where(sub < 64, jnp.exp(last[2 * p]), jnp.exp(last[2 * p + 1])) + st[p]
                  for p in pairs]
    gz = jnp.concatenate(ys, axis=1) * _silu(z)
    g0, g1 = split_cols(gz, (256, 256))
    n0, n1 = split_cols(norm_w, (256, 256))
    out = jnp.concatenate([_rms(g0, n0), _rms(g1, n1)], axis=1)
    return (*new_states, last8(xbc_raw)), (out,)


def lru_in_fn(carry, seq, params):
    (tail,) = carry
    (x,) = seq
    conv_w, conv_b, w_a, b_a, w_x, b_x, lam = params
    xc = conv4(x, tail, conv_w) + conv_b
    r = _sigmoid(mm(xc, w_a, False, False) + b_a)
    i = _sigmoid(mm(xc, w_x, False, False) + b_x)
    log_a = -LRU_C * r * _softplus(-lam)
    u = jnp.sqrt(-_expm1(2.0 * log_a)) * (i * xc)
    return (last8(x),), (jnp.exp(log_a), u)


def lru_out_fn(carry, seq, params):
    hs, gate = seq
    return (), (hs * _gelu(gate),)


def merge_fn(carry, seq, params):
    ya, yb, yc, gl = seq
    g = split_cols(_sigmoid(gl), (D_MODEL,) * 3)
    merged = sum(g[r] * mm(y, params[r], False, False) for r, y in enumerate((ya, yb, yc)))
    return (), (merged,)


def _adaln(x, w, sc, sh):
    return _rms(x, w) * (1.0 + sc) + sh


def norm1_fn(carry, seq, params):
    (x,) = seq
    return (), (_adaln(x, *params), x)


def resid_norm_fn(carry, seq, params):
    x, mix = seq
    gt, w, sc, sh = params
    x1 = x + gt * mix
    return (), (_adaln(x1, w, sc, sh), x1)


def resid_fn(carry, seq, params):
    x, dn = seq
    (gt,) = params
    return (), (x + gt * dn,)


def silu_fn(carry, seq, params):
    return (), (_silu(seq[0]),)


def _full_spec(a):
    nd = a.ndim
    return pl.BlockSpec(a.shape, lambda i: (0,) * nd)


def _cparams(*sem):
    return pltpu.CompilerParams(dimension_semantics=sem, vmem_limit_bytes=VMEM_LIMIT)


def scan_fwd(fn, name, tile, seqs, params, carry_shapes, outs, save_carry=False):
    rows = seqs[0].shape[0]
    tile = min(tile, rows)
    n = rows // tile
    ns, npar, nc, no = len(seqs), len(params), len(carry_shapes), len(outs)

    def body(*refs):
        seq_refs, refs = refs[:ns], refs[ns:]
        par_refs, refs = refs[:npar], refs[npar:]
        out_refs, refs = refs[:no], refs[no:]
        save_refs, refs = (refs[:nc], refs[nc:]) if save_carry else ((), refs)
        carry_refs = refs

        @pl.when(pl.program_id(0) == 0)
        def _():
            for cr in carry_refs:
                cr[...] = jnp.zeros_like(cr)

        carry = tuple(cr[...] for cr in carry_refs)
        for sr, cv in zip(save_refs, carry):
            sr[0] = cv
        new_carry, res = fn(carry, tuple(r[...].astype(F32) for r in seq_refs),
                            tuple(r[...].astype(F32) for r in par_refs))
        for r, v in zip(out_refs, res):
            r[...] = v.astype(r.dtype)
        for cr, v in zip(carry_refs, new_carry):
            cr[...] = v

    out_shape = [jax.ShapeDtypeStruct((rows, w), dt) for w, dt in outs]
    out_specs = [pl.BlockSpec((tile, w), lambda i: (i, 0)) for w, _ in outs]
    if save_carry:
        out_shape += [jax.ShapeDtypeStruct((n, *s), F32) for s in carry_shapes]
        out_specs += [pl.BlockSpec((1, *s), lambda i: (i, 0, 0)) for s in carry_shapes]
    res = pl.pallas_call(
        body, name=name, grid=(n,),
        in_specs=[pl.BlockSpec((tile, s.shape[1]), lambda i: (i, 0)) for s in seqs] + [_full_spec(p) for p in params],
        out_specs=out_specs, out_shape=out_shape,
        scratch_shapes=[pltpu.VMEM(s, F32) for s in carry_shapes],
        compiler_params=_cparams("arbitrary"),
    )(*seqs, *params)
    return res[:no], res[no:]


def scan_bwd(fn, name, tile, seqs, params, saved, douts, n_dseq, n_dpar, dseq_dtypes=None):
    dseq_dtypes = dseq_dtypes or [F32] * n_dseq
    rows = seqs[0].shape[0]
    tile = min(tile, rows)
    n = rows // tile
    ns, npar, nc, no = len(seqs), len(params), len(saved), len(douts)

    def body(*refs):
        seq_refs, refs = refs[:ns], refs[ns:]
        par_refs, refs = refs[:npar], refs[npar:]
        save_refs, refs = refs[:nc], refs[nc:]
        dout_refs, refs = refs[:no], refs[no:]
        dseq_refs, refs = refs[:n_dseq], refs[n_dseq:]
        dpar_refs, refs = refs[:n_dpar], refs[n_dpar:]
        dcarry_refs = refs

        @pl.when(pl.program_id(0) == 0)
        def _():
            for r in (*dpar_refs, *dcarry_refs):
                r[...] = jnp.zeros_like(r)

        carry = tuple(r[0] for r in save_refs)
        seq = tuple(r[...].astype(F32) for r in seq_refs)
        par = tuple(r[...].astype(F32) for r in par_refs)

        def f(carry, dseq, dpar):
            return fn(carry, (*dseq, *seq[n_dseq:]), (*dpar, *par[n_dpar:]))

        _, vjp = jax.vjp(f, carry, seq[:n_dseq], par[:n_dpar])
        d_carry, d_seq, d_par = vjp((tuple(r[...] for r in dcarry_refs),
                                     tuple(r[...].astype(F32) for r in dout_refs)))
        for r, v in zip(dseq_refs, d_seq):
            r[...] = v.astype(r.dtype)
        for r, v in zip(dpar_refs, d_par):
            r[...] += v
        for r, v in zip(dcarry_refs, d_carry):
            r[...] = v

    rev = lambda i: (n - 1 - i, 0)
    res = pl.pallas_call(
        body, name=name, grid=(n,),
        in_specs=([pl.BlockSpec((tile, s.shape[1]), rev) for s in seqs] + [_full_spec(p) for p in params]
                  + [pl.BlockSpec((1, *s.shape[1:]), lambda i: (n - 1 - i, 0, 0)) for s in saved]
                  + [pl.BlockSpec((tile, d.shape[1]), rev) for d in douts]),
        out_specs=([pl.BlockSpec((tile, s.shape[1]), rev) for s in seqs[:n_dseq]]
                   + [_full_spec(p) for p in params[:n_dpar]]),
        out_shape=([jax.ShapeDtypeStruct((rows, s.shape[1]), dt) for s, dt in zip(seqs[:n_dseq], dseq_dtypes)]
                   + [jax.ShapeDtypeStruct(p.shape, F32) for p in params[:n_dpar]]),
        scratch_shapes=[pltpu.VMEM(s.shape[1:], F32) for s in saved],
        compiler_params=_cparams("arbitrary"),
    )(*seqs, *params, *saved, *douts)
    return res[:n_dseq], res[n_dseq:]


def _tile_of(dim, pref):
    if dim <= pref:
        return dim
    best = max((t for t in range(128, pref + 1, 128) if dim % t == 0), default=None)
    if best is None or (best < 512 and dim <= 2304):
        return dim
    return best


def _row_tile(rows, pref):
    if rows <= pref:
        return rows
    return max(t for t in range(8, pref + 1, 8) if rows % t == 0)


def matmul(a, b, name, ta=False, tb=False, out_dtype=F32, add=None, bias=None, relu2=False, relu2_of=None,
           tm=1024, tn=2048, tk=1024):
    m, k = (a.shape[1], a.shape[0]) if ta else a.shape
    n = b.shape[0] if tb else b.shape[1]
    assert k == (b.shape[1] if tb else b.shape[0])
    tm, tn, tk = _tile_of(m, tm), _tile_of(n, tn), _tile_of(k, tk)
    nm, nn, nk = m // tm, n // tn, k // tk
    assert nk == 1 or (out_dtype == F32 and not relu2 and relu2_of is None)
    dn = (((0 if ta else 1,), (1 if tb else 0,)), ((), ()))
    has_add, has_bias, has_u = add is not None, bias is not None, relu2_of is not None
    n_inner = a.size * a.dtype.itemsize * (nn - 1) >= b.size * b.dtype.itemsize * (nm - 1)
    ij = (lambda g0, g1: (g0, g1)) if n_inner else (lambda g0, g1: (g1, g0))

    def body(*refs):
        a_ref, b_ref, refs = refs[0], refs[1], refs[2:]
        add_ref, refs = (refs[0], refs[1:]) if has_add else (None, refs)
        bias_ref, refs = (refs[0], refs[1:]) if has_bias else (None, refs)
        u_ref, refs = (refs[0], refs[1:]) if has_u else (None, refs)
        o_ref = refs[0]
        r = lax.dot_general(a_ref[...].astype(BF16), b_ref[...].astype(BF16), dn, preferred_element_type=F32)

        def first():
            v = r
            if has_add:
                v = v + add_ref[...]
            if has_bias:
                v = v + bias_ref[...]
            if has_u:
                v = v * (2.0 * jnp.maximum(u_ref[...], 0.0))
            o_ref[...] = v.astype(o_ref.dtype)
            if relu2:
                p = jnp.maximum(v, 0.0)
                refs[1][...] = (p * p).astype(BF16)

        if nk == 1:
            first()
        else:
            pl.when(pl.program_id(2) == 0)(first)

            @pl.when(pl.program_id(2) > 0)
            def _():
                o_ref[...] += r

    def spec(shape, fn):
        return pl.BlockSpec(shape, lambda g0, g1, l: fn(*ij(g0, g1), l))

    a_spec = spec((tk, tm), lambda i, j, l: (l, i)) if ta else spec((tm, tk), lambda i, j, l: (i, l))
    b_spec = spec((tn, tk), lambda i, j, l: (j, l)) if tb else spec((tk, tn), lambda i, j, l: (l, j))
    o_spec = spec((tm, tn), lambda i, j, l: (i, j))
    in_specs, args = [a_spec, b_spec], [a, b]
    if has_add:
        in_specs.append(o_spec)
        args.append(add)
    if has_bias:
        in_specs.append(spec((1, tn), lambda i, j, l: (0, j)))
        args.append(bias)
    if has_u:
        in_specs.append(o_spec)
        args.append(relu2_of)
    out_shape = [jax.ShapeDtypeStruct((m, n), out_dtype)] + ([jax.ShapeDtypeStruct((m, n), BF16)] if relu2 else [])
    res = pl.pallas_call(
        body, name=name, grid=(nm, nn, nk) if n_inner else (nn, nm, nk), in_specs=in_specs,
        out_specs=[o_spec] * len(out_shape), out_shape=out_shape,
        compiler_params=_cparams("parallel", "parallel", "arbitrary"),
    )(*args)
    return res if relu2 else res[0]


LIN_TILE = 512


def linscan_fwd(a, u, name):
    rows, w = a.shape
    tile = min(LIN_TILE, rows)

    def body(a_ref, u_ref, h_ref, hc):
        @pl.when(pl.program_id(0) == 0)
        def _():
            hc[...] = jnp.zeros_like(hc)

        row = lax.broadcasted_iota(jnp.int32, (8, 1), 0)

        def group(k, h_in):
            rows8 = pl.ds(pl.multiple_of(k * 8, 8), 8)
            pa, pu = a_ref[rows8, :], u_ref[rows8, :]
            for d in (1, 2, 4):
                pu = pu + pa * jnp.where(row >= d, pltpu.roll(pu, d, 0), 0.0)
                pa = pa * jnp.where(row >= d, pltpu.roll(pa, d, 0), 1.0)
            h_ref[rows8, :] = pa * h_in + pu
            return h_ref[pl.ds(k * 8 + 7, 1), :]

        hc[...] = lax.fori_loop(0, tile // 8, group, hc[...], unroll=4)

    spec = pl.BlockSpec((tile, w), lambda i: (i, 0))
    return pl.pallas_call(
        body, name=name, grid=(rows // tile,), in_specs=[spec, spec], out_specs=spec,
        out_shape=jax.ShapeDtypeStruct((rows, w), F32), scratch_shapes=[pltpu.VMEM((1, w), F32)],
        compiler_params=_cparams("arbitrary"),
    )(a, u)


def linscan_bwd(a, hs, dh, name):
    rows, w = a.shape
    tile = min(LIN_TILE, rows)
    n = rows // tile
    per = tile // 8

    def body(a_ref, h_ref, hprev_ref, dh_ref, da_ref, du_ref, cc):
        i = pl.program_id(0)

        @pl.when(i == 0)
        def _():
            cc[...] = jnp.zeros_like(cc)

        row = lax.broadcasted_iota(jnp.int32, (8, 1), 0)
        h_before = jnp.where(i == n - 1, 0.0, hprev_ref[7:8, :])

        def group(s, c_in):
            k = per - 1 - s
            rows8 = pl.ds(pl.multiple_of(k * 8, 8), 8)
            av, hv = a_ref[rows8, :], h_ref[rows8, :]
            pb = jnp.where(row < 7, pltpu.roll(av, 7, 0), 1.0)
            pg = dh_ref[rows8, :]
            for d in (1, 2, 4):
                pg = pg + pb * jnp.where(row < 8 - d, pltpu.roll(pg, 8 - d, 0), 0.0)
                pb = pb * jnp.where(row < 8 - d, pltpu.roll(pb, 8 - d, 0), 1.0)
            g = pg + pb * c_in
            du_ref[rows8, :] = g
            h_prev = jnp.where(k == 0, h_before, h_ref[pl.ds(jnp.maximum(k * 8 - 1, 0), 1), :])
            da_ref[rows8, :] = g * jnp.where(row >= 1, pltpu.roll(hv, 1, 0), h_prev)
            return a_ref[pl.ds(k * 8, 1), :] * du_ref[pl.ds(k * 8, 1), :]

        cc[...] = lax.fori_loop(0, per, group, cc[...], unroll=4)

    rev = pl.BlockSpec((tile, w), lambda i: (n - 1 - i, 0))
    prev = pl.BlockSpec((8, w), lambda i: (jnp.maximum((n - 1 - i) * per - 1, 0), 0))
    return pl.pallas_call(
        body, name=name, grid=(n,), in_specs=[rev, rev, prev, rev], out_specs=[rev, rev],
        out_shape=[jax.ShapeDtypeStruct((rows, w), F32)] * 2, scratch_shapes=[pltpu.VMEM((1, w), F32)],
        compiler_params=_cparams("arbitrary"),
    )(a, hs, hs, dh)


def loss_head(x, target, w, name):
    rows, d = x.shape
    tile = min(512, rows)

    def body(x_ref, t_ref, w_ref, loss_ref, dx_ref, dw_ref):
        @pl.when(pl.program_id(0) == 0)
        def _():
            loss_ref[...] = jnp.zeros_like(loss_ref)
            dw_ref[...] = jnp.zeros_like(dw_ref)

        tv = t_ref[...]

        def f(xv, wv):
            e = _rms(xv, wv) - tv
            return 0.5 * jnp.sum(jnp.mean(e * e, axis=-1, keepdims=True), axis=0, keepdims=True)

        val, vjp = jax.vjp(f, x_ref[...], w_ref[...])
        dxv, dwv = vjp(jnp.ones((1, 1), F32))
        loss_ref[...] += jnp.broadcast_to(val, loss_ref.shape)
        dx_ref[...] = dxv
        dw_ref[...] += dwv

    spec = pl.BlockSpec((tile, d), lambda i: (i, 0))
    return pl.pallas_call(
        body, name=name, grid=(rows // tile,), in_specs=[spec, spec, _full_spec(w)],
        out_specs=[pl.BlockSpec((8, 128), lambda i: (0, 0)), spec, _full_spec(w)],
        out_shape=[jax.ShapeDtypeStruct((8, 128), F32), jax.ShapeDtypeStruct((rows, d), F32),
                   jax.ShapeDtypeStruct(w.shape, F32)],
        compiler_params=_cparams("arbitrary"),
    )(x, target, w)


def adamw(g, w, m, v, name, copy_g=False):
    layers, rows, cols = g.shape
    tile = _row_tile(rows, 256)
    n_out = 4 if copy_g else 3

    def body(g_ref, w_ref, m_ref, v_ref, d_ref, nm_ref, nv_ref, *g_out):
        gv = g_ref[...]
        if copy_g:
            g_out[0][...] = gv
        nm = ADAM_B1 * m_ref[...] + (1.0 - ADAM_B1) * gv
        nv = ADAM_B2 * v_ref[...] + (1.0 - ADAM_B2) * (gv * gv)
        m_hat = nm / (1.0 - ADAM_B1 ** ADAM_STEP)
        v_hat = nv / (1.0 - ADAM_B2 ** ADAM_STEP)
        d_ref[...] = -ADAM_LR * (m_hat / (jnp.sqrt(v_hat) + ADAM_EPS) + ADAM_WD * w_ref[...])
        nm_ref[...] = nm
        nv_ref[...] = nv

    spec = pl.BlockSpec((None, tile, cols), lambda l, i: (l, i, 0))
    return pl.pallas_call(
        body, name=name, grid=(layers, rows // tile), in_specs=[spec] * 4, out_specs=[spec] * n_out,
        out_shape=[jax.ShapeDtypeStruct((layers, rows, cols), F32)] * n_out,
        compiler_params=_cparams("parallel", "parallel"),
    )(g, w, m, v)


def add_cast(g0, g1, sib, place, name):
    shape = sib.shape
    g0, g1, sib = (a.reshape(-1, shape[-1]) for a in (g0, g1, sib))
    rows, cols = sib.shape
    tile = _row_tile(rows, max(8, min(256, (512 * 1024) // cols)))

    def body(k_ref, g0_ref, g1_ref, s_ref, o_ref, ob_ref):
        s = jnp.where(k_ref[1] == 0, g0_ref[...], g1_ref[...]) + s_ref[...]
        o_ref[...] = s
        ob_ref[...] = s.astype(BF16)

    spec = pl.BlockSpec((tile, cols), lambda i, k: (i, 0))
    s, sb = pl.pallas_call(
        body, name=name,
        grid_spec=pltpu.PrefetchScalarGridSpec(
            num_scalar_prefetch=1, grid=(rows // tile,),
            in_specs=[pl.BlockSpec((tile, cols), lambda i, k: (i * (1 - k[1]), 0)),
                      pl.BlockSpec((tile, cols), lambda i, k: (i * k[1], 0)), spec],
            out_specs=[spec, spec]),
        out_shape=[jax.ShapeDtypeStruct((rows, cols), F32), jax.ShapeDtypeStruct((rows, cols), BF16)],
        compiler_params=_cparams("arbitrary"),
    )(place, g0, g1, sib)
    return s.reshape(shape), sb.reshape(shape)


def sum4(own, recv, by_cols, place, name):
    _, r, c = recv.shape
    tile = _row_tile(r, 256)
    nt = r // tile
    own_map = (lambda i, k: (i, k[0])) if by_cols else (lambda i, k: (k[0] * nt + i, 0))

    def body(k_ref, own_ref, recv_ref, o_ref):
        o_ref[...] = ((own_ref[...] + recv_ref[0].astype(F32)) + recv_ref[1].astype(F32)) + recv_ref[2].astype(F32)

    return pl.pallas_call(
        body, name=name,
        grid_spec=pltpu.PrefetchScalarGridSpec(
            num_scalar_prefetch=1, grid=(nt,),
            in_specs=[pl.BlockSpec((tile, c), own_map), pl.BlockSpec((3, tile, c), lambda i, k: (0, i, 0))],
            out_specs=pl.BlockSpec((None, tile, c), lambda i, k: (k[1], i, 0))),
        out_shape=jax.ShapeDtypeStruct((2, r, c), F32),
        compiler_params=_cparams("arbitrary"),
    )(place, own, recv)


def add8(parts, name):
    _, rows, cols = parts.shape

    def body(p_ref, o_ref):
        acc = p_ref[0]
        for k in range(1, 8):
            acc = acc + p_ref[k]
        o_ref[...] = acc

    return pl.pallas_call(
        body, name=name, in_specs=[pl.BlockSpec(memory_space=pltpu.VMEM)],
        out_specs=pl.BlockSpec(memory_space=pltpu.VMEM),
        out_shape=jax.ShapeDtypeStruct((rows, cols), F32),
        compiler_params=pltpu.CompilerParams(vmem_limit_bytes=VMEM_LIMIT),
    )(parts)


def _place():
    return lax.axis_index("x"), lax.axis_index("y"), lax.axis_index("c")


def _other_chips(x, y):
    return [(1 - x, y), (x, 1 - y), (1 - x, 1 - y)]


_ANY = pl.BlockSpec(memory_space=pl.ANY)


BIG_LAYOUT = (("ada_w", "col", (1024, 6144)), ("w_in", "chip", (4, 1024, 1924)), ("w_branch", "col", (3, 512, 1024)),
              ("w_out", "row", (1024, 1024)), ("w_up", "col", (1024, 4096)), ("w_down", "row", (4096, 1024)))
N_BIG = len(BIG_LAYOUT)
REDUCE_LAYOUT = BIG_LAYOUT[1:]


def _local_shape(kind, full):
    if kind == "col":
        return (*full[:-1], full[-1] // 4)
    if kind == "row":
        return (full[0] // 4, *full[1:])
    return full[1:]


def _window(ref, kind, k, local):
    if kind == "chip":
        return ref.at[k]
    if kind == "row":
        return ref.at[pl.ds(pl.multiple_of(k * local[0], 8), local[0])]
    idx = (slice(None),) * (len(local) - 1) + (pl.ds(pl.multiple_of(k * local[-1], 128), local[-1]),)
    return ref.at[idx]


def _dma_call(body, name, n_in, out_shape, sems, aliases=None):
    return pl.pallas_call(
        body, name=name, in_specs=[_ANY] * n_in, out_specs=[_ANY] * len(out_shape), out_shape=out_shape,
        scratch_shapes=[pltpu.SemaphoreType.DMA((n,)) for n in sems],
        input_output_aliases=aliases or {},
        compiler_params=pltpu.CompilerParams(has_side_effects=True))


def _remote(src, dst, send_sem, recv_sem, to):
    return pltpu.make_async_remote_copy(src_ref=src, dst_ref=dst, send_sem=send_sem, recv_sem=recv_sem,
                                        device_id=to, device_id_type=MESH)


CORE_PARAMS = ((1, 4), (0, 2, 3, 5))


def gather_layer(shards, name):
    locals_ = [_local_shape(kind, full) for _, kind, full in BIG_LAYOUT]

    def body(*refs):
        sh, full, (send_sems, recv_sems, local_sems, pass_send, pass_recv) = (
            refs[:N_BIG], refs[N_BIG:2 * N_BIG], refs[2 * N_BIG:])
        x, y, c = _place()
        me = 2 * x + y
        chips = _other_chips(x, y)
        win = lambda n, k: _window(full[n], BIG_LAYOUT[n][1], k, locals_[n])
        for cc in (0, 1):
            @pl.when(c == cc)
            def _():
                mine, sends = {}, []
                for n in CORE_PARAMS[cc]:
                    mine[n] = pltpu.make_async_copy(sh[n], win(n, me), local_sems.at[n])
                    mine[n].start()
                    for j, chip in enumerate(chips):
                        sends.append(_remote(sh[n], win(n, me), send_sems.at[3 * n + j], recv_sems.at[3 * n + j],
                                             (chip[0], chip[1], c)))
                        sends[-1].start()
                for n in CORE_PARAMS[cc]:
                    for j, chip in enumerate(chips):
                        _remote(sh[n], win(n, 2 * chip[0] + chip[1]), send_sems.at[3 * n + j],
                                recv_sems.at[3 * n + j], (chip[0], chip[1], c)).wait_recv()
                    mine[n].wait()
                    sends.append(_remote(full[n], full[n], pass_send.at[n], pass_recv.at[n], (x, y, 1 - c)))
                    sends[-1].start()
                for n in CORE_PARAMS[1 - cc]:
                    _remote(full[n], full[n], pass_send.at[n], pass_recv.at[n], (x, y, 1 - c)).wait_recv()
                for cp in sends:
                    cp.wait_send()

    out_shape = [jax.ShapeDtypeStruct(full, BF16) for _, _, full in BIG_LAYOUT]
    return _dma_call(body, name, N_BIG, out_shape, (3 * N_BIG, 3 * N_BIG, N_BIG, N_BIG, N_BIG))(*shards)


_HBM = pl.BlockSpec(memory_space=pltpu.HBM)
_SEM = pl.BlockSpec(memory_space=pltpu.SEMAPHORE)


def _hbm(a):
    return pltpu.with_memory_space_constraint(a, pltpu.HBM)


def _gather_copies(sh, full, send_sems, recv_sems):
    locals_ = [_local_shape(kind, f) for _, kind, f in BIG_LAYOUT]
    x, y, c = _place()
    me = 2 * x + y
    pairs = []
    for n in range(N_BIG):
        win = lambda k: _window(full[n], BIG_LAYOUT[n][1], k, locals_[n])
        for j, chip in enumerate(_other_chips(x, y)):
            mk = lambda dst: _remote(sh[n], dst, send_sems.at[3 * n + j], recv_sems.at[3 * n + j],
                                     (chip[0], chip[1], c))
            pairs.append((mk(win(me)), mk(win(2 * chip[0] + chip[1]))))
    return pairs


def gather_start(shards, fulls, after, name):
    def body(*refs):
        sh, full = refs[:N_BIG], refs[N_BIG:2 * N_BIG]
        send_sems, recv_sems = refs[2 * N_BIG + 1:2 * N_BIG + 3]
        for out, _ in _gather_copies(sh, full, send_sems, recv_sems):
            out.start()
        refs[-1][...] = jnp.zeros_like(refs[-1])

    thru = [pltpu.HBM(a.shape, a.dtype) for a in (*shards, *fulls)]
    res = pl.pallas_call(
        body, name=name,
        out_shape=(pltpu.SemaphoreType.DMA((3 * N_BIG,)), pltpu.SemaphoreType.DMA((3 * N_BIG,)), *thru,
                   jax.ShapeDtypeStruct((8, 128), F32)),
        in_specs=[_HBM] * (2 * N_BIG) + [_ANY],
        out_specs=(_SEM, _SEM, *[_HBM] * (2 * N_BIG), pl.BlockSpec(memory_space=pltpu.VMEM)),
        input_output_aliases={i: 2 + i for i in range(2 * N_BIG)},
        compiler_params=pltpu.CompilerParams(has_side_effects=pltpu.SideEffectType.DATAFLOW_SIDE_EFFECTING),
    )(*[_hbm(a) for a in (*shards, *fulls)], after)
    return res[0], res[1], res[2:2 + N_BIG], res[2 + N_BIG:2 + 2 * N_BIG]


def gather_wait(send_sems, recv_sems, shards, fulls, after, name):
    def body(*refs):
        sh, full = refs[:N_BIG], refs[N_BIG:2 * N_BIG]
        ssem, rsem = refs[2 * N_BIG:2 * N_BIG + 2]
        for out, inc in _gather_copies(sh, full, ssem, rsem):
            out.wait_send()
            inc.wait_recv()

    thru = [pltpu.HBM(a.shape, a.dtype) for a in (*shards, *fulls)]
    res = pl.pallas_call(
        body, name=name, out_shape=thru,
        in_specs=[_HBM] * (2 * N_BIG) + [_SEM, _SEM, _ANY], out_specs=[_HBM] * (2 * N_BIG),
        input_output_aliases={i: i for i in range(2 * N_BIG)},
        compiler_params=pltpu.CompilerParams(has_side_effects=pltpu.SideEffectType.DATAFLOW_SIDE_EFFECTING),
    )(*shards, *fulls, send_sems, recv_sems, after)
    return res[N_BIG:]


def reduce_d2d(g0, g1, name):
    nb = len(g0)

    def body(*refs):
        g, refs = (refs[:nb], refs[nb:2 * nb]), refs[2 * nb:]
        sib, (send_sems, recv_sems) = refs[:nb], refs[nb:]
        x, y, c = _place()
        for cc in (0, 1):
            @pl.when(c == cc)
            def _():
                sends = [_remote(g[1 - cc][n], sib[n], send_sems.at[n], recv_sems.at[n], (x, y, 1 - c))
                         for n in range(nb)]
                for cp in sends:
                    cp.start()
                for cp in sends:
                    cp.wait_recv()
                for cp in sends:
                    cp.wait_send()

    out_shape = [jax.ShapeDtypeStruct(a.shape, a.dtype) for a in g0]
    return _dma_call(body, name, 2 * nb, out_shape, (nb, nb))(*g0, *g1)


def reduce_ici(sums, layout, name):
    nb = len(sums)
    locals_ = [_local_shape(kind, full) for _, kind, full in layout]

    def body(*refs):
        src, recv, (send_sems, recv_sems) = refs[:nb], refs[nb:2 * nb], refs[2 * nb:]
        x, y, c = _place()
        chips = _other_chips(x, y)
        copies = []
        for n in range(nb):
            for j, chip in enumerate(chips):
                cp = _remote(_window(src[n], layout[n][1], 2 * chip[0] + chip[1], locals_[n]), recv[n].at[j],
                             send_sems.at[3 * n + j], recv_sems.at[3 * n + j], (chip[0], chip[1], c))
                cp.start()
                copies.append(cp)
        for cp in copies:
            cp.wait_recv()
        for cp in copies:
            cp.wait_send()

    out_shape = [jax.ShapeDtypeStruct((3, *ls), a.dtype) for ls, a in zip(locals_, sums)]
    return _dma_call(body, name, nb, out_shape, (3 * nb, 3 * nb))(*sums)


def share_d2d(finals, name):
    nb = len(finals)

    def body(*refs):
        out, (send_sems, recv_sems) = refs[nb:2 * nb], refs[2 * nb:]
        x, y, c = _place()
        sends = [_remote(out[n].at[c], out[n].at[c], send_sems.at[n], recv_sems.at[n], (x, y, 1 - c))
                 for n in range(nb)]
        for cp in sends:
            cp.start()
        for n in range(nb):
            _remote(out[n].at[c], out[n].at[1 - c], send_sems.at[n], recv_sems.at[n], (x, y, 1 - c)).wait_recv()
        for cp in sends:
            cp.wait_send()

    out_shape = [jax.ShapeDtypeStruct(a.shape, a.dtype) for a in finals]
    return _dma_call(body, name, nb, out_shape, (nb, nb), aliases={n: n for n in range(nb)})(*finals)


def allgather8(block, name):
    m_per, n = block.shape

    def body(x_ref, out_ref, send_sems, recv_sems, local_sem):
        x, y, c = _place()
        me, sibling = (x, y, c), (x, y, 1 - c)
        chips = _other_chips(x, y)

        def rows(px, py, pc):
            return out_ref.at[pl.ds((4 * px + 2 * py + pc) * m_per, m_per), :]

        def copy(k, blk, to, src=None):
            return pltpu.make_async_remote_copy(
                src_ref=rows(*blk) if src is None else src, dst_ref=rows(*blk), send_sem=send_sems.at[k],
                recv_sem=recv_sems.at[k], device_id=to, device_id_type=MESH)

        mine = pltpu.make_async_copy(x_ref, rows(*me), local_sem)
        mine.start()
        first = [copy(0, me, sibling, src=x_ref)]
        first += [copy(1 + j, me, (*chip, c), src=x_ref) for j, chip in enumerate(chips)]
        for cp in first:
            cp.start()
        passed = [copy(4 + j, (*chip, c), sibling) for j, chip in enumerate(chips)]
        for j, chip in enumerate(chips):
            copy(1 + j, (*chip, c), me).wait_recv()
            passed[j].start()
        copy(0, sibling, me).wait_recv()
        for j, chip in enumerate(chips):
            copy(4 + j, (*chip, 1 - c), me).wait_recv()
        for cp in first + passed:
            cp.wait_send()
        mine.wait()

    return pl.pallas_call(
        body, name=name, in_specs=[pl.BlockSpec(memory_space=pltpu.VMEM)],
        out_specs=pl.BlockSpec(memory_space=pltpu.VMEM),
        out_shape=jax.ShapeDtypeStruct((8 * m_per, n), block.dtype),
        scratch_shapes=[pltpu.SemaphoreType.DMA((7,)), pltpu.SemaphoreType.DMA((7,)), pltpu.SemaphoreType.DMA],
        compiler_params=pltpu.CompilerParams(vmem_limit_bytes=VMEM_LIMIT),
    )(block)


CONV = ("gdn_conv_w", "ssd_conv_w", "lru_conv_w")
SMALL = ("ada_b", "norm_mix", "gdn_a_log", "gdn_dt_bias", "gdn_norm", "ssd_conv_b", "ssd_a_log", "ssd_dt_bias",
         "ssd_d", "ssd_norm", "lru_conv_b", "lru_w_a", "lru_b_a", "lru_w_x", "lru_b_x", "lru_lambda", "norm_mlp",
         "final_norm")
WEIGHTS = ("ada_w", "ada_b", "norm_mix", "w_in", "gdn_conv_w", "gdn_a_log", "gdn_dt_bias", "gdn_norm", "ssd_conv_w",
           "ssd_conv_b", "ssd_a_log", "ssd_dt_bias", "ssd_d", "ssd_norm", "lru_conv_w", "lru_conv_b", "lru_w_a",
           "lru_b_a", "lru_w_x", "lru_b_x", "lru_lambda", "w_branch", "w_out", "norm_mlp", "w_up", "w_down",
           "final_norm")
PACK_COLS = 1024


def _pack_rows(shape):
    return 8 * -(-math.prod(shape) // (8 * PACK_COLS))


def _pack(arrays, dtype):
    parts = []
    for a in arrays:
        flat = a.reshape(-1).astype(dtype)
        pad = _pack_rows(a.shape) * PACK_COLS - flat.shape[0]
        parts.append((jnp.concatenate([flat, jnp.zeros((pad,), dtype)]) if pad else flat).reshape(-1, PACK_COLS))
    return jnp.concatenate(parts, axis=0)


def _unpack(pack, shapes):
    out, o = [], 0
    for s in shapes:
        r = _pack_rows(s)
        out.append(pack[o:o + r].reshape(-1)[:math.prod(s)].reshape(s))
        o += r
    return out


def _split_w_in(w4):
    w = jnp.concatenate([w4[k] for k in range(4)], axis=1)
    pad = jnp.zeros((w.shape[0], 120), w.dtype)
    gdn = jnp.concatenate([w[:, 0:2056], pad], axis=1)
    ssd = jnp.concatenate([w[:, 2056:2568], w[:, 3080:3592], w[:, 2568:3080], w[:, 3592:3600], pad], axis=1)
    return gdn, ssd, w[:, 3600:4112], w[:, 4112:4624], w[:, 4624:7696]


def _join_w_in(gdn, ssd, lx, lg, gate):
    w = jnp.concatenate([gdn[:, 0:2056], ssd[:, 0:512], ssd[:, 1024:1536], ssd[:, 512:1024], ssd[:, 1536:1544],
                         lx, lg, gate], axis=1)
    return jnp.stack([w[:, k * 1924:(k + 1) * 1924] for k in range(4)])


def _lanes(v, at, width=128):
    return jnp.zeros((1, width), F32).at[0, at:at + v.shape[0]].set(v)


def _block_diag(w):
    return (jnp.eye(8, dtype=w.dtype)[:, None, :, None] * w[:, :, None, :]).reshape(512, 512)


def _diag_blocks(w):
    return jnp.stack([w[n * 64:(n + 1) * 64, n * 64:(n + 1) * 64] for n in range(8)])


TOK_TILE = 512
WIDE_TILE = 256


def _layer_params(p, big, l):
    row = lambda v: v.reshape(1, -1)
    b = dict(zip((n for n, _, _ in BIG_LAYOUT), big))
    gdn = (p["gdn_conv_w"][l], _lanes(p["gdn_a_log"][l], 4), _lanes(p["gdn_dt_bias"][l], 4), row(p["gdn_norm"][l]))
    ssd = (p["ssd_conv_w"][l], row(p["ssd_conv_b"][l]), _lanes(p["ssd_a_log"][l], 0), _lanes(p["ssd_dt_bias"][l], 0),
           row(jnp.repeat(p["ssd_d"][l], 64)), row(p["ssd_norm"][l]))
    lru = (p["lru_conv_w"][l], row(p["lru_conv_b"][l]), _block_diag(p["lru_w_a"][l]), row(p["lru_b_a"][l]),
           _block_diag(p["lru_w_x"][l]), row(p["lru_b_x"][l]), row(p["lru_lambda"][l]))
    return dict(gdn=gdn, ssd=ssd, lru=lru, w_in=_split_w_in(b["w_in"]),
                wb=tuple(b["w_branch"][r] for r in range(3)), w_out=b["w_out"], w_up=b["w_up"],
                w_down=b["w_down"], ada_w=b["ada_w"], ada_b=row(p["ada_b"][l]),
                norm_mix=row(p["norm_mix"][l]), norm_mlp=row(p["norm_mlp"][l]))


def _layer_fwd(x, silu_c, lp, l):
    nm = lambda s: f"l{l}_{s}"
    mod = matmul(silu_c, lp["ada_w"], nm("mod"), bias=lp["ada_b"])
    sh1, sc1, gt1, sh2, sc2, gt2 = (mod[0:1, k * D_MODEL:(k + 1) * D_MODEL] for k in range(N_MOD))
    (h,), _ = scan_fwd(norm1_fn, nm("norm1"), TOK_TILE, [x], [lp["norm_mix"], sc1, sh1], [], [(D_MODEL, BF16)])
    w_gdn, w_ssd, w_lx, w_lg, w_gate = lp["w_in"]
    p_gdn = matmul(h, w_gdn, nm("in_gdn"))
    p_ssd = matmul(h, w_ssd, nm("in_ssd"))
    p_lx = matmul(h, w_lx, nm("in_lx"))
    p_lg = matmul(h, w_lg, nm("in_lg"))
    p_gate = matmul(h, w_gate, nm("in_gate"))
    (ya,), sv_gdn = scan_fwd(gdn_fn, nm("gdn"), CHUNK, [p_gdn], lp["gdn"], [(128, 128)] * 4 + [(8, 1536)],
                             [(512, F32)], save_carry=True)
    (yb,), sv_ssd = scan_fwd(ssd_fn, nm("ssd"), CHUNK, [p_ssd], lp["ssd"], [(128, 128)] * 4 + [(8, 1024)],
                             [(512, F32)], save_carry=True)
    (a, u), sv_lru = scan_fwd(lru_in_fn, nm("lru_in"), TOK_TILE, [p_lx], lp["lru"], [(8, 512)],
                              [(512, F32), (512, F32)], save_carry=True)
    hs = linscan_fwd(a, u, nm("lru_scan"))
    (yc,), _ = scan_fwd(lru_out_fn, nm("lru_out"), TOK_TILE, [hs, p_lg], [], [], [(512, F32)])
    (merged,), _ = scan_fwd(merge_fn, nm("merge"), WIDE_TILE, [ya, yb, yc, p_gate], lp["wb"], [], [(D_MODEL, BF16)])
    mix = matmul(merged, lp["w_out"], nm("out"))
    (h2, x1), _ = scan_fwd(resid_norm_fn, nm("norm2"), TOK_TILE, [x, mix], [gt1, lp["norm_mlp"], sc2, sh2], [],
                           [(D_MODEL, BF16), (D_MODEL, F32)])
    up, act = matmul(h2, lp["w_up"], nm("up"), relu2=True)
    dn = matmul(act, lp["w_down"], nm("down"))
    (x2,), _ = scan_fwd(resid_fn, nm("resid"), TOK_TILE, [x1, dn], [gt2], [], [(D_MODEL, F32)])
    saved = dict(x=x, h=h, p_gdn=p_gdn, p_ssd=p_ssd, p_lx=p_lx, p_lg=p_lg, p_gate=p_gate, sv_gdn=sv_gdn,
                 sv_ssd=sv_ssd, sv_lru=sv_lru, a=a, hs=hs, ya=ya, yb=yb, yc=yc, merged=merged, mix=mix, x1=x1,
                 h2=h2, up=up, act=act, dn=dn, mod=(sh1, sc1, gt1, sh2, sc2, gt2))
    return x2, saved


def _layer_bwd(d_x2, silu_c, lp, sv, l):
    nm = lambda s: f"l{l}_b_{s}"
    sh1, sc1, gt1, sh2, sc2, gt2 = sv["mod"]
    (d_x1, d_dn), (d_gt2,) = scan_bwd(resid_fn, nm("resid"), TOK_TILE, [sv["x1"], sv["dn"]], [gt2], [], [d_x2], 2, 1,
                                      [F32, BF16])
    d_up = matmul(d_dn, lp["w_down"], nm("down_x"), tb=True, relu2_of=sv["up"], out_dtype=BF16)
    g_w_down = matmul(sv["act"], d_dn, nm("down_w"), ta=True)
    d_h2 = matmul(d_up, lp["w_up"], nm("up_x"), tb=True)
    g_w_up = matmul(sv["h2"], d_up, nm("up_w"), ta=True)
    (d_x, d_mix), (d_gt1, g_norm_mlp, d_sc2, d_sh2) = scan_bwd(
        resid_norm_fn, nm("norm2"), TOK_TILE, [sv["x"], sv["mix"]], [gt1, lp["norm_mlp"], sc2, sh2], [],
        [d_h2, d_x1], 2, 4, [F32, BF16])
    d_merged = matmul(d_mix, lp["w_out"], nm("out_x"), tb=True)
    g_w_out = matmul(sv["merged"], d_mix, nm("out_w"), ta=True)
    (d_ya, d_yb, d_yc, d_pgate), g_wb = scan_bwd(
        merge_fn, nm("merge"), WIDE_TILE, [sv["ya"], sv["yb"], sv["yc"], sv["p_gate"]], lp["wb"], [], [d_merged], 4, 3,
        [F32, F32, F32, BF16])
    (d_hs, d_plg), _ = scan_bwd(lru_out_fn, nm("lru_out"), TOK_TILE, [sv["hs"], sv["p_lg"]], [], [], [d_yc], 2, 0,
                                [F32, BF16])
    d_a, d_u = linscan_bwd(sv["a"], sv["hs"], d_hs, nm("lru_scan"))
    (d_plx,), g_lru = scan_bwd(lru_in_fn, nm("lru_in"), TOK_TILE, [sv["p_lx"]], lp["lru"], sv["sv_lru"],
                               [d_a, d_u], 1, 7, [BF16])
    (d_pssd,), g_ssd = scan_bwd(ssd_fn, nm("ssd"), CHUNK, [sv["p_ssd"]], lp["ssd"], sv["sv_ssd"], [d_yb], 1, 6,
                                [BF16])
    (d_pgdn,), g_gdn = scan_bwd(gdn_fn, nm("gdn"), CHUNK, [sv["p_gdn"]], lp["gdn"], sv["sv_gdn"], [d_ya], 1, 4,
                                [BF16])
    d_h = None
    g_w_in = []
    for tag, dp, w in zip(("gdn", "ssd", "lx", "lg", "gate"), (d_pgdn, d_pssd, d_plx, d_plg, d_pgate), lp["w_in"]):
        d_h = matmul(dp, w, nm("in_x_" + tag), tb=True, add=d_h)
        g_w_in.append(matmul(sv["h"], dp, nm("in_w_" + tag), ta=True))
    (d_x0,), (g_norm_mix, d_sc1, d_sh1) = scan_bwd(norm1_fn, nm("norm1"), TOK_TILE, [sv["x"]],
                                                   [lp["norm_mix"], sc1, sh1], [], [d_h, d_x], 1, 3)
    d_mod = jnp.concatenate([d_sh1, d_sc1, d_gt1, d_sh2, d_sc2, d_gt2], axis=1)
    flat = lambda v: v.reshape(-1)
    grads = dict(
        ada_b=flat(d_mod), norm_mix=flat(g_norm_mix),
        gdn_conv_w=g_gdn[0], gdn_a_log=g_gdn[1][0, 4:8], gdn_dt_bias=g_gdn[2][0, 4:8], gdn_norm=flat(g_gdn[3]),
        ssd_conv_w=g_ssd[0], ssd_conv_b=flat(g_ssd[1]), ssd_a_log=g_ssd[2][0, 0:8], ssd_dt_bias=g_ssd[3][0, 0:8],
        ssd_d=g_ssd[4].reshape(8, 64).sum(axis=1), ssd_norm=flat(g_ssd[5]),
        lru_conv_w=g_lru[0], lru_conv_b=flat(g_lru[1]), lru_w_a=_diag_blocks(g_lru[2]), lru_b_a=flat(g_lru[3]),
        lru_w_x=_diag_blocks(g_lru[4]), lru_b_x=flat(g_lru[5]), lru_lambda=flat(g_lru[6]),
        norm_mlp=flat(g_norm_mlp))
    big = [_join_w_in(*g_w_in), jnp.stack(g_wb), g_w_out, g_w_up, g_w_down]
    return d_x0, grads, big


def local_step(x, c, target, p, big):
    c8 = jnp.concatenate([c, jnp.zeros((7, c.shape[1]), F32)], axis=0)
    (silu_c,), _ = scan_fwd(silu_fn, "silu_c", 8, [c8], [], [], [(D_MODEL, F32)])
    lps, saved = [], []
    for l in range(DEPTH):
        lps.append(_layer_params(p, big[l](x), l))
        x, sv = _layer_fwd(x, silu_c, lps[l], l)
        saved.append(sv)
    loss, d_x, g_final = loss_head(x, target, p["final_norm"].reshape(1, -1), "loss_head")
    layer_grads, big_grads = [None] * DEPTH, [None] * DEPTH
    for l in reversed(range(DEPTH)):
        d_x, layer_grads[l], big_grads[l] = _layer_bwd(d_x, silu_c, lps[l], saved[l], l)
    grads = {k: jnp.stack([layer_grads[l][k] for l in range(DEPTH)]) for k in layer_grads[0]}
    grads["final_norm"] = g_final.reshape(-1)
    return loss, d_x, grads, big_grads, silu_c[0]


def _place_shard(shard, kind, full, chip):
    base = lax.empty(full, shard.dtype)
    if kind == "chip":
        return lax.dynamic_update_index_in_dim(base, shard, chip, axis=0)
    axis = 0 if kind == "row" else len(full) - 1
    return lax.dynamic_update_slice_in_dim(base, shard, chip * shard.shape[axis], axis=axis)


def _adam_nd(g, w, m, v, name):
    three = lambda a: a.reshape(-1, *a.shape[-2:])
    return tuple(r.reshape(w.shape) for r in adamw(three(g), three(w), three(m), three(v), name, copy_g=True))


def _reduce_big(g0, g1, place):
    sib = reduce_d2d(g0, g1, "reduce_pool")
    pooled = [add_cast(a, b, s, place, "reduce_pool_" + n) for (n, _, _), a, b, s in zip(REDUCE_LAYOUT, g0, g1, sib)]
    recv = reduce_ici([pb for _, pb in pooled], REDUCE_LAYOUT, "reduce_ici")
    finals = []
    for (n, kind, full), (pf, _), r in zip(REDUCE_LAYOUT, pooled, recv):
        local = r.shape[1:]
        own2 = pf.reshape(-1, full[-1])
        r3 = r.reshape(3, -1, local[-1])
        finals.append(sum4(own2, r3, kind == "col", place, "reduce_sum_" + n).reshape(2, *local))
    return share_d2d(finals, "reduce_share")


def _step(w, m, v, x, c, target):
    chip = 2 * lax.axis_index("x") + lax.axis_index("y")
    place = jnp.stack([chip, lax.axis_index("c")]).astype(jnp.int32)
    conv_shapes = [w[n].shape for n in CONV]
    small_shapes = [w[n].shape for n in SMALL]

    shards = [w[n].astype(BF16) for n, _, _ in BIG_LAYOUT]
    big0 = gather_layer([s[0] for s in shards], "gather_l0")
    own1 = [_place_shard(s[1], kind, full, chip) for s, (_, kind, full) in zip(shards, BIG_LAYOUT)]
    in_flight = gather_start([s[1] for s in shards], own1, big0[0], "gather_l1_start")
    big = [lambda _: big0, lambda x_in: gather_wait(*in_flight, x_in, "gather_l1_wait")]
    conv_all = allgather8(_pack([w[n] for n in CONV], F32), "gather_conv").reshape(8, -1, PACK_COLS)
    conv_parts = [_unpack(conv_all[2 * k], conv_shapes) for k in range(4)]
    p = {n: w[n] for n in SMALL}
    for i, n in enumerate(CONV):
        p[n] = jnp.concatenate([conv_parts[k][i] for k in range(4)], axis=2)

    loss_blk, grad_x, g, big_g, silu_c = local_step(x[0], c, target[0], p, big)
    big_g = dict(zip((n for n, _, _ in REDUCE_LAYOUT), _reduce_big(big_g[0], big_g[1], place)))

    assert SMALL[0] == "ada_b"
    small_pack = _pack([g["ada_b"], silu_c, loss_blk[0, 0:1]] + [g[n] for n in SMALL[1:]] + [g[n] for n in CONV], F32)
    small_all = allgather8(small_pack, "gather_small").reshape(8, -1, PACK_COLS)
    total = _unpack(add8(small_all, "reduce_small"),
                    [small_shapes[0], (D_MODEL,), (1,)] + small_shapes[1:] + [g[n].shape for n in CONV])
    loss = total[2][0]
    small_g = dict(zip(SMALL, [total[0]] + total[3:2 + len(SMALL)]))
    conv_g = {n: lax.dynamic_slice_in_dim(t, chip * w[n].shape[2], w[n].shape[2], axis=2)
              for n, t in zip(CONV, total[2 + len(SMALL):])}

    cols = w["ada_w"].shape[2]
    silu_all = small_all[:, _pack_rows(small_shapes[0]), :]
    big_g["ada_w"] = jnp.stack([
        matmul(silu_all, lax.dynamic_slice_in_dim(small_all[:, N_MOD * l:N_MOD * (l + 1), :].reshape(8, -1),
                                                  chip * cols, cols, axis=1), f"ada_w_grad{l}", ta=True)
        for l in range(DEPTH)])

    grad, delta, new_m, new_v = {}, {}, {}, {}
    for n, _, _ in BIG_LAYOUT:
        delta[n], new_m[n], new_v[n], grad[n] = _adam_nd(big_g[n], w[n], m[n], v[n], "adam_" + n)
    for names, gs, shapes, tag in ((SMALL, small_g, small_shapes, "small"), (CONV, conv_g, conv_shapes, "conv")):
        pk = lambda d: _pack([d[n] for n in names], F32)[None]
        res = adamw(pk(gs), pk(w), pk(m), pk(v), "adam_" + tag)
        for out, r in zip((delta, new_m, new_v), res):
            out.update(zip(names, _unpack(r[0], shapes)))
        grad.update({n: gs[n] for n in names})
    outs = [loss, grad_x[None]]
    for d in (grad, delta, new_m, new_v):
        outs += [d[n] for n in WEIGHTS]
    return tuple(outs)


def kernel(x, c, ada_w, ada_b, norm_mix, w_in, gdn_conv_w, gdn_a_log, gdn_dt_bias, gdn_norm, ssd_conv_w, ssd_conv_b, ssd_a_log, ssd_dt_bias, ssd_d, ssd_norm, lru_conv_w, lru_conv_b, lru_w_a, lru_b_a, lru_w_x, lru_b_x, lru_lambda, w_branch, w_out, norm_mlp, w_up, w_down, final_norm, loss_target, m_ada_w, m_ada_b, m_norm_mix, m_w_in, m_gdn_conv_w, m_gdn_a_log, m_gdn_dt_bias, m_gdn_norm, m_ssd_conv_w, m_ssd_conv_b, m_ssd_a_log, m_ssd_dt_bias, m_ssd_d, m_ssd_norm, m_lru_conv_w, m_lru_conv_b, m_lru_w_a, m_lru_b_a, m_lru_w_x, m_lru_b_x, m_lru_lambda, m_w_branch, m_w_out, m_norm_mlp, m_w_up, m_w_down, m_final_norm, v_ada_w, v_ada_b, v_norm_mix, v_w_in, v_gdn_conv_w, v_gdn_a_log, v_gdn_dt_bias, v_gdn_norm, v_ssd_conv_w, v_ssd_conv_b, v_ssd_a_log, v_ssd_dt_bias, v_ssd_d, v_ssd_norm, v_lru_conv_w, v_lru_conv_b, v_lru_w_a, v_lru_b_a, v_lru_w_x, v_lru_b_x, v_lru_lambda, v_w_branch, v_w_out, v_norm_mlp, v_w_up, v_w_down, v_final_norm):
    given = dict(locals())
    w = {n: given[n] for n in WEIGHTS}
    m = {n: given["m_" + n] for n in WEIGHTS}
    v = {n: given["v_" + n] for n in WEIGHTS}
    return _step(w, m, v, x, c, loss_target)
```

```python
import functools
import math

import jax
import jax.numpy as jnp
from jax import lax
from jax.experimental import pallas as pl
from jax.experimental.pallas import tpu as pltpu

F32 = jnp.float32
BF16 = jnp.bfloat16

D_MODEL = 1024
DEPTH = 2
RMS_EPS = 1e-6
CHUNK = 128
GDN_HEADS = 4
SSD_HEADS = 8
LRU_C = 8.0
D_FF = 4096
N_MOD = 6
W_GDN = 2176
W_SSD = 1664
W_LRU = 512
W_GATE = 3072
ADAM_LR = 0.001
ADAM_B1 = 0.9
ADAM_B2 = 0.999
ADAM_EPS = 1e-08
ADAM_WD = 0.01
ADAM_STEP = 10
VMEM_LIMIT = 56 * 1024 * 1024
MESH = pl.DeviceIdType.MESH


def _dot(a, b, ta, tb):
    dn = (((0 if ta else 1,), (1 if tb else 0,)), ((), ()))
    return lax.dot_general(a.astype(BF16), b.astype(BF16), dn, preferred_element_type=F32)


@functools.partial(jax.custom_vjp, nondiff_argnums=(2, 3))
def mm(a, b, ta, tb):
    return _dot(a, b, ta, tb)


def _mm_fwd(a, b, ta, tb):
    return _dot(a, b, ta, tb), (a, b)


def _mm_bwd(ta, tb, res, g):
    a, b = res
    if not ta and not tb:
        return mm(g, b, False, True), mm(a, g, True, False)
    if not ta and tb:
        return mm(g, b, False, False), mm(g, a, True, False)
    assert ta and not tb
    return mm(b, g, False, True), mm(a, g, False, False)


mm.defvjp(_mm_fwd, _mm_bwd)


def _tri_apply(x, upper):
    t = x.shape[0]
    r = lax.broadcasted_iota(jnp.int32, (t, t), 0)
    c = lax.broadcasted_iota(jnp.int32, (t, t), 1)
    tri = jnp.where((r <= c) if upper else (r >= c), 1.0, 0.0).astype(BF16)
    x1 = x.astype(BF16)
    r1 = x - x1.astype(F32)
    x2 = r1.astype(BF16)
    x3 = (r1 - x2.astype(F32)).astype(BF16)
    d = lambda p: jnp.dot(tri, p, preferred_element_type=F32)
    return (d(x1) + d(x2)) + d(x3)


@jax.custom_vjp
def cumsum_rows(x):
    return _tri_apply(x, False)


cumsum_rows.defvjp(lambda x: (_tri_apply(x, False), None), lambda _, g: (_tri_apply(g, True),))


def _dot_split(a, b):
    a1, b1 = a.astype(BF16), b.astype(BF16)
    a2, b2 = (a - a1.astype(F32)).astype(BF16), (b - b1.astype(F32)).astype(BF16)
    d = lambda p, q: jnp.dot(p, q, preferred_element_type=F32)
    return d(a1, b1) + (d(a1, b2) + d(a2, b1))


def _neumann(ms):
    t = ms[0].shape[0]
    xs = [-m for m in ms]
    qs = [_dot(m, m, False, False) for m in ms]
    n = 2
    while True:
        xs = [x + q + _dot(x, q, False, False) for x, q in zip(xs, qs)]
        n *= 2
        if n >= t:
            break
        qs = [_dot(q, q, False, False) for q in qs]
    rs = [-(x + m + _dot_split(m, x)) for x, m in zip(xs, ms)]
    return [x + r + _dot(x, r, False, False) for x, r in zip(xs, rs)]


@jax.custom_vjp
def tri_solve(ms, rhss):
    return tuple(rhs + _dot(x, rhs, False, False) for x, rhs in zip(_neumann(ms), rhss))


def _tri_solve_fwd(ms, rhss):
    xs = _neumann(ms)
    sols = tuple(rhs + _dot(x, rhs, False, False) for x, rhs in zip(xs, rhss))
    return sols, (tuple(xs), sols)


def _tri_solve_bwd(res, gs):
    xs, sols = res
    d_rhss = tuple(g + _dot(x, g, True, False) for x, g in zip(xs, gs))
    return tuple(-_dot(d, sol, False, True) for d, sol in zip(d_rhss, sols)), d_rhss


tri_solve.defvjp(_tri_solve_fwd, _tri_solve_bwd)


@functools.partial(jax.custom_vjp, nondiff_argnums=(1,))
def split_cols(x, sizes):
    out, o = [], 0
    for s in sizes:
        out.append(x[:, o:o + s])
        o += s
    return tuple(out)


split_cols.defvjp(lambda x, sizes: (split_cols(x, sizes), None),
                  lambda sizes, _, g: (jnp.concatenate(list(g), axis=1),))


@functools.partial(jax.custom_vjp, nondiff_argnums=(1,))
def _last_rows(x, t):
    return x[t - 8:, :]


_last_rows.defvjp(lambda x, t: (_last_rows(x, t), None),
                  lambda t, _, g: (jnp.concatenate([jnp.zeros((t - 8, g.shape[1]), g.dtype), g], axis=0),))


def last8(x):
    return _last_rows(x, x.shape[0])


def _shifted(xp, d, t):
    return (pltpu.roll(xp, d, 0) if d else xp)[8:8 + t, :]


@jax.custom_vjp
def conv4(x, tail, w):
    t = x.shape[0]
    xp = jnp.concatenate([tail, x], axis=0)
    return sum(_shifted(xp, 3 - k, t) * w[k:k + 1, :] for k in range(4))


def _conv4_fwd(x, tail, w):
    return conv4(x, tail, w), (x, tail, w)


def _conv4_bwd(res, g):
    x, tail, w = res
    t = x.shape[0]
    xp = jnp.concatenate([tail, x], axis=0)
    zero8 = jnp.zeros((8, g.shape[1]), g.dtype)
    d_xp = jnp.zeros_like(xp)
    d_w = []
    for k in range(4):
        gk = jnp.concatenate([zero8, g * w[k:k + 1, :]], axis=0)
        d_xp = d_xp + (pltpu.roll(gk, t + 8 - (3 - k), 0) if k < 3 else gk)
        d_w.append(jnp.sum(g * _shifted(xp, 3 - k, t), axis=0, keepdims=True))
    return d_xp[8:, :], d_xp[:8, :], jnp.concatenate(d_w, axis=0)


conv4.defvjp(_conv4_fwd, _conv4_bwd)


def _sigmoid(x):
    return 0.5 * (jnp.tanh(0.5 * x) + 1.0)


def _silu(x):
    return x * _sigmoid(x)


def _softplus(x):
    ax = jnp.where(x > 0, x, -x)
    return jnp.where(x > 0, x, 0.0) + jnp.log(1.0 + jnp.exp(-ax))


def _gelu(x):
    return 0.5 * x * (1.0 + jnp.tanh(math.sqrt(2.0 / math.pi) * (x + 0.044715 * (x * x * x))))


def _expm1(x):
    series = x * (1.0 + x * (0.5 + x * (1.0 / 6.0 + x * (1.0 / 24.0))))
    return jnp.where(jnp.abs(x) < 0.03, series, jnp.exp(x) - 1.0)


def _rms(x, w):
    return x * lax.rsqrt(jnp.mean(x * x, axis=-1, keepdims=True) + RMS_EPS) * w


def _lane_pick(x, j):
    lane = lax.broadcasted_iota(jnp.int32, (1, x.shape[1]), 1)
    return jnp.sum(jnp.where(lane == j, x, 0.0), axis=1, keepdims=True)


def _row_pick(x, j):
    row = lax.broadcasted_iota(jnp.int32, (x.shape[0], 1), 0)
    return jnp.sum(jnp.where(row == j, x, 0.0), axis=0, keepdims=True)


def gdn_fn(carry, seq, params):
    *states, tail = carry
    (tile,) = seq
    conv_w, alog_row, dtb_row, norm_w = params
    t = tile.shape[0]
    qkv_raw, z, sm = split_cols(tile, (1536, 512, 128))
    qkv = _silu(conv4(qkv_raw, tail, conv_w))
    parts = split_cols(qkv, (128,) * 12)
    zs = split_cols(z, (128,) * 4)
    lane = lax.broadcasted_iota(jnp.int32, (1, 128), 1)
    beta_all = _sigmoid(sm)
    g_all = jnp.where((lane >= 4) & (lane < 8), -jnp.exp(alog_row) * _softplus(sm + dtb_row), 0.0)
    gc_all = cumsum_rows(g_all)
    gr_all = gc_all.T
    gl_all = _row_pick(gc_all, t - 1)
    r = lax.broadcasted_iota(jnp.int32, (t, t), 0)
    c = lax.broadcasted_iota(jnp.int32, (t, t), 1)
    heads = range(GDN_HEADS)
    l2 = lambda a: a * lax.rsqrt(jnp.sum(a * a, axis=-1, keepdims=True) + RMS_EPS)
    qn = [l2(parts[h]) * (128.0 ** -0.5) for h in heads]
    kn = [l2(parts[4 + h]) for h in heads]
    beta = [_lane_pick(beta_all, h) for h in heads]
    gc = [_lane_pick(gc_all, 4 + h) for h in heads]
    gl = [_lane_pick(gl_all, 4 + h) for h in heads]
    decay = [jnp.exp(jnp.where(r >= c, gc[h] - _row_pick(gr_all, 4 + h), -1e30)) for h in heads]
    kk = [mm(kn[h], kn[h], False, True) for h in heads]
    qk = [mm(qn[h], kn[h], False, True) for h in heads]
    m = tuple(jnp.where(r > c, beta[h] * kk[h] * decay[h], 0.0) for h in heads)
    eg = [jnp.exp(gc[h]) for h in heads]
    rhs = tuple(jnp.concatenate([beta[h] * parts[8 + h], (beta[h] * eg[h]) * kn[h]], axis=1) for h in heads)
    uw = [split_cols(s, (128, 128)) for s in tri_solve(m, rhs)]
    ws = [mm(uw[h][1], states[h], False, False) for h in heads]
    qs = [mm(qn[h] * eg[h], states[h], False, False) for h in heads]
    v_new = [uw[h][0] - ws[h] for h in heads]
    o = [qs[h] + mm(qk[h] * decay[h], v_new[h], False, False) for h in heads]
    kv = [mm(kn[h] * jnp.exp(gl[h] - gc[h]), v_new[h], True, False) for h in heads]
    new_states = [states[h] * jnp.exp(gl[h]) + kv[h] for h in heads]
    outs = [_rms(o[h], norm_w) * _silu(zs[h]) for h in heads]
    return (*new_states, last8(qkv_raw)), (jnp.concatenate(outs, axis=1),)


def ssd_fn(carry, seq, params):
    *states, tail = carry
    (tile,) = seq
    conv_w, conv_b, alog_row, dtb_row, d_row, norm_w = params
    t = tile.shape[0]
    xbc_raw, z, sm = split_cols(tile, (1024, 512, 128))
    xbc = _silu(conv4(xbc_raw, tail, conv_w) + conv_b)
    x0, x1, x2, x3, b0, b1, c0, c1 = split_cols(xbc, (128,) * 8)
    xs, bs, cs = (x0, x1, x2, x3), (b0, b1), (c0, c1)
    ds = split_cols(d_row, (128,) * 4)
    lane = lax.broadcasted_iota(jnp.int32, (1, 128), 1)
    sub = lax.broadcasted_iota(jnp.int32, (128, 1), 0)
    low = lane < 64
    dt_all = jnp.where(lane < SSD_HEADS, _softplus(sm + dtb_row), 0.0)
    ac_all = cumsum_rows(dt_all * (-jnp.exp(alog_row)))
    ar_all = ac_all.T
    al_all = _row_pick(ac_all, t - 1)
    r = lax.broadcasted_iota(jnp.int32, (t, t), 0)
    c = lax.broadcasted_iota(jnp.int32, (t, t), 1)
    pairs, heads = range(4), range(SSD_HEADS)
    col = [_lane_pick(ac_all, h) for h in heads]
    last = [_lane_pick(al_all, h) for h in heads]
    dt = [_lane_pick(dt_all, h) for h in heads]
    lm = [jnp.exp(jnp.where(r >= c, col[h] - _row_pick(ar_all, h), -1e30)) for h in heads]
    cb = [mm(cs[g], bs[g], False, True) for g in range(2)]
    both = lambda a, b: jnp.where(low, a, b)
    xdt = [xs[p] * both(dt[2 * p], dt[2 * p + 1]) for p in pairs]
    y_off = [mm(cs[p // 2], states[p], False, True) for p in pairs]
    y_lo = [mm(cb[p // 2] * lm[2 * p], jnp.where(low, xdt[p], 0.0), False, False) for p in pairs]
    y_hi = [mm(cb[p // 2] * lm[2 * p + 1], jnp.where(low, 0.0, xdt[p]), False, False) for p in pairs]
    st = [mm(xdt[p] * both(jnp.exp(last[2 * p] - col[2 * p]), jnp.exp(last[2 * p + 1] - col[2 * p + 1])),
             bs[p // 2], True, False) for p in pairs]
    ys = [ds[p] * xs[p] + y_lo[p] + y_hi[p] + y_off[p] * both(jnp.exp(col[2 * p]), jnp.exp(col[2 * p + 1]))
          for p in pairs]
    new_states = [states[p] * jnp.where(sub < 64, jnp.exp(last[2 * p]), jnp.exp(last[2 * p + 1])) + st[p]
                  for p in pairs]
    gz = jnp.concatenate(ys, axis=1) * _silu(z)
    g0, g1 = split_cols(gz, (256, 256))
    n0, n1 = split_cols(norm_w, (256, 256))
    out = jnp.concatenate([_rms(g0, n0), _rms(g1, n1)], axis=1)
    return (*new_states, last8(xbc_raw)), (out,)


def lru_in_fn(carry, seq, params):
    (tail,) = carry
    (x,) = seq
    conv_w, conv_b, w_a, b_a, w_x, b_x, lam = params
    xc = conv4(x, tail, conv_w) + conv_b
    r = _sigmoid(mm(xc, w_a, False, False) + b_a)
    i = _sigmoid(mm(xc, w_x, False, False) + b_x)
    log_a = -LRU_C * r * _softplus(-lam)
    u = jnp.sqrt(-_expm1(2.0 * log_a)) * (i * xc)
    return (last8(x),), (jnp.exp(log_a), u)


def lru_out_fn(carry, seq, params):
    hs, gate = seq
    return (), (hs * _gelu(gate),)


def merge_fn(carry, seq, params):
    ya, yb, yc, gl = seq
    g = split_cols(_sigmoid(gl), (D_MODEL,) * 3)
    merged = sum(g[r] * mm(y, params[r], False, False) for r, y in enumerate((ya, yb, yc)))
    return (), (merged,)


def _adaln(x, w, sc, sh):
    return _rms(x, w) * (1.0 + sc) + sh


def norm1_fn(carry, seq, params):
    (x,) = seq
    return (), (_adaln(x, *params), x)


def resid_norm_fn(carry, seq, params):
    x, mix = seq
    gt, w, sc, sh = params
    x1 = x + gt * mix
    return (), (_adaln(x1, w, sc, sh), x1)


def resid_fn(carry, seq, params):
    x, dn = seq
    (gt,) = params
    return (), (x + gt * dn,)


def silu_fn(carry, seq, params):
    return (), (_silu(seq[0]),)


def _full_spec(a):
    nd = a.ndim
    return pl.BlockSpec(a.shape, lambda i: (0,) * nd)


def _cparams(*sem):
    return pltpu.CompilerParams(dimension_semantics=sem, vmem_limit_bytes=VMEM_LIMIT)


def scan_fwd(fn, name, tile, seqs, params, carry_shapes, outs, save_carry=False):
    rows = seqs[0].shape[0]
    tile = min(tile, rows)
    n = rows // tile
    ns, npar, nc, no = len(seqs), len(params), len(carry_shapes), len(outs)

    def body(*refs):
        seq_refs, refs = refs[:ns], refs[ns:]
        par_refs, refs = refs[:npar], refs[npar:]
        out_refs, refs = refs[:no], refs[no:]
        save_refs, refs = (refs[:nc], refs[nc:]) if save_carry else ((), refs)
        carry_refs = refs

        @pl.when(pl.program_id(0) == 0)
        def _():
            for cr in carry_refs:
                cr[...] = jnp.zeros_like(cr)

        carry = tuple(cr[...] for cr in carry_refs)
        for sr, cv in zip(save_refs, carry):
            sr[0] = cv
        new_carry, res = fn(carry, tuple(r[...].astype(F32) for r in seq_refs),
                            tuple(r[...].astype(F32) for r in par_refs))
        for r, v in zip(out_refs, res):
            r[...] = v.astype(r.dtype)
        for cr, v in zip(carry_refs, new_carry):
            cr[...] = v

    out_shape = [jax.ShapeDtypeStruct((rows, w), dt) for w, dt in outs]
    out_specs = [pl.BlockSpec((tile, w), lambda i: (i, 0)) for w, _ in outs]
    if save_carry:
        out_shape += [jax.ShapeDtypeStruct((n, *s), F32) for s in carry_shapes]
        out_specs += [pl.BlockSpec((1, *s), lambda i: (i, 0, 0)) for s in carry_shapes]
    res = pl.pallas_call(
        body, name=name, grid=(n,),
        in_specs=[pl.BlockSpec((tile, s.shape[1]), lambda i: (i, 0)) for s in seqs] + [_full_spec(p) for p in params],
        out_specs=out_specs, out_shape=out_shape,
        scratch_shapes=[pltpu.VMEM(s, F32) for s in carry_shapes],
        compiler_params=_cparams("arbitrary"),
    )(*seqs, *params)
    return res[:no], res[no:]


def scan_bwd(fn, name, tile, seqs, params, saved, douts, n_dseq, n_dpar, dseq_dtypes=None):
    dseq_dtypes = dseq_dtypes or [F32] * n_dseq
    rows = seqs[0].shape[0]
    tile = min(tile, rows)
    n = rows // tile
    ns, npar, nc, no = len(seqs), len(params), len(saved), len(douts)

    def body(*refs):
        seq_refs, refs = refs[:ns], refs[ns:]
        par_refs, refs = refs[:npar], refs[npar:]
        save_refs, refs = refs[:nc], refs[nc:]
        dout_refs, refs = refs[:no], refs[no:]
        dseq_refs, refs = refs[:n_dseq], refs[n_dseq:]
        dpar_refs, refs = refs[:n_dpar], refs[n_dpar:]
        dcarry_refs = refs

        @pl.when(pl.program_id(0) == 0)
        def _():
            for r in (*dpar_refs, *dcarry_refs):
                r[...] = jnp.zeros_like(r)

        carry = tuple(r[0] for r in save_refs)
        seq = tuple(r[...].astype(F32) for r in seq_refs)
        par = tuple(r[...].astype(F32) for r in par_refs)

        def f(carry, dseq, dpar):
            return fn(carry, (*dseq, *seq[n_dseq:]), (*dpar, *par[n_dpar:]))

        _, vjp = jax.vjp(f, carry, seq[:n_dseq], par[:n_dpar])
        d_carry, d_seq, d_par = vjp((tuple(r[...] for r in dcarry_refs),
                                     tuple(r[...].astype(F32) for r in dout_refs)))
        for r, v in zip(dseq_refs, d_seq):
            r[...] = v.astype(r.dtype)
        for r, v in zip(dpar_refs, d_par):
            r[...] += v
        for r, v in zip(dcarry_refs, d_carry):
            r[...] = v

    rev = lambda i: (n - 1 - i, 0)
    res = pl.pallas_call(
        body, name=name, grid=(n,),
        in_specs=([pl.BlockSpec((tile, s.shape[1]), rev) for s in seqs] + [_full_spec(p) for p in params]
                  + [pl.BlockSpec((1, *s.shape[1:]), lambda i: (n - 1 - i, 0, 0)) for s in saved]
                  + [pl.BlockSpec((tile, d.shape[1]), rev) for d in douts]),
        out_specs=([pl.BlockSpec((tile, s.shape[1]), rev) for s in seqs[:n_dseq]]
                   + [_full_spec(p) for p in params[:n_dpar]]),
        out_shape=([jax.ShapeDtypeStruct((rows, s.shape[1]), dt) for s, dt in zip(seqs[:n_dseq], dseq_dtypes)]
                   + [jax.ShapeDtypeStruct(p.shape, F32) for p in params[:n_dpar]]),
        scratch_shapes=[pltpu.VMEM(s.shape[1:], F32) for s in saved],
        compiler_params=_cparams("arbitrary"),
    )(*seqs, *params, *saved, *douts)
    return res[:n_dseq], res[n_dseq:]


def _tile_of(dim, pref):
    if dim <= pref:
        return dim
    best = max((t for t in range(128, pref + 1, 128) if dim % t == 0), default=None)
    if best is None or (best < 512 and dim <= 2304):
        return dim
    return best


def _row_tile(rows, pref):
    if rows <= pref:
        return rows
    return max(t for t in range(8, pref + 1, 8) if rows % t == 0)


def matmul(a, b, name, ta=False, tb=False, out_dtype=F32, add=None, bias=None, relu2=False, relu2_of=None,
           tm=1024, tn=2048, tk=1024):
    m, k = (a.shape[1], a.shape[0]) if ta else a.shape
    n = b.shape[0] if tb else b.shape[1]
    assert k == (b.shape[1] if tb else b.shape[0])
    tm, tn, tk = _tile_of(m, tm), _tile_of(n, tn), _tile_of(k, tk)
    nm, nn, nk = m // tm, n // tn, k // tk
    assert nk == 1 or (out_dtype == F32 and not relu2 and relu2_of is None)
    dn = (((0 if ta else 1,), (1 if tb else 0,)), ((), ()))
    has_add, has_bias, has_u = add is not None, bias is not None, relu2_of is not None
    n_inner = a.size * a.dtype.itemsize * (nn - 1) >= b.size * b.dtype.itemsize * (nm - 1)
    ij = (lambda g0, g1: (g0, g1)) if n_inner else (lambda g0, g1: (g1, g0))

    def body(*refs):
        a_ref, b_ref, refs = refs[0], refs[1], refs[2:]
        add_ref, refs = (refs[0], refs[1:]) if has_add else (None, refs)
        bias_ref, refs = (refs[0], refs[1:]) if has_bias else (None, refs)
        u_ref, refs = (refs[0], refs[1:]) if has_u else (None, refs)
        o_ref = refs[0]
        r = lax.dot_general(a_ref[...].astype(BF16), b_ref[...].astype(BF16), dn, preferred_element_type=F32)

        def first():
            v = r
            if has_add:
                v = v + add_ref[...]
            if has_bias:
                v = v + bias_ref[...]
            if has_u:
                v = v * (2.0 * jnp.maximum(u_ref[...], 0.0))
            o_ref[...] = v.astype(o_ref.dtype)
            if relu2:
                p = jnp.maximum(v, 0.0)
                refs[1][...] = (p * p).astype(BF16)

        if nk == 1:
            first()
        else:
            pl.when(pl.program_id(2) == 0)(first)

            @pl.when(pl.program_id(2) > 0)
            def _():
                o_ref[...] += r

    def spec(shape, fn):
        return pl.BlockSpec(shape, lambda g0, g1, l: fn(*ij(g0, g1), l))

    a_spec = spec((tk, tm), lambda i, j, l: (l, i)) if ta else spec((tm, tk), lambda i, j, l: (i, l))
    b_spec = spec((tn, tk), lambda i, j, l: (j, l)) if tb else spec((tk, tn), lambda i, j, l: (l, j))
    o_spec = spec((tm, tn), lambda i, j, l: (i, j))
    in_specs, args = [a_spec, b_spec], [a, b]
    if has_add:
        in_specs.append(o_spec)
        args.append(add)
    if has_bias:
        in_specs.append(spec((1, tn), lambda i, j, l: (0, j)))
        args.append(bias)
    if has_u:
        in_specs.append(o_spec)
        args.append(relu2_of)
    out_shape = [jax.ShapeDtypeStruct((m, n), out_dtype)] + ([jax.ShapeDtypeStruct((m, n), BF16)] if relu2 else [])
    res = pl.pallas_call(
        body, name=name, grid=(nm, nn, nk) if n_inner else (nn, nm, nk), in_specs=in_specs,
        out_specs=[o_spec] * len(out_shape), out_shape=out_shape,
        compiler_params=_cparams("parallel", "parallel", "arbitrary"),
    )(*args)
    return res if relu2 else res[0]


LIN_TILE = 512


def linscan_fwd(a, u, name):
    rows, w = a.shape
    tile = min(LIN_TILE, rows)

    def body(a_ref, u_ref, h_ref, hc):
        @pl.when(pl.program_id(0) == 0)
        def _():
            hc[...] = jnp.zeros_like(hc)

        row = lax.broadcasted_iota(jnp.int32, (8, 1), 0)

        def group(k, h_in):
            rows8 = pl.ds(pl.multiple_of(k * 8, 8), 8)
            pa, pu = a_ref[rows8, :], u_ref[rows8, :]
            for d in (1, 2, 4):
                pu = pu + pa * jnp.where(row >= d, pltpu.roll(pu, d, 0), 0.0)
                pa = pa * jnp.where(row >= d, pltpu.roll(pa, d, 0), 1.0)
            h_ref[rows8, :] = pa * h_in + pu
            return h_ref[pl.ds(k * 8 + 7, 1), :]

        hc[...] = lax.fori_loop(0, tile // 8, group, hc[...], unroll=4)

    spec = pl.BlockSpec((tile, w), lambda i: (i, 0))
    return pl.pallas_call(
        body, name=name, grid=(rows // tile,), in_specs=[spec, spec], out_specs=spec,
        out_shape=jax.ShapeDtypeStruct((rows, w), F32), scratch_shapes=[pltpu.VMEM((1, w), F32)],
        compiler_params=_cparams("arbitrary"),
    )(a, u)


def linscan_bwd(a, hs, dh, name):
    rows, w = a.shape
    tile = min(LIN_TILE, rows)
    n = rows // tile
    per = tile // 8

    def body(a_ref, h_ref, hprev_ref, dh_ref, da_ref, du_ref, cc):
        i = pl.program_id(0)

        @pl.when(i == 0)
        def _():
            cc[...] = jnp.zeros_like(cc)

        row = lax.broadcasted_iota(jnp.int32, (8, 1), 0)
        h_before = jnp.where(i == n - 1, 0.0, hprev_ref[7:8, :])

        def group(s, c_in):
            k = per - 1 - s
            rows8 = pl.ds(pl.multiple_of(k * 8, 8), 8)
            av, hv = a_ref[rows8, :], h_ref[rows8, :]
            pb = jnp.where(row < 7, pltpu.roll(av, 7, 0), 1.0)
            pg = dh_ref[rows8, :]
            for d in (1, 2, 4):
                pg = pg + pb * jnp.where(row < 8 - d, pltpu.roll(pg, 8 - d, 0), 0.0)
                pb = pb * jnp.where(row < 8 - d, pltpu.roll(pb, 8 - d, 0), 1.0)
            g = pg + pb * c_in
            du_ref[rows8, :] = g
            h_prev = jnp.where(k == 0, h_before, h_ref[pl.ds(jnp.maximum(k * 8 - 1, 0), 1), :])
            da_ref[rows8, :] = g * jnp.where(row >= 1, pltpu.roll(hv, 1, 0), h_prev)
            return a_ref[pl.ds(k * 8, 1), :] * du_ref[pl.ds(k * 8, 1), :]

        cc[...] = lax.fori_loop(0, per, group, cc[...], unroll=4)

    rev = pl.BlockSpec((tile, w), lambda i: (n - 1 - i, 0))
    prev = pl.BlockSpec((8, w), lambda i: (jnp.maximum((n - 1 - i) * per - 1, 0), 0))
    return pl.pallas_call(
        body, name=name, grid=(n,), in_specs=[rev, rev, prev, rev], out_specs=[rev, rev],
        out_shape=[jax.ShapeDtypeStruct((rows, w), F32)] * 2, scratch_shapes=[pltpu.VMEM((1, w), F32)],
        compiler_params=_cparams("arbitrary"),
    )(a, hs, hs, dh)


def loss_head(x, target, w, name):
    rows, d = x.shape
    tile = min(512, rows)

    def body(x_ref, t_ref, w_ref, loss_ref, dx_ref, dw_ref):
        @pl.when(pl.program_id(0) == 0)
        def _():
            loss_ref[...] = jnp.zeros_like(loss_ref)
            dw_ref[...] = jnp.zeros_like(dw_ref)

        tv = t_ref[...]

        def f(xv, wv):
            e = _rms(xv, wv) - tv
            return 0.5 * jnp.sum(jnp.mean(e * e, axis=-1, keepdims=True), axis=0, keepdims=True)

        val, vjp = jax.vjp(f, x_ref[...], w_ref[...])
        dxv, dwv = vjp(jnp.ones((1, 1), F32))
        loss_ref[...] += jnp.broadcast_to(val, loss_ref.shape)
        dx_ref[...] = dxv
        dw_ref[...] += dwv

    spec = pl.BlockSpec((tile, d), lambda i: (i, 0))
    return pl.pallas_call(
        body, name=name, grid=(rows // tile,), in_specs=[spec, spec, _full_spec(w)],
        out_specs=[pl.BlockSpec((8, 128), lambda i: (0, 0)), spec, _full_spec(w)],
        out_shape=[jax.ShapeDtypeStruct((8, 128), F32), jax.ShapeDtypeStruct((rows, d), F32),
                   jax.ShapeDtypeStruct(w.shape, F32)],
        compiler_params=_cparams("arbitrary"),
    )(x, target, w)


def adamw(g, w, m, v, name, copy_g=False):
    layers, rows, cols = g.shape
    tile = _row_tile(rows, 256)
    n_out = 4 if copy_g else 3

    def body(g_ref, w_ref, m_ref, v_ref, d_ref, nm_ref, nv_ref, *g_out):
        gv = g_ref[...]
        if copy_g:
            g_out[0][...] = gv
        nm = ADAM_B1 * m_ref[...] + (1.0 - ADAM_B1) * gv
        nv = ADAM_B2 * v_ref[...] + (1.0 - ADAM_B2) * (gv * gv)
        m_hat = nm / (1.0 - ADAM_B1 ** ADAM_STEP)
        v_hat = nv / (1.0 - ADAM_B2 ** ADAM_STEP)
        d_ref[...] = -ADAM_LR * (m_hat / (jnp.sqrt(v_hat) + ADAM_EPS) + ADAM_WD * w_ref[...])
        nm_ref[...] = nm
        nv_ref[...] = nv

    spec = pl.BlockSpec((None, tile, cols), lambda l, i: (l, i, 0))
    return pl.pallas_call(
        body, name=name, grid=(layers, rows // tile), in_specs=[spec] * 4, out_specs=[spec] * n_out,
        out_shape=[jax.ShapeDtypeStruct((layers, rows, cols), F32)] * n_out,
        compiler_params=_cparams("parallel", "parallel"),
    )(g, w, m, v)


def add_cast(g0, g1, sib, place, name):
    shape = sib.shape
    g0, g1, sib = (a.reshape(-1, shape[-1]) for a in (g0, g1, sib))
    rows, cols = sib.shape
    tile = _row_tile(rows, max(8, min(256, (512 * 1024) // cols)))

    def body(k_ref, g0_ref, g1_ref, s_ref, o_ref, ob_ref):
        s = jnp.where(k_ref[1] == 0, g0_ref[...], g1_ref[...]) + s_ref[...]
        o_ref[...] = s
        ob_ref[...] = s.astype(BF16)

    spec = pl.BlockSpec((tile, cols), lambda i, k: (i, 0))
    s, sb = pl.pallas_call(
        body, name=name,
        grid_spec=pltpu.PrefetchScalarGridSpec(
            num_scalar_prefetch=1, grid=(rows // tile,),
            in_specs=[pl.BlockSpec((tile, cols), lambda i, k: (i * (1 - k[1]), 0)),
                      pl.BlockSpec((tile, cols), lambda i, k: (i * k[1], 0)), spec],
            out_specs=[spec, spec]),
        out_shape=[jax.ShapeDtypeStruct((rows, cols), F32), jax.ShapeDtypeStruct((rows, cols), BF16)],
        compiler_params=_cparams("arbitrary"),
    )(place, g0, g1, sib)
    return s.reshape(shape), sb.reshape(shape)


def sum4(own, recv, by_cols, place, name):
    _, r, c = recv.shape
    tile = _row_tile(r, 256)
    nt = r // tile
    own_map = (lambda i, k: (i, k[0])) if by_cols else (lambda i, k: (k[0] * nt + i, 0))

    def body(k_ref, own_ref, recv_ref, o_ref):
        o_ref[...] = ((own_ref[...] + recv_ref[0].astype(F32)) + recv_ref[1].astype(F32)) + recv_ref[2].astype(F32)

    return pl.pallas_call(
        body, name=name,
        grid_spec=pltpu.PrefetchScalarGridSpec(
            num_scalar_prefetch=1, grid=(nt,),
            in_specs=[pl.BlockSpec((tile, c), own_map), pl.BlockSpec((3, tile, c), lambda i, k: (0, i, 0))],
            out_specs=pl.BlockSpec((None, tile, c), lambda i, k: (k[1], i, 0))),
        out_shape=jax.ShapeDtypeStruct((2, r, c), F32),
        compiler_params=_cparams("arbitrary"),
    )(place, own, recv)


def add8(parts, name):
    _, rows, cols = parts.shape

    def body(p_ref, o_ref):
        acc = p_ref[0]
        for k in range(1, 8):
            acc = acc + p_ref[k]
        o_ref[...] = acc

    return pl.pallas_call(
        body, name=name, in_specs=[pl.BlockSpec(memory_space=pltpu.VMEM)],
        out_specs=pl.BlockSpec(memory_space=pltpu.VMEM),
        out_shape=jax.ShapeDtypeStruct((rows, cols), F32),
        compiler_params=pltpu.CompilerParams(vmem_limit_bytes=VMEM_LIMIT),
    )(parts)


def _place():
    return lax.axis_index("x"), lax.axis_index("y"), lax.axis_index("c")


def _other_chips(x, y):
    return [(1 - x, y), (x, 1 - y), (1 - x, 1 - y)]


_ANY = pl.BlockSpec(memory_space=pl.ANY)


BIG_LAYOUT = (("ada_w", "col", (1024, 6144)), ("w_in", "chip", (4, 1024, 1924)), ("w_branch", "col", (3, 512, 1024)),
              ("w_out", "row", (1024, 1024)), ("w_up", "col", (1024, 4096)), ("w_down", "row", (4096, 1024)))
N_BIG = len(BIG_LAYOUT)
REDUCE_LAYOUT = BIG_LAYOUT[1:]


def _local_shape(kind, full):
    if kind == "col":
        return (*full[:-1], full[-1] // 4)
    if kind == "row":
        return (full[0] // 4, *full[1:])
    return full[1:]


def _window(ref, kind, k, local):
    if kind == "chip":
        return ref.at[k]
    if kind == "row":
        return ref.at[pl.ds(pl.multiple_of(k * local[0], 8), local[0])]
    idx = (slice(None),) * (len(local) - 1) + (pl.ds(pl.multiple_of(k * local[-1], 128), local[-1]),)
    return ref.at[idx]


def _dma_call(body, name, n_in, out_shape, sems, aliases=None):
    return pl.pallas_call(
        body, name=name, in_specs=[_ANY] * n_in, out_specs=[_ANY] * len(out_shape), out_shape=out_shape,
        scratch_shapes=[pltpu.SemaphoreType.DMA((n,)) for n in sems],
        input_output_aliases=aliases or {},
        compiler_params=pltpu.CompilerParams(has_side_effects=True))


def _remote(src, dst, send_sem, recv_sem, to):
    return pltpu.make_async_remote_copy(src_ref=src, dst_ref=dst, send_sem=send_sem, recv_sem=recv_sem,
                                        device_id=to, device_id_type=MESH)


CORE_PARAMS = ((1, 4), (0, 2, 3, 5))


def gather_layer(shards, name):
    locals_ = [_local_shape(kind, full) for _, kind, full in BIG_LAYOUT]

    def body(*refs):
        sh, full, (send_sems, recv_sems, local_sems, pass_send, pass_recv) = (
            refs[:N_BIG], refs[N_BIG:2 * N_BIG], refs[2 * N_BIG:])
        x, y, c = _place()
        me = 2 * x + y
        chips = _other_chips(x, y)
        win = lambda n, k: _window(full[n], BIG_LAYOUT[n][1], k, locals_[n])
        for cc in (0, 1):
            @pl.when(c == cc)
            def _():
                mine, sends = {}, []
                for n in CORE_PARAMS[cc]:
                    mine[n] = pltpu.make_async_copy(sh[n], win(n, me), local_sems.at[n])
                    mine[n].start()
                    for j, chip in enumerate(chips):
                        sends.append(_remote(sh[n], win(n, me), send_sems.at[3 * n + j], recv_sems.at[3 * n + j],
                                             (chip[0], chip[1], c)))
                        sends[-1].start()
                for n in CORE_PARAMS[cc]:
                    for j, chip in enumerate(chips):
                        _remote(sh[n], win(n, 2 * chip[0] + chip[1]), send_sems.at[3 * n + j],
                                recv_sems.at[3 * n + j], (chip[0], chip[1], c)).wait_recv()
                    mine[n].wait()
                    sends.append(_remote(full[n], full[n], pass_send.at[n], pass_recv.at[n], (x, y, 1 - c)))
                    sends[-1].start()
                for n in CORE_PARAMS[1 - cc]:
                    _remote(full[n], full[n], pass_send.at[n], pass_recv.at[n], (x, y, 1 - c)).wait_recv()
                for cp in sends:
                    cp.wait_send()

    out_shape = [jax.ShapeDtypeStruct(full, BF16) for _, _, full in BIG_LAYOUT]
    return _dma_call(body, name, N_BIG, out_shape, (3 * N_BIG, 3 * N_BIG, N_BIG, N_BIG, N_BIG))(*shards)


_HBM = pl.BlockSpec(memory_space=pltpu.HBM)
_SEM = pl.BlockSpec(memory_space=pltpu.SEMAPHORE)


def _hbm(a):
    return pltpu.with_memory_space_constraint(a, pltpu.HBM)


def _gather_copies(sh, full, send_sems, recv_sems):
    locals_ = [_local_shape(kind, f) for _, kind, f in BIG_LAYOUT]
    x, y, c = _place()
    me = 2 * x + y
    pairs = []
    for n in range(N_BIG):
        win = lambda k: _window(full[n], BIG_LAYOUT[n][1], k, locals_[n])
        for j, chip in enumerate(_other_chips(x, y)):
            mk = lambda dst: _remote(sh[n], dst, send_sems.at[3 * n + j], recv_sems.at[3 * n + j],
                                     (chip[0], chip[1], c))
            pairs.append((mk(win(me)), mk(win(2 * chip[0] + chip[1]))))
    return pairs


def gather_start(shards, fulls, after, name):
    def body(*refs):
        sh, full = refs[:N_BIG], refs[N_BIG:2 * N_BIG]
        send_sems, recv_sems = refs[2 * N_BIG + 1:2 * N_BIG + 3]
        for out, _ in _gather_copies(sh, full, send_sems, recv_sems):
            out.start()
        refs[-1][...] = jnp.zeros_like(refs[-1])

    thru = [pltpu.HBM(a.shape, a.dtype) for a in (*shards, *fulls)]
    res = pl.pallas_call(
        body, name=name,
        out_shape=(pltpu.SemaphoreType.DMA((3 * N_BIG,)), pltpu.SemaphoreType.DMA((3 * N_BIG,)), *thru,
                   jax.ShapeDtypeStruct((8, 128), F32)),
        in_specs=[_HBM] * (2 * N_BIG) + [_ANY],
        out_specs=(_SEM, _SEM, *[_HBM] * (2 * N_BIG), pl.BlockSpec(memory_space=pltpu.VMEM)),
        input_output_aliases={i: 2 + i for i in range(2 * N_BIG)},
        compiler_params=pltpu.CompilerParams(has_side_effects=pltpu.SideEffectType.DATAFLOW_SIDE_EFFECTING),
    )(*[_hbm(a) for a in (*shards, *fulls)], after)
    return (res[0], res[1], res[2:2 + N_BIG], res[2 + N_BIG:2 + 2 * N_BIG]), res[-1]


def gather_wait(send_sems, recv_sems, shards, fulls, after, name):
    def body(*refs):
        sh, full = refs[:N_BIG], refs[N_BIG:2 * N_BIG]
        ssem, rsem = refs[2 * N_BIG:2 * N_BIG + 2]
        for out, inc in _gather_copies(sh, full, ssem, rsem):
            out.wait_send()
            inc.wait_recv()

    thru = [pltpu.HBM(a.shape, a.dtype) for a in (*shards, *fulls)]
    res = pl.pallas_call(
        body, name=name, out_shape=thru,
        in_specs=[_HBM] * (2 * N_BIG) + [_SEM, _SEM, _ANY], out_specs=[_HBM] * (2 * N_BIG),
        input_output_aliases={i: i for i in range(2 * N_BIG)},
        compiler_params=pltpu.CompilerParams(has_side_effects=pltpu.SideEffectType.DATAFLOW_SIDE_EFFECTING),
    )(*shards, *fulls, send_sems, recv_sems, after)
    return res[N_BIG:]


def reduce_d2d(g0, g1, name):
    nb = len(g0)

    def body(*refs):
        g, refs = (refs[:nb], refs[nb:2 * nb]), refs[2 * nb:]
        sib, (send_sems, recv_sems) = refs[:nb], refs[nb:]
        x, y, c = _place()
        for cc in (0, 1):
            @pl.when(c == cc)
            def _():
                sends = [_remote(g[1 - cc][n], sib[n], send_sems.at[n], recv_sems.at[n], (x, y, 1 - c))
                         for n in range(nb)]
                for cp in sends:
                    cp.start()
                for cp in sends:
                    cp.wait_recv()
                for cp in sends:
                    cp.wait_send()

    out_shape = [jax.ShapeDtypeStruct(a.shape, a.dtype) for a in g0]
    return _dma_call(body, name, 2 * nb, out_shape, (nb, nb))(*g0, *g1)


def reduce_ici(sums, layout, name):
    nb = len(sums)
    locals_ = [_local_shape(kind, full) for _, kind, full in layout]

    def body(*refs):
        src, recv, (send_sems, recv_sems) = refs[:nb], refs[nb:2 * nb], refs[2 * nb:]
        x, y, c = _place()
        chips = _other_chips(x, y)
        copies = []
        for n in range(nb):
            for j, chip in enumerate(chips):
                cp = _remote(_window(src[n], layout[n][1], 2 * chip[0] + chip[1], locals_[n]), recv[n].at[j],
                             send_sems.at[3 * n + j], recv_sems.at[3 * n + j], (chip[0], chip[1], c))
                cp.start()
                copies.append(cp)
        for cp in copies:
            cp.wait_recv()
        for cp in copies:
            cp.wait_send()

    out_shape = [jax.ShapeDtypeStruct((3, *ls), a.dtype) for ls, a in zip(locals_, sums)]
    return _dma_call(body, name, nb, out_shape, (3 * nb, 3 * nb))(*sums)


def share_d2d(finals, name):
    nb = len(finals)

    def body(*refs):
        out, (send_sems, recv_sems) = refs[nb:2 * nb], refs[2 * nb:]
        x, y, c = _place()
        sends = [_remote(out[n].at[c], out[n].at[c], send_sems.at[n], recv_sems.at[n], (x, y, 1 - c))
                 for n in range(nb)]
        for cp in sends:
            cp.start()
        for n in range(nb):
            _remote(out[n].at[c], out[n].at[1 - c], send_sems.at[n], recv_sems.at[n], (x, y, 1 - c)).wait_recv()
        for cp in sends:
            cp.wait_send()

    out_shape = [jax.ShapeDtypeStruct(a.shape, a.dtype) for a in finals]
    return _dma_call(body, name, nb, out_shape, (nb, nb), aliases={n: n for n in range(nb)})(*finals)


def allgather8(block, name):
    m_per, n = block.shape

    def body(x_ref, out_ref, send_sems, recv_sems, local_sem):
        x, y, c = _place()
        me, sibling = (x, y, c), (x, y, 1 - c)
        chips = _other_chips(x, y)

        def rows(px, py, pc):
            return out_ref.at[pl.ds((4 * px + 2 * py + pc) * m_per, m_per), :]

        def copy(k, blk, to, src=None):
            return pltpu.make_async_remote_copy(
                src_ref=rows(*blk) if src is None else src, dst_ref=rows(*blk), send_sem=send_sems.at[k],
                recv_sem=recv_sems.at[k], device_id=to, device_id_type=MESH)

        mine = pltpu.make_async_copy(x_ref, rows(*me), local_sem)
        mine.start()
        first = [copy(0, me, sibling, src=x_ref)]
        first += [copy(1 + j, me, (*chip, c), src=x_ref) for j, chip in enumerate(chips)]
        for cp in first:
            cp.start()
        passed = [copy(4 + j, (*chip, c), sibling) for j, chip in enumerate(chips)]
        for j, chip in enumerate(chips):
            copy(1 + j, (*chip, c), me).wait_recv()
            passed[j].start()
        copy(0, sibling, me).wait_recv()
        for j, chip in enumerate(chips):
            copy(4 + j, (*chip, 1 - c), me).wait_recv()
        for cp in first + passed:
            cp.wait_send()
        mine.wait()

    return pl.pallas_call(
        body, name=name, in_specs=[pl.BlockSpec(memory_space=pltpu.VMEM)],
        out_specs=pl.BlockSpec(memory_space=pltpu.VMEM),
        out_shape=jax.ShapeDtypeStruct((8 * m_per, n), block.dtype),
        scratch_shapes=[pltpu.SemaphoreType.DMA((7,)), pltpu.SemaphoreType.DMA((7,)), pltpu.SemaphoreType.DMA],
        compiler_params=pltpu.CompilerParams(vmem_limit_bytes=VMEM_LIMIT),
    )(block)


CONV = ("gdn_conv_w", "ssd_conv_w", "lru_conv_w")
SMALL = ("ada_b", "norm_mix", "gdn_a_log", "gdn_dt_bias", "gdn_norm", "ssd_conv_b", "ssd_a_log", "ssd_dt_bias",
         "ssd_d", "ssd_norm", "lru_conv_b", "lru_w_a", "lru_b_a", "lru_w_x", "lru_b_x", "lru_lambda", "norm_mlp",
         "final_norm")
WEIGHTS = ("ada_w", "ada_b", "norm_mix", "w_in", "gdn_conv_w", "gdn_a_log", "gdn_dt_bias", "gdn_norm", "ssd_conv_w",
           "ssd_conv_b", "ssd_a_log", "ssd_dt_bias", "ssd_d", "ssd_norm", "lru_conv_w", "lru_conv_b", "lru_w_a",
           "lru_b_a", "lru_w_x", "lru_b_x", "lru_lambda", "w_branch", "w_out", "norm_mlp", "w_up", "w_down",
           "final_norm")
PACK_COLS = 1024


def _pack_rows(shape):
    return 8 * -(-math.prod(shape) // (8 * PACK_COLS))


def _pack(arrays, dtype):
    parts = []
    for a in arrays:
        flat = a.reshape(-1).astype(dtype)
        pad = _pack_rows(a.shape) * PACK_COLS - flat.shape[0]
        parts.append((jnp.concatenate([flat, jnp.zeros((pad,), dtype)]) if pad else flat).reshape(-1, PACK_COLS))
    return jnp.concatenate(parts, axis=0)


def _unpack(pack, shapes):
    out, o = [], 0
    for s in shapes:
        r = _pack_rows(s)
        out.append(pack[o:o + r].reshape(-1)[:math.prod(s)].reshape(s))
        o += r
    return out


def _split_w_in(w4):
    w = jnp.concatenate([w4[k] for k in range(4)], axis=1)
    pad = jnp.zeros((w.shape[0], 120), w.dtype)
    gdn = jnp.concatenate([w[:, 0:2056], pad], axis=1)
    ssd = jnp.concatenate([w[:, 2056:2568], w[:, 3080:3592], w[:, 2568:3080], w[:, 3592:3600], pad], axis=1)
    return gdn, ssd, w[:, 3600:4112], w[:, 4112:4624], w[:, 4624:7696]


def _join_w_in(gdn, ssd, lx, lg, gate):
    w = jnp.concatenate([gdn[:, 0:2056], ssd[:, 0:512], ssd[:, 1024:1536], ssd[:, 512:1024], ssd[:, 1536:1544],
                         lx, lg, gate], axis=1)
    return jnp.stack([w[:, k * 1924:(k + 1) * 1924] for k in range(4)])


def _lanes(v, at, width=128):
    return jnp.zeros((1, width), F32).at[0, at:at + v.shape[0]].set(v)


def _block_diag(w):
    return (jnp.eye(8, dtype=w.dtype)[:, None, :, None] * w[:, :, None, :]).reshape(512, 512)


def _diag_blocks(w):
    return jnp.stack([w[n * 64:(n + 1) * 64, n * 64:(n + 1) * 64] for n in range(8)])


TOK_TILE = 512
WIDE_TILE = 256


def _layer_params(p, big, l):
    row = lambda v: v.reshape(1, -1)
    b = dict(zip((n for n, _, _ in BIG_LAYOUT), big))
    gdn = (p["gdn_conv_w"][l], _lanes(p["gdn_a_log"][l], 4), _lanes(p["gdn_dt_bias"][l], 4), row(p["gdn_norm"][l]))
    ssd = (p["ssd_conv_w"][l], row(p["ssd_conv_b"][l]), _lanes(p["ssd_a_log"][l], 0), _lanes(p["ssd_dt_bias"][l], 0),
           row(jnp.repeat(p["ssd_d"][l], 64)), row(p["ssd_norm"][l]))
    lru = (p["lru_conv_w"][l], row(p["lru_conv_b"][l]), _block_diag(p["lru_w_a"][l]), row(p["lru_b_a"][l]),
           _block_diag(p["lru_w_x"][l]), row(p["lru_b_x"][l]), row(p["lru_lambda"][l]))
    return dict(gdn=gdn, ssd=ssd, lru=lru, w_in=_split_w_in(b["w_in"]),
                wb=tuple(b["w_branch"][r] for r in range(3)), w_out=b["w_out"], w_up=b["w_up"],
                w_down=b["w_down"], ada_w=b["ada_w"], ada_b=row(p["ada_b"][l]),
                norm_mix=row(p["norm_mix"][l]), norm_mlp=row(p["norm_mlp"][l]))


def _layer_fwd(x, silu_c, lp, l):
    nm = lambda s: f"l{l}_{s}"
    mod = matmul(silu_c, lp["ada_w"], nm("mod"), bias=lp["ada_b"])
    sh1, sc1, gt1, sh2, sc2, gt2 = (mod[0:1, k * D_MODEL:(k + 1) * D_MODEL] for k in range(N_MOD))
    (h,), _ = scan_fwd(norm1_fn, nm("norm1"), TOK_TILE, [x], [lp["norm_mix"], sc1, sh1], [], [(D_MODEL, BF16)])
    w_gdn, w_ssd, w_lx, w_lg, w_gate = lp["w_in"]
    p_gdn = matmul(h, w_gdn, nm("in_gdn"))
    p_ssd = matmul(h, w_ssd, nm("in_ssd"))
    p_lx = matmul(h, w_lx, nm("in_lx"))
    p_lg = matmul(h, w_lg, nm("in_lg"))
    p_gate = matmul(h, w_gate, nm("in_gate"))
    (ya,), sv_gdn = scan_fwd(gdn_fn, nm("gdn"), CHUNK, [p_gdn], lp["gdn"], [(128, 128)] * 4 + [(8, 1536)],
                             [(512, F32)], save_carry=True)
    (yb,), sv_ssd = scan_fwd(ssd_fn, nm("ssd"), CHUNK, [p_ssd], lp["ssd"], [(128, 128)] * 4 + [(8, 1024)],
                             [(512, F32)], save_carry=True)
    (a, u), sv_lru = scan_fwd(lru_in_fn, nm("lru_in"), TOK_TILE, [p_lx], lp["lru"], [(8, 512)],
                              [(512, F32), (512, F32)], save_carry=True)
    hs = linscan_fwd(a, u, nm("lru_scan"))
    (yc,), _ = scan_fwd(lru_out_fn, nm("lru_out"), TOK_TILE, [hs, p_lg], [], [], [(512, F32)])
    (merged,), _ = scan_fwd(merge_fn, nm("merge"), WIDE_TILE, [ya, yb, yc, p_gate], lp["wb"], [], [(D_MODEL, BF16)])
    mix = matmul(merged, lp["w_out"], nm("out"))
    (h2, x1), _ = scan_fwd(resid_norm_fn, nm("norm2"), TOK_TILE, [x, mix], [gt1, lp["norm_mlp"], sc2, sh2], [],
                           [(D_MODEL, BF16), (D_MODEL, F32)])
    up, act = matmul(h2, lp["w_up"], nm("up"), relu2=True)
    dn = matmul(act, lp["w_down"], nm("down"))
    (x2,), _ = scan_fwd(resid_fn, nm("resid"), TOK_TILE, [x1, dn], [gt2], [], [(D_MODEL, F32)])
    saved = dict(x=x, h=h, p_gdn=p_gdn, p_ssd=p_ssd, p_lx=p_lx, p_lg=p_lg, p_gate=p_gate, sv_gdn=sv_gdn,
                 sv_ssd=sv_ssd, sv_lru=sv_lru, a=a, hs=hs, ya=ya, yb=yb, yc=yc, merged=merged, mix=mix, x1=x1,
                 h2=h2, up=up, act=act, dn=dn, mod=(sh1, sc1, gt1, sh2, sc2, gt2))
    return x2, saved


def _layer_bwd(d_x2, silu_c, lp, sv, l):
    nm = lambda s: f"l{l}_b_{s}"
    sh1, sc1, gt1, sh2, sc2, gt2 = sv["mod"]
    (d_x1, d_dn), (d_gt2,) = scan_bwd(resid_fn, nm("resid"), TOK_TILE, [sv["x1"], sv["dn"]], [gt2], [], [d_x2], 2, 1,
                                      [F32, BF16])
    d_up = matmul(d_dn, lp["w_down"], nm("down_x"), tb=True, relu2_of=sv["up"], out_dtype=BF16)
    g_w_down = matmul(sv["act"], d_dn, nm("down_w"), ta=True)
    d_h2 = matmul(d_up, lp["w_up"], nm("up_x"), tb=True)
    g_w_up = matmul(sv["h2"], d_up, nm("up_w"), ta=True)
    (d_x, d_mix), (d_gt1, g_norm_mlp, d_sc2, d_sh2) = scan_bwd(
        resid_norm_fn, nm("norm2"), TOK_TILE, [sv["x"], sv["mix"]], [gt1, lp["norm_mlp"], sc2, sh2], [],
        [d_h2, d_x1], 2, 4, [F32, BF16])
    d_merged = matmul(d_mix, lp["w_out"], nm("out_x"), tb=True)
    g_w_out = matmul(sv["merged"], d_mix, nm("out_w"), ta=True)
    (d_ya, d_yb, d_yc, d_pgate), g_wb = scan_bwd(
        merge_fn, nm("merge"), WIDE_TILE, [sv["ya"], sv["yb"], sv["yc"], sv["p_gate"]], lp["wb"], [], [d_merged], 4, 3,
        [F32, F32, F32, BF16])
    (d_hs, d_plg), _ = scan_bwd(lru_out_fn, nm("lru_out"), TOK_TILE, [sv["hs"], sv["p_lg"]], [], [], [d_yc], 2, 0,
                                [F32, BF16])
    d_a, d_u = linscan_bwd(sv["a"], sv["hs"], d_hs, nm("lru_scan"))
    (d_plx,), g_lru = scan_bwd(lru_in_fn, nm("lru_in"), TOK_TILE, [sv["p_lx"]], lp["lru"], sv["sv_lru"],
                               [d_a, d_u], 1, 7, [BF16])
    (d_pssd,), g_ssd = scan_bwd(ssd_fn, nm("ssd"), CHUNK, [sv["p_ssd"]], lp["ssd"], sv["sv_ssd"], [d_yb], 1, 6,
                                [BF16])
    (d_pgdn,), g_gdn = scan_bwd(gdn_fn, nm("gdn"), CHUNK, [sv["p_gdn"]], lp["gdn"], sv["sv_gdn"], [d_ya], 1, 4,
                                [BF16])
    d_h = None
    g_w_in = []
    for tag, dp, w in zip(("gdn", "ssd", "lx", "lg", "gate"), (d_pgdn, d_pssd, d_plx, d_plg, d_pgate), lp["w_in"]):
        d_h = matmul(dp, w, nm("in_x_" + tag), tb=True, add=d_h)
        g_w_in.append(matmul(sv["h"], dp, nm("in_w_" + tag), ta=True))
    (d_x0,), (g_norm_mix, d_sc1, d_sh1) = scan_bwd(norm1_fn, nm("norm1"), TOK_TILE, [sv["x"]],
                                                   [lp["norm_mix"], sc1, sh1], [], [d_h, d_x], 1, 3)
    d_mod = jnp.concatenate([d_sh1, d_sc1, d_gt1, d_sh2, d_sc2, d_gt2], axis=1)
    flat = lambda v: v.reshape(-1)
    grads = dict(
        ada_b=flat(d_mod), norm_mix=flat(g_norm_mix),
        gdn_conv_w=g_gdn[0], gdn_a_log=g_gdn[1][0, 4:8], gdn_dt_bias=g_gdn[2][0, 4:8], gdn_norm=flat(g_gdn[3]),
        ssd_conv_w=g_ssd[0], ssd_conv_b=flat(g_ssd[1]), ssd_a_log=g_ssd[2][0, 0:8], ssd_dt_bias=g_ssd[3][0, 0:8],
        ssd_d=g_ssd[4].reshape(8, 64).sum(axis=1), ssd_norm=flat(g_ssd[5]),
        lru_conv_w=g_lru[0], lru_conv_b=flat(g_lru[1]), lru_w_a=_diag_blocks(g_lru[2]), lru_b_a=flat(g_lru[3]),
        lru_w_x=_diag_blocks(g_lru[4]), lru_b_x=flat(g_lru[5]), lru_lambda=flat(g_lru[6]),
        norm_mlp=flat(g_norm_mlp))
    big = [_join_w_in(*g_w_in), jnp.stack(g_wb), g_w_out, g_w_up, g_w_down]
    return d_x0, grads, big


def local_step(x, c, target, p, big):
    c8 = jnp.concatenate([c, jnp.zeros((7, c.shape[1]), F32)], axis=0)
    (silu_c,), _ = scan_fwd(silu_fn, "silu_c", 8, [c8], [], [], [(D_MODEL, F32)])
    lps, saved = [], []
    for l in range(DEPTH):
        lps.append(_layer_params(p, big[l](x), l))
        x, sv = _layer_fwd(x, silu_c, lps[l], l)
        saved.append(sv)
    loss, d_x, g_final = loss_head(x, target, p["final_norm"].reshape(1, -1), "loss_head")
    layer_grads, big_grads = [None] * DEPTH, [None] * DEPTH
    for l in reversed(range(DEPTH)):
        d_x, layer_grads[l], big_grads[l] = _layer_bwd(d_x, silu_c, lps[l], saved[l], l)
    grads = {k: jnp.stack([layer_grads[l][k] for l in range(DEPTH)]) for k in layer_grads[0]}
    grads["final_norm"] = g_final.reshape(-1)
    return loss, d_x, grads, big_grads, silu_c[0]


def _place_shard(shard, kind, full, chip):
    base = lax.empty(full, shard.dtype)
    if kind == "chip":
        return lax.dynamic_update_index_in_dim(base, shard, chip, axis=0)
    axis = 0 if kind == "row" else len(full) - 1
    return lax.dynamic_update_slice_in_dim(base, shard, chip * shard.shape[axis], axis=axis)


def _adam_nd(g, w, m, v, name):
    three = lambda a: a.reshape(-1, *a.shape[-2:])
    return tuple(r.reshape(w.shape) for r in adamw(three(g), three(w), three(m), three(v), name, copy_g=True))


def _reduce_big(g0, g1, place):
    sib = reduce_d2d(g0, g1, "reduce_pool")
    pooled = [add_cast(a, b, s, place, "reduce_pool_" + n) for (n, _, _), a, b, s in zip(REDUCE_LAYOUT, g0, g1, sib)]
    recv = reduce_ici([pb for _, pb in pooled], REDUCE_LAYOUT, "reduce_ici")
    finals = []
    for (n, kind, full), (pf, _), r in zip(REDUCE_LAYOUT, pooled, recv):
        local = r.shape[1:]
        own2 = pf.reshape(-1, full[-1])
        r3 = r.reshape(3, -1, local[-1])
        finals.append(sum4(own2, r3, kind == "col", place, "reduce_sum_" + n).reshape(2, *local))
    return share_d2d(finals, "reduce_share")


def _step(w, m, v, x, c, target):
    chip = 2 * lax.axis_index("x") + lax.axis_index("y")
    place = jnp.stack([chip, lax.axis_index("c")]).astype(jnp.int32)
    conv_shapes = [w[n].shape for n in CONV]
    small_shapes = [w[n].shape for n in SMALL]

    shards = [w[n].astype(BF16) for n, _, _ in BIG_LAYOUT]
    big0 = gather_layer([s[0] for s in shards], "gather_l0")
    own1 = [_place_shard(s[1], kind, full, chip) for s, (_, kind, full) in zip(shards, BIG_LAYOUT)]
    in_flight, token = gather_start([s[1] for s in shards], own1, big0[0], "gather_l1_start")
    c = c + token[0, 0]
    big = [lambda _: big0, lambda x_in: gather_wait(*in_flight, x_in, "gather_l1_wait")]
    conv_all = allgather8(_pack([w[n] for n in CONV], F32), "gather_conv").reshape(8, -1, PACK_COLS)
    conv_parts = [_unpack(conv_all[2 * k], conv_shapes) for k in range(4)]
    p = {n: w[n] for n in SMALL}
    for i, n in enumerate(CONV):
        p[n] = jnp.concatenate([conv_parts[k][i] for k in range(4)], axis=2)

    loss_blk, grad_x, g, big_g, silu_c = local_step(x[0], c, target[0], p, big)
    big_g = dict(zip((n for n, _, _ in REDUCE_LAYOUT), _reduce_big(big_g[0], big_g[1], place)))

    assert SMALL[0] == "ada_b"
    small_pack = _pack([g["ada_b"], silu_c, loss_blk[0, 0:1]] + [g[n] for n in SMALL[1:]] + [g[n] for n in CONV], F32)
    small_all = allgather8(small_pack, "gather_small").reshape(8, -1, PACK_COLS)
    total = _unpack(add8(small_all, "reduce_small"),
                    [small_shapes[0], (D_MODEL,), (1,)] + small_shapes[1:] + [g[n].shape for n in CONV])
    loss = total[2][0]
    small_g = dict(zip(SMALL, [total[0]] + total[3:2 + len(SMALL)]))
    conv_g = {n: lax.dynamic_slice_in_dim(t, chip * w[n].shape[2], w[n].shape[2], axis=2)
              for n, t in zip(CONV, total[2 + len(SMALL):])}

    cols = w["ada_w"].shape[2]
    silu_all = small_all[:, _pack_rows(small_shapes[0]), :]
    big_g["ada_w"] = jnp.stack([
        matmul(silu_all, lax.dynamic_slice_in_dim(small_all[:, N_MOD * l:N_MOD * (l + 1), :].reshape(8, -1),
                                                  chip * cols, cols, axis=1), f"ada_w_grad{l}", ta=True)
        for l in range(DEPTH)])

    grad, delta, new_m, new_v = {}, {}, {}, {}
    for n, _, _ in BIG_LAYOUT:
        delta[n], new_m[n], new_v[n], grad[n] = _adam_nd(big_g[n], w[n], m[n], v[n], "adam_" + n)
    for names, gs, shapes, tag in ((SMALL, small_g, small_shapes, "small"), (CONV, conv_g, conv_shapes, "conv")):
        pk = lambda d: _pack([d[n] for n in names], F32)[None]
        res = adamw(pk(gs), pk(w), pk(m), pk(v), "adam_" + tag)
        for out, r in zip((delta, new_m, new_v), res):
            out.update(zip(names, _unpack(r[0], shapes)))
        grad.update({n: gs[n] for n in names})
    outs = [loss, grad_x[None]]
    for d in (grad, delta, new_m, new_v):
        outs += [d[n] for n in WEIGHTS]
    return tuple(outs)


def kernel(x, c, ada_w, ada_b, norm_mix, w_in, gdn_conv_w, gdn_a_log, gdn_dt_bias, gdn_norm, ssd_conv_w, ssd_conv_b, ssd_a_log, ssd_dt_bias, ssd_d, ssd_norm, lru_conv_w, lru_conv_b, lru_w_a, lru_b_a, lru_w_x, lru_b_x, lru_lambda, w_branch, w_out, norm_mlp, w_up, w_down, final_norm, loss_target, m_ada_w, m_ada_b, m_norm_mix, m_w_in, m_gdn_conv_w, m_gdn_a_log, m_gdn_dt_bias, m_gdn_norm, m_ssd_conv_w, m_ssd_conv_b, m_ssd_a_log, m_ssd_dt_bias, m_ssd_d, m_ssd_norm, m_lru_conv_w, m_lru_conv_b, m_lru_w_a, m_lru_b_a, m_lru_w_x, m_lru_b_x, m_lru_lambda, m_w_branch, m_w_out, m_norm_mlp, m_w_up, m_w_down, m_final_norm, v_ada_w, v_ada_b, v_norm_mix, v_w_in, v_gdn_conv_w, v_gdn_a_log, v_gdn_dt_bias, v_gdn_norm, v_ssd_conv_w, v_ssd_conv_b, v_ssd_a_log, v_ssd_dt_bias, v_ssd_d, v_ssd_norm, v_lru_conv_w, v_lru_conv_b, v_lru_w_a, v_lru_b_a, v_lru_w_x, v_lru_b_x, v_lru_lambda, v_w_branch, v_w_out, v_norm_mlp, v_w_up, v_w_down, v_final_norm):
    given = dict(locals())
    w = {n: given[n] for n in WEIGHTS}
    m = {n: given["m_" + n] for n in WEIGHTS}
    v = {n: given["v_" + n] for n in WEIGHTS}
    return _step(w, m, v, x, c, loss_target)
```

```python
import functools
import math

import jax
import jax.numpy as jnp
from jax import lax
from jax.experimental import pallas as pl
from jax.experimental.pallas import tpu as pltpu

F32 = jnp.float32
BF16 = jnp.bfloat16

D_MODEL = 1024
DEPTH = 2
RMS_EPS = 1e-6
CHUNK = 128
GDN_HEADS = 4
SSD_HEADS = 8
LRU_C = 8.0
D_FF = 4096
N_MOD = 6
W_GDN = 2176
W_SSD = 1664
W_LRU = 512
W_GATE = 3072
ADAM_LR = 0.001
ADAM_B1 = 0.9
ADAM_B2 = 0.999
ADAM_EPS = 1e-08
ADAM_WD = 0.01
ADAM_STEP = 10
VMEM_LIMIT = 56 * 1024 * 1024
MESH = pl.DeviceIdType.MESH


def _dot(a, b, ta, tb):
    dn = (((0 if ta else 1,), (1 if tb else 0,)), ((), ()))
    return lax.dot_general(a.astype(BF16), b.astype(BF16), dn, preferred_element_type=F32)


@functools.partial(jax.custom_vjp, nondiff_argnums=(2, 3))
def mm(a, b, ta, tb):
    return _dot(a, b, ta, tb)


def _mm_fwd(a, b, ta, tb):
    return _dot(a, b, ta, tb), (a, b)


def _mm_bwd(ta, tb, res, g):
    a, b = res
    if not ta and not tb:
        return mm(g, b, False, True), mm(a, g, True, False)
    if not ta and tb:
        return mm(g, b, False, False), mm(g, a, True, False)
    assert ta and not tb
    return mm(b, g, False, True), mm(a, g, False, False)


mm.defvjp(_mm_fwd, _mm_bwd)


def _tri_apply(x, upper):
    t = x.shape[0]
    r = lax.broadcasted_iota(jnp.int32, (t, t), 0)
    c = lax.broadcasted_iota(jnp.int32, (t, t), 1)
    tri = jnp.where((r <= c) if upper else (r >= c), 1.0, 0.0).astype(BF16)
    x1 = x.astype(BF16)
    r1 = x - x1.astype(F32)
    x2 = r1.astype(BF16)
    x3 = (r1 - x2.astype(F32)).astype(BF16)
    d = lambda p: jnp.dot(tri, p, preferred_element_type=F32)
    return (d(x1) + d(x2)) + d(x3)


@jax.custom_vjp
def cumsum_rows(x):
    return _tri_apply(x, False)


cumsum_rows.defvjp(lambda x: (_tri_apply(x, False), None), lambda _, g: (_tri_apply(g, True),))


def _dot_split(a, b):
    a1, b1 = a.astype(BF16), b.astype(BF16)
    a2, b2 = (a - a1.astype(F32)).astype(BF16), (b - b1.astype(F32)).astype(BF16)
    d = lambda p, q: jnp.dot(p, q, preferred_element_type=F32)
    return d(a1, b1) + (d(a1, b2) + d(a2, b1))


def _neumann(ms):
    t = ms[0].shape[0]
    xs = [-m for m in ms]
    qs = [_dot(m, m, False, False) for m in ms]
    n = 2
    while True:
        xs = [x + q + _dot(x, q, False, False) for x, q in zip(xs, qs)]
        n *= 2
        if n >= t:
            break
        qs = [_dot(q, q, False, False) for q in qs]
    rs = [-(x + m + _dot_split(m, x)) for x, m in zip(xs, ms)]
    return [x + r + _dot(x, r, False, False) for x, r in zip(xs, rs)]


@jax.custom_vjp
def tri_solve(ms, rhss):
    return tuple(rhs + _dot(x, rhs, False, False) for x, rhs in zip(_neumann(ms), rhss))


def _tri_solve_fwd(ms, rhss):
    xs = _neumann(ms)
    sols = tuple(rhs + _dot(x, rhs, False, False) for x, rhs in zip(xs, rhss))
    return sols, (tuple(xs), sols)


def _tri_solve_bwd(res, gs):
    xs, sols = res
    d_rhss = tuple(g + _dot(x, g, True, False) for x, g in zip(xs, gs))
    return tuple(-_dot(d, sol, False, True) for d, sol in zip(d_rhss, sols)), d_rhss


tri_solve.defvjp(_tri_solve_fwd, _tri_solve_bwd)


@functools.partial(jax.custom_vjp, nondiff_argnums=(1,))
def split_cols(x, sizes):
    out, o = [], 0
    for s in sizes:
        out.append(x[:, o:o + s])
        o += s
    return tuple(out)


split_cols.defvjp(lambda x, sizes: (split_cols(x, sizes), None),
                  lambda sizes, _, g: (jnp.concatenate(list(g), axis=1),))


@functools.partial(jax.custom_vjp, nondiff_argnums=(1,))
def _last_rows(x, t):
    return x[t - 8:, :]


_last_rows.defvjp(lambda x, t: (_last_rows(x, t), None),
                  lambda t, _, g: (jnp.concatenate([jnp.zeros((t - 8, g.shape[1]), g.dtype), g], axis=0),))


def last8(x):
    return _last_rows(x, x.shape[0])


def _shifted(xp, d, t):
    return (pltpu.roll(xp, d, 0) if d else xp)[8:8 + t, :]


@jax.custom_vjp
def conv4(x, tail, w):
    t = x.shape[0]
    xp = jnp.concatenate([tail, x], axis=0)
    return sum(_shifted(xp, 3 - k, t) * w[k:k + 1, :] for k in range(4))


def _conv4_fwd(x, tail, w):
    return conv4(x, tail, w), (x, tail, w)


def _conv4_bwd(res, g):
    x, tail, w = res
    t = x.shape[0]
    xp = jnp.concatenate([tail, x], axis=0)
    zero8 = jnp.zeros((8, g.shape[1]), g.dtype)
    d_xp = jnp.zeros_like(xp)
    d_w = []
    for k in range(4):
        gk = jnp.concatenate([zero8, g * w[k:k + 1, :]], axis=0)
        d_xp = d_xp + (pltpu.roll(gk, t + 8 - (3 - k), 0) if k < 3 else gk)
        d_w.append(jnp.sum(g * _shifted(xp, 3 - k, t), axis=0, keepdims=True))
    return d_xp[8:, :], d_xp[:8, :], jnp.concatenate(d_w, axis=0)


conv4.defvjp(_conv4_fwd, _conv4_bwd)


def _sigmoid(x):
    return 0.5 * (jnp.tanh(0.5 * x) + 1.0)


def _silu(x):
    return x * _sigmoid(x)


def _softplus(x):
    ax = jnp.where(x > 0, x, -x)
    return jnp.where(x > 0, x, 0.0) + jnp.log(1.0 + jnp.exp(-ax))


def _gelu(x):
    return 0.5 * x * (1.0 + jnp.tanh(math.sqrt(2.0 / math.pi) * (x + 0.044715 * (x * x * x))))


def _expm1(x):
    series = x * (1.0 + x * (0.5 + x * (1.0 / 6.0 + x * (1.0 / 24.0))))
    return jnp.where(jnp.abs(x) < 0.03, series, jnp.exp(x) - 1.0)


def _rms(x, w):
    return x * lax.rsqrt(jnp.mean(x * x, axis=-1, keepdims=True) + RMS_EPS) * w


def _lane_pick(x, j):
    lane = lax.broadcasted_iota(jnp.int32, (1, x.shape[1]), 1)
    return jnp.sum(jnp.where(lane == j, x, 0.0), axis=1, keepdims=True)


def _row_pick(x, j):
    row = lax.broadcasted_iota(jnp.int32, (x.shape[0], 1), 0)
    return jnp.sum(jnp.where(row == j, x, 0.0), axis=0, keepdims=True)


def gdn_fn(carry, seq, params):
    *states, tail = carry
    (tile,) = seq
    conv_w, alog_row, dtb_row, norm_w = params
    t = tile.shape[0]
    qkv_raw, z, sm = split_cols(tile, (1536, 512, 128))
    qkv = _silu(conv4(qkv_raw, tail, conv_w))
    parts = split_cols(qkv, (128,) * 12)
    zs = split_cols(z, (128,) * 4)
    lane = lax.broadcasted_iota(jnp.int32, (1, 128), 1)
    beta_all = _sigmoid(sm)
    g_all = jnp.where((lane >= 4) & (lane < 8), -jnp.exp(alog_row) * _softplus(sm + dtb_row), 0.0)
    gc_all = cumsum_rows(g_all)
    gr_all = gc_all.T
    gl_all = _row_pick(gc_all, t - 1)
    r = lax.broadcasted_iota(jnp.int32, (t, t), 0)
    c = lax.broadcasted_iota(jnp.int32, (t, t), 1)
    heads = range(GDN_HEADS)
    l2 = lambda a: a * lax.rsqrt(jnp.sum(a * a, axis=-1, keepdims=True) + RMS_EPS)
    qn = [l2(parts[h]) * (128.0 ** -0.5) for h in heads]
    kn = [l2(parts[4 + h]) for h in heads]
    beta = [_lane_pick(beta_all, h) for h in heads]
    gc = [_lane_pick(gc_all, 4 + h) for h in heads]
    gl = [_lane_pick(gl_all, 4 + h) for h in heads]
    decay = [jnp.exp(jnp.where(r >= c, gc[h] - _row_pick(gr_all, 4 + h), -1e30)) for h in heads]
    kk = [mm(kn[h], kn[h], False, True) for h in heads]
    qk = [mm(qn[h], kn[h], False, True) for h in heads]
    m = tuple(jnp.where(r > c, beta[h] * kk[h] * decay[h], 0.0) for h in heads)
    eg = [jnp.exp(gc[h]) for h in heads]
    rhs = tuple(jnp.concatenate([beta[h] * parts[8 + h], (beta[h] * eg[h]) * kn[h]], axis=1) for h in heads)
    uw = [split_cols(s, (128, 128)) for s in tri_solve(m, rhs)]
    ws = [mm(uw[h][1], states[h], False, False) for h in heads]
    qs = [mm(qn[h] * eg[h], states[h], False, False) for h in heads]
    v_new = [uw[h][0] - ws[h] for h in heads]
    o = [qs[h] + mm(qk[h] * decay[h], v_new[h], False, False) for h in heads]
    kv = [mm(kn[h] * jnp.exp(gl[h] - gc[h]), v_new[h], True, False) for h in heads]
    new_states = [states[h] * jnp.exp(gl[h]) + kv[h] for h in heads]
    outs = [_rms(o[h], norm_w) * _silu(zs[h]) for h in heads]
    return (*new_states, last8(qkv_raw)), (jnp.concatenate(outs, axis=1),)


def ssd_fn(carry, seq, params):
    *states, tail = carry
    (tile,) = seq
    conv_w, conv_b, alog_row, dtb_row, d_row, norm_w = params
    t = tile.shape[0]
    xbc_raw, z, sm = split_cols(tile, (1024, 512, 128))
    xbc = _silu(conv4(xbc_raw, tail, conv_w) + conv_b)
    x0, x1, x2, x3, b0, b1, c0, c1 = split_cols(xbc, (128,) * 8)
    xs, bs, cs = (x0, x1, x2, x3), (b0, b1), (c0, c1)
    ds = split_cols(d_row, (128,) * 4)
    lane = lax.broadcasted_iota(jnp.int32, (1, 128), 1)
    sub = lax.broadcasted_iota(jnp.int32, (128, 1), 0)
    low = lane < 64
    dt_all = jnp.where(lane < SSD_HEADS, _softplus(sm + dtb_row), 0.0)
    ac_all = cumsum_rows(dt_all * (-jnp.exp(alog_row)))
    ar_all = ac_all.T
    al_all = _row_pick(ac_all, t - 1)
    r = lax.broadcasted_iota(jnp.int32, (t, t), 0)
    c = lax.broadcasted_iota(jnp.int32, (t, t), 1)
    pairs, heads = range(4), range(SSD_HEADS)
    col = [_lane_pick(ac_all, h) for h in heads]
    last = [_lane_pick(al_all, h) for h in heads]
    dt = [_lane_pick(dt_all, h) for h in heads]
    lm = [jnp.exp(jnp.where(r >= c, col[h] - _row_pick(ar_all, h), -1e30)) for h in heads]
    cb = [mm(cs[g], bs[g], False, True) for g in range(2)]
    both = lambda a, b: jnp.where(low, a, b)
    xdt = [xs[p] * both(dt[2 * p], dt[2 * p + 1]) for p in pairs]
    y_off = [mm(cs[p // 2], states[p], False, True) for p in pairs]
    y_lo = [mm(cb[p // 2] * lm[2 * p], jnp.where(low, xdt[p], 0.0), False, False) for p in pairs]
    y_hi = [mm(cb[p // 2] * lm[2 * p + 1], jnp.where(low, 0.0, xdt[p]), False, False) for p in pairs]
    st = [mm(xdt[p] * both(jnp.exp(last[2 * p] - col[2 * p]), jnp.exp(last[2 * p + 1] - col[2 * p + 1])),
             bs[p // 2], True, False) for p in pairs]
    ys = [ds[p] * xs[p] + y_lo[p] + y_hi[p] + y_off[p] * both(jnp.exp(col[2 * p]), jnp.exp(col[2 * p + 1]))
          for p in pairs]
    new_states = [states[p] * jnp.where(sub < 64, jnp.exp(last[2 * p]), jnp.exp(last[2 * p + 1])) + st[p]
                  for p in pairs]
    gz = jnp.concatenate(ys, axis=1) * _silu(z)
    g0, g1 = split_cols(gz, (256, 256))
    n0, n1 = split_cols(norm_w, (256, 256))
    out = jnp.concatenate([_rms(g0, n0), _rms(g1, n1)], axis=1)
    return (*new_states, last8(xbc_raw)), (out,)


def lru_in_fn(carry, seq, params):
    (tail,) = carry
    (x,) = seq
    conv_w, conv_b, w_a, b_a, w_x, b_x, lam = params
    xc = conv4(x, tail, conv_w) + conv_b
    r = _sigmoid(mm(xc, w_a, False, False) + b_a)
    i = _sigmoid(mm(xc, w_x, False, False) + b_x)
    log_a = -LRU_C * r * _softplus(-lam)
    u = jnp.sqrt(-_expm1(2.0 * log_a)) * (i * xc)
    return (last8(x),), (jnp.exp(log_a), u)


def lru_out_fn(carry, seq, params):
    hs, gate = seq
    return (), (hs * _gelu(gate),)


def merge_fn(carry, seq, params):
    ya, yb, yc, gl = seq
    g = split_cols(_sigmoid(gl), (D_MODEL,) * 3)
    merged = sum(g[r] * mm(y, params[r], False, False) for r, y in enumerate((ya, yb, yc)))
    return (), (merged,)


def _adaln(x, w, sc, sh):
    return _rms(x, w) * (1.0 + sc) + sh


def norm1_fn(carry, seq, params):
    (x,) = seq
    return (), (_adaln(x, *params), x)


def resid_norm_fn(carry, seq, params):
    x, mix = seq
    gt, w, sc, sh = params
    x1 = x + gt * mix
    return (), (_adaln(x1, w, sc, sh), x1)


def resid_fn(carry, seq, params):
    x, dn = seq
    (gt,) = params
    return (), (x + gt * dn,)


def silu_fn(carry, seq, params):
    return (), (_silu(seq[0]),)


def _full_spec(a):
    nd = a.ndim
    return pl.BlockSpec(a.shape, lambda i: (0,) * nd)


def _cparams(*sem):
    return pltpu.CompilerParams(dimension_semantics=sem, vmem_limit_bytes=VMEM_LIMIT)


def scan_fwd(fn, name, tile, seqs, params, carry_shapes, outs, save_carry=False):
    rows = seqs[0].shape[0]
    tile = min(tile, rows)
    n = rows // tile
    ns, npar, nc, no = len(seqs), len(params), len(carry_shapes), len(outs)

    def body(*refs):
        seq_refs, refs = refs[:ns], refs[ns:]
        par_refs, refs = refs[:npar], refs[npar:]
        out_refs, refs = refs[:no], refs[no:]
        save_refs, refs = (refs[:nc], refs[nc:]) if save_carry else ((), refs)
        carry_refs = refs

        @pl.when(pl.program_id(0) == 0)
        def _():
            for cr in carry_refs:
                cr[...] = jnp.zeros_like(cr)

        carry = tuple(cr[...] for cr in carry_refs)
        for sr, cv in zip(save_refs, carry):
            sr[0] = cv
        new_carry, res = fn(carry, tuple(r[...].astype(F32) for r in seq_refs),
                            tuple(r[...].astype(F32) for r in par_refs))
        for r, v in zip(out_refs, res):
            r[...] = v.astype(r.dtype)
        for cr, v in zip(carry_refs, new_carry):
            cr[...] = v

    out_shape = [jax.ShapeDtypeStruct((rows, w), dt) for w, dt in outs]
    out_specs = [pl.BlockSpec((tile, w), lambda i: (i, 0)) for w, _ in outs]
    if save_carry:
        out_shape += [jax.ShapeDtypeStruct((n, *s), F32) for s in carry_shapes]
        out_specs += [pl.BlockSpec((1, *s), lambda i: (i, 0, 0)) for s in carry_shapes]
    res = pl.pallas_call(
        body, name=name, grid=(n,),
        in_specs=[pl.BlockSpec((tile, s.shape[1]), lambda i: (i, 0)) for s in seqs] + [_full_spec(p) for p in params],
        out_specs=out_specs, out_shape=out_shape,
        scratch_shapes=[pltpu.VMEM(s, F32) for s in carry_shapes],
        compiler_params=_cparams("arbitrary"),
    )(*seqs, *params)
    return res[:no], res[no:]


def scan_bwd(fn, name, tile, seqs, params, saved, douts, n_dseq, n_dpar, dseq_dtypes=None):
    dseq_dtypes = dseq_dtypes or [F32] * n_dseq
    rows = seqs[0].shape[0]
    tile = min(tile, rows)
    n = rows // tile
    ns, npar, nc, no = len(seqs), len(params), len(saved), len(douts)

    def body(*refs):
        seq_refs, refs = refs[:ns], refs[ns:]
        par_refs, refs = refs[:npar], refs[npar:]
        save_refs, refs = refs[:nc], refs[nc:]
        dout_refs, refs = refs[:no], refs[no:]
        dseq_refs, refs = refs[:n_dseq], refs[n_dseq:]
        dpar_refs, refs = refs[:n_dpar], refs[n_dpar:]
        dcarry_refs = refs

        @pl.when(pl.program_id(0) == 0)
        def _():
            for r in (*dpar_refs, *dcarry_refs):
                r[...] = jnp.zeros_like(r)

        carry = tuple(r[0] for r in save_refs)
        seq = tuple(r[...].astype(F32) for r in seq_refs)
        par = tuple(r[...].astype(F32) for r in par_refs)

        def f(carry, dseq, dpar):
            return fn(carry, (*dseq, *seq[n_dseq:]), (*dpar, *par[n_dpar:]))

        _, vjp = jax.vjp(f, carry, seq[:n_dseq], par[:n_dpar])
        d_carry, d_seq, d_par = vjp((tuple(r[...] for r in dcarry_refs),
                                     tuple(r[...].astype(F32) for r in dout_refs)))
        for r, v in zip(dseq_refs, d_seq):
            r[...] = v.astype(r.dtype)
        for r, v in zip(dpar_refs, d_par):
            r[...] += v
        for r, v in zip(dcarry_refs, d_carry):
            r[...] = v

    rev = lambda i: (n - 1 - i, 0)
    res = pl.pallas_call(
        body, name=name, grid=(n,),
        in_specs=([pl.BlockSpec((tile, s.shape[1]), rev) for s in seqs] + [_full_spec(p) for p in params]
                  + [pl.BlockSpec((1, *s.shape[1:]), lambda i: (n - 1 - i, 0, 0)) for s in saved]
                  + [pl.BlockSpec((tile, d.shape[1]), rev) for d in douts]),
        out_specs=([pl.BlockSpec((tile, s.shape[1]), rev) for s in seqs[:n_dseq]]
                   + [_full_spec(p) for p in params[:n_dpar]]),
        out_shape=([jax.ShapeDtypeStruct((rows, s.shape[1]), dt) for s, dt in zip(seqs[:n_dseq], dseq_dtypes)]
                   + [jax.ShapeDtypeStruct(p.shape, F32) for p in params[:n_dpar]]),
        scratch_shapes=[pltpu.VMEM(s.shape[1:], F32) for s in saved],
        compiler_params=_cparams("arbitrary"),
    )(*seqs, *params, *saved, *douts)
    return res[:n_dseq], res[n_dseq:]


def _tile_of(dim, pref):
    if dim <= pref:
        return dim
    best = max((t for t in range(128, pref + 1, 128) if dim % t == 0), default=None)
    if best is None or (best < 512 and dim <= 2304):
        return dim
    return best


def _row_tile(rows, pref):
    if rows <= pref:
        return rows
    return max(t for t in range(8, pref + 1, 8) if rows % t == 0)


def matmul(a, b, name, ta=False, tb=False, out_dtype=F32, add=None, bias=None, relu2=False, relu2_of=None,
           tm=1024, tn=2048, tk=1024):
    m, k = (a.shape[1], a.shape[0]) if ta else a.shape
    n = b.shape[0] if tb else b.shape[1]
    assert k == (b.shape[1] if tb else b.shape[0])
    tm, tn, tk = _tile_of(m, tm), _tile_of(n, tn), _tile_of(k, tk)
    nm, nn, nk = m // tm, n // tn, k // tk
    assert nk == 1 or (out_dtype == F32 and not relu2 and relu2_of is None)
    dn = (((0 if ta else 1,), (1 if tb else 0,)), ((), ()))
    has_add, has_bias, has_u = add is not None, bias is not None, relu2_of is not None
    n_inner = a.size * a.dtype.itemsize * (nn - 1) >= b.size * b.dtype.itemsize * (nm - 1)
    ij = (lambda g0, g1: (g0, g1)) if n_inner else (lambda g0, g1: (g1, g0))

    def body(*refs):
        a_ref, b_ref, refs = refs[0], refs[1], refs[2:]
        add_ref, refs = (refs[0], refs[1:]) if has_add else (None, refs)
        bias_ref, refs = (refs[0], refs[1:]) if has_bias else (None, refs)
        u_ref, refs = (refs[0], refs[1:]) if has_u else (None, refs)
        o_ref = refs[0]
        r = lax.dot_general(a_ref[...].astype(BF16), b_ref[...].astype(BF16), dn, preferred_element_type=F32)

        def first():
            v = r
            if has_add:
                v = v + add_ref[...]
            if has_bias:
                v = v + bias_ref[...]
            if has_u:
                v = v * (2.0 * jnp.maximum(u_ref[...], 0.0))
            o_ref[...] = v.astype(o_ref.dtype)
            if relu2:
                p = jnp.maximum(v, 0.0)
                refs[1][...] = (p * p).astype(BF16)

        if nk == 1:
            first()
        else:
            pl.when(pl.program_id(2) == 0)(first)

            @pl.when(pl.program_id(2) > 0)
            def _():
                o_ref[...] += r

    def spec(shape, fn):
        return pl.BlockSpec(shape, lambda g0, g1, l: fn(*ij(g0, g1), l))

    a_spec = spec((tk, tm), lambda i, j, l: (l, i)) if ta else spec((tm, tk), lambda i, j, l: (i, l))
    b_spec = spec((tn, tk), lambda i, j, l: (j, l)) if tb else spec((tk, tn), lambda i, j, l: (l, j))
    o_spec = spec((tm, tn), lambda i, j, l: (i, j))
    in_specs, args = [a_spec, b_spec], [a, b]
    if has_add:
        in_specs.append(o_spec)
        args.append(add)
    if has_bias:
        in_specs.append(spec((1, tn), lambda i, j, l: (0, j)))
        args.append(bias)
    if has_u:
        in_specs.append(o_spec)
        args.append(relu2_of)
    out_shape = [jax.ShapeDtypeStruct((m, n), out_dtype)] + ([jax.ShapeDtypeStruct((m, n), BF16)] if relu2 else [])
    res = pl.pallas_call(
        body, name=name, grid=(nm, nn, nk) if n_inner else (nn, nm, nk), in_specs=in_specs,
        out_specs=[o_spec] * len(out_shape), out_shape=out_shape,
        compiler_params=_cparams("parallel", "parallel", "arbitrary"),
    )(*args)
    return res if relu2 else res[0]


LIN_TILE = 512


def linscan_fwd(a, u, name):
    rows, w = a.shape
    tile = min(LIN_TILE, rows)

    def body(a_ref, u_ref, h_ref, hc):
        @pl.when(pl.program_id(0) == 0)
        def _():
            hc[...] = jnp.zeros_like(hc)

        row = lax.broadcasted_iota(jnp.int32, (8, 1), 0)

        def group(k, h_in):
            rows8 = pl.ds(pl.multiple_of(k * 8, 8), 8)
            pa, pu = a_ref[rows8, :], u_ref[rows8, :]
            for d in (1, 2, 4):
                pu = pu + pa * jnp.where(row >= d, pltpu.roll(pu, d, 0), 0.0)
                pa = pa * jnp.where(row >= d, pltpu.roll(pa, d, 0), 1.0)
            h_ref[rows8, :] = pa * h_in + pu
            return h_ref[pl.ds(k * 8 + 7, 1), :]

        hc[...] = lax.fori_loop(0, tile // 8, group, hc[...], unroll=4)

    spec = pl.BlockSpec((tile, w), lambda i: (i, 0))
    return pl.pallas_call(
        body, name=name, grid=(rows // tile,), in_specs=[spec, spec], out_specs=spec,
        out_shape=jax.ShapeDtypeStruct((rows, w), F32), scratch_shapes=[pltpu.VMEM((1, w), F32)],
        compiler_params=_cparams("arbitrary"),
    )(a, u)


def linscan_bwd(a, hs, dh, name):
    rows, w = a.shape
    tile = min(LIN_TILE, rows)
    n = rows // tile
    per = tile // 8

    def body(a_ref, h_ref, hprev_ref, dh_ref, da_ref, du_ref, cc):
        i = pl.program_id(0)

        @pl.when(i == 0)
        def _():
            cc[...] = jnp.zeros_like(cc)

        row = lax.broadcasted_iota(jnp.int32, (8, 1), 0)
        h_before = jnp.where(i == n - 1, 0.0, hprev_ref[7:8, :])

        def group(s, c_in):
            k = per - 1 - s
            rows8 = pl.ds(pl.multiple_of(k * 8, 8), 8)
            av, hv = a_ref[rows8, :], h_ref[rows8, :]
            pb = jnp.where(row < 7, pltpu.roll(av, 7, 0), 1.0)
            pg = dh_ref[rows8, :]
            for d in (1, 2, 4):
                pg = pg + pb * jnp.where(row < 8 - d, pltpu.roll(pg, 8 - d, 0), 0.0)
                pb = pb * jnp.where(row < 8 - d, pltpu.roll(pb, 8 - d, 0), 1.0)
            g = pg + pb * c_in
            du_ref[rows8, :] = g
            h_prev = jnp.where(k == 0, h_before, h_ref[pl.ds(jnp.maximum(k * 8 - 1, 0), 1), :])
            da_ref[rows8, :] = g * jnp.where(row >= 1, pltpu.roll(hv, 1, 0), h_prev)
            return a_ref[pl.ds(k * 8, 1), :] * du_ref[pl.ds(k * 8, 1), :]

        cc[...] = lax.fori_loop(0, per, group, cc[...], unroll=4)

    rev = pl.BlockSpec((tile, w), lambda i: (n - 1 - i, 0))
    prev = pl.BlockSpec((8, w), lambda i: (jnp.maximum((n - 1 - i) * per - 1, 0), 0))
    return pl.pallas_call(
        body, name=name, grid=(n,), in_specs=[rev, rev, prev, rev], out_specs=[rev, rev],
        out_shape=[jax.ShapeDtypeStruct((rows, w), F32)] * 2, scratch_shapes=[pltpu.VMEM((1, w), F32)],
        compiler_params=_cparams("arbitrary"),
    )(a, hs, hs, dh)


def loss_head(x, target, w, name):
    rows, d = x.shape
    tile = min(512, rows)

    def body(x_ref, t_ref, w_ref, loss_ref, dx_ref, dw_ref):
        @pl.when(pl.program_id(0) == 0)
        def _():
            loss_ref[...] = jnp.zeros_like(loss_ref)
            dw_ref[...] = jnp.zeros_like(dw_ref)

        tv = t_ref[...]

        def f(xv, wv):
            e = _rms(xv, wv) - tv
            return 0.5 * jnp.sum(jnp.mean(e * e, axis=-1, keepdims=True), axis=0, keepdims=True)

        val, vjp = jax.vjp(f, x_ref[...], w_ref[...])
        dxv, dwv = vjp(jnp.ones((1, 1), F32))
        loss_ref[...] += jnp.broadcast_to(val, loss_ref.shape)
        dx_ref[...] = dxv
        dw_ref[...] += dwv

    spec = pl.BlockSpec((tile, d), lambda i: (i, 0))
    return pl.pallas_call(
        body, name=name, grid=(rows // tile,), in_specs=[spec, spec, _full_spec(w)],
        out_specs=[pl.BlockSpec((8, 128), lambda i: (0, 0)), spec, _full_spec(w)],
        out_shape=[jax.ShapeDtypeStruct((8, 128), F32), jax.ShapeDtypeStruct((rows, d), F32),
                   jax.ShapeDtypeStruct(w.shape, F32)],
        compiler_params=_cparams("arbitrary"),
    )(x, target, w)


def adamw(g, w, m, v, name, copy_g=False):
    layers, rows, cols = g.shape
    tile = _row_tile(rows, 256)
    n_out = 4 if copy_g else 3

    def body(g_ref, w_ref, m_ref, v_ref, d_ref, nm_ref, nv_ref, *g_out):
        gv = g_ref[...]
        if copy_g:
            g_out[0][...] = gv
        nm = ADAM_B1 * m_ref[...] + (1.0 - ADAM_B1) * gv
        nv = ADAM_B2 * v_ref[...] + (1.0 - ADAM_B2) * (gv * gv)
        m_hat = nm / (1.0 - ADAM_B1 ** ADAM_STEP)
        v_hat = nv / (1.0 - ADAM_B2 ** ADAM_STEP)
        d_ref[...] = -ADAM_LR * (m_hat / (jnp.sqrt(v_hat) + ADAM_EPS) + ADAM_WD * w_ref[...])
        nm_ref[...] = nm
        nv_ref[...] = nv

    spec = pl.BlockSpec((None, tile, cols), lambda l, i: (l, i, 0))
    return pl.pallas_call(
        body, name=name, grid=(layers, rows // tile), in_specs=[spec] * 4, out_specs=[spec] * n_out,
        out_shape=[jax.ShapeDtypeStruct((layers, rows, cols), F32)] * n_out,
        compiler_params=_cparams("parallel", "parallel"),
    )(g, w, m, v)


def add_cast(g0, g1, sib, place, name):
    shape = sib.shape
    g0, g1, sib = (a.reshape(-1, shape[-1]) for a in (g0, g1, sib))
    rows, cols = sib.shape
    tile = _row_tile(rows, max(8, min(256, (512 * 1024) // cols)))

    def body(k_ref, g0_ref, g1_ref, s_ref, o_ref, ob_ref):
        s = jnp.where(k_ref[1] == 0, g0_ref[...], g1_ref[...]) + s_ref[...]
        o_ref[...] = s
        ob_ref[...] = s.astype(BF16)

    spec = pl.BlockSpec((tile, cols), lambda i, k: (i, 0))
    s, sb = pl.pallas_call(
        body, name=name,
        grid_spec=pltpu.PrefetchScalarGridSpec(
            num_scalar_prefetch=1, grid=(rows // tile,),
            in_specs=[pl.BlockSpec((tile, cols), lambda i, k: (i * (1 - k[1]), 0)),
                      pl.BlockSpec((tile, cols), lambda i, k: (i * k[1], 0)), spec],
            out_specs=[spec, spec]),
        out_shape=[jax.ShapeDtypeStruct((rows, cols), F32), jax.ShapeDtypeStruct((rows, cols), BF16)],
        compiler_params=_cparams("arbitrary"),
    )(place, g0, g1, sib)
    return s.reshape(shape), sb.reshape(shape)


def sum4(own, recv, by_cols, place, name):
    _, r, c = recv.shape
    tile = _row_tile(r, 256)
    nt = r // tile
    own_map = (lambda i, k: (i, k[0])) if by_cols else (lambda i, k: (k[0] * nt + i, 0))

    def body(k_ref, own_ref, recv_ref, o_ref):
        o_ref[...] = ((own_ref[...] + recv_ref[0].astype(F32)) + recv_ref[1].astype(F32)) + recv_ref[2].astype(F32)

    return pl.pallas_call(
        body, name=name,
        grid_spec=pltpu.PrefetchScalarGridSpec(
            num_scalar_prefetch=1, grid=(nt,),
            in_specs=[pl.BlockSpec((tile, c), own_map), pl.BlockSpec((3, tile, c), lambda i, k: (0, i, 0))],
            out_specs=pl.BlockSpec((None, tile, c), lambda i, k: (k[1], i, 0))),
        out_shape=jax.ShapeDtypeStruct((2, r, c), F32),
        compiler_params=_cparams("arbitrary"),
    )(place, own, recv)


def add8(parts, name):
    _, rows, cols = parts.shape

    def body(p_ref, o_ref):
        acc = p_ref[0]
        for k in range(1, 8):
            acc = acc + p_ref[k]
        o_ref[...] = acc

    return pl.pallas_call(
        body, name=name, in_specs=[pl.BlockSpec(memory_space=pltpu.VMEM)],
        out_specs=pl.BlockSpec(memory_space=pltpu.VMEM),
        out_shape=jax.ShapeDtypeStruct((rows, cols), F32),
        compiler_params=pltpu.CompilerParams(vmem_limit_bytes=VMEM_LIMIT),
    )(parts)


def _place():
    return lax.axis_index("x"), lax.axis_index("y"), lax.axis_index("c")


def _other_chips(x, y):
    return [(1 - x, y), (x, 1 - y), (1 - x, 1 - y)]


_ANY = pl.BlockSpec(memory_space=pl.ANY)


BIG_LAYOUT = (("ada_w", "col", (1024, 6144)), ("w_in", "chip", (4, 1024, 1924)), ("w_branch", "col", (3, 512, 1024)),
              ("w_out", "row", (1024, 1024)), ("w_up", "col", (1024, 4096)), ("w_down", "row", (4096, 1024)))
N_BIG = len(BIG_LAYOUT)
REDUCE_LAYOUT = BIG_LAYOUT[1:]


def _local_shape(kind, full):
    if kind == "col":
        return (*full[:-1], full[-1] // 4)
    if kind == "row":
        return (full[0] // 4, *full[1:])
    return full[1:]


def _window(ref, kind, k, local):
    if kind == "chip":
        return ref.at[k]
    if kind == "row":
        return ref.at[pl.ds(pl.multiple_of(k * local[0], 8), local[0])]
    idx = (slice(None),) * (len(local) - 1) + (pl.ds(pl.multiple_of(k * local[-1], 128), local[-1]),)
    return ref.at[idx]


def _dma_call(body, name, n_in, out_shape, sems, aliases=None):
    return pl.pallas_call(
        body, name=name, in_specs=[_ANY] * n_in, out_specs=[_ANY] * len(out_shape), out_shape=out_shape,
        scratch_shapes=[pltpu.SemaphoreType.DMA((n,)) for n in sems],
        input_output_aliases=aliases or {},
        compiler_params=pltpu.CompilerParams(has_side_effects=True))


def _remote(src, dst, send_sem, recv_sem, to):
    return pltpu.make_async_remote_copy(src_ref=src, dst_ref=dst, send_sem=send_sem, recv_sem=recv_sem,
                                        device_id=to, device_id_type=MESH)


CORE_PARAMS = ((1, 4), (0, 2, 3, 5))


def gather_layer(shards, name):
    locals_ = [_local_shape(kind, full) for _, kind, full in BIG_LAYOUT]

    def body(*refs):
        sh, full, (send_sems, recv_sems, local_sems, pass_send, pass_recv) = (
            refs[:N_BIG], refs[N_BIG:2 * N_BIG], refs[2 * N_BIG:])
        x, y, c = _place()
        me = 2 * x + y
        chips = _other_chips(x, y)
        win = lambda n, k: _window(full[n], BIG_LAYOUT[n][1], k, locals_[n])
        for cc in (0, 1):
            @pl.when(c == cc)
            def _():
                mine, sends = {}, []
                for n in CORE_PARAMS[cc]:
                    mine[n] = pltpu.make_async_copy(sh[n], win(n, me), local_sems.at[n])
                    mine[n].start()
                    for j, chip in enumerate(chips):
                        sends.append(_remote(sh[n], win(n, me), send_sems.at[3 * n + j], recv_sems.at[3 * n + j],
                                             (chip[0], chip[1], c)))
                        sends[-1].start()
                for n in CORE_PARAMS[cc]:
                    for j, chip in enumerate(chips):
                        _remote(sh[n], win(n, 2 * chip[0] + chip[1]), send_sems.at[3 * n + j],
                                recv_sems.at[3 * n + j], (chip[0], chip[1], c)).wait_recv()
                    mine[n].wait()
                    sends.append(_remote(full[n], full[n], pass_send.at[n], pass_recv.at[n], (x, y, 1 - c)))
                    sends[-1].start()
                for n in CORE_PARAMS[1 - cc]:
                    _remote(full[n], full[n], pass_send.at[n], pass_recv.at[n], (x, y, 1 - c)).wait_recv()
                for cp in sends:
                    cp.wait_send()

    out_shape = [jax.ShapeDtypeStruct(full, BF16) for _, _, full in BIG_LAYOUT]
    return _dma_call(body, name, N_BIG, out_shape, (3 * N_BIG, 3 * N_BIG, N_BIG, N_BIG, N_BIG))(*shards)


_HBM = pl.BlockSpec(memory_space=pltpu.HBM)
_SEM = pl.BlockSpec(memory_space=pltpu.SEMAPHORE)


def _hbm(a):
    return pltpu.with_memory_space_constraint(a, pltpu.HBM)


def _gather_copies(sh, full, send_sems, recv_sems):
    locals_ = [_local_shape(kind, f) for _, kind, f in BIG_LAYOUT]
    x, y, c = _place()
    me = 2 * x + y
    pairs = []
    for n in range(N_BIG):
        win = lambda k: _window(full[n], BIG_LAYOUT[n][1], k, locals_[n])
        for j, chip in enumerate(_other_chips(x, y)):
            mk = lambda dst: _remote(sh[n], dst, send_sems.at[3 * n + j], recv_sems.at[3 * n + j],
                                     (chip[0], chip[1], c))
            pairs.append((mk(win(me)), mk(win(2 * chip[0] + chip[1]))))
    return pairs


def gather_start(shards, fulls, after, name):
    def body(*refs):
        sh, full = refs[:N_BIG], refs[N_BIG:2 * N_BIG]
        send_sems, recv_sems = refs[2 * N_BIG + len(after):2 * N_BIG + len(after) + 2]
        for out, _ in _gather_copies(sh, full, send_sems, recv_sems):
            out.start()
        refs[-1][...] = jnp.zeros_like(refs[-1])

    thru = [pltpu.HBM(a.shape, a.dtype) for a in (*shards, *fulls)]
    res = pl.pallas_call(
        body, name=name,
        out_shape=(pltpu.SemaphoreType.DMA((3 * N_BIG,)), pltpu.SemaphoreType.DMA((3 * N_BIG,)), *thru,
                   jax.ShapeDtypeStruct((8, 128), F32)),
        in_specs=[_HBM] * (2 * N_BIG) + [_ANY] * len(after),
        out_specs=(_SEM, _SEM, *[_HBM] * (2 * N_BIG), pl.BlockSpec(memory_space=pltpu.VMEM)),
        input_output_aliases={i: 2 + i for i in range(2 * N_BIG)},
        compiler_params=pltpu.CompilerParams(has_side_effects=pltpu.SideEffectType.DATAFLOW_SIDE_EFFECTING),
    )(*[_hbm(a) for a in (*shards, *fulls)], *after)
    return (res[0], res[1], res[2:2 + N_BIG], res[2 + N_BIG:2 + 2 * N_BIG]), res[-1]


def gather_wait(send_sems, recv_sems, shards, fulls, after, name):
    def body(*refs):
        sh, full = refs[:N_BIG], refs[N_BIG:2 * N_BIG]
        ssem, rsem = refs[2 * N_BIG:2 * N_BIG + 2]
        for out, inc in _gather_copies(sh, full, ssem, rsem):
            out.wait_send()
            inc.wait_recv()

    thru = [pltpu.HBM(a.shape, a.dtype) for a in (*shards, *fulls)]
    res = pl.pallas_call(
        body, name=name, out_shape=thru,
        in_specs=[_HBM] * (2 * N_BIG) + [_SEM, _SEM, _ANY], out_specs=[_HBM] * (2 * N_BIG),
        input_output_aliases={i: i for i in range(2 * N_BIG)},
        compiler_params=pltpu.CompilerParams(has_side_effects=pltpu.SideEffectType.DATAFLOW_SIDE_EFFECTING),
    )(*shards, *fulls, send_sems, recv_sems, after)
    return res[N_BIG:]


def reduce_d2d(g0, g1, name):
    nb = len(g0)

    def body(*refs):
        g, refs = (refs[:nb], refs[nb:2 * nb]), refs[2 * nb:]
        sib, (send_sems, recv_sems) = refs[:nb], refs[nb:]
        x, y, c = _place()
        for cc in (0, 1):
            @pl.when(c == cc)
            def _():
                sends = [_remote(g[1 - cc][n], sib[n], send_sems.at[n], recv_sems.at[n], (x, y, 1 - c))
                         for n in range(nb)]
                for cp in sends:
                    cp.start()
                for cp in sends:
                    cp.wait_recv()
                for cp in sends:
                    cp.wait_send()

    out_shape = [jax.ShapeDtypeStruct(a.shape, a.dtype) for a in g0]
    return _dma_call(body, name, 2 * nb, out_shape, (nb, nb))(*g0, *g1)


def reduce_ici(sums, layout, name):
    nb = len(sums)
    locals_ = [_local_shape(kind, full) for _, kind, full in layout]

    def body(*refs):
        src, recv, (send_sems, recv_sems) = refs[:nb], refs[nb:2 * nb], refs[2 * nb:]
        x, y, c = _place()
        chips = _other_chips(x, y)
        copies = []
        for n in range(nb):
            for j, chip in enumerate(chips):
                cp = _remote(_window(src[n], layout[n][1], 2 * chip[0] + chip[1], locals_[n]), recv[n].at[j],
                             send_sems.at[3 * n + j], recv_sems.at[3 * n + j], (chip[0], chip[1], c))
                cp.start()
                copies.append(cp)
        for cp in copies:
            cp.wait_recv()
        for cp in copies:
            cp.wait_send()

    out_shape = [jax.ShapeDtypeStruct((3, *ls), a.dtype) for ls, a in zip(locals_, sums)]
    return _dma_call(body, name, nb, out_shape, (3 * nb, 3 * nb))(*sums)


def share_d2d(finals, name):
    nb = len(finals)

    def body(*refs):
        out, (send_sems, recv_sems) = refs[nb:2 * nb], refs[2 * nb:]
        x, y, c = _place()
        sends = [_remote(out[n].at[c], out[n].at[c], send_sems.at[n], recv_sems.at[n], (x, y, 1 - c))
                 for n in range(nb)]
        for cp in sends:
            cp.start()
        for n in range(nb):
            _remote(out[n].at[c], out[n].at[1 - c], send_sems.at[n], recv_sems.at[n], (x, y, 1 - c)).wait_recv()
        for cp in sends:
            cp.wait_send()

    out_shape = [jax.ShapeDtypeStruct(a.shape, a.dtype) for a in finals]
    return _dma_call(body, name, nb, out_shape, (nb, nb), aliases={n: n for n in range(nb)})(*finals)


def allgather8(block, name):
    m_per, n = block.shape

    def body(x_ref, out_ref, send_sems, recv_sems, local_sem):
        x, y, c = _place()
        me, sibling = (x, y, c), (x, y, 1 - c)
        chips = _other_chips(x, y)

        def rows(px, py, pc):
            return out_ref.at[pl.ds((4 * px + 2 * py + pc) * m_per, m_per), :]

        def copy(k, blk, to, src=None):
            return pltpu.make_async_remote_copy(
                src_ref=rows(*blk) if src is None else src, dst_ref=rows(*blk), send_sem=send_sems.at[k],
                recv_sem=recv_sems.at[k], device_id=to, device_id_type=MESH)

        mine = pltpu.make_async_copy(x_ref, rows(*me), local_sem)
        mine.start()
        first = [copy(0, me, sibling, src=x_ref)]
        first += [copy(1 + j, me, (*chip, c), src=x_ref) for j, chip in enumerate(chips)]
        for cp in first:
            cp.start()
        passed = [copy(4 + j, (*chip, c), sibling) for j, chip in enumerate(chips)]
        for j, chip in enumerate(chips):
            copy(1 + j, (*chip, c), me).wait_recv()
            passed[j].start()
        copy(0, sibling, me).wait_recv()
        for j, chip in enumerate(chips):
            copy(4 + j, (*chip, 1 - c), me).wait_recv()
        for cp in first + passed:
            cp.wait_send()
        mine.wait()

    return pl.pallas_call(
        body, name=name, in_specs=[pl.BlockSpec(memory_space=pltpu.VMEM)],
        out_specs=pl.BlockSpec(memory_space=pltpu.VMEM),
        out_shape=jax.ShapeDtypeStruct((8 * m_per, n), block.dtype),
        scratch_shapes=[pltpu.SemaphoreType.DMA((7,)), pltpu.SemaphoreType.DMA((7,)), pltpu.SemaphoreType.DMA],
        compiler_params=pltpu.CompilerParams(vmem_limit_bytes=VMEM_LIMIT),
    )(block)


CONV = ("gdn_conv_w", "ssd_conv_w", "lru_conv_w")
SMALL = ("ada_b", "norm_mix", "gdn_a_log", "gdn_dt_bias", "gdn_norm", "ssd_conv_b", "ssd_a_log", "ssd_dt_bias",
         "ssd_d", "ssd_norm", "lru_conv_b", "lru_w_a", "lru_b_a", "lru_w_x", "lru_b_x", "lru_lambda", "norm_mlp",
         "final_norm")
WEIGHTS = ("ada_w", "ada_b", "norm_mix", "w_in", "gdn_conv_w", "gdn_a_log", "gdn_dt_bias", "gdn_norm", "ssd_conv_w",
           "ssd_conv_b", "ssd_a_log", "ssd_dt_bias", "ssd_d", "ssd_norm", "lru_conv_w", "lru_conv_b", "lru_w_a",
           "lru_b_a", "lru_w_x", "lru_b_x", "lru_lambda", "w_branch", "w_out", "norm_mlp", "w_up", "w_down",
           "final_norm")
PACK_COLS = 1024


def _pack_rows(shape):
    return 8 * -(-math.prod(shape) // (8 * PACK_COLS))


def _pack(arrays, dtype):
    parts = []
    for a in arrays:
        flat = a.reshape(-1).astype(dtype)
        pad = _pack_rows(a.shape) * PACK_COLS - flat.shape[0]
        parts.append((jnp.concatenate([flat, jnp.zeros((pad,), dtype)]) if pad else flat).reshape(-1, PACK_COLS))
    return jnp.concatenate(parts, axis=0)


def _unpack(pack, shapes):
    out, o = [], 0
    for s in shapes:
        r = _pack_rows(s)
        out.append(pack[o:o + r].reshape(-1)[:math.prod(s)].reshape(s))
        o += r
    return out


def _split_w_in(w4):
    w = jnp.concatenate([w4[k] for k in range(4)], axis=1)
    pad = jnp.zeros((w.shape[0], 120), w.dtype)
    gdn = jnp.concatenate([w[:, 0:2056], pad], axis=1)
    ssd = jnp.concatenate([w[:, 2056:2568], w[:, 3080:3592], w[:, 2568:3080], w[:, 3592:3600], pad], axis=1)
    return gdn, ssd, w[:, 3600:4112], w[:, 4112:4624], w[:, 4624:7696]


def _join_w_in(gdn, ssd, lx, lg, gate):
    w = jnp.concatenate([gdn[:, 0:2056], ssd[:, 0:512], ssd[:, 1024:1536], ssd[:, 512:1024], ssd[:, 1536:1544],
                         lx, lg, gate], axis=1)
    return jnp.stack([w[:, k * 1924:(k + 1) * 1924] for k in range(4)])


def _lanes(v, at, width=128):
    return jnp.zeros((1, width), F32).at[0, at:at + v.shape[0]].set(v)


def _block_diag(w):
    return (jnp.eye(8, dtype=w.dtype)[:, None, :, None] * w[:, :, None, :]).reshape(512, 512)


def _diag_blocks(w):
    return jnp.stack([w[n * 64:(n + 1) * 64, n * 64:(n + 1) * 64] for n in range(8)])


TOK_TILE = 512
WIDE_TILE = 256


def _layer_params(p, big, l):
    row = lambda v: v.reshape(1, -1)
    b = dict(zip((n for n, _, _ in BIG_LAYOUT), big))
    gdn = (p["gdn_conv_w"][l], _lanes(p["gdn_a_log"][l], 4), _lanes(p["gdn_dt_bias"][l], 4), row(p["gdn_norm"][l]))
    ssd = (p["ssd_conv_w"][l], row(p["ssd_conv_b"][l]), _lanes(p["ssd_a_log"][l], 0), _lanes(p["ssd_dt_bias"][l], 0),
           row(jnp.repeat(p["ssd_d"][l], 64)), row(p["ssd_norm"][l]))
    lru = (p["lru_conv_w"][l], row(p["lru_conv_b"][l]), _block_diag(p["lru_w_a"][l]), row(p["lru_b_a"][l]),
           _block_diag(p["lru_w_x"][l]), row(p["lru_b_x"][l]), row(p["lru_lambda"][l]))
    return dict(gdn=gdn, ssd=ssd, lru=lru, w_in=_split_w_in(b["w_in"]),
                wb=tuple(b["w_branch"][r] for r in range(3)), w_out=b["w_out"], w_up=b["w_up"],
                w_down=b["w_down"], ada_w=b["ada_w"], ada_b=row(p["ada_b"][l]),
                norm_mix=row(p["norm_mix"][l]), norm_mlp=row(p["norm_mlp"][l]))


def _layer_fwd(x, silu_c, lp, l):
    nm = lambda s: f"l{l}_{s}"
    mod = matmul(silu_c, lp["ada_w"], nm("mod"), bias=lp["ada_b"])
    sh1, sc1, gt1, sh2, sc2, gt2 = (mod[0:1, k * D_MODEL:(k + 1) * D_MODEL] for k in range(N_MOD))
    (h,), _ = scan_fwd(norm1_fn, nm("norm1"), TOK_TILE, [x], [lp["norm_mix"], sc1, sh1], [], [(D_MODEL, BF16)])
    w_gdn, w_ssd, w_lx, w_lg, w_gate = lp["w_in"]
    p_gdn = matmul(h, w_gdn, nm("in_gdn"))
    p_ssd = matmul(h, w_ssd, nm("in_ssd"))
    p_lx = matmul(h, w_lx, nm("in_lx"))
    p_lg = matmul(h, w_lg, nm("in_lg"))
    p_gate = matmul(h, w_gate, nm("in_gate"))
    (ya,), sv_gdn = scan_fwd(gdn_fn, nm("gdn"), CHUNK, [p_gdn], lp["gdn"], [(128, 128)] * 4 + [(8, 1536)],
                             [(512, F32)], save_carry=True)
    (yb,), sv_ssd = scan_fwd(ssd_fn, nm("ssd"), CHUNK, [p_ssd], lp["ssd"], [(128, 128)] * 4 + [(8, 1024)],
                             [(512, F32)], save_carry=True)
    (a, u), sv_lru = scan_fwd(lru_in_fn, nm("lru_in"), TOK_TILE, [p_lx], lp["lru"], [(8, 512)],
                              [(512, F32), (512, F32)], save_carry=True)
    hs = linscan_fwd(a, u, nm("lru_scan"))
    (yc,), _ = scan_fwd(lru_out_fn, nm("lru_out"), TOK_TILE, [hs, p_lg], [], [], [(512, F32)])
    (merged,), _ = scan_fwd(merge_fn, nm("merge"), WIDE_TILE, [ya, yb, yc, p_gate], lp["wb"], [], [(D_MODEL, BF16)])
    mix = matmul(merged, lp["w_out"], nm("out"))
    (h2, x1), _ = scan_fwd(resid_norm_fn, nm("norm2"), TOK_TILE, [x, mix], [gt1, lp["norm_mlp"], sc2, sh2], [],
                           [(D_MODEL, BF16), (D_MODEL, F32)])
    up, act = matmul(h2, lp["w_up"], nm("up"), relu2=True)
    dn = matmul(act, lp["w_down"], nm("down"))
    (x2,), _ = scan_fwd(resid_fn, nm("resid"), TOK_TILE, [x1, dn], [gt2], [], [(D_MODEL, F32)])
    saved = dict(x=x, h=h, p_gdn=p_gdn, p_ssd=p_ssd, p_lx=p_lx, p_lg=p_lg, p_gate=p_gate, sv_gdn=sv_gdn,
                 sv_ssd=sv_ssd, sv_lru=sv_lru, a=a, hs=hs, ya=ya, yb=yb, yc=yc, merged=merged, mix=mix, x1=x1,
                 h2=h2, up=up, act=act, dn=dn, mod=(sh1, sc1, gt1, sh2, sc2, gt2))
    return x2, saved


def _layer_bwd(d_x2, silu_c, lp, sv, l):
    nm = lambda s: f"l{l}_b_{s}"
    sh1, sc1, gt1, sh2, sc2, gt2 = sv["mod"]
    (d_x1, d_dn), (d_gt2,) = scan_bwd(resid_fn, nm("resid"), TOK_TILE, [sv["x1"], sv["dn"]], [gt2], [], [d_x2], 2, 1,
                                      [F32, BF16])
    d_up = matmul(d_dn, lp["w_down"], nm("down_x"), tb=True, relu2_of=sv["up"], out_dtype=BF16)
    g_w_down = matmul(sv["act"], d_dn, nm("down_w"), ta=True)
    d_h2 = matmul(d_up, lp["w_up"], nm("up_x"), tb=True)
    g_w_up = matmul(sv["h2"], d_up, nm("up_w"), ta=True)
    (d_x, d_mix), (d_gt1, g_norm_mlp, d_sc2, d_sh2) = scan_bwd(
        resid_norm_fn, nm("norm2"), TOK_TILE, [sv["x"], sv["mix"]], [gt1, lp["norm_mlp"], sc2, sh2], [],
        [d_h2, d_x1], 2, 4, [F32, BF16])
    d_merged = matmul(d_mix, lp["w_out"], nm("out_x"), tb=True)
    g_w_out = matmul(sv["merged"], d_mix, nm("out_w"), ta=True)
    (d_ya, d_yb, d_yc, d_pgate), g_wb = scan_bwd(
        merge_fn, nm("merge"), WIDE_TILE, [sv["ya"], sv["yb"], sv["yc"], sv["p_gate"]], lp["wb"], [], [d_merged], 4, 3,
        [F32, F32, F32, BF16])
    (d_hs, d_plg), _ = scan_bwd(lru_out_fn, nm("lru_out"), TOK_TILE, [sv["hs"], sv["p_lg"]], [], [], [d_yc], 2, 0,
                                [F32, BF16])
    d_a, d_u = linscan_bwd(sv["a"], sv["hs"], d_hs, nm("lru_scan"))
    (d_plx,), g_lru = scan_bwd(lru_in_fn, nm("lru_in"), TOK_TILE, [sv["p_lx"]], lp["lru"], sv["sv_lru"],
                               [d_a, d_u], 1, 7, [BF16])
    (d_pssd,), g_ssd = scan_bwd(ssd_fn, nm("ssd"), CHUNK, [sv["p_ssd"]], lp["ssd"], sv["sv_ssd"], [d_yb], 1, 6,
                                [BF16])
    (d_pgdn,), g_gdn = scan_bwd(gdn_fn, nm("gdn"), CHUNK, [sv["p_gdn"]], lp["gdn"], sv["sv_gdn"], [d_ya], 1, 4,
                                [BF16])
    d_h = None
    g_w_in = []
    for tag, dp, w in zip(("gdn", "ssd", "lx", "lg", "gate"), (d_pgdn, d_pssd, d_plx, d_plg, d_pgate), lp["w_in"]):
        d_h = matmul(dp, w, nm("in_x_" + tag), tb=True, add=d_h)
        g_w_in.append(matmul(sv["h"], dp, nm("in_w_" + tag), ta=True))
    (d_x0,), (g_norm_mix, d_sc1, d_sh1) = scan_bwd(norm1_fn, nm("norm1"), TOK_TILE, [sv["x"]],
                                                   [lp["norm_mix"], sc1, sh1], [], [d_h, d_x], 1, 3)
    d_mod = jnp.concatenate([d_sh1, d_sc1, d_gt1, d_sh2, d_sc2, d_gt2], axis=1)
    flat = lambda v: v.reshape(-1)
    grads = dict(
        ada_b=flat(d_mod), norm_mix=flat(g_norm_mix),
        gdn_conv_w=g_gdn[0], gdn_a_log=g_gdn[1][0, 4:8], gdn_dt_bias=g_gdn[2][0, 4:8], gdn_norm=flat(g_gdn[3]),
        ssd_conv_w=g_ssd[0], ssd_conv_b=flat(g_ssd[1]), ssd_a_log=g_ssd[2][0, 0:8], ssd_dt_bias=g_ssd[3][0, 0:8],
        ssd_d=g_ssd[4].reshape(8, 64).sum(axis=1), ssd_norm=flat(g_ssd[5]),
        lru_conv_w=g_lru[0], lru_conv_b=flat(g_lru[1]), lru_w_a=_diag_blocks(g_lru[2]), lru_b_a=flat(g_lru[3]),
        lru_w_x=_diag_blocks(g_lru[4]), lru_b_x=flat(g_lru[5]), lru_lambda=flat(g_lru[6]),
        norm_mlp=flat(g_norm_mlp))
    big = [_join_w_in(*g_w_in), jnp.stack(g_wb), g_w_out, g_w_up, g_w_down]
    return d_x0, grads, big


def local_step(x, c, target, p, big):
    c8 = jnp.concatenate([c, jnp.zeros((7, c.shape[1]), F32)], axis=0)
    (silu_c,), _ = scan_fwd(silu_fn, "silu_c", 8, [c8], [], [], [(D_MODEL, F32)])
    lps, saved = [], []
    for l in range(DEPTH):
        lps.append(_layer_params(p, big[l](x), l))
        x, sv = _layer_fwd(x, silu_c, lps[l], l)
        saved.append(sv)
    loss, d_x, g_final = loss_head(x, target, p["final_norm"].reshape(1, -1), "loss_head")
    layer_grads, big_grads = [None] * DEPTH, [None] * DEPTH
    for l in reversed(range(DEPTH)):
        d_x, layer_grads[l], big_grads[l] = _layer_bwd(d_x, silu_c, lps[l], saved[l], l)
    grads = {k: jnp.stack([layer_grads[l][k] for l in range(DEPTH)]) for k in layer_grads[0]}
    grads["final_norm"] = g_final.reshape(-1)
    return loss, d_x, grads, big_grads, silu_c[0]


def _place_shard(shard, kind, full, chip):
    base = lax.empty(full, shard.dtype)
    if kind == "chip":
        return lax.dynamic_update_index_in_dim(base, shard, chip, axis=0)
    axis = 0 if kind == "row" else len(full) - 1
    return lax.dynamic_update_slice_in_dim(base, shard, chip * shard.shape[axis], axis=axis)


def _adam_nd(g, w, m, v, name):
    three = lambda a: a.reshape(-1, *a.shape[-2:])
    return tuple(r.reshape(w.shape) for r in adamw(three(g), three(w), three(m), three(v), name, copy_g=True))


def _reduce_big(g0, g1, place):
    sib = reduce_d2d(g0, g1, "reduce_pool")
    pooled = [add_cast(a, b, s, place, "reduce_pool_" + n) for (n, _, _), a, b, s in zip(REDUCE_LAYOUT, g0, g1, sib)]
    recv = reduce_ici([pb for _, pb in pooled], REDUCE_LAYOUT, "reduce_ici")
    finals = []
    for (n, kind, full), (pf, _), r in zip(REDUCE_LAYOUT, pooled, recv):
        local = r.shape[1:]
        own2 = pf.reshape(-1, full[-1])
        r3 = r.reshape(3, -1, local[-1])
        finals.append(sum4(own2, r3, kind == "col", place, "reduce_sum_" + n).reshape(2, *local))
    return share_d2d(finals, "reduce_share")


def _step(w, m, v, x, c, target):
    chip = 2 * lax.axis_index("x") + lax.axis_index("y")
    place = jnp.stack([chip, lax.axis_index("c")]).astype(jnp.int32)
    conv_shapes = [w[n].shape for n in CONV]
    small_shapes = [w[n].shape for n in SMALL]

    shards = [w[n].astype(BF16) for n, _, _ in BIG_LAYOUT]
    big0 = gather_layer([s[0] for s in shards], "gather_l0")
    conv_all = allgather8(_pack([w[n] for n in CONV], F32), "gather_conv").reshape(8, -1, PACK_COLS)
    own1 = [_place_shard(s[1], kind, full, chip) for s, (_, kind, full) in zip(shards, BIG_LAYOUT)]
    in_flight, token = gather_start([s[1] for s in shards], own1, [big0[0], conv_all], "gather_l1_start")
    c = c + token[0, 0]
    big = [lambda _: big0, lambda x_in: gather_wait(*in_flight, x_in, "gather_l1_wait")]
    conv_parts = [_unpack(conv_all[2 * k], conv_shapes) for k in range(4)]
    p = {n: w[n] for n in SMALL}
    for i, n in enumerate(CONV):
        p[n] = jnp.concatenate([conv_parts[k][i] for k in range(4)], axis=2)

    loss_blk, grad_x, g, big_g, silu_c = local_step(x[0], c, target[0], p, big)
    big_g = dict(zip((n for n, _, _ in REDUCE_LAYOUT), _reduce_big(big_g[0], big_g[1], place)))

    assert SMALL[0] == "ada_b"
    small_pack = _pack([g["ada_b"], silu_c, loss_blk[0, 0:1]] + [g[n] for n in SMALL[1:]] + [g[n] for n in CONV], F32)
    small_all = allgather8(small_pack, "gather_small").reshape(8, -1, PACK_COLS)
    total = _unpack(add8(small_all, "reduce_small"),
                    [small_shapes[0], (D_MODEL,), (1,)] + small_shapes[1:] + [g[n].shape for n in CONV])
    loss = total[2][0]
    small_g = dict(zip(SMALL, [total[0]] + total[3:2 + len(SMALL)]))
    conv_g = {n: lax.dynamic_slice_in_dim(t, chip * w[n].shape[2], w[n].shape[2], axis=2)
              for n, t in zip(CONV, total[2 + len(SMALL):])}

    cols = w["ada_w"].shape[2]
    silu_all = small_all[:, _pack_rows(small_shapes[0]), :]
    big_g["ada_w"] = jnp.stack([
        matmul(silu_all, lax.dynamic_slice_in_dim(small_all[:, N_MOD * l:N_MOD * (l + 1), :].reshape(8, -1),
                                                  chip * cols, cols, axis=1), f"ada_w_grad{l}", ta=True)
        for l in range(DEPTH)])

    grad, delta, new_m, new_v = {}, {}, {}, {}
    for n, _, _ in BIG_LAYOUT:
        delta[n], new_m[n], new_v[n], grad[n] = _adam_nd(big_g[n], w[n], m[n], v[n], "adam_" + n)
    for names, gs, shapes, tag in ((SMALL, small_g, small_shapes, "small"), (CONV, conv_g, conv_shapes, "conv")):
        pk = lambda d: _pack([d[n] for n in names], F32)[None]
        res = adamw(pk(gs), pk(w), pk(m), pk(v), "adam_" + tag)
        for out, r in zip((delta, new_m, new_v), res):
            out.update(zip(names, _unpack(r[0], shapes)))
        grad.update({n: gs[n] for n in names})
    outs = [loss, grad_x[None]]
    for d in (grad, delta, new_m, new_v):
        outs += [d[n] for n in WEIGHTS]
    return tuple(outs)


def kernel(x, c, ada_w, ada_b, norm_mix, w_in, gdn_conv_w, gdn_a_log, gdn_dt_bias, gdn_norm, ssd_conv_w, ssd_conv_b, ssd_a_log, ssd_dt_bias, ssd_d, ssd_norm, lru_conv_w, lru_conv_b, lru_w_a, lru_b_a, lru_w_x, lru_b_x, lru_lambda, w_branch, w_out, norm_mlp, w_up, w_down, final_norm, loss_target, m_ada_w, m_ada_b, m_norm_mix, m_w_in, m_gdn_conv_w, m_gdn_a_log, m_gdn_dt_bias, m_gdn_norm, m_ssd_conv_w, m_ssd_conv_b, m_ssd_a_log, m_ssd_dt_bias, m_ssd_d, m_ssd_norm, m_lru_conv_w, m_lru_conv_b, m_lru_w_a, m_lru_b_a, m_lru_w_x, m_lru_b_x, m_lru_lambda, m_w_branch, m_w_out, m_norm_mlp, m_w_up, m_w_down, m_final_norm, v_ada_w, v_ada_b, v_norm_mix, v_w_in, v_gdn_conv_w, v_gdn_a_log, v_gdn_dt_bias, v_gdn_norm, v_ssd_conv_w, v_ssd_conv_b, v_ssd_a_log, v_ssd_dt_bias, v_ssd_d, v_ssd_norm, v_lru_conv_w, v_lru_conv_b, v_lru_w_a, v_lru_b_a, v_lru_w_x, v_lru_b_x, v_lru_lambda, v_w_branch, v_w_out, v_norm_mlp, v_w_up, v_w_down, v_final_norm):
    given = dict(locals())
    w = {n: given[n] for n in WEIGHTS}
    m = {n: given["m_" + n] for n in WEIGHTS}
    v = {n: given["v_" + n] for n in WEIGHTS}
    return _step(w, m, v, x, c, loss_target)
```

```python
import functools
import math

import jax
import jax.numpy as jnp
from jax import lax
from jax.experimental import pallas as pl
from jax.experimental.pallas import tpu as pltpu

F32 = jnp.float32
BF16 = jnp.bfloat16

D_MODEL = 1024
DEPTH = 2
RMS_EPS = 1e-6
CHUNK = 128
GDN_HEADS = 4
SSD_HEADS = 8
LRU_C = 8.0
D_FF = 4096
N_MOD = 6
W_GDN = 2176
W_SSD = 1664
W_LRU = 512
W_GATE = 3072
ADAM_LR = 0.001
ADAM_B1 = 0.9
ADAM_B2 = 0.999
ADAM_EPS = 1e-08
ADAM_WD = 0.01
ADAM_STEP = 10
VMEM_LIMIT = 56 * 1024 * 1024
MESH = pl.DeviceIdType.MESH


def _dot(a, b, ta, tb):
    dn = (((0 if ta else 1,), (1 if tb else 0,)), ((), ()))
    return lax.dot_general(a.astype(BF16), b.astype(BF16), dn, preferred_element_type=F32)


@functools.partial(jax.custom_vjp, nondiff_argnums=(2, 3))
def mm(a, b, ta, tb):
    return _dot(a, b, ta, tb)


def _mm_fwd(a, b, ta, tb):
    return _dot(a, b, ta, tb), (a, b)


def _mm_bwd(ta, tb, res, g):
    a, b = res
    if not ta and not tb:
        return mm(g, b, False, True), mm(a, g, True, False)
    if not ta and tb:
        return mm(g, b, False, False), mm(g, a, True, False)
    assert ta and not tb
    return mm(b, g, False, True), mm(a, g, False, False)


mm.defvjp(_mm_fwd, _mm_bwd)


def _tri_apply(x, upper):
    t = x.shape[0]
    r = lax.broadcasted_iota(jnp.int32, (t, t), 0)
    c = lax.broadcasted_iota(jnp.int32, (t, t), 1)
    tri = jnp.where((r <= c) if upper else (r >= c), 1.0, 0.0).astype(BF16)
    x1 = x.astype(BF16)
    r1 = x - x1.astype(F32)
    x2 = r1.astype(BF16)
    x3 = (r1 - x2.astype(F32)).astype(BF16)
    d = lambda p: jnp.dot(tri, p, preferred_element_type=F32)
    return (d(x1) + d(x2)) + d(x3)


@jax.custom_vjp
def cumsum_rows(x):
    return _tri_apply(x, False)


cumsum_rows.defvjp(lambda x: (_tri_apply(x, False), None), lambda _, g: (_tri_apply(g, True),))


def _dot_split(a, b):
    a1, b1 = a.astype(BF16), b.astype(BF16)
    a2, b2 = (a - a1.astype(F32)).astype(BF16), (b - b1.astype(F32)).astype(BF16)
    d = lambda p, q: jnp.dot(p, q, preferred_element_type=F32)
    return d(a1, b1) + (d(a1, b2) + d(a2, b1))


def _neumann(ms):
    t = ms[0].shape[0]
    xs = [-m for m in ms]
    qs = [_dot(m, m, False, False) for m in ms]
    n = 2
    while True:
        xs = [x + q + _dot(x, q, False, False) for x, q in zip(xs, qs)]
        n *= 2
        if n >= t:
            break
        qs = [_dot(q, q, False, False) for q in qs]
    rs = [-(x + m + _dot_split(m, x)) for x, m in zip(xs, ms)]
    return [x + r + _dot(x, r, False, False) for x, r in zip(xs, rs)]


@jax.custom_vjp
def tri_solve(ms, rhss):
    return tuple(rhs + _dot(x, rhs, False, False) for x, rhs in zip(_neumann(ms), rhss))


def _tri_solve_fwd(ms, rhss):
    xs = _neumann(ms)
    sols = tuple(rhs + _dot(x, rhs, False, False) for x, rhs in zip(xs, rhss))
    return sols, (tuple(xs), sols)


def _tri_solve_bwd(res, gs):
    xs, sols = res
    d_rhss = tuple(g + _dot(x, g, True, False) for x, g in zip(xs, gs))
    return tuple(-_dot(d, sol, False, True) for d, sol in zip(d_rhss, sols)), d_rhss


tri_solve.defvjp(_tri_solve_fwd, _tri_solve_bwd)


@functools.partial(jax.custom_vjp, nondiff_argnums=(1,))
def split_cols(x, sizes):
    out, o = [], 0
    for s in sizes:
        out.append(x[:, o:o + s])
        o += s
    return tuple(out)


split_cols.defvjp(lambda x, sizes: (split_cols(x, sizes), None),
                  lambda sizes, _, g: (jnp.concatenate(list(g), axis=1),))


@functools.partial(jax.custom_vjp, nondiff_argnums=(1,))
def _last_rows(x, t):
    return x[t - 8:, :]


_last_rows.defvjp(lambda x, t: (_last_rows(x, t), None),
                  lambda t, _, g: (jnp.concatenate([jnp.zeros((t - 8, g.shape[1]), g.dtype), g], axis=0),))


def last8(x):
    return _last_rows(x, x.shape[0])


def _shifted(xp, d, t):
    return (pltpu.roll(xp, d, 0) if d else xp)[8:8 + t, :]


@jax.custom_vjp
def conv4(x, tail, w):
    t = x.shape[0]
    xp = jnp.concatenate([tail, x], axis=0)
    return sum(_shifted(xp, 3 - k, t) * w[k:k + 1, :] for k in range(4))


def _conv4_fwd(x, tail, w):
    return conv4(x, tail, w), (x, tail, w)


def _conv4_bwd(res, g):
    x, tail, w = res
    t = x.shape[0]
    xp = jnp.concatenate([tail, x], axis=0)
    zero8 = jnp.zeros((8, g.shape[1]), g.dtype)
    d_xp = jnp.zeros_like(xp)
    d_w = []
    for k in range(4):
        gk = jnp.concatenate([zero8, g * w[k:k + 1, :]], axis=0)
        d_xp = d_xp + (pltpu.roll(gk, t + 8 - (3 - k), 0) if k < 3 else gk)
        d_w.append(jnp.sum(g * _shifted(xp, 3 - k, t), axis=0, keepdims=True))
    return d_xp[8:, :], d_xp[:8, :], jnp.concatenate(d_w, axis=0)


conv4.defvjp(_conv4_fwd, _conv4_bwd)


def _sigmoid(x):
    return 0.5 * (jnp.tanh(0.5 * x) + 1.0)


def _silu(x):
    return x * _sigmoid(x)


def _softplus(x):
    ax = jnp.where(x > 0, x, -x)
    return jnp.where(x > 0, x, 0.0) + jnp.log(1.0 + jnp.exp(-ax))


def _gelu(x):
    return 0.5 * x * (1.0 + jnp.tanh(math.sqrt(2.0 / math.pi) * (x + 0.044715 * (x * x * x))))


def _expm1(x):
    series = x * (1.0 + x * (0.5 + x * (1.0 / 6.0 + x * (1.0 / 24.0))))
    return jnp.where(jnp.abs(x) < 0.03, series, jnp.exp(x) - 1.0)


def _rms(x, w):
    return x * lax.rsqrt(jnp.mean(x * x, axis=-1, keepdims=True) + RMS_EPS) * w


def _lane_pick(x, j):
    lane = lax.broadcasted_iota(jnp.int32, (1, x.shape[1]), 1)
    return jnp.sum(jnp.where(lane == j, x, 0.0), axis=1, keepdims=True)


def _row_pick(x, j):
    row = lax.broadcasted_iota(jnp.int32, (x.shape[0], 1), 0)
    return jnp.sum(jnp.where(row == j, x, 0.0), axis=0, keepdims=True)


def gdn_fn(carry, seq, params):
    *states, tail = carry
    (tile,) = seq
    conv_w, alog_row, dtb_row, norm_w = params
    t = tile.shape[0]
    qkv_raw, z, sm = split_cols(tile, (1536, 512, 128))
    qkv = _silu(conv4(qkv_raw, tail, conv_w))
    parts = split_cols(qkv, (128,) * 12)
    zs = split_cols(z, (128,) * 4)
    lane = lax.broadcasted_iota(jnp.int32, (1, 128), 1)
    beta_all = _sigmoid(sm)
    g_all = jnp.where((lane >= 4) & (lane < 8), -jnp.exp(alog_row) * _softplus(sm + dtb_row), 0.0)
    gc_all = cumsum_rows(g_all)
    gr_all = gc_all.T
    gl_all = _row_pick(gc_all, t - 1)
    r = lax.broadcasted_iota(jnp.int32, (t, t), 0)
    c = lax.broadcasted_iota(jnp.int32, (t, t), 1)
    heads = range(GDN_HEADS)
    l2 = lambda a: a * lax.rsqrt(jnp.sum(a * a, axis=-1, keepdims=True) + RMS_EPS)
    qn = [l2(parts[h]) * (128.0 ** -0.5) for h in heads]
    kn = [l2(parts[4 + h]) for h in heads]
    beta = [_lane_pick(beta_all, h) for h in heads]
    gc = [_lane_pick(gc_all, 4 + h) for h in heads]
    gl = [_lane_pick(gl_all, 4 + h) for h in heads]
    decay = [jnp.exp(jnp.where(r >= c, gc[h] - _row_pick(gr_all, 4 + h), -1e30)) for h in heads]
    kk = [mm(kn[h], kn[h], False, True) for h in heads]
    qk = [mm(qn[h], kn[h], False, True) for h in heads]
    m = tuple(jnp.where(r > c, beta[h] * kk[h] * decay[h], 0.0) for h in heads)
    eg = [jnp.exp(gc[h]) for h in heads]
    rhs = tuple(jnp.concatenate([beta[h] * parts[8 + h], (beta[h] * eg[h]) * kn[h]], axis=1) for h in heads)
    uw = [split_cols(s, (128, 128)) for s in tri_solve(m, rhs)]
    ws = [mm(uw[h][1], states[h], False, False) for h in heads]
    qs = [mm(qn[h] * eg[h], states[h], False, False) for h in heads]
    v_new = [uw[h][0] - ws[h] for h in heads]
    o = [qs[h] + mm(qk[h] * decay[h], v_new[h], False, False) for h in heads]
    kv = [mm(kn[h] * jnp.exp(gl[h] - gc[h]), v_new[h], True, False) for h in heads]
    new_states = [states[h] * jnp.exp(gl[h]) + kv[h] for h in heads]
    outs = [_rms(o[h], norm_w) * _silu(zs[h]) for h in heads]
    return (*new_states, last8(qkv_raw)), (jnp.concatenate(outs, axis=1),)


def ssd_fn(carry, seq, params):
    *states, tail = carry
    (tile,) = seq
    conv_w, conv_b, alog_row, dtb_row, d_row, norm_w = params
    t = tile.shape[0]
    xbc_raw, z, sm = split_cols(tile, (1024, 512, 128))
    xbc = _silu(conv4(xbc_raw, tail, conv_w) + conv_b)
    x0, x1, x2, x3, b0, b1, c0, c1 = split_cols(xbc, (128,) * 8)
    xs, bs, cs = (x0, x1, x2, x3), (b0, b1), (c0, c1)
    ds = split_cols(d_row, (128,) * 4)
    lane = lax.broadcasted_iota(jnp.int32, (1, 128), 1)
    sub = lax.broadcasted_iota(jnp.int32, (128, 1), 0)
    low = lane < 64
    dt_all = jnp.where(lane < SSD_HEADS, _softplus(sm + dtb_row), 0.0)
    ac_all = cumsum_rows(dt_all * (-jnp.exp(alog_row)))
    ar_all = ac_all.T
    al_all = _row_pick(ac_all, t - 1)
    r = lax.broadcasted_iota(jnp.int32, (t, t), 0)
    c = lax.broadcasted_iota(jnp.int32, (t, t), 1)
    pairs, heads = range(4), range(SSD_HEADS)
    col = [_lane_pick(ac_all, h) for h in heads]
    last = [_lane_pick(al_all, h) for h in heads]
    dt = [_lane_pick(dt_all, h) for h in heads]
    lm = [jnp.exp(jnp.where(r >= c, col[h] - _row_pick(ar_all, h), -1e30)) for h in heads]
    cb = [mm(cs[g], bs[g], False, True) for g in range(2)]
    both = lambda a, b: jnp.where(low, a, b)
    xdt = [xs[p] * both(dt[2 * p], dt[2 * p + 1]) for p in pairs]
    y_off = [mm(cs[p // 2], states[p], False, True) for p in pairs]
    y_lo = [mm(cb[p // 2] * lm[2 * p], jnp.where(low, xdt[p], 0.0), False, False) for p in pairs]
    y_hi = [mm(cb[p // 2] * lm[2 * p + 1], jnp.where(low, 0.0, xdt[p]), False, False) for p in pairs]
    st = [mm(xdt[p] * both(jnp.exp(last[2 * p] - col[2 * p]), jnp.exp(last[2 * p + 1] - col[2 * p + 1])),
             bs[p // 2], True, False) for p in pairs]
    ys = [ds[p] * xs[p] + y_lo[p] + y_hi[p] + y_off[p] * both(jnp.exp(col[2 * p]), jnp.exp(col[2 * p + 1]))
          for p in pairs]
    new_states = [states[p] * jnp.where(sub < 64, jnp.exp(last[2 * p]), jnp.exp(last[2 * p + 1])) + st[p]
                  for p in pairs]
    gz = jnp.concatenate(ys, axis=1) * _silu(z)
    g0, g1 = split_cols(gz, (256, 256))
    n0, n1 = split_cols(norm_w, (256, 256))
    out = jnp.concatenate([_rms(g0, n0), _rms(g1, n1)], axis=1)
    return (*new_states, last8(xbc_raw)), (out,)


def lru_in_fn(carry, seq, params):
    (tail,) = carry
    (x,) = seq
    conv_w, conv_b, w_a, b_a, w_x, b_x, lam = params
    xc = conv4(x, tail, conv_w) + conv_b
    r = _sigmoid(mm(xc, w_a, False, False) + b_a)
    i = _sigmoid(mm(xc, w_x, False, False) + b_x)
    log_a = -LRU_C * r * _softplus(-lam)
    u = jnp.sqrt(-_expm1(2.0 * log_a)) * (i * xc)
    return (last8(x),), (jnp.exp(log_a), u)


def lru_out_fn(carry, seq, params):
    hs, gate = seq
    return (), (hs * _gelu(gate),)


def merge_fn(carry, seq, params):
    ya, yb, yc, gl = seq
    g = split_cols(_sigmoid(gl), (D_MODEL,) * 3)
    merged = sum(g[r] * mm(y, params[r], False, False) for r, y in enumerate((ya, yb, yc)))
    return (), (merged,)


def _adaln(x, w, sc, sh):
    return _rms(x, w) * (1.0 + sc) + sh


def norm1_fn(carry, seq, params):
    (x,) = seq
    return (), (_adaln(x, *params), x)


def resid_norm_fn(carry, seq, params):
    x, mix = seq
    gt, w, sc, sh = params
    x1 = x + gt * mix
    return (), (_adaln(x1, w, sc, sh), x1)


def resid_fn(carry, seq, params):
    x, dn = seq
    (gt,) = params
    return (), (x + gt * dn,)


def silu_fn(carry, seq, params):
    return (), (_silu(seq[0]),)


def _full_spec(a):
    nd = a.ndim
    return pl.BlockSpec(a.shape, lambda i: (0,) * nd)


def _cparams(*sem):
    return pltpu.CompilerParams(dimension_semantics=sem, vmem_limit_bytes=VMEM_LIMIT)


def scan_fwd(fn, name, tile, seqs, params, carry_shapes, outs, save_carry=False):
    rows = seqs[0].shape[0]
    tile = min(tile, rows)
    n = rows // tile
    ns, npar, nc, no = len(seqs), len(params), len(carry_shapes), len(outs)

    def body(*refs):
        seq_refs, refs = refs[:ns], refs[ns:]
        par_refs, refs = refs[:npar], refs[npar:]
        out_refs, refs = refs[:no], refs[no:]
        save_refs, refs = (refs[:nc], refs[nc:]) if save_carry else ((), refs)
        carry_refs = refs

        @pl.when(pl.program_id(0) == 0)
        def _():
            for cr in carry_refs:
                cr[...] = jnp.zeros_like(cr)

        carry = tuple(cr[...] for cr in carry_refs)
        for sr, cv in zip(save_refs, carry):
            sr[0] = cv
        new_carry, res = fn(carry, tuple(r[...].astype(F32) for r in seq_refs),
                            tuple(r[...].astype(F32) for r in par_refs))
        for r, v in zip(out_refs, res):
            r[...] = v.astype(r.dtype)
        for cr, v in zip(carry_refs, new_carry):
            cr[...] = v

    out_shape = [jax.ShapeDtypeStruct((rows, w), dt) for w, dt in outs]
    out_specs = [pl.BlockSpec((tile, w), lambda i: (i, 0)) for w, _ in outs]
    if save_carry:
        out_shape += [jax.ShapeDtypeStruct((n, *s), F32) for s in carry_shapes]
        out_specs += [pl.BlockSpec((1, *s), lambda i: (i, 0, 0)) for s in carry_shapes]
    res = pl.pallas_call(
        body, name=name, grid=(n,),
        in_specs=[pl.BlockSpec((tile, s.shape[1]), lambda i: (i, 0)) for s in seqs] + [_full_spec(p) for p in params],
        out_specs=out_specs, out_shape=out_shape,
        scratch_shapes=[pltpu.VMEM(s, F32) for s in carry_shapes],
        compiler_params=_cparams("arbitrary"),
    )(*seqs, *params)
    return res[:no], res[no:]


def scan_bwd(fn, name, tile, seqs, params, saved, douts, n_dseq, n_dpar, dseq_dtypes=None):
    dseq_dtypes = dseq_dtypes or [F32] * n_dseq
    rows = seqs[0].shape[0]
    tile = min(tile, rows)
    n = rows // tile
    ns, npar, nc, no = len(seqs), len(params), len(saved), len(douts)

    def body(*refs):
        seq_refs, refs = refs[:ns], refs[ns:]
        par_refs, refs = refs[:npar], refs[npar:]
        save_refs, refs = refs[:nc], refs[nc:]
        dout_refs, refs = refs[:no], refs[no:]
        dseq_refs, refs = refs[:n_dseq], refs[n_dseq:]
        dpar_refs, refs = refs[:n_dpar], refs[n_dpar:]
        dcarry_refs = refs

        @pl.when(pl.program_id(0) == 0)
        def _():
            for r in (*dpar_refs, *dcarry_refs):
                r[...] = jnp.zeros_like(r)

        carry = tuple(r[0] for r in save_refs)
        seq = tuple(r[...].astype(F32) for r in seq_refs)
        par = tuple(r[...].astype(F32) for r in par_refs)

        def f(carry, dseq, dpar):
            return fn(carry, (*dseq, *seq[n_dseq:]), (*dpar, *par[n_dpar:]))

        _, vjp = jax.vjp(f, carry, seq[:n_dseq], par[:n_dpar])
        d_carry, d_seq, d_par = vjp((tuple(r[...] for r in dcarry_refs),
                                     tuple(r[...].astype(F32) for r in dout_refs)))
        for r, v in zip(dseq_refs, d_seq):
            r[...] = v.astype(r.dtype)
        for r, v in zip(dpar_refs, d_par):
            r[...] += v
        for r, v in zip(dcarry_refs, d_carry):
            r[...] = v

    rev = lambda i: (n - 1 - i, 0)
    res = pl.pallas_call(
        body, name=name, grid=(n,),
        in_specs=([pl.BlockSpec((tile, s.shape[1]), rev) for s in seqs] + [_full_spec(p) for p in params]
                  + [pl.BlockSpec((1, *s.shape[1:]), lambda i: (n - 1 - i, 0, 0)) for s in saved]
                  + [pl.BlockSpec((tile, d.shape[1]), rev) for d in douts]),
        out_specs=([pl.BlockSpec((tile, s.shape[1]), rev) for s in seqs[:n_dseq]]
                   + [_full_spec(p) for p in params[:n_dpar]]),
        out_shape=([jax.ShapeDtypeStruct((rows, s.shape[1]), dt) for s, dt in zip(seqs[:n_dseq], dseq_dtypes)]
                   + [jax.ShapeDtypeStruct(p.shape, F32) for p in params[:n_dpar]]),
        scratch_shapes=[pltpu.VMEM(s.shape[1:], F32) for s in saved],
        compiler_params=_cparams("arbitrary"),
    )(*seqs, *params, *saved, *douts)
    return res[:n_dseq], res[n_dseq:]


def _tile_of(dim, pref):
    if dim <= pref:
        return dim
    best = max((t for t in range(128, pref + 1, 128) if dim % t == 0), default=None)
    if best is None or (best < 512 and dim <= 2304):
        return dim
    return best


def _row_tile(rows, pref):
    if rows <= pref:
        return rows
    return max(t for t in range(8, pref + 1, 8) if rows % t == 0)


def matmul(a, b, name, ta=False, tb=False, out_dtype=F32, add=None, bias=None, relu2=False, relu2_of=None,
           tm=1024, tn=2048, tk=1024):
    m, k = (a.shape[1], a.shape[0]) if ta else a.shape
    n = b.shape[0] if tb else b.shape[1]
    assert k == (b.shape[1] if tb else b.shape[0])
    tm, tn, tk = _tile_of(m, tm), _tile_of(n, tn), _tile_of(k, tk)
    nm, nn, nk = m // tm, n // tn, k // tk
    assert nk == 1 or (out_dtype == F32 and not relu2 and relu2_of is None)
    dn = (((0 if ta else 1,), (1 if tb else 0,)), ((), ()))
    has_add, has_bias, has_u = add is not None, bias is not None, relu2_of is not None
    n_inner = a.size * a.dtype.itemsize * (nn - 1) >= b.size * b.dtype.itemsize * (nm - 1)
    ij = (lambda g0, g1: (g0, g1)) if n_inner else (lambda g0, g1: (g1, g0))

    def body(*refs):
        a_ref, b_ref, refs = refs[0], refs[1], refs[2:]
        add_ref, refs = (refs[0], refs[1:]) if has_add else (None, refs)
        bias_ref, refs = (refs[0], refs[1:]) if has_bias else (None, refs)
        u_ref, refs = (refs[0], refs[1:]) if has_u else (None, refs)
        o_ref = refs[0]
        r = lax.dot_general(a_ref[...].astype(BF16), b_ref[...].astype(BF16), dn, preferred_element_type=F32)

        def first():
            v = r
            if has_add:
                v = v + add_ref[...]
            if has_bias:
                v = v + bias_ref[...]
            if has_u:
                v = v * (2.0 * jnp.maximum(u_ref[...], 0.0))
            o_ref[...] = v.astype(o_ref.dtype)
            if relu2:
                p = jnp.maximum(v, 0.0)
                refs[1][...] = (p * p).astype(BF16)

        if nk == 1:
            first()
        else:
            pl.when(pl.program_id(2) == 0)(first)

            @pl.when(pl.program_id(2) > 0)
            def _():
                o_ref[...] += r

    def spec(shape, fn):
        return pl.BlockSpec(shape, lambda g0, g1, l: fn(*ij(g0, g1), l))

    a_spec = spec((tk, tm), lambda i, j, l: (l, i)) if ta else spec((tm, tk), lambda i, j, l: (i, l))
    b_spec = spec((tn, tk), lambda i, j, l: (j, l)) if tb else spec((tk, tn), lambda i, j, l: (l, j))
    o_spec = spec((tm, tn), lambda i, j, l: (i, j))
    in_specs, args = [a_spec, b_spec], [a, b]
    if has_add:
        in_specs.append(o_spec)
        args.append(add)
    if has_bias:
        in_specs.append(spec((1, tn), lambda i, j, l: (0, j)))
        args.append(bias)
    if has_u:
        in_specs.append(o_spec)
        args.append(relu2_of)
    out_shape = [jax.ShapeDtypeStruct((m, n), out_dtype)] + ([jax.ShapeDtypeStruct((m, n), BF16)] if relu2 else [])
    res = pl.pallas_call(
        body, name=name, grid=(nm, nn, nk) if n_inner else (nn, nm, nk), in_specs=in_specs,
        out_specs=[o_spec] * len(out_shape), out_shape=out_shape,
        compiler_params=_cparams("parallel", "parallel", "arbitrary"),
    )(*args)
    return res if relu2 else res[0]


LIN_TILE = 512


def linscan_fwd(a, u, name):
    rows, w = a.shape
    tile = min(LIN_TILE, rows)

    def body(a_ref, u_ref, h_ref, hc):
        @pl.when(pl.program_id(0) == 0)
        def _():
            hc[...] = jnp.zeros_like(hc)

        row = lax.broadcasted_iota(jnp.int32, (8, 1), 0)

        def group(k, h_in):
            rows8 = pl.ds(pl.multiple_of(k * 8, 8), 8)
            pa, pu = a_ref[rows8, :], u_ref[rows8, :]
            for d in (1, 2, 4):
                pu = pu + pa * jnp.where(row >= d, pltpu.roll(pu, d, 0), 0.0)
                pa = pa * jnp.where(row >= d, pltpu.roll(pa, d, 0), 1.0)
            h_ref[rows8, :] = pa * h_in + pu
            return h_ref[pl.ds(k * 8 + 7, 1), :]

        hc[...] = lax.fori_loop(0, tile // 8, group, hc[...], unroll=4)

    spec = pl.BlockSpec((tile, w), lambda i: (i, 0))
    return pl.pallas_call(
        body, name=name, grid=(rows // tile,), in_specs=[spec, spec], out_specs=spec,
        out_shape=jax.ShapeDtypeStruct((rows, w), F32), scratch_shapes=[pltpu.VMEM((1, w), F32)],
        compiler_params=_cparams("arbitrary"),
    )(a, u)


def linscan_bwd(a, hs, dh, name):
    rows, w = a.shape
    tile = min(LIN_TILE, rows)
    n = rows // tile
    per = tile // 8

    def body(a_ref, h_ref, hprev_ref, dh_ref, da_ref, du_ref, cc):
        i = pl.program_id(0)

        @pl.when(i == 0)
        def _():
            cc[...] = jnp.zeros_like(cc)

        row = lax.broadcasted_iota(jnp.int32, (8, 1), 0)
        h_before = jnp.where(i == n - 1, 0.0, hprev_ref[7:8, :])

        def group(s, c_in):
            k = per - 1 - s
            rows8 = pl.ds(pl.multiple_of(k * 8, 8), 8)
            av, hv = a_ref[rows8, :], h_ref[rows8, :]
            pb = jnp.where(row < 7, pltpu.roll(av, 7, 0), 1.0)
            pg = dh_ref[rows8, :]
            for d in (1, 2, 4):
                pg = pg + pb * jnp.where(row < 8 - d, pltpu.roll(pg, 8 - d, 0), 0.0)
                pb = pb * jnp.where(row < 8 - d, pltpu.roll(pb, 8 - d, 0), 1.0)
            g = pg + pb * c_in
            du_ref[rows8, :] = g
            h_prev = jnp.where(k == 0, h_before, h_ref[pl.ds(jnp.maximum(k * 8 - 1, 0), 1), :])
            da_ref[rows8, :] = g * jnp.where(row >= 1, pltpu.roll(hv, 1, 0), h_prev)
            return a_ref[pl.ds(k * 8, 1), :] * du_ref[pl.ds(k * 8, 1), :]

        cc[...] = lax.fori_loop(0, per, group, cc[...], unroll=4)

    rev = pl.BlockSpec((tile, w), lambda i: (n - 1 - i, 0))
    prev = pl.BlockSpec((8, w), lambda i: (jnp.maximum((n - 1 - i) * per - 1, 0), 0))
    return pl.pallas_call(
        body, name=name, grid=(n,), in_specs=[rev, rev, prev, rev], out_specs=[rev, rev],
        out_shape=[jax.ShapeDtypeStruct((rows, w), F32)] * 2, scratch_shapes=[pltpu.VMEM((1, w), F32)],
        compiler_params=_cparams("arbitrary"),
    )(a, hs, hs, dh)


def loss_head(x, target, w, name):
    rows, d = x.shape
    tile = min(512, rows)

    def body(x_ref, t_ref, w_ref, loss_ref, dx_ref, dw_ref):
        @pl.when(pl.program_id(0) == 0)
        def _():
            loss_ref[...] = jnp.zeros_like(loss_ref)
            dw_ref[...] = jnp.zeros_like(dw_ref)

        tv = t_ref[...]

        def f(xv, wv):
            e = _rms(xv, wv) - tv
            return 0.5 * jnp.sum(jnp.mean(e * e, axis=-1, keepdims=True), axis=0, keepdims=True)

        val, vjp = jax.vjp(f, x_ref[...], w_ref[...])
        dxv, dwv = vjp(jnp.ones((1, 1), F32))
        loss_ref[...] += jnp.broadcast_to(val, loss_ref.shape)
        dx_ref[...] = dxv
        dw_ref[...] += dwv

    spec = pl.BlockSpec((tile, d), lambda i: (i, 0))
    return pl.pallas_call(
        body, name=name, grid=(rows // tile,), in_specs=[spec, spec, _full_spec(w)],
        out_specs=[pl.BlockSpec((8, 128), lambda i: (0, 0)), spec, _full_spec(w)],
        out_shape=[jax.ShapeDtypeStruct((8, 128), F32), jax.ShapeDtypeStruct((rows, d), F32),
                   jax.ShapeDtypeStruct(w.shape, F32)],
        compiler_params=_cparams("arbitrary"),
    )(x, target, w)


def adamw(g, w, m, v, name, copy_g=False):
    layers, rows, cols = g.shape
    tile = _row_tile(rows, 256)
    n_out = 4 if copy_g else 3

    def body(g_ref, w_ref, m_ref, v_ref, d_ref, nm_ref, nv_ref, *g_out):
        gv = g_ref[...]
        if copy_g:
            g_out[0][...] = gv
        nm = ADAM_B1 * m_ref[...] + (1.0 - ADAM_B1) * gv
        nv = ADAM_B2 * v_ref[...] + (1.0 - ADAM_B2) * (gv * gv)
        m_hat = nm / (1.0 - ADAM_B1 ** ADAM_STEP)
        v_hat = nv / (1.0 - ADAM_B2 ** ADAM_STEP)
        d_ref[...] = -ADAM_LR * (m_hat / (jnp.sqrt(v_hat) + ADAM_EPS) + ADAM_WD * w_ref[...])
        nm_ref[...] = nm
        nv_ref[...] = nv

    spec = pl.BlockSpec((None, tile, cols), lambda l, i: (l, i, 0))
    return pl.pallas_call(
        body, name=name, grid=(layers, rows // tile), in_specs=[spec] * 4, out_specs=[spec] * n_out,
        out_shape=[jax.ShapeDtypeStruct((layers, rows, cols), F32)] * n_out,
        compiler_params=_cparams("parallel", "parallel"),
    )(g, w, m, v)


def add_cast(g0, g1, sib, place, name):
    shape = sib.shape
    g0, g1, sib = (a.reshape(-1, shape[-1]) for a in (g0, g1, sib))
    rows, cols = sib.shape
    tile = _row_tile(rows, max(8, min(256, (512 * 1024) // cols)))

    def body(k_ref, g0_ref, g1_ref, s_ref, o_ref, ob_ref):
        s = jnp.where(k_ref[1] == 0, g0_ref[...], g1_ref[...]) + s_ref[...]
        o_ref[...] = s
        ob_ref[...] = s.astype(BF16)

    spec = pl.BlockSpec((tile, cols), lambda i, k: (i, 0))
    s, sb = pl.pallas_call(
        body, name=name,
        grid_spec=pltpu.PrefetchScalarGridSpec(
            num_scalar_prefetch=1, grid=(rows // tile,),
            in_specs=[pl.BlockSpec((tile, cols), lambda i, k: (i * (1 - k[1]), 0)),
                      pl.BlockSpec((tile, cols), lambda i, k: (i * k[1], 0)), spec],
            out_specs=[spec, spec]),
        out_shape=[jax.ShapeDtypeStruct((rows, cols), F32), jax.ShapeDtypeStruct((rows, cols), BF16)],
        compiler_params=_cparams("arbitrary"),
    )(place, g0, g1, sib)
    return s.reshape(shape), sb.reshape(shape)


def sum4(owns, recvs, by_cols, place, name):
    _, r, c = recvs[0].shape
    tile = _row_tile(r, 256)
    nt = r // tile

    def own_map(l):
        on = (lambda k: 1 - k[1]) if l == 0 else (lambda k: k[1])
        if by_cols:
            return lambda i, k: (i * on(k), k[0] * on(k))
        return lambda i, k: ((k[0] * nt + i) * on(k), 0)

    def recv_map(l):
        on = (lambda k: 1 - k[1]) if l == 0 else (lambda k: k[1])
        return lambda i, k: (0, i * on(k), 0)

    def body(k_ref, own0, own1, recv0, recv1, o_ref):
        first = k_ref[1] == 0
        own = jnp.where(first, own0[...], own1[...])
        rv = [jnp.where(first, recv0[j], recv1[j]).astype(F32) for j in range(3)]
        o_ref[...] = ((own + rv[0]) + rv[1]) + rv[2]

    return pl.pallas_call(
        body, name=name,
        grid_spec=pltpu.PrefetchScalarGridSpec(
            num_scalar_prefetch=1, grid=(nt,),
            in_specs=[pl.BlockSpec((tile, c), own_map(0)), pl.BlockSpec((tile, c), own_map(1)),
                      pl.BlockSpec((3, tile, c), recv_map(0)), pl.BlockSpec((3, tile, c), recv_map(1))],
            out_specs=pl.BlockSpec((None, tile, c), lambda i, k: (k[1], i, 0))),
        out_shape=jax.ShapeDtypeStruct((2, r, c), F32),
        compiler_params=_cparams("arbitrary"),
    )(place, owns[0], owns[1], recvs[0], recvs[1])


def add8(parts, name):
    _, rows, cols = parts.shape

    def body(p_ref, o_ref):
        acc = p_ref[0]
        for k in range(1, 8):
            acc = acc + p_ref[k]
        o_ref[...] = acc

    return pl.pallas_call(
        body, name=name, in_specs=[pl.BlockSpec(memory_space=pltpu.VMEM)],
        out_specs=pl.BlockSpec(memory_space=pltpu.VMEM),
        out_shape=jax.ShapeDtypeStruct((rows, cols), F32),
        compiler_params=pltpu.CompilerParams(vmem_limit_bytes=VMEM_LIMIT),
    )(parts)


def _place():
    return lax.axis_index("x"), lax.axis_index("y"), lax.axis_index("c")


def _other_chips(x, y):
    return [(1 - x, y), (x, 1 - y), (1 - x, 1 - y)]


_ANY = pl.BlockSpec(memory_space=pl.ANY)


BIG_LAYOUT = (("ada_w", "col", (1024, 6144)), ("w_in", "chip", (4, 1024, 1924)), ("w_branch", "col", (3, 512, 1024)),
              ("w_out", "row", (1024, 1024)), ("w_up", "col", (1024, 4096)), ("w_down", "row", (4096, 1024)))
N_BIG = len(BIG_LAYOUT)
REDUCE_LAYOUT = BIG_LAYOUT[1:]


def _local_shape(kind, full):
    if kind == "col":
        return (*full[:-1], full[-1] // 4)
    if kind == "row":
        return (full[0] // 4, *full[1:])
    return full[1:]


def _window(ref, kind, k, local):
    if kind == "chip":
        return ref.at[k]
    if kind == "row":
        return ref.at[pl.ds(pl.multiple_of(k * local[0], 8), local[0])]
    idx = (slice(None),) * (len(local) - 1) + (pl.ds(pl.multiple_of(k * local[-1], 128), local[-1]),)
    return ref.at[idx]


def _dma_call(body, name, n_in, out_shape, sems, aliases=None):
    return pl.pallas_call(
        body, name=name, in_specs=[_ANY] * n_in, out_specs=[_ANY] * len(out_shape), out_shape=out_shape,
        scratch_shapes=[pltpu.SemaphoreType.DMA((n,)) for n in sems],
        input_output_aliases=aliases or {},
        compiler_params=pltpu.CompilerParams(has_side_effects=True))


def _remote(src, dst, send_sem, recv_sem, to):
    return pltpu.make_async_remote_copy(src_ref=src, dst_ref=dst, send_sem=send_sem, recv_sem=recv_sem,
                                        device_id=to, device_id_type=MESH)


CORE_PARAMS = ((1, 4), (0, 2, 3, 5))


def gather_layer(shards, name):
    locals_ = [_local_shape(kind, full) for _, kind, full in BIG_LAYOUT]

    def body(*refs):
        sh, full, (send_sems, recv_sems, local_sems, pass_send, pass_recv) = (
            refs[:N_BIG], refs[N_BIG:2 * N_BIG], refs[2 * N_BIG:])
        x, y, c = _place()
        me = 2 * x + y
        chips = _other_chips(x, y)
        win = lambda n, k: _window(full[n], BIG_LAYOUT[n][1], k, locals_[n])
        for cc in (0, 1):
            @pl.when(c == cc)
            def _():
                mine, sends = {}, []
                for n in CORE_PARAMS[cc]:
                    mine[n] = pltpu.make_async_copy(sh[n], win(n, me), local_sems.at[n])
                    mine[n].start()
                    for j, chip in enumerate(chips):
                        sends.append(_remote(sh[n], win(n, me), send_sems.at[3 * n + j], recv_sems.at[3 * n + j],
                                             (chip[0], chip[1], c)))
                        sends[-1].start()
                for n in CORE_PARAMS[cc]:
                    for j, chip in enumerate(chips):
                        _remote(sh[n], win(n, 2 * chip[0] + chip[1]), send_sems.at[3 * n + j],
                                recv_sems.at[3 * n + j], (chip[0], chip[1], c)).wait_recv()
                    mine[n].wait()
                    sends.append(_remote(full[n], full[n], pass_send.at[n], pass_recv.at[n], (x, y, 1 - c)))
                    sends[-1].start()
                for n in CORE_PARAMS[1 - cc]:
                    _remote(full[n], full[n], pass_send.at[n], pass_recv.at[n], (x, y, 1 - c)).wait_recv()
                for cp in sends:
                    cp.wait_send()

    out_shape = [jax.ShapeDtypeStruct(full, BF16) for _, _, full in BIG_LAYOUT]
    return _dma_call(body, name, N_BIG, out_shape, (3 * N_BIG, 3 * N_BIG, N_BIG, N_BIG, N_BIG))(*shards)


_HBM = pl.BlockSpec(memory_space=pltpu.HBM)
_SEM = pl.BlockSpec(memory_space=pltpu.SEMAPHORE)


def _hbm(a):
    return pltpu.with_memory_space_constraint(a, pltpu.HBM)


def _gather_copies(sh, full, send_sems, recv_sems):
    locals_ = [_local_shape(kind, f) for _, kind, f in BIG_LAYOUT]
    x, y, c = _place()
    me = 2 * x + y
    pairs = []
    for n in range(N_BIG):
        win = lambda k: _window(full[n], BIG_LAYOUT[n][1], k, locals_[n])
        for j, chip in enumerate(_other_chips(x, y)):
            mk = lambda dst: _remote(sh[n], dst, send_sems.at[3 * n + j], recv_sems.at[3 * n + j],
                                     (chip[0], chip[1], c))
            pairs.append((mk(win(me)), mk(win(2 * chip[0] + chip[1]))))
    return pairs


def gather_start(shards, fulls, after, name):
    def body(*refs):
        sh, full = refs[:N_BIG], refs[N_BIG:2 * N_BIG]
        send_sems, recv_sems = refs[2 * N_BIG + len(after):2 * N_BIG + len(after) + 2]
        for out, _ in _gather_copies(sh, full, send_sems, recv_sems):
            out.start()
        refs[-1][...] = jnp.zeros_like(refs[-1])

    thru = [pltpu.HBM(a.shape, a.dtype) for a in (*shards, *fulls)]
    res = pl.pallas_call(
        body, name=name,
        out_shape=(pltpu.SemaphoreType.DMA((3 * N_BIG,)), pltpu.SemaphoreType.DMA((3 * N_BIG,)), *thru,
                   jax.ShapeDtypeStruct((8, 128), F32)),
        in_specs=[_HBM] * (2 * N_BIG) + [_ANY] * len(after),
        out_specs=(_SEM, _SEM, *[_HBM] * (2 * N_BIG), pl.BlockSpec(memory_space=pltpu.VMEM)),
        input_output_aliases={i: 2 + i for i in range(2 * N_BIG)},
        compiler_params=pltpu.CompilerParams(has_side_effects=pltpu.SideEffectType.DATAFLOW_SIDE_EFFECTING),
    )(*[_hbm(a) for a in (*shards, *fulls)], *after)
    return (res[0], res[1], res[2:2 + N_BIG], res[2 + N_BIG:2 + 2 * N_BIG]), res[-1]


def gather_wait(send_sems, recv_sems, shards, fulls, after, name):
    def body(*refs):
        sh, full = refs[:N_BIG], refs[N_BIG:2 * N_BIG]
        ssem, rsem = refs[2 * N_BIG:2 * N_BIG + 2]
        for out, inc in _gather_copies(sh, full, ssem, rsem):
            out.wait_send()
            inc.wait_recv()

    thru = [pltpu.HBM(a.shape, a.dtype) for a in (*shards, *fulls)]
    res = pl.pallas_call(
        body, name=name, out_shape=thru,
        in_specs=[_HBM] * (2 * N_BIG) + [_SEM, _SEM, _ANY], out_specs=[_HBM] * (2 * N_BIG),
        input_output_aliases={i: i for i in range(2 * N_BIG)},
        compiler_params=pltpu.CompilerParams(has_side_effects=pltpu.SideEffectType.DATAFLOW_SIDE_EFFECTING),
    )(*shards, *fulls, send_sems, recv_sems, after)
    return res[N_BIG:]


def pool_d2d(g, sender, name):
    nb = len(g)

    def body(*refs):
        src, sib, (send_sems, recv_sems) = refs[:nb], refs[nb:2 * nb], refs[2 * nb:]
        x, y, c = _place()
        copies = [_remote(src[n], sib[n], send_sems.at[n], recv_sems.at[n], (x, y, 1 - c)) for n in range(nb)]

        @pl.when(c == sender)
        def _():
            for cp in copies:
                cp.start()
            for cp in copies:
                cp.wait_send()

        @pl.when(c != sender)
        def _():
            for cp in copies:
                cp.wait_recv()

    out_shape = [jax.ShapeDtypeStruct(a.shape, a.dtype) for a in g]
    return _dma_call(body, name, nb, out_shape, (nb, nb))(*g)


def _reduce_copies(src, recv, layout, send_sems, recv_sems):
    locals_ = [_local_shape(kind, full) for _, kind, full in layout]
    x, y, c = _place()
    copies = []
    for n in range(len(src)):
        for j, chip in enumerate(_other_chips(x, y)):
            copies.append(_remote(_window(src[n], layout[n][1], 2 * chip[0] + chip[1], locals_[n]), recv[n].at[j],
                                  send_sems.at[3 * n + j], recv_sems.at[3 * n + j], (chip[0], chip[1], c)))
    return copies


def _recv_shapes(sums, layout):
    return [(3, *_local_shape(kind, full)) for _, kind, full in layout]


def reduce_ici(sums, layout, core, name):
    nb = len(sums)

    def body(*refs):
        src, recv, (send_sems, recv_sems) = refs[:nb], refs[nb:2 * nb], refs[2 * nb:]
        copies = _reduce_copies(src, recv, layout, send_sems, recv_sems)

        @pl.when(lax.axis_index("c") == core)
        def _():
            for cp in copies:
                cp.start()
            for cp in copies:
                cp.wait_recv()
            for cp in copies:
                cp.wait_send()

    out_shape = [jax.ShapeDtypeStruct(s, a.dtype) for s, a in zip(_recv_shapes(sums, layout), sums)]
    return _dma_call(body, name, nb, out_shape, (3 * nb, 3 * nb))(*sums)


def reduce_start(sums, layout, core, after, name):
    nb = len(sums)
    lands = [lax.empty(s, a.dtype) for s, a in zip(_recv_shapes(sums, layout), sums)]

    def body(*refs):
        src, recv = refs[:nb], refs[nb:2 * nb]
        send_sems, recv_sems = refs[2 * nb + len(after):2 * nb + len(after) + 2]
        copies = _reduce_copies(src, recv, layout, send_sems, recv_sems)

        @pl.when(lax.axis_index("c") == core)
        def _():
            for cp in copies:
                cp.start()

        refs[-1][...] = jnp.zeros_like(refs[-1])

    thru = [pltpu.HBM(a.shape, a.dtype) for a in (*sums, *lands)]
    res = pl.pallas_call(
        body, name=name,
        out_shape=(pltpu.SemaphoreType.DMA((3 * nb,)), pltpu.SemaphoreType.DMA((3 * nb,)), *thru,
                   jax.ShapeDtypeStruct((8, 128), F32)),
        in_specs=[_HBM] * (2 * nb) + [_ANY] * len(after),
        out_specs=(_SEM, _SEM, *[_HBM] * (2 * nb), pl.BlockSpec(memory_space=pltpu.VMEM)),
        input_output_aliases={i: 2 + i for i in range(2 * nb)},
        compiler_params=pltpu.CompilerParams(has_side_effects=pltpu.SideEffectType.DATAFLOW_SIDE_EFFECTING),
    )(*[_hbm(a) for a in (*sums, *lands)], *after)
    return (res[0], res[1], res[2:2 + nb], res[2 + nb:2 + 2 * nb]), res[-1]


def reduce_wait(send_sems, recv_sems, sums, lands, layout, core, after, name):
    nb = len(sums)

    def body(*refs):
        src, recv = refs[:nb], refs[nb:2 * nb]
        ssem, rsem = refs[2 * nb:2 * nb + 2]
        copies = _reduce_copies(src, recv, layout, ssem, rsem)

        @pl.when(lax.axis_index("c") == core)
        def _():
            for cp in copies:
                cp.wait_send()
                cp.wait_recv()

    thru = [pltpu.HBM(a.shape, a.dtype) for a in (*sums, *lands)]
    res = pl.pallas_call(
        body, name=name, out_shape=thru,
        in_specs=[_HBM] * (2 * nb) + [_SEM, _SEM, _ANY], out_specs=[_HBM] * (2 * nb),
        input_output_aliases={i: i for i in range(2 * nb)},
        compiler_params=pltpu.CompilerParams(has_side_effects=pltpu.SideEffectType.DATAFLOW_SIDE_EFFECTING),
    )(*sums, *lands, send_sems, recv_sems, after)
    return res[nb:]


def share_d2d(finals, name):
    nb = len(finals)

    def body(*refs):
        out, (send_sems, recv_sems) = refs[nb:2 * nb], refs[2 * nb:]
        x, y, c = _place()
        sends = [_remote(out[n].at[c], out[n].at[c], send_sems.at[n], recv_sems.at[n], (x, y, 1 - c))
                 for n in range(nb)]
        for cp in sends:
            cp.start()
        for n in range(nb):
            _remote(out[n].at[c], out[n].at[1 - c], send_sems.at[n], recv_sems.at[n], (x, y, 1 - c)).wait_recv()
        for cp in sends:
            cp.wait_send()

    out_shape = [jax.ShapeDtypeStruct(a.shape, a.dtype) for a in finals]
    return _dma_call(body, name, nb, out_shape, (nb, nb), aliases={n: n for n in range(nb)})(*finals)


def allgather8(block, name):
    m_per, n = block.shape

    def body(x_ref, out_ref, send_sems, recv_sems, local_sem):
        x, y, c = _place()
        me, sibling = (x, y, c), (x, y, 1 - c)
        chips = _other_chips(x, y)

        def rows(px, py, pc):
            return out_ref.at[pl.ds((4 * px + 2 * py + pc) * m_per, m_per), :]

        def copy(k, blk, to, src=None):
            return pltpu.make_async_remote_copy(
                src_ref=rows(*blk) if src is None else src, dst_ref=rows(*blk), send_sem=send_sems.at[k],
                recv_sem=recv_sems.at[k], device_id=to, device_id_type=MESH)

        mine = pltpu.make_async_copy(x_ref, rows(*me), local_sem)
        mine.start()
        first = [copy(0, me, sibling, src=x_ref)]
        first += [copy(1 + j, me, (*chip, c), src=x_ref) for j, chip in enumerate(chips)]
        for cp in first:
            cp.start()
        passed = [copy(4 + j, (*chip, c), sibling) for j, chip in enumerate(chips)]
        for j, chip in enumerate(chips):
            copy(1 + j, (*chip, c), me).wait_recv()
            passed[j].start()
        copy(0, sibling, me).wait_recv()
        for j, chip in enumerate(chips):
            copy(4 + j, (*chip, 1 - c), me).wait_recv()
        for cp in first + passed:
            cp.wait_send()
        mine.wait()

    return pl.pallas_call(
        body, name=name, in_specs=[pl.BlockSpec(memory_space=pltpu.VMEM)],
        out_specs=pl.BlockSpec(memory_space=pltpu.VMEM),
        out_shape=jax.ShapeDtypeStruct((8 * m_per, n), block.dtype),
        scratch_shapes=[pltpu.SemaphoreType.DMA((7,)), pltpu.SemaphoreType.DMA((7,)), pltpu.SemaphoreType.DMA],
        compiler_params=pltpu.CompilerParams(vmem_limit_bytes=VMEM_LIMIT),
    )(block)


CONV = ("gdn_conv_w", "ssd_conv_w", "lru_conv_w")
SMALL = ("ada_b", "norm_mix", "gdn_a_log", "gdn_dt_bias", "gdn_norm", "ssd_conv_b", "ssd_a_log", "ssd_dt_bias",
         "ssd_d", "ssd_norm", "lru_conv_b", "lru_w_a", "lru_b_a", "lru_w_x", "lru_b_x", "lru_lambda", "norm_mlp",
         "final_norm")
WEIGHTS = ("ada_w", "ada_b", "norm_mix", "w_in", "gdn_conv_w", "gdn_a_log", "gdn_dt_bias", "gdn_norm", "ssd_conv_w",
           "ssd_conv_b", "ssd_a_log", "ssd_dt_bias", "ssd_d", "ssd_norm", "lru_conv_w", "lru_conv_b", "lru_w_a",
           "lru_b_a", "lru_w_x", "lru_b_x", "lru_lambda", "w_branch", "w_out", "norm_mlp", "w_up", "w_down",
           "final_norm")
PACK_COLS = 1024


def _pack_rows(shape):
    return 8 * -(-math.prod(shape) // (8 * PACK_COLS))


def _pack(arrays, dtype):
    parts = []
    for a in arrays:
        flat = a.reshape(-1).astype(dtype)
        pad = _pack_rows(a.shape) * PACK_COLS - flat.shape[0]
        parts.append((jnp.concatenate([flat, jnp.zeros((pad,), dtype)]) if pad else flat).reshape(-1, PACK_COLS))
    return jnp.concatenate(parts, axis=0)


def _unpack(pack, shapes):
    out, o = [], 0
    for s in shapes:
        r = _pack_rows(s)
        out.append(pack[o:o + r].reshape(-1)[:math.prod(s)].reshape(s))
        o += r
    return out


def _split_w_in(w4):
    w = jnp.concatenate([w4[k] for k in range(4)], axis=1)
    pad = jnp.zeros((w.shape[0], 120), w.dtype)
    gdn = jnp.concatenate([w[:, 0:2056], pad], axis=1)
    ssd = jnp.concatenate([w[:, 2056:2568], w[:, 3080:3592], w[:, 2568:3080], w[:, 3592:3600], pad], axis=1)
    return gdn, ssd, w[:, 3600:4112], w[:, 4112:4624], w[:, 4624:7696]


def _join_w_in(gdn, ssd, lx, lg, gate):
    w = jnp.concatenate([gdn[:, 0:2056], ssd[:, 0:512], ssd[:, 1024:1536], ssd[:, 512:1024], ssd[:, 1536:1544],
                         lx, lg, gate], axis=1)
    return jnp.stack([w[:, k * 1924:(k + 1) * 1924] for k in range(4)])


def _lanes(v, at, width=128):
    return jnp.zeros((1, width), F32).at[0, at:at + v.shape[0]].set(v)


def _block_diag(w):
    return (jnp.eye(8, dtype=w.dtype)[:, None, :, None] * w[:, :, None, :]).reshape(512, 512)


def _diag_blocks(w):
    return jnp.stack([w[n * 64:(n + 1) * 64, n * 64:(n + 1) * 64] for n in range(8)])


TOK_TILE = 512
WIDE_TILE = 256


def _layer_params(p, big, l):
    row = lambda v: v.reshape(1, -1)
    b = dict(zip((n for n, _, _ in BIG_LAYOUT), big))
    gdn = (p["gdn_conv_w"][l], _lanes(p["gdn_a_log"][l], 4), _lanes(p["gdn_dt_bias"][l], 4), row(p["gdn_norm"][l]))
    ssd = (p["ssd_conv_w"][l], row(p["ssd_conv_b"][l]), _lanes(p["ssd_a_log"][l], 0), _lanes(p["ssd_dt_bias"][l], 0),
           row(jnp.repeat(p["ssd_d"][l], 64)), row(p["ssd_norm"][l]))
    lru = (p["lru_conv_w"][l], row(p["lru_conv_b"][l]), _block_diag(p["lru_w_a"][l]), row(p["lru_b_a"][l]),
           _block_diag(p["lru_w_x"][l]), row(p["lru_b_x"][l]), row(p["lru_lambda"][l]))
    return dict(gdn=gdn, ssd=ssd, lru=lru, w_in=_split_w_in(b["w_in"]),
                wb=tuple(b["w_branch"][r] for r in range(3)), w_out=b["w_out"], w_up=b["w_up"],
                w_down=b["w_down"], ada_w=b["ada_w"], ada_b=row(p["ada_b"][l]),
                norm_mix=row(p["norm_mix"][l]), norm_mlp=row(p["norm_mlp"][l]))


def _layer_fwd(x, silu_c, lp, l):
    nm = lambda s: f"l{l}_{s}"
    mod = matmul(silu_c, lp["ada_w"], nm("mod"), bias=lp["ada_b"])
    sh1, sc1, gt1, sh2, sc2, gt2 = (mod[0:1, k * D_MODEL:(k + 1) * D_MODEL] for k in range(N_MOD))
    (h,), _ = scan_fwd(norm1_fn, nm("norm1"), TOK_TILE, [x], [lp["norm_mix"], sc1, sh1], [], [(D_MODEL, BF16)])
    w_gdn, w_ssd, w_lx, w_lg, w_gate = lp["w_in"]
    p_gdn = matmul(h, w_gdn, nm("in_gdn"))
    p_ssd = matmul(h, w_ssd, nm("in_ssd"))
    p_lx = matmul(h, w_lx, nm("in_lx"))
    p_lg = matmul(h, w_lg, nm("in_lg"))
    p_gate = matmul(h, w_gate, nm("in_gate"))
    (ya,), sv_gdn = scan_fwd(gdn_fn, nm("gdn"), CHUNK, [p_gdn], lp["gdn"], [(128, 128)] * 4 + [(8, 1536)],
                             [(512, F32)], save_carry=True)
    (yb,), sv_ssd = scan_fwd(ssd_fn, nm("ssd"), CHUNK, [p_ssd], lp["ssd"], [(128, 128)] * 4 + [(8, 1024)],
                             [(512, F32)], save_carry=True)
    (a, u), sv_lru = scan_fwd(lru_in_fn, nm("lru_in"), TOK_TILE, [p_lx], lp["lru"], [(8, 512)],
                              [(512, F32), (512, F32)], save_carry=True)
    hs = linscan_fwd(a, u, nm("lru_scan"))
    (yc,), _ = scan_fwd(lru_out_fn, nm("lru_out"), TOK_TILE, [hs, p_lg], [], [], [(512, F32)])
    (merged,), _ = scan_fwd(merge_fn, nm("merge"), WIDE_TILE, [ya, yb, yc, p_gate], lp["wb"], [], [(D_MODEL, BF16)])
    mix = matmul(merged, lp["w_out"], nm("out"))
    (h2, x1), _ = scan_fwd(resid_norm_fn, nm("norm2"), TOK_TILE, [x, mix], [gt1, lp["norm_mlp"], sc2, sh2], [],
                           [(D_MODEL, BF16), (D_MODEL, F32)])
    up, act = matmul(h2, lp["w_up"], nm("up"), relu2=True)
    dn = matmul(act, lp["w_down"], nm("down"))
    (x2,), _ = scan_fwd(resid_fn, nm("resid"), TOK_TILE, [x1, dn], [gt2], [], [(D_MODEL, F32)])
    saved = dict(x=x, h=h, p_gdn=p_gdn, p_ssd=p_ssd, p_lx=p_lx, p_lg=p_lg, p_gate=p_gate, sv_gdn=sv_gdn,
                 sv_ssd=sv_ssd, sv_lru=sv_lru, a=a, hs=hs, ya=ya, yb=yb, yc=yc, merged=merged, mix=mix, x1=x1,
                 h2=h2, up=up, act=act, dn=dn, mod=(sh1, sc1, gt1, sh2, sc2, gt2))
    return x2, saved


def _layer_bwd(d_x2, silu_c, lp, sv, l):
    nm = lambda s: f"l{l}_b_{s}"
    sh1, sc1, gt1, sh2, sc2, gt2 = sv["mod"]
    (d_x1, d_dn), (d_gt2,) = scan_bwd(resid_fn, nm("resid"), TOK_TILE, [sv["x1"], sv["dn"]], [gt2], [], [d_x2], 2, 1,
                                      [F32, BF16])
    d_up = matmul(d_dn, lp["w_down"], nm("down_x"), tb=True, relu2_of=sv["up"], out_dtype=BF16)
    g_w_down = matmul(sv["act"], d_dn, nm("down_w"), ta=True)
    d_h2 = matmul(d_up, lp["w_up"], nm("up_x"), tb=True)
    g_w_up = matmul(sv["h2"], d_up, nm("up_w"), ta=True)
    (d_x, d_mix), (d_gt1, g_norm_mlp, d_sc2, d_sh2) = scan_bwd(
        resid_norm_fn, nm("norm2"), TOK_TILE, [sv["x"], sv["mix"]], [gt1, lp["norm_mlp"], sc2, sh2], [],
        [d_h2, d_x1], 2, 4, [F32, BF16])
    d_merged = matmul(d_mix, lp["w_out"], nm("out_x"), tb=True)
    g_w_out = matmul(sv["merged"], d_mix, nm("out_w"), ta=True)
    (d_ya, d_yb, d_yc, d_pgate), g_wb = scan_bwd(
        merge_fn, nm("merge"), WIDE_TILE, [sv["ya"], sv["yb"], sv["yc"], sv["p_gate"]], lp["wb"], [], [d_merged], 4, 3,
        [F32, F32, F32, BF16])
    (d_hs, d_plg), _ = scan_bwd(lru_out_fn, nm("lru_out"), TOK_TILE, [sv["hs"], sv["p_lg"]], [], [], [d_yc], 2, 0,
                                [F32, BF16])
    d_a, d_u = linscan_bwd(sv["a"], sv["hs"], d_hs, nm("lru_scan"))
    (d_plx,), g_lru = scan_bwd(lru_in_fn, nm("lru_in"), TOK_TILE, [sv["p_lx"]], lp["lru"], sv["sv_lru"],
                               [d_a, d_u], 1, 7, [BF16])
    (d_pssd,), g_ssd = scan_bwd(ssd_fn, nm("ssd"), CHUNK, [sv["p_ssd"]], lp["ssd"], sv["sv_ssd"], [d_yb], 1, 6,
                                [BF16])
    (d_pgdn,), g_gdn = scan_bwd(gdn_fn, nm("gdn"), CHUNK, [sv["p_gdn"]], lp["gdn"], sv["sv_gdn"], [d_ya], 1, 4,
                                [BF16])
    d_h = None
    g_w_in = []
    for tag, dp, w in zip(("gdn", "ssd", "lx", "lg", "gate"), (d_pgdn, d_pssd, d_plx, d_plg, d_pgate), lp["w_in"]):
        d_h = matmul(dp, w, nm("in_x_" + tag), tb=True, add=d_h)
        g_w_in.append(matmul(sv["h"], dp, nm("in_w_" + tag), ta=True))
    (d_x0,), (g_norm_mix, d_sc1, d_sh1) = scan_bwd(norm1_fn, nm("norm1"), TOK_TILE, [sv["x"]],
                                                   [lp["norm_mix"], sc1, sh1], [], [d_h, d_x], 1, 3)
    d_mod = jnp.concatenate([d_sh1, d_sc1, d_gt1, d_sh2, d_sc2, d_gt2], axis=1)
    flat = lambda v: v.reshape(-1)
    grads = dict(
        ada_b=flat(d_mod), norm_mix=flat(g_norm_mix),
        gdn_conv_w=g_gdn[0], gdn_a_log=g_gdn[1][0, 4:8], gdn_dt_bias=g_gdn[2][0, 4:8], gdn_norm=flat(g_gdn[3]),
        ssd_conv_w=g_ssd[0], ssd_conv_b=flat(g_ssd[1]), ssd_a_log=g_ssd[2][0, 0:8], ssd_dt_bias=g_ssd[3][0, 0:8],
        ssd_d=g_ssd[4].reshape(8, 64).sum(axis=1), ssd_norm=flat(g_ssd[5]),
        lru_conv_w=g_lru[0], lru_conv_b=flat(g_lru[1]), lru_w_a=_diag_blocks(g_lru[2]), lru_b_a=flat(g_lru[3]),
        lru_w_x=_diag_blocks(g_lru[4]), lru_b_x=flat(g_lru[5]), lru_lambda=flat(g_lru[6]),
        norm_mlp=flat(g_norm_mlp))
    big = [_join_w_in(*g_w_in), jnp.stack(g_wb), g_w_out, g_w_up, g_w_down]
    return d_x0, grads, big


def local_step(x, c, target, p, big, on_big_grads):
    c8 = jnp.concatenate([c, jnp.zeros((7, c.shape[1]), F32)], axis=0)
    (silu_c,), _ = scan_fwd(silu_fn, "silu_c", 8, [c8], [], [], [(D_MODEL, F32)])
    lps, saved = [], []
    for l in range(DEPTH):
        lps.append(_layer_params(p, big[l](x), l))
        x, sv = _layer_fwd(x, silu_c, lps[l], l)
        saved.append(sv)
    loss, d_x, g_final = loss_head(x, target, p["final_norm"].reshape(1, -1), "loss_head")
    layer_grads, big_grads = [None] * DEPTH, [None] * DEPTH
    for l in reversed(range(DEPTH)):
        d_x, layer_grads[l], big_grads[l] = _layer_bwd(d_x, silu_c, lps[l], saved[l], l)
        if l > 0:
            token = on_big_grads(l, big_grads[l])
            mod = saved[l - 1]["mod"]
            saved[l - 1]["mod"] = (*mod[:5], mod[5] + token[0, 0])
    grads = {k: jnp.stack([layer_grads[l][k] for l in range(DEPTH)]) for k in layer_grads[0]}
    grads["final_norm"] = g_final.reshape(-1)
    return loss, d_x, grads, big_grads, silu_c[0]


def _place_shard(shard, kind, full, chip):
    base = lax.empty(full, shard.dtype)
    if kind == "chip":
        return lax.dynamic_update_index_in_dim(base, shard, chip, axis=0)
    axis = 0 if kind == "row" else len(full) - 1
    return lax.dynamic_update_slice_in_dim(base, shard, chip * shard.shape[axis], axis=axis)


def _adam_nd(g, w, m, v, name):
    three = lambda a: a.reshape(-1, *a.shape[-2:])
    return tuple(r.reshape(w.shape) for r in adamw(three(g), three(w), three(m), three(v), name, copy_g=True))


def _pool_layer(g, l, place):
    sib = pool_d2d(g, 1 - l, f"reduce_pool_l{l}")
    return [add_cast(a, a, s, place, f"reduce_add_l{l}_{n}") for (n, _, _), a, s in zip(REDUCE_LAYOUT, g, sib)]


def _reduce_start(g1, place):
    pooled = _pool_layer(g1, 1, place)
    flight, token = reduce_start([pb for _, pb in pooled], REDUCE_LAYOUT, 1, [], "reduce_l1_start")
    return (pooled, flight), token


def _reduce_finish(g0, started, place):
    pooled1, flight = started
    recv1 = reduce_wait(*flight, REDUCE_LAYOUT, 1, g0[0], "reduce_l1_wait")
    pooled0 = _pool_layer(g0, 0, place)
    recv0 = reduce_ici([pb for _, pb in pooled0], REDUCE_LAYOUT, 0, "reduce_ici_l0")
    finals = []
    for (n, kind, full), (p0, _), (p1, _), r0, r1 in zip(REDUCE_LAYOUT, pooled0, pooled1, recv0, recv1):
        local = r0.shape[1:]
        owns = [p.reshape(-1, full[-1]) for p in (p0, p1)]
        recvs = [r.reshape(3, -1, local[-1]) for r in (r0, r1)]
        finals.append(sum4(owns, recvs, kind == "col", place, "reduce_sum_" + n).reshape(2, *local))
    return share_d2d(finals, "reduce_share")


def _step(w, m, v, x, c, target):
    chip = 2 * lax.axis_index("x") + lax.axis_index("y")
    place = jnp.stack([chip, lax.axis_index("c")]).astype(jnp.int32)
    conv_shapes = [w[n].shape for n in CONV]
    small_shapes = [w[n].shape for n in SMALL]

    shards = [w[n].astype(BF16) for n, _, _ in BIG_LAYOUT]
    big0 = gather_layer([s[0] for s in shards], "gather_l0")
    conv_all = allgather8(_pack([w[n] for n in CONV], F32), "gather_conv").reshape(8, -1, PACK_COLS)
    own1 = [_place_shard(s[1], kind, full, chip) for s, (_, kind, full) in zip(shards, BIG_LAYOUT)]
    in_flight, token = gather_start([s[1] for s in shards], own1, [big0[0], conv_all], "gather_l1_start")
    c = c + token[0, 0]
    big = [lambda _: big0, lambda x_in: gather_wait(*in_flight, x_in, "gather_l1_wait")]
    conv_parts = [_unpack(conv_all[2 * k], conv_shapes) for k in range(4)]
    p = {n: w[n] for n in SMALL}
    for i, n in enumerate(CONV):
        p[n] = jnp.concatenate([conv_parts[k][i] for k in range(4)], axis=2)

    started = []

    def on_big_grads(l, grads_l):
        state, token = _reduce_start(grads_l, place)
        started.append(state)
        return token

    loss_blk, grad_x, g, big_g, silu_c = local_step(x[0], c, target[0], p, big, on_big_grads)
    big_g = dict(zip((n for n, _, _ in REDUCE_LAYOUT), _reduce_finish(big_g[0], started[0], place)))

    assert SMALL[0] == "ada_b"
    small_pack = _pack([g["ada_b"], silu_c, loss_blk[0, 0:1]] + [g[n] for n in SMALL[1:]] + [g[n] for n in CONV], F32)
    small_all = allgather8(small_pack, "gather_small").reshape(8, -1, PACK_COLS)
    total = _unpack(add8(small_all, "reduce_small"),
                    [small_shapes[0], (D_MODEL,), (1,)] + small_shapes[1:] + [g[n].shape for n in CONV])
    loss = total[2][0]
    small_g = dict(zip(SMALL, [total[0]] + total[3:2 + len(SMALL)]))
    conv_g = {n: lax.dynamic_slice_in_dim(t, chip * w[n].shape[2], w[n].shape[2], axis=2)
              for n, t in zip(CONV, total[2 + len(SMALL):])}

    cols = w["ada_w"].shape[2]
    silu_all = small_all[:, _pack_rows(small_shapes[0]), :]
    big_g["ada_w"] = jnp.stack([
        matmul(silu_all, lax.dynamic_slice_in_dim(small_all[:, N_MOD * l:N_MOD * (l + 1), :].reshape(8, -1),
                                                  chip * cols, cols, axis=1), f"ada_w_grad{l}", ta=True)
        for l in range(DEPTH)])

    grad, delta, new_m, new_v = {}, {}, {}, {}
    for n, _, _ in BIG_LAYOUT:
        delta[n], new_m[n], new_v[n], grad[n] = _adam_nd(big_g[n], w[n], m[n], v[n], "adam_" + n)
    for names, gs, shapes, tag in ((SMALL, small_g, small_shapes, "small"), (CONV, conv_g, conv_shapes, "conv")):
        pk = lambda d: _pack([d[n] for n in names], F32)[None]
        res = adamw(pk(gs), pk(w), pk(m), pk(v), "adam_" + tag)
        for out, r in zip((delta, new_m, new_v), res):
            out.update(zip(names, _unpack(r[0], shapes)))
        grad.update({n: gs[n] for n in names})
    outs = [loss, grad_x[None]]
    for d in (grad, delta, new_m, new_v):
        outs += [d[n] for n in WEIGHTS]
    return tuple(outs)


def kernel(x, c, ada_w, ada_b, norm_mix, w_in, gdn_conv_w, gdn_a_log, gdn_dt_bias, gdn_norm, ssd_conv_w, ssd_conv_b, ssd_a_log, ssd_dt_bias, ssd_d, ssd_norm, lru_conv_w, lru_conv_b, lru_w_a, lru_b_a, lru_w_x, lru_b_x, lru_lambda, w_branch, w_out, norm_mlp, w_up, w_down, final_norm, loss_target, m_ada_w, m_ada_b, m_norm_mix, m_w_in, m_gdn_conv_w, m_gdn_a_log, m_gdn_dt_bias, m_gdn_norm, m_ssd_conv_w, m_ssd_conv_b, m_ssd_a_log, m_ssd_dt_bias, m_ssd_d, m_ssd_norm, m_lru_conv_w, m_lru_conv_b, m_lru_w_a, m_lru_b_a, m_lru_w_x, m_lru_b_x, m_lru_lambda, m_w_branch, m_w_out, m_norm_mlp, m_w_up, m_w_down, m_final_norm, v_ada_w, v_ada_b, v_norm_mix, v_w_in, v_gdn_conv_w, v_gdn_a_log, v_gdn_dt_bias, v_gdn_norm, v_ssd_conv_w, v_ssd_conv_b, v_ssd_a_log, v_ssd_dt_bias, v_ssd_d, v_ssd_norm, v_lru_conv_w, v_lru_conv_b, v_lru_w_a, v_lru_b_a, v_lru_w_x, v_lru_b_x, v_lru_lambda, v_w_branch, v_w_out, v_norm_mlp, v_w_up, v_w_down, v_final_norm):
    given = dict(locals())
    w = {n: given[n] for n in WEIGHTS}
    m = {n: given["m_" + n] for n in WEIGHTS}
    v = {n: given["v_" + n] for n in WEIGHTS}
    return _step(w, m, v, x, c, loss_target)
```

```python
import functools
import math

import jax
import jax.numpy as jnp
from jax import lax
from jax.experimental import pallas as pl
from jax.experimental.pallas import tpu as pltpu

F32 = jnp.float32
BF16 = jnp.bfloat16

D_MODEL = 1024
DEPTH = 2
RMS_EPS = 1e-6
CHUNK = 128
GDN_HEADS = 4
SSD_HEADS = 8
LRU_C = 8.0
D_FF = 4096
N_MOD = 6
W_GDN = 2176
W_SSD = 1664
W_LRU = 512
W_GATE = 3072
ADAM_LR = 0.001
ADAM_B1 = 0.9
ADAM_B2 = 0.999
ADAM_EPS = 1e-08
ADAM_WD = 0.01
ADAM_STEP = 10
VMEM_LIMIT = 56 * 1024 * 1024
MESH = pl.DeviceIdType.MESH


def _dot(a, b, ta, tb):
    dn = (((0 if ta else 1,), (1 if tb else 0,)), ((), ()))
    return lax.dot_general(a.astype(BF16), b.astype(BF16), dn, preferred_element_type=F32)


@functools.partial(jax.custom_vjp, nondiff_argnums=(2, 3))
def mm(a, b, ta, tb):
    return _dot(a, b, ta, tb)


def _mm_fwd(a, b, ta, tb):
    return _dot(a, b, ta, tb), (a, b)


def _mm_bwd(ta, tb, res, g):
    a, b = res
    if not ta and not tb:
        return mm(g, b, False, True), mm(a, g, True, False)
    if not ta and tb:
        return mm(g, b, False, False), mm(g, a, True, False)
    assert ta and not tb
    return mm(b, g, False, True), mm(a, g, False, False)


mm.defvjp(_mm_fwd, _mm_bwd)


def _tri_apply(x, upper):
    t = x.shape[0]
    r = lax.broadcasted_iota(jnp.int32, (t, t), 0)
    c = lax.broadcasted_iota(jnp.int32, (t, t), 1)
    tri = jnp.where((r <= c) if upper else (r >= c), 1.0, 0.0).astype(BF16)
    x1 = x.astype(BF16)
    r1 = x - x1.astype(F32)
    x2 = r1.astype(BF16)
    x3 = (r1 - x2.astype(F32)).astype(BF16)
    d = lambda p: jnp.dot(tri, p, preferred_element_type=F32)
    return (d(x1) + d(x2)) + d(x3)


@jax.custom_vjp
def cumsum_rows(x):
    return _tri_apply(x, False)


cumsum_rows.defvjp(lambda x: (_tri_apply(x, False), None), lambda _, g: (_tri_apply(g, True),))


def _dot_split(a, b):
    a1, b1 = a.astype(BF16), b.astype(BF16)
    a2, b2 = (a - a1.astype(F32)).astype(BF16), (b - b1.astype(F32)).astype(BF16)
    d = lambda p, q: jnp.dot(p, q, preferred_element_type=F32)
    return d(a1, b1) + (d(a1, b2) + d(a2, b1))


def _neumann(ms):
    t = ms[0].shape[0]
    xs = [-m for m in ms]
    qs = [_dot(m, m, False, False) for m in ms]
    n = 2
    while True:
        xs = [x + q + _dot(x, q, False, False) for x, q in zip(xs, qs)]
        n *= 2
        if n >= t:
            break
        qs = [_dot(q, q, False, False) for q in qs]
    rs = [-(x + m + _dot_split(m, x)) for x, m in zip(xs, ms)]
    return [x + r + _dot(x, r, False, False) for x, r in zip(xs, rs)]


@jax.custom_vjp
def tri_solve(ms, rhss):
    return tuple(rhs + _dot(x, rhs, False, False) for x, rhs in zip(_neumann(ms), rhss))


def _tri_solve_fwd(ms, rhss):
    xs = _neumann(ms)
    sols = tuple(rhs + _dot(x, rhs, False, False) for x, rhs in zip(xs, rhss))
    return sols, (tuple(xs), sols)


def _tri_solve_bwd(res, gs):
    xs, sols = res
    d_rhss = tuple(g + _dot(x, g, True, False) for x, g in zip(xs, gs))
    return tuple(-_dot(d, sol, False, True) for d, sol in zip(d_rhss, sols)), d_rhss


tri_solve.defvjp(_tri_solve_fwd, _tri_solve_bwd)


@functools.partial(jax.custom_vjp, nondiff_argnums=(1,))
def split_cols(x, sizes):
    out, o = [], 0
    for s in sizes:
        out.append(x[:, o:o + s])
        o += s
    return tuple(out)


split_cols.defvjp(lambda x, sizes: (split_cols(x, sizes), None),
                  lambda sizes, _, g: (jnp.concatenate(list(g), axis=1),))


@functools.partial(jax.custom_vjp, nondiff_argnums=(1,))
def _last_rows(x, t):
    return x[t - 8:, :]


_last_rows.defvjp(lambda x, t: (_last_rows(x, t), None),
                  lambda t, _, g: (jnp.concatenate([jnp.zeros((t - 8, g.shape[1]), g.dtype), g], axis=0),))


def last8(x):
    return _last_rows(x, x.shape[0])


def _shifted(xp, d, t):
    return (pltpu.roll(xp, d, 0) if d else xp)[8:8 + t, :]


@jax.custom_vjp
def conv4(x, tail, w):
    t = x.shape[0]
    xp = jnp.concatenate([tail, x], axis=0)
    return sum(_shifted(xp, 3 - k, t) * w[k:k + 1, :] for k in range(4))


def _conv4_fwd(x, tail, w):
    return conv4(x, tail, w), (x, tail, w)


def _conv4_bwd(res, g):
    x, tail, w = res
    t = x.shape[0]
    xp = jnp.concatenate([tail, x], axis=0)
    zero8 = jnp.zeros((8, g.shape[1]), g.dtype)
    d_xp = jnp.zeros_like(xp)
    d_w = []
    for k in range(4):
        gk = jnp.concatenate([zero8, g * w[k:k + 1, :]], axis=0)
        d_xp = d_xp + (pltpu.roll(gk, t + 8 - (3 - k), 0) if k < 3 else gk)
        d_w.append(jnp.sum(g * _shifted(xp, 3 - k, t), axis=0, keepdims=True))
    return d_xp[8:, :], d_xp[:8, :], jnp.concatenate(d_w, axis=0)


conv4.defvjp(_conv4_fwd, _conv4_bwd)


def _sigmoid(x):
    return 0.5 * (jnp.tanh(0.5 * x) + 1.0)


def _silu(x):
    return x * _sigmoid(x)


def _softplus(x):
    ax = jnp.where(x > 0, x, -x)
    return jnp.where(x > 0, x, 0.0) + jnp.log(1.0 + jnp.exp(-ax))


def _gelu(x):
    return 0.5 * x * (1.0 + jnp.tanh(math.sqrt(2.0 / math.pi) * (x + 0.044715 * (x * x * x))))


def _expm1(x):
    series = x * (1.0 + x * (0.5 + x * (1.0 / 6.0 + x * (1.0 / 24.0))))
    return jnp.where(jnp.abs(x) < 0.03, series, jnp.exp(x) - 1.0)


def _rms(x, w):
    return x * lax.rsqrt(jnp.mean(x * x, axis=-1, keepdims=True) + RMS_EPS) * w


def _lane_pick(x, j):
    lane = lax.broadcasted_iota(jnp.int32, (1, x.shape[1]), 1)
    return jnp.sum(jnp.where(lane == j, x, 0.0), axis=1, keepdims=True)


def _row_pick(x, j):
    row = lax.broadcasted_iota(jnp.int32, (x.shape[0], 1), 0)
    return jnp.sum(jnp.where(row == j, x, 0.0), axis=0, keepdims=True)


def gdn_fn(carry, seq, params):
    *states, tail = carry
    (tile,) = seq
    conv_w, alog_row, dtb_row, norm_w = params
    t = tile.shape[0]
    qkv_raw, z, sm = split_cols(tile, (1536, 512, 128))
    qkv = _silu(conv4(qkv_raw, tail, conv_w))
    parts = split_cols(qkv, (128,) * 12)
    zs = split_cols(z, (128,) * 4)
    lane = lax.broadcasted_iota(jnp.int32, (1, 128), 1)
    beta_all = _sigmoid(sm)
    g_all = jnp.where((lane >= 4) & (lane < 8), -jnp.exp(alog_row) * _softplus(sm + dtb_row), 0.0)
    gc_all = cumsum_rows(g_all)
    gr_all = gc_all.T
    gl_all = _row_pick(gc_all, t - 1)
    r = lax.broadcasted_iota(jnp.int32, (t, t), 0)
    c = lax.broadcasted_iota(jnp.int32, (t, t), 1)
    heads = range(GDN_HEADS)
    l2 = lambda a: a * lax.rsqrt(jnp.sum(a * a, axis=-1, keepdims=True) + RMS_EPS)
    qn = [l2(parts[h]) * (128.0 ** -0.5) for h in heads]
    kn = [l2(parts[4 + h]) for h in heads]
    beta = [_lane_pick(beta_all, h) for h in heads]
    gc = [_lane_pick(gc_all, 4 + h) for h in heads]
    gl = [_lane_pick(gl_all, 4 + h) for h in heads]
    decay = [jnp.exp(jnp.where(r >= c, gc[h] - _row_pick(gr_all, 4 + h), -1e30)) for h in heads]
    kk = [mm(kn[h], kn[h], False, True) for h in heads]
    qk = [mm(qn[h], kn[h], False, True) for h in heads]
    m = tuple(jnp.where(r > c, beta[h] * kk[h] * decay[h], 0.0) for h in heads)
    eg = [jnp.exp(gc[h]) for h in heads]
    rhs = tuple(jnp.concatenate([beta[h] * parts[8 + h], (beta[h] * eg[h]) * kn[h]], axis=1) for h in heads)
    uw = [split_cols(s, (128, 128)) for s in tri_solve(m, rhs)]
    ws = [mm(uw[h][1], states[h], False, False) for h in heads]
    qs = [mm(qn[h] * eg[h], states[h], False, False) for h in heads]
    v_new = [uw[h][0] - ws[h] for h in heads]
    o = [qs[h] + mm(qk[h] * decay[h], v_new[h], False, False) for h in heads]
    kv = [mm(kn[h] * jnp.exp(gl[h] - gc[h]), v_new[h], True, False) for h in heads]
    new_states = [states[h] * jnp.exp(gl[h]) + kv[h] for h in heads]
    outs = [_rms(o[h], norm_w) * _silu(zs[h]) for h in heads]
    return (*new_states, last8(qkv_raw)), (jnp.concatenate(outs, axis=1),)


def ssd_fn(carry, seq, params):
    *states, tail = carry
    (tile,) = seq
    conv_w, conv_b, alog_row, dtb_row, d_row, norm_w = params
    t = tile.shape[0]
    xbc_raw, z, sm = split_cols(tile, (1024, 512, 128))
    xbc = _silu(conv4(xbc_raw, tail, conv_w) + conv_b)
    x0, x1, x2, x3, b0, b1, c0, c1 = split_cols(xbc, (128,) * 8)
    xs, bs, cs = (x0, x1, x2, x3), (b0, b1), (c0, c1)
    ds = split_cols(d_row, (128,) * 4)
    lane = lax.broadcasted_iota(jnp.int32, (1, 128), 1)
    sub = lax.broadcasted_iota(jnp.int32, (128, 1), 0)
    low = lane < 64
    dt_all = jnp.where(lane < SSD_HEADS, _softplus(sm + dtb_row), 0.0)
    ac_all = cumsum_rows(dt_all * (-jnp.exp(alog_row)))
    ar_all = ac_all.T
    al_all = _row_pick(ac_all, t - 1)
    r = lax.broadcasted_iota(jnp.int32, (t, t), 0)
    c = lax.broadcasted_iota(jnp.int32, (t, t), 1)
    pairs, heads = range(4), range(SSD_HEADS)
    col = [_lane_pick(ac_all, h) for h in heads]
    last = [_lane_pick(al_all, h) for h in heads]
    dt = [_lane_pick(dt_all, h) for h in heads]
    lm = [jnp.exp(jnp.where(r >= c, col[h] - _row_pick(ar_all, h), -1e30)) for h in heads]
    cb = [mm(cs[g], bs[g], False, True) for g in range(2)]
    both = lambda a, b: jnp.where(low, a, b)
    xdt = [xs[p] * both(dt[2 * p], dt[2 * p + 1]) for p in pairs]
    y_off = [mm(cs[p // 2], states[p], False, True) for p in pairs]
    y_lo = [mm(cb[p // 2] * lm[2 * p], jnp.where(low, xdt[p], 0.0), False, False) for p in pairs]
    y_hi = [mm(cb[p // 2] * lm[2 * p + 1], jnp.where(low, 0.0, xdt[p]), False, False) for p in pairs]
    st = [mm(xdt[p] * both(jnp.exp(last[2 * p] - col[2 * p]), jnp.exp(last[2 * p + 1] - col[2 * p + 1])),
             bs[p // 2], True, False) for p in pairs]
    ys = [ds[p] * xs[p] + y_lo[p] + y_hi[p] + y_off[p] * both(jnp.exp(col[2 * p]), jnp.exp(col[2 * p + 1]))
          for p in pairs]
    new_states = [states[p] * jnp.where(sub < 64, jnp.exp(last[2 * p]), jnp.exp(last[2 * p + 1])) + st[p]
                  for p in pairs]
    gz = jnp.concatenate(ys, axis=1) * _silu(z)
    g0, g1 = split_cols(gz, (256, 256))
    n0, n1 = split_cols(norm_w, (256, 256))
    out = jnp.concatenate([_rms(g0, n0), _rms(g1, n1)], axis=1)
    return (*new_states, last8(xbc_raw)), (out,)


def lru_in_fn(carry, seq, params):
    (tail,) = carry
    (x,) = seq
    conv_w, conv_b, w_a, b_a, w_x, b_x, lam = params
    xc = conv4(x, tail, conv_w) + conv_b
    r = _sigmoid(mm(xc, w_a, False, False) + b_a)
    i = _sigmoid(mm(xc, w_x, False, False) + b_x)
    log_a = -LRU_C * r * _softplus(-lam)
    u = jnp.sqrt(-_expm1(2.0 * log_a)) * (i * xc)
    return (last8(x),), (jnp.exp(log_a), u)


def lru_out_fn(carry, seq, params):
    hs, gate = seq
    return (), (hs * _gelu(gate),)


def merge_fn(carry, seq, params):
    ya, yb, yc, gl = seq
    g = split_cols(_sigmoid(gl), (D_MODEL,) * 3)
    merged = sum(g[r] * mm(y, params[r], False, False) for r, y in enumerate((ya, yb, yc)))
    return (), (merged,)


def _adaln(x, w, sc, sh):
    return _rms(x, w) * (1.0 + sc) + sh


def norm1_fn(carry, seq, params):
    (x,) = seq
    return (), (_adaln(x, *params), x)


def resid_norm_fn(carry, seq, params):
    x, mix = seq
    gt, w, sc, sh = params
    x1 = x + gt * mix
    return (), (_adaln(x1, w, sc, sh), x1)


def resid_fn(carry, seq, params):
    x, dn = seq
    (gt,) = params
    return (), (x + gt * dn,)


def silu_fn(carry, seq, params):
    return (), (_silu(seq[0]),)


def _full_spec(a):
    nd = a.ndim
    return pl.BlockSpec(a.shape, lambda i: (0,) * nd)


def _cparams(*sem):
    return pltpu.CompilerParams(dimension_semantics=sem, vmem_limit_bytes=VMEM_LIMIT)


def scan_fwd(fn, name, tile, seqs, params, carry_shapes, outs, save_carry=False):
    rows = seqs[0].shape[0]
    tile = min(tile, rows)
    n = rows // tile
    ns, npar, nc, no = len(seqs), len(params), len(carry_shapes), len(outs)

    def body(*refs):
        seq_refs, refs = refs[:ns], refs[ns:]
        par_refs, refs = refs[:npar], refs[npar:]
        out_refs, refs = refs[:no], refs[no:]
        save_refs, refs = (refs[:nc], refs[nc:]) if save_carry else ((), refs)
        carry_refs = refs

        @pl.when(pl.program_id(0) == 0)
        def _():
            for cr in carry_refs:
                cr[...] = jnp.zeros_like(cr)

        carry = tuple(cr[...] for cr in carry_refs)
        for sr, cv in zip(save_refs, carry):
            sr[0] = cv
        new_carry, res = fn(carry, tuple(r[...].astype(F32) for r in seq_refs),
                            tuple(r[...].astype(F32) for r in par_refs))
        for r, v in zip(out_refs, res):
            r[...] = v.astype(r.dtype)
        for cr, v in zip(carry_refs, new_carry):
            cr[...] = v

    out_shape = [jax.ShapeDtypeStruct((rows, w), dt) for w, dt in outs]
    out_specs = [pl.BlockSpec((tile, w), lambda i: (i, 0)) for w, _ in outs]
    if save_carry:
        out_shape += [jax.ShapeDtypeStruct((n, *s), F32) for s in carry_shapes]
        out_specs += [pl.BlockSpec((1, *s), lambda i: (i, 0, 0)) for s in carry_shapes]
    res = pl.pallas_call(
        body, name=name, grid=(n,),
        in_specs=[pl.BlockSpec((tile, s.shape[1]), lambda i: (i, 0)) for s in seqs] + [_full_spec(p) for p in params],
        out_specs=out_specs, out_shape=out_shape,
        scratch_shapes=[pltpu.VMEM(s, F32) for s in carry_shapes],
        compiler_params=_cparams("arbitrary"),
    )(*seqs, *params)
    return res[:no], res[no:]


def scan_bwd(fn, name, tile, seqs, params, saved, douts, n_dseq, n_dpar, dseq_dtypes=None):
    dseq_dtypes = dseq_dtypes or [F32] * n_dseq
    rows = seqs[0].shape[0]
    tile = min(tile, rows)
    n = rows // tile
    ns, npar, nc, no = len(seqs), len(params), len(saved), len(douts)

    def body(*refs):
        seq_refs, refs = refs[:ns], refs[ns:]
        par_refs, refs = refs[:npar], refs[npar:]
        save_refs, refs = refs[:nc], refs[nc:]
        dout_refs, refs = refs[:no], refs[no:]
        dseq_refs, refs = refs[:n_dseq], refs[n_dseq:]
        dpar_refs, refs = refs[:n_dpar], refs[n_dpar:]
        dcarry_refs = refs

        @pl.when(pl.program_id(0) == 0)
        def _():
            for r in (*dpar_refs, *dcarry_refs):
                r[...] = jnp.zeros_like(r)

        carry = tuple(r[0] for r in save_refs)
        seq = tuple(r[...].astype(F32) for r in seq_refs)
        par = tuple(r[...].astype(F32) for r in par_refs)

        def f(carry, dseq, dpar):
            return fn(carry, (*dseq, *seq[n_dseq:]), (*dpar, *par[n_dpar:]))

        _, vjp = jax.vjp(f, carry, seq[:n_dseq], par[:n_dpar])
        d_carry, d_seq, d_par = vjp((tuple(r[...] for r in dcarry_refs),
                                     tuple(r[...].astype(F32) for r in dout_refs)))
        for r, v in zip(dseq_refs, d_seq):
            r[...] = v.astype(r.dtype)
        for r, v in zip(dpar_refs, d_par):
            r[...] += v
        for r, v in zip(dcarry_refs, d_carry):
            r[...] = v

    rev = lambda i: (n - 1 - i, 0)
    res = pl.pallas_call(
        body, name=name, grid=(n,),
        in_specs=([pl.BlockSpec((tile, s.shape[1]), rev) for s in seqs] + [_full_spec(p) for p in params]
                  + [pl.BlockSpec((1, *s.shape[1:]), lambda i: (n - 1 - i, 0, 0)) for s in saved]
                  + [pl.BlockSpec((tile, d.shape[1]), rev) for d in douts]),
        out_specs=([pl.BlockSpec((tile, s.shape[1]), rev) for s in seqs[:n_dseq]]
                   + [_full_spec(p) for p in params[:n_dpar]]),
        out_shape=([jax.ShapeDtypeStruct((rows, s.shape[1]), dt) for s, dt in zip(seqs[:n_dseq], dseq_dtypes)]
                   + [jax.ShapeDtypeStruct(p.shape, F32) for p in params[:n_dpar]]),
        scratch_shapes=[pltpu.VMEM(s.shape[1:], F32) for s in saved],
        compiler_params=_cparams("arbitrary"),
    )(*seqs, *params, *saved, *douts)
    return res[:n_dseq], res[n_dseq:]


def _tile_of(dim, pref):
    if dim <= pref:
        return dim
    best = max((t for t in range(128, pref + 1, 128) if dim % t == 0), default=None)
    if best is None or (best < 512 and dim <= 2304):
        return dim
    return best


def _row_tile(rows, pref):
    if rows <= pref:
        return rows
    return max(t for t in range(8, pref + 1, 8) if rows % t == 0)


def matmul(a, b, name, ta=False, tb=False, out_dtype=F32, add=None, bias=None, relu2=False, relu2_of=None,
           tm=1024, tn=2048, tk=1024):
    m, k = (a.shape[1], a.shape[0]) if ta else a.shape
    n = b.shape[0] if tb else b.shape[1]
    assert k == (b.shape[1] if tb else b.shape[0])
    tm, tn, tk = _tile_of(m, tm), _tile_of(n, tn), _tile_of(k, tk)
    nm, nn, nk = m // tm, n // tn, k // tk
    assert nk == 1 or (out_dtype == F32 and not relu2 and relu2_of is None)
    dn = (((0 if ta else 1,), (1 if tb else 0,)), ((), ()))
    has_add, has_bias, has_u = add is not None, bias is not None, relu2_of is not None
    n_inner = a.size * a.dtype.itemsize * (nn - 1) >= b.size * b.dtype.itemsize * (nm - 1)
    ij = (lambda g0, g1: (g0, g1)) if n_inner else (lambda g0, g1: (g1, g0))

    def body(*refs):
        a_ref, b_ref, refs = refs[0], refs[1], refs[2:]
        add_ref, refs = (refs[0], refs[1:]) if has_add else (None, refs)
        bias_ref, refs = (refs[0], refs[1:]) if has_bias else (None, refs)
        u_ref, refs = (refs[0], refs[1:]) if has_u else (None, refs)
        o_ref = refs[0]
        r = lax.dot_general(a_ref[...].astype(BF16), b_ref[...].astype(BF16), dn, preferred_element_type=F32)

        def first():
            v = r
            if has_add:
                v = v + add_ref[...]
            if has_bias:
                v = v + bias_ref[...]
            if has_u:
                v = v * (2.0 * jnp.maximum(u_ref[...], 0.0))
            o_ref[...] = v.astype(o_ref.dtype)
            if relu2:
                p = jnp.maximum(v, 0.0)
                refs[1][...] = (p * p).astype(BF16)

        if nk == 1:
            first()
        else:
            pl.when(pl.program_id(2) == 0)(first)

            @pl.when(pl.program_id(2) > 0)
            def _():
                o_ref[...] += r

    def spec(shape, fn):
        return pl.BlockSpec(shape, lambda g0, g1, l: fn(*ij(g0, g1), l))

    a_spec = spec((tk, tm), lambda i, j, l: (l, i)) if ta else spec((tm, tk), lambda i, j, l: (i, l))
    b_spec = spec((tn, tk), lambda i, j, l: (j, l)) if tb else spec((tk, tn), lambda i, j, l: (l, j))
    o_spec = spec((tm, tn), lambda i, j, l: (i, j))
    in_specs, args = [a_spec, b_spec], [a, b]
    if has_add:
        in_specs.append(o_spec)
        args.append(add)
    if has_bias:
        in_specs.append(spec((1, tn), lambda i, j, l: (0, j)))
        args.append(bias)
    if has_u:
        in_specs.append(o_spec)
        args.append(relu2_of)
    out_shape = [jax.ShapeDtypeStruct((m, n), out_dtype)] + ([jax.ShapeDtypeStruct((m, n), BF16)] if relu2 else [])
    res = pl.pallas_call(
        body, name=name, grid=(nm, nn, nk) if n_inner else (nn, nm, nk), in_specs=in_specs,
        out_specs=[o_spec] * len(out_shape), out_shape=out_shape,
        compiler_params=_cparams("parallel", "parallel", "arbitrary"),
    )(*args)
    return res if relu2 else res[0]


LIN_TILE = 512


def linscan_fwd(a, u, name):
    rows, w = a.shape
    tile = min(LIN_TILE, rows)

    def body(a_ref, u_ref, h_ref, hc):
        @pl.when(pl.program_id(0) == 0)
        def _():
            hc[...] = jnp.zeros_like(hc)

        row = lax.broadcasted_iota(jnp.int32, (8, 1), 0)

        def group(k, h_in):
            rows8 = pl.ds(pl.multiple_of(k * 8, 8), 8)
            pa, pu = a_ref[rows8, :], u_ref[rows8, :]
            for d in (1, 2, 4):
                pu = pu + pa * jnp.where(row >= d, pltpu.roll(pu, d, 0), 0.0)
                pa = pa * jnp.where(row >= d, pltpu.roll(pa, d, 0), 1.0)
            h_ref[rows8, :] = pa * h_in + pu
            return h_ref[pl.ds(k * 8 + 7, 1), :]

        hc[...] = lax.fori_loop(0, tile // 8, group, hc[...], unroll=4)

    spec = pl.BlockSpec((tile, w), lambda i: (i, 0))
    return pl.pallas_call(
        body, name=name, grid=(rows // tile,), in_specs=[spec, spec], out_specs=spec,
        out_shape=jax.ShapeDtypeStruct((rows, w), F32), scratch_shapes=[pltpu.VMEM((1, w), F32)],
        compiler_params=_cparams("arbitrary"),
    )(a, u)


def linscan_bwd(a, hs, dh, name):
    rows, w = a.shape
    tile = min(LIN_TILE, rows)
    n = rows // tile
    per = tile // 8

    def body(a_ref, h_ref, hprev_ref, dh_ref, da_ref, du_ref, cc):
        i = pl.program_id(0)

        @pl.when(i == 0)
        def _():
            cc[...] = jnp.zeros_like(cc)

        row = lax.broadcasted_iota(jnp.int32, (8, 1), 0)
        h_before = jnp.where(i == n - 1, 0.0, hprev_ref[7:8, :])

        def group(s, c_in):
            k = per - 1 - s
            rows8 = pl.ds(pl.multiple_of(k * 8, 8), 8)
            av, hv = a_ref[rows8, :], h_ref[rows8, :]
            pb = jnp.where(row < 7, pltpu.roll(av, 7, 0), 1.0)
            pg = dh_ref[rows8, :]
            for d in (1, 2, 4):
                pg = pg + pb * jnp.where(row < 8 - d, pltpu.roll(pg, 8 - d, 0), 0.0)
                pb = pb * jnp.where(row < 8 - d, pltpu.roll(pb, 8 - d, 0), 1.0)
            g = pg + pb * c_in
            du_ref[rows8, :] = g
            h_prev = jnp.where(k == 0, h_before, h_ref[pl.ds(jnp.maximum(k * 8 - 1, 0), 1), :])
            da_ref[rows8, :] = g * jnp.where(row >= 1, pltpu.roll(hv, 1, 0), h_prev)
            return a_ref[pl.ds(k * 8, 1), :] * du_ref[pl.ds(k * 8, 1), :]

        cc[...] = lax.fori_loop(0, per, group, cc[...], unroll=4)

    rev = pl.BlockSpec((tile, w), lambda i: (n - 1 - i, 0))
    prev = pl.BlockSpec((8, w), lambda i: (jnp.maximum((n - 1 - i) * per - 1, 0), 0))
    return pl.pallas_call(
        body, name=name, grid=(n,), in_specs=[rev, rev, prev, rev], out_specs=[rev, rev],
        out_shape=[jax.ShapeDtypeStruct((rows, w), F32)] * 2, scratch_shapes=[pltpu.VMEM((1, w), F32)],
        compiler_params=_cparams("arbitrary"),
    )(a, hs, hs, dh)


def loss_head(x, target, w, name):
    rows, d = x.shape
    tile = min(512, rows)

    def body(x_ref, t_ref, w_ref, loss_ref, dx_ref, dw_ref):
        @pl.when(pl.program_id(0) == 0)
        def _():
            loss_ref[...] = jnp.zeros_like(loss_ref)
            dw_ref[...] = jnp.zeros_like(dw_ref)

        tv = t_ref[...]

        def f(xv, wv):
            e = _rms(xv, wv) - tv
            return 0.5 * jnp.sum(jnp.mean(e * e, axis=-1, keepdims=True), axis=0, keepdims=True)

        val, vjp = jax.vjp(f, x_ref[...], w_ref[...])
        dxv, dwv = vjp(jnp.ones((1, 1), F32))
        loss_ref[...] += jnp.broadcast_to(val, loss_ref.shape)
        dx_ref[...] = dxv
        dw_ref[...] += dwv

    spec = pl.BlockSpec((tile, d), lambda i: (i, 0))
    return pl.pallas_call(
        body, name=name, grid=(rows // tile,), in_specs=[spec, spec, _full_spec(w)],
        out_specs=[pl.BlockSpec((8, 128), lambda i: (0, 0)), spec, _full_spec(w)],
        out_shape=[jax.ShapeDtypeStruct((8, 128), F32), jax.ShapeDtypeStruct((rows, d), F32),
                   jax.ShapeDtypeStruct(w.shape, F32)],
        compiler_params=_cparams("arbitrary"),
    )(x, target, w)


def adamw(g, w, m, v, name, copy_g=False):
    layers, rows, cols = g.shape
    tile = _row_tile(rows, 256)
    n_out = 4 if copy_g else 3

    def body(g_ref, w_ref, m_ref, v_ref, d_ref, nm_ref, nv_ref, *g_out):
        gv = g_ref[...]
        if copy_g:
            g_out[0][...] = gv
        nm = ADAM_B1 * m_ref[...] + (1.0 - ADAM_B1) * gv
        nv = ADAM_B2 * v_ref[...] + (1.0 - ADAM_B2) * (gv * gv)
        m_hat = nm / (1.0 - ADAM_B1 ** ADAM_STEP)
        v_hat = nv / (1.0 - ADAM_B2 ** ADAM_STEP)
        d_ref[...] = -ADAM_LR * (m_hat / (jnp.sqrt(v_hat) + ADAM_EPS) + ADAM_WD * w_ref[...])
        nm_ref[...] = nm
        nv_ref[...] = nv

    spec = pl.BlockSpec((None, tile, cols), lambda l, i: (l, i, 0))
    return pl.pallas_call(
        body, name=name, grid=(layers, rows // tile), in_specs=[spec] * 4, out_specs=[spec] * n_out,
        out_shape=[jax.ShapeDtypeStruct((layers, rows, cols), F32)] * n_out,
        compiler_params=_cparams("parallel", "parallel"),
    )(g, w, m, v)


def add_layers(mine, theirs, name):
    r, c = mine[0].shape
    tile = _row_tile(r, 256)
    nl = len(mine)

    def body(*refs):
        o_ref = refs[-1]
        for l in range(nl):
            o_ref[l] = refs[l][...] + refs[nl + l][...]

    spec = pl.BlockSpec((tile, c), lambda i: (i, 0))
    return pl.pallas_call(
        body, name=name, grid=(r // tile,), in_specs=[spec] * (2 * nl),
        out_specs=pl.BlockSpec((nl, tile, c), lambda i: (0, i, 0)),
        out_shape=jax.ShapeDtypeStruct((nl, r, c), F32), compiler_params=_cparams("parallel"),
    )(*mine, *theirs)


def sum4(own, recv, by_cols, place, name):
    _, r, c = recv.shape
    tile = _row_tile(r, 256)
    nt = r // tile
    own_map = (lambda i, k: (i, k[0])) if by_cols else (lambda i, k: (k[0] * nt + i, 0))

    def body(k_ref, own_ref, recv_ref, o_ref):
        o_ref[...] = ((own_ref[...] + recv_ref[0].astype(F32)) + recv_ref[1].astype(F32)) + recv_ref[2].astype(F32)

    return pl.pallas_call(
        body, name=name,
        grid_spec=pltpu.PrefetchScalarGridSpec(
            num_scalar_prefetch=1, grid=(nt,),
            in_specs=[pl.BlockSpec((tile, c), own_map), pl.BlockSpec((3, tile, c), lambda i, k: (0, i, 0))],
            out_specs=pl.BlockSpec((tile, c), lambda i, k: (i, 0))),
        out_shape=jax.ShapeDtypeStruct((r, c), F32),
        compiler_params=_cparams("arbitrary"),
    )(place, own, recv)


def add8(parts, name):
    _, rows, cols = parts.shape

    def body(p_ref, o_ref):
        acc = p_ref[0]
        for k in range(1, 8):
            acc = acc + p_ref[k]
        o_ref[...] = acc

    return pl.pallas_call(
        body, name=name, in_specs=[pl.BlockSpec(memory_space=pltpu.VMEM)],
        out_specs=pl.BlockSpec(memory_space=pltpu.VMEM),
        out_shape=jax.ShapeDtypeStruct((rows, cols), F32),
        compiler_params=pltpu.CompilerParams(vmem_limit_bytes=VMEM_LIMIT),
    )(parts)


def _place():
    return lax.axis_index("x"), lax.axis_index("y"), lax.axis_index("c")


def _other_chips(x, y):
    return [(1 - x, y), (x, 1 - y), (1 - x, 1 - y)]


_ANY = pl.BlockSpec(memory_space=pl.ANY)


BIG_LAYOUT = (("ada_w", "col", (1024, 6144)), ("w_in", "chip", (4, 1024, 1924)), ("w_branch", "col", (3, 512, 1024)),
              ("w_out", "row", (1024, 1024)), ("w_up", "col", (1024, 4096)), ("w_down", "row", (4096, 1024)))
N_BIG = len(BIG_LAYOUT)
REDUCE_LAYOUT = BIG_LAYOUT[1:]


def _local_shape(kind, full):
    if kind == "col":
        return (*full[:-1], full[-1] // 4)
    if kind == "row":
        return (full[0] // 4, *full[1:])
    return full[1:]


def _window(ref, kind, k, local):
    if kind == "chip":
        return ref.at[k]
    if kind == "row":
        return ref.at[pl.ds(pl.multiple_of(k * local[0], 8), local[0])]
    idx = (slice(None),) * (len(local) - 1) + (pl.ds(pl.multiple_of(k * local[-1], 128), local[-1]),)
    return ref.at[idx]


def _dma_call(body, name, n_in, out_shape, sems, aliases=None):
    return pl.pallas_call(
        body, name=name, in_specs=[_ANY] * n_in, out_specs=[_ANY] * len(out_shape), out_shape=out_shape,
        scratch_shapes=[pltpu.SemaphoreType.DMA((n,)) for n in sems],
        input_output_aliases=aliases or {},
        compiler_params=pltpu.CompilerParams(has_side_effects=True))


def _remote(src, dst, send_sem, recv_sem, to):
    return pltpu.make_async_remote_copy(src_ref=src, dst_ref=dst, send_sem=send_sem, recv_sem=recv_sem,
                                        device_id=to, device_id_type=MESH)


CORE_PARAMS = ((1, 4), (0, 2, 3, 5))


def gather_layer(shards, name):
    locals_ = [_local_shape(kind, full) for _, kind, full in BIG_LAYOUT]

    def body(*refs):
        sh, full, (send_sems, recv_sems, local_sems, pass_send, pass_recv) = (
            refs[:N_BIG], refs[N_BIG:2 * N_BIG], refs[2 * N_BIG:])
        x, y, c = _place()
        me = 2 * x + y
        chips = _other_chips(x, y)
        win = lambda n, k: _window(full[n], BIG_LAYOUT[n][1], k, locals_[n])
        for cc in (0, 1):
            @pl.when(c == cc)
            def _():
                mine, sends = {}, []
                for n in CORE_PARAMS[cc]:
                    mine[n] = pltpu.make_async_copy(sh[n], win(n, me), local_sems.at[n])
                    mine[n].start()
                    for j, chip in enumerate(chips):
                        sends.append(_remote(sh[n], win(n, me), send_sems.at[3 * n + j], recv_sems.at[3 * n + j],
                                             (chip[0], chip[1], c)))
                        sends[-1].start()
                for n in CORE_PARAMS[cc]:
                    for j, chip in enumerate(chips):
                        _remote(sh[n], win(n, 2 * chip[0] + chip[1]), send_sems.at[3 * n + j],
                                recv_sems.at[3 * n + j], (chip[0], chip[1], c)).wait_recv()
                    mine[n].wait()
                    sends.append(_remote(full[n], full[n], pass_send.at[n], pass_recv.at[n], (x, y, 1 - c)))
                    sends[-1].start()
                for n in CORE_PARAMS[1 - cc]:
                    _remote(full[n], full[n], pass_send.at[n], pass_recv.at[n], (x, y, 1 - c)).wait_recv()
                for cp in sends:
                    cp.wait_send()

    out_shape = [jax.ShapeDtypeStruct(full, BF16) for _, _, full in BIG_LAYOUT]
    return _dma_call(body, name, N_BIG, out_shape, (3 * N_BIG, 3 * N_BIG, N_BIG, N_BIG, N_BIG))(*shards)


_HBM = pl.BlockSpec(memory_space=pltpu.HBM)
_SEM = pl.BlockSpec(memory_space=pltpu.SEMAPHORE)


def _hbm(a):
    return pltpu.with_memory_space_constraint(a, pltpu.HBM)


def _gather_copies(sh, full, send_sems, recv_sems):
    locals_ = [_local_shape(kind, f) for _, kind, f in BIG_LAYOUT]
    x, y, c = _place()
    me = 2 * x + y
    pairs = []
    for n in range(N_BIG):
        win = lambda k: _window(full[n], BIG_LAYOUT[n][1], k, locals_[n])
        for j, chip in enumerate(_other_chips(x, y)):
            mk = lambda dst: _remote(sh[n], dst, send_sems.at[3 * n + j], recv_sems.at[3 * n + j],
                                     (chip[0], chip[1], c))
            pairs.append((mk(win(me)), mk(win(2 * chip[0] + chip[1]))))
    return pairs


def gather_start(shards, fulls, after, name):
    def body(*refs):
        sh, full = refs[:N_BIG], refs[N_BIG:2 * N_BIG]
        send_sems, recv_sems = refs[2 * N_BIG + len(after):2 * N_BIG + len(after) + 2]
        for out, _ in _gather_copies(sh, full, send_sems, recv_sems):
            out.start()
        refs[-1][...] = jnp.zeros_like(refs[-1])

    thru = [pltpu.HBM(a.shape, a.dtype) for a in (*shards, *fulls)]
    res = pl.pallas_call(
        body, name=name,
        out_shape=(pltpu.SemaphoreType.DMA((3 * N_BIG,)), pltpu.SemaphoreType.DMA((3 * N_BIG,)), *thru,
                   jax.ShapeDtypeStruct((8, 128), F32)),
        in_specs=[_HBM] * (2 * N_BIG) + [_ANY] * len(after),
        out_specs=(_SEM, _SEM, *[_HBM] * (2 * N_BIG), pl.BlockSpec(memory_space=pltpu.VMEM)),
        input_output_aliases={i: 2 + i for i in range(2 * N_BIG)},
        compiler_params=pltpu.CompilerParams(has_side_effects=pltpu.SideEffectType.DATAFLOW_SIDE_EFFECTING),
    )(*[_hbm(a) for a in (*shards, *fulls)], *after)
    return (res[0], res[1], res[2:2 + N_BIG], res[2 + N_BIG:2 + 2 * N_BIG]), res[-1]


def gather_wait(send_sems, recv_sems, shards, fulls, after, name):
    def body(*refs):
        sh, full = refs[:N_BIG], refs[N_BIG:2 * N_BIG]
        ssem, rsem = refs[2 * N_BIG:2 * N_BIG + 2]
        for out, inc in _gather_copies(sh, full, ssem, rsem):
            out.wait_send()
            inc.wait_recv()

    thru = [pltpu.HBM(a.shape, a.dtype) for a in (*shards, *fulls)]
    res = pl.pallas_call(
        body, name=name, out_shape=thru,
        in_specs=[_HBM] * (2 * N_BIG) + [_SEM, _SEM, _ANY], out_specs=[_HBM] * (2 * N_BIG),
        input_output_aliases={i: i for i in range(2 * N_BIG)},
        compiler_params=pltpu.CompilerParams(has_side_effects=pltpu.SideEffectType.DATAFLOW_SIDE_EFFECTING),
    )(*shards, *fulls, send_sems, recv_sems, after)
    return res[N_BIG:]


def swap_d2d(arrays, name):
    nb = len(arrays)

    def body(*refs):
        src, got, (send_sems, recv_sems) = refs[:nb], refs[nb:2 * nb], refs[2 * nb:]
        x, y, c = _place()
        copies = [_remote(src[n], got[n], send_sems.at[n], recv_sems.at[n], (x, y, 1 - c)) for n in range(nb)]
        for cp in copies:
            cp.start()
        for cp in copies:
            cp.wait_recv()
        for cp in copies:
            cp.wait_send()

    out_shape = [jax.ShapeDtypeStruct(a.shape, a.dtype) for a in arrays]
    return _dma_call(body, name, nb, out_shape, (nb, nb))(*arrays)


def _reduce_copies(src, recv, layout, send_sems, recv_sems):
    locals_ = [_local_shape(kind, full) for _, kind, full in layout]
    x, y, c = _place()
    copies = []
    for n in range(len(src)):
        for j, chip in enumerate(_other_chips(x, y)):
            copies.append(_remote(_window(src[n], layout[n][1], 2 * chip[0] + chip[1], locals_[n]), recv[n].at[j],
                                  send_sems.at[3 * n + j], recv_sems.at[3 * n + j], (chip[0], chip[1], c)))
    return copies


def _recv_shapes(sums, layout):
    return [(3, *_local_shape(kind, full)) for _, kind, full in layout]


def reduce_ici(sums, layout, name):
    nb = len(sums)

    def body(*refs):
        src, recv, (send_sems, recv_sems) = refs[:nb], refs[nb:2 * nb], refs[2 * nb:]
        copies = _reduce_copies(src, recv, layout, send_sems, recv_sems)
        for cp in copies:
            cp.start()
        for cp in copies:
            cp.wait_recv()
        for cp in copies:
            cp.wait_send()

    out_shape = [jax.ShapeDtypeStruct(s, a.dtype) for s, a in zip(_recv_shapes(sums, layout), sums)]
    return _dma_call(body, name, nb, out_shape, (3 * nb, 3 * nb))(*sums)


def reduce_start(sums, layout, after, name):
    nb = len(sums)
    lands = [lax.empty(s, a.dtype) for s, a in zip(_recv_shapes(sums, layout), sums)]

    def body(*refs):
        src, recv = refs[:nb], refs[nb:2 * nb]
        send_sems, recv_sems = refs[2 * nb + len(after):2 * nb + len(after) + 2]
        for cp in _reduce_copies(src, recv, layout, send_sems, recv_sems):
            cp.start()
        refs[-1][...] = jnp.zeros_like(refs[-1])

    thru = [pltpu.HBM(a.shape, a.dtype) for a in (*sums, *lands)]
    res = pl.pallas_call(
        body, name=name,
        out_shape=(pltpu.SemaphoreType.DMA((3 * nb,)), pltpu.SemaphoreType.DMA((3 * nb,)), *thru,
                   jax.ShapeDtypeStruct((8, 128), F32)),
        in_specs=[_HBM] * (2 * nb) + [_ANY] * len(after),
        out_specs=(_SEM, _SEM, *[_HBM] * (2 * nb), pl.BlockSpec(memory_space=pltpu.VMEM)),
        input_output_aliases={i: 2 + i for i in range(2 * nb)},
        compiler_params=pltpu.CompilerParams(has_side_effects=pltpu.SideEffectType.DATAFLOW_SIDE_EFFECTING),
    )(*[_hbm(a) for a in (*sums, *lands)], *after)
    return (res[0], res[1], res[2:2 + nb], res[2 + nb:2 + 2 * nb]), res[-1]


def reduce_wait(send_sems, recv_sems, sums, lands, layout, after, name):
    nb = len(sums)

    def body(*refs):
        src, recv = refs[:nb], refs[nb:2 * nb]
        ssem, rsem = refs[2 * nb:2 * nb + 2]
        for cp in _reduce_copies(src, recv, layout, ssem, rsem):
            cp.wait_send()
            cp.wait_recv()

    thru = [pltpu.HBM(a.shape, a.dtype) for a in (*sums, *lands)]
    res = pl.pallas_call(
        body, name=name, out_shape=thru,
        in_specs=[_HBM] * (2 * nb) + [_SEM, _SEM, _ANY], out_specs=[_HBM] * (2 * nb),
        input_output_aliases={i: i for i in range(2 * nb)},
        compiler_params=pltpu.CompilerParams(has_side_effects=pltpu.SideEffectType.DATAFLOW_SIDE_EFFECTING),
    )(*sums, *lands, send_sems, recv_sems, after)
    return res[nb:]


def allgather8(block, name):
    m_per, n = block.shape

    def body(x_ref, out_ref, send_sems, recv_sems, local_sem):
        x, y, c = _place()
        me, sibling = (x, y, c), (x, y, 1 - c)
        chips = _other_chips(x, y)

        def rows(px, py, pc):
            return out_ref.at[pl.ds((4 * px + 2 * py + pc) * m_per, m_per), :]

        def copy(k, blk, to, src=None):
            return pltpu.make_async_remote_copy(
                src_ref=rows(*blk) if src is None else src, dst_ref=rows(*blk), send_sem=send_sems.at[k],
                recv_sem=recv_sems.at[k], device_id=to, device_id_type=MESH)

        mine = pltpu.make_async_copy(x_ref, rows(*me), local_sem)
        mine.start()
        first = [copy(0, me, sibling, src=x_ref)]
        first += [copy(1 + j, me, (*chip, c), src=x_ref) for j, chip in enumerate(chips)]
        for cp in first:
            cp.start()
        passed = [copy(4 + j, (*chip, c), sibling) for j, chip in enumerate(chips)]
        for j, chip in enumerate(chips):
            copy(1 + j, (*chip, c), me).wait_recv()
            passed[j].start()
        copy(0, sibling, me).wait_recv()
        for j, chip in enumerate(chips):
            copy(4 + j, (*chip, 1 - c), me).wait_recv()
        for cp in first + passed:
            cp.wait_send()
        mine.wait()

    return pl.pallas_call(
        body, name=name, in_specs=[pl.BlockSpec(memory_space=pltpu.VMEM)],
        out_specs=pl.BlockSpec(memory_space=pltpu.VMEM),
        out_shape=jax.ShapeDtypeStruct((8 * m_per, n), block.dtype),
        scratch_shapes=[pltpu.SemaphoreType.DMA((7,)), pltpu.SemaphoreType.DMA((7,)), pltpu.SemaphoreType.DMA],
        compiler_params=pltpu.CompilerParams(vmem_limit_bytes=VMEM_LIMIT),
    )(block)


CONV = ("gdn_conv_w", "ssd_conv_w", "lru_conv_w")
SMALL = ("ada_b", "norm_mix", "gdn_a_log", "gdn_dt_bias", "gdn_norm", "ssd_conv_b", "ssd_a_log", "ssd_dt_bias",
         "ssd_d", "ssd_norm", "lru_conv_b", "lru_w_a", "lru_b_a", "lru_w_x", "lru_b_x", "lru_lambda", "norm_mlp",
         "final_norm")
WEIGHTS = ("ada_w", "ada_b", "norm_mix", "w_in", "gdn_conv_w", "gdn_a_log", "gdn_dt_bias", "gdn_norm", "ssd_conv_w",
           "ssd_conv_b", "ssd_a_log", "ssd_dt_bias", "ssd_d", "ssd_norm", "lru_conv_w", "lru_conv_b", "lru_w_a",
           "lru_b_a", "lru_w_x", "lru_b_x", "lru_lambda", "w_branch", "w_out", "norm_mlp", "w_up", "w_down",
           "final_norm")
PACK_COLS = 1024


def _pack_rows(shape):
    return 8 * -(-math.prod(shape) // (8 * PACK_COLS))


def _pack(arrays, dtype):
    parts = []
    for a in arrays:
        flat = a.reshape(-1).astype(dtype)
        pad = _pack_rows(a.shape) * PACK_COLS - flat.shape[0]
        parts.append((jnp.concatenate([flat, jnp.zeros((pad,), dtype)]) if pad else flat).reshape(-1, PACK_COLS))
    return jnp.concatenate(parts, axis=0)


def _unpack(pack, shapes):
    out, o = [], 0
    for s in shapes:
        r = _pack_rows(s)
        out.append(pack[o:o + r].reshape(-1)[:math.prod(s)].reshape(s))
        o += r
    return out


def _split_w_in(w4):
    w = jnp.concatenate([w4[k] for k in range(4)], axis=1)
    pad = jnp.zeros((w.shape[0], 120), w.dtype)
    gdn = jnp.concatenate([w[:, 0:2056], pad], axis=1)
    ssd = jnp.concatenate([w[:, 2056:2568], w[:, 3080:3592], w[:, 2568:3080], w[:, 3592:3600], pad], axis=1)
    return gdn, ssd, w[:, 3600:4112], w[:, 4112:4624], w[:, 4624:7696]


def _join_w_in(gdn, ssd, lx, lg, gate):
    w = jnp.concatenate([gdn[:, 0:2056], ssd[:, 0:512], ssd[:, 1024:1536], ssd[:, 512:1024], ssd[:, 1536:1544],
                         lx, lg, gate], axis=1)
    return jnp.stack([w[:, k * 1924:(k + 1) * 1924] for k in range(4)])


def _lanes(v, at, width=128):
    return jnp.concatenate([jnp.zeros((at,), F32), v, jnp.zeros((width - at - v.shape[0],), F32)]).reshape(1, width)


def _block_diag(w):
    return (jnp.eye(8, dtype=w.dtype)[:, None, :, None] * w[:, :, None, :]).reshape(512, 512)


def _diag_blocks(w):
    return jnp.stack([w[n * 64:(n + 1) * 64, n * 64:(n + 1) * 64] for n in range(8)])


TOK_TILE = 512
WIDE_TILE = 256


def _layer_params(p, big, l):
    row = lambda v: v.reshape(1, -1)
    b = dict(zip((n for n, _, _ in BIG_LAYOUT), big))
    gdn = (p["gdn_conv_w"][l], _lanes(p["gdn_a_log"][l], 4), _lanes(p["gdn_dt_bias"][l], 4), row(p["gdn_norm"][l]))
    ssd = (p["ssd_conv_w"][l], row(p["ssd_conv_b"][l]), _lanes(p["ssd_a_log"][l], 0), _lanes(p["ssd_dt_bias"][l], 0),
           row(jnp.repeat(p["ssd_d"][l], 64)), row(p["ssd_norm"][l]))
    lru = (p["lru_conv_w"][l], row(p["lru_conv_b"][l]), _block_diag(p["lru_w_a"][l]), row(p["lru_b_a"][l]),
           _block_diag(p["lru_w_x"][l]), row(p["lru_b_x"][l]), row(p["lru_lambda"][l]))
    return dict(gdn=gdn, ssd=ssd, lru=lru, w_in=_split_w_in(b["w_in"]),
                wb=tuple(b["w_branch"][r] for r in range(3)), w_out=b["w_out"], w_up=b["w_up"],
                w_down=b["w_down"], ada_w=b["ada_w"], ada_b=row(p["ada_b"][l]),
                norm_mix=row(p["norm_mix"][l]), norm_mlp=row(p["norm_mlp"][l]))


def _layer_fwd(x, silu_c, lp, l):
    nm = lambda s: f"l{l}_{s}"
    mod = matmul(silu_c, lp["ada_w"], nm("mod"), bias=lp["ada_b"])
    sh1, sc1, gt1, sh2, sc2, gt2 = (mod[0:1, k * D_MODEL:(k + 1) * D_MODEL] for k in range(N_MOD))
    (h,), _ = scan_fwd(norm1_fn, nm("norm1"), TOK_TILE, [x], [lp["norm_mix"], sc1, sh1], [], [(D_MODEL, BF16)])
    w_gdn, w_ssd, w_lx, w_lg, w_gate = lp["w_in"]
    p_gdn = matmul(h, w_gdn, nm("in_gdn"))
    p_ssd = matmul(h, w_ssd, nm("in_ssd"))
    p_lx = matmul(h, w_lx, nm("in_lx"))
    p_lg = matmul(h, w_lg, nm("in_lg"))
    p_gate = matmul(h, w_gate, nm("in_gate"))
    (ya,), sv_gdn = scan_fwd(gdn_fn, nm("gdn"), CHUNK, [p_gdn], lp["gdn"], [(128, 128)] * 4 + [(8, 1536)],
                             [(512, F32)], save_carry=True)
    (yb,), sv_ssd = scan_fwd(ssd_fn, nm("ssd"), CHUNK, [p_ssd], lp["ssd"], [(128, 128)] * 4 + [(8, 1024)],
                             [(512, F32)], save_carry=True)
    (a, u), sv_lru = scan_fwd(lru_in_fn, nm("lru_in"), TOK_TILE, [p_lx], lp["lru"], [(8, 512)],
                              [(512, F32), (512, F32)], save_carry=True)
    hs = linscan_fwd(a, u, nm("lru_scan"))
    (yc,), _ = scan_fwd(lru_out_fn, nm("lru_out"), TOK_TILE, [hs, p_lg], [], [], [(512, F32)])
    (merged,), _ = scan_fwd(merge_fn, nm("merge"), WIDE_TILE, [ya, yb, yc, p_gate], lp["wb"], [], [(D_MODEL, BF16)])
    mix = matmul(merged, lp["w_out"], nm("out"))
    (h2, x1), _ = scan_fwd(resid_norm_fn, nm("norm2"), TOK_TILE, [x, mix], [gt1, lp["norm_mlp"], sc2, sh2], [],
                           [(D_MODEL, BF16), (D_MODEL, F32)])
    up, act = matmul(h2, lp["w_up"], nm("up"), relu2=True)
    dn = matmul(act, lp["w_down"], nm("down"))
    (x2,), _ = scan_fwd(resid_fn, nm("resid"), TOK_TILE, [x1, dn], [gt2], [], [(D_MODEL, F32)])
    saved = dict(x=x, h=h, p_gdn=p_gdn, p_ssd=p_ssd, p_lx=p_lx, p_lg=p_lg, p_gate=p_gate, sv_gdn=sv_gdn,
                 sv_ssd=sv_ssd, sv_lru=sv_lru, a=a, hs=hs, ya=ya, yb=yb, yc=yc, merged=merged, mix=mix, x1=x1,
                 h2=h2, up=up, act=act, dn=dn, mod=(sh1, sc1, gt1, sh2, sc2, gt2))
    return x2, saved


def _layer_bwd(d_x2, lp, sv, l, on_early):
    nm = lambda s: f"l{l}_b_{s}"
    sh1, sc1, gt1, sh2, sc2, gt2 = sv["mod"]
    (d_x1, d_dn), (d_gt2,) = scan_bwd(resid_fn, nm("resid"), TOK_TILE, [sv["x1"], sv["dn"]], [gt2], [], [d_x2], 2, 1,
                                      [F32, BF16])
    d_up = matmul(d_dn, lp["w_down"], nm("down_x"), tb=True, relu2_of=sv["up"], out_dtype=BF16)
    g_w_down = matmul(sv["act"], d_dn, nm("down_w"), ta=True)
    d_h2 = matmul(d_up, lp["w_up"], nm("up_x"), tb=True)
    g_w_up = matmul(sv["h2"], d_up, nm("up_w"), ta=True)
    (d_x, d_mix), (d_gt1, g_norm_mlp, d_sc2, d_sh2) = scan_bwd(
        resid_norm_fn, nm("norm2"), TOK_TILE, [sv["x"], sv["mix"]], [gt1, lp["norm_mlp"], sc2, sh2], [],
        [d_h2, d_x1], 2, 4, [F32, BF16])
    d_merged = matmul(d_mix, lp["w_out"], nm("out_x"), tb=True)
    g_w_out = matmul(sv["merged"], d_mix, nm("out_w"), ta=True)
    (d_ya, d_yb, d_yc, d_pgate), g_wb = scan_bwd(
        merge_fn, nm("merge"), WIDE_TILE, [sv["ya"], sv["yb"], sv["yc"], sv["p_gate"]], lp["wb"], [], [d_merged], 4, 3,
        [F32, F32, F32, BF16])
    early = [jnp.stack(g_wb), g_w_out, g_w_up, g_w_down]
    token = on_early(l, early)
    lru_params = lp["lru"] if token is None else (lp["lru"][0], lp["lru"][1] + token[0, 0], *lp["lru"][2:])
    (d_hs, d_plg), _ = scan_bwd(lru_out_fn, nm("lru_out"), TOK_TILE, [sv["hs"], sv["p_lg"]], [], [], [d_yc], 2, 0,
                                [F32, BF16])
    d_a, d_u = linscan_bwd(sv["a"], sv["hs"], d_hs, nm("lru_scan"))
    (d_plx,), g_lru = scan_bwd(lru_in_fn, nm("lru_in"), TOK_TILE, [sv["p_lx"]], lru_params, sv["sv_lru"],
                               [d_a, d_u], 1, 7, [BF16])
    (d_pssd,), g_ssd = scan_bwd(ssd_fn, nm("ssd"), CHUNK, [sv["p_ssd"]], lp["ssd"], sv["sv_ssd"], [d_yb], 1, 6,
                                [BF16])
    (d_pgdn,), g_gdn = scan_bwd(gdn_fn, nm("gdn"), CHUNK, [sv["p_gdn"]], lp["gdn"], sv["sv_gdn"], [d_ya], 1, 4,
                                [BF16])
    d_h = None
    g_w_in = []
    for tag, dp, w in zip(("gdn", "ssd", "lx", "lg", "gate"), (d_pgdn, d_pssd, d_plx, d_plg, d_pgate), lp["w_in"]):
        d_h = matmul(dp, w, nm("in_x_" + tag), tb=True, add=d_h)
        g_w_in.append(matmul(sv["h"], dp, nm("in_w_" + tag), ta=True))
    (d_x0,), (g_norm_mix, d_sc1, d_sh1) = scan_bwd(norm1_fn, nm("norm1"), TOK_TILE, [sv["x"]],
                                                   [lp["norm_mix"], sc1, sh1], [], [d_h, d_x], 1, 3)
    d_mod = jnp.concatenate([d_sh1, d_sc1, d_gt1, d_sh2, d_sc2, d_gt2], axis=1)
    flat = lambda v: v.reshape(-1)
    grads = dict(
        ada_b=flat(d_mod), norm_mix=flat(g_norm_mix),
        gdn_conv_w=g_gdn[0], gdn_a_log=g_gdn[1][0, 4:8], gdn_dt_bias=g_gdn[2][0, 4:8], gdn_norm=flat(g_gdn[3]),
        ssd_conv_w=g_ssd[0], ssd_conv_b=flat(g_ssd[1]), ssd_a_log=g_ssd[2][0, 0:8], ssd_dt_bias=g_ssd[3][0, 0:8],
        ssd_d=g_ssd[4].reshape(8, 64).sum(axis=1), ssd_norm=flat(g_ssd[5]),
        lru_conv_w=g_lru[0], lru_conv_b=flat(g_lru[1]), lru_w_a=_diag_blocks(g_lru[2]), lru_b_a=flat(g_lru[3]),
        lru_w_x=_diag_blocks(g_lru[4]), lru_b_x=flat(g_lru[5]), lru_lambda=flat(g_lru[6]),
        norm_mlp=flat(g_norm_mlp))
    big = [_join_w_in(*g_w_in), *early]
    return d_x0, grads, big


def local_step(x, c, target, p, big, on_big_grads, on_early):
    c8 = jnp.concatenate([c, jnp.zeros((7, c.shape[1]), F32)], axis=0)
    (silu_c,), _ = scan_fwd(silu_fn, "silu_c", 8, [c8], [], [], [(D_MODEL, F32)])
    lps, saved = [], []
    for l in range(DEPTH):
        lps.append(_layer_params(p, big[l](x), l))
        x, sv = _layer_fwd(x, silu_c, lps[l], l)
        saved.append(sv)
    loss, d_x, g_final = loss_head(x, target, p["final_norm"].reshape(1, -1), "loss_head")
    layer_grads, big_grads = [None] * DEPTH, [None] * DEPTH
    for l in reversed(range(DEPTH)):
        d_x, layer_grads[l], big_grads[l] = _layer_bwd(d_x, lps[l], saved[l], l, on_early)
        if l > 0:
            token = on_big_grads(l, big_grads[l])
            mod = saved[l - 1]["mod"]
            saved[l - 1]["mod"] = (*mod[:5], mod[5] + token[0, 0])
    grads = {k: jnp.stack([layer_grads[l][k] for l in range(DEPTH)]) for k in layer_grads[0]}
    grads["final_norm"] = g_final.reshape(-1)
    return loss, d_x, grads, big_grads, silu_c[0]


def _place_shard(shard, kind, full, chip):
    base = lax.empty(full, shard.dtype)
    if kind == "chip":
        return lax.dynamic_update_index_in_dim(base, shard, chip, axis=0)
    axis = 0 if kind == "row" else len(full) - 1
    return lax.dynamic_update_slice_in_dim(base, shard, chip * shard.shape[axis], axis=axis)


def _adam_nd(g, w, m, v, name):
    three = lambda a: a.reshape(-1, *a.shape[-2:])
    return tuple(r.reshape(w.shape) for r in adamw(three(g), three(w), three(m), three(v), name, copy_g=True))


EARLY_LAYOUT, LAST_LAYOUT = REDUCE_LAYOUT[1:], REDUCE_LAYOUT[:1]


def _core_sums(grads, recvs, layout, place, tag):
    out = []
    for (n, kind, full), g, r in zip(layout, grads, recvs):
        local = r.shape[1:]
        out.append(sum4(g.reshape(-1, full[-1]), r.reshape(3, -1, local[-1]), kind == "col", place,
                        f"reduce_sum_{tag}_{n}"))
    return out


def _reduce_total(sums0, sums1, shapes):
    theirs = swap_d2d(sums0 + sums1, "reduce_swap")
    nb = len(sums0)
    return [add_layers([sums0[n], sums1[n]], [theirs[n], theirs[nb + n]], f"reduce_total_{n}").reshape(2, *s)
            for n, s in enumerate(shapes)]


def _step(w, m, v, x, c, target):
    chip = 2 * lax.axis_index("x") + lax.axis_index("y")
    place = jnp.stack([chip, lax.axis_index("c")]).astype(jnp.int32)
    conv_shapes = [w[n].shape for n in CONV]
    small_shapes = [w[n].shape for n in SMALL]

    shards = [w[n].astype(BF16) for n, _, _ in BIG_LAYOUT]
    big0 = gather_layer([s[0] for s in shards], "gather_l0")
    conv_all = allgather8(_pack([w[n] for n in CONV], F32), "gather_conv").reshape(8, -1, PACK_COLS)
    own1 = [_place_shard(s[1], kind, full, chip) for s, (_, kind, full) in zip(shards, BIG_LAYOUT)]
    in_flight, token = gather_start([s[1] for s in shards], own1, [big0[0], conv_all], "gather_l1_start")
    c = c + token[0, 0]
    big = [lambda _: big0, lambda x_in: gather_wait(*in_flight, x_in, "gather_l1_wait")]
    conv_parts = [_unpack(conv_all[2 * k], conv_shapes) for k in range(4)]
    p = {n: w[n] for n in SMALL}
    for i, n in enumerate(CONV):
        p[n] = jnp.concatenate([conv_parts[k][i] for k in range(4)], axis=2)

    st = {}
    bf16 = lambda arrays: [a.astype(BF16) for a in arrays]

    def on_big_grads(l, grads_l):
        st["g1"] = grads_l
        st["f1"], token = reduce_start(bf16(grads_l), REDUCE_LAYOUT, [], "reduce_l1_start")
        return token

    def on_early(l, early):
        if l != 0:
            return None
        st["r1"] = reduce_wait(*st["f1"], REDUCE_LAYOUT, early[0], "reduce_l1_wait")
        st["fe"], token = reduce_start(bf16(early), EARLY_LAYOUT, [st["r1"][0]], "reduce_l0_start")
        return token

    loss_blk, grad_x, g, big_g, silu_c = local_step(x[0], c, target[0], p, big, on_big_grads, on_early)
    g0 = big_g[0]
    recv_early = reduce_wait(*st["fe"], EARLY_LAYOUT, g0[0], "reduce_l0_wait")
    recv_last = reduce_ici(bf16(g0[:1]), LAST_LAYOUT, "reduce_ici_w_in")
    sums1 = _core_sums(st["g1"], st["r1"], REDUCE_LAYOUT, place, "l1")
    sums0 = (_core_sums(g0[:1], recv_last, LAST_LAYOUT, place, "l0")
             + _core_sums(g0[1:], recv_early, EARLY_LAYOUT, place, "l0"))
    shapes = [_local_shape(kind, full) for _, kind, full in REDUCE_LAYOUT]
    big_g = dict(zip((n for n, _, _ in REDUCE_LAYOUT), _reduce_total(sums0, sums1, shapes)))

    assert SMALL[0] == "ada_b"
    small_pack = _pack([g["ada_b"], silu_c, loss_blk[0, 0:1]] + [g[n] for n in SMALL[1:]] + [g[n] for n in CONV], F32)
    small_all = allgather8(small_pack, "gather_small").reshape(8, -1, PACK_COLS)
    total = _unpack(add8(small_all, "reduce_small"),
                    [small_shapes[0], (D_MODEL,), (1,)] + small_shapes[1:] + [g[n].shape for n in CONV])
    loss = total[2][0]
    small_g = dict(zip(SMALL, [total[0]] + total[3:2 + len(SMALL)]))
    conv_g = {n: lax.dynamic_slice_in_dim(t, chip * w[n].shape[2], w[n].shape[2], axis=2)
              for n, t in zip(CONV, total[2 + len(SMALL):])}

    cols = w["ada_w"].shape[2]
    silu_all = small_all[:, _pack_rows(small_shapes[0]), :]
    big_g["ada_w"] = jnp.stack([
        matmul(silu_all, lax.dynamic_slice_in_dim(small_all[:, N_MOD * l:N_MOD * (l + 1), :].reshape(8, -1),
                                                  chip * cols, cols, axis=1), f"ada_w_grad{l}", ta=True)
        for l in range(DEPTH)])

    grad, delta, new_m, new_v = {}, {}, {}, {}
    for n, _, _ in BIG_LAYOUT:
        delta[n], new_m[n], new_v[n], grad[n] = _adam_nd(big_g[n], w[n], m[n], v[n], "adam_" + n)
    for names, gs, shapes, tag in ((SMALL, small_g, small_shapes, "small"), (CONV, conv_g, conv_shapes, "conv")):
        pk = lambda d: _pack([d[n] for n in names], F32)[None]
        res = adamw(pk(gs), pk(w), pk(m), pk(v), "adam_" + tag)
        for out, r in zip((delta, new_m, new_v), res):
            out.update(zip(names, _unpack(r[0], shapes)))
        grad.update({n: gs[n] for n in names})
    outs = [loss, grad_x[None]]
    for d in (grad, delta, new_m, new_v):
        outs += [d[n] for n in WEIGHTS]
    return tuple(outs)


def kernel(x, c, ada_w, ada_b, norm_mix, w_in, gdn_conv_w, gdn_a_log, gdn_dt_bias, gdn_norm, ssd_conv_w, ssd_conv_b, ssd_a_log, ssd_dt_bias, ssd_d, ssd_norm, lru_conv_w, lru_conv_b, lru_w_a, lru_b_a, lru_w_x, lru_b_x, lru_lambda, w_branch, w_out, norm_mlp, w_up, w_down, final_norm, loss_target, m_ada_w, m_ada_b, m_norm_mix, m_w_in, m_gdn_conv_w, m_gdn_a_log, m_gdn_dt_bias, m_gdn_norm, m_ssd_conv_w, m_ssd_conv_b, m_ssd_a_log, m_ssd_dt_bias, m_ssd_d, m_ssd_norm, m_lru_conv_w, m_lru_conv_b, m_lru_w_a, m_lru_b_a, m_lru_w_x, m_lru_b_x, m_lru_lambda, m_w_branch, m_w_out, m_norm_mlp, m_w_up, m_w_down, m_final_norm, v_ada_w, v_ada_b, v_norm_mix, v_w_in, v_gdn_conv_w, v_gdn_a_log, v_gdn_dt_bias, v_gdn_norm, v_ssd_conv_w, v_ssd_conv_b, v_ssd_a_log, v_ssd_dt_bias, v_ssd_d, v_ssd_norm, v_lru_conv_w, v_lru_conv_b, v_lru_w_a, v_lru_b_a, v_lru_w_x, v_lru_b_x, v_lru_lambda, v_w_branch, v_w_out, v_norm_mlp, v_w_up, v_w_down, v_final_norm):
    given = dict(locals())
    w = {n: given[n] for n in WEIGHTS}
    m = {n: given["m_" + n] for n in WEIGHTS}
    v = {n: given["v_" + n] for n in WEIGHTS}
    return _step(w, m, v, x, c, loss_target)
```

```python
import functools
import math

import jax
import jax.numpy as jnp
from jax import lax
from jax.experimental import pallas as pl
from jax.experimental.pallas import tpu as pltpu

F32 = jnp.float32
BF16 = jnp.bfloat16

D_MODEL = 1024
DEPTH = 2
RMS_EPS = 1e-6
CHUNK = 128
GDN_HEADS = 4
SSD_HEADS = 8
LRU_C = 8.0
D_FF = 4096
N_MOD = 6
W_GDN = 2176
W_SSD = 1664
W_LRU = 512
W_GATE = 3072
ADAM_LR = 0.001
ADAM_B1 = 0.9
ADAM_B2 = 0.999
ADAM_EPS = 1e-08
ADAM_WD = 0.01
ADAM_STEP = 10
VMEM_LIMIT = 56 * 1024 * 1024
MESH = pl.DeviceIdType.MESH


def _dot(a, b, ta, tb):
    dn = (((0 if ta else 1,), (1 if tb else 0,)), ((), ()))
    return lax.dot_general(a.astype(BF16), b.astype(BF16), dn, preferred_element_type=F32)


@functools.partial(jax.custom_vjp, nondiff_argnums=(2, 3))
def mm(a, b, ta, tb):
    return _dot(a, b, ta, tb)


def _mm_fwd(a, b, ta, tb):
    return _dot(a, b, ta, tb), (a, b)


def _mm_bwd(ta, tb, res, g):
    a, b = res
    if not ta and not tb:
        return mm(g, b, False, True), mm(a, g, True, False)
    if not ta and tb:
        return mm(g, b, False, False), mm(g, a, True, False)
    assert ta and not tb
    return mm(b, g, False, True), mm(a, g, False, False)


mm.defvjp(_mm_fwd, _mm_bwd)


def _tri_apply(x, upper):
    t = x.shape[0]
    r = lax.broadcasted_iota(jnp.int32, (t, t), 0)
    c = lax.broadcasted_iota(jnp.int32, (t, t), 1)
    tri = jnp.where((r <= c) if upper else (r >= c), 1.0, 0.0).astype(BF16)
    x1 = x.astype(BF16)
    r1 = x - x1.astype(F32)
    x2 = r1.astype(BF16)
    x3 = (r1 - x2.astype(F32)).astype(BF16)
    d = lambda p: jnp.dot(tri, p, preferred_element_type=F32)
    return (d(x1) + d(x2)) + d(x3)


@jax.custom_vjp
def cumsum_rows(x):
    return _tri_apply(x, False)


cumsum_rows.defvjp(lambda x: (_tri_apply(x, False), None), lambda _, g: (_tri_apply(g, True),))


def _dot_split(a, b):
    a1, b1 = a.astype(BF16), b.astype(BF16)
    a2, b2 = (a - a1.astype(F32)).astype(BF16), (b - b1.astype(F32)).astype(BF16)
    d = lambda p, q: jnp.dot(p, q, preferred_element_type=F32)
    return d(a1, b1) + (d(a1, b2) + d(a2, b1))


def _neumann(ms):
    t = ms[0].shape[0]
    xs = [-m for m in ms]
    qs = [_dot(m, m, False, False) for m in ms]
    n = 2
    while True:
        xs = [x + q + _dot(x, q, False, False) for x, q in zip(xs, qs)]
        n *= 2
        if n >= t:
            break
        qs = [_dot(q, q, False, False) for q in qs]
    rs = [-(x + m + _dot_split(m, x)) for x, m in zip(xs, ms)]
    return [x + r + _dot(x, r, False, False) for x, r in zip(xs, rs)]


@jax.custom_vjp
def tri_solve(ms, rhss):
    return tuple(rhs + _dot(x, rhs, False, False) for x, rhs in zip(_neumann(ms), rhss))


def _tri_solve_fwd(ms, rhss):
    xs = _neumann(ms)
    sols = tuple(rhs + _dot(x, rhs, False, False) for x, rhs in zip(xs, rhss))
    return sols, (tuple(xs), sols)


def _tri_solve_bwd(res, gs):
    xs, sols = res
    d_rhss = tuple(g + _dot(x, g, True, False) for x, g in zip(xs, gs))
    return tuple(-_dot(d, sol, False, True) for d, sol in zip(d_rhss, sols)), d_rhss


tri_solve.defvjp(_tri_solve_fwd, _tri_solve_bwd)


@functools.partial(jax.custom_vjp, nondiff_argnums=(1,))
def split_cols(x, sizes):
    out, o = [], 0
    for s in sizes:
        out.append(x[:, o:o + s])
        o += s
    return tuple(out)


split_cols.defvjp(lambda x, sizes: (split_cols(x, sizes), None),
                  lambda sizes, _, g: (jnp.concatenate(list(g), axis=1),))


@functools.partial(jax.custom_vjp, nondiff_argnums=(1,))
def _last_rows(x, t):
    return x[t - 8:, :]


_last_rows.defvjp(lambda x, t: (_last_rows(x, t), None),
                  lambda t, _, g: (jnp.concatenate([jnp.zeros((t - 8, g.shape[1]), g.dtype), g], axis=0),))


def last8(x):
    return _last_rows(x, x.shape[0])


def _shifted(xp, d, t):
    return (pltpu.roll(xp, d, 0) if d else xp)[8:8 + t, :]


@jax.custom_vjp
def conv4(x, tail, w):
    t = x.shape[0]
    xp = jnp.concatenate([tail, x], axis=0)
    return sum(_shifted(xp, 3 - k, t) * w[k:k + 1, :] for k in range(4))


def _conv4_fwd(x, tail, w):
    return conv4(x, tail, w), (x, tail, w)


def _conv4_bwd(res, g):
    x, tail, w = res
    t = x.shape[0]
    xp = jnp.concatenate([tail, x], axis=0)
    zero8 = jnp.zeros((8, g.shape[1]), g.dtype)
    d_xp = jnp.zeros_like(xp)
    d_w = []
    for k in range(4):
        gk = jnp.concatenate([zero8, g * w[k:k + 1, :]], axis=0)
        d_xp = d_xp + (pltpu.roll(gk, t + 8 - (3 - k), 0) if k < 3 else gk)
        d_w.append(jnp.sum(g * _shifted(xp, 3 - k, t), axis=0, keepdims=True))
    return d_xp[8:, :], d_xp[:8, :], jnp.concatenate(d_w, axis=0)


conv4.defvjp(_conv4_fwd, _conv4_bwd)


def _sigmoid(x):
    return 0.5 * (jnp.tanh(0.5 * x) + 1.0)


def _silu(x):
    return x * _sigmoid(x)


def _softplus(x):
    ax = jnp.where(x > 0, x, -x)
    return jnp.where(x > 0, x, 0.0) + jnp.log(1.0 + jnp.exp(-ax))


def _gelu(x):
    return 0.5 * x * (1.0 + jnp.tanh(math.sqrt(2.0 / math.pi) * (x + 0.044715 * (x * x * x))))


def _expm1(x):
    series = x * (1.0 + x * (0.5 + x * (1.0 / 6.0 + x * (1.0 / 24.0))))
    return jnp.where(jnp.abs(x) < 0.03, series, jnp.exp(x) - 1.0)


def _rms(x, w):
    return x * lax.rsqrt(jnp.mean(x * x, axis=-1, keepdims=True) + RMS_EPS) * w


def _lane_pick(x, j):
    lane = lax.broadcasted_iota(jnp.int32, (1, x.shape[1]), 1)
    return jnp.sum(jnp.where(lane == j, x, 0.0), axis=1, keepdims=True)


def _row_pick(x, j):
    row = lax.broadcasted_iota(jnp.int32, (x.shape[0], 1), 0)
    return jnp.sum(jnp.where(row == j, x, 0.0), axis=0, keepdims=True)


def gdn_fn(carry, seq, params):
    *states, tail = carry
    (tile,) = seq
    conv_w, alog_row, dtb_row, norm_w = params
    t = tile.shape[0]
    qkv_raw, z, sm = split_cols(tile, (1536, 512, 128))
    qkv = _silu(conv4(qkv_raw, tail, conv_w))
    parts = split_cols(qkv, (128,) * 12)
    zs = split_cols(z, (128,) * 4)
    lane = lax.broadcasted_iota(jnp.int32, (1, 128), 1)
    beta_all = _sigmoid(sm)
    g_all = jnp.where((lane >= 4) & (lane < 8), -jnp.exp(alog_row) * _softplus(sm + dtb_row), 0.0)
    gc_all = cumsum_rows(g_all)
    gr_all = gc_all.T
    gl_all = _row_pick(gc_all, t - 1)
    r = lax.broadcasted_iota(jnp.int32, (t, t), 0)
    c = lax.broadcasted_iota(jnp.int32, (t, t), 1)
    heads = range(GDN_HEADS)
    l2 = lambda a: a * lax.rsqrt(jnp.sum(a * a, axis=-1, keepdims=True) + RMS_EPS)
    qn = [l2(parts[h]) * (128.0 ** -0.5) for h in heads]
    kn = [l2(parts[4 + h]) for h in heads]
    beta = [_lane_pick(beta_all, h) for h in heads]
    gc = [_lane_pick(gc_all, 4 + h) for h in heads]
    gl = [_lane_pick(gl_all, 4 + h) for h in heads]
    decay = [jnp.exp(jnp.where(r >= c, gc[h] - _row_pick(gr_all, 4 + h), -1e30)) for h in heads]
    kk = [mm(kn[h], kn[h], False, True) for h in heads]
    qk = [mm(qn[h], kn[h], False, True) for h in heads]
    m = tuple(jnp.where(r > c, beta[h] * kk[h] * decay[h], 0.0) for h in heads)
    eg = [jnp.exp(gc[h]) for h in heads]
    rhs = tuple(jnp.concatenate([beta[h] * parts[8 + h], (beta[h] * eg[h]) * kn[h]], axis=1) for h in heads)
    uw = [split_cols(s, (128, 128)) for s in tri_solve(m, rhs)]
    ws = [mm(uw[h][1], states[h], False, False) for h in heads]
    qs = [mm(qn[h] * eg[h], states[h], False, False) for h in heads]
    v_new = [uw[h][0] - ws[h] for h in heads]
    o = [qs[h] + mm(qk[h] * decay[h], v_new[h], False, False) for h in heads]
    kv = [mm(kn[h] * jnp.exp(gl[h] - gc[h]), v_new[h], True, False) for h in heads]
    new_states = [states[h] * jnp.exp(gl[h]) + kv[h] for h in heads]
    outs = [_rms(o[h], norm_w) * _silu(zs[h]) for h in heads]
    return (*new_states, last8(qkv_raw)), (jnp.concatenate(outs, axis=1),)


def ssd_fn(carry, seq, params):
    *states, tail = carry
    (tile,) = seq
    conv_w, conv_b, alog_row, dtb_row, d_row, norm_w = params
    t = tile.shape[0]
    xbc_raw, z, sm = split_cols(tile, (1024, 512, 128))
    xbc = _silu(conv4(xbc_raw, tail, conv_w) + conv_b)
    x0, x1, x2, x3, b0, b1, c0, c1 = split_cols(xbc, (128,) * 8)
    xs, bs, cs = (x0, x1, x2, x3), (b0, b1), (c0, c1)
    ds = split_cols(d_row, (128,) * 4)
    lane = lax.broadcasted_iota(jnp.int32, (1, 128), 1)
    sub = lax.broadcasted_iota(jnp.int32, (128, 1), 0)
    low = lane < 64
    dt_all = jnp.where(lane < SSD_HEADS, _softplus(sm + dtb_row), 0.0)
    ac_all = cumsum_rows(dt_all * (-jnp.exp(alog_row)))
    ar_all = ac_all.T
    al_all = _row_pick(ac_all, t - 1)
    r = lax.broadcasted_iota(jnp.int32, (t, t), 0)
    c = lax.broadcasted_iota(jnp.int32, (t, t), 1)
    pairs, heads = range(4), range(SSD_HEADS)
    col = [_lane_pick(ac_all, h) for h in heads]
    last = [_lane_pick(al_all, h) for h in heads]
    dt = [_lane_pick(dt_all, h) for h in heads]
    lm = [jnp.exp(jnp.where(r >= c, col[h] - _row_pick(ar_all, h), -1e30)) for h in heads]
    cb = [mm(cs[g], bs[g], False, True) for g in range(2)]
    both = lambda a, b: jnp.where(low, a, b)
    xdt = [xs[p] * both(dt[2 * p], dt[2 * p + 1]) for p in pairs]
    y_off = [mm(cs[p // 2], states[p], False, True) for p in pairs]
    y_lo = [mm(cb[p // 2] * lm[2 * p], jnp.where(low, xdt[p], 0.0), False, False) for p in pairs]
    y_hi = [mm(cb[p // 2] * lm[2 * p + 1], jnp.where(low, 0.0, xdt[p]), False, False) for p in pairs]
    st = [mm(xdt[p] * both(jnp.exp(last[2 * p] - col[2 * p]), jnp.exp(last[2 * p + 1] - col[2 * p + 1])),
             bs[p // 2], True, False) for p in pairs]
    ys = [ds[p] * xs[p] + y_lo[p] + y_hi[p] + y_off[p] * both(jnp.exp(col[2 * p]), jnp.exp(col[2 * p + 1]))
          for p in pairs]
    new_states = [states[p] * jnp.where(sub < 64, jnp.exp(last[2 * p]), jnp.exp(last[2 * p + 1])) + st[p]
                  for p in pairs]
    gz = jnp.concatenate(ys, axis=1) * _silu(z)
    g0, g1 = split_cols(gz, (256, 256))
    n0, n1 = split_cols(norm_w, (256, 256))
    out = jnp.concatenate([_rms(g0, n0), _rms(g1, n1)], axis=1)
    return (*new_states, last8(xbc_raw)), (out,)


def lru_in_fn(carry, seq, params):
    (tail,) = carry
    (x,) = seq
    conv_w, conv_b, w_a, b_a, w_x, b_x, lam = params
    xc = conv4(x, tail, conv_w) + conv_b
    r = _sigmoid(mm(xc, w_a, False, False) + b_a)
    i = _sigmoid(mm(xc, w_x, False, False) + b_x)
    log_a = -LRU_C * r * _softplus(-lam)
    u = jnp.sqrt(-_expm1(2.0 * log_a)) * (i * xc)
    return (last8(x),), (jnp.exp(log_a), u)


def lru_out_fn(carry, seq, params):
    hs, gate = seq
    return (), (hs * _gelu(gate),)


def merge_fn(carry, seq, params):
    ya, yb, yc, gl = seq
    g = split_cols(_sigmoid(gl), (D_MODEL,) * 3)
    merged = sum(g[r] * mm(y, params[r], False, False) for r, y in enumerate((ya, yb, yc)))
    return (), (merged,)


def _adaln(x, w, sc, sh):
    return _rms(x, w) * (1.0 + sc) + sh


def norm1_fn(carry, seq, params):
    (x,) = seq
    return (), (_adaln(x, *params), x)


def resid_norm_fn(carry, seq, params):
    x, mix = seq
    gt, w, sc, sh = params
    x1 = x + gt * mix
    return (), (_adaln(x1, w, sc, sh), x1)


def resid_fn(carry, seq, params):
    x, dn = seq
    (gt,) = params
    return (), (x + gt * dn,)


def silu_fn(carry, seq, params):
    return (), (_silu(seq[0]),)


def _full_spec(a):
    nd = a.ndim
    return pl.BlockSpec(a.shape, lambda i: (0,) * nd)


def _cparams(*sem):
    return pltpu.CompilerParams(dimension_semantics=sem, vmem_limit_bytes=VMEM_LIMIT)


def scan_fwd(fn, name, tile, seqs, params, carry_shapes, outs, save_carry=False):
    rows = seqs[0].shape[0]
    tile = min(tile, rows)
    n = rows // tile
    ns, npar, nc, no = len(seqs), len(params), len(carry_shapes), len(outs)

    def body(*refs):
        seq_refs, refs = refs[:ns], refs[ns:]
        par_refs, refs = refs[:npar], refs[npar:]
        out_refs, refs = refs[:no], refs[no:]
        save_refs, refs = (refs[:nc], refs[nc:]) if save_carry else ((), refs)
        carry_refs = refs

        @pl.when(pl.program_id(0) == 0)
        def _():
            for cr in carry_refs:
                cr[...] = jnp.zeros_like(cr)

        carry = tuple(cr[...] for cr in carry_refs)
        for sr, cv in zip(save_refs, carry):
            sr[0] = cv
        new_carry, res = fn(carry, tuple(r[...].astype(F32) for r in seq_refs),
                            tuple(r[...].astype(F32) for r in par_refs))
        for r, v in zip(out_refs, res):
            r[...] = v.astype(r.dtype)
        for cr, v in zip(carry_refs, new_carry):
            cr[...] = v

    out_shape = [jax.ShapeDtypeStruct((rows, w), dt) for w, dt in outs]
    out_specs = [pl.BlockSpec((tile, w), lambda i: (i, 0)) for w, _ in outs]
    if save_carry:
        out_shape += [jax.ShapeDtypeStruct((n, *s), F32) for s in carry_shapes]
        out_specs += [pl.BlockSpec((1, *s), lambda i: (i, 0, 0)) for s in carry_shapes]
    res = pl.pallas_call(
        body, name=name, grid=(n,),
        in_specs=[pl.BlockSpec((tile, s.shape[1]), lambda i: (i, 0)) for s in seqs] + [_full_spec(p) for p in params],
        out_specs=out_specs, out_shape=out_shape,
        scratch_shapes=[pltpu.VMEM(s, F32) for s in carry_shapes],
        compiler_params=_cparams("arbitrary"),
    )(*seqs, *params)
    return res[:no], res[no:]


def scan_bwd(fn, name, tile, seqs, params, saved, douts, n_dseq, n_dpar, dseq_dtypes=None):
    dseq_dtypes = dseq_dtypes or [F32] * n_dseq
    rows = seqs[0].shape[0]
    tile = min(tile, rows)
    n = rows // tile
    ns, npar, nc, no = len(seqs), len(params), len(saved), len(douts)

    def body(*refs):
        seq_refs, refs = refs[:ns], refs[ns:]
        par_refs, refs = refs[:npar], refs[npar:]
        save_refs, refs = refs[:nc], refs[nc:]
        dout_refs, refs = refs[:no], refs[no:]
        dseq_refs, refs = refs[:n_dseq], refs[n_dseq:]
        dpar_refs, refs = refs[:n_dpar], refs[n_dpar:]
        dcarry_refs = refs

        @pl.when(pl.program_id(0) == 0)
        def _():
            for r in (*dpar_refs, *dcarry_refs):
                r[...] = jnp.zeros_like(r)

        carry = tuple(r[0] for r in save_refs)
        seq = tuple(r[...].astype(F32) for r in seq_refs)
        par = tuple(r[...].astype(F32) for r in par_refs)

        def f(carry, dseq, dpar):
            return fn(carry, (*dseq, *seq[n_dseq:]), (*dpar, *par[n_dpar:]))

        _, vjp = jax.vjp(f, carry, seq[:n_dseq], par[:n_dpar])
        d_carry, d_seq, d_par = vjp((tuple(r[...] for r in dcarry_refs),
                                     tuple(r[...].astype(F32) for r in dout_refs)))
        for r, v in zip(dseq_refs, d_seq):
            r[...] = v.astype(r.dtype)
        for r, v in zip(dpar_refs, d_par):
            r[...] += v
        for r, v in zip(dcarry_refs, d_carry):
            r[...] = v

    rev = lambda i: (n - 1 - i, 0)
    res = pl.pallas_call(
        body, name=name, grid=(n,),
        in_specs=([pl.BlockSpec((tile, s.shape[1]), rev) for s in seqs] + [_full_spec(p) for p in params]
                  + [pl.BlockSpec((1, *s.shape[1:]), lambda i: (n - 1 - i, 0, 0)) for s in saved]
                  + [pl.BlockSpec((tile, d.shape[1]), rev) for d in douts]),
        out_specs=([pl.BlockSpec((tile, s.shape[1]), rev) for s in seqs[:n_dseq]]
                   + [_full_spec(p) for p in params[:n_dpar]]),
        out_shape=([jax.ShapeDtypeStruct((rows, s.shape[1]), dt) for s, dt in zip(seqs[:n_dseq], dseq_dtypes)]
                   + [jax.ShapeDtypeStruct(p.shape, F32) for p in params[:n_dpar]]),
        scratch_shapes=[pltpu.VMEM(s.shape[1:], F32) for s in saved],
        compiler_params=_cparams("arbitrary"),
    )(*seqs, *params, *saved, *douts)
    return res[:n_dseq], res[n_dseq:]


def _tile_of(dim, pref):
    if dim <= pref:
        return dim
    best = max((t for t in range(128, pref + 1, 128) if dim % t == 0), default=None)
    if best is None or (best < 512 and dim <= 2304):
        return dim
    return best


def _row_tile(rows, pref):
    if rows <= pref:
        return rows
    return max(t for t in range(8, pref + 1, 8) if rows % t == 0)


def matmul(a, b, name, ta=False, tb=False, out_dtype=F32, add=None, bias=None, relu2=False, relu2_of=None,
           tm=1024, tn=2048, tk=1024):
    m, k = (a.shape[1], a.shape[0]) if ta else a.shape
    n = b.shape[0] if tb else b.shape[1]
    assert k == (b.shape[1] if tb else b.shape[0])
    tm, tn, tk = _tile_of(m, tm), _tile_of(n, tn), _tile_of(k, tk)
    nm, nn, nk = m // tm, n // tn, k // tk
    assert nk == 1 or (out_dtype == F32 and not relu2 and relu2_of is None)
    dn = (((0 if ta else 1,), (1 if tb else 0,)), ((), ()))
    has_add, has_bias, has_u = add is not None, bias is not None, relu2_of is not None
    n_inner = a.size * a.dtype.itemsize * (nn - 1) >= b.size * b.dtype.itemsize * (nm - 1)
    ij = (lambda g0, g1: (g0, g1)) if n_inner else (lambda g0, g1: (g1, g0))

    def body(*refs):
        a_ref, b_ref, refs = refs[0], refs[1], refs[2:]
        add_ref, refs = (refs[0], refs[1:]) if has_add else (None, refs)
        bias_ref, refs = (refs[0], refs[1:]) if has_bias else (None, refs)
        u_ref, refs = (refs[0], refs[1:]) if has_u else (None, refs)
        o_ref = refs[0]
        r = lax.dot_general(a_ref[...].astype(BF16), b_ref[...].astype(BF16), dn, preferred_element_type=F32)

        def first():
            v = r
            if has_add:
                v = v + add_ref[...]
            if has_bias:
                v = v + bias_ref[...]
            if has_u:
                v = v * (2.0 * jnp.maximum(u_ref[...], 0.0))
            o_ref[...] = v.astype(o_ref.dtype)
            if relu2:
                p = jnp.maximum(v, 0.0)
                refs[1][...] = (p * p).astype(BF16)

        if nk == 1:
            first()
        else:
            pl.when(pl.program_id(2) == 0)(first)

            @pl.when(pl.program_id(2) > 0)
            def _():
                o_ref[...] += r

    def spec(shape, fn):
        return pl.BlockSpec(shape, lambda g0, g1, l: fn(*ij(g0, g1), l))

    a_spec = spec((tk, tm), lambda i, j, l: (l, i)) if ta else spec((tm, tk), lambda i, j, l: (i, l))
    b_spec = spec((tn, tk), lambda i, j, l: (j, l)) if tb else spec((tk, tn), lambda i, j, l: (l, j))
    o_spec = spec((tm, tn), lambda i, j, l: (i, j))
    in_specs, args = [a_spec, b_spec], [a, b]
    if has_add:
        in_specs.append(o_spec)
        args.append(add)
    if has_bias:
        in_specs.append(spec((1, tn), lambda i, j, l: (0, j)))
        args.append(bias)
    if has_u:
        in_specs.append(o_spec)
        args.append(relu2_of)
    out_shape = [jax.ShapeDtypeStruct((m, n), out_dtype)] + ([jax.ShapeDtypeStruct((m, n), BF16)] if relu2 else [])
    res = pl.pallas_call(
        body, name=name, grid=(nm, nn, nk) if n_inner else (nn, nm, nk), in_specs=in_specs,
        out_specs=[o_spec] * len(out_shape), out_shape=out_shape,
        compiler_params=_cparams("parallel", "parallel", "arbitrary"),
    )(*args)
    return res if relu2 else res[0]


LIN_TILE = 512


def linscan_fwd(a, u, name):
    rows, w = a.shape
    tile = min(LIN_TILE, rows)

    def body(a_ref, u_ref, h_ref, hc):
        @pl.when(pl.program_id(0) == 0)
        def _():
            hc[...] = jnp.zeros_like(hc)

        row = lax.broadcasted_iota(jnp.int32, (8, 1), 0)

        def group(k, h_in):
            rows8 = pl.ds(pl.multiple_of(k * 8, 8), 8)
            pa, pu = a_ref[rows8, :], u_ref[rows8, :]
            for d in (1, 2, 4):
                pu = pu + pa * jnp.where(row >= d, pltpu.roll(pu, d, 0), 0.0)
                pa = pa * jnp.where(row >= d, pltpu.roll(pa, d, 0), 1.0)
            h_ref[rows8, :] = pa * h_in + pu
            return h_ref[pl.ds(k * 8 + 7, 1), :]

        hc[...] = lax.fori_loop(0, tile // 8, group, hc[...], unroll=4)

    spec = pl.BlockSpec((tile, w), lambda i: (i, 0))
    return pl.pallas_call(
        body, name=name, grid=(rows // tile,), in_specs=[spec, spec], out_specs=spec,
        out_shape=jax.ShapeDtypeStruct((rows, w), F32), scratch_shapes=[pltpu.VMEM((1, w), F32)],
        compiler_params=_cparams("arbitrary"),
    )(a, u)


def linscan_bwd(a, hs, dh, name):
    rows, w = a.shape
    tile = min(LIN_TILE, rows)
    n = rows // tile
    per = tile // 8

    def body(a_ref, h_ref, hprev_ref, dh_ref, da_ref, du_ref, cc):
        i = pl.program_id(0)

        @pl.when(i == 0)
        def _():
            cc[...] = jnp.zeros_like(cc)

        row = lax.broadcasted_iota(jnp.int32, (8, 1), 0)
        h_before = jnp.where(i == n - 1, 0.0, hprev_ref[7:8, :])

        def group(s, c_in):
            k = per - 1 - s
            rows8 = pl.ds(pl.multiple_of(k * 8, 8), 8)
            av, hv = a_ref[rows8, :], h_ref[rows8, :]
            pb = jnp.where(row < 7, pltpu.roll(av, 7, 0), 1.0)
            pg = dh_ref[rows8, :]
            for d in (1, 2, 4):
                pg = pg + pb * jnp.where(row < 8 - d, pltpu.roll(pg, 8 - d, 0), 0.0)
                pb = pb * jnp.where(row < 8 - d, pltpu.roll(pb, 8 - d, 0), 1.0)
            g = pg + pb * c_in
            du_ref[rows8, :] = g
            h_prev = jnp.where(k == 0, h_before, h_ref[pl.ds(jnp.maximum(k * 8 - 1, 0), 1), :])
            da_ref[rows8, :] = g * jnp.where(row >= 1, pltpu.roll(hv, 1, 0), h_prev)
            return a_ref[pl.ds(k * 8, 1), :] * du_ref[pl.ds(k * 8, 1), :]

        cc[...] = lax.fori_loop(0, per, group, cc[...], unroll=4)

    rev = pl.BlockSpec((tile, w), lambda i: (n - 1 - i, 0))
    prev = pl.BlockSpec((8, w), lambda i: (jnp.maximum((n - 1 - i) * per - 1, 0), 0))
    return pl.pallas_call(
        body, name=name, grid=(n,), in_specs=[rev, rev, prev, rev], out_specs=[rev, rev],
        out_shape=[jax.ShapeDtypeStruct((rows, w), F32)] * 2, scratch_shapes=[pltpu.VMEM((1, w), F32)],
        compiler_params=_cparams("arbitrary"),
    )(a, hs, hs, dh)


def loss_head(x, target, w, name):
    rows, d = x.shape
    tile = min(512, rows)

    def body(x_ref, t_ref, w_ref, loss_ref, dx_ref, dw_ref):
        @pl.when(pl.program_id(0) == 0)
        def _():
            loss_ref[...] = jnp.zeros_like(loss_ref)
            dw_ref[...] = jnp.zeros_like(dw_ref)

        tv = t_ref[...]

        def f(xv, wv):
            e = _rms(xv, wv) - tv
            return 0.5 * jnp.sum(jnp.mean(e * e, axis=-1, keepdims=True), axis=0, keepdims=True)

        val, vjp = jax.vjp(f, x_ref[...], w_ref[...])
        dxv, dwv = vjp(jnp.ones((1, 1), F32))
        loss_ref[...] += jnp.broadcast_to(val, loss_ref.shape)
        dx_ref[...] = dxv
        dw_ref[...] += dwv

    spec = pl.BlockSpec((tile, d), lambda i: (i, 0))
    return pl.pallas_call(
        body, name=name, grid=(rows // tile,), in_specs=[spec, spec, _full_spec(w)],
        out_specs=[pl.BlockSpec((8, 128), lambda i: (0, 0)), spec, _full_spec(w)],
        out_shape=[jax.ShapeDtypeStruct((8, 128), F32), jax.ShapeDtypeStruct((rows, d), F32),
                   jax.ShapeDtypeStruct(w.shape, F32)],
        compiler_params=_cparams("arbitrary"),
    )(x, target, w)


def adamw(g, w, m, v, name, copy_g=False):
    layers, rows, cols = g.shape
    tile = _row_tile(rows, 256)
    n_out = 4 if copy_g else 3

    def body(g_ref, w_ref, m_ref, v_ref, d_ref, nm_ref, nv_ref, *g_out):
        gv = g_ref[...]
        if copy_g:
            g_out[0][...] = gv
        nm = ADAM_B1 * m_ref[...] + (1.0 - ADAM_B1) * gv
        nv = ADAM_B2 * v_ref[...] + (1.0 - ADAM_B2) * (gv * gv)
        m_hat = nm / (1.0 - ADAM_B1 ** ADAM_STEP)
        v_hat = nv / (1.0 - ADAM_B2 ** ADAM_STEP)
        d_ref[...] = -ADAM_LR * (m_hat / (jnp.sqrt(v_hat) + ADAM_EPS) + ADAM_WD * w_ref[...])
        nm_ref[...] = nm
        nv_ref[...] = nv

    spec = pl.BlockSpec((None, tile, cols), lambda l, i: (l, i, 0))
    return pl.pallas_call(
        body, name=name, grid=(layers, rows // tile), in_specs=[spec] * 4, out_specs=[spec] * n_out,
        out_shape=[jax.ShapeDtypeStruct((layers, rows, cols), F32)] * n_out,
        compiler_params=_cparams("parallel", "parallel"),
    )(g, w, m, v)


def add_layers(mine, theirs, name):
    r, c = mine[0].shape
    tile = _row_tile(r, 256)
    nl = len(mine)

    def body(*refs):
        o_ref = refs[-1]
        for l in range(nl):
            o_ref[l] = refs[l][...] + refs[nl + l][...]

    spec = pl.BlockSpec((tile, c), lambda i: (i, 0))
    return pl.pallas_call(
        body, name=name, grid=(r // tile,), in_specs=[spec] * (2 * nl),
        out_specs=pl.BlockSpec((nl, tile, c), lambda i: (0, i, 0)),
        out_shape=jax.ShapeDtypeStruct((nl, r, c), F32), compiler_params=_cparams("parallel"),
    )(*mine, *theirs)


def sum4(own, recv, by_cols, place, name):
    _, r, c = recv.shape
    tile = _row_tile(r, 256)
    nt = r // tile
    own_map = (lambda i, k: (i, k[0])) if by_cols else (lambda i, k: (k[0] * nt + i, 0))

    def body(k_ref, own_ref, recv_ref, o_ref):
        o_ref[...] = ((own_ref[...] + recv_ref[0].astype(F32)) + recv_ref[1].astype(F32)) + recv_ref[2].astype(F32)

    return pl.pallas_call(
        body, name=name,
        grid_spec=pltpu.PrefetchScalarGridSpec(
            num_scalar_prefetch=1, grid=(nt,),
            in_specs=[pl.BlockSpec((tile, c), own_map), pl.BlockSpec((3, tile, c), lambda i, k: (0, i, 0))],
            out_specs=pl.BlockSpec((tile, c), lambda i, k: (i, 0))),
        out_shape=jax.ShapeDtypeStruct((r, c), F32),
        compiler_params=_cparams("arbitrary"),
    )(place, own, recv)


def add8(parts, name):
    _, rows, cols = parts.shape

    def body(p_ref, o_ref):
        acc = p_ref[0]
        for k in range(1, 8):
            acc = acc + p_ref[k]
        o_ref[...] = acc

    return pl.pallas_call(
        body, name=name, in_specs=[pl.BlockSpec(memory_space=pltpu.VMEM)],
        out_specs=pl.BlockSpec(memory_space=pltpu.VMEM),
        out_shape=jax.ShapeDtypeStruct((rows, cols), F32),
        compiler_params=pltpu.CompilerParams(vmem_limit_bytes=VMEM_LIMIT),
    )(parts)


def _place():
    return lax.axis_index("x"), lax.axis_index("y"), lax.axis_index("c")


def _other_chips(x, y):
    return [(1 - x, y), (x, 1 - y), (1 - x, 1 - y)]


_ANY = pl.BlockSpec(memory_space=pl.ANY)


BIG_LAYOUT = (("ada_w", "col", (1024, 6144)), ("w_in", "chip", (4, 1024, 1924)), ("w_branch", "col", (3, 512, 1024)),
              ("w_out", "row", (1024, 1024)), ("w_up", "col", (1024, 4096)), ("w_down", "row", (4096, 1024)))
N_BIG = len(BIG_LAYOUT)
REDUCE_LAYOUT = BIG_LAYOUT[1:]


def _local_shape(kind, full):
    if kind == "col":
        return (*full[:-1], full[-1] // 4)
    if kind == "row":
        return (full[0] // 4, *full[1:])
    return full[1:]


def _window(ref, kind, k, local):
    if kind == "chip":
        return ref.at[k]
    if kind == "row":
        return ref.at[pl.ds(pl.multiple_of(k * local[0], 8), local[0])]
    idx = (slice(None),) * (len(local) - 1) + (pl.ds(pl.multiple_of(k * local[-1], 128), local[-1]),)
    return ref.at[idx]


def _dma_call(body, name, n_in, out_shape, sems, aliases=None):
    return pl.pallas_call(
        body, name=name, in_specs=[_ANY] * n_in, out_specs=[_ANY] * len(out_shape), out_shape=out_shape,
        scratch_shapes=[pltpu.SemaphoreType.DMA((n,)) for n in sems],
        input_output_aliases=aliases or {},
        compiler_params=pltpu.CompilerParams(has_side_effects=True))


def _remote(src, dst, send_sem, recv_sem, to):
    return pltpu.make_async_remote_copy(src_ref=src, dst_ref=dst, send_sem=send_sem, recv_sem=recv_sem,
                                        device_id=to, device_id_type=MESH)


CORE_PARAMS = ((1, 4), (0, 2, 3, 5))


def gather_layer(shards, name):
    locals_ = [_local_shape(kind, full) for _, kind, full in BIG_LAYOUT]

    def body(*refs):
        sh, full, (send_sems, recv_sems, local_sems, pass_send, pass_recv) = (
            refs[:N_BIG], refs[N_BIG:2 * N_BIG], refs[2 * N_BIG:])
        x, y, c = _place()
        me = 2 * x + y
        chips = _other_chips(x, y)
        win = lambda n, k: _window(full[n], BIG_LAYOUT[n][1], k, locals_[n])
        for cc in (0, 1):
            @pl.when(c == cc)
            def _():
                mine, sends = {}, []
                for n in CORE_PARAMS[cc]:
                    mine[n] = pltpu.make_async_copy(sh[n], win(n, me), local_sems.at[n])
                    mine[n].start()
                    for j, chip in enumerate(chips):
                        sends.append(_remote(sh[n], win(n, me), send_sems.at[3 * n + j], recv_sems.at[3 * n + j],
                                             (chip[0], chip[1], c)))
                        sends[-1].start()
                for n in CORE_PARAMS[cc]:
                    for j, chip in enumerate(chips):
                        _remote(sh[n], win(n, 2 * chip[0] + chip[1]), send_sems.at[3 * n + j],
                                recv_sems.at[3 * n + j], (chip[0], chip[1], c)).wait_recv()
                    mine[n].wait()
                    sends.append(_remote(full[n], full[n], pass_send.at[n], pass_recv.at[n], (x, y, 1 - c)))
                    sends[-1].start()
                for n in CORE_PARAMS[1 - cc]:
                    _remote(full[n], full[n], pass_send.at[n], pass_recv.at[n], (x, y, 1 - c)).wait_recv()
                for cp in sends:
                    cp.wait_send()

    out_shape = [jax.ShapeDtypeStruct(full, BF16) for _, _, full in BIG_LAYOUT]
    return _dma_call(body, name, N_BIG, out_shape, (3 * N_BIG, 3 * N_BIG, N_BIG, N_BIG, N_BIG))(*shards)


_HBM = pl.BlockSpec(memory_space=pltpu.HBM)
_SEM = pl.BlockSpec(memory_space=pltpu.SEMAPHORE)


def _hbm(a):
    return pltpu.with_memory_space_constraint(a, pltpu.HBM)


def _gather_copies(sh, full, send_sems, recv_sems):
    locals_ = [_local_shape(kind, f) for _, kind, f in BIG_LAYOUT]
    x, y, c = _place()
    me = 2 * x + y
    pairs = []
    for n in range(N_BIG):
        win = lambda k: _window(full[n], BIG_LAYOUT[n][1], k, locals_[n])
        for j, chip in enumerate(_other_chips(x, y)):
            mk = lambda dst: _remote(sh[n], dst, send_sems.at[3 * n + j], recv_sems.at[3 * n + j],
                                     (chip[0], chip[1], c))
            pairs.append((mk(win(me)), mk(win(2 * chip[0] + chip[1]))))
    return pairs


def gather_start(shards, fulls, after, name):
    def body(*refs):
        sh, full = refs[:N_BIG], refs[N_BIG:2 * N_BIG]
        send_sems, recv_sems = refs[2 * N_BIG + len(after):2 * N_BIG + len(after) + 2]
        for out, _ in _gather_copies(sh, full, send_sems, recv_sems):
            out.start()
        refs[-1][...] = jnp.zeros_like(refs[-1])

    thru = [pltpu.HBM(a.shape, a.dtype) for a in (*shards, *fulls)]
    res = pl.pallas_call(
        body, name=name,
        out_shape=(pltpu.SemaphoreType.DMA((3 * N_BIG,)), pltpu.SemaphoreType.DMA((3 * N_BIG,)), *thru,
                   jax.ShapeDtypeStruct((8, 128), F32)),
        in_specs=[_HBM] * (2 * N_BIG) + [_ANY] * len(after),
        out_specs=(_SEM, _SEM, *[_HBM] * (2 * N_BIG), pl.BlockSpec(memory_space=pltpu.VMEM)),
        input_output_aliases={i: 2 + i for i in range(2 * N_BIG)},
        compiler_params=pltpu.CompilerParams(has_side_effects=pltpu.SideEffectType.DATAFLOW_SIDE_EFFECTING),
    )(*[_hbm(a) for a in (*shards, *fulls)], *after)
    return (res[0], res[1], res[2:2 + N_BIG], res[2 + N_BIG:2 + 2 * N_BIG]), res[-1]


def gather_wait(send_sems, recv_sems, shards, fulls, after, name):
    def body(*refs):
        sh, full = refs[:N_BIG], refs[N_BIG:2 * N_BIG]
        ssem, rsem = refs[2 * N_BIG:2 * N_BIG + 2]
        for out, inc in _gather_copies(sh, full, ssem, rsem):
            out.wait_send()
            inc.wait_recv()

    thru = [pltpu.HBM(a.shape, a.dtype) for a in (*shards, *fulls)]
    res = pl.pallas_call(
        body, name=name, out_shape=thru,
        in_specs=[_HBM] * (2 * N_BIG) + [_SEM, _SEM, _ANY], out_specs=[_HBM] * (2 * N_BIG),
        input_output_aliases={i: i for i in range(2 * N_BIG)},
        compiler_params=pltpu.CompilerParams(has_side_effects=pltpu.SideEffectType.DATAFLOW_SIDE_EFFECTING),
    )(*shards, *fulls, send_sems, recv_sems, after)
    return res[N_BIG:]


def swap_d2d(arrays, name):
    nb = len(arrays)

    def body(*refs):
        src, got, (send_sems, recv_sems) = refs[:nb], refs[nb:2 * nb], refs[2 * nb:]
        x, y, c = _place()
        copies = [_remote(src[n], got[n], send_sems.at[n], recv_sems.at[n], (x, y, 1 - c)) for n in range(nb)]
        for cp in copies:
            cp.start()
        for cp in copies:
            cp.wait_recv()
        for cp in copies:
            cp.wait_send()

    out_shape = [jax.ShapeDtypeStruct(a.shape, a.dtype) for a in arrays]
    return _dma_call(body, name, nb, out_shape, (nb, nb))(*arrays)


def _reduce_copies(src, recv, layout, send_sems, recv_sems):
    locals_ = [_local_shape(kind, full) for _, kind, full in layout]
    x, y, c = _place()
    copies = []
    for n in range(len(src)):
        for j, chip in enumerate(_other_chips(x, y)):
            copies.append(_remote(_window(src[n], layout[n][1], 2 * chip[0] + chip[1], locals_[n]), recv[n].at[j],
                                  send_sems.at[3 * n + j], recv_sems.at[3 * n + j], (chip[0], chip[1], c)))
    return copies


def _recv_shapes(sums, layout):
    return [(3, *_local_shape(kind, full)) for _, kind, full in layout]


def reduce_ici(sums, layout, name):
    nb = len(sums)

    def body(*refs):
        src, recv, (send_sems, recv_sems) = refs[:nb], refs[nb:2 * nb], refs[2 * nb:]
        copies = _reduce_copies(src, recv, layout, send_sems, recv_sems)
        for cp in copies:
            cp.start()
        for cp in copies:
            cp.wait_recv()
        for cp in copies:
            cp.wait_send()

    out_shape = [jax.ShapeDtypeStruct(s, a.dtype) for s, a in zip(_recv_shapes(sums, layout), sums)]
    return _dma_call(body, name, nb, out_shape, (3 * nb, 3 * nb))(*sums)


def reduce_start(sums, layout, after, name):
    nb = len(sums)
    lands = [lax.empty(s, a.dtype) for s, a in zip(_recv_shapes(sums, layout), sums)]

    def body(*refs):
        src, recv = refs[:nb], refs[nb:2 * nb]
        send_sems, recv_sems = refs[2 * nb + len(after):2 * nb + len(after) + 2]
        for cp in _reduce_copies(src, recv, layout, send_sems, recv_sems):
            cp.start()
        refs[-1][...] = jnp.zeros_like(refs[-1])

    thru = [pltpu.HBM(a.shape, a.dtype) for a in (*sums, *lands)]
    res = pl.pallas_call(
        body, name=name,
        out_shape=(pltpu.SemaphoreType.DMA((3 * nb,)), pltpu.SemaphoreType.DMA((3 * nb,)), *thru,
                   jax.ShapeDtypeStruct((8, 128), F32)),
        in_specs=[_HBM] * (2 * nb) + [_ANY] * len(after),
        out_specs=(_SEM, _SEM, *[_HBM] * (2 * nb), pl.BlockSpec(memory_space=pltpu.VMEM)),
        input_output_aliases={i: 2 + i for i in range(2 * nb)},
        compiler_params=pltpu.CompilerParams(has_side_effects=pltpu.SideEffectType.DATAFLOW_SIDE_EFFECTING),
    )(*[_hbm(a) for a in (*sums, *lands)], *after)
    return (res[0], res[1], res[2:2 + nb], res[2 + nb:2 + 2 * nb]), res[-1]


def reduce_wait(send_sems, recv_sems, sums, lands, layout, after, name):
    nb = len(sums)

    def body(*refs):
        src, recv = refs[:nb], refs[nb:2 * nb]
        ssem, rsem = refs[2 * nb:2 * nb + 2]
        for cp in _reduce_copies(src, recv, layout, ssem, rsem):
            cp.wait_send()
            cp.wait_recv()

    thru = [pltpu.HBM(a.shape, a.dtype) for a in (*sums, *lands)]
    res = pl.pallas_call(
        body, name=name, out_shape=thru,
        in_specs=[_HBM] * (2 * nb) + [_SEM, _SEM, _ANY], out_specs=[_HBM] * (2 * nb),
        input_output_aliases={i: i for i in range(2 * nb)},
        compiler_params=pltpu.CompilerParams(has_side_effects=pltpu.SideEffectType.DATAFLOW_SIDE_EFFECTING),
    )(*sums, *lands, send_sems, recv_sems, after)
    return res[nb:]


def allgather8(block, name):
    m_per, n = block.shape

    def body(x_ref, out_ref, send_sems, recv_sems, local_sem):
        x, y, c = _place()
        me, sibling = (x, y, c), (x, y, 1 - c)
        chips = _other_chips(x, y)

        def rows(px, py, pc):
            return out_ref.at[pl.ds((4 * px + 2 * py + pc) * m_per, m_per), :]

        def copy(k, blk, to, src=None):
            return pltpu.make_async_remote_copy(
                src_ref=rows(*blk) if src is None else src, dst_ref=rows(*blk), send_sem=send_sems.at[k],
                recv_sem=recv_sems.at[k], device_id=to, device_id_type=MESH)

        mine = pltpu.make_async_copy(x_ref, rows(*me), local_sem)
        mine.start()
        first = [copy(0, me, sibling, src=x_ref)]
        first += [copy(1 + j, me, (*chip, c), src=x_ref) for j, chip in enumerate(chips)]
        for cp in first:
            cp.start()
        passed = [copy(4 + j, (*chip, c), sibling) for j, chip in enumerate(chips)]
        for j, chip in enumerate(chips):
            copy(1 + j, (*chip, c), me).wait_recv()
            passed[j].start()
        copy(0, sibling, me).wait_recv()
        for j, chip in enumerate(chips):
            copy(4 + j, (*chip, 1 - c), me).wait_recv()
        for cp in first + passed:
            cp.wait_send()
        mine.wait()

    return pl.pallas_call(
        body, name=name, in_specs=[pl.BlockSpec(memory_space=pltpu.VMEM)],
        out_specs=pl.BlockSpec(memory_space=pltpu.VMEM),
        out_shape=jax.ShapeDtypeStruct((8 * m_per, n), block.dtype),
        scratch_shapes=[pltpu.SemaphoreType.DMA((7,)), pltpu.SemaphoreType.DMA((7,)), pltpu.SemaphoreType.DMA],
        compiler_params=pltpu.CompilerParams(vmem_limit_bytes=VMEM_LIMIT),
    )(block)


CONV = ("gdn_conv_w", "ssd_conv_w", "lru_conv_w")
SMALL = ("ada_b", "norm_mix", "gdn_a_log", "gdn_dt_bias", "gdn_norm", "ssd_conv_b", "ssd_a_log", "ssd_dt_bias",
         "ssd_d", "ssd_norm", "lru_conv_b", "lru_w_a", "lru_b_a", "lru_w_x", "lru_b_x", "lru_lambda", "norm_mlp",
         "final_norm")
WEIGHTS = ("ada_w", "ada_b", "norm_mix", "w_in", "gdn_conv_w", "gdn_a_log", "gdn_dt_bias", "gdn_norm", "ssd_conv_w",
           "ssd_conv_b", "ssd_a_log", "ssd_dt_bias", "ssd_d", "ssd_norm", "lru_conv_w", "lru_conv_b", "lru_w_a",
           "lru_b_a", "lru_w_x", "lru_b_x", "lru_lambda", "w_branch", "w_out", "norm_mlp", "w_up", "w_down",
           "final_norm")
PACK_COLS = 1024


def _pack_rows(shape):
    return 8 * -(-math.prod(shape) // (8 * PACK_COLS))


def _pack(arrays, dtype):
    parts = []
    for a in arrays:
        flat = a.reshape(-1).astype(dtype)
        pad = _pack_rows(a.shape) * PACK_COLS - flat.shape[0]
        parts.append((jnp.concatenate([flat, jnp.zeros((pad,), dtype)]) if pad else flat).reshape(-1, PACK_COLS))
    return jnp.concatenate(parts, axis=0)


def _unpack(pack, shapes):
    out, o = [], 0
    for s in shapes:
        r = _pack_rows(s)
        out.append(pack[o:o + r].reshape(-1)[:math.prod(s)].reshape(s))
        o += r
    return out


def _split_w_in(w4):
    w = jnp.concatenate([w4[k] for k in range(4)], axis=1)
    pad = jnp.zeros((w.shape[0], 120), w.dtype)
    gdn = jnp.concatenate([w[:, 0:2056], pad], axis=1)
    ssd = jnp.concatenate([w[:, 2056:2568], w[:, 3080:3592], w[:, 2568:3080], w[:, 3592:3600], pad], axis=1)
    return gdn, ssd, w[:, 3600:4112], w[:, 4112:4624], w[:, 4624:7696]


def _join_w_in(gdn, ssd, lx, lg, gate):
    w = jnp.concatenate([gdn[:, 0:2056], ssd[:, 0:512], ssd[:, 1024:1536], ssd[:, 512:1024], ssd[:, 1536:1544],
                         lx, lg, gate], axis=1)
    return jnp.stack([w[:, k * 1924:(k + 1) * 1924] for k in range(4)])


def _lanes(v, at, width=128):
    return jnp.concatenate([jnp.zeros((at,), F32), v, jnp.zeros((width - at - v.shape[0],), F32)]).reshape(1, width)


def _block_diag(w):
    return (jnp.eye(8, dtype=w.dtype)[:, None, :, None] * w[:, :, None, :]).reshape(512, 512)


def _diag_blocks(w):
    return jnp.stack([w[n * 64:(n + 1) * 64, n * 64:(n + 1) * 64] for n in range(8)])


TOK_TILE = 512
WIDE_TILE = 256


def _layer_params(p, big, l):
    row = lambda v: v.reshape(1, -1)
    b = dict(zip((n for n, _, _ in BIG_LAYOUT), big))
    gdn = (p["gdn_conv_w"][l], _lanes(p["gdn_a_log"][l], 4), _lanes(p["gdn_dt_bias"][l], 4), row(p["gdn_norm"][l]))
    ssd = (p["ssd_conv_w"][l], row(p["ssd_conv_b"][l]), _lanes(p["ssd_a_log"][l], 0), _lanes(p["ssd_dt_bias"][l], 0),
           row(jnp.repeat(p["ssd_d"][l], 64)), row(p["ssd_norm"][l]))
    lru = (p["lru_conv_w"][l], row(p["lru_conv_b"][l]), _block_diag(p["lru_w_a"][l]), row(p["lru_b_a"][l]),
           _block_diag(p["lru_w_x"][l]), row(p["lru_b_x"][l]), row(p["lru_lambda"][l]))
    return dict(gdn=gdn, ssd=ssd, lru=lru, w_in=_split_w_in(b["w_in"]),
                wb=tuple(b["w_branch"][r] for r in range(3)), w_out=b["w_out"], w_up=b["w_up"],
                w_down=b["w_down"], ada_w=b["ada_w"], ada_b=row(p["ada_b"][l]),
                norm_mix=row(p["norm_mix"][l]), norm_mlp=row(p["norm_mlp"][l]))


def _layer_fwd(x, silu_c, lp, l):
    nm = lambda s: f"l{l}_{s}"
    mod = matmul(silu_c, lp["ada_w"], nm("mod"), bias=lp["ada_b"])
    sh1, sc1, gt1, sh2, sc2, gt2 = (mod[0:1, k * D_MODEL:(k + 1) * D_MODEL] for k in range(N_MOD))
    (h,), _ = scan_fwd(norm1_fn, nm("norm1"), TOK_TILE, [x], [lp["norm_mix"], sc1, sh1], [], [(D_MODEL, BF16)])
    w_gdn, w_ssd, w_lx, w_lg, w_gate = lp["w_in"]
    p_gdn = matmul(h, w_gdn, nm("in_gdn"))
    p_ssd = matmul(h, w_ssd, nm("in_ssd"))
    p_lx = matmul(h, w_lx, nm("in_lx"))
    p_lg = matmul(h, w_lg, nm("in_lg"))
    p_gate = matmul(h, w_gate, nm("in_gate"))
    (ya,), sv_gdn = scan_fwd(gdn_fn, nm("gdn"), CHUNK, [p_gdn], lp["gdn"], [(128, 128)] * 4 + [(8, 1536)],
                             [(512, F32)], save_carry=True)
    (yb,), sv_ssd = scan_fwd(ssd_fn, nm("ssd"), CHUNK, [p_ssd], lp["ssd"], [(128, 128)] * 4 + [(8, 1024)],
                             [(512, F32)], save_carry=True)
    (a, u), sv_lru = scan_fwd(lru_in_fn, nm("lru_in"), TOK_TILE, [p_lx], lp["lru"], [(8, 512)],
                              [(512, F32), (512, F32)], save_carry=True)
    hs = linscan_fwd(a, u, nm("lru_scan"))
    (yc,), _ = scan_fwd(lru_out_fn, nm("lru_out"), TOK_TILE, [hs, p_lg], [], [], [(512, F32)])
    (merged,), _ = scan_fwd(merge_fn, nm("merge"), WIDE_TILE, [ya, yb, yc, p_gate], lp["wb"], [], [(D_MODEL, BF16)])
    mix = matmul(merged, lp["w_out"], nm("out"))
    (h2, x1), _ = scan_fwd(resid_norm_fn, nm("norm2"), TOK_TILE, [x, mix], [gt1, lp["norm_mlp"], sc2, sh2], [],
                           [(D_MODEL, BF16), (D_MODEL, F32)])
    up, act = matmul(h2, lp["w_up"], nm("up"), relu2=True)
    dn = matmul(act, lp["w_down"], nm("down"))
    (x2,), _ = scan_fwd(resid_fn, nm("resid"), TOK_TILE, [x1, dn], [gt2], [], [(D_MODEL, F32)])
    saved = dict(x=x, h=h, p_gdn=p_gdn, p_ssd=p_ssd, p_lx=p_lx, p_lg=p_lg, p_gate=p_gate, sv_gdn=sv_gdn,
                 sv_ssd=sv_ssd, sv_lru=sv_lru, a=a, hs=hs, ya=ya, yb=yb, yc=yc, merged=merged, mix=mix, x1=x1,
                 h2=h2, up=up, act=act, dn=dn, mod=(sh1, sc1, gt1, sh2, sc2, gt2))
    return x2, saved


def _layer_bwd(d_x2, lp, sv, l, on_early, on_last):
    nm = lambda s: f"l{l}_b_{s}"
    sh1, sc1, gt1, sh2, sc2, gt2 = sv["mod"]
    (d_x1, d_dn), (d_gt2,) = scan_bwd(resid_fn, nm("resid"), TOK_TILE, [sv["x1"], sv["dn"]], [gt2], [], [d_x2], 2, 1,
                                      [F32, BF16])
    d_up = matmul(d_dn, lp["w_down"], nm("down_x"), tb=True, relu2_of=sv["up"], out_dtype=BF16)
    g_w_down = matmul(sv["act"], d_dn, nm("down_w"), ta=True)
    d_h2 = matmul(d_up, lp["w_up"], nm("up_x"), tb=True)
    g_w_up = matmul(sv["h2"], d_up, nm("up_w"), ta=True)
    (d_x, d_mix), (d_gt1, g_norm_mlp, d_sc2, d_sh2) = scan_bwd(
        resid_norm_fn, nm("norm2"), TOK_TILE, [sv["x"], sv["mix"]], [gt1, lp["norm_mlp"], sc2, sh2], [],
        [d_h2, d_x1], 2, 4, [F32, BF16])
    d_merged = matmul(d_mix, lp["w_out"], nm("out_x"), tb=True)
    g_w_out = matmul(sv["merged"], d_mix, nm("out_w"), ta=True)
    (d_ya, d_yb, d_yc, d_pgate), g_wb = scan_bwd(
        merge_fn, nm("merge"), WIDE_TILE, [sv["ya"], sv["yb"], sv["yc"], sv["p_gate"]], lp["wb"], [], [d_merged], 4, 3,
        [F32, F32, F32, BF16])
    early = [jnp.stack(g_wb), g_w_out, g_w_up, g_w_down]
    token = on_early(l, early)
    lru_params = lp["lru"] if token is None else (lp["lru"][0], lp["lru"][1] + token[0, 0], *lp["lru"][2:])
    (d_hs, d_plg), _ = scan_bwd(lru_out_fn, nm("lru_out"), TOK_TILE, [sv["hs"], sv["p_lg"]], [], [], [d_yc], 2, 0,
                                [F32, BF16])
    d_a, d_u = linscan_bwd(sv["a"], sv["hs"], d_hs, nm("lru_scan"))
    (d_plx,), g_lru = scan_bwd(lru_in_fn, nm("lru_in"), TOK_TILE, [sv["p_lx"]], lru_params, sv["sv_lru"],
                               [d_a, d_u], 1, 7, [BF16])
    (d_pssd,), g_ssd = scan_bwd(ssd_fn, nm("ssd"), CHUNK, [sv["p_ssd"]], lp["ssd"], sv["sv_ssd"], [d_yb], 1, 6,
                                [BF16])
    (d_pgdn,), g_gdn = scan_bwd(gdn_fn, nm("gdn"), CHUNK, [sv["p_gdn"]], lp["gdn"], sv["sv_gdn"], [d_ya], 1, 4,
                                [BF16])
    groups = list(zip(("gdn", "ssd", "lx", "lg", "gate"), (d_pgdn, d_pssd, d_plx, d_plg, d_pgate), lp["w_in"]))
    g_w_in = _join_w_in(*[matmul(sv["h"], dp, nm("in_w_" + tag), ta=True) for tag, dp, _ in groups])
    token = on_last(l, g_w_in)
    bias = None if token is None else jnp.zeros((1, D_MODEL), F32) + token[0, 0]
    d_h = None
    for i, (tag, dp, w) in enumerate(groups):
        d_h = matmul(dp, w, nm("in_x_" + tag), tb=True, add=d_h, bias=bias if i == 0 else None)
    (d_x0,), (g_norm_mix, d_sc1, d_sh1) = scan_bwd(norm1_fn, nm("norm1"), TOK_TILE, [sv["x"]],
                                                   [lp["norm_mix"], sc1, sh1], [], [d_h, d_x], 1, 3)
    d_mod = jnp.concatenate([d_sh1, d_sc1, d_gt1, d_sh2, d_sc2, d_gt2], axis=1)
    flat = lambda v: v.reshape(-1)
    grads = dict(
        ada_b=flat(d_mod), norm_mix=flat(g_norm_mix),
        gdn_conv_w=g_gdn[0], gdn_a_log=g_gdn[1][0, 4:8], gdn_dt_bias=g_gdn[2][0, 4:8], gdn_norm=flat(g_gdn[3]),
        ssd_conv_w=g_ssd[0], ssd_conv_b=flat(g_ssd[1]), ssd_a_log=g_ssd[2][0, 0:8], ssd_dt_bias=g_ssd[3][0, 0:8],
        ssd_d=g_ssd[4].reshape(8, 64).sum(axis=1), ssd_norm=flat(g_ssd[5]),
        lru_conv_w=g_lru[0], lru_conv_b=flat(g_lru[1]), lru_w_a=_diag_blocks(g_lru[2]), lru_b_a=flat(g_lru[3]),
        lru_w_x=_diag_blocks(g_lru[4]), lru_b_x=flat(g_lru[5]), lru_lambda=flat(g_lru[6]),
        norm_mlp=flat(g_norm_mlp))
    big = [g_w_in, *early]
    return d_x0, grads, big


def local_step(x, c, target, p, big, on_big_grads, on_early, on_last):
    c8 = jnp.concatenate([c, jnp.zeros((7, c.shape[1]), F32)], axis=0)
    (silu_c,), _ = scan_fwd(silu_fn, "silu_c", 8, [c8], [], [], [(D_MODEL, F32)])
    lps, saved = [], []
    for l in range(DEPTH):
        lps.append(_layer_params(p, big[l](x), l))
        x, sv = _layer_fwd(x, silu_c, lps[l], l)
        saved.append(sv)
    loss, d_x, g_final = loss_head(x, target, p["final_norm"].reshape(1, -1), "loss_head")
    layer_grads, big_grads = [None] * DEPTH, [None] * DEPTH
    for l in reversed(range(DEPTH)):
        d_x, layer_grads[l], big_grads[l] = _layer_bwd(d_x, lps[l], saved[l], l, on_early, on_last)
        if l > 0:
            token = on_big_grads(l, big_grads[l])
            mod = saved[l - 1]["mod"]
            saved[l - 1]["mod"] = (*mod[:5], mod[5] + token[0, 0])
    grads = {k: jnp.stack([layer_grads[l][k] for l in range(DEPTH)]) for k in layer_grads[0]}
    grads["final_norm"] = g_final.reshape(-1)
    return loss, d_x, grads, big_grads, silu_c[0]


def _place_shard(shard, kind, full, chip):
    base = lax.empty(full, shard.dtype)
    if kind == "chip":
        return lax.dynamic_update_index_in_dim(base, shard, chip, axis=0)
    axis = 0 if kind == "row" else len(full) - 1
    return lax.dynamic_update_slice_in_dim(base, shard, chip * shard.shape[axis], axis=axis)


def _adam_nd(g, w, m, v, name):
    three = lambda a: a.reshape(-1, *a.shape[-2:])
    return tuple(r.reshape(w.shape) for r in adamw(three(g), three(w), three(m), three(v), name, copy_g=True))


EARLY_LAYOUT, LAST_LAYOUT = REDUCE_LAYOUT[1:], REDUCE_LAYOUT[:1]


def _core_sums(grads, recvs, layout, place, tag):
    out = []
    for (n, kind, full), g, r in zip(layout, grads, recvs):
        local = r.shape[1:]
        out.append(sum4(g.reshape(-1, full[-1]), r.reshape(3, -1, local[-1]), kind == "col", place,
                        f"reduce_sum_{tag}_{n}"))
    return out


def _reduce_total(sums0, sums1, shapes):
    theirs = swap_d2d(sums0 + sums1, "reduce_swap")
    nb = len(sums0)
    return [add_layers([sums0[n], sums1[n]], [theirs[n], theirs[nb + n]], f"reduce_total_{n}").reshape(2, *s)
            for n, s in enumerate(shapes)]


def _step(w, m, v, x, c, target):
    chip = 2 * lax.axis_index("x") + lax.axis_index("y")
    place = jnp.stack([chip, lax.axis_index("c")]).astype(jnp.int32)
    conv_shapes = [w[n].shape for n in CONV]
    small_shapes = [w[n].shape for n in SMALL]

    shards = [w[n].astype(BF16) for n, _, _ in BIG_LAYOUT]
    big0 = gather_layer([s[0] for s in shards], "gather_l0")
    conv_all = allgather8(_pack([w[n] for n in CONV], F32), "gather_conv").reshape(8, -1, PACK_COLS)
    own1 = [_place_shard(s[1], kind, full, chip) for s, (_, kind, full) in zip(shards, BIG_LAYOUT)]
    in_flight, token = gather_start([s[1] for s in shards], own1, [big0[0], conv_all], "gather_l1_start")
    c = c + token[0, 0]
    big = [lambda _: big0, lambda x_in: gather_wait(*in_flight, x_in, "gather_l1_wait")]
    conv_parts = [_unpack(conv_all[2 * k], conv_shapes) for k in range(4)]
    p = {n: w[n] for n in SMALL}
    for i, n in enumerate(CONV):
        p[n] = jnp.concatenate([conv_parts[k][i] for k in range(4)], axis=2)

    st = {}
    bf16 = lambda arrays: [a.astype(BF16) for a in arrays]

    def on_big_grads(l, grads_l):
        st["g1"] = grads_l
        st["f1"], token = reduce_start(bf16(grads_l), REDUCE_LAYOUT, [], "reduce_l1_start")
        return token

    def on_early(l, early):
        if l != 0:
            return None
        st["r1"] = reduce_wait(*st["f1"], REDUCE_LAYOUT, early[0], "reduce_l1_wait")
        st["fe"], token = reduce_start(bf16(early), EARLY_LAYOUT, [st["r1"][0]], "reduce_l0_start")
        return token

    def on_last(l, g_w_in):
        if l != 0:
            return None
        st["re"] = reduce_wait(*st["fe"], EARLY_LAYOUT, g_w_in, "reduce_l0_wait")
        st["fl"], token = reduce_start(bf16([g_w_in]), LAST_LAYOUT, [st["re"][0]], "reduce_w_in_start")
        return token

    loss_blk, grad_x, g, big_g, silu_c = local_step(x[0], c, target[0], p, big, on_big_grads, on_early, on_last)
    g0 = big_g[0]
    recv_last = reduce_wait(*st["fl"], LAST_LAYOUT, grad_x, "reduce_w_in_wait")
    sums1 = _core_sums(st["g1"], st["r1"], REDUCE_LAYOUT, place, "l1")
    sums0 = (_core_sums(g0[:1], recv_last, LAST_LAYOUT, place, "l0")
             + _core_sums(g0[1:], st["re"], EARLY_LAYOUT, place, "l0"))
    shapes = [_local_shape(kind, full) for _, kind, full in REDUCE_LAYOUT]
    big_g = dict(zip((n for n, _, _ in REDUCE_LAYOUT), _reduce_total(sums0, sums1, shapes)))

    assert SMALL[0] == "ada_b"
    small_pack = _pack([g["ada_b"], silu_c, loss_blk[0, 0:1]] + [g[n] for n in SMALL[1:]] + [g[n] for n in CONV], F32)
    small_all = allgather8(small_pack, "gather_small").reshape(8, -1, PACK_COLS)
    total = _unpack(add8(small_all, "reduce_small"),
                    [small_shapes[0], (D_MODEL,), (1,)] + small_shapes[1:] + [g[n].shape for n in CONV])
    loss = total[2][0]
    small_g = dict(zip(SMALL, [total[0]] + total[3:2 + len(SMALL)]))
    conv_g = {n: lax.dynamic_slice_in_dim(t, chip * w[n].shape[2], w[n].shape[2], axis=2)
              for n, t in zip(CONV, total[2 + len(SMALL):])}

    cols = w["ada_w"].shape[2]
    silu_all = small_all[:, _pack_rows(small_shapes[0]), :]
    big_g["ada_w"] = jnp.stack([
        matmul(silu_all, lax.dynamic_slice_in_dim(small_all[:, N_MOD * l:N_MOD * (l + 1), :].reshape(8, -1),
                                                  chip * cols, cols, axis=1), f"ada_w_grad{l}", ta=True)
        for l in range(DEPTH)])

    grad, delta, new_m, new_v = {}, {}, {}, {}
    for n, _, _ in BIG_LAYOUT:
        delta[n], new_m[n], new_v[n], grad[n] = _adam_nd(big_g[n], w[n], m[n], v[n], "adam_" + n)
    for names, gs, shapes, tag in ((SMALL, small_g, small_shapes, "small"), (CONV, conv_g, conv_shapes, "conv")):
        pk = lambda d: _pack([d[n] for n in names], F32)[None]
        res = adamw(pk(gs), pk(w), pk(m), pk(v), "adam_" + tag)
        for out, r in zip((delta, new_m, new_v), res):
            out.update(zip(names, _unpack(r[0], shapes)))
        grad.update({n: gs[n] for n in names})
    outs = [loss, grad_x[None]]
    for d in (grad, delta, new_m, new_v):
        outs += [d[n] for n in WEIGHTS]
    return tuple(outs)


def kernel(x, c, ada_w, ada_b, norm_mix, w_in, gdn_conv_w, gdn_a_log, gdn_dt_bias, gdn_norm, ssd_conv_w, ssd_conv_b, ssd_a_log, ssd_dt_bias, ssd_d, ssd_norm, lru_conv_w, lru_conv_b, lru_w_a, lru_b_a, lru_w_x, lru_b_x, lru_lambda, w_branch, w_out, norm_mlp, w_up, w_down, final_norm, loss_target, m_ada_w, m_ada_b, m_norm_mix, m_w_in, m_gdn_conv_w, m_gdn_a_log, m_gdn_dt_bias, m_gdn_norm, m_ssd_conv_w, m_ssd_conv_b, m_ssd_a_log, m_ssd_dt_bias, m_ssd_d, m_ssd_norm, m_lru_conv_w, m_lru_conv_b, m_lru_w_a, m_lru_b_a, m_lru_w_x, m_lru_b_x, m_lru_lambda, m_w_branch, m_w_out, m_norm_mlp, m_w_up, m_w_down, m_final_norm, v_ada_w, v_ada_b, v_norm_mix, v_w_in, v_gdn_conv_w, v_gdn_a_log, v_gdn_dt_bias, v_gdn_norm, v_ssd_conv_w, v_ssd_conv_b, v_ssd_a_log, v_ssd_dt_bias, v_ssd_d, v_ssd_norm, v_lru_conv_w, v_lru_conv_b, v_lru_w_a, v_lru_b_a, v_lru_w_x, v_lru_b_x, v_lru_lambda, v_w_branch, v_w_out, v_norm_mlp, v_w_up, v_w_down, v_final_norm):
    given = dict(locals())
    w = {n: given[n] for n in WEIGHTS}
    m = {n: given["m_" + n] for n in WEIGHTS}
    v = {n: given["v_" + n] for n in WEIGHTS}
    return _step(w, m, v, x, c, loss_target)
```

```python
import functools
import math

import jax
import jax.numpy as jnp
from jax import lax
from jax.experimental import pallas as pl
from jax.experimental.pallas import tpu as pltpu

F32 = jnp.float32
BF16 = jnp.bfloat16

D_MODEL = 1024
DEPTH = 2
RMS_EPS = 1e-6
CHUNK = 128
GDN_HEADS = 4
SSD_HEADS = 8
LRU_C = 8.0
D_FF = 4096
N_MOD = 6
W_GDN = 2176
W_SSD = 1664
W_LRU = 512
W_GATE = 3072
ADAM_LR = 0.001
ADAM_B1 = 0.9
ADAM_B2 = 0.999
ADAM_EPS = 1e-08
ADAM_WD = 0.01
ADAM_STEP = 10
VMEM_LIMIT = 56 * 1024 * 1024
MESH = pl.DeviceIdType.MESH


def _dot(a, b, ta, tb):
    dn = (((0 if ta else 1,), (1 if tb else 0,)), ((), ()))
    return lax.dot_general(a.astype(BF16), b.astype(BF16), dn, preferred_element_type=F32)


@functools.partial(jax.custom_vjp, nondiff_argnums=(2, 3))
def mm(a, b, ta, tb):
    return _dot(a, b, ta, tb)


def _mm_fwd(a, b, ta, tb):
    return _dot(a, b, ta, tb), (a, b)


def _mm_bwd(ta, tb, res, g):
    a, b = res
    if not ta and not tb:
        return mm(g, b, False, True), mm(a, g, True, False)
    if not ta and tb:
        return mm(g, b, False, False), mm(g, a, True, False)
    assert ta and not tb
    return mm(b, g, False, True), mm(a, g, False, False)


mm.defvjp(_mm_fwd, _mm_bwd)


def _tri_apply(x, upper):
    t = x.shape[0]
    r = lax.broadcasted_iota(jnp.int32, (t, t), 0)
    c = lax.broadcasted_iota(jnp.int32, (t, t), 1)
    tri = jnp.where((r <= c) if upper else (r >= c), 1.0, 0.0).astype(BF16)
    x1 = x.astype(BF16)
    r1 = x - x1.astype(F32)
    x2 = r1.astype(BF16)
    x3 = (r1 - x2.astype(F32)).astype(BF16)
    d = lambda p: jnp.dot(tri, p, preferred_element_type=F32)
    return (d(x1) + d(x2)) + d(x3)


@jax.custom_vjp
def cumsum_rows(x):
    return _tri_apply(x, False)


cumsum_rows.defvjp(lambda x: (_tri_apply(x, False), None), lambda _, g: (_tri_apply(g, True),))


def _dot_split(a, b):
    a1, b1 = a.astype(BF16), b.astype(BF16)
    a2, b2 = (a - a1.astype(F32)).astype(BF16), (b - b1.astype(F32)).astype(BF16)
    d = lambda p, q: jnp.dot(p, q, preferred_element_type=F32)
    return d(a1, b1) + (d(a1, b2) + d(a2, b1))


def _neumann(ms):
    t = ms[0].shape[0]
    xs = [-m for m in ms]
    qs = [_dot(m, m, False, False) for m in ms]
    n = 2
    while True:
        xs = [x + q + _dot(x, q, False, False) for x, q in zip(xs, qs)]
        n *= 2
        if n >= t:
            break
        qs = [_dot(q, q, False, False) for q in qs]
    rs = [-(x + m + _dot_split(m, x)) for x, m in zip(xs, ms)]
    return [x + r + _dot(x, r, False, False) for x, r in zip(xs, rs)]


@jax.custom_vjp
def tri_solve(ms, rhss):
    return tuple(rhs + _dot(x, rhs, False, False) for x, rhs in zip(_neumann(ms), rhss))


def _tri_solve_fwd(ms, rhss):
    xs = _neumann(ms)
    sols = tuple(rhs + _dot(x, rhs, False, False) for x, rhs in zip(xs, rhss))
    return sols, (tuple(xs), sols)


def _tri_solve_bwd(res, gs):
    xs, sols = res
    d_rhss = tuple(g + _dot(x, g, True, False) for x, g in zip(xs, gs))
    return tuple(-_dot(d, sol, False, True) for d, sol in zip(d_rhss, sols)), d_rhss


tri_solve.defvjp(_tri_solve_fwd, _tri_solve_bwd)


@functools.partial(jax.custom_vjp, nondiff_argnums=(1,))
def split_cols(x, sizes):
    out, o = [], 0
    for s in sizes:
        out.append(x[:, o:o + s])
        o += s
    return tuple(out)


split_cols.defvjp(lambda x, sizes: (split_cols(x, sizes), None),
                  lambda sizes, _, g: (jnp.concatenate(list(g), axis=1),))


@functools.partial(jax.custom_vjp, nondiff_argnums=(1,))
def _last_rows(x, t):
    return x[t - 8:, :]


_last_rows.defvjp(lambda x, t: (_last_rows(x, t), None),
                  lambda t, _, g: (jnp.concatenate([jnp.zeros((t - 8, g.shape[1]), g.dtype), g], axis=0),))


def last8(x):
    return _last_rows(x, x.shape[0])


def _shifted(xp, d, t):
    return (pltpu.roll(xp, d, 0) if d else xp)[8:8 + t, :]


@jax.custom_vjp
def conv4(x, tail, w):
    t = x.shape[0]
    xp = jnp.concatenate([tail, x], axis=0)
    return sum(_shifted(xp, 3 - k, t) * w[k:k + 1, :] for k in range(4))


def _conv4_fwd(x, tail, w):
    return conv4(x, tail, w), (x, tail, w)


def _conv4_bwd(res, g):
    x, tail, w = res
    t = x.shape[0]
    xp = jnp.concatenate([tail, x], axis=0)
    zero8 = jnp.zeros((8, g.shape[1]), g.dtype)
    d_xp = jnp.zeros_like(xp)
    d_w = []
    for k in range(4):
        gk = jnp.concatenate([zero8, g * w[k:k + 1, :]], axis=0)
        d_xp = d_xp + (pltpu.roll(gk, t + 8 - (3 - k), 0) if k < 3 else gk)
        d_w.append(jnp.sum(g * _shifted(xp, 3 - k, t), axis=0, keepdims=True))
    return d_xp[8:, :], d_xp[:8, :], jnp.concatenate(d_w, axis=0)


conv4.defvjp(_conv4_fwd, _conv4_bwd)


def _sigmoid(x):
    return 0.5 * (jnp.tanh(0.5 * x) + 1.0)


def _silu(x):
    return x * _sigmoid(x)


def _softplus(x):
    ax = jnp.where(x > 0, x, -x)
    return jnp.where(x > 0, x, 0.0) + jnp.log(1.0 + jnp.exp(-ax))


def _gelu(x):
    return 0.5 * x * (1.0 + jnp.tanh(math.sqrt(2.0 / math.pi) * (x + 0.044715 * (x * x * x))))


def _expm1(x):
    series = x * (1.0 + x * (0.5 + x * (1.0 / 6.0 + x * (1.0 / 24.0))))
    return jnp.where(jnp.abs(x) < 0.03, series, jnp.exp(x) - 1.0)


def _rms(x, w):
    return x * lax.rsqrt(jnp.mean(x * x, axis=-1, keepdims=True) + RMS_EPS) * w


def _lane_pick(x, j):
    lane = lax.broadcasted_iota(jnp.int32, (1, x.shape[1]), 1)
    return jnp.sum(jnp.where(lane == j, x, 0.0), axis=1, keepdims=True)


def _row_pick(x, j):
    row = lax.broadcasted_iota(jnp.int32, (x.shape[0], 1), 0)
    return jnp.sum(jnp.where(row == j, x, 0.0), axis=0, keepdims=True)


def gdn_fn(carry, seq, params):
    *states, tail = carry
    (tile,) = seq
    conv_w, alog_row, dtb_row, norm_w = params
    t = tile.shape[0]
    qkv_raw, z, sm = split_cols(tile, (1536, 512, 128))
    qkv = _silu(conv4(qkv_raw, tail, conv_w))
    parts = split_cols(qkv, (128,) * 12)
    zs = split_cols(z, (128,) * 4)
    lane = lax.broadcasted_iota(jnp.int32, (1, 128), 1)
    beta_all = _sigmoid(sm)
    g_all = jnp.where((lane >= 4) & (lane < 8), -jnp.exp(alog_row) * _softplus(sm + dtb_row), 0.0)
    gc_all = cumsum_rows(g_all)
    gr_all = gc_all.T
    gl_all = _row_pick(gc_all, t - 1)
    r = lax.broadcasted_iota(jnp.int32, (t, t), 0)
    c = lax.broadcasted_iota(jnp.int32, (t, t), 1)
    heads = range(GDN_HEADS)
    l2 = lambda a: a * lax.rsqrt(jnp.sum(a * a, axis=-1, keepdims=True) + RMS_EPS)
    qn = [l2(parts[h]) * (128.0 ** -0.5) for h in heads]
    kn = [l2(parts[4 + h]) for h in heads]
    beta = [_lane_pick(beta_all, h) for h in heads]
    gc = [_lane_pick(gc_all, 4 + h) for h in heads]
    gl = [_lane_pick(gl_all, 4 + h) for h in heads]
    decay = [jnp.exp(jnp.where(r >= c, gc[h] - _row_pick(gr_all, 4 + h), -1e30)) for h in heads]
    kk = [mm(kn[h], kn[h], False, True) for h in heads]
    qk = [mm(qn[h], kn[h], False, True) for h in heads]
    m = tuple(jnp.where(r > c, beta[h] * kk[h] * decay[h], 0.0) for h in heads)
    eg = [jnp.exp(gc[h]) for h in heads]
    rhs = tuple(jnp.concatenate([beta[h] * parts[8 + h], (beta[h] * eg[h]) * kn[h]], axis=1) for h in heads)
    uw = [split_cols(s, (128, 128)) for s in tri_solve(m, rhs)]
    ws = [mm(uw[h][1], states[h], False, False) for h in heads]
    qs = [mm(qn[h] * eg[h], states[h], False, False) for h in heads]
    v_new = [uw[h][0] - ws[h] for h in heads]
    o = [qs[h] + mm(qk[h] * decay[h], v_new[h], False, False) for h in heads]
    kv = [mm(kn[h] * jnp.exp(gl[h] - gc[h]), v_new[h], True, False) for h in heads]
    new_states = [states[h] * jnp.exp(gl[h]) + kv[h] for h in heads]
    outs = [_rms(o[h], norm_w) * _silu(zs[h]) for h in heads]
    return (*new_states, last8(qkv_raw)), (jnp.concatenate(outs, axis=1),)


def ssd_fn(carry, seq, params):
    *states, tail = carry
    (tile,) = seq
    conv_w, conv_b, alog_row, dtb_row, d_row, norm_w = params
    t = tile.shape[0]
    xbc_raw, z, sm = split_cols(tile, (1024, 512, 128))
    xbc = _silu(conv4(xbc_raw, tail, conv_w) + conv_b)
    x0, x1, x2, x3, b0, b1, c0, c1 = split_cols(xbc, (128,) * 8)
    xs, bs, cs = (x0, x1, x2, x3), (b0, b1), (c0, c1)
    ds = split_cols(d_row, (128,) * 4)
    lane = lax.broadcasted_iota(jnp.int32, (1, 128), 1)
    sub = lax.broadcasted_iota(jnp.int32, (128, 1), 0)
    low = lane < 64
    dt_all = jnp.where(lane < SSD_HEADS, _softplus(sm + dtb_row), 0.0)
    ac_all = cumsum_rows(dt_all * (-jnp.exp(alog_row)))
    ar_all = ac_all.T
    al_all = _row_pick(ac_all, t - 1)
    r = lax.broadcasted_iota(jnp.int32, (t, t), 0)
    c = lax.broadcasted_iota(jnp.int32, (t, t), 1)
    pairs, heads = range(4), range(SSD_HEADS)
    col = [_lane_pick(ac_all, h) for h in heads]
    last = [_lane_pick(al_all, h) for h in heads]
    dt = [_lane_pick(dt_all, h) for h in heads]
    lm = [jnp.exp(jnp.where(r >= c, col[h] - _row_pick(ar_all, h), -1e30)) for h in heads]
    cb = [mm(cs[g], bs[g], False, True) for g in range(2)]
    both = lambda a, b: jnp.where(low, a, b)
    xdt = [xs[p] * both(dt[2 * p], dt[2 * p + 1]) for p in pairs]
    y_off = [mm(cs[p // 2], states[p], False, True) for p in pairs]
    y_lo = [mm(cb[p // 2] * lm[2 * p], jnp.where(low, xdt[p], 0.0), False, False) for p in pairs]
    y_hi = [mm(cb[p // 2] * lm[2 * p + 1], jnp.where(low, 0.0, xdt[p]), False, False) for p in pairs]
    st = [mm(xdt[p] * both(jnp.exp(last[2 * p] - col[2 * p]), jnp.exp(last[2 * p + 1] - col[2 * p + 1])),
             bs[p // 2], True, False) for p in pairs]
    ys = [ds[p] * xs[p] + y_lo[p] + y_hi[p] + y_off[p] * both(jnp.exp(col[2 * p]), jnp.exp(col[2 * p + 1]))
          for p in pairs]
    new_states = [states[p] * jnp.where(sub < 64, jnp.exp(last[2 * p]), jnp.exp(last[2 * p + 1])) + st[p]
                  for p in pairs]
    gz = jnp.concatenate(ys, axis=1) * _silu(z)
    g0, g1 = split_cols(gz, (256, 256))
    n0, n1 = split_cols(norm_w, (256, 256))
    out = jnp.concatenate([_rms(g0, n0), _rms(g1, n1)], axis=1)
    return (*new_states, last8(xbc_raw)), (out,)


def lru_in_fn(carry, seq, params):
    (tail,) = carry
    (x,) = seq
    conv_w, conv_b, w_a, b_a, w_x, b_x, lam = params
    xc = conv4(x, tail, conv_w) + conv_b
    r = _sigmoid(mm(xc, w_a, False, False) + b_a)
    i = _sigmoid(mm(xc, w_x, False, False) + b_x)
    log_a = -LRU_C * r * _softplus(-lam)
    u = jnp.sqrt(-_expm1(2.0 * log_a)) * (i * xc)
    return (last8(x),), (jnp.exp(log_a), u)


def lru_out_fn(carry, seq, params):
    hs, gate = seq
    return (), (hs * _gelu(gate),)


def merge_fn(carry, seq, params):
    ya, yb, yc, gl = seq
    g = split_cols(_sigmoid(gl), (D_MODEL,) * 3)
    merged = sum(g[r] * mm(y, params[r], False, False) for r, y in enumerate((ya, yb, yc)))
    return (), (merged,)


def _adaln(x, w, sc, sh):
    return _rms(x, w) * (1.0 + sc) + sh


def norm1_fn(carry, seq, params):
    (x,) = seq
    return (), (_adaln(x, *params), x)


def resid_norm_fn(carry, seq, params):
    x, mix = seq
    gt, w, sc, sh = params
    x1 = x + gt * mix
    return (), (_adaln(x1, w, sc, sh), x1)


def resid_fn(carry, seq, params):
    x, dn = seq
    (gt,) = params
    return (), (x + gt * dn,)


def silu_fn(carry, seq, params):
    return (), (_silu(seq[0]),)


def _full_spec(a):
    nd = a.ndim
    return pl.BlockSpec(a.shape, lambda i: (0,) * nd)


def _cparams(*sem):
    return pltpu.CompilerParams(dimension_semantics=sem, vmem_limit_bytes=VMEM_LIMIT)


def scan_fwd(fn, name, tile, seqs, params, carry_shapes, outs, save_carry=False):
    rows = seqs[0].shape[0]
    tile = min(tile, rows)
    n = rows // tile
    ns, npar, nc, no = len(seqs), len(params), len(carry_shapes), len(outs)

    def body(*refs):
        seq_refs, refs = refs[:ns], refs[ns:]
        par_refs, refs = refs[:npar], refs[npar:]
        out_refs, refs = refs[:no], refs[no:]
        save_refs, refs = (refs[:nc], refs[nc:]) if save_carry else ((), refs)
        carry_refs = refs

        @pl.when(pl.program_id(0) == 0)
        def _():
            for cr in carry_refs:
                cr[...] = jnp.zeros_like(cr)

        carry = tuple(cr[...] for cr in carry_refs)
        for sr, cv in zip(save_refs, carry):
            sr[0] = cv
        new_carry, res = fn(carry, tuple(r[...].astype(F32) for r in seq_refs),
                            tuple(r[...].astype(F32) for r in par_refs))
        for r, v in zip(out_refs, res):
            r[...] = v.astype(r.dtype)
        for cr, v in zip(carry_refs, new_carry):
            cr[...] = v

    out_shape = [jax.ShapeDtypeStruct((rows, w), dt) for w, dt in outs]
    out_specs = [pl.BlockSpec((tile, w), lambda i: (i, 0)) for w, _ in outs]
    if save_carry:
        out_shape += [jax.ShapeDtypeStruct((n, *s), F32) for s in carry_shapes]
        out_specs += [pl.BlockSpec((1, *s), lambda i: (i, 0, 0)) for s in carry_shapes]
    res = pl.pallas_call(
        body, name=name, grid=(n,),
        in_specs=[pl.BlockSpec((tile, s.shape[1]), lambda i: (i, 0)) for s in seqs] + [_full_spec(p) for p in params],
        out_specs=out_specs, out_shape=out_shape,
        scratch_shapes=[pltpu.VMEM(s, F32) for s in carry_shapes],
        compiler_params=_cparams("arbitrary"),
    )(*seqs, *params)
    return res[:no], res[no:]


def scan_bwd(fn, name, tile, seqs, params, saved, douts, n_dseq, n_dpar, dseq_dtypes=None):
    dseq_dtypes = dseq_dtypes or [F32] * n_dseq
    rows = seqs[0].shape[0]
    tile = min(tile, rows)
    n = rows // tile
    ns, npar, nc, no = len(seqs), len(params), len(saved), len(douts)

    def body(*refs):
        seq_refs, refs = refs[:ns], refs[ns:]
        par_refs, refs = refs[:npar], refs[npar:]
        save_refs, refs = refs[:nc], refs[nc:]
        dout_refs, refs = refs[:no], refs[no:]
        dseq_refs, refs = refs[:n_dseq], refs[n_dseq:]
        dpar_refs, refs = refs[:n_dpar], refs[n_dpar:]
        dcarry_refs = refs

        @pl.when(pl.program_id(0) == 0)
        def _():
            for r in (*dpar_refs, *dcarry_refs):
                r[...] = jnp.zeros_like(r)

        carry = tuple(r[0] for r in save_refs)
        seq = tuple(r[...].astype(F32) for r in seq_refs)
        par = tuple(r[...].astype(F32) for r in par_refs)

        def f(carry, dseq, dpar):
            return fn(carry, (*dseq, *seq[n_dseq:]), (*dpar, *par[n_dpar:]))

        _, vjp = jax.vjp(f, carry, seq[:n_dseq], par[:n_dpar])
        d_carry, d_seq, d_par = vjp((tuple(r[...] for r in dcarry_refs),
                                     tuple(r[...].astype(F32) for r in dout_refs)))
        for r, v in zip(dseq_refs, d_seq):
            r[...] = v.astype(r.dtype)
        for r, v in zip(dpar_refs, d_par):
            r[...] += v
        for r, v in zip(dcarry_refs, d_carry):
            r[...] = v

    rev = lambda i: (n - 1 - i, 0)
    res = pl.pallas_call(
        body, name=name, grid=(n,),
        in_specs=([pl.BlockSpec((tile, s.shape[1]), rev) for s in seqs] + [_full_spec(p) for p in params]
                  + [pl.BlockSpec((1, *s.shape[1:]), lambda i: (n - 1 - i, 0, 0)) for s in saved]
                  + [pl.BlockSpec((tile, d.shape[1]), rev) for d in douts]),
        out_specs=([pl.BlockSpec((tile, s.shape[1]), rev) for s in seqs[:n_dseq]]
                   + [_full_spec(p) for p in params[:n_dpar]]),
        out_shape=([jax.ShapeDtypeStruct((rows, s.shape[1]), dt) for s, dt in zip(seqs[:n_dseq], dseq_dtypes)]
                   + [jax.ShapeDtypeStruct(p.shape, F32) for p in params[:n_dpar]]),
        scratch_shapes=[pltpu.VMEM(s.shape[1:], F32) for s in saved],
        compiler_params=_cparams("arbitrary"),
    )(*seqs, *params, *saved, *douts)
    return res[:n_dseq], res[n_dseq:]


def _tile_of(dim, pref):
    if dim <= pref:
        return dim
    best = max((t for t in range(128, pref + 1, 128) if dim % t == 0), default=None)
    if best is None or (best < 512 and dim <= 2304):
        return dim
    return best


def _row_tile(rows, pref):
    if rows <= pref:
        return rows
    return max(t for t in range(8, pref + 1, 8) if rows % t == 0)


def matmul(a, b, name, ta=False, tb=False, out_dtype=F32, add=None, bias=None, relu2=False, relu2_of=None,
           tm=1024, tn=2048, tk=1024):
    m, k = (a.shape[1], a.shape[0]) if ta else a.shape
    n = b.shape[0] if tb else b.shape[1]
    assert k == (b.shape[1] if tb else b.shape[0])
    tm, tn, tk = _tile_of(m, tm), _tile_of(n, tn), _tile_of(k, tk)
    nm, nn, nk = m // tm, n // tn, k // tk
    assert nk == 1 or (out_dtype == F32 and not relu2 and relu2_of is None)
    dn = (((0 if ta else 1,), (1 if tb else 0,)), ((), ()))
    has_add, has_bias, has_u = add is not None, bias is not None, relu2_of is not None
    n_inner = a.size * a.dtype.itemsize * (nn - 1) >= b.size * b.dtype.itemsize * (nm - 1)
    ij = (lambda g0, g1: (g0, g1)) if n_inner else (lambda g0, g1: (g1, g0))

    def body(*refs):
        a_ref, b_ref, refs = refs[0], refs[1], refs[2:]
        add_ref, refs = (refs[0], refs[1:]) if has_add else (None, refs)
        bias_ref, refs = (refs[0], refs[1:]) if has_bias else (None, refs)
        u_ref, refs = (refs[0], refs[1:]) if has_u else (None, refs)
        o_ref = refs[0]
        r = lax.dot_general(a_ref[...].astype(BF16), b_ref[...].astype(BF16), dn, preferred_element_type=F32)

        def first():
            v = r
            if has_add:
                v = v + add_ref[...]
            if has_bias:
                v = v + bias_ref[...]
            if has_u:
                v = v * (2.0 * jnp.maximum(u_ref[...], 0.0))
            o_ref[...] = v.astype(o_ref.dtype)
            if relu2:
                p = jnp.maximum(v, 0.0)
                refs[1][...] = (p * p).astype(BF16)

        if nk == 1:
            first()
        else:
            pl.when(pl.program_id(2) == 0)(first)

            @pl.when(pl.program_id(2) > 0)
            def _():
                o_ref[...] += r

    def spec(shape, fn):
        return pl.BlockSpec(shape, lambda g0, g1, l: fn(*ij(g0, g1), l))

    a_spec = spec((tk, tm), lambda i, j, l: (l, i)) if ta else spec((tm, tk), lambda i, j, l: (i, l))
    b_spec = spec((tn, tk), lambda i, j, l: (j, l)) if tb else spec((tk, tn), lambda i, j, l: (l, j))
    o_spec = spec((tm, tn), lambda i, j, l: (i, j))
    in_specs, args = [a_spec, b_spec], [a, b]
    if has_add:
        in_specs.append(o_spec)
        args.append(add)
    if has_bias:
        in_specs.append(spec((1, tn), lambda i, j, l: (0, j)))
        args.append(bias)
    if has_u:
        in_specs.append(o_spec)
        args.append(relu2_of)
    out_shape = [jax.ShapeDtypeStruct((m, n), out_dtype)] + ([jax.ShapeDtypeStruct((m, n), BF16)] if relu2 else [])
    res = pl.pallas_call(
        body, name=name, grid=(nm, nn, nk) if n_inner else (nn, nm, nk), in_specs=in_specs,
        out_specs=[o_spec] * len(out_shape), out_shape=out_shape,
        compiler_params=_cparams("parallel", "parallel", "arbitrary"),
    )(*args)
    return res if relu2 else res[0]


LIN_TILE = 512


def linscan_fwd(a, u, name):
    rows, w = a.shape
    tile = min(LIN_TILE, rows)

    def body(a_ref, u_ref, h_ref, hc):
        @pl.when(pl.program_id(0) == 0)
        def _():
            hc[...] = jnp.zeros_like(hc)

        row = lax.broadcasted_iota(jnp.int32, (8, 1), 0)

        def group(k, h_in):
            rows8 = pl.ds(pl.multiple_of(k * 8, 8), 8)
            pa, pu = a_ref[rows8, :], u_ref[rows8, :]
            for d in (1, 2, 4):
                pu = pu + pa * jnp.where(row >= d, pltpu.roll(pu, d, 0), 0.0)
                pa = pa * jnp.where(row >= d, pltpu.roll(pa, d, 0), 1.0)
            h_ref[rows8, :] = pa * h_in + pu
            return h_ref[pl.ds(k * 8 + 7, 1), :]

        hc[...] = lax.fori_loop(0, tile // 8, group, hc[...], unroll=4)

    spec = pl.BlockSpec((tile, w), lambda i: (i, 0))
    return pl.pallas_call(
        body, name=name, grid=(rows // tile,), in_specs=[spec, spec], out_specs=spec,
        out_shape=jax.ShapeDtypeStruct((rows, w), F32), scratch_shapes=[pltpu.VMEM((1, w), F32)],
        compiler_params=_cparams("arbitrary"),
    )(a, u)


def linscan_bwd(a, hs, dh, name):
    rows, w = a.shape
    tile = min(LIN_TILE, rows)
    n = rows // tile
    per = tile // 8

    def body(a_ref, h_ref, hprev_ref, dh_ref, da_ref, du_ref, cc):
        i = pl.program_id(0)

        @pl.when(i == 0)
        def _():
            cc[...] = jnp.zeros_like(cc)

        row = lax.broadcasted_iota(jnp.int32, (8, 1), 0)
        h_before = jnp.where(i == n - 1, 0.0, hprev_ref[7:8, :])

        def group(s, c_in):
            k = per - 1 - s
            rows8 = pl.ds(pl.multiple_of(k * 8, 8), 8)
            av, hv = a_ref[rows8, :], h_ref[rows8, :]
            pb = jnp.where(row < 7, pltpu.roll(av, 7, 0), 1.0)
            pg = dh_ref[rows8, :]
            for d in (1, 2, 4):
                pg = pg + pb * jnp.where(row < 8 - d, pltpu.roll(pg, 8 - d, 0), 0.0)
                pb = pb * jnp.where(row < 8 - d, pltpu.roll(pb, 8 - d, 0), 1.0)
            g = pg + pb * c_in
            du_ref[rows8, :] = g
            h_prev = jnp.where(k == 0, h_before, h_ref[pl.ds(jnp.maximum(k * 8 - 1, 0), 1), :])
            da_ref[rows8, :] = g * jnp.where(row >= 1, pltpu.roll(hv, 1, 0), h_prev)
            return a_ref[pl.ds(k * 8, 1), :] * du_ref[pl.ds(k * 8, 1), :]

        cc[...] = lax.fori_loop(0, per, group, cc[...], unroll=4)

    rev = pl.BlockSpec((tile, w), lambda i: (n - 1 - i, 0))
    prev = pl.BlockSpec((8, w), lambda i: (jnp.maximum((n - 1 - i) * per - 1, 0), 0))
    return pl.pallas_call(
        body, name=name, grid=(n,), in_specs=[rev, rev, prev, rev], out_specs=[rev, rev],
        out_shape=[jax.ShapeDtypeStruct((rows, w), F32)] * 2, scratch_shapes=[pltpu.VMEM((1, w), F32)],
        compiler_params=_cparams("arbitrary"),
    )(a, hs, hs, dh)


def loss_head(x, target, w, name):
    rows, d = x.shape
    tile = min(512, rows)

    def body(x_ref, t_ref, w_ref, loss_ref, dx_ref, dw_ref):
        @pl.when(pl.program_id(0) == 0)
        def _():
            loss_ref[...] = jnp.zeros_like(loss_ref)
            dw_ref[...] = jnp.zeros_like(dw_ref)

        tv = t_ref[...]

        def f(xv, wv):
            e = _rms(xv, wv) - tv
            return 0.5 * jnp.sum(jnp.mean(e * e, axis=-1, keepdims=True), axis=0, keepdims=True)

        val, vjp = jax.vjp(f, x_ref[...], w_ref[...])
        dxv, dwv = vjp(jnp.ones((1, 1), F32))
        loss_ref[...] += jnp.broadcast_to(val, loss_ref.shape)
        dx_ref[...] = dxv
        dw_ref[...] += dwv

    spec = pl.BlockSpec((tile, d), lambda i: (i, 0))
    return pl.pallas_call(
        body, name=name, grid=(rows // tile,), in_specs=[spec, spec, _full_spec(w)],
        out_specs=[pl.BlockSpec((8, 128), lambda i: (0, 0)), spec, _full_spec(w)],
        out_shape=[jax.ShapeDtypeStruct((8, 128), F32), jax.ShapeDtypeStruct((rows, d), F32),
                   jax.ShapeDtypeStruct(w.shape, F32)],
        compiler_params=_cparams("arbitrary"),
    )(x, target, w)


def adamw(g, w, m, v, name, copy_g=False):
    layers, rows, cols = g.shape
    tile = _row_tile(rows, 256)
    n_out = 4 if copy_g else 3

    def body(g_ref, w_ref, m_ref, v_ref, d_ref, nm_ref, nv_ref, *g_out):
        gv = g_ref[...]
        if copy_g:
            g_out[0][...] = gv
        nm = ADAM_B1 * m_ref[...] + (1.0 - ADAM_B1) * gv
        nv = ADAM_B2 * v_ref[...] + (1.0 - ADAM_B2) * (gv * gv)
        m_hat = nm / (1.0 - ADAM_B1 ** ADAM_STEP)
        v_hat = nv / (1.0 - ADAM_B2 ** ADAM_STEP)
        d_ref[...] = -ADAM_LR * (m_hat / (jnp.sqrt(v_hat) + ADAM_EPS) + ADAM_WD * w_ref[...])
        nm_ref[...] = nm
        nv_ref[...] = nv

    spec = pl.BlockSpec((None, tile, cols), lambda l, i: (l, i, 0))
    return pl.pallas_call(
        body, name=name, grid=(layers, rows // tile), in_specs=[spec] * 4, out_specs=[spec] * n_out,
        out_shape=[jax.ShapeDtypeStruct((layers, rows, cols), F32)] * n_out,
        compiler_params=_cparams("parallel", "parallel"),
    )(g, w, m, v)


def add_layers(mine, theirs, name):
    r, c = mine[0].shape
    tile = _row_tile(r, 256)
    nl = len(mine)

    def body(*refs):
        o_ref = refs[-1]
        for l in range(nl):
            o_ref[l] = refs[l][...] + refs[nl + l][...]

    spec = pl.BlockSpec((tile, c), lambda i: (i, 0))
    return pl.pallas_call(
        body, name=name, grid=(r // tile,), in_specs=[spec] * (2 * nl),
        out_specs=pl.BlockSpec((nl, tile, c), lambda i: (0, i, 0)),
        out_shape=jax.ShapeDtypeStruct((nl, r, c), F32), compiler_params=_cparams("parallel"),
    )(*mine, *theirs)


def sum4(own, recv, by_cols, place, name):
    _, r, c = recv.shape
    tile = _row_tile(r, 256)
    nt = r // tile
    own_map = (lambda i, k: (i, k[0])) if by_cols else (lambda i, k: (k[0] * nt + i, 0))

    def body(k_ref, own_ref, recv_ref, o_ref):
        o_ref[...] = ((own_ref[...] + recv_ref[0].astype(F32)) + recv_ref[1].astype(F32)) + recv_ref[2].astype(F32)

    return pl.pallas_call(
        body, name=name,
        grid_spec=pltpu.PrefetchScalarGridSpec(
            num_scalar_prefetch=1, grid=(nt,),
            in_specs=[pl.BlockSpec((tile, c), own_map), pl.BlockSpec((3, tile, c), lambda i, k: (0, i, 0))],
            out_specs=pl.BlockSpec((tile, c), lambda i, k: (i, 0))),
        out_shape=jax.ShapeDtypeStruct((r, c), F32),
        compiler_params=_cparams("arbitrary"),
    )(place, own, recv)


def add8(parts, name):
    _, rows, cols = parts.shape

    def body(p_ref, o_ref):
        acc = p_ref[0]
        for k in range(1, 8):
            acc = acc + p_ref[k]
        o_ref[...] = acc

    return pl.pallas_call(
        body, name=name, in_specs=[pl.BlockSpec(memory_space=pltpu.VMEM)],
        out_specs=pl.BlockSpec(memory_space=pltpu.VMEM),
        out_shape=jax.ShapeDtypeStruct((rows, cols), F32),
        compiler_params=pltpu.CompilerParams(vmem_limit_bytes=VMEM_LIMIT),
    )(parts)


def _place():
    return lax.axis_index("x"), lax.axis_index("y"), lax.axis_index("c")


def _other_chips(x, y):
    return [(1 - x, y), (x, 1 - y), (1 - x, 1 - y)]


_ANY = pl.BlockSpec(memory_space=pl.ANY)


BIG_LAYOUT = (("ada_w", "col", (1024, 6144)), ("w_in", "chip", (4, 1024, 1924)), ("w_branch", "col", (3, 512, 1024)),
              ("w_out", "row", (1024, 1024)), ("w_up", "col", (1024, 4096)), ("w_down", "row", (4096, 1024)))
N_BIG = len(BIG_LAYOUT)
REDUCE_LAYOUT = BIG_LAYOUT[1:]


def _local_shape(kind, full):
    if kind == "col":
        return (*full[:-1], full[-1] // 4)
    if kind == "row":
        return (full[0] // 4, *full[1:])
    return full[1:]


def _window(ref, kind, k, local):
    if kind == "chip":
        return ref.at[k]
    if kind == "row":
        return ref.at[pl.ds(pl.multiple_of(k * local[0], 8), local[0])]
    idx = (slice(None),) * (len(local) - 1) + (pl.ds(pl.multiple_of(k * local[-1], 128), local[-1]),)
    return ref.at[idx]


def _dma_call(body, name, n_in, out_shape, sems, aliases=None):
    return pl.pallas_call(
        body, name=name, in_specs=[_ANY] * n_in, out_specs=[_ANY] * len(out_shape), out_shape=out_shape,
        scratch_shapes=[pltpu.SemaphoreType.DMA((n,)) for n in sems],
        input_output_aliases=aliases or {},
        compiler_params=pltpu.CompilerParams(has_side_effects=True))


def _remote(src, dst, send_sem, recv_sem, to):
    return pltpu.make_async_remote_copy(src_ref=src, dst_ref=dst, send_sem=send_sem, recv_sem=recv_sem,
                                        device_id=to, device_id_type=MESH)


def gather_layer(shards, layout, per_core, name):
    nb = len(shards)
    locals_ = [_local_shape(kind, full) for _, kind, full in layout]

    def body(*refs):
        sh, full, (send_sems, recv_sems, local_sems, pass_send, pass_recv) = (
            refs[:nb], refs[nb:2 * nb], refs[2 * nb:])
        x, y, c = _place()
        me = 2 * x + y
        chips = _other_chips(x, y)
        win = lambda n, k: _window(full[n], layout[n][1], k, locals_[n])
        for cc in (0, 1):
            @pl.when(c == cc)
            def _():
                mine, sends = {}, []
                for n in per_core[cc]:
                    mine[n] = pltpu.make_async_copy(sh[n], win(n, me), local_sems.at[n])
                    mine[n].start()
                    for j, chip in enumerate(chips):
                        sends.append(_remote(sh[n], win(n, me), send_sems.at[3 * n + j], recv_sems.at[3 * n + j],
                                             (chip[0], chip[1], c)))
                        sends[-1].start()
                for n in per_core[cc]:
                    for j, chip in enumerate(chips):
                        _remote(sh[n], win(n, 2 * chip[0] + chip[1]), send_sems.at[3 * n + j],
                                recv_sems.at[3 * n + j], (chip[0], chip[1], c)).wait_recv()
                    mine[n].wait()
                    sends.append(_remote(full[n], full[n], pass_send.at[n], pass_recv.at[n], (x, y, 1 - c)))
                    sends[-1].start()
                for n in per_core[1 - cc]:
                    _remote(full[n], full[n], pass_send.at[n], pass_recv.at[n], (x, y, 1 - c)).wait_recv()
                for cp in sends:
                    cp.wait_send()

    out_shape = [jax.ShapeDtypeStruct(full, BF16) for _, _, full in layout]
    return _dma_call(body, name, nb, out_shape, (3 * nb, 3 * nb, nb, nb, nb))(*shards)


_HBM = pl.BlockSpec(memory_space=pltpu.HBM)
_SEM = pl.BlockSpec(memory_space=pltpu.SEMAPHORE)


def _hbm(a):
    return pltpu.with_memory_space_constraint(a, pltpu.HBM)


def _gather_copies(sh, full, layout, send_sems, recv_sems, first=0):
    locals_ = [_local_shape(kind, f) for _, kind, f in layout]
    x, y, c = _place()
    me = 2 * x + y
    pairs = []
    for n in range(len(sh)):
        win = lambda k: _window(full[n], layout[n][1], k, locals_[n])
        for j, chip in enumerate(_other_chips(x, y)):
            k = 3 * (first + n) + j
            mk = lambda dst: _remote(sh[n], dst, send_sems.at[k], recv_sems.at[k], (chip[0], chip[1], c))
            pairs.append((mk(win(me)), mk(win(2 * chip[0] + chip[1]))))
    return pairs


def gather_start(shards, fulls, layout, after, name):
    nb = len(shards)

    def body(*refs):
        sh, full = refs[:nb], refs[nb:2 * nb]
        send_sems, recv_sems = refs[2 * nb + len(after):2 * nb + len(after) + 2]
        for out, _ in _gather_copies(sh, full, layout, send_sems, recv_sems):
            out.start()
        refs[-1][...] = jnp.zeros_like(refs[-1])

    thru = [pltpu.HBM(a.shape, a.dtype) for a in (*shards, *fulls)]
    res = pl.pallas_call(
        body, name=name,
        out_shape=(pltpu.SemaphoreType.DMA((3 * nb,)), pltpu.SemaphoreType.DMA((3 * nb,)), *thru,
                   jax.ShapeDtypeStruct((8, 128), F32)),
        in_specs=[_HBM] * (2 * nb) + [_ANY] * len(after),
        out_specs=(_SEM, _SEM, *[_HBM] * (2 * nb), pl.BlockSpec(memory_space=pltpu.VMEM)),
        input_output_aliases={i: 2 + i for i in range(2 * nb)},
        compiler_params=pltpu.CompilerParams(has_side_effects=pltpu.SideEffectType.DATAFLOW_SIDE_EFFECTING),
    )(*[_hbm(a) for a in (*shards, *fulls)], *after)
    return (res[0], res[1], res[2:2 + nb], res[2 + nb:2 + 2 * nb]), res[-1]


def gather_wait(send_sems, recv_sems, shards, fulls, layout, first, after, name):
    nb = len(shards)

    def body(*refs):
        sh, full = refs[:nb], refs[nb:2 * nb]
        ssem, rsem = refs[2 * nb:2 * nb + 2]
        for out, inc in _gather_copies(sh, full, layout, ssem, rsem, first):
            out.wait_send()
            inc.wait_recv()

    thru = [pltpu.HBM(a.shape, a.dtype) for a in (*shards, *fulls)]
    res = pl.pallas_call(
        body, name=name, out_shape=thru,
        in_specs=[_HBM] * (2 * nb) + [_SEM, _SEM, _ANY], out_specs=[_HBM] * (2 * nb),
        input_output_aliases={i: i for i in range(2 * nb)},
        compiler_params=pltpu.CompilerParams(has_side_effects=pltpu.SideEffectType.DATAFLOW_SIDE_EFFECTING),
    )(*shards, *fulls, send_sems, recv_sems, after)
    return res[nb:]


def swap_d2d(arrays, name):
    nb = len(arrays)

    def body(*refs):
        src, got, (send_sems, recv_sems) = refs[:nb], refs[nb:2 * nb], refs[2 * nb:]
        x, y, c = _place()
        copies = [_remote(src[n], got[n], send_sems.at[n], recv_sems.at[n], (x, y, 1 - c)) for n in range(nb)]
        for cp in copies:
            cp.start()
        for cp in copies:
            cp.wait_recv()
        for cp in copies:
            cp.wait_send()

    out_shape = [jax.ShapeDtypeStruct(a.shape, a.dtype) for a in arrays]
    return _dma_call(body, name, nb, out_shape, (nb, nb))(*arrays)


def _reduce_copies(src, recv, layout, send_sems, recv_sems):
    locals_ = [_local_shape(kind, full) for _, kind, full in layout]
    x, y, c = _place()
    copies = []
    for n in range(len(src)):
        for j, chip in enumerate(_other_chips(x, y)):
            copies.append(_remote(_window(src[n], layout[n][1], 2 * chip[0] + chip[1], locals_[n]), recv[n].at[j],
                                  send_sems.at[3 * n + j], recv_sems.at[3 * n + j], (chip[0], chip[1], c)))
    return copies


def _recv_shapes(sums, layout):
    return [(3, *_local_shape(kind, full)) for _, kind, full in layout]


def reduce_ici(sums, layout, name):
    nb = len(sums)

    def body(*refs):
        src, recv, (send_sems, recv_sems) = refs[:nb], refs[nb:2 * nb], refs[2 * nb:]
        copies = _reduce_copies(src, recv, layout, send_sems, recv_sems)
        for cp in copies:
            cp.start()
        for cp in copies:
            cp.wait_recv()
        for cp in copies:
            cp.wait_send()

    out_shape = [jax.ShapeDtypeStruct(s, a.dtype) for s, a in zip(_recv_shapes(sums, layout), sums)]
    return _dma_call(body, name, nb, out_shape, (3 * nb, 3 * nb))(*sums)


def reduce_start(sums, layout, after, name):
    nb = len(sums)
    lands = [lax.empty(s, a.dtype) for s, a in zip(_recv_shapes(sums, layout), sums)]

    def body(*refs):
        src, recv = refs[:nb], refs[nb:2 * nb]
        send_sems, recv_sems = refs[2 * nb + len(after):2 * nb + len(after) + 2]
        for cp in _reduce_copies(src, recv, layout, send_sems, recv_sems):
            cp.start()
        refs[-1][...] = jnp.zeros_like(refs[-1])

    thru = [pltpu.HBM(a.shape, a.dtype) for a in (*sums, *lands)]
    res = pl.pallas_call(
        body, name=name,
        out_shape=(pltpu.SemaphoreType.DMA((3 * nb,)), pltpu.SemaphoreType.DMA((3 * nb,)), *thru,
                   jax.ShapeDtypeStruct((8, 128), F32)),
        in_specs=[_HBM] * (2 * nb) + [_ANY] * len(after),
        out_specs=(_SEM, _SEM, *[_HBM] * (2 * nb), pl.BlockSpec(memory_space=pltpu.VMEM)),
        input_output_aliases={i: 2 + i for i in range(2 * nb)},
        compiler_params=pltpu.CompilerParams(has_side_effects=pltpu.SideEffectType.DATAFLOW_SIDE_EFFECTING),
    )(*[_hbm(a) for a in (*sums, *lands)], *after)
    return (res[0], res[1], res[2:2 + nb], res[2 + nb:2 + 2 * nb]), res[-1]


def reduce_wait(send_sems, recv_sems, sums, lands, layout, after, name):
    nb = len(sums)

    def body(*refs):
        src, recv = refs[:nb], refs[nb:2 * nb]
        ssem, rsem = refs[2 * nb:2 * nb + 2]
        for cp in _reduce_copies(src, recv, layout, ssem, rsem):
            cp.wait_send()
            cp.wait_recv()

    thru = [pltpu.HBM(a.shape, a.dtype) for a in (*sums, *lands)]
    res = pl.pallas_call(
        body, name=name, out_shape=thru,
        in_specs=[_HBM] * (2 * nb) + [_SEM, _SEM, _ANY], out_specs=[_HBM] * (2 * nb),
        input_output_aliases={i: i for i in range(2 * nb)},
        compiler_params=pltpu.CompilerParams(has_side_effects=pltpu.SideEffectType.DATAFLOW_SIDE_EFFECTING),
    )(*sums, *lands, send_sems, recv_sems, after)
    return res[nb:]


def allgather8(block, name):
    m_per, n = block.shape

    def body(x_ref, out_ref, send_sems, recv_sems, local_sem):
        x, y, c = _place()
        me, sibling = (x, y, c), (x, y, 1 - c)
        chips = _other_chips(x, y)

        def rows(px, py, pc):
            return out_ref.at[pl.ds((4 * px + 2 * py + pc) * m_per, m_per), :]

        def copy(k, blk, to, src=None):
            return pltpu.make_async_remote_copy(
                src_ref=rows(*blk) if src is None else src, dst_ref=rows(*blk), send_sem=send_sems.at[k],
                recv_sem=recv_sems.at[k], device_id=to, device_id_type=MESH)

        mine = pltpu.make_async_copy(x_ref, rows(*me), local_sem)
        mine.start()
        first = [copy(0, me, sibling, src=x_ref)]
        first += [copy(1 + j, me, (*chip, c), src=x_ref) for j, chip in enumerate(chips)]
        for cp in first:
            cp.start()
        passed = [copy(4 + j, (*chip, c), sibling) for j, chip in enumerate(chips)]
        for j, chip in enumerate(chips):
            copy(1 + j, (*chip, c), me).wait_recv()
            passed[j].start()
        copy(0, sibling, me).wait_recv()
        for j, chip in enumerate(chips):
            copy(4 + j, (*chip, 1 - c), me).wait_recv()
        for cp in first + passed:
            cp.wait_send()
        mine.wait()

    return pl.pallas_call(
        body, name=name, in_specs=[pl.BlockSpec(memory_space=pltpu.VMEM)],
        out_specs=pl.BlockSpec(memory_space=pltpu.VMEM),
        out_shape=jax.ShapeDtypeStruct((8 * m_per, n), block.dtype),
        scratch_shapes=[pltpu.SemaphoreType.DMA((7,)), pltpu.SemaphoreType.DMA((7,)), pltpu.SemaphoreType.DMA],
        compiler_params=pltpu.CompilerParams(vmem_limit_bytes=VMEM_LIMIT),
    )(block)


CONV = ("gdn_conv_w", "ssd_conv_w", "lru_conv_w")
SMALL = ("ada_b", "norm_mix", "gdn_a_log", "gdn_dt_bias", "gdn_norm", "ssd_conv_b", "ssd_a_log", "ssd_dt_bias",
         "ssd_d", "ssd_norm", "lru_conv_b", "lru_w_a", "lru_b_a", "lru_w_x", "lru_b_x", "lru_lambda", "norm_mlp",
         "final_norm")
WEIGHTS = ("ada_w", "ada_b", "norm_mix", "w_in", "gdn_conv_w", "gdn_a_log", "gdn_dt_bias", "gdn_norm", "ssd_conv_w",
           "ssd_conv_b", "ssd_a_log", "ssd_dt_bias", "ssd_d", "ssd_norm", "lru_conv_w", "lru_conv_b", "lru_w_a",
           "lru_b_a", "lru_w_x", "lru_b_x", "lru_lambda", "w_branch", "w_out", "norm_mlp", "w_up", "w_down",
           "final_norm")
PACK_COLS = 1024


def _pack_rows(shape):
    return 8 * -(-math.prod(shape) // (8 * PACK_COLS))


def _pack(arrays, dtype):
    parts = []
    for a in arrays:
        flat = a.reshape(-1).astype(dtype)
        pad = _pack_rows(a.shape) * PACK_COLS - flat.shape[0]
        parts.append((jnp.concatenate([flat, jnp.zeros((pad,), dtype)]) if pad else flat).reshape(-1, PACK_COLS))
    return jnp.concatenate(parts, axis=0)


def _unpack(pack, shapes):
    out, o = [], 0
    for s in shapes:
        r = _pack_rows(s)
        out.append(pack[o:o + r].reshape(-1)[:math.prod(s)].reshape(s))
        o += r
    return out


def _split_w_in(w4):
    w = jnp.concatenate([w4[k] for k in range(4)], axis=1)
    pad = jnp.zeros((w.shape[0], 120), w.dtype)
    gdn = jnp.concatenate([w[:, 0:2056], pad], axis=1)
    ssd = jnp.concatenate([w[:, 2056:2568], w[:, 3080:3592], w[:, 2568:3080], w[:, 3592:3600], pad], axis=1)
    return gdn, ssd, w[:, 3600:4112], w[:, 4112:4624], w[:, 4624:7696]


def _join_w_in(gdn, ssd, lx, lg, gate):
    w = jnp.concatenate([gdn[:, 0:2056], ssd[:, 0:512], ssd[:, 1024:1536], ssd[:, 512:1024], ssd[:, 1536:1544],
                         lx, lg, gate], axis=1)
    return jnp.stack([w[:, k * 1924:(k + 1) * 1924] for k in range(4)])


def _lanes(v, at, width=128):
    return jnp.concatenate([jnp.zeros((at,), F32), v, jnp.zeros((width - at - v.shape[0],), F32)]).reshape(1, width)


def _block_diag(w):
    return (jnp.eye(8, dtype=w.dtype)[:, None, :, None] * w[:, :, None, :]).reshape(512, 512)


def _diag_blocks(w):
    return jnp.stack([w[n * 64:(n + 1) * 64, n * 64:(n + 1) * 64] for n in range(8)])


TOK_TILE = 512
WIDE_TILE = 256


def _layer_params(p, big, l):
    row = lambda v: v.reshape(1, -1)
    (ada_w, w_in), late = big
    gdn = (p["gdn_conv_w"][l], _lanes(p["gdn_a_log"][l], 4), _lanes(p["gdn_dt_bias"][l], 4), row(p["gdn_norm"][l]))
    ssd = (p["ssd_conv_w"][l], row(p["ssd_conv_b"][l]), _lanes(p["ssd_a_log"][l], 0), _lanes(p["ssd_dt_bias"][l], 0),
           row(jnp.repeat(p["ssd_d"][l], 64)), row(p["ssd_norm"][l]))
    lru = (p["lru_conv_w"][l], row(p["lru_conv_b"][l]), _block_diag(p["lru_w_a"][l]), row(p["lru_b_a"][l]),
           _block_diag(p["lru_w_x"][l]), row(p["lru_b_x"][l]), row(p["lru_lambda"][l]))
    return dict(gdn=gdn, ssd=ssd, lru=lru, w_in=_split_w_in(w_in), late=late, ada_w=ada_w, ada_b=row(p["ada_b"][l]),
                norm_mix=row(p["norm_mix"][l]), norm_mlp=row(p["norm_mlp"][l]))


def _layer_fwd(x, silu_c, lp, l):
    nm = lambda s: f"l{l}_{s}"
    mod = matmul(silu_c, lp["ada_w"], nm("mod"), bias=lp["ada_b"])
    sh1, sc1, gt1, sh2, sc2, gt2 = (mod[0:1, k * D_MODEL:(k + 1) * D_MODEL] for k in range(N_MOD))
    (h,), _ = scan_fwd(norm1_fn, nm("norm1"), TOK_TILE, [x], [lp["norm_mix"], sc1, sh1], [], [(D_MODEL, BF16)])
    w_gdn, w_ssd, w_lx, w_lg, w_gate = lp["w_in"]
    p_gdn = matmul(h, w_gdn, nm("in_gdn"))
    p_ssd = matmul(h, w_ssd, nm("in_ssd"))
    p_lx = matmul(h, w_lx, nm("in_lx"))
    p_lg = matmul(h, w_lg, nm("in_lg"))
    p_gate = matmul(h, w_gate, nm("in_gate"))
    (ya,), sv_gdn = scan_fwd(gdn_fn, nm("gdn"), CHUNK, [p_gdn], lp["gdn"], [(128, 128)] * 4 + [(8, 1536)],
                             [(512, F32)], save_carry=True)
    (yb,), sv_ssd = scan_fwd(ssd_fn, nm("ssd"), CHUNK, [p_ssd], lp["ssd"], [(128, 128)] * 4 + [(8, 1024)],
                             [(512, F32)], save_carry=True)
    (a, u), sv_lru = scan_fwd(lru_in_fn, nm("lru_in"), TOK_TILE, [p_lx], lp["lru"], [(8, 512)],
                              [(512, F32), (512, F32)], save_carry=True)
    hs = linscan_fwd(a, u, nm("lru_scan"))
    (yc,), _ = scan_fwd(lru_out_fn, nm("lru_out"), TOK_TILE, [hs, p_lg], [], [], [(512, F32)])
    w_branch, lp["w_out"], lp["w_up"], lp["w_down"] = lp.pop("late")(yc)
    lp["wb"] = tuple(w_branch[r] for r in range(3))
    (merged,), _ = scan_fwd(merge_fn, nm("merge"), WIDE_TILE, [ya, yb, yc, p_gate], lp["wb"], [], [(D_MODEL, BF16)])
    mix = matmul(merged, lp["w_out"], nm("out"))
    (h2, x1), _ = scan_fwd(resid_norm_fn, nm("norm2"), TOK_TILE, [x, mix], [gt1, lp["norm_mlp"], sc2, sh2], [],
                           [(D_MODEL, BF16), (D_MODEL, F32)])
    up, act = matmul(h2, lp["w_up"], nm("up"), relu2=True)
    dn = matmul(act, lp["w_down"], nm("down"))
    (x2,), _ = scan_fwd(resid_fn, nm("resid"), TOK_TILE, [x1, dn], [gt2], [], [(D_MODEL, F32)])
    saved = dict(x=x, h=h, p_gdn=p_gdn, p_ssd=p_ssd, p_lx=p_lx, p_lg=p_lg, p_gate=p_gate, sv_gdn=sv_gdn,
                 sv_ssd=sv_ssd, sv_lru=sv_lru, a=a, hs=hs, ya=ya, yb=yb, yc=yc, merged=merged, mix=mix, x1=x1,
                 h2=h2, up=up, act=act, dn=dn, mod=(sh1, sc1, gt1, sh2, sc2, gt2))
    return x2, saved


def _layer_bwd(d_x2, lp, sv, l, on_early, on_last):
    nm = lambda s: f"l{l}_b_{s}"
    sh1, sc1, gt1, sh2, sc2, gt2 = sv["mod"]
    (d_x1, d_dn), (d_gt2,) = scan_bwd(resid_fn, nm("resid"), TOK_TILE, [sv["x1"], sv["dn"]], [gt2], [], [d_x2], 2, 1,
                                      [F32, BF16])
    d_up = matmul(d_dn, lp["w_down"], nm("down_x"), tb=True, relu2_of=sv["up"], out_dtype=BF16)
    g_w_down = matmul(sv["act"], d_dn, nm("down_w"), ta=True)
    d_h2 = matmul(d_up, lp["w_up"], nm("up_x"), tb=True)
    g_w_up = matmul(sv["h2"], d_up, nm("up_w"), ta=True)
    (d_x, d_mix), (d_gt1, g_norm_mlp, d_sc2, d_sh2) = scan_bwd(
        resid_norm_fn, nm("norm2"), TOK_TILE, [sv["x"], sv["mix"]], [gt1, lp["norm_mlp"], sc2, sh2], [],
        [d_h2, d_x1], 2, 4, [F32, BF16])
    d_merged = matmul(d_mix, lp["w_out"], nm("out_x"), tb=True)
    g_w_out = matmul(sv["merged"], d_mix, nm("out_w"), ta=True)
    (d_ya, d_yb, d_yc, d_pgate), g_wb = scan_bwd(
        merge_fn, nm("merge"), WIDE_TILE, [sv["ya"], sv["yb"], sv["yc"], sv["p_gate"]], lp["wb"], [], [d_merged], 4, 3,
        [F32, F32, F32, BF16])
    early = [jnp.stack(g_wb), g_w_out, g_w_up, g_w_down]
    token = on_early(l, early)
    lru_params = lp["lru"] if token is None else (lp["lru"][0], lp["lru"][1] + token[0, 0], *lp["lru"][2:])
    (d_hs, d_plg), _ = scan_bwd(lru_out_fn, nm("lru_out"), TOK_TILE, [sv["hs"], sv["p_lg"]], [], [], [d_yc], 2, 0,
                                [F32, BF16])
    d_a, d_u = linscan_bwd(sv["a"], sv["hs"], d_hs, nm("lru_scan"))
    (d_plx,), g_lru = scan_bwd(lru_in_fn, nm("lru_in"), TOK_TILE, [sv["p_lx"]], lru_params, sv["sv_lru"],
                               [d_a, d_u], 1, 7, [BF16])
    (d_pssd,), g_ssd = scan_bwd(ssd_fn, nm("ssd"), CHUNK, [sv["p_ssd"]], lp["ssd"], sv["sv_ssd"], [d_yb], 1, 6,
                                [BF16])
    (d_pgdn,), g_gdn = scan_bwd(gdn_fn, nm("gdn"), CHUNK, [sv["p_gdn"]], lp["gdn"], sv["sv_gdn"], [d_ya], 1, 4,
                                [BF16])
    groups = list(zip(("gdn", "ssd", "lx", "lg", "gate"), (d_pgdn, d_pssd, d_plx, d_plg, d_pgate), lp["w_in"]))
    g_w_in = _join_w_in(*[matmul(sv["h"], dp, nm("in_w_" + tag), ta=True) for tag, dp, _ in groups])
    token = on_last(l, g_w_in)
    bias = None if token is None else jnp.zeros((1, D_MODEL), F32) + token[0, 0]
    d_h = None
    for i, (tag, dp, w) in enumerate(groups):
        d_h = matmul(dp, w, nm("in_x_" + tag), tb=True, add=d_h, bias=bias if i == 0 else None)
    (d_x0,), (g_norm_mix, d_sc1, d_sh1) = scan_bwd(norm1_fn, nm("norm1"), TOK_TILE, [sv["x"]],
                                                   [lp["norm_mix"], sc1, sh1], [], [d_h, d_x], 1, 3)
    d_mod = jnp.concatenate([d_sh1, d_sc1, d_gt1, d_sh2, d_sc2, d_gt2], axis=1)
    flat = lambda v: v.reshape(-1)
    grads = dict(
        ada_b=flat(d_mod), norm_mix=flat(g_norm_mix),
        gdn_conv_w=g_gdn[0], gdn_a_log=g_gdn[1][0, 4:8], gdn_dt_bias=g_gdn[2][0, 4:8], gdn_norm=flat(g_gdn[3]),
        ssd_conv_w=g_ssd[0], ssd_conv_b=flat(g_ssd[1]), ssd_a_log=g_ssd[2][0, 0:8], ssd_dt_bias=g_ssd[3][0, 0:8],
        ssd_d=g_ssd[4].reshape(8, 64).sum(axis=1), ssd_norm=flat(g_ssd[5]),
        lru_conv_w=g_lru[0], lru_conv_b=flat(g_lru[1]), lru_w_a=_diag_blocks(g_lru[2]), lru_b_a=flat(g_lru[3]),
        lru_w_x=_diag_blocks(g_lru[4]), lru_b_x=flat(g_lru[5]), lru_lambda=flat(g_lru[6]),
        norm_mlp=flat(g_norm_mlp))
    big = [g_w_in, *early]
    return d_x0, grads, big


def local_step(x, c, target, p, big, on_big_grads, on_early, on_last):
    c8 = jnp.concatenate([c, jnp.zeros((7, c.shape[1]), F32)], axis=0)
    (silu_c,), _ = scan_fwd(silu_fn, "silu_c", 8, [c8], [], [], [(D_MODEL, F32)])
    lps, saved = [], []
    for l in range(DEPTH):
        lps.append(_layer_params(p, big[l](x), l))
        x, sv = _layer_fwd(x, silu_c, lps[l], l)
        saved.append(sv)
    loss, d_x, g_final = loss_head(x, target, p["final_norm"].reshape(1, -1), "loss_head")
    layer_grads, big_grads = [None] * DEPTH, [None] * DEPTH
    for l in reversed(range(DEPTH)):
        d_x, layer_grads[l], big_grads[l] = _layer_bwd(d_x, lps[l], saved[l], l, on_early, on_last)
        if l > 0:
            token = on_big_grads(l, big_grads[l])
            mod = saved[l - 1]["mod"]
            saved[l - 1]["mod"] = (*mod[:5], mod[5] + token[0, 0])
    grads = {k: jnp.stack([layer_grads[l][k] for l in range(DEPTH)]) for k in layer_grads[0]}
    grads["final_norm"] = g_final.reshape(-1)
    return loss, d_x, grads, big_grads, silu_c[0]


def _place_shard(shard, kind, full, chip):
    base = lax.empty(full, shard.dtype)
    if kind == "chip":
        return lax.dynamic_update_index_in_dim(base, shard, chip, axis=0)
    axis = 0 if kind == "row" else len(full) - 1
    return lax.dynamic_update_slice_in_dim(base, shard, chip * shard.shape[axis], axis=axis)


def _adam_nd(g, w, m, v, name):
    three = lambda a: a.reshape(-1, *a.shape[-2:])
    return tuple(r.reshape(w.shape) for r in adamw(three(g), three(w), three(m), three(v), name, copy_g=True))


EARLY_LAYOUT, LAST_LAYOUT = REDUCE_LAYOUT[1:], REDUCE_LAYOUT[:1]


def _core_sums(grads, recvs, layout, place, tag):
    out = []
    for (n, kind, full), g, r in zip(layout, grads, recvs):
        local = r.shape[1:]
        out.append(sum4(g.reshape(-1, full[-1]), r.reshape(3, -1, local[-1]), kind == "col", place,
                        f"reduce_sum_{tag}_{n}"))
    return out


def _reduce_total(sums0, sums1, shapes):
    theirs = swap_d2d(sums0 + sums1, "reduce_swap")
    nb = len(sums0)
    return [add_layers([sums0[n], sums1[n]], [theirs[n], theirs[nb + n]], f"reduce_total_{n}").reshape(2, *s)
            for n, s in enumerate(shapes)]


def _step(w, m, v, x, c, target):
    chip = 2 * lax.axis_index("x") + lax.axis_index("y")
    place = jnp.stack([chip, lax.axis_index("c")]).astype(jnp.int32)
    conv_shapes = [w[n].shape for n in CONV]
    small_shapes = [w[n].shape for n in SMALL]

    shards = [w[n].astype(BF16) for n, _, _ in BIG_LAYOUT]
    first0 = gather_layer([shards[0][0], shards[1][0]], BIG_LAYOUT[:2], ((1,), (0,)), "gather_l0")
    conv_all = allgather8(_pack([w[n] for n in CONV], F32), "gather_conv").reshape(8, -1, PACK_COLS)
    rest = [(s[0], e) for s, e in zip(shards[2:], BIG_LAYOUT[2:])] + [(s[1], e) for s, e in zip(shards, BIG_LAYOUT)]
    rest_layout = tuple(e for _, e in rest)
    own = [_place_shard(s, kind, full, chip) for s, (_, kind, full) in rest]
    (ssem, rsem, sent, landed), token = gather_start([s for s, _ in rest], own, rest_layout, [first0[0], conv_all],
                                                     "gather_rest_start")
    c = c + token[0, 0]
    n0 = N_BIG - 2

    def layer1(x_in):
        full1 = gather_wait(ssem, rsem, sent[n0:], landed[n0:], BIG_LAYOUT, n0, x_in, "gather_l1_wait")
        return full1[:2], lambda _: full1[2:]

    big = [lambda _: (first0, lambda after: gather_wait(ssem, rsem, sent[:n0], landed[:n0], BIG_LAYOUT[2:], 0, after,
                                                        "gather_l0_rest_wait")), layer1]
    conv_parts = [_unpack(conv_all[2 * k], conv_shapes) for k in range(4)]
    p = {n: w[n] for n in SMALL}
    for i, n in enumerate(CONV):
        p[n] = jnp.concatenate([conv_parts[k][i] for k in range(4)], axis=2)

    st = {}
    bf16 = lambda arrays: [a.astype(BF16) for a in arrays]

    def on_big_grads(l, grads_l):
        st["g1"] = grads_l
        st["f1"], token = reduce_start(bf16(grads_l), REDUCE_LAYOUT, [], "reduce_l1_start")
        return token

    def on_early(l, early):
        if l != 0:
            return None
        st["r1"] = reduce_wait(*st["f1"], REDUCE_LAYOUT, early[0], "reduce_l1_wait")
        st["fe"], token = reduce_start(bf16(early), EARLY_LAYOUT, [st["r1"][0]], "reduce_l0_start")
        return token

    def on_last(l, g_w_in):
        if l != 0:
            return None
        st["re"] = reduce_wait(*st["fe"], EARLY_LAYOUT, g_w_in, "reduce_l0_wait")
        st["fl"], token = reduce_start(bf16([g_w_in]), LAST_LAYOUT, [st["re"][0]], "reduce_w_in_start")
        return token

    loss_blk, grad_x, g, big_g, silu_c = local_step(x[0], c, target[0], p, big, on_big_grads, on_early, on_last)
    g0 = big_g[0]
    recv_last = reduce_wait(*st["fl"], LAST_LAYOUT, grad_x, "reduce_w_in_wait")
    sums1 = _core_sums(st["g1"], st["r1"], REDUCE_LAYOUT, place, "l1")
    sums0 = (_core_sums(g0[:1], recv_last, LAST_LAYOUT, place, "l0")
             + _core_sums(g0[1:], st["re"], EARLY_LAYOUT, place, "l0"))
    shapes = [_local_shape(kind, full) for _, kind, full in REDUCE_LAYOUT]
    big_g = dict(zip((n for n, _, _ in REDUCE_LAYOUT), _reduce_total(sums0, sums1, shapes)))

    assert SMALL[0] == "ada_b"
    small_pack = _pack([g["ada_b"], silu_c, loss_blk[0, 0:1]] + [g[n] for n in SMALL[1:]] + [g[n] for n in CONV], F32)
    small_all = allgather8(small_pack, "gather_small").reshape(8, -1, PACK_COLS)
    total = _unpack(add8(small_all, "reduce_small"),
                    [small_shapes[0], (D_MODEL,), (1,)] + small_shapes[1:] + [g[n].shape for n in CONV])
    loss = total[2][0]
    small_g = dict(zip(SMALL, [total[0]] + total[3:2 + len(SMALL)]))
    conv_g = {n: lax.dynamic_slice_in_dim(t, chip * w[n].shape[2], w[n].shape[2], axis=2)
              for n, t in zip(CONV, total[2 + len(SMALL):])}

    cols = w["ada_w"].shape[2]
    silu_all = small_all[:, _pack_rows(small_shapes[0]), :]
    big_g["ada_w"] = jnp.stack([
        matmul(silu_all, lax.dynamic_slice_in_dim(small_all[:, N_MOD * l:N_MOD * (l + 1), :].reshape(8, -1),
                                                  chip * cols, cols, axis=1), f"ada_w_grad{l}", ta=True)
        for l in range(DEPTH)])

    grad, delta, new_m, new_v = {}, {}, {}, {}
    for n, _, _ in BIG_LAYOUT:
        delta[n], new_m[n], new_v[n], grad[n] = _adam_nd(big_g[n], w[n], m[n], v[n], "adam_" + n)
    for names, gs, shapes, tag in ((SMALL, small_g, small_shapes, "small"), (CONV, conv_g, conv_shapes, "conv")):
        pk = lambda d: _pack([d[n] for n in names], F32)[None]
        res = adamw(pk(gs), pk(w), pk(m), pk(v), "adam_" + tag)
        for out, r in zip((delta, new_m, new_v), res):
            out.update(zip(names, _unpack(r[0], shapes)))
        grad.update({n: gs[n] for n in names})
    outs = [loss, grad_x[None]]
    for d in (grad, delta, new_m, new_v):
        outs += [d[n] for n in WEIGHTS]
    return tuple(outs)


def kernel(x, c, ada_w, ada_b, norm_mix, w_in, gdn_conv_w, gdn_a_log, gdn_dt_bias, gdn_norm, ssd_conv_w, ssd_conv_b, ssd_a_log, ssd_dt_bias, ssd_d, ssd_norm, lru_conv_w, lru_conv_b, lru_w_a, lru_b_a, lru_w_x, lru_b_x, lru_lambda, w_branch, w_out, norm_mlp, w_up, w_down, final_norm, loss_target, m_ada_w, m_ada_b, m_norm_mix, m_w_in, m_gdn_conv_w, m_gdn_a_log, m_gdn_dt_bias, m_gdn_norm, m_ssd_conv_w, m_ssd_conv_b, m_ssd_a_log, m_ssd_dt_bias, m_ssd_d, m_ssd_norm, m_lru_conv_w, m_lru_conv_b, m_lru_w_a, m_lru_b_a, m_lru_w_x, m_lru_b_x, m_lru_lambda, m_w_branch, m_w_out, m_norm_mlp, m_w_up, m_w_down, m_final_norm, v_ada_w, v_ada_b, v_norm_mix, v_w_in, v_gdn_conv_w, v_gdn_a_log, v_gdn_dt_bias, v_gdn_norm, v_ssd_conv_w, v_ssd_conv_b, v_ssd_a_log, v_ssd_dt_bias, v_ssd_d, v_ssd_norm, v_lru_conv_w, v_lru_conv_b, v_lru_w_a, v_lru_b_a, v_lru_w_x, v_lru_b_x, v_lru_lambda, v_w_branch, v_w_out, v_norm_mlp, v_w_up, v_w_down, v_final_norm):
    given = dict(locals())
    w = {n: given[n] for n in WEIGHTS}
    m = {n: given["m_" + n] for n in WEIGHTS}
    v = {n: given["v_" + n] for n in WEIGHTS}
    return _step(w, m, v, x, c, loss_target)
```

```python
import functools
import math

import jax
import jax.numpy as jnp
from jax import lax
from jax.experimental import pallas as pl
from jax.experimental.pallas import tpu as pltpu

F32 = jnp.float32
BF16 = jnp.bfloat16

D_MODEL = 1024
DEPTH = 2
RMS_EPS = 1e-6
CHUNK = 128
GDN_HEADS = 4
SSD_HEADS = 8
LRU_C = 8.0
D_FF = 4096
N_MOD = 6
W_GDN = 2176
W_SSD = 1664
W_LRU = 512
W_GATE = 3072
ADAM_LR = 0.001
ADAM_B1 = 0.9
ADAM_B2 = 0.999
ADAM_EPS = 1e-08
ADAM_WD = 0.01
ADAM_STEP = 10
VMEM_LIMIT = 56 * 1024 * 1024
MESH = pl.DeviceIdType.MESH


def _dot(a, b, ta, tb):
    dn = (((0 if ta else 1,), (1 if tb else 0,)), ((), ()))
    return lax.dot_general(a.astype(BF16), b.astype(BF16), dn, preferred_element_type=F32)


@functools.partial(jax.custom_vjp, nondiff_argnums=(2, 3))
def mm(a, b, ta, tb):
    return _dot(a, b, ta, tb)


def _mm_fwd(a, b, ta, tb):
    return _dot(a, b, ta, tb), (a, b)


def _mm_bwd(ta, tb, res, g):
    a, b = res
    if not ta and not tb:
        return mm(g, b, False, True), mm(a, g, True, False)
    if not ta and tb:
        return mm(g, b, False, False), mm(g, a, True, False)
    assert ta and not tb
    return mm(b, g, False, True), mm(a, g, False, False)


mm.defvjp(_mm_fwd, _mm_bwd)


def _tri_apply(x, upper):
    t = x.shape[0]
    r = lax.broadcasted_iota(jnp.int32, (t, t), 0)
    c = lax.broadcasted_iota(jnp.int32, (t, t), 1)
    tri = jnp.where((r <= c) if upper else (r >= c), 1.0, 0.0).astype(BF16)
    x1 = x.astype(BF16)
    r1 = x - x1.astype(F32)
    x2 = r1.astype(BF16)
    x3 = (r1 - x2.astype(F32)).astype(BF16)
    d = lambda p: jnp.dot(tri, p, preferred_element_type=F32)
    return (d(x1) + d(x2)) + d(x3)


@jax.custom_vjp
def cumsum_rows(x):
    return _tri_apply(x, False)


cumsum_rows.defvjp(lambda x: (_tri_apply(x, False), None), lambda _, g: (_tri_apply(g, True),))


def _dot_split(a, b):
    a1, b1 = a.astype(BF16), b.astype(BF16)
    a2, b2 = (a - a1.astype(F32)).astype(BF16), (b - b1.astype(F32)).astype(BF16)
    d = lambda p, q: jnp.dot(p, q, preferred_element_type=F32)
    return d(a1, b1) + (d(a1, b2) + d(a2, b1))


def _neumann(ms):
    t = ms[0].shape[0]
    xs = [-m for m in ms]
    qs = [_dot(m, m, False, False) for m in ms]
    n = 2
    while True:
        xs = [x + q + _dot(x, q, False, False) for x, q in zip(xs, qs)]
        n *= 2
        if n >= t:
            break
        qs = [_dot(q, q, False, False) for q in qs]
    rs = [-(x + m + _dot_split(m, x)) for x, m in zip(xs, ms)]
    return [x + r + _dot(x, r, False, False) for x, r in zip(xs, rs)]


@jax.custom_vjp
def tri_solve(ms, rhss):
    return tuple(rhs + _dot(x, rhs, False, False) for x, rhs in zip(_neumann(ms), rhss))


def _tri_solve_fwd(ms, rhss):
    xs = _neumann(ms)
    sols = tuple(rhs + _dot(x, rhs, False, False) for x, rhs in zip(xs, rhss))
    return sols, (tuple(xs), sols)


def _tri_solve_bwd(res, gs):
    xs, sols = res
    d_rhss = tuple(g + _dot(x, g, True, False) for x, g in zip(xs, gs))
    return tuple(-_dot(d, sol, False, True) for d, sol in zip(d_rhss, sols)), d_rhss


tri_solve.defvjp(_tri_solve_fwd, _tri_solve_bwd)


@functools.partial(jax.custom_vjp, nondiff_argnums=(1,))
def split_cols(x, sizes):
    out, o = [], 0
    for s in sizes:
        out.append(x[:, o:o + s])
        o += s
    return tuple(out)


split_cols.defvjp(lambda x, sizes: (split_cols(x, sizes), None),
                  lambda sizes, _, g: (jnp.concatenate(list(g), axis=1),))


@functools.partial(jax.custom_vjp, nondiff_argnums=(1,))
def _last_rows(x, t):
    return x[t - 8:, :]


_last_rows.defvjp(lambda x, t: (_last_rows(x, t), None),
                  lambda t, _, g: (jnp.concatenate([jnp.zeros((t - 8, g.shape[1]), g.dtype), g], axis=0),))


def last8(x):
    return _last_rows(x, x.shape[0])


def _shifted(xp, d, t):
    return (pltpu.roll(xp, d, 0) if d else xp)[8:8 + t, :]


@jax.custom_vjp
def conv4(x, tail, w):
    t = x.shape[0]
    xp = jnp.concatenate([tail, x], axis=0)
    return sum(_shifted(xp, 3 - k, t) * w[k:k + 1, :] for k in range(4))


def _conv4_fwd(x, tail, w):
    return conv4(x, tail, w), (x, tail, w)


def _conv4_bwd(res, g):
    x, tail, w = res
    t = x.shape[0]
    xp = jnp.concatenate([tail, x], axis=0)
    zero8 = jnp.zeros((8, g.shape[1]), g.dtype)
    d_xp = jnp.zeros_like(xp)
    d_w = []
    for k in range(4):
        gk = jnp.concatenate([zero8, g * w[k:k + 1, :]], axis=0)
        d_xp = d_xp + (pltpu.roll(gk, t + 8 - (3 - k), 0) if k < 3 else gk)
        d_w.append(jnp.sum(g * _shifted(xp, 3 - k, t), axis=0, keepdims=True))
    return d_xp[8:, :], d_xp[:8, :], jnp.concatenate(d_w, axis=0)


conv4.defvjp(_conv4_fwd, _conv4_bwd)


def _sigmoid(x):
    return 0.5 * (jnp.tanh(0.5 * x) + 1.0)


def _silu(x):
    return x * _sigmoid(x)


def _softplus(x):
    ax = jnp.where(x > 0, x, -x)
    return jnp.where(x > 0, x, 0.0) + jnp.log(1.0 + jnp.exp(-ax))


def _gelu(x):
    return 0.5 * x * (1.0 + jnp.tanh(math.sqrt(2.0 / math.pi) * (x + 0.044715 * (x * x * x))))


def _expm1(x):
    series = x * (1.0 + x * (0.5 + x * (1.0 / 6.0 + x * (1.0 / 24.0))))
    return jnp.where(jnp.abs(x) < 0.03, series, jnp.exp(x) - 1.0)


def _rms(x, w):
    return x * lax.rsqrt(jnp.mean(x * x, axis=-1, keepdims=True) + RMS_EPS) * w


def _lane_pick(x, j):
    lane = lax.broadcasted_iota(jnp.int32, (1, x.shape[1]), 1)
    return jnp.sum(jnp.where(lane == j, x, 0.0), axis=1, keepdims=True)


def _row_pick(x, j):
    row = lax.broadcasted_iota(jnp.int32, (x.shape[0], 1), 0)
    return jnp.sum(jnp.where(row == j, x, 0.0), axis=0, keepdims=True)


def gdn_fn(carry, seq, params):
    *states, tail = carry
    (tile,) = seq
    conv_w, alog_row, dtb_row, norm_w = params
    t = tile.shape[0]
    qkv_raw, z, sm = split_cols(tile, (1536, 512, 128))
    qkv = _silu(conv4(qkv_raw, tail, conv_w))
    parts = split_cols(qkv, (128,) * 12)
    zs = split_cols(z, (128,) * 4)
    lane = lax.broadcasted_iota(jnp.int32, (1, 128), 1)
    beta_all = _sigmoid(sm)
    g_all = jnp.where((lane >= 4) & (lane < 8), -jnp.exp(alog_row) * _softplus(sm + dtb_row), 0.0)
    gc_all = cumsum_rows(g_all)
    gr_all = gc_all.T
    gl_all = _row_pick(gc_all, t - 1)
    r = lax.broadcasted_iota(jnp.int32, (t, t), 0)
    c = lax.broadcasted_iota(jnp.int32, (t, t), 1)
    heads = range(GDN_HEADS)
    l2 = lambda a: a * lax.rsqrt(jnp.sum(a * a, axis=-1, keepdims=True) + RMS_EPS)
    qn = [l2(parts[h]) * (128.0 ** -0.5) for h in heads]
    kn = [l2(parts[4 + h]) for h in heads]
    beta = [_lane_pick(beta_all, h) for h in heads]
    gc = [_lane_pick(gc_all, 4 + h) for h in heads]
    gl = [_lane_pick(gl_all, 4 + h) for h in heads]
    decay = [jnp.exp(jnp.where(r >= c, gc[h] - _row_pick(gr_all, 4 + h), -1e30)) for h in heads]
    kk = [mm(kn[h], kn[h], False, True) for h in heads]
    qk = [mm(qn[h], kn[h], False, True) for h in heads]
    m = tuple(jnp.where(r > c, beta[h] * kk[h] * decay[h], 0.0) for h in heads)
    eg = [jnp.exp(gc[h]) for h in heads]
    rhs = tuple(jnp.concatenate([beta[h] * parts[8 + h], (beta[h] * eg[h]) * kn[h]], axis=1) for h in heads)
    uw = [split_cols(s, (128, 128)) for s in tri_solve(m, rhs)]
    ws = [mm(uw[h][1], states[h], False, False) for h in heads]
    qs = [mm(qn[h] * eg[h], states[h], False, False) for h in heads]
    v_new = [uw[h][0] - ws[h] for h in heads]
    o = [qs[h] + mm(qk[h] * decay[h], v_new[h], False, False) for h in heads]
    kv = [mm(kn[h] * jnp.exp(gl[h] - gc[h]), v_new[h], True, False) for h in heads]
    new_states = [states[h] * jnp.exp(gl[h]) + kv[h] for h in heads]
    outs = [_rms(o[h], norm_w) * _silu(zs[h]) for h in heads]
    return (*new_states, last8(qkv_raw)), (jnp.concatenate(outs, axis=1),)


def ssd_fn(carry, seq, params):
    *states, tail = carry
    (tile,) = seq
    conv_w, conv_b, alog_row, dtb_row, d_row, norm_w = params
    t = tile.shape[0]
    xbc_raw, z, sm = split_cols(tile, (1024, 512, 128))
    xbc = _silu(conv4(xbc_raw, tail, conv_w) + conv_b)
    x0, x1, x2, x3, b0, b1, c0, c1 = split_cols(xbc, (128,) * 8)
    xs, bs, cs = (x0, x1, x2, x3), (b0, b1), (c0, c1)
    ds = split_cols(d_row, (128,) * 4)
    lane = lax.broadcasted_iota(jnp.int32, (1, 128), 1)
    sub = lax.broadcasted_iota(jnp.int32, (128, 1), 0)
    low = lane < 64
    dt_all = jnp.where(lane < SSD_HEADS, _softplus(sm + dtb_row), 0.0)
    ac_all = cumsum_rows(dt_all * (-jnp.exp(alog_row)))
    ar_all = ac_all.T
    al_all = _row_pick(ac_all, t - 1)
    r = lax.broadcasted_iota(jnp.int32, (t, t), 0)
    c = lax.broadcasted_iota(jnp.int32, (t, t), 1)
    pairs, heads = range(4), range(SSD_HEADS)
    col = [_lane_pick(ac_all, h) for h in heads]
    last = [_lane_pick(al_all, h) for h in heads]
    dt = [_lane_pick(dt_all, h) for h in heads]
    lm = [jnp.exp(jnp.where(r >= c, col[h] - _row_pick(ar_all, h), -1e30)) for h in heads]
    cb = [mm(cs[g], bs[g], False, True) for g in range(2)]
    both = lambda a, b: jnp.where(low, a, b)
    xdt = [xs[p] * both(dt[2 * p], dt[2 * p + 1]) for p in pairs]
    y_off = [mm(cs[p // 2], states[p], False, True) for p in pairs]
    y_lo = [mm(cb[p // 2] * lm[2 * p], jnp.where(low, xdt[p], 0.0), False, False) for p in pairs]
    y_hi = [mm(cb[p // 2] * lm[2 * p + 1], jnp.where(low, 0.0, xdt[p]), False, False) for p in pairs]
    st = [mm(xdt[p] * both(jnp.exp(last[2 * p] - col[2 * p]), jnp.exp(last[2 * p + 1] - col[2 * p + 1])),
             bs[p // 2], True, False) for p in pairs]
    ys = [ds[p] * xs[p] + y_lo[p] + y_hi[p] + y_off[p] * both(jnp.exp(col[2 * p]), jnp.exp(col[2 * p + 1]))
          for p in pairs]
    new_states = [states[p] * jnp.where(sub < 64, jnp.exp(last[2 * p]), jnp.exp(last[2 * p + 1])) + st[p]
                  for p in pairs]
    gz = jnp.concatenate(ys, axis=1) * _silu(z)
    g0, g1 = split_cols(gz, (256, 256))
    n0, n1 = split_cols(norm_w, (256, 256))
    out = jnp.concatenate([_rms(g0, n0), _rms(g1, n1)], axis=1)
    return (*new_states, last8(xbc_raw)), (out,)


def lru_in_fn(carry, seq, params):
    (tail,) = carry
    (x,) = seq
    conv_w, conv_b, w_a, b_a, w_x, b_x, lam = params
    xc = conv4(x, tail, conv_w) + conv_b
    r = _sigmoid(mm(xc, w_a, False, False) + b_a)
    i = _sigmoid(mm(xc, w_x, False, False) + b_x)
    log_a = -LRU_C * r * _softplus(-lam)
    u = jnp.sqrt(-_expm1(2.0 * log_a)) * (i * xc)
    return (last8(x),), (jnp.exp(log_a), u)


def lru_out_fn(carry, seq, params):
    hs, gate = seq
    return (), (hs * _gelu(gate),)


def merge_fn(carry, seq, params):
    ya, yb, yc, gl = seq
    g = split_cols(_sigmoid(gl), (D_MODEL,) * 3)
    merged = sum(g[r] * mm(y, params[r], False, False) for r, y in enumerate((ya, yb, yc)))
    return (), (merged,)


def _adaln(x, w, sc, sh):
    return _rms(x, w) * (1.0 + sc) + sh


def norm1_fn(carry, seq, params):
    (x,) = seq
    return (), (_adaln(x, *params), x)


def resid_norm_fn(carry, seq, params):
    x, mix = seq
    gt, w, sc, sh = params
    x1 = x + gt * mix
    return (), (_adaln(x1, w, sc, sh), x1)


def resid_fn(carry, seq, params):
    x, dn = seq
    (gt,) = params
    return (), (x + gt * dn,)


def silu_fn(carry, seq, params):
    return (), (_silu(seq[0]),)


def _full_spec(a):
    nd = a.ndim
    return pl.BlockSpec(a.shape, lambda i: (0,) * nd)


def _cparams(*sem):
    return pltpu.CompilerParams(dimension_semantics=sem, vmem_limit_bytes=VMEM_LIMIT)


def scan_fwd(fn, name, tile, seqs, params, carry_shapes, outs, save_carry=False):
    rows = seqs[0].shape[0]
    tile = min(tile, rows)
    n = rows // tile
    ns, npar, nc, no = len(seqs), len(params), len(carry_shapes), len(outs)

    def body(*refs):
        seq_refs, refs = refs[:ns], refs[ns:]
        par_refs, refs = refs[:npar], refs[npar:]
        out_refs, refs = refs[:no], refs[no:]
        save_refs, refs = (refs[:nc], refs[nc:]) if save_carry else ((), refs)
        carry_refs = refs

        @pl.when(pl.program_id(0) == 0)
        def _():
            for cr in carry_refs:
                cr[...] = jnp.zeros_like(cr)

        carry = tuple(cr[...] for cr in carry_refs)
        for sr, cv in zip(save_refs, carry):
            sr[0] = cv
        new_carry, res = fn(carry, tuple(r[...].astype(F32) for r in seq_refs),
                            tuple(r[...].astype(F32) for r in par_refs))
        for r, v in zip(out_refs, res):
            r[...] = v.astype(r.dtype)
        for cr, v in zip(carry_refs, new_carry):
            cr[...] = v

    out_shape = [jax.ShapeDtypeStruct((rows, w), dt) for w, dt in outs]
    out_specs = [pl.BlockSpec((tile, w), lambda i: (i, 0)) for w, _ in outs]
    if save_carry:
        out_shape += [jax.ShapeDtypeStruct((n, *s), F32) for s in carry_shapes]
        out_specs += [pl.BlockSpec((1, *s), lambda i: (i, 0, 0)) for s in carry_shapes]
    res = pl.pallas_call(
        body, name=name, grid=(n,),
        in_specs=[pl.BlockSpec((tile, s.shape[1]), lambda i: (i, 0)) for s in seqs] + [_full_spec(p) for p in params],
        out_specs=out_specs, out_shape=out_shape,
        scratch_shapes=[pltpu.VMEM(s, F32) for s in carry_shapes],
        compiler_params=_cparams("arbitrary"),
    )(*seqs, *params)
    return res[:no], res[no:]


def scan_bwd(fn, name, tile, seqs, params, saved, douts, n_dseq, n_dpar, dseq_dtypes=None):
    dseq_dtypes = dseq_dtypes or [F32] * n_dseq
    rows = seqs[0].shape[0]
    tile = min(tile, rows)
    n = rows // tile
    ns, npar, nc, no = len(seqs), len(params), len(saved), len(douts)

    def body(*refs):
        seq_refs, refs = refs[:ns], refs[ns:]
        par_refs, refs = refs[:npar], refs[npar:]
        save_refs, refs = refs[:nc], refs[nc:]
        dout_refs, refs = refs[:no], refs[no:]
        dseq_refs, refs = refs[:n_dseq], refs[n_dseq:]
        dpar_refs, refs = refs[:n_dpar], refs[n_dpar:]
        dcarry_refs = refs

        @pl.when(pl.program_id(0) == 0)
        def _():
            for r in (*dpar_refs, *dcarry_refs):
                r[...] = jnp.zeros_like(r)

        carry = tuple(r[0] for r in save_refs)
        seq = tuple(r[...].astype(F32) for r in seq_refs)
        par = tuple(r[...].astype(F32) for r in par_refs)

        def f(carry, dseq, dpar):
            return fn(carry, (*dseq, *seq[n_dseq:]), (*dpar, *par[n_dpar:]))

        _, vjp = jax.vjp(f, carry, seq[:n_dseq], par[:n_dpar])
        d_carry, d_seq, d_par = vjp((tuple(r[...] for r in dcarry_refs),
                                     tuple(r[...].astype(F32) for r in dout_refs)))
        for r, v in zip(dseq_refs, d_seq):
            r[...] = v.astype(r.dtype)
        for r, v in zip(dpar_refs, d_par):
            r[...] += v
        for r, v in zip(dcarry_refs, d_carry):
            r[...] = v

    rev = lambda i: (n - 1 - i, 0)
    res = pl.pallas_call(
        body, name=name, grid=(n,),
        in_specs=([pl.BlockSpec((tile, s.shape[1]), rev) for s in seqs] + [_full_spec(p) for p in params]
                  + [pl.BlockSpec((1, *s.shape[1:]), lambda i: (n - 1 - i, 0, 0)) for s in saved]
                  + [pl.BlockSpec((tile, d.shape[1]), rev) for d in douts]),
        out_specs=([pl.BlockSpec((tile, s.shape[1]), rev) for s in seqs[:n_dseq]]
                   + [_full_spec(p) for p in params[:n_dpar]]),
        out_shape=([jax.ShapeDtypeStruct((rows, s.shape[1]), dt) for s, dt in zip(seqs[:n_dseq], dseq_dtypes)]
                   + [jax.ShapeDtypeStruct(p.shape, F32) for p in params[:n_dpar]]),
        scratch_shapes=[pltpu.VMEM(s.shape[1:], F32) for s in saved],
        compiler_params=_cparams("arbitrary"),
    )(*seqs, *params, *saved, *douts)
    return res[:n_dseq], res[n_dseq:]


def _tile_of(dim, pref):
    if dim <= pref:
        return dim
    best = max((t for t in range(128, pref + 1, 128) if dim % t == 0), default=None)
    if best is None or (best < 512 and dim <= 2304):
        return dim
    return best


def _row_tile(rows, pref):
    if rows <= pref:
        return rows
    return max(t for t in range(8, pref + 1, 8) if rows % t == 0)


def matmul(a, b, name, ta=False, tb=False, out_dtype=F32, add=None, bias=None, relu2=False, relu2_of=None,
           tm=1024, tn=2048, tk=2048):
    m, k = (a.shape[1], a.shape[0]) if ta else a.shape
    n = b.shape[0] if tb else b.shape[1]
    assert k == (b.shape[1] if tb else b.shape[0])
    tm, tn, tk = _tile_of(m, tm), _tile_of(n, tn), _tile_of(k, tk)
    nm, nn, nk = m // tm, n // tn, k // tk
    assert nk == 1 or (out_dtype == F32 and not relu2 and relu2_of is None)
    dn = (((0 if ta else 1,), (1 if tb else 0,)), ((), ()))
    has_add, has_bias, has_u = add is not None, bias is not None, relu2_of is not None
    n_inner = a.size * a.dtype.itemsize * (nn - 1) >= b.size * b.dtype.itemsize * (nm - 1)
    ij = (lambda g0, g1: (g0, g1)) if n_inner else (lambda g0, g1: (g1, g0))

    def body(*refs):
        a_ref, b_ref, refs = refs[0], refs[1], refs[2:]
        add_ref, refs = (refs[0], refs[1:]) if has_add else (None, refs)
        bias_ref, refs = (refs[0], refs[1:]) if has_bias else (None, refs)
        u_ref, refs = (refs[0], refs[1:]) if has_u else (None, refs)
        o_ref = refs[0]
        r = lax.dot_general(a_ref[...].astype(BF16), b_ref[...].astype(BF16), dn, preferred_element_type=F32)

        def first():
            v = r
            if has_add:
                v = v + add_ref[...]
            if has_bias:
                v = v + bias_ref[...]
            if has_u:
                v = v * (2.0 * jnp.maximum(u_ref[...], 0.0))
            o_ref[...] = v.astype(o_ref.dtype)
            if relu2:
                p = jnp.maximum(v, 0.0)
                refs[1][...] = (p * p).astype(BF16)

        if nk == 1:
            first()
        else:
            pl.when(pl.program_id(2) == 0)(first)

            @pl.when(pl.program_id(2) > 0)
            def _():
                o_ref[...] += r

    def spec(shape, fn):
        return pl.BlockSpec(shape, lambda g0, g1, l: fn(*ij(g0, g1), l))

    a_spec = spec((tk, tm), lambda i, j, l: (l, i)) if ta else spec((tm, tk), lambda i, j, l: (i, l))
    b_spec = spec((tn, tk), lambda i, j, l: (j, l)) if tb else spec((tk, tn), lambda i, j, l: (l, j))
    o_spec = spec((tm, tn), lambda i, j, l: (i, j))
    in_specs, args = [a_spec, b_spec], [a, b]
    if has_add:
        in_specs.append(o_spec)
        args.append(add)
    if has_bias:
        in_specs.append(spec((1, tn), lambda i, j, l: (0, j)))
        args.append(bias)
    if has_u:
        in_specs.append(o_spec)
        args.append(relu2_of)
    out_shape = [jax.ShapeDtypeStruct((m, n), out_dtype)] + ([jax.ShapeDtypeStruct((m, n), BF16)] if relu2 else [])
    res = pl.pallas_call(
        body, name=name, grid=(nm, nn, nk) if n_inner else (nn, nm, nk), in_specs=in_specs,
        out_specs=[o_spec] * len(out_shape), out_shape=out_shape,
        compiler_params=_cparams("parallel", "parallel", "arbitrary"),
    )(*args)
    return res if relu2 else res[0]


LIN_TILE = 512


def linscan_fwd(a, u, name):
    rows, w = a.shape
    tile = min(LIN_TILE, rows)

    def body(a_ref, u_ref, h_ref, hc):
        @pl.when(pl.program_id(0) == 0)
        def _():
            hc[...] = jnp.zeros_like(hc)

        row = lax.broadcasted_iota(jnp.int32, (8, 1), 0)

        def group(k, h_in):
            rows8 = pl.ds(pl.multiple_of(k * 8, 8), 8)
            pa, pu = a_ref[rows8, :], u_ref[rows8, :]
            for d in (1, 2, 4):
                pu = pu + pa * jnp.where(row >= d, pltpu.roll(pu, d, 0), 0.0)
                pa = pa * jnp.where(row >= d, pltpu.roll(pa, d, 0), 1.0)
            h_ref[rows8, :] = pa * h_in + pu
            return h_ref[pl.ds(k * 8 + 7, 1), :]

        hc[...] = lax.fori_loop(0, tile // 8, group, hc[...], unroll=4)

    spec = pl.BlockSpec((tile, w), lambda i: (i, 0))
    return pl.pallas_call(
        body, name=name, grid=(rows // tile,), in_specs=[spec, spec], out_specs=spec,
        out_shape=jax.ShapeDtypeStruct((rows, w), F32), scratch_shapes=[pltpu.VMEM((1, w), F32)],
        compiler_params=_cparams("arbitrary"),
    )(a, u)


def linscan_bwd(a, hs, dh, name):
    rows, w = a.shape
    tile = min(LIN_TILE, rows)
    n = rows // tile
    per = tile // 8

    def body(a_ref, h_ref, hprev_ref, dh_ref, da_ref, du_ref, cc):
        i = pl.program_id(0)

        @pl.when(i == 0)
        def _():
            cc[...] = jnp.zeros_like(cc)

        row = lax.broadcasted_iota(jnp.int32, (8, 1), 0)
        h_before = jnp.where(i == n - 1, 0.0, hprev_ref[7:8, :])

        def group(s, c_in):
            k = per - 1 - s
            rows8 = pl.ds(pl.multiple_of(k * 8, 8), 8)
            av, hv = a_ref[rows8, :], h_ref[rows8, :]
            pb = jnp.where(row < 7, pltpu.roll(av, 7, 0), 1.0)
            pg = dh_ref[rows8, :]
            for d in (1, 2, 4):
                pg = pg + pb * jnp.where(row < 8 - d, pltpu.roll(pg, 8 - d, 0), 0.0)
                pb = pb * jnp.where(row < 8 - d, pltpu.roll(pb, 8 - d, 0), 1.0)
            g = pg + pb * c_in
            du_ref[rows8, :] = g
            h_prev = jnp.where(k == 0, h_before, h_ref[pl.ds(jnp.maximum(k * 8 - 1, 0), 1), :])
            da_ref[rows8, :] = g * jnp.where(row >= 1, pltpu.roll(hv, 1, 0), h_prev)
            return a_ref[pl.ds(k * 8, 1), :] * du_ref[pl.ds(k * 8, 1), :]

        cc[...] = lax.fori_loop(0, per, group, cc[...], unroll=4)

    rev = pl.BlockSpec((tile, w), lambda i: (n - 1 - i, 0))
    prev = pl.BlockSpec((8, w), lambda i: (jnp.maximum((n - 1 - i) * per - 1, 0), 0))
    return pl.pallas_call(
        body, name=name, grid=(n,), in_specs=[rev, rev, prev, rev], out_specs=[rev, rev],
        out_shape=[jax.ShapeDtypeStruct((rows, w), F32)] * 2, scratch_shapes=[pltpu.VMEM((1, w), F32)],
        compiler_params=_cparams("arbitrary"),
    )(a, hs, hs, dh)


def loss_head(x, target, w, name):
    rows, d = x.shape
    tile = min(512, rows)

    def body(x_ref, t_ref, w_ref, loss_ref, dx_ref, dw_ref):
        @pl.when(pl.program_id(0) == 0)
        def _():
            loss_ref[...] = jnp.zeros_like(loss_ref)
            dw_ref[...] = jnp.zeros_like(dw_ref)

        tv = t_ref[...]

        def f(xv, wv):
            e = _rms(xv, wv) - tv
            return 0.5 * jnp.sum(jnp.mean(e * e, axis=-1, keepdims=True), axis=0, keepdims=True)

        val, vjp = jax.vjp(f, x_ref[...], w_ref[...])
        dxv, dwv = vjp(jnp.ones((1, 1), F32))
        loss_ref[...] += jnp.broadcast_to(val, loss_ref.shape)
        dx_ref[...] = dxv
        dw_ref[...] += dwv

    spec = pl.BlockSpec((tile, d), lambda i: (i, 0))
    return pl.pallas_call(
        body, name=name, grid=(rows // tile,), in_specs=[spec, spec, _full_spec(w)],
        out_specs=[pl.BlockSpec((8, 128), lambda i: (0, 0)), spec, _full_spec(w)],
        out_shape=[jax.ShapeDtypeStruct((8, 128), F32), jax.ShapeDtypeStruct((rows, d), F32),
                   jax.ShapeDtypeStruct(w.shape, F32)],
        compiler_params=_cparams("arbitrary"),
    )(x, target, w)


def adamw(g, w, m, v, name, copy_g=False):
    layers, rows, cols = g.shape
    tile = _row_tile(rows, 256)
    n_out = 4 if copy_g else 3

    def body(g_ref, w_ref, m_ref, v_ref, d_ref, nm_ref, nv_ref, *g_out):
        gv = g_ref[...]
        if copy_g:
            g_out[0][...] = gv
        nm = ADAM_B1 * m_ref[...] + (1.0 - ADAM_B1) * gv
        nv = ADAM_B2 * v_ref[...] + (1.0 - ADAM_B2) * (gv * gv)
        m_hat = nm / (1.0 - ADAM_B1 ** ADAM_STEP)
        v_hat = nv / (1.0 - ADAM_B2 ** ADAM_STEP)
        d_ref[...] = -ADAM_LR * (m_hat / (jnp.sqrt(v_hat) + ADAM_EPS) + ADAM_WD * w_ref[...])
        nm_ref[...] = nm
        nv_ref[...] = nv

    spec = pl.BlockSpec((None, tile, cols), lambda l, i: (l, i, 0))
    return pl.pallas_call(
        body, name=name, grid=(layers, rows // tile), in_specs=[spec] * 4, out_specs=[spec] * n_out,
        out_shape=[jax.ShapeDtypeStruct((layers, rows, cols), F32)] * n_out,
        compiler_params=_cparams("parallel", "parallel"),
    )(g, w, m, v)


def add_layers(mine, theirs, name):
    r, c = mine[0].shape
    tile = _row_tile(r, 256)
    nl = len(mine)

    def body(*refs):
        o_ref = refs[-1]
        for l in range(nl):
            o_ref[l] = refs[l][...] + refs[nl + l][...]

    spec = pl.BlockSpec((tile, c), lambda i: (i, 0))
    return pl.pallas_call(
        body, name=name, grid=(r // tile,), in_specs=[spec] * (2 * nl),
        out_specs=pl.BlockSpec((nl, tile, c), lambda i: (0, i, 0)),
        out_shape=jax.ShapeDtypeStruct((nl, r, c), F32), compiler_params=_cparams("parallel"),
    )(*mine, *theirs)


def sum4(own, recv, by_cols, place, name):
    _, r, c = recv.shape
    tile = _row_tile(r, 256)
    nt = r // tile
    own_map = (lambda i, k: (i, k[0])) if by_cols else (lambda i, k: (k[0] * nt + i, 0))

    def body(k_ref, own_ref, recv_ref, o_ref):
        o_ref[...] = ((own_ref[...] + recv_ref[0].astype(F32)) + recv_ref[1].astype(F32)) + recv_ref[2].astype(F32)

    return pl.pallas_call(
        body, name=name,
        grid_spec=pltpu.PrefetchScalarGridSpec(
            num_scalar_prefetch=1, grid=(nt,),
            in_specs=[pl.BlockSpec((tile, c), own_map), pl.BlockSpec((3, tile, c), lambda i, k: (0, i, 0))],
            out_specs=pl.BlockSpec((tile, c), lambda i, k: (i, 0))),
        out_shape=jax.ShapeDtypeStruct((r, c), F32),
        compiler_params=_cparams("arbitrary"),
    )(place, own, recv)


def add8(parts, name):
    _, rows, cols = parts.shape

    def body(p_ref, o_ref):
        acc = p_ref[0]
        for k in range(1, 8):
            acc = acc + p_ref[k]
        o_ref[...] = acc

    return pl.pallas_call(
        body, name=name, in_specs=[pl.BlockSpec(memory_space=pltpu.VMEM)],
        out_specs=pl.BlockSpec(memory_space=pltpu.VMEM),
        out_shape=jax.ShapeDtypeStruct((rows, cols), F32),
        compiler_params=pltpu.CompilerParams(vmem_limit_bytes=VMEM_LIMIT),
    )(parts)


def _place():
    return lax.axis_index("x"), lax.axis_index("y"), lax.axis_index("c")


def _other_chips(x, y):
    return [(1 - x, y), (x, 1 - y), (1 - x, 1 - y)]


_ANY = pl.BlockSpec(memory_space=pl.ANY)


BIG_LAYOUT = (("ada_w", "col", (1024, 6144)), ("w_in", "chip", (4, 1024, 1924)), ("w_branch", "col", (3, 512, 1024)),
              ("w_out", "row", (1024, 1024)), ("w_up", "col", (1024, 4096)), ("w_down", "row", (4096, 1024)))
N_BIG = len(BIG_LAYOUT)
REDUCE_LAYOUT = BIG_LAYOUT[1:]


def _local_shape(kind, full):
    if kind == "col":
        return (*full[:-1], full[-1] // 4)
    if kind == "row":
        return (full[0] // 4, *full[1:])
    return full[1:]


def _window(ref, kind, k, local):
    if kind == "chip":
        return ref.at[k]
    if kind == "row":
        return ref.at[pl.ds(pl.multiple_of(k * local[0], 8), local[0])]
    idx = (slice(None),) * (len(local) - 1) + (pl.ds(pl.multiple_of(k * local[-1], 128), local[-1]),)
    return ref.at[idx]


def _dma_call(body, name, n_in, out_shape, sems, aliases=None):
    return pl.pallas_call(
        body, name=name, in_specs=[_ANY] * n_in, out_specs=[_ANY] * len(out_shape), out_shape=out_shape,
        scratch_shapes=[pltpu.SemaphoreType.DMA((n,)) for n in sems],
        input_output_aliases=aliases or {},
        compiler_params=pltpu.CompilerParams(has_side_effects=True))


def _remote(src, dst, send_sem, recv_sem, to):
    return pltpu.make_async_remote_copy(src_ref=src, dst_ref=dst, send_sem=send_sem, recv_sem=recv_sem,
                                        device_id=to, device_id_type=MESH)


def gather_layer(shards, layout, per_core, name):
    nb = len(shards)
    locals_ = [_local_shape(kind, full) for _, kind, full in layout]

    def body(*refs):
        sh, full, (send_sems, recv_sems, local_sems, pass_send, pass_recv) = (
            refs[:nb], refs[nb:2 * nb], refs[2 * nb:])
        x, y, c = _place()
        me = 2 * x + y
        chips = _other_chips(x, y)
        win = lambda n, k: _window(full[n], layout[n][1], k, locals_[n])
        for cc in (0, 1):
            @pl.when(c == cc)
            def _():
                mine, sends = {}, []
                for n in per_core[cc]:
                    mine[n] = pltpu.make_async_copy(sh[n], win(n, me), local_sems.at[n])
                    mine[n].start()
                    for j, chip in enumerate(chips):
                        sends.append(_remote(sh[n], win(n, me), send_sems.at[3 * n + j], recv_sems.at[3 * n + j],
                                             (chip[0], chip[1], c)))
                        sends[-1].start()
                for n in per_core[cc]:
                    for j, chip in enumerate(chips):
                        _remote(sh[n], win(n, 2 * chip[0] + chip[1]), send_sems.at[3 * n + j],
                                recv_sems.at[3 * n + j], (chip[0], chip[1], c)).wait_recv()
                    mine[n].wait()
                    sends.append(_remote(full[n], full[n], pass_send.at[n], pass_recv.at[n], (x, y, 1 - c)))
                    sends[-1].start()
                for n in per_core[1 - cc]:
                    _remote(full[n], full[n], pass_send.at[n], pass_recv.at[n], (x, y, 1 - c)).wait_recv()
                for cp in sends:
                    cp.wait_send()

    out_shape = [jax.ShapeDtypeStruct(full, BF16) for _, _, full in layout]
    return _dma_call(body, name, nb, out_shape, (3 * nb, 3 * nb, nb, nb, nb))(*shards)


_HBM = pl.BlockSpec(memory_space=pltpu.HBM)
_SEM = pl.BlockSpec(memory_space=pltpu.SEMAPHORE)


def _hbm(a):
    return pltpu.with_memory_space_constraint(a, pltpu.HBM)


def _gather_copies(sh, full, layout, send_sems, recv_sems, first=0):
    locals_ = [_local_shape(kind, f) for _, kind, f in layout]
    x, y, c = _place()
    me = 2 * x + y
    pairs = []
    for n in range(len(sh)):
        win = lambda k: _window(full[n], layout[n][1], k, locals_[n])
        for j, chip in enumerate(_other_chips(x, y)):
            k = 3 * (first + n) + j
            mk = lambda dst: _remote(sh[n], dst, send_sems.at[k], recv_sems.at[k], (chip[0], chip[1], c))
            pairs.append((mk(win(me)), mk(win(2 * chip[0] + chip[1]))))
    return pairs


def gather_start(shards, fulls, layout, after, name):
    nb = len(shards)

    def body(*refs):
        sh, full = refs[:nb], refs[nb:2 * nb]
        send_sems, recv_sems = refs[2 * nb + len(after):2 * nb + len(after) + 2]
        for out, _ in _gather_copies(sh, full, layout, send_sems, recv_sems):
            out.start()
        refs[-1][...] = jnp.zeros_like(refs[-1])

    thru = [pltpu.HBM(a.shape, a.dtype) for a in (*shards, *fulls)]
    res = pl.pallas_call(
        body, name=name,
        out_shape=(pltpu.SemaphoreType.DMA((3 * nb,)), pltpu.SemaphoreType.DMA((3 * nb,)), *thru,
                   jax.ShapeDtypeStruct((8, 128), F32)),
        in_specs=[_HBM] * (2 * nb) + [_ANY] * len(after),
        out_specs=(_SEM, _SEM, *[_HBM] * (2 * nb), pl.BlockSpec(memory_space=pltpu.VMEM)),
        input_output_aliases={i: 2 + i for i in range(2 * nb)},
        compiler_params=pltpu.CompilerParams(has_side_effects=pltpu.SideEffectType.DATAFLOW_SIDE_EFFECTING),
    )(*[_hbm(a) for a in (*shards, *fulls)], *after)
    return (res[0], res[1], res[2:2 + nb], res[2 + nb:2 + 2 * nb]), res[-1]


def gather_wait(send_sems, recv_sems, shards, fulls, layout, first, after, name):
    nb = len(shards)

    def body(*refs):
        sh, full = refs[:nb], refs[nb:2 * nb]
        ssem, rsem = refs[2 * nb:2 * nb + 2]
        for out, inc in _gather_copies(sh, full, layout, ssem, rsem, first):
            out.wait_send()
            inc.wait_recv()

    thru = [pltpu.HBM(a.shape, a.dtype) for a in (*shards, *fulls)]
    res = pl.pallas_call(
        body, name=name, out_shape=thru,
        in_specs=[_HBM] * (2 * nb) + [_SEM, _SEM, _ANY], out_specs=[_HBM] * (2 * nb),
        input_output_aliases={i: i for i in range(2 * nb)},
        compiler_params=pltpu.CompilerParams(has_side_effects=pltpu.SideEffectType.DATAFLOW_SIDE_EFFECTING),
    )(*shards, *fulls, send_sems, recv_sems, after)
    return res[nb:]


def swap_d2d(arrays, name):
    nb = len(arrays)

    def body(*refs):
        src, got, (send_sems, recv_sems) = refs[:nb], refs[nb:2 * nb], refs[2 * nb:]
        x, y, c = _place()
        copies = [_remote(src[n], got[n], send_sems.at[n], recv_sems.at[n], (x, y, 1 - c)) for n in range(nb)]
        for cp in copies:
            cp.start()
        for cp in copies:
            cp.wait_recv()
        for cp in copies:
            cp.wait_send()

    out_shape = [jax.ShapeDtypeStruct(a.shape, a.dtype) for a in arrays]
    return _dma_call(body, name, nb, out_shape, (nb, nb))(*arrays)


def _reduce_copies(src, recv, layout, send_sems, recv_sems):
    locals_ = [_local_shape(kind, full) for _, kind, full in layout]
    x, y, c = _place()
    copies = []
    for n in range(len(src)):
        for j, chip in enumerate(_other_chips(x, y)):
            copies.append(_remote(_window(src[n], layout[n][1], 2 * chip[0] + chip[1], locals_[n]), recv[n].at[j],
                                  send_sems.at[3 * n + j], recv_sems.at[3 * n + j], (chip[0], chip[1], c)))
    return copies


def _recv_shapes(sums, layout):
    return [(3, *_local_shape(kind, full)) for _, kind, full in layout]


def reduce_ici(sums, layout, name):
    nb = len(sums)

    def body(*refs):
        src, recv, (send_sems, recv_sems) = refs[:nb], refs[nb:2 * nb], refs[2 * nb:]
        copies = _reduce_copies(src, recv, layout, send_sems, recv_sems)
        for cp in copies:
            cp.start()
        for cp in copies:
            cp.wait_recv()
        for cp in copies:
            cp.wait_send()

    out_shape = [jax.ShapeDtypeStruct(s, a.dtype) for s, a in zip(_recv_shapes(sums, layout), sums)]
    return _dma_call(body, name, nb, out_shape, (3 * nb, 3 * nb))(*sums)


def reduce_start(sums, layout, after, name):
    nb = len(sums)
    lands = [lax.empty(s, a.dtype) for s, a in zip(_recv_shapes(sums, layout), sums)]

    def body(*refs):
        src, recv = refs[:nb], refs[nb:2 * nb]
        send_sems, recv_sems = refs[2 * nb + len(after):2 * nb + len(after) + 2]
        for cp in _reduce_copies(src, recv, layout, send_sems, recv_sems):
            cp.start()
        refs[-1][...] = jnp.zeros_like(refs[-1])

    thru = [pltpu.HBM(a.shape, a.dtype) for a in (*sums, *lands)]
    res = pl.pallas_call(
        body, name=name,
        out_shape=(pltpu.SemaphoreType.DMA((3 * nb,)), pltpu.SemaphoreType.DMA((3 * nb,)), *thru,
                   jax.ShapeDtypeStruct((8, 128), F32)),
        in_specs=[_HBM] * (2 * nb) + [_ANY] * len(after),
        out_specs=(_SEM, _SEM, *[_HBM] * (2 * nb), pl.BlockSpec(memory_space=pltpu.VMEM)),
        input_output_aliases={i: 2 + i for i in range(2 * nb)},
        compiler_params=pltpu.CompilerParams(has_side_effects=pltpu.SideEffectType.DATAFLOW_SIDE_EFFECTING),
    )(*[_hbm(a) for a in (*sums, *lands)], *after)
    return (res[0], res[1], res[2:2 + nb], res[2 + nb:2 + 2 * nb]), res[-1]


def reduce_wait(send_sems, recv_sems, sums, lands, layout, after, name):
    nb = len(sums)

    def body(*refs):
        src, recv = refs[:nb], refs[nb:2 * nb]
        ssem, rsem = refs[2 * nb:2 * nb + 2]
        for cp in _reduce_copies(src, recv, layout, ssem, rsem):
            cp.wait_send()
            cp.wait_recv()

    thru = [pltpu.HBM(a.shape, a.dtype) for a in (*sums, *lands)]
    res = pl.pallas_call(
        body, name=name, out_shape=thru,
        in_specs=[_HBM] * (2 * nb) + [_SEM, _SEM, _ANY], out_specs=[_HBM] * (2 * nb),
        input_output_aliases={i: i for i in range(2 * nb)},
        compiler_params=pltpu.CompilerParams(has_side_effects=pltpu.SideEffectType.DATAFLOW_SIDE_EFFECTING),
    )(*sums, *lands, send_sems, recv_sems, after)
    return res[nb:]


def allgather8(block, name):
    m_per, n = block.shape

    def body(x_ref, out_ref, send_sems, recv_sems, local_sem):
        x, y, c = _place()
        me, sibling = (x, y, c), (x, y, 1 - c)
        chips = _other_chips(x, y)

        def rows(px, py, pc):
            return out_ref.at[pl.ds((4 * px + 2 * py + pc) * m_per, m_per), :]

        def copy(k, blk, to, src=None):
            return pltpu.make_async_remote_copy(
                src_ref=rows(*blk) if src is None else src, dst_ref=rows(*blk), send_sem=send_sems.at[k],
                recv_sem=recv_sems.at[k], device_id=to, device_id_type=MESH)

        mine = pltpu.make_async_copy(x_ref, rows(*me), local_sem)
        mine.start()
        first = [copy(0, me, sibling, src=x_ref)]
        first += [copy(1 + j, me, (*chip, c), src=x_ref) for j, chip in enumerate(chips)]
        for cp in first:
            cp.start()
        passed = [copy(4 + j, (*chip, c), sibling) for j, chip in enumerate(chips)]
        for j, chip in enumerate(chips):
            copy(1 + j, (*chip, c), me).wait_recv()
            passed[j].start()
        copy(0, sibling, me).wait_recv()
        for j, chip in enumerate(chips):
            copy(4 + j, (*chip, 1 - c), me).wait_recv()
        for cp in first + passed:
            cp.wait_send()
        mine.wait()

    return pl.pallas_call(
        body, name=name, in_specs=[pl.BlockSpec(memory_space=pltpu.VMEM)],
        out_specs=pl.BlockSpec(memory_space=pltpu.VMEM),
        out_shape=jax.ShapeDtypeStruct((8 * m_per, n), block.dtype),
        scratch_shapes=[pltpu.SemaphoreType.DMA((7,)), pltpu.SemaphoreType.DMA((7,)), pltpu.SemaphoreType.DMA],
        compiler_params=pltpu.CompilerParams(vmem_limit_bytes=VMEM_LIMIT),
    )(block)


CONV = ("gdn_conv_w", "ssd_conv_w", "lru_conv_w")
SMALL = ("ada_b", "norm_mix", "gdn_a_log", "gdn_dt_bias", "gdn_norm", "ssd_conv_b", "ssd_a_log", "ssd_dt_bias",
         "ssd_d", "ssd_norm", "lru_conv_b", "lru_w_a", "lru_b_a", "lru_w_x", "lru_b_x", "lru_lambda", "norm_mlp",
         "final_norm")
WEIGHTS = ("ada_w", "ada_b", "norm_mix", "w_in", "gdn_conv_w", "gdn_a_log", "gdn_dt_bias", "gdn_norm", "ssd_conv_w",
           "ssd_conv_b", "ssd_a_log", "ssd_dt_bias", "ssd_d", "ssd_norm", "lru_conv_w", "lru_conv_b", "lru_w_a",
           "lru_b_a", "lru_w_x", "lru_b_x", "lru_lambda", "w_branch", "w_out", "norm_mlp", "w_up", "w_down",
           "final_norm")
PACK_COLS = 1024


def _pack_rows(shape):
    return 8 * -(-math.prod(shape) // (8 * PACK_COLS))


def _pack(arrays, dtype):
    parts = []
    for a in arrays:
        flat = a.reshape(-1).astype(dtype)
        pad = _pack_rows(a.shape) * PACK_COLS - flat.shape[0]
        parts.append((jnp.concatenate([flat, jnp.zeros((pad,), dtype)]) if pad else flat).reshape(-1, PACK_COLS))
    return jnp.concatenate(parts, axis=0)


def _unpack(pack, shapes):
    out, o = [], 0
    for s in shapes:
        r = _pack_rows(s)
        out.append(pack[o:o + r].reshape(-1)[:math.prod(s)].reshape(s))
        o += r
    return out


def _split_w_in(w4):
    w = jnp.concatenate([w4[k] for k in range(4)], axis=1)
    pad = jnp.zeros((w.shape[0], 120), w.dtype)
    gdn = jnp.concatenate([w[:, 0:2056], pad], axis=1)
    ssd = jnp.concatenate([w[:, 2056:2568], w[:, 3080:3592], w[:, 2568:3080], w[:, 3592:3600], pad], axis=1)
    return gdn, ssd, w[:, 3600:4112], w[:, 4112:4624], w[:, 4624:7696]


def _join_w_in(gdn, ssd, lx, lg, gate):
    w = jnp.concatenate([gdn[:, 0:2056], ssd[:, 0:512], ssd[:, 1024:1536], ssd[:, 512:1024], ssd[:, 1536:1544],
                         lx, lg, gate], axis=1)
    return jnp.stack([w[:, k * 1924:(k + 1) * 1924] for k in range(4)])


def _lanes(v, at, width=128):
    return jnp.concatenate([jnp.zeros((at,), F32), v, jnp.zeros((width - at - v.shape[0],), F32)]).reshape(1, width)


def _block_diag(w):
    return (jnp.eye(8, dtype=w.dtype)[:, None, :, None] * w[:, :, None, :]).reshape(512, 512)


def _diag_blocks(w):
    return jnp.stack([w[n * 64:(n + 1) * 64, n * 64:(n + 1) * 64] for n in range(8)])


TOK_TILE = 512
WIDE_TILE = 256


def _layer_params(p, big, l):
    row = lambda v: v.reshape(1, -1)
    (ada_w, w_in), late = big
    gdn = (p["gdn_conv_w"][l], _lanes(p["gdn_a_log"][l], 4), _lanes(p["gdn_dt_bias"][l], 4), row(p["gdn_norm"][l]))
    ssd = (p["ssd_conv_w"][l], row(p["ssd_conv_b"][l]), _lanes(p["ssd_a_log"][l], 0), _lanes(p["ssd_dt_bias"][l], 0),
           row(jnp.repeat(p["ssd_d"][l], 64)), row(p["ssd_norm"][l]))
    lru = (p["lru_conv_w"][l], row(p["lru_conv_b"][l]), _block_diag(p["lru_w_a"][l]), row(p["lru_b_a"][l]),
           _block_diag(p["lru_w_x"][l]), row(p["lru_b_x"][l]), row(p["lru_lambda"][l]))
    return dict(gdn=gdn, ssd=ssd, lru=lru, w_in=_split_w_in(w_in), late=late, ada_w=ada_w, ada_b=row(p["ada_b"][l]),
                norm_mix=row(p["norm_mix"][l]), norm_mlp=row(p["norm_mlp"][l]))


def _layer_fwd(x, silu_c, lp, l):
    nm = lambda s: f"l{l}_{s}"
    mod = matmul(silu_c, lp["ada_w"], nm("mod"), bias=lp["ada_b"])
    sh1, sc1, gt1, sh2, sc2, gt2 = (mod[0:1, k * D_MODEL:(k + 1) * D_MODEL] for k in range(N_MOD))
    (h,), _ = scan_fwd(norm1_fn, nm("norm1"), TOK_TILE, [x], [lp["norm_mix"], sc1, sh1], [], [(D_MODEL, BF16)])
    w_gdn, w_ssd, w_lx, w_lg, w_gate = lp["w_in"]
    p_gdn = matmul(h, w_gdn, nm("in_gdn"))
    p_ssd = matmul(h, w_ssd, nm("in_ssd"))
    p_lx = matmul(h, w_lx, nm("in_lx"))
    p_lg = matmul(h, w_lg, nm("in_lg"))
    p_gate = matmul(h, w_gate, nm("in_gate"))
    (ya,), sv_gdn = scan_fwd(gdn_fn, nm("gdn"), CHUNK, [p_gdn], lp["gdn"], [(128, 128)] * 4 + [(8, 1536)],
                             [(512, F32)], save_carry=True)
    (yb,), sv_ssd = scan_fwd(ssd_fn, nm("ssd"), CHUNK, [p_ssd], lp["ssd"], [(128, 128)] * 4 + [(8, 1024)],
                             [(512, F32)], save_carry=True)
    (a, u), sv_lru = scan_fwd(lru_in_fn, nm("lru_in"), TOK_TILE, [p_lx], lp["lru"], [(8, 512)],
                              [(512, F32), (512, F32)], save_carry=True)
    hs = linscan_fwd(a, u, nm("lru_scan"))
    (yc,), _ = scan_fwd(lru_out_fn, nm("lru_out"), TOK_TILE, [hs, p_lg], [], [], [(512, F32)])
    w_branch, lp["w_out"], lp["w_up"], lp["w_down"] = lp.pop("late")(yc)
    lp["wb"] = tuple(w_branch[r] for r in range(3))
    (merged,), _ = scan_fwd(merge_fn, nm("merge"), WIDE_TILE, [ya, yb, yc, p_gate], lp["wb"], [], [(D_MODEL, BF16)])
    mix = matmul(merged, lp["w_out"], nm("out"))
    (h2, x1), _ = scan_fwd(resid_norm_fn, nm("norm2"), TOK_TILE, [x, mix], [gt1, lp["norm_mlp"], sc2, sh2], [],
                           [(D_MODEL, BF16), (D_MODEL, F32)])
    up, act = matmul(h2, lp["w_up"], nm("up"), relu2=True)
    dn = matmul(act, lp["w_down"], nm("down"))
    (x2,), _ = scan_fwd(resid_fn, nm("resid"), TOK_TILE, [x1, dn], [gt2], [], [(D_MODEL, F32)])
    saved = dict(x=x, h=h, p_gdn=p_gdn, p_ssd=p_ssd, p_lx=p_lx, p_lg=p_lg, p_gate=p_gate, sv_gdn=sv_gdn,
                 sv_ssd=sv_ssd, sv_lru=sv_lru, a=a, hs=hs, ya=ya, yb=yb, yc=yc, merged=merged, mix=mix, x1=x1,
                 h2=h2, up=up, act=act, dn=dn, mod=(sh1, sc1, gt1, sh2, sc2, gt2))
    return x2, saved


def _layer_bwd(d_x2, lp, sv, l, on_early, on_last):
    nm = lambda s: f"l{l}_b_{s}"
    sh1, sc1, gt1, sh2, sc2, gt2 = sv["mod"]
    (d_x1, d_dn), (d_gt2,) = scan_bwd(resid_fn, nm("resid"), TOK_TILE, [sv["x1"], sv["dn"]], [gt2], [], [d_x2], 2, 1,
                                      [F32, BF16])
    d_up = matmul(d_dn, lp["w_down"], nm("down_x"), tb=True, relu2_of=sv["up"], out_dtype=BF16)
    g_w_down = matmul(sv["act"], d_dn, nm("down_w"), ta=True)
    d_h2 = matmul(d_up, lp["w_up"], nm("up_x"), tb=True)
    g_w_up = matmul(sv["h2"], d_up, nm("up_w"), ta=True)
    (d_x, d_mix), (d_gt1, g_norm_mlp, d_sc2, d_sh2) = scan_bwd(
        resid_norm_fn, nm("norm2"), TOK_TILE, [sv["x"], sv["mix"]], [gt1, lp["norm_mlp"], sc2, sh2], [],
        [d_h2, d_x1], 2, 4, [F32, BF16])
    d_merged = matmul(d_mix, lp["w_out"], nm("out_x"), tb=True)
    g_w_out = matmul(sv["merged"], d_mix, nm("out_w"), ta=True)
    (d_ya, d_yb, d_yc, d_pgate), g_wb = scan_bwd(
        merge_fn, nm("merge"), WIDE_TILE, [sv["ya"], sv["yb"], sv["yc"], sv["p_gate"]], lp["wb"], [], [d_merged], 4, 3,
        [F32, F32, F32, BF16])
    early = [jnp.stack(g_wb), g_w_out, g_w_up, g_w_down]
    token = on_early(l, early)
    lru_params = lp["lru"] if token is None else (lp["lru"][0], lp["lru"][1] + token[0, 0], *lp["lru"][2:])
    (d_hs, d_plg), _ = scan_bwd(lru_out_fn, nm("lru_out"), TOK_TILE, [sv["hs"], sv["p_lg"]], [], [], [d_yc], 2, 0,
                                [F32, BF16])
    d_a, d_u = linscan_bwd(sv["a"], sv["hs"], d_hs, nm("lru_scan"))
    (d_plx,), g_lru = scan_bwd(lru_in_fn, nm("lru_in"), TOK_TILE, [sv["p_lx"]], lru_params, sv["sv_lru"],
                               [d_a, d_u], 1, 7, [BF16])
    (d_pssd,), g_ssd = scan_bwd(ssd_fn, nm("ssd"), CHUNK, [sv["p_ssd"]], lp["ssd"], sv["sv_ssd"], [d_yb], 1, 6,
                                [BF16])
    (d_pgdn,), g_gdn = scan_bwd(gdn_fn, nm("gdn"), CHUNK, [sv["p_gdn"]], lp["gdn"], sv["sv_gdn"], [d_ya], 1, 4,
                                [BF16])
    groups = list(zip(("gdn", "ssd", "lx", "lg", "gate"), (d_pgdn, d_pssd, d_plx, d_plg, d_pgate), lp["w_in"]))
    g_w_in = _join_w_in(*[matmul(sv["h"], dp, nm("in_w_" + tag), ta=True) for tag, dp, _ in groups])
    token = on_last(l, g_w_in)
    bias = None if token is None else jnp.zeros((1, D_MODEL), F32) + token[0, 0]
    d_h = None
    for i, (tag, dp, w) in enumerate(groups):
        d_h = matmul(dp, w, nm("in_x_" + tag), tb=True, add=d_h, bias=bias if i == 0 else None)
    (d_x0,), (g_norm_mix, d_sc1, d_sh1) = scan_bwd(norm1_fn, nm("norm1"), TOK_TILE, [sv["x"]],
                                                   [lp["norm_mix"], sc1, sh1], [], [d_h, d_x], 1, 3)
    d_mod = jnp.concatenate([d_sh1, d_sc1, d_gt1, d_sh2, d_sc2, d_gt2], axis=1)
    flat = lambda v: v.reshape(-1)
    grads = dict(
        ada_b=flat(d_mod), norm_mix=flat(g_norm_mix),
        gdn_conv_w=g_gdn[0], gdn_a_log=g_gdn[1][0, 4:8], gdn_dt_bias=g_gdn[2][0, 4:8], gdn_norm=flat(g_gdn[3]),
        ssd_conv_w=g_ssd[0], ssd_conv_b=flat(g_ssd[1]), ssd_a_log=g_ssd[2][0, 0:8], ssd_dt_bias=g_ssd[3][0, 0:8],
        ssd_d=g_ssd[4].reshape(8, 64).sum(axis=1), ssd_norm=flat(g_ssd[5]),
        lru_conv_w=g_lru[0], lru_conv_b=flat(g_lru[1]), lru_w_a=_diag_blocks(g_lru[2]), lru_b_a=flat(g_lru[3]),
        lru_w_x=_diag_blocks(g_lru[4]), lru_b_x=flat(g_lru[5]), lru_lambda=flat(g_lru[6]),
        norm_mlp=flat(g_norm_mlp))
    big = [g_w_in, *early]
    return d_x0, grads, big


def local_step(x, c, target, p, big, on_big_grads, on_early, on_last):
    c8 = jnp.concatenate([c, jnp.zeros((7, c.shape[1]), F32)], axis=0)
    (silu_c,), _ = scan_fwd(silu_fn, "silu_c", 8, [c8], [], [], [(D_MODEL, F32)])
    lps, saved = [], []
    for l in range(DEPTH):
        lps.append(_layer_params(p, big[l](x), l))
        x, sv = _layer_fwd(x, silu_c, lps[l], l)
        saved.append(sv)
    loss, d_x, g_final = loss_head(x, target, p["final_norm"].reshape(1, -1), "loss_head")
    layer_grads, big_grads = [None] * DEPTH, [None] * DEPTH
    for l in reversed(range(DEPTH)):
        d_x, layer_grads[l], big_grads[l] = _layer_bwd(d_x, lps[l], saved[l], l, on_early, on_last)
        if l > 0:
            token = on_big_grads(l, big_grads[l])
            mod = saved[l - 1]["mod"]
            saved[l - 1]["mod"] = (*mod[:5], mod[5] + token[0, 0])
    grads = {k: jnp.stack([layer_grads[l][k] for l in range(DEPTH)]) for k in layer_grads[0]}
    grads["final_norm"] = g_final.reshape(-1)
    return loss, d_x, grads, big_grads, silu_c[0]


def _place_shard(shard, kind, full, chip):
    base = lax.empty(full, shard.dtype)
    if kind == "chip":
        return lax.dynamic_update_index_in_dim(base, shard, chip, axis=0)
    axis = 0 if kind == "row" else len(full) - 1
    return lax.dynamic_update_slice_in_dim(base, shard, chip * shard.shape[axis], axis=axis)


def _adam_nd(g, w, m, v, name):
    three = lambda a: a.reshape(-1, *a.shape[-2:])
    return tuple(r.reshape(w.shape) for r in adamw(three(g), three(w), three(m), three(v), name, copy_g=True))


EARLY_LAYOUT, LAST_LAYOUT = REDUCE_LAYOUT[1:], REDUCE_LAYOUT[:1]


def _core_sums(grads, recvs, layout, place, tag):
    out = []
    for (n, kind, full), g, r in zip(layout, grads, recvs):
        local = r.shape[1:]
        out.append(sum4(g.reshape(-1, full[-1]), r.reshape(3, -1, local[-1]), kind == "col", place,
                        f"reduce_sum_{tag}_{n}"))
    return out


def _reduce_total(sums0, sums1, shapes):
    theirs = swap_d2d(sums0 + sums1, "reduce_swap")
    nb = len(sums0)
    return [add_layers([sums0[n], sums1[n]], [theirs[n], theirs[nb + n]], f"reduce_total_{n}").reshape(2, *s)
            for n, s in enumerate(shapes)]


def _step(w, m, v, x, c, target):
    chip = 2 * lax.axis_index("x") + lax.axis_index("y")
    place = jnp.stack([chip, lax.axis_index("c")]).astype(jnp.int32)
    conv_shapes = [w[n].shape for n in CONV]
    small_shapes = [w[n].shape for n in SMALL]

    shards = [w[n].astype(BF16) for n, _, _ in BIG_LAYOUT]
    first0 = gather_layer([shards[0][0], shards[1][0]], BIG_LAYOUT[:2], ((1,), (0,)), "gather_l0")
    conv_all = allgather8(_pack([w[n] for n in CONV], F32), "gather_conv").reshape(8, -1, PACK_COLS)
    rest = [(s[0], e) for s, e in zip(shards[2:], BIG_LAYOUT[2:])] + [(s[1], e) for s, e in zip(shards, BIG_LAYOUT)]
    rest_layout = tuple(e for _, e in rest)
    own = [_place_shard(s, kind, full, chip) for s, (_, kind, full) in rest]
    (ssem, rsem, sent, landed), token = gather_start([s for s, _ in rest], own, rest_layout, [first0[0], conv_all],
                                                     "gather_rest_start")
    c = c + token[0, 0]
    n0 = N_BIG - 2

    def layer1(x_in):
        full1 = gather_wait(ssem, rsem, sent[n0:], landed[n0:], BIG_LAYOUT, n0, x_in, "gather_l1_wait")
        return full1[:2], lambda _: full1[2:]

    big = [lambda _: (first0, lambda after: gather_wait(ssem, rsem, sent[:n0], landed[:n0], BIG_LAYOUT[2:], 0, after,
                                                        "gather_l0_rest_wait")), layer1]
    conv_parts = [_unpack(conv_all[2 * k], conv_shapes) for k in range(4)]
    p = {n: w[n] for n in SMALL}
    for i, n in enumerate(CONV):
        p[n] = jnp.concatenate([conv_parts[k][i] for k in range(4)], axis=2)

    st = {}
    bf16 = lambda arrays: [a.astype(BF16) for a in arrays]

    def on_big_grads(l, grads_l):
        st["g1"] = grads_l
        st["f1"], token = reduce_start(bf16(grads_l), REDUCE_LAYOUT, [], "reduce_l1_start")
        return token

    def on_early(l, early):
        if l != 0:
            return None
        st["r1"] = reduce_wait(*st["f1"], REDUCE_LAYOUT, early[0], "reduce_l1_wait")
        st["fe"], token = reduce_start(bf16(early), EARLY_LAYOUT, [st["r1"][0]], "reduce_l0_start")
        return token

    def on_last(l, g_w_in):
        if l != 0:
            return None
        st["re"] = reduce_wait(*st["fe"], EARLY_LAYOUT, g_w_in, "reduce_l0_wait")
        st["fl"], token = reduce_start(bf16([g_w_in]), LAST_LAYOUT, [st["re"][0]], "reduce_w_in_start")
        return token

    loss_blk, grad_x, g, big_g, silu_c = local_step(x[0], c, target[0], p, big, on_big_grads, on_early, on_last)
    g0 = big_g[0]
    recv_last = reduce_wait(*st["fl"], LAST_LAYOUT, grad_x, "reduce_w_in_wait")
    sums1 = _core_sums(st["g1"], st["r1"], REDUCE_LAYOUT, place, "l1")
    sums0 = (_core_sums(g0[:1], recv_last, LAST_LAYOUT, place, "l0")
             + _core_sums(g0[1:], st["re"], EARLY_LAYOUT, place, "l0"))
    shapes = [_local_shape(kind, full) for _, kind, full in REDUCE_LAYOUT]
    big_g = dict(zip((n for n, _, _ in REDUCE_LAYOUT), _reduce_total(sums0, sums1, shapes)))

    assert SMALL[0] == "ada_b"
    small_pack = _pack([g["ada_b"], silu_c, loss_blk[0, 0:1]] + [g[n] for n in SMALL[1:]] + [g[n] for n in CONV], F32)
    small_all = allgather8(small_pack, "gather_small").reshape(8, -1, PACK_COLS)
    total = _unpack(add8(small_all, "reduce_small"),
                    [small_shapes[0], (D_MODEL,), (1,)] + small_shapes[1:] + [g[n].shape for n in CONV])
    loss = total[2][0]
    small_g = dict(zip(SMALL, [total[0]] + total[3:2 + len(SMALL)]))
    conv_g = {n: lax.dynamic_slice_in_dim(t, chip * w[n].shape[2], w[n].shape[2], axis=2)
              for n, t in zip(CONV, total[2 + len(SMALL):])}

    cols = w["ada_w"].shape[2]
    silu_all = small_all[:, _pack_rows(small_shapes[0]), :]
    big_g["ada_w"] = jnp.stack([
        matmul(silu_all, lax.dynamic_slice_in_dim(small_all[:, N_MOD * l:N_MOD * (l + 1), :].reshape(8, -1),
                                                  chip * cols, cols, axis=1), f"ada_w_grad{l}", ta=True)
        for l in range(DEPTH)])

    grad, delta, new_m, new_v = {}, {}, {}, {}
    for n, _, _ in BIG_LAYOUT:
        delta[n], new_m[n], new_v[n], grad[n] = _adam_nd(big_g[n], w[n], m[n], v[n], "adam_" + n)
    for names, gs, shapes, tag in ((SMALL, small_g, small_shapes, "small"), (CONV, conv_g, conv_shapes, "conv")):
        pk = lambda d: _pack([d[n] for n in names], F32)[None]
        res = adamw(pk(gs), pk(w), pk(m), pk(v), "adam_" + tag)
        for out, r in zip((delta, new_m, new_v), res):
            out.update(zip(names, _unpack(r[0], shapes)))
        grad.update({n: gs[n] for n in names})
    outs = [loss, grad_x[None]]
    for d in (grad, delta, new_m, new_v):
        outs += [d[n] for n in WEIGHTS]
    return tuple(outs)


def kernel(x, c, ada_w, ada_b, norm_mix, w_in, gdn_conv_w, gdn_a_log, gdn_dt_bias, gdn_norm, ssd_conv_w, ssd_conv_b, ssd_a_log, ssd_dt_bias, ssd_d, ssd_norm, lru_conv_w, lru_conv_b, lru_w_a, lru_b_a, lru_w_x, lru_b_x, lru_lambda, w_branch, w_out, norm_mlp, w_up, w_down, final_norm, loss_target, m_ada_w, m_ada_b, m_norm_mix, m_w_in, m_gdn_conv_w, m_gdn_a_log, m_gdn_dt_bias, m_gdn_norm, m_ssd_conv_w, m_ssd_conv_b, m_ssd_a_log, m_ssd_dt_bias, m_ssd_d, m_ssd_norm, m_lru_conv_w, m_lru_conv_b, m_lru_w_a, m_lru_b_a, m_lru_w_x, m_lru_b_x, m_lru_lambda, m_w_branch, m_w_out, m_norm_mlp, m_w_up, m_w_down, m_final_norm, v_ada_w, v_ada_b, v_norm_mix, v_w_in, v_gdn_conv_w, v_gdn_a_log, v_gdn_dt_bias, v_gdn_norm, v_ssd_conv_w, v_ssd_conv_b, v_ssd_a_log, v_ssd_dt_bias, v_ssd_d, v_ssd_norm, v_lru_conv_w, v_lru_conv_b, v_lru_w_a, v_lru_b_a, v_lru_w_x, v_lru_b_x, v_lru_lambda, v_w_branch, v_w_out, v_norm_mlp, v_w_up, v_w_down, v_final_norm):
    given = dict(locals())
    w = {n: given[n] for n in WEIGHTS}
    m = {n: given["m_" + n] for n in WEIGHTS}
    v = {n: given["v_" + n] for n in WEIGHTS}
    return _step(w, m, v, x, c, loss_target)
```

```python
import functools
import math

import jax
import jax.numpy as jnp
from jax import lax
from jax.experimental import pallas as pl
from jax.experimental.pallas import tpu as pltpu

F32 = jnp.float32
BF16 = jnp.bfloat16

D_MODEL = 1024
DEPTH = 2
RMS_EPS = 1e-6
CHUNK = 128
GDN_HEADS = 4
SSD_HEADS = 8
LRU_C = 8.0
D_FF = 4096
N_MOD = 6
W_GDN = 2176
W_SSD = 1664
W_LRU = 512
W_GATE = 3072
ADAM_LR = 0.001
ADAM_B1 = 0.9
ADAM_B2 = 0.999
ADAM_EPS = 1e-08
ADAM_WD = 0.01
ADAM_STEP = 10
VMEM_LIMIT = 56 * 1024 * 1024
MESH = pl.DeviceIdType.MESH


def _dot(a, b, ta, tb):
    dn = (((0 if ta else 1,), (1 if tb else 0,)), ((), ()))
    return lax.dot_general(a.astype(BF16), b.astype(BF16), dn, preferred_element_type=F32)


@functools.partial(jax.custom_vjp, nondiff_argnums=(2, 3))
def mm(a, b, ta, tb):
    return _dot(a, b, ta, tb)


def _mm_fwd(a, b, ta, tb):
    return _dot(a, b, ta, tb), (a, b)


def _mm_bwd(ta, tb, res, g):
    a, b = res
    if not ta and not tb:
        return mm(g, b, False, True), mm(a, g, True, False)
    if not ta and tb:
        return mm(g, b, False, False), mm(g, a, True, False)
    assert ta and not tb
    return mm(b, g, False, True), mm(a, g, False, False)


mm.defvjp(_mm_fwd, _mm_bwd)


def _tri_apply(x, upper):
    t = x.shape[0]
    r = lax.broadcasted_iota(jnp.int32, (t, t), 0)
    c = lax.broadcasted_iota(jnp.int32, (t, t), 1)
    tri = jnp.where((r <= c) if upper else (r >= c), 1.0, 0.0).astype(BF16)
    x1 = x.astype(BF16)
    r1 = x - x1.astype(F32)
    x2 = r1.astype(BF16)
    x3 = (r1 - x2.astype(F32)).astype(BF16)
    d = lambda p: jnp.dot(tri, p, preferred_element_type=F32)
    return (d(x1) + d(x2)) + d(x3)


@jax.custom_vjp
def cumsum_rows(x):
    return _tri_apply(x, False)


cumsum_rows.defvjp(lambda x: (_tri_apply(x, False), None), lambda _, g: (_tri_apply(g, True),))


def _dot_split(a, b):
    a1, b1 = a.astype(BF16), b.astype(BF16)
    a2, b2 = (a - a1.astype(F32)).astype(BF16), (b - b1.astype(F32)).astype(BF16)
    d = lambda p, q: jnp.dot(p, q, preferred_element_type=F32)
    return d(a1, b1) + (d(a1, b2) + d(a2, b1))


def _neumann(ms):
    t = ms[0].shape[0]
    xs = [-m for m in ms]
    qs = [_dot(m, m, False, False) for m in ms]
    n = 2
    while True:
        xs = [x + q + _dot(x, q, False, False) for x, q in zip(xs, qs)]
        n *= 2
        if n >= t:
            break
        qs = [_dot(q, q, False, False) for q in qs]
    rs = [-(x + m + _dot_split(m, x)) for x, m in zip(xs, ms)]
    return [x + r + _dot(x, r, False, False) for x, r in zip(xs, rs)]


@jax.custom_vjp
def tri_solve(ms, rhss):
    return tuple(rhs + _dot(x, rhs, False, False) for x, rhs in zip(_neumann(ms), rhss))


def _tri_solve_fwd(ms, rhss):
    xs = _neumann(ms)
    sols = tuple(rhs + _dot(x, rhs, False, False) for x, rhs in zip(xs, rhss))
    return sols, (tuple(xs), sols)


def _tri_solve_bwd(res, gs):
    xs, sols = res
    d_rhss = tuple(g + _dot(x, g, True, False) for x, g in zip(xs, gs))
    return tuple(-_dot(d, sol, False, True) for d, sol in zip(d_rhss, sols)), d_rhss


tri_solve.defvjp(_tri_solve_fwd, _tri_solve_bwd)


@functools.partial(jax.custom_vjp, nondiff_argnums=(1,))
def split_cols(x, sizes):
    out, o = [], 0
    for s in sizes:
        out.append(x[:, o:o + s])
        o += s
    return tuple(out)


split_cols.defvjp(lambda x, sizes: (split_cols(x, sizes), None),
                  lambda sizes, _, g: (jnp.concatenate(list(g), axis=1),))


@functools.partial(jax.custom_vjp, nondiff_argnums=(1,))
def _last_rows(x, t):
    return x[t - 8:, :]


_last_rows.defvjp(lambda x, t: (_last_rows(x, t), None),
                  lambda t, _, g: (jnp.concatenate([jnp.zeros((t - 8, g.shape[1]), g.dtype), g], axis=0),))


def last8(x):
    return _last_rows(x, x.shape[0])


def _shifted(xp, d, t):
    return (pltpu.roll(xp, d, 0) if d else xp)[8:8 + t, :]


@jax.custom_vjp
def conv4(x, tail, w):
    t = x.shape[0]
    xp = jnp.concatenate([tail, x], axis=0)
    return sum(_shifted(xp, 3 - k, t) * w[k:k + 1, :] for k in range(4))


def _conv4_fwd(x, tail, w):
    return conv4(x, tail, w), (x, tail, w)


def _conv4_bwd(res, g):
    x, tail, w = res
    t = x.shape[0]
    xp = jnp.concatenate([tail, x], axis=0)
    zero8 = jnp.zeros((8, g.shape[1]), g.dtype)
    d_xp = jnp.zeros_like(xp)
    d_w = []
    for k in range(4):
        gk = jnp.concatenate([zero8, g * w[k:k + 1, :]], axis=0)
        d_xp = d_xp + (pltpu.roll(gk, t + 8 - (3 - k), 0) if k < 3 else gk)
        d_w.append(jnp.sum(g * _shifted(xp, 3 - k, t), axis=0, keepdims=True))
    return d_xp[8:, :], d_xp[:8, :], jnp.concatenate(d_w, axis=0)


conv4.defvjp(_conv4_fwd, _conv4_bwd)


def _sigmoid(x):
    return 0.5 * (jnp.tanh(0.5 * x) + 1.0)


def _silu(x):
    return x * _sigmoid(x)


def _softplus(x):
    ax = jnp.where(x > 0, x, -x)
    return jnp.where(x > 0, x, 0.0) + jnp.log(1.0 + jnp.exp(-ax))


def _gelu(x):
    return 0.5 * x * (1.0 + jnp.tanh(math.sqrt(2.0 / math.pi) * (x + 0.044715 * (x * x * x))))


def _expm1(x):
    series = x * (1.0 + x * (0.5 + x * (1.0 / 6.0 + x * (1.0 / 24.0))))
    return jnp.where(jnp.abs(x) < 0.03, series, jnp.exp(x) - 1.0)


def _rms(x, w):
    return x * lax.rsqrt(jnp.mean(x * x, axis=-1, keepdims=True) + RMS_EPS) * w


def _lane_pick(x, j):
    lane = lax.broadcasted_iota(jnp.int32, (1, x.shape[1]), 1)
    return jnp.sum(jnp.where(lane == j, x, 0.0), axis=1, keepdims=True)


def _row_pick(x, j):
    row = lax.broadcasted_iota(jnp.int32, (x.shape[0], 1), 0)
    return jnp.sum(jnp.where(row == j, x, 0.0), axis=0, keepdims=True)


def gdn_fn(carry, seq, params):
    *states, tail = carry
    (tile,) = seq
    conv_w, alog_row, dtb_row, norm_w = params
    t = tile.shape[0]
    qkv_raw, z, sm = split_cols(tile, (1536, 512, 128))
    qkv = _silu(conv4(qkv_raw, tail, conv_w))
    parts = split_cols(qkv, (128,) * 12)
    zs = split_cols(z, (128,) * 4)
    lane = lax.broadcasted_iota(jnp.int32, (1, 128), 1)
    beta_all = _sigmoid(sm)
    g_all = jnp.where((lane >= 4) & (lane < 8), -jnp.exp(alog_row) * _softplus(sm + dtb_row), 0.0)
    gc_all = cumsum_rows(g_all)
    gr_all = gc_all.T
    gl_all = _row_pick(gc_all, t - 1)
    r = lax.broadcasted_iota(jnp.int32, (t, t), 0)
    c = lax.broadcasted_iota(jnp.int32, (t, t), 1)
    heads = range(GDN_HEADS)
    l2 = lambda a: a * lax.rsqrt(jnp.sum(a * a, axis=-1, keepdims=True) + RMS_EPS)
    qn = [l2(parts[h]) * (128.0 ** -0.5) for h in heads]
    kn = [l2(parts[4 + h]) for h in heads]
    beta = [_lane_pick(beta_all, h) for h in heads]
    gc = [_lane_pick(gc_all, 4 + h) for h in heads]
    gl = [_lane_pick(gl_all, 4 + h) for h in heads]
    decay = [jnp.exp(jnp.where(r >= c, gc[h] - _row_pick(gr_all, 4 + h), -1e30)) for h in heads]
    kk = [mm(kn[h], kn[h], False, True) for h in heads]
    qk = [mm(qn[h], kn[h], False, True) for h in heads]
    m = tuple(jnp.where(r > c, beta[h] * kk[h] * decay[h], 0.0) for h in heads)
    eg = [jnp.exp(gc[h]) for h in heads]
    rhs = tuple(jnp.concatenate([beta[h] * parts[8 + h], (beta[h] * eg[h]) * kn[h]], axis=1) for h in heads)
    uw = [split_cols(s, (128, 128)) for s in tri_solve(m, rhs)]
    ws = [mm(uw[h][1], states[h], False, False) for h in heads]
    qs = [mm(qn[h] * eg[h], states[h], False, False) for h in heads]
    v_new = [uw[h][0] - ws[h] for h in heads]
    o = [qs[h] + mm(qk[h] * decay[h], v_new[h], False, False) for h in heads]
    kv = [mm(kn[h] * jnp.exp(gl[h] - gc[h]), v_new[h], True, False) for h in heads]
    new_states = [states[h] * jnp.exp(gl[h]) + kv[h] for h in heads]
    outs = [_rms(o[h], norm_w) * _silu(zs[h]) for h in heads]
    return (*new_states, last8(qkv_raw)), (jnp.concatenate(outs, axis=1),)


def ssd_fn(carry, seq, params):
    *states, tail = carry
    (tile,) = seq
    conv_w, conv_b, alog_row, dtb_row, d_row, norm_w = params
    t = tile.shape[0]
    xbc_raw, z, sm = split_cols(tile, (1024, 512, 128))
    xbc = _silu(conv4(xbc_raw, tail, conv_w) + conv_b)
    x0, x1, x2, x3, b0, b1, c0, c1 = split_cols(xbc, (128,) * 8)
    xs, bs, cs = (x0, x1, x2, x3), (b0, b1), (c0, c1)
    ds = split_cols(d_row, (128,) * 4)
    lane = lax.broadcasted_iota(jnp.int32, (1, 128), 1)
    sub = lax.broadcasted_iota(jnp.int32, (128, 1), 0)
    low = lane < 64
    dt_all = jnp.where(lane < SSD_HEADS, _softplus(sm + dtb_row), 0.0)
    ac_all = cumsum_rows(dt_all * (-jnp.exp(alog_row)))
    ar_all = ac_all.T
    al_all = _row_pick(ac_all, t - 1)
    r = lax.broadcasted_iota(jnp.int32, (t, t), 0)
    c = lax.broadcasted_iota(jnp.int32, (t, t), 1)
    pairs, heads = range(4), range(SSD_HEADS)
    col = [_lane_pick(ac_all, h) for h in heads]
    last = [_lane_pick(al_all, h) for h in heads]
    dt = [_lane_pick(dt_all, h) for h in heads]
    lm = [jnp.exp(jnp.where(r >= c, col[h] - _row_pick(ar_all, h), -1e30)) for h in heads]
    cb = [mm(cs[g], bs[g], False, True) for g in range(2)]
    both = lambda a, b: jnp.where(low, a, b)
    xdt = [xs[p] * both(dt[2 * p], dt[2 * p + 1]) for p in pairs]
    y_off = [mm(cs[p // 2], states[p], False, True) for p in pairs]
    y_lo = [mm(cb[p // 2] * lm[2 * p], jnp.where(low, xdt[p], 0.0), False, False) for p in pairs]
    y_hi = [mm(cb[p // 2] * lm[2 * p + 1], jnp.where(low, 0.0, xdt[p]), False, False) for p in pairs]
    st = [mm(xdt[p] * both(jnp.exp(last[2 * p] - col[2 * p]), jnp.exp(last[2 * p + 1] - col[2 * p + 1])),
             bs[p // 2], True, False) for p in pairs]
    ys = [ds[p] * xs[p] + y_lo[p] + y_hi[p] + y_off[p] * both(jnp.exp(col[2 * p]), jnp.exp(col[2 * p + 1]))
          for p in pairs]
    new_states = [states[p] * jnp.where(sub < 64, jnp.exp(last[2 * p]), jnp.exp(last[2 * p + 1])) + st[p]
                  for p in pairs]
    gz = jnp.concatenate(ys, axis=1) * _silu(z)
    g0, g1 = split_cols(gz, (256, 256))
    n0, n1 = split_cols(norm_w, (256, 256))
    out = jnp.concatenate([_rms(g0, n0), _rms(g1, n1)], axis=1)
    return (*new_states, last8(xbc_raw)), (out,)


def lru_in_fn(carry, seq, params):
    (tail,) = carry
    (x,) = seq
    conv_w, conv_b, w_a, b_a, w_x, b_x, lam = params
    xc = conv4(x, tail, conv_w) + conv_b
    r = _sigmoid(mm(xc, w_a, False, False) + b_a)
    i = _sigmoid(mm(xc, w_x, False, False) + b_x)
    log_a = -LRU_C * r * _softplus(-lam)
    u = jnp.sqrt(-_expm1(2.0 * log_a)) * (i * xc)
    return (last8(x),), (jnp.exp(log_a), u)


def lru_out_fn(carry, seq, params):
    hs, gate = seq
    return (), (hs * _gelu(gate),)


def merge_fn(carry, seq, params):
    ya, yb, yc, gl = seq
    g = split_cols(_sigmoid(gl), (D_MODEL,) * 3)
    merged = sum(g[r] * mm(y, params[r], False, False) for r, y in enumerate((ya, yb, yc)))
    return (), (merged,)


def _adaln(x, w, sc, sh):
    return _rms(x, w) * (1.0 + sc) + sh


def norm1_fn(carry, seq, params):
    (x,) = seq
    return (), (_adaln(x, *params), x)


def resid_norm_fn(carry, seq, params):
    x, mix = seq
    gt, w, sc, sh = params
    x1 = x + gt * mix
    return (), (_adaln(x1, w, sc, sh), x1)


def resid_fn(carry, seq, params):
    x, dn = seq
    (gt,) = params
    return (), (x + gt * dn,)


def silu_fn(carry, seq, params):
    return (), (_silu(seq[0]),)


def _full_spec(a, single=False):
    nd = a.ndim
    return pl.BlockSpec(a.shape, lambda i: (0,) * nd, pipeline_mode=pl.Buffered(1) if single else None)


def _cparams(*sem):
    return pltpu.CompilerParams(dimension_semantics=sem, vmem_limit_bytes=VMEM_LIMIT)


def scan_fwd(fn, name, tile, seqs, params, carry_shapes, outs, save_carry=False):
    rows = seqs[0].shape[0]
    tile = min(tile, rows)
    n = rows // tile
    ns, npar, nc, no = len(seqs), len(params), len(carry_shapes), len(outs)

    def body(*refs):
        seq_refs, refs = refs[:ns], refs[ns:]
        par_refs, refs = refs[:npar], refs[npar:]
        out_refs, refs = refs[:no], refs[no:]
        save_refs, refs = (refs[:nc], refs[nc:]) if save_carry else ((), refs)
        carry_refs = refs

        @pl.when(pl.program_id(0) == 0)
        def _():
            for cr in carry_refs:
                cr[...] = jnp.zeros_like(cr)

        carry = tuple(cr[...] for cr in carry_refs)
        for sr, cv in zip(save_refs, carry):
            sr[0] = cv
        new_carry, res = fn(carry, tuple(r[...].astype(F32) for r in seq_refs),
                            tuple(r[...].astype(F32) for r in par_refs))
        for r, v in zip(out_refs, res):
            r[...] = v.astype(r.dtype)
        for cr, v in zip(carry_refs, new_carry):
            cr[...] = v

    out_shape = [jax.ShapeDtypeStruct((rows, w), dt) for w, dt in outs]
    out_specs = [pl.BlockSpec((tile, w), lambda i: (i, 0)) for w, _ in outs]
    if save_carry:
        out_shape += [jax.ShapeDtypeStruct((n, *s), F32) for s in carry_shapes]
        out_specs += [pl.BlockSpec((1, *s), lambda i: (i, 0, 0)) for s in carry_shapes]
    res = pl.pallas_call(
        body, name=name, grid=(n,),
        in_specs=([pl.BlockSpec((tile, s.shape[1]), lambda i: (i, 0)) for s in seqs]
                  + [_full_spec(p, single=True) for p in params]),
        out_specs=out_specs, out_shape=out_shape,
        scratch_shapes=[pltpu.VMEM(s, F32) for s in carry_shapes],
        compiler_params=_cparams("arbitrary"),
    )(*seqs, *params)
    return res[:no], res[no:]


def scan_bwd(fn, name, tile, seqs, params, saved, douts, n_dseq, n_dpar, dseq_dtypes=None):
    dseq_dtypes = dseq_dtypes or [F32] * n_dseq
    rows = seqs[0].shape[0]
    tile = min(tile, rows)
    n = rows // tile
    ns, npar, nc, no = len(seqs), len(params), len(saved), len(douts)

    def body(*refs):
        seq_refs, refs = refs[:ns], refs[ns:]
        par_refs, refs = refs[:npar], refs[npar:]
        save_refs, refs = refs[:nc], refs[nc:]
        dout_refs, refs = refs[:no], refs[no:]
        dseq_refs, refs = refs[:n_dseq], refs[n_dseq:]
        dpar_refs, refs = refs[:n_dpar], refs[n_dpar:]
        dcarry_refs = refs

        @pl.when(pl.program_id(0) == 0)
        def _():
            for r in (*dpar_refs, *dcarry_refs):
                r[...] = jnp.zeros_like(r)

        carry = tuple(r[0] for r in save_refs)
        seq = tuple(r[...].astype(F32) for r in seq_refs)
        par = tuple(r[...].astype(F32) for r in par_refs)

        def f(carry, dseq, dpar):
            return fn(carry, (*dseq, *seq[n_dseq:]), (*dpar, *par[n_dpar:]))

        _, vjp = jax.vjp(f, carry, seq[:n_dseq], par[:n_dpar])
        d_carry, d_seq, d_par = vjp((tuple(r[...] for r in dcarry_refs),
                                     tuple(r[...].astype(F32) for r in dout_refs)))
        for r, v in zip(dseq_refs, d_seq):
            r[...] = v.astype(r.dtype)
        for r, v in zip(dpar_refs, d_par):
            r[...] += v
        for r, v in zip(dcarry_refs, d_carry):
            r[...] = v

    rev = lambda i: (n - 1 - i, 0)
    res = pl.pallas_call(
        body, name=name, grid=(n,),
        in_specs=([pl.BlockSpec((tile, s.shape[1]), rev) for s in seqs] + [_full_spec(p, single=True) for p in params]
                  + [pl.BlockSpec((1, *s.shape[1:]), lambda i: (n - 1 - i, 0, 0)) for s in saved]
                  + [pl.BlockSpec((tile, d.shape[1]), rev) for d in douts]),
        out_specs=([pl.BlockSpec((tile, s.shape[1]), rev) for s in seqs[:n_dseq]]
                   + [_full_spec(p) for p in params[:n_dpar]]),
        out_shape=([jax.ShapeDtypeStruct((rows, s.shape[1]), dt) for s, dt in zip(seqs[:n_dseq], dseq_dtypes)]
                   + [jax.ShapeDtypeStruct(p.shape, F32) for p in params[:n_dpar]]),
        scratch_shapes=[pltpu.VMEM(s.shape[1:], F32) for s in saved],
        compiler_params=_cparams("arbitrary"),
    )(*seqs, *params, *saved, *douts)
    return res[:n_dseq], res[n_dseq:]


def _tile_of(dim, pref):
    if dim <= pref:
        return dim
    best = max((t for t in range(128, pref + 1, 128) if dim % t == 0), default=None)
    if best is None or (best < 512 and dim <= 2304):
        return dim
    return best


def _row_tile(rows, pref):
    if rows <= pref:
        return rows
    return max(t for t in range(8, pref + 1, 8) if rows % t == 0)


def matmul(a, b, name, ta=False, tb=False, out_dtype=F32, add=None, bias=None, relu2=False, relu2_of=None,
           tm=1024, tn=2048, tk=2048):
    m, k = (a.shape[1], a.shape[0]) if ta else a.shape
    n = b.shape[0] if tb else b.shape[1]
    assert k == (b.shape[1] if tb else b.shape[0])
    tm, tn, tk = _tile_of(m, tm), _tile_of(n, tn), _tile_of(k, tk)
    nm, nn, nk = m // tm, n // tn, k // tk
    assert nk == 1 or (out_dtype == F32 and not relu2 and relu2_of is None)
    dn = (((0 if ta else 1,), (1 if tb else 0,)), ((), ()))
    has_add, has_bias, has_u = add is not None, bias is not None, relu2_of is not None
    n_inner = a.size * a.dtype.itemsize * (nn - 1) >= b.size * b.dtype.itemsize * (nm - 1)
    ij = (lambda g0, g1: (g0, g1)) if n_inner else (lambda g0, g1: (g1, g0))

    def body(*refs):
        a_ref, b_ref, refs = refs[0], refs[1], refs[2:]
        add_ref, refs = (refs[0], refs[1:]) if has_add else (None, refs)
        bias_ref, refs = (refs[0], refs[1:]) if has_bias else (None, refs)
        u_ref, refs = (refs[0], refs[1:]) if has_u else (None, refs)
        o_ref = refs[0]
        r = lax.dot_general(a_ref[...].astype(BF16), b_ref[...].astype(BF16), dn, preferred_element_type=F32)

        def first():
            v = r
            if has_add:
                v = v + add_ref[...]
            if has_bias:
                v = v + bias_ref[...]
            if has_u:
                v = v * (2.0 * jnp.maximum(u_ref[...], 0.0))
            o_ref[...] = v.astype(o_ref.dtype)
            if relu2:
                p = jnp.maximum(v, 0.0)
                refs[1][...] = (p * p).astype(BF16)

        if nk == 1:
            first()
        else:
            pl.when(pl.program_id(2) == 0)(first)

            @pl.when(pl.program_id(2) > 0)
            def _():
                o_ref[...] += r

    def spec(shape, fn):
        return pl.BlockSpec(shape, lambda g0, g1, l: fn(*ij(g0, g1), l))

    a_spec = spec((tk, tm), lambda i, j, l: (l, i)) if ta else spec((tm, tk), lambda i, j, l: (i, l))
    b_spec = spec((tn, tk), lambda i, j, l: (j, l)) if tb else spec((tk, tn), lambda i, j, l: (l, j))
    o_spec = spec((tm, tn), lambda i, j, l: (i, j))
    in_specs, args = [a_spec, b_spec], [a, b]
    if has_add:
        in_specs.append(o_spec)
        args.append(add)
    if has_bias:
        in_specs.append(spec((1, tn), lambda i, j, l: (0, j)))
        args.append(bias)
    if has_u:
        in_specs.append(o_spec)
        args.append(relu2_of)
    out_shape = [jax.ShapeDtypeStruct((m, n), out_dtype)] + ([jax.ShapeDtypeStruct((m, n), BF16)] if relu2 else [])
    res = pl.pallas_call(
        body, name=name, grid=(nm, nn, nk) if n_inner else (nn, nm, nk), in_specs=in_specs,
        out_specs=[o_spec] * len(out_shape), out_shape=out_shape,
        compiler_params=_cparams("parallel", "parallel", "arbitrary"),
    )(*args)
    return res if relu2 else res[0]


LIN_TILE = 512


def linscan_fwd(a, u, name):
    rows, w = a.shape
    tile = min(LIN_TILE, rows)

    def body(a_ref, u_ref, h_ref, hc):
        @pl.when(pl.program_id(0) == 0)
        def _():
            hc[...] = jnp.zeros_like(hc)

        row = lax.broadcasted_iota(jnp.int32, (8, 1), 0)

        def group(k, h_in):
            rows8 = pl.ds(pl.multiple_of(k * 8, 8), 8)
            pa, pu = a_ref[rows8, :], u_ref[rows8, :]
            for d in (1, 2, 4):
                pu = pu + pa * jnp.where(row >= d, pltpu.roll(pu, d, 0), 0.0)
                pa = pa * jnp.where(row >= d, pltpu.roll(pa, d, 0), 1.0)
            h_ref[rows8, :] = pa * h_in + pu
            return h_ref[pl.ds(k * 8 + 7, 1), :]

        hc[...] = lax.fori_loop(0, tile // 8, group, hc[...], unroll=4)

    spec = pl.BlockSpec((tile, w), lambda i: (i, 0))
    return pl.pallas_call(
        body, name=name, grid=(rows // tile,), in_specs=[spec, spec], out_specs=spec,
        out_shape=jax.ShapeDtypeStruct((rows, w), F32), scratch_shapes=[pltpu.VMEM((1, w), F32)],
        compiler_params=_cparams("arbitrary"),
    )(a, u)


def linscan_bwd(a, hs, dh, name):
    rows, w = a.shape
    tile = min(LIN_TILE, rows)
    n = rows // tile
    per = tile // 8

    def body(a_ref, h_ref, hprev_ref, dh_ref, da_ref, du_ref, cc):
        i = pl.program_id(0)

        @pl.when(i == 0)
        def _():
            cc[...] = jnp.zeros_like(cc)

        row = lax.broadcasted_iota(jnp.int32, (8, 1), 0)
        h_before = jnp.where(i == n - 1, 0.0, hprev_ref[7:8, :])

        def group(s, c_in):
            k = per - 1 - s
            rows8 = pl.ds(pl.multiple_of(k * 8, 8), 8)
            av, hv = a_ref[rows8, :], h_ref[rows8, :]
            pb = jnp.where(row < 7, pltpu.roll(av, 7, 0), 1.0)
            pg = dh_ref[rows8, :]
            for d in (1, 2, 4):
                pg = pg + pb * jnp.where(row < 8 - d, pltpu.roll(pg, 8 - d, 0), 0.0)
                pb = pb * jnp.where(row < 8 - d, pltpu.roll(pb, 8 - d, 0), 1.0)
            g = pg + pb * c_in
            du_ref[rows8, :] = g
            h_prev = jnp.where(k == 0, h_before, h_ref[pl.ds(jnp.maximum(k * 8 - 1, 0), 1), :])
            da_ref[rows8, :] = g * jnp.where(row >= 1, pltpu.roll(hv, 1, 0), h_prev)
            return a_ref[pl.ds(k * 8, 1), :] * du_ref[pl.ds(k * 8, 1), :]

        cc[...] = lax.fori_loop(0, per, group, cc[...], unroll=4)

    rev = pl.BlockSpec((tile, w), lambda i: (n - 1 - i, 0))
    prev = pl.BlockSpec((8, w), lambda i: (jnp.maximum((n - 1 - i) * per - 1, 0), 0))
    return pl.pallas_call(
        body, name=name, grid=(n,), in_specs=[rev, rev, prev, rev], out_specs=[rev, rev],
        out_shape=[jax.ShapeDtypeStruct((rows, w), F32)] * 2, scratch_shapes=[pltpu.VMEM((1, w), F32)],
        compiler_params=_cparams("arbitrary"),
    )(a, hs, hs, dh)


def loss_head(x, target, w, name):
    rows, d = x.shape
    tile = min(512, rows)

    def body(x_ref, t_ref, w_ref, loss_ref, dx_ref, dw_ref):
        @pl.when(pl.program_id(0) == 0)
        def _():
            loss_ref[...] = jnp.zeros_like(loss_ref)
            dw_ref[...] = jnp.zeros_like(dw_ref)

        tv = t_ref[...]

        def f(xv, wv):
            e = _rms(xv, wv) - tv
            return 0.5 * jnp.sum(jnp.mean(e * e, axis=-1, keepdims=True), axis=0, keepdims=True)

        val, vjp = jax.vjp(f, x_ref[...], w_ref[...])
        dxv, dwv = vjp(jnp.ones((1, 1), F32))
        loss_ref[...] += jnp.broadcast_to(val, loss_ref.shape)
        dx_ref[...] = dxv
        dw_ref[...] += dwv

    spec = pl.BlockSpec((tile, d), lambda i: (i, 0))
    return pl.pallas_call(
        body, name=name, grid=(rows // tile,), in_specs=[spec, spec, _full_spec(w)],
        out_specs=[pl.BlockSpec((8, 128), lambda i: (0, 0)), spec, _full_spec(w)],
        out_shape=[jax.ShapeDtypeStruct((8, 128), F32), jax.ShapeDtypeStruct((rows, d), F32),
                   jax.ShapeDtypeStruct(w.shape, F32)],
        compiler_params=_cparams("arbitrary"),
    )(x, target, w)


def adamw(g, w, m, v, name, copy_g=False):
    layers, rows, cols = g.shape
    tile = _row_tile(rows, 256)
    n_out = 4 if copy_g else 3

    def body(g_ref, w_ref, m_ref, v_ref, d_ref, nm_ref, nv_ref, *g_out):
        gv = g_ref[...]
        if copy_g:
            g_out[0][...] = gv
        nm = ADAM_B1 * m_ref[...] + (1.0 - ADAM_B1) * gv
        nv = ADAM_B2 * v_ref[...] + (1.0 - ADAM_B2) * (gv * gv)
        m_hat = nm / (1.0 - ADAM_B1 ** ADAM_STEP)
        v_hat = nv / (1.0 - ADAM_B2 ** ADAM_STEP)
        d_ref[...] = -ADAM_LR * (m_hat / (jnp.sqrt(v_hat) + ADAM_EPS) + ADAM_WD * w_ref[...])
        nm_ref[...] = nm
        nv_ref[...] = nv

    spec = pl.BlockSpec((None, tile, cols), lambda l, i: (l, i, 0))
    return pl.pallas_call(
        body, name=name, grid=(layers, rows // tile), in_specs=[spec] * 4, out_specs=[spec] * n_out,
        out_shape=[jax.ShapeDtypeStruct((layers, rows, cols), F32)] * n_out,
        compiler_params=_cparams("parallel", "parallel"),
    )(g, w, m, v)


def add_layers(mine, theirs, name):
    r, c = mine[0].shape
    tile = _row_tile(r, 256)
    nl = len(mine)

    def body(*refs):
        o_ref = refs[-1]
        for l in range(nl):
            o_ref[l] = refs[l][...] + refs[nl + l][...]

    spec = pl.BlockSpec((tile, c), lambda i: (i, 0))
    return pl.pallas_call(
        body, name=name, grid=(r // tile,), in_specs=[spec] * (2 * nl),
        out_specs=pl.BlockSpec((nl, tile, c), lambda i: (0, i, 0)),
        out_shape=jax.ShapeDtypeStruct((nl, r, c), F32), compiler_params=_cparams("parallel"),
    )(*mine, *theirs)


def sum4(own, recv, by_cols, place, name):
    _, r, c = recv.shape
    tile = _row_tile(r, 256)
    nt = r // tile
    own_map = (lambda i, k: (i, k[0])) if by_cols else (lambda i, k: (k[0] * nt + i, 0))

    def body(k_ref, own_ref, recv_ref, o_ref):
        o_ref[...] = ((own_ref[...] + recv_ref[0].astype(F32)) + recv_ref[1].astype(F32)) + recv_ref[2].astype(F32)

    return pl.pallas_call(
        body, name=name,
        grid_spec=pltpu.PrefetchScalarGridSpec(
            num_scalar_prefetch=1, grid=(nt,),
            in_specs=[pl.BlockSpec((tile, c), own_map), pl.BlockSpec((3, tile, c), lambda i, k: (0, i, 0))],
            out_specs=pl.BlockSpec((tile, c), lambda i, k: (i, 0))),
        out_shape=jax.ShapeDtypeStruct((r, c), F32),
        compiler_params=_cparams("arbitrary"),
    )(place, own, recv)


def add8(parts, name):
    _, rows, cols = parts.shape

    def body(p_ref, o_ref):
        acc = p_ref[0]
        for k in range(1, 8):
            acc = acc + p_ref[k]
        o_ref[...] = acc

    return pl.pallas_call(
        body, name=name, in_specs=[pl.BlockSpec(memory_space=pltpu.VMEM)],
        out_specs=pl.BlockSpec(memory_space=pltpu.VMEM),
        out_shape=jax.ShapeDtypeStruct((rows, cols), F32),
        compiler_params=pltpu.CompilerParams(vmem_limit_bytes=VMEM_LIMIT),
    )(parts)


def _place():
    return lax.axis_index("x"), lax.axis_index("y"), lax.axis_index("c")


def _other_chips(x, y):
    return [(1 - x, y), (x, 1 - y), (1 - x, 1 - y)]


_ANY = pl.BlockSpec(memory_space=pl.ANY)


BIG_LAYOUT = (("ada_w", "col", (1024, 6144)), ("w_in", "chip", (4, 1024, 1924)), ("w_branch", "col", (3, 512, 1024)),
              ("w_out", "row", (1024, 1024)), ("w_up", "col", (1024, 4096)), ("w_down", "row", (4096, 1024)))
N_BIG = len(BIG_LAYOUT)
REDUCE_LAYOUT = BIG_LAYOUT[1:]


def _local_shape(kind, full):
    if kind == "col":
        return (*full[:-1], full[-1] // 4)
    if kind == "row":
        return (full[0] // 4, *full[1:])
    return full[1:]


def _window(ref, kind, k, local):
    if kind == "chip":
        return ref.at[k]
    if kind == "row":
        return ref.at[pl.ds(pl.multiple_of(k * local[0], 8), local[0])]
    idx = (slice(None),) * (len(local) - 1) + (pl.ds(pl.multiple_of(k * local[-1], 128), local[-1]),)
    return ref.at[idx]


def _dma_call(body, name, n_in, out_shape, sems, aliases=None):
    return pl.pallas_call(
        body, name=name, in_specs=[_ANY] * n_in, out_specs=[_ANY] * len(out_shape), out_shape=out_shape,
        scratch_shapes=[pltpu.SemaphoreType.DMA((n,)) for n in sems],
        input_output_aliases=aliases or {},
        compiler_params=pltpu.CompilerParams(has_side_effects=True))


def _remote(src, dst, send_sem, recv_sem, to):
    return pltpu.make_async_remote_copy(src_ref=src, dst_ref=dst, send_sem=send_sem, recv_sem=recv_sem,
                                        device_id=to, device_id_type=MESH)


def gather_layer(shards, layout, per_core, name):
    nb = len(shards)
    locals_ = [_local_shape(kind, full) for _, kind, full in layout]

    def body(*refs):
        sh, full, (send_sems, recv_sems, local_sems, pass_send, pass_recv) = (
            refs[:nb], refs[nb:2 * nb], refs[2 * nb:])
        x, y, c = _place()
        me = 2 * x + y
        chips = _other_chips(x, y)
        win = lambda n, k: _window(full[n], layout[n][1], k, locals_[n])
        for cc in (0, 1):
            @pl.when(c == cc)
            def _():
                mine, sends = {}, []
                for n in per_core[cc]:
                    mine[n] = pltpu.make_async_copy(sh[n], win(n, me), local_sems.at[n])
                    mine[n].start()
                    for j, chip in enumerate(chips):
                        sends.append(_remote(sh[n], win(n, me), send_sems.at[3 * n + j], recv_sems.at[3 * n + j],
                                             (chip[0], chip[1], c)))
                        sends[-1].start()
                for n in per_core[cc]:
                    for j, chip in enumerate(chips):
                        _remote(sh[n], win(n, 2 * chip[0] + chip[1]), send_sems.at[3 * n + j],
                                recv_sems.at[3 * n + j], (chip[0], chip[1], c)).wait_recv()
                    mine[n].wait()
                    sends.append(_remote(full[n], full[n], pass_send.at[n], pass_recv.at[n], (x, y, 1 - c)))
                    sends[-1].start()
                for n in per_core[1 - cc]:
                    _remote(full[n], full[n], pass_send.at[n], pass_recv.at[n], (x, y, 1 - c)).wait_recv()
                for cp in sends:
                    cp.wait_send()

    out_shape = [jax.ShapeDtypeStruct(full, BF16) for _, _, full in layout]
    return _dma_call(body, name, nb, out_shape, (3 * nb, 3 * nb, nb, nb, nb))(*shards)


_HBM = pl.BlockSpec(memory_space=pltpu.HBM)
_SEM = pl.BlockSpec(memory_space=pltpu.SEMAPHORE)


def _hbm(a):
    return pltpu.with_memory_space_constraint(a, pltpu.HBM)


def _gather_copies(sh, full, layout, send_sems, recv_sems, first=0):
    locals_ = [_local_shape(kind, f) for _, kind, f in layout]
    x, y, c = _place()
    me = 2 * x + y
    pairs = []
    for n in range(len(sh)):
        win = lambda k: _window(full[n], layout[n][1], k, locals_[n])
        for j, chip in enumerate(_other_chips(x, y)):
            k = 3 * (first + n) + j
            mk = lambda dst: _remote(sh[n], dst, send_sems.at[k], recv_sems.at[k], (chip[0], chip[1], c))
            pairs.append((mk(win(me)), mk(win(2 * chip[0] + chip[1]))))
    return pairs


def gather_start(shards, fulls, layout, after, name):
    nb = len(shards)

    def body(*refs):
        sh, full = refs[:nb], refs[nb:2 * nb]
        send_sems, recv_sems = refs[2 * nb + len(after):2 * nb + len(after) + 2]
        for out, _ in _gather_copies(sh, full, layout, send_sems, recv_sems):
            out.start()
        refs[-1][...] = jnp.zeros_like(refs[-1])

    thru = [pltpu.HBM(a.shape, a.dtype) for a in (*shards, *fulls)]
    res = pl.pallas_call(
        body, name=name,
        out_shape=(pltpu.SemaphoreType.DMA((3 * nb,)), pltpu.SemaphoreType.DMA((3 * nb,)), *thru,
                   jax.ShapeDtypeStruct((8, 128), F32)),
        in_specs=[_HBM] * (2 * nb) + [_ANY] * len(after),
        out_specs=(_SEM, _SEM, *[_HBM] * (2 * nb), pl.BlockSpec(memory_space=pltpu.VMEM)),
        input_output_aliases={i: 2 + i for i in range(2 * nb)},
        compiler_params=pltpu.CompilerParams(has_side_effects=pltpu.SideEffectType.DATAFLOW_SIDE_EFFECTING),
    )(*[_hbm(a) for a in (*shards, *fulls)], *after)
    return (res[0], res[1], res[2:2 + nb], res[2 + nb:2 + 2 * nb]), res[-1]


def gather_wait(send_sems, recv_sems, shards, fulls, layout, first, after, name):
    nb = len(shards)

    def body(*refs):
        sh, full = refs[:nb], refs[nb:2 * nb]
        ssem, rsem = refs[2 * nb:2 * nb + 2]
        for out, inc in _gather_copies(sh, full, layout, ssem, rsem, first):
            out.wait_send()
            inc.wait_recv()

    thru = [pltpu.HBM(a.shape, a.dtype) for a in (*shards, *fulls)]
    res = pl.pallas_call(
        body, name=name, out_shape=thru,
        in_specs=[_HBM] * (2 * nb) + [_SEM, _SEM, _ANY], out_specs=[_HBM] * (2 * nb),
        input_output_aliases={i: i for i in range(2 * nb)},
        compiler_params=pltpu.CompilerParams(has_side_effects=pltpu.SideEffectType.DATAFLOW_SIDE_EFFECTING),
    )(*shards, *fulls, send_sems, recv_sems, after)
    return res[nb:]


def swap_d2d(arrays, name):
    nb = len(arrays)

    def body(*refs):
        src, got, (send_sems, recv_sems) = refs[:nb], refs[nb:2 * nb], refs[2 * nb:]
        x, y, c = _place()
        copies = [_remote(src[n], got[n], send_sems.at[n], recv_sems.at[n], (x, y, 1 - c)) for n in range(nb)]
        for cp in copies:
            cp.start()
        for cp in copies:
            cp.wait_recv()
        for cp in copies:
            cp.wait_send()

    out_shape = [jax.ShapeDtypeStruct(a.shape, a.dtype) for a in arrays]
    return _dma_call(body, name, nb, out_shape, (nb, nb))(*arrays)


def _reduce_copies(src, recv, layout, send_sems, recv_sems):
    locals_ = [_local_shape(kind, full) for _, kind, full in layout]
    x, y, c = _place()
    copies = []
    for n in range(len(src)):
        for j, chip in enumerate(_other_chips(x, y)):
            copies.append(_remote(_window(src[n], layout[n][1], 2 * chip[0] + chip[1], locals_[n]), recv[n].at[j],
                                  send_sems.at[3 * n + j], recv_sems.at[3 * n + j], (chip[0], chip[1], c)))
    return copies


def _recv_shapes(sums, layout):
    return [(3, *_local_shape(kind, full)) for _, kind, full in layout]


def reduce_ici(sums, layout, name):
    nb = len(sums)

    def body(*refs):
        src, recv, (send_sems, recv_sems) = refs[:nb], refs[nb:2 * nb], refs[2 * nb:]
        copies = _reduce_copies(src, recv, layout, send_sems, recv_sems)
        for cp in copies:
            cp.start()
        for cp in copies:
            cp.wait_recv()
        for cp in copies:
            cp.wait_send()

    out_shape = [jax.ShapeDtypeStruct(s, a.dtype) for s, a in zip(_recv_shapes(sums, layout), sums)]
    return _dma_call(body, name, nb, out_shape, (3 * nb, 3 * nb))(*sums)


def reduce_start(sums, layout, after, name):
    nb = len(sums)
    lands = [lax.empty(s, a.dtype) for s, a in zip(_recv_shapes(sums, layout), sums)]

    def body(*refs):
        src, recv = refs[:nb], refs[nb:2 * nb]
        send_sems, recv_sems = refs[2 * nb + len(after):2 * nb + len(after) + 2]
        for cp in _reduce_copies(src, recv, layout, send_sems, recv_sems):
            cp.start()
        refs[-1][...] = jnp.zeros_like(refs[-1])

    thru = [pltpu.HBM(a.shape, a.dtype) for a in (*sums, *lands)]
    res = pl.pallas_call(
        body, name=name,
        out_shape=(pltpu.SemaphoreType.DMA((3 * nb,)), pltpu.SemaphoreType.DMA((3 * nb,)), *thru,
                   jax.ShapeDtypeStruct((8, 128), F32)),
        in_specs=[_HBM] * (2 * nb) + [_ANY] * len(after),
        out_specs=(_SEM, _SEM, *[_HBM] * (2 * nb), pl.BlockSpec(memory_space=pltpu.VMEM)),
        input_output_aliases={i: 2 + i for i in range(2 * nb)},
        compiler_params=pltpu.CompilerParams(has_side_effects=pltpu.SideEffectType.DATAFLOW_SIDE_EFFECTING),
    )(*[_hbm(a) for a in (*sums, *lands)], *after)
    return (res[0], res[1], res[2:2 + nb], res[2 + nb:2 + 2 * nb]), res[-1]


def reduce_wait(send_sems, recv_sems, sums, lands, layout, after, name):
    nb = len(sums)

    def body(*refs):
        src, recv = refs[:nb], refs[nb:2 * nb]
        ssem, rsem = refs[2 * nb:2 * nb + 2]
        for cp in _reduce_copies(src, recv, layout, ssem, rsem):
            cp.wait_send()
            cp.wait_recv()

    thru = [pltpu.HBM(a.shape, a.dtype) for a in (*sums, *lands)]
    res = pl.pallas_call(
        body, name=name, out_shape=thru,
        in_specs=[_HBM] * (2 * nb) + [_SEM, _SEM, _ANY], out_specs=[_HBM] * (2 * nb),
        input_output_aliases={i: i for i in range(2 * nb)},
        compiler_params=pltpu.CompilerParams(has_side_effects=pltpu.SideEffectType.DATAFLOW_SIDE_EFFECTING),
    )(*sums, *lands, send_sems, recv_sems, after)
    return res[nb:]


def allgather8(block, name):
    m_per, n = block.shape

    def body(x_ref, out_ref, send_sems, recv_sems, local_sem):
        x, y, c = _place()
        me, sibling = (x, y, c), (x, y, 1 - c)
        chips = _other_chips(x, y)

        def rows(px, py, pc):
            return out_ref.at[pl.ds((4 * px + 2 * py + pc) * m_per, m_per), :]

        def copy(k, blk, to, src=None):
            return pltpu.make_async_remote_copy(
                src_ref=rows(*blk) if src is None else src, dst_ref=rows(*blk), send_sem=send_sems.at[k],
                recv_sem=recv_sems.at[k], device_id=to, device_id_type=MESH)

        mine = pltpu.make_async_copy(x_ref, rows(*me), local_sem)
        mine.start()
        first = [copy(0, me, sibling, src=x_ref)]
        first += [copy(1 + j, me, (*chip, c), src=x_ref) for j, chip in enumerate(chips)]
        for cp in first:
            cp.start()
        passed = [copy(4 + j, (*chip, c), sibling) for j, chip in enumerate(chips)]
        for j, chip in enumerate(chips):
            copy(1 + j, (*chip, c), me).wait_recv()
            passed[j].start()
        copy(0, sibling, me).wait_recv()
        for j, chip in enumerate(chips):
            copy(4 + j, (*chip, 1 - c), me).wait_recv()
        for cp in first + passed:
            cp.wait_send()
        mine.wait()

    return pl.pallas_call(
        body, name=name, in_specs=[pl.BlockSpec(memory_space=pltpu.VMEM)],
        out_specs=pl.BlockSpec(memory_space=pltpu.VMEM),
        out_shape=jax.ShapeDtypeStruct((8 * m_per, n), block.dtype),
        scratch_shapes=[pltpu.SemaphoreType.DMA((7,)), pltpu.SemaphoreType.DMA((7,)), pltpu.SemaphoreType.DMA],
        compiler_params=pltpu.CompilerParams(vmem_limit_bytes=VMEM_LIMIT),
    )(block)


CONV = ("gdn_conv_w", "ssd_conv_w", "lru_conv_w")
SMALL = ("ada_b", "norm_mix", "gdn_a_log", "gdn_dt_bias", "gdn_norm", "ssd_conv_b", "ssd_a_log", "ssd_dt_bias",
         "ssd_d", "ssd_norm", "lru_conv_b", "lru_w_a", "lru_b_a", "lru_w_x", "lru_b_x", "lru_lambda", "norm_mlp",
         "final_norm")
WEIGHTS = ("ada_w", "ada_b", "norm_mix", "w_in", "gdn_conv_w", "gdn_a_log", "gdn_dt_bias", "gdn_norm", "ssd_conv_w",
           "ssd_conv_b", "ssd_a_log", "ssd_dt_bias", "ssd_d", "ssd_norm", "lru_conv_w", "lru_conv_b", "lru_w_a",
           "lru_b_a", "lru_w_x", "lru_b_x", "lru_lambda", "w_branch", "w_out", "norm_mlp", "w_up", "w_down",
           "final_norm")
PACK_COLS = 1024


def _pack_rows(shape):
    return 8 * -(-math.prod(shape) // (8 * PACK_COLS))


def _pack(arrays, dtype):
    parts = []
    for a in arrays:
        flat = a.reshape(-1).astype(dtype)
        pad = _pack_rows(a.shape) * PACK_COLS - flat.shape[0]
        parts.append((jnp.concatenate([flat, jnp.zeros((pad,), dtype)]) if pad else flat).reshape(-1, PACK_COLS))
    return jnp.concatenate(parts, axis=0)


def _unpack(pack, shapes):
    out, o = [], 0
    for s in shapes:
        r = _pack_rows(s)
        out.append(pack[o:o + r].reshape(-1)[:math.prod(s)].reshape(s))
        o += r
    return out


def _split_w_in(w4):
    w = jnp.concatenate([w4[k] for k in range(4)], axis=1)
    pad = jnp.zeros((w.shape[0], 120), w.dtype)
    gdn = jnp.concatenate([w[:, 0:2056], pad], axis=1)
    ssd = jnp.concatenate([w[:, 2056:2568], w[:, 3080:3592], w[:, 2568:3080], w[:, 3592:3600], pad], axis=1)
    return gdn, ssd, w[:, 3600:4112], w[:, 4112:4624], w[:, 4624:7696]


def _join_w_in(gdn, ssd, lx, lg, gate):
    w = jnp.concatenate([gdn[:, 0:2056], ssd[:, 0:512], ssd[:, 1024:1536], ssd[:, 512:1024], ssd[:, 1536:1544],
                         lx, lg, gate], axis=1)
    return jnp.stack([w[:, k * 1924:(k + 1) * 1924] for k in range(4)])


def _lanes(v, at, width=128):
    return jnp.concatenate([jnp.zeros((at,), F32), v, jnp.zeros((width - at - v.shape[0],), F32)]).reshape(1, width)


def _block_diag(w):
    return (jnp.eye(8, dtype=w.dtype)[:, None, :, None] * w[:, :, None, :]).reshape(512, 512)


def _diag_blocks(w):
    return jnp.stack([w[n * 64:(n + 1) * 64, n * 64:(n + 1) * 64] for n in range(8)])


TOK_TILE = 512
WIDE_TILE = 512


def _layer_params(p, big, l):
    row = lambda v: v.reshape(1, -1)
    (ada_w, w_in), late = big
    gdn = (p["gdn_conv_w"][l], _lanes(p["gdn_a_log"][l], 4), _lanes(p["gdn_dt_bias"][l], 4), row(p["gdn_norm"][l]))
    ssd = (p["ssd_conv_w"][l], row(p["ssd_conv_b"][l]), _lanes(p["ssd_a_log"][l], 0), _lanes(p["ssd_dt_bias"][l], 0),
           row(jnp.repeat(p["ssd_d"][l], 64)), row(p["ssd_norm"][l]))
    lru = (p["lru_conv_w"][l], row(p["lru_conv_b"][l]), _block_diag(p["lru_w_a"][l]), row(p["lru_b_a"][l]),
           _block_diag(p["lru_w_x"][l]), row(p["lru_b_x"][l]), row(p["lru_lambda"][l]))
    return dict(gdn=gdn, ssd=ssd, lru=lru, w_in=_split_w_in(w_in), late=late, ada_w=ada_w, ada_b=row(p["ada_b"][l]),
                norm_mix=row(p["norm_mix"][l]), norm_mlp=row(p["norm_mlp"][l]))


def _layer_fwd(x, silu_c, lp, l):
    nm = lambda s: f"l{l}_{s}"
    mod = matmul(silu_c, lp["ada_w"], nm("mod"), bias=lp["ada_b"])
    sh1, sc1, gt1, sh2, sc2, gt2 = (mod[0:1, k * D_MODEL:(k + 1) * D_MODEL] for k in range(N_MOD))
    (h,), _ = scan_fwd(norm1_fn, nm("norm1"), TOK_TILE, [x], [lp["norm_mix"], sc1, sh1], [], [(D_MODEL, BF16)])
    w_gdn, w_ssd, w_lx, w_lg, w_gate = lp["w_in"]
    p_gdn = matmul(h, w_gdn, nm("in_gdn"))
    p_ssd = matmul(h, w_ssd, nm("in_ssd"))
    p_lx = matmul(h, w_lx, nm("in_lx"))
    p_lg = matmul(h, w_lg, nm("in_lg"))
    p_gate = matmul(h, w_gate, nm("in_gate"))
    (ya,), sv_gdn = scan_fwd(gdn_fn, nm("gdn"), CHUNK, [p_gdn], lp["gdn"], [(128, 128)] * 4 + [(8, 1536)],
                             [(512, F32)], save_carry=True)
    (yb,), sv_ssd = scan_fwd(ssd_fn, nm("ssd"), CHUNK, [p_ssd], lp["ssd"], [(128, 128)] * 4 + [(8, 1024)],
                             [(512, F32)], save_carry=True)
    (a, u), sv_lru = scan_fwd(lru_in_fn, nm("lru_in"), TOK_TILE, [p_lx], lp["lru"], [(8, 512)],
                              [(512, F32), (512, F32)], save_carry=True)
    hs = linscan_fwd(a, u, nm("lru_scan"))
    (yc,), _ = scan_fwd(lru_out_fn, nm("lru_out"), TOK_TILE, [hs, p_lg], [], [], [(512, F32)])
    w_branch, lp["w_out"], lp["w_up"], lp["w_down"] = lp.pop("late")(yc)
    lp["wb"] = tuple(w_branch[r] for r in range(3))
    (merged,), _ = scan_fwd(merge_fn, nm("merge"), WIDE_TILE, [ya, yb, yc, p_gate], lp["wb"], [], [(D_MODEL, BF16)])
    mix = matmul(merged, lp["w_out"], nm("out"))
    (h2, x1), _ = scan_fwd(resid_norm_fn, nm("norm2"), TOK_TILE, [x, mix], [gt1, lp["norm_mlp"], sc2, sh2], [],
                           [(D_MODEL, BF16), (D_MODEL, F32)])
    up, act = matmul(h2, lp["w_up"], nm("up"), relu2=True)
    dn = matmul(act, lp["w_down"], nm("down"))
    (x2,), _ = scan_fwd(resid_fn, nm("resid"), TOK_TILE, [x1, dn], [gt2], [], [(D_MODEL, F32)])
    saved = dict(x=x, h=h, p_gdn=p_gdn, p_ssd=p_ssd, p_lx=p_lx, p_lg=p_lg, p_gate=p_gate, sv_gdn=sv_gdn,
                 sv_ssd=sv_ssd, sv_lru=sv_lru, a=a, hs=hs, ya=ya, yb=yb, yc=yc, merged=merged, mix=mix, x1=x1,
                 h2=h2, up=up, act=act, dn=dn, mod=(sh1, sc1, gt1, sh2, sc2, gt2))
    return x2, saved


def _layer_bwd(d_x2, lp, sv, l, on_early, on_last):
    nm = lambda s: f"l{l}_b_{s}"
    sh1, sc1, gt1, sh2, sc2, gt2 = sv["mod"]
    (d_x1, d_dn), (d_gt2,) = scan_bwd(resid_fn, nm("resid"), TOK_TILE, [sv["x1"], sv["dn"]], [gt2], [], [d_x2], 2, 1,
                                      [F32, BF16])
    d_up = matmul(d_dn, lp["w_down"], nm("down_x"), tb=True, relu2_of=sv["up"], out_dtype=BF16)
    g_w_down = matmul(sv["act"], d_dn, nm("down_w"), ta=True)
    d_h2 = matmul(d_up, lp["w_up"], nm("up_x"), tb=True)
    g_w_up = matmul(sv["h2"], d_up, nm("up_w"), ta=True)
    (d_x, d_mix), (d_gt1, g_norm_mlp, d_sc2, d_sh2) = scan_bwd(
        resid_norm_fn, nm("norm2"), TOK_TILE, [sv["x"], sv["mix"]], [gt1, lp["norm_mlp"], sc2, sh2], [],
        [d_h2, d_x1], 2, 4, [F32, BF16])
    d_merged = matmul(d_mix, lp["w_out"], nm("out_x"), tb=True)
    g_w_out = matmul(sv["merged"], d_mix, nm("out_w"), ta=True)
    (d_ya, d_yb, d_yc, d_pgate), g_wb = scan_bwd(
        merge_fn, nm("merge"), WIDE_TILE, [sv["ya"], sv["yb"], sv["yc"], sv["p_gate"]], lp["wb"], [], [d_merged], 4, 3,
        [F32, F32, F32, BF16])
    early = [jnp.stack(g_wb), g_w_out, g_w_up, g_w_down]
    token = on_early(l, early)
    lru_params = lp["lru"] if token is None else (lp["lru"][0], lp["lru"][1] + token[0, 0], *lp["lru"][2:])
    (d_hs, d_plg), _ = scan_bwd(lru_out_fn, nm("lru_out"), TOK_TILE, [sv["hs"], sv["p_lg"]], [], [], [d_yc], 2, 0,
                                [F32, BF16])
    d_a, d_u = linscan_bwd(sv["a"], sv["hs"], d_hs, nm("lru_scan"))
    (d_plx,), g_lru = scan_bwd(lru_in_fn, nm("lru_in"), TOK_TILE, [sv["p_lx"]], lru_params, sv["sv_lru"],
                               [d_a, d_u], 1, 7, [BF16])
    (d_pssd,), g_ssd = scan_bwd(ssd_fn, nm("ssd"), CHUNK, [sv["p_ssd"]], lp["ssd"], sv["sv_ssd"], [d_yb], 1, 6,
                                [BF16])
    (d_pgdn,), g_gdn = scan_bwd(gdn_fn, nm("gdn"), CHUNK, [sv["p_gdn"]], lp["gdn"], sv["sv_gdn"], [d_ya], 1, 4,
                                [BF16])
    groups = list(zip(("gdn", "ssd", "lx", "lg", "gate"), (d_pgdn, d_pssd, d_plx, d_plg, d_pgate), lp["w_in"]))
    g_w_in = _join_w_in(*[matmul(sv["h"], dp, nm("in_w_" + tag), ta=True) for tag, dp, _ in groups])
    token = on_last(l, g_w_in)
    bias = None if token is None else jnp.zeros((1, D_MODEL), F32) + token[0, 0]
    d_h = None
    for i, (tag, dp, w) in enumerate(groups):
        d_h = matmul(dp, w, nm("in_x_" + tag), tb=True, add=d_h, bias=bias if i == 0 else None)
    (d_x0,), (g_norm_mix, d_sc1, d_sh1) = scan_bwd(norm1_fn, nm("norm1"), TOK_TILE, [sv["x"]],
                                                   [lp["norm_mix"], sc1, sh1], [], [d_h, d_x], 1, 3)
    d_mod = jnp.concatenate([d_sh1, d_sc1, d_gt1, d_sh2, d_sc2, d_gt2], axis=1)
    flat = lambda v: v.reshape(-1)
    grads = dict(
        ada_b=flat(d_mod), norm_mix=flat(g_norm_mix),
        gdn_conv_w=g_gdn[0], gdn_a_log=g_gdn[1][0, 4:8], gdn_dt_bias=g_gdn[2][0, 4:8], gdn_norm=flat(g_gdn[3]),
        ssd_conv_w=g_ssd[0], ssd_conv_b=flat(g_ssd[1]), ssd_a_log=g_ssd[2][0, 0:8], ssd_dt_bias=g_ssd[3][0, 0:8],
        ssd_d=g_ssd[4].reshape(8, 64).sum(axis=1), ssd_norm=flat(g_ssd[5]),
        lru_conv_w=g_lru[0], lru_conv_b=flat(g_lru[1]), lru_w_a=_diag_blocks(g_lru[2]), lru_b_a=flat(g_lru[3]),
        lru_w_x=_diag_blocks(g_lru[4]), lru_b_x=flat(g_lru[5]), lru_lambda=flat(g_lru[6]),
        norm_mlp=flat(g_norm_mlp))
    big = [g_w_in, *early]
    return d_x0, grads, big


def local_step(x, c, target, p, big, on_big_grads, on_early, on_last):
    c8 = jnp.concatenate([c, jnp.zeros((7, c.shape[1]), F32)], axis=0)
    (silu_c,), _ = scan_fwd(silu_fn, "silu_c", 8, [c8], [], [], [(D_MODEL, F32)])
    lps, saved = [], []
    for l in range(DEPTH):
        lps.append(_layer_params(p, big[l](x), l))
        x, sv = _layer_fwd(x, silu_c, lps[l], l)
        saved.append(sv)
    loss, d_x, g_final = loss_head(x, target, p["final_norm"].reshape(1, -1), "loss_head")
    layer_grads, big_grads = [None] * DEPTH, [None] * DEPTH
    for l in reversed(range(DEPTH)):
        d_x, layer_grads[l], big_grads[l] = _layer_bwd(d_x, lps[l], saved[l], l, on_early, on_last)
        if l > 0:
            token = on_big_grads(l, big_grads[l])
            mod = saved[l - 1]["mod"]
            saved[l - 1]["mod"] = (*mod[:5], mod[5] + token[0, 0])
    grads = {k: jnp.stack([layer_grads[l][k] for l in range(DEPTH)]) for k in layer_grads[0]}
    grads["final_norm"] = g_final.reshape(-1)
    return loss, d_x, grads, big_grads, silu_c[0]


def _place_shard(shard, kind, full, chip):
    base = lax.empty(full, shard.dtype)
    if kind == "chip":
        return lax.dynamic_update_index_in_dim(base, shard, chip, axis=0)
    axis = 0 if kind == "row" else len(full) - 1
    return lax.dynamic_update_slice_in_dim(base, shard, chip * shard.shape[axis], axis=axis)


def _adam_nd(g, w, m, v, name):
    three = lambda a: a.reshape(-1, *a.shape[-2:])
    return tuple(r.reshape(w.shape) for r in adamw(three(g), three(w), three(m), three(v), name, copy_g=True))


EARLY_LAYOUT, LAST_LAYOUT = REDUCE_LAYOUT[1:], REDUCE_LAYOUT[:1]


def _core_sums(grads, recvs, layout, place, tag):
    out = []
    for (n, kind, full), g, r in zip(layout, grads, recvs):
        local = r.shape[1:]
        out.append(sum4(g.reshape(-1, full[-1]), r.reshape(3, -1, local[-1]), kind == "col", place,
                        f"reduce_sum_{tag}_{n}"))
    return out


def _reduce_total(sums0, sums1, shapes):
    theirs = swap_d2d(sums0 + sums1, "reduce_swap")
    nb = len(sums0)
    return [add_layers([sums0[n], sums1[n]], [theirs[n], theirs[nb + n]], f"reduce_total_{n}").reshape(2, *s)
            for n, s in enumerate(shapes)]


def _step(w, m, v, x, c, target):
    chip = 2 * lax.axis_index("x") + lax.axis_index("y")
    place = jnp.stack([chip, lax.axis_index("c")]).astype(jnp.int32)
    conv_shapes = [w[n].shape for n in CONV]
    small_shapes = [w[n].shape for n in SMALL]

    shards = [w[n].astype(BF16) for n, _, _ in BIG_LAYOUT]
    first0 = gather_layer([shards[0][0], shards[1][0]], BIG_LAYOUT[:2], ((1,), (0,)), "gather_l0")
    conv_all = allgather8(_pack([w[n] for n in CONV], F32), "gather_conv").reshape(8, -1, PACK_COLS)
    rest = [(s[0], e) for s, e in zip(shards[2:], BIG_LAYOUT[2:])] + [(s[1], e) for s, e in zip(shards, BIG_LAYOUT)]
    rest_layout = tuple(e for _, e in rest)
    own = [_place_shard(s, kind, full, chip) for s, (_, kind, full) in rest]
    (ssem, rsem, sent, landed), token = gather_start([s for s, _ in rest], own, rest_layout, [first0[0], conv_all],
                                                     "gather_rest_start")
    c = c + token[0, 0]
    n0 = N_BIG - 2

    def layer1(x_in):
        full1 = gather_wait(ssem, rsem, sent[n0:], landed[n0:], BIG_LAYOUT, n0, x_in, "gather_l1_wait")
        return full1[:2], lambda _: full1[2:]

    big = [lambda _: (first0, lambda after: gather_wait(ssem, rsem, sent[:n0], landed[:n0], BIG_LAYOUT[2:], 0, after,
                                                        "gather_l0_rest_wait")), layer1]
    conv_parts = [_unpack(conv_all[2 * k], conv_shapes) for k in range(4)]
    p = {n: w[n] for n in SMALL}
    for i, n in enumerate(CONV):
        p[n] = jnp.concatenate([conv_parts[k][i] for k in range(4)], axis=2)

    st = {}
    bf16 = lambda arrays: [a.astype(BF16) for a in arrays]

    def on_big_grads(l, grads_l):
        st["g1"] = grads_l
        st["f1"], token = reduce_start(bf16(grads_l), REDUCE_LAYOUT, [], "reduce_l1_start")
        return token

    def on_early(l, early):
        if l != 0:
            return None
        st["r1"] = reduce_wait(*st["f1"], REDUCE_LAYOUT, early[0], "reduce_l1_wait")
        st["fe"], token = reduce_start(bf16(early), EARLY_LAYOUT, [st["r1"][0]], "reduce_l0_start")
        return token

    def on_last(l, g_w_in):
        if l != 0:
            return None
        st["re"] = reduce_wait(*st["fe"], EARLY_LAYOUT, g_w_in, "reduce_l0_wait")
        st["fl"], token = reduce_start(bf16([g_w_in]), LAST_LAYOUT, [st["re"][0]], "reduce_w_in_start")
        return token

    loss_blk, grad_x, g, big_g, silu_c = local_step(x[0], c, target[0], p, big, on_big_grads, on_early, on_last)
    g0 = big_g[0]
    recv_last = reduce_wait(*st["fl"], LAST_LAYOUT, grad_x, "reduce_w_in_wait")
    sums1 = _core_sums(st["g1"], st["r1"], REDUCE_LAYOUT, place, "l1")
    sums0 = (_core_sums(g0[:1], recv_last, LAST_LAYOUT, place, "l0")
             + _core_sums(g0[1:], st["re"], EARLY_LAYOUT, place, "l0"))
    shapes = [_local_shape(kind, full) for _, kind, full in REDUCE_LAYOUT]
    big_g = dict(zip((n for n, _, _ in REDUCE_LAYOUT), _reduce_total(sums0, sums1, shapes)))

    assert SMALL[0] == "ada_b"
    small_pack = _pack([g["ada_b"], silu_c, loss_blk[0, 0:1]] + [g[n] for n in SMALL[1:]] + [g[n] for n in CONV], F32)
    small_all = allgather8(small_pack, "gather_small").reshape(8, -1, PACK_COLS)
    total = _unpack(add8(small_all, "reduce_small"),
                    [small_shapes[0], (D_MODEL,), (1,)] + small_shapes[1:] + [g[n].shape for n in CONV])
    loss = total[2][0]
    small_g = dict(zip(SMALL, [total[0]] + total[3:2 + len(SMALL)]))
    conv_g = {n: lax.dynamic_slice_in_dim(t, chip * w[n].shape[2], w[n].shape[2], axis=2)
              for n, t in zip(CONV, total[2 + len(SMALL):])}

    cols = w["ada_w"].shape[2]
    silu_all = small_all[:, _pack_rows(small_shapes[0]), :]
    big_g["ada_w"] = jnp.stack([
        matmul(silu_all, lax.dynamic_slice_in_dim(small_all[:, N_MOD * l:N_MOD * (l + 1), :].reshape(8, -1),
                                                  chip * cols, cols, axis=1), f"ada_w_grad{l}", ta=True)
        for l in range(DEPTH)])

    grad, delta, new_m, new_v = {}, {}, {}, {}
    for n, _, _ in BIG_LAYOUT:
        delta[n], new_m[n], new_v[n], grad[n] = _adam_nd(big_g[n], w[n], m[n], v[n], "adam_" + n)
    for names, gs, shapes, tag in ((SMALL, small_g, small_shapes, "small"), (CONV, conv_g, conv_shapes, "conv")):
        pk = lambda d: _pack([d[n] for n in names], F32)[None]
        res = adamw(pk(gs), pk(w), pk(m), pk(v), "adam_" + tag)
        for out, r in zip((delta, new_m, new_v), res):
            out.update(zip(names, _unpack(r[0], shapes)))
        grad.update({n: gs[n] for n in names})
    outs = [loss, grad_x[None]]
    for d in (grad, delta, new_m, new_v):
        outs += [d[n] for n in WEIGHTS]
    return tuple(outs)


def kernel(x, c, ada_w, ada_b, norm_mix, w_in, gdn_conv_w, gdn_a_log, gdn_dt_bias, gdn_norm, ssd_conv_w, ssd_conv_b, ssd_a_log, ssd_dt_bias, ssd_d, ssd_norm, lru_conv_w, lru_conv_b, lru_w_a, lru_b_a, lru_w_x, lru_b_x, lru_lambda, w_branch, w_out, norm_mlp, w_up, w_down, final_norm, loss_target, m_ada_w, m_ada_b, m_norm_mix, m_w_in, m_gdn_conv_w, m_gdn_a_log, m_gdn_dt_bias, m_gdn_norm, m_ssd_conv_w, m_ssd_conv_b, m_ssd_a_log, m_ssd_dt_bias, m_ssd_d, m_ssd_norm, m_lru_conv_w, m_lru_conv_b, m_lru_w_a, m_lru_b_a, m_lru_w_x, m_lru_b_x, m_lru_lambda, m_w_branch, m_w_out, m_norm_mlp, m_w_up, m_w_down, m_final_norm, v_ada_w, v_ada_b, v_norm_mix, v_w_in, v_gdn_conv_w, v_gdn_a_log, v_gdn_dt_bias, v_gdn_norm, v_ssd_conv_w, v_ssd_conv_b, v_ssd_a_log, v_ssd_dt_bias, v_ssd_d, v_ssd_norm, v_lru_conv_w, v_lru_conv_b, v_lru_w_a, v_lru_b_a, v_lru_w_x, v_lru_b_x, v_lru_lambda, v_w_branch, v_w_out, v_norm_mlp, v_w_up, v_w_down, v_final_norm):
    given = dict(locals())
    w = {n: given[n] for n in WEIGHTS}
    m = {n: given["m_" + n] for n in WEIGHTS}
    v = {n: given["v_" + n] for n in WEIGHTS}
    return _step(w, m, v, x, c, loss_target)
```
